```python
import jax
import jax.numpy as jnp
from jax import lax
import numpy as np

D_MODEL = 1024
BATCH = 8
SEQ = 8192
DEPTH = 2

N_META = 16
D_MIX = D_MODEL
CONV_WIDTH = D_MIX // 4
HG_WIDTH = D_MIX // 2
HG_HEAD_DIM = 128
HG_HEADS = HG_WIDTH // HG_HEAD_DIM
POOL_WIDTH = D_MIX - CONV_WIDTH - HG_WIDTH
POOL_WINDOWS = (2, 4, 8, 16)
POOL_GROUPS = len(POOL_WINDOWS)
POOL_GROUP_DIM = POOL_WIDTH // POOL_GROUPS
SHORT_CONV_K = 3
FFN_CONV_K = 3
D_FF = 2816
CHUNK = 64
D_IN = 3 * CONV_WIDTH + 4 * HG_WIDTH + POOL_WIDTH
ALPHA = (2 * DEPTH) ** 0.25
BETA = (8 * DEPTH) ** -0.25
LN_EPS = 1e-5
RMS_EPS = 1e-6
F_FLOOR = 1e-30
SPLIT_SIZES = (CONV_WIDTH,) * 3 + (HG_WIDTH,) * 4 + (POOL_WIDTH,)
SPLIT_IDX = tuple(int(s) for s in np.cumsum(SPLIT_SIZES)[:-1])

kernel_name = "hymba_conv_hgrn2_pool_deepnorm"


def causal_dwconv(x, w, b=None):
    K = w.shape[-1]
    L = x.shape[1]
    xp = jnp.pad(x, ((0, 0), (K - 1, 0), (0, 0)))
    y = xp[:, 0:L, :] * w[:, 0]
    for k in range(1, K):
        y = y + xp[:, k:k + L, :] * w[:, k]
    if b is not None:
        y = y + b
    return y


def layer_norm(x, g, b):
    xf = x.astype(jnp.float32)
    mu = jnp.mean(xf, axis=-1, keepdims=True)
    var = jnp.mean(jnp.square(xf - mu), axis=-1, keepdims=True)
    return ((xf - mu) * lax.rsqrt(var + LN_EPS) * g + b).astype(x.dtype)


def short_conv_mixer(bg, cg, v, w_conv):
    return bg * causal_dwconv(cg * v, w_conv)


def multiscale_pool_mixer(v, w_pool, pool_scale):
    B, L, _ = v.shape
    vf = v.astype(jnp.float32)
    c = jnp.pad(jnp.cumsum(vf, axis=1), ((0, 0), (1, 0), (0, 0)))
    t = jnp.arange(L)
    outs = []
    for gi, win in enumerate(POOL_WINDOWS):
        lo, hi = gi * POOL_GROUP_DIM, (gi + 1) * POOL_GROUP_DIM
        cg = c[..., lo:hi]
        prev = jnp.pad(cg, ((0, 0), (win, 0), (0, 0)))[:, 1:L + 1]
        count = jnp.minimum(t + 1, win).astype(jnp.float32)[:, None]
        outs.append((cg[:, 1:] - prev) / count - vf[..., lo:hi])
    d = jnp.stack(outs, axis=2)
    y = jnp.einsum('blgc,gcd->blgd', d, w_pool).reshape(B, L, POOL_WIDTH)
    return (y * pool_scale).astype(v.dtype)


def hgrn2_mixer(q, fz, i, gz, lb, g_norm):
    B, L, _ = q.shape
    f32 = jnp.float32
    fz = fz.astype(f32)
    lb = lb.astype(f32)
    sig = jax.nn.sigmoid(fz)
    f = lb + (1.0 - lb) * sig
    log_f = jnp.log(jnp.maximum(f, F_FLOOR))
    k = (1.0 - lb) * (1.0 - sig)
    pad = CHUNK - N_META

    def to_chunks(a):
        a = jnp.pad(a.astype(f32), ((0, 0), (pad, 0), (0, 0)))
        n = a.shape[1] // CHUNK
        return a.reshape(B, n, CHUNK, HG_HEADS, HG_HEAD_DIM).transpose(1, 0, 3, 2, 4)

    qc = to_chunks(q.astype(f32) * (HG_HEAD_DIM ** -0.5))
    kc, ic, gc = to_chunks(k), to_chunks(i), to_chunks(log_f)
    mask = jnp.tril(jnp.ones((CHUNK, CHUNK), dtype=bool))[:, :, None]

    def step(S, xs):
        qb, kb, ib, gb = xs
        G = jnp.cumsum(gb, axis=2)
        o_inter = jnp.einsum('bhtd,bhde->bhte', qb * jnp.exp(G), S)
        diff = G[:, :, :, None, :] - G[:, :, None, :, :]
        decay = jnp.where(mask, jnp.exp(jnp.where(mask, diff, 0.0)), 0.0)
        A = jnp.einsum('bhtd,bhsd,bhtsd->bhts', qb, kb, decay)
        o_intra = jnp.einsum('bhts,bhse->bhte', A, ib)
        G_last = G[:, :, -1:, :]
        S_new = jnp.exp(G_last[:, :, 0, :, None]) * S + jnp.einsum(
            'bhsd,bhse->bhde', kb * jnp.exp(G_last - G), ib)
        return S_new, o_inter + o_intra

    S0 = jnp.zeros((B, HG_HEADS, HG_HEAD_DIM, HG_HEAD_DIM), f32)
    _, o = lax.scan(step, S0, (qc, kc, ic, gc))
    o = o.transpose(1, 0, 3, 2, 4).reshape(B, -1, HG_HEADS, HG_HEAD_DIM)[:, pad:]
    o = o * lax.rsqrt(jnp.mean(jnp.square(o), axis=-1, keepdims=True) + RMS_EPS) * g_norm
    o = o * jax.nn.silu(gz.astype(f32).reshape(B, L, HG_HEADS, HG_HEAD_DIM))
    return o.reshape(B, L, HG_WIDTH).astype(q.dtype)


def hybrid_layer(x, lb, w_in, w_conv, w_pool, pool_scale, hg_norm_g, w_o, ln1_g, ln1_b,
                 w_up, w_ffn_conv, b_ffn_conv, w_down, ln2_g, ln2_b):
    h = x @ w_in
    cb, cc, cv, hq, hf, hi, hgate, pv = jnp.split(h, SPLIT_IDX, axis=-1)
    y_conv = short_conv_mixer(cb, cc, cv, w_conv)
    y_hg = hgrn2_mixer(hq, hf, hi, hgate, lb, hg_norm_g)
    y_pool = multiscale_pool_mixer(pv, w_pool, pool_scale)
    mix = jnp.concatenate([y_conv, y_hg, y_pool], axis=-1) @ w_o
    x = layer_norm(ALPHA * x + mix, ln1_g, ln1_b)
    u = causal_dwconv(x @ w_up, w_ffn_conv, b_ffn_conv)
    gate, val = jnp.split(u, 2, axis=-1)
    ffn = (jax.nn.silu(gate) * val) @ w_down
    return layer_norm(ALPHA * x + ffn, ln2_g, ln2_b)


def _fwd_setup_inputs(seed: int = 0) -> dict:
    key = jax.random.key(seed)
    ks = jax.random.split(key, 17)
    f32 = jnp.float32

    def nrm(k, shape):
        return jax.random.normal(k, shape, f32)

    col_scale = jnp.concatenate([
        jnp.ones((2 * CONV_WIDTH,), f32), jnp.full((CONV_WIDTH,), BETA, f32),
        jnp.ones((2 * HG_WIDTH,), f32), jnp.full((HG_WIDTH,), BETA, f32),
        jnp.ones((HG_WIDTH,), f32), jnp.full((POOL_WIDTH,), BETA, f32)])
    return {
        'x': nrm(ks[0], (BATCH, SEQ, D_MODEL)),
        'meta_tokens': nrm(ks[1], (N_META, D_MODEL)),
        'hg_lower_bounds': 0.1 * nrm(ks[2], (DEPTH, HG_WIDTH)),
        'w_in': nrm(ks[3], (DEPTH, D_MODEL, D_IN)) * (D_MODEL ** -0.5) * col_scale,
        'w_conv': nrm(ks[4], (DEPTH, CONV_WIDTH, SHORT_CONV_K)) * (SHORT_CONV_K ** -0.5),
        'w_pool': nrm(ks[5], (DEPTH, POOL_GROUPS, POOL_GROUP_DIM, POOL_GROUP_DIM)) * (POOL_GROUP_DIM ** -0.5),
        'pool_scale': 1.0 + 0.02 * nrm(ks[6], (DEPTH, POOL_WIDTH)),
        'hg_norm_g': 1.0 + 0.02 * nrm(ks[7], (DEPTH, HG_HEAD_DIM)),
        'w_o': nrm(ks[8], (DEPTH, D_MIX, D_MODEL)) * (D_MIX ** -0.5) * BETA,
        'ln1_g': 1.0 + 0.02 * nrm(ks[9], (DEPTH, D_MODEL)),
        'ln1_b': 0.02 * nrm(ks[10], (DEPTH, D_MODEL)),
        'w_up': nrm(ks[11], (DEPTH, D_MODEL, 2 * D_FF)) * (D_MODEL ** -0.5),
        'w_ffn_conv': nrm(ks[12], (DEPTH, 2 * D_FF, FFN_CONV_K)) * (FFN_CONV_K ** -0.5),
        'b_ffn_conv': 0.02 * nrm(ks[13], (DEPTH, 2 * D_FF)),
        'w_down': nrm(ks[14], (DEPTH, D_FF, D_MODEL)) * (D_FF ** -0.5) * BETA,
        'ln2_g': 1.0 + 0.02 * nrm(ks[15], (DEPTH, D_MODEL)),
        'ln2_b': 0.02 * nrm(ks[16], (DEPTH, D_MODEL)),
    }


def _fwd_reference(x, meta_tokens, hg_lower_bounds, w_in, w_conv, w_pool, pool_scale, hg_norm_g,
              w_o, ln1_g, ln1_b, w_up, w_ffn_conv, b_ffn_conv, w_down, ln2_g, ln2_b):
    B = x.shape[0]
    meta = jnp.broadcast_to(meta_tokens[None].astype(x.dtype), (B, N_META, D_MODEL))
    h = jnp.concatenate([meta, x], axis=1)
    p = jax.nn.softmax(hg_lower_bounds.astype(jnp.float32), axis=0)
    lbs = jnp.cumsum(p, axis=0) - p[0]
    for l in range(DEPTH):
        h = hybrid_layer(h, lbs[l], w_in[l], w_conv[l], w_pool[l], pool_scale[l], hg_norm_g[l],
                         w_o[l], ln1_g[l], ln1_b[l], w_up[l], w_ffn_conv[l], b_ffn_conv[l],
                         w_down[l], ln2_g[l], ln2_b[l])
    return h[:, N_META:]


import jax as _jax
import jax.numpy as _jnp

TWIN_FORMAT = 'train_step'
FWD_PARAMS = ['x', 'meta_tokens', 'hg_lower_bounds', 'w_in', 'w_conv', 'w_pool', 'pool_scale', 'hg_norm_g', 'w_o', 'ln1_g', 'ln1_b', 'w_up', 'w_ffn_conv', 'b_ffn_conv', 'w_down', 'ln2_g', 'ln2_b']
TWIN_WEIGHTS = ['meta_tokens', 'hg_lower_bounds', 'w_in', 'w_conv', 'w_pool', 'pool_scale', 'hg_norm_g', 'w_o', 'ln1_g', 'ln1_b', 'w_up', 'w_ffn_conv', 'b_ffn_conv', 'w_down', 'ln2_g', 'ln2_b']
TWIN_DIFF_INPUT = 'x'
TWIN_INPUTS = ['x', 'meta_tokens', 'hg_lower_bounds', 'w_in', 'w_conv', 'w_pool', 'pool_scale', 'hg_norm_g', 'w_o', 'ln1_g', 'ln1_b', 'w_up', 'w_ffn_conv', 'b_ffn_conv', 'w_down', 'ln2_g', 'ln2_b', 'loss_target', 'm_meta_tokens', 'm_hg_lower_bounds', 'm_w_in', 'm_w_conv', 'm_w_pool', 'm_pool_scale', 'm_hg_norm_g', 'm_w_o', 'm_ln1_g', 'm_ln1_b', 'm_w_up', 'm_w_ffn_conv', 'm_b_ffn_conv', 'm_w_down', 'm_ln2_g', 'm_ln2_b', 'v_meta_tokens', 'v_hg_lower_bounds', 'v_w_in', 'v_w_conv', 'v_w_pool', 'v_pool_scale', 'v_hg_norm_g', 'v_w_o', 'v_ln1_g', 'v_ln1_b', 'v_w_up', 'v_w_ffn_conv', 'v_b_ffn_conv', 'v_w_down', 'v_ln2_g', 'v_ln2_b']
TWIN_OUTPUTS = ['loss', 'grad_x', 'grad_meta_tokens', 'grad_hg_lower_bounds', 'grad_w_in', 'grad_w_conv', 'grad_w_pool', 'grad_pool_scale', 'grad_hg_norm_g', 'grad_w_o', 'grad_ln1_g', 'grad_ln1_b', 'grad_w_up', 'grad_w_ffn_conv', 'grad_b_ffn_conv', 'grad_w_down', 'grad_ln2_g', 'grad_ln2_b', 'delta_meta_tokens', 'delta_hg_lower_bounds', 'delta_w_in', 'delta_w_conv', 'delta_w_pool', 'delta_pool_scale', 'delta_hg_norm_g', 'delta_w_o', 'delta_ln1_g', 'delta_ln1_b', 'delta_w_up', 'delta_w_ffn_conv', 'delta_b_ffn_conv', 'delta_w_down', 'delta_ln2_g', 'delta_ln2_b', 'new_m_meta_tokens', 'new_m_hg_lower_bounds', 'new_m_w_in', 'new_m_w_conv', 'new_m_w_pool', 'new_m_pool_scale', 'new_m_hg_norm_g', 'new_m_w_o', 'new_m_ln1_g', 'new_m_ln1_b', 'new_m_w_up', 'new_m_w_ffn_conv', 'new_m_b_ffn_conv', 'new_m_w_down', 'new_m_ln2_g', 'new_m_ln2_b', 'new_v_meta_tokens', 'new_v_hg_lower_bounds', 'new_v_w_in', 'new_v_w_conv', 'new_v_w_pool', 'new_v_pool_scale', 'new_v_hg_norm_g', 'new_v_w_o', 'new_v_ln1_g', 'new_v_ln1_b', 'new_v_w_up', 'new_v_w_ffn_conv', 'new_v_b_ffn_conv', 'new_v_w_down', 'new_v_ln2_g', 'new_v_ln2_b']
TWIN_LEAF_KINDS = {'loss': 'loss', 'grad_x': 'grad_x', 'grad_meta_tokens': 'grad_w', 'grad_hg_lower_bounds': 'grad_w', 'grad_w_in': 'grad_w', 'grad_w_conv': 'grad_w', 'grad_w_pool': 'grad_w', 'grad_pool_scale': 'grad_w', 'grad_hg_norm_g': 'grad_w', 'grad_w_o': 'grad_w', 'grad_ln1_g': 'grad_w', 'grad_ln1_b': 'grad_w', 'grad_w_up': 'grad_w', 'grad_w_ffn_conv': 'grad_w', 'grad_b_ffn_conv': 'grad_w', 'grad_w_down': 'grad_w', 'grad_ln2_g': 'grad_w', 'grad_ln2_b': 'grad_w', 'delta_meta_tokens': 'delta_w', 'delta_hg_lower_bounds': 'delta_w', 'delta_w_in': 'delta_w', 'delta_w_conv': 'delta_w', 'delta_w_pool': 'delta_w', 'delta_pool_scale': 'delta_w', 'delta_hg_norm_g': 'delta_w', 'delta_w_o': 'delta_w', 'delta_ln1_g': 'delta_w', 'delta_ln1_b': 'delta_w', 'delta_w_up': 'delta_w', 'delta_w_ffn_conv': 'delta_w', 'delta_b_ffn_conv': 'delta_w', 'delta_w_down': 'delta_w', 'delta_ln2_g': 'delta_w', 'delta_ln2_b': 'delta_w', 'new_m_meta_tokens': 'new_m', 'new_m_hg_lower_bounds': 'new_m', 'new_m_w_in': 'new_m', 'new_m_w_conv': 'new_m', 'new_m_w_pool': 'new_m', 'new_m_pool_scale': 'new_m', 'new_m_hg_norm_g': 'new_m', 'new_m_w_o': 'new_m', 'new_m_ln1_g': 'new_m', 'new_m_ln1_b': 'new_m', 'new_m_w_up': 'new_m', 'new_m_w_ffn_conv': 'new_m', 'new_m_b_ffn_conv': 'new_m', 'new_m_w_down': 'new_m', 'new_m_ln2_g': 'new_m', 'new_m_ln2_b': 'new_m', 'new_v_meta_tokens': 'new_v', 'new_v_hg_lower_bounds': 'new_v', 'new_v_w_in': 'new_v', 'new_v_w_conv': 'new_v', 'new_v_w_pool': 'new_v', 'new_v_pool_scale': 'new_v', 'new_v_hg_norm_g': 'new_v', 'new_v_w_o': 'new_v', 'new_v_ln1_g': 'new_v', 'new_v_ln1_b': 'new_v', 'new_v_w_up': 'new_v', 'new_v_w_ffn_conv': 'new_v', 'new_v_b_ffn_conv': 'new_v', 'new_v_w_down': 'new_v', 'new_v_ln2_g': 'new_v', 'new_v_ln2_b': 'new_v'}


def _forward(args):
    return _fwd_reference(*[args[k] for k in FWD_PARAMS])


def _output_shape():
    def fwd():
        inp = _fwd_setup_inputs(0)
        return _fwd_reference(*[inp[k] for k in FWD_PARAMS])
    out = _jax.eval_shape(fwd)
    return out.shape, out.dtype

N_MICROBATCH = 1
ADAM_LR = 0.001
ADAM_B1 = 0.9
ADAM_B2 = 0.999
ADAM_EPS = 1e-08
ADAM_WD = 0.01
ADAM_STEP = 10
PER_EXAMPLE_BATCH_AXIS = {'x': 0, 'loss_target': 0}
SHARED_INPUTS = []
_WEIGHT_DTYPES = {'meta_tokens': _jnp.float32, 'hg_lower_bounds': _jnp.float32, 'w_in': _jnp.float32, 'w_conv': _jnp.float32, 'w_pool': _jnp.float32, 'pool_scale': _jnp.float32, 'hg_norm_g': _jnp.float32, 'w_o': _jnp.float32, 'ln1_g': _jnp.float32, 'ln1_b': _jnp.float32, 'w_up': _jnp.float32, 'w_ffn_conv': _jnp.float32, 'b_ffn_conv': _jnp.float32, 'w_down': _jnp.float32, 'ln2_g': _jnp.float32, 'ln2_b': _jnp.float32}
MOMENT_SCALE = {'meta_tokens': 2.277688e-03, 'hg_lower_bounds': 3.819722e-02, 'w_in': 7.578542e-02, 'w_conv': 4.697969e-02, 'w_pool': 4.078989e-02, 'pool_scale': 4.020582e-02, 'hg_norm_g': 1.088620e-01, 'w_o': 9.919394e-02, 'ln1_g': 2.059692e+00, 'ln1_b': 9.802580e-01, 'w_up': 3.420131e-02, 'w_ffn_conv': 3.410617e-02, 'b_ffn_conv': 4.081405e-02, 'w_down': 1.113161e-01, 'ln2_g': 4.529865e+01, 'ln2_b': 1.586235e+00}


def _to_microbatches(a, axis):
    t = _jnp.moveaxis(a, axis, 0)
    t = t.reshape((N_MICROBATCH, t.shape[0] // N_MICROBATCH) + t.shape[1:])
    return _jnp.moveaxis(t, 1, axis + 1)


def setup_inputs(seed: int = 0) -> dict:
    inp = _fwd_setup_inputs(seed)
    key = _jax.random.fold_in(_jax.random.key(seed), 7919)
    shape, _ = _output_shape()
    out = dict(inp)
    out["loss_target"] = _jax.random.normal(_jax.random.fold_in(key, 0), shape, _jnp.float32)
    for i, name in enumerate(TWIN_WEIGHTS):
        w = inp[name].astype(_jnp.float32)
        if MOMENT_SCALE is None:
            s = _jnp.sqrt(_jnp.mean(_jnp.square(w)) + 1e-30)
        else:
            s = MOMENT_SCALE[name]
        km, kv = _jax.random.split(_jax.random.fold_in(key, i + 1))
        out[name] = w
        out["m_" + name] = s * _jax.random.normal(km, w.shape, _jnp.float32)
        out["v_" + name] = (s * s) * _jax.random.uniform(kv, w.shape, _jnp.float32, 0.5, 1.5)
    if N_MICROBATCH > 1:
        for name, axis in PER_EXAMPLE_BATCH_AXIS.items():
            out[name] = _to_microbatches(out[name], axis)
    return {'x': out['x'], 'meta_tokens': out['meta_tokens'], 'hg_lower_bounds': out['hg_lower_bounds'], 'w_in': out['w_in'], 'w_conv': out['w_conv'], 'w_pool': out['w_pool'], 'pool_scale': out['pool_scale'], 'hg_norm_g': out['hg_norm_g'], 'w_o': out['w_o'], 'ln1_g': out['ln1_g'], 'ln1_b': out['ln1_b'], 'w_up': out['w_up'], 'w_ffn_conv': out['w_ffn_conv'], 'b_ffn_conv': out['b_ffn_conv'], 'w_down': out['w_down'], 'ln2_g': out['ln2_g'], 'ln2_b': out['ln2_b'], 'loss_target': out['loss_target'], 'm_meta_tokens': out['m_meta_tokens'], 'm_hg_lower_bounds': out['m_hg_lower_bounds'], 'm_w_in': out['m_w_in'], 'm_w_conv': out['m_w_conv'], 'm_w_pool': out['m_w_pool'], 'm_pool_scale': out['m_pool_scale'], 'm_hg_norm_g': out['m_hg_norm_g'], 'm_w_o': out['m_w_o'], 'm_ln1_g': out['m_ln1_g'], 'm_ln1_b': out['m_ln1_b'], 'm_w_up': out['m_w_up'], 'm_w_ffn_conv': out['m_w_ffn_conv'], 'm_b_ffn_conv': out['m_b_ffn_conv'], 'm_w_down': out['m_w_down'], 'm_ln2_g': out['m_ln2_g'], 'm_ln2_b': out['m_ln2_b'], 'v_meta_tokens': out['v_meta_tokens'], 'v_hg_lower_bounds': out['v_hg_lower_bounds'], 'v_w_in': out['v_w_in'], 'v_w_conv': out['v_w_conv'], 'v_w_pool': out['v_w_pool'], 'v_pool_scale': out['v_pool_scale'], 'v_hg_norm_g': out['v_hg_norm_g'], 'v_w_o': out['v_w_o'], 'v_ln1_g': out['v_ln1_g'], 'v_ln1_b': out['v_ln1_b'], 'v_w_up': out['v_w_up'], 'v_w_ffn_conv': out['v_w_ffn_conv'], 'v_b_ffn_conv': out['v_b_ffn_conv'], 'v_w_down': out['v_w_down'], 'v_ln2_g': out['v_ln2_g'], 'v_ln2_b': out['v_ln2_b']}


def _loss(weights, diff, rest, loss_target):
    with _jax.named_scope("forward"):
        args = {**rest, TWIN_DIFF_INPUT: diff, **{k: w.astype(_WEIGHT_DTYPES[k]) for k, w in weights.items()}}
        y = _forward(args)
    with _jax.named_scope("loss_head"):
        err = _jnp.square(y.astype(_jnp.float32) - loss_target)
        return 0.5 * _jnp.sum(_jnp.mean(err, axis=-1)) if err.ndim else 0.5 * err


def _adamw(w, g, m, v):
    m = ADAM_B1 * m + (1.0 - ADAM_B1) * g
    v = ADAM_B2 * v + (1.0 - ADAM_B2) * _jnp.square(g)
    m_hat = m / (1.0 - ADAM_B1 ** ADAM_STEP)
    v_hat = v / (1.0 - ADAM_B2 ** ADAM_STEP)
    delta = -ADAM_LR * (m_hat / (_jnp.sqrt(v_hat) + ADAM_EPS) + ADAM_WD * w)
    return delta, m, v


def reference(x, meta_tokens, hg_lower_bounds, w_in, w_conv, w_pool, pool_scale, hg_norm_g, w_o, ln1_g, ln1_b, w_up, w_ffn_conv, b_ffn_conv, w_down, ln2_g, ln2_b, loss_target, m_meta_tokens, m_hg_lower_bounds, m_w_in, m_w_conv, m_w_pool, m_pool_scale, m_hg_norm_g, m_w_o, m_ln1_g, m_ln1_b, m_w_up, m_w_ffn_conv, m_b_ffn_conv, m_w_down, m_ln2_g, m_ln2_b, v_meta_tokens, v_hg_lower_bounds, v_w_in, v_w_conv, v_w_pool, v_pool_scale, v_hg_norm_g, v_w_o, v_ln1_g, v_ln1_b, v_w_up, v_w_ffn_conv, v_b_ffn_conv, v_w_down, v_ln2_g, v_ln2_b):
    given = dict(x=x, meta_tokens=meta_tokens, hg_lower_bounds=hg_lower_bounds, w_in=w_in, w_conv=w_conv, w_pool=w_pool, pool_scale=pool_scale, hg_norm_g=hg_norm_g, w_o=w_o, ln1_g=ln1_g, ln1_b=ln1_b, w_up=w_up, w_ffn_conv=w_ffn_conv, b_ffn_conv=b_ffn_conv, w_down=w_down, ln2_g=ln2_g, ln2_b=ln2_b, loss_target=loss_target, m_meta_tokens=m_meta_tokens, m_hg_lower_bounds=m_hg_lower_bounds, m_w_in=m_w_in, m_w_conv=m_w_conv, m_w_pool=m_w_pool, m_pool_scale=m_pool_scale, m_hg_norm_g=m_hg_norm_g, m_w_o=m_w_o, m_ln1_g=m_ln1_g, m_ln1_b=m_ln1_b, m_w_up=m_w_up, m_w_ffn_conv=m_w_ffn_conv, m_b_ffn_conv=m_b_ffn_conv, m_w_down=m_w_down, m_ln2_g=m_ln2_g, m_ln2_b=m_ln2_b, v_meta_tokens=v_meta_tokens, v_hg_lower_bounds=v_hg_lower_bounds, v_w_in=v_w_in, v_w_conv=v_w_conv, v_w_pool=v_w_pool, v_pool_scale=v_pool_scale, v_hg_norm_g=v_hg_norm_g, v_w_o=v_w_o, v_ln1_g=v_ln1_g, v_ln1_b=v_ln1_b, v_w_up=v_w_up, v_w_ffn_conv=v_w_ffn_conv, v_b_ffn_conv=v_b_ffn_conv, v_w_down=v_w_down, v_ln2_g=v_ln2_g, v_ln2_b=v_ln2_b)
    weights = {n: given[n] for n in TWIN_WEIGHTS}
    shared = {n: given[n] for n in SHARED_INPUTS}
    per_example = {n: given[n] for n in ['x']}
    grad_fn = _jax.value_and_grad(_loss, argnums=(0, 1))

    def one_microbatch(ex, loss_target):
        ex = dict(ex)
        diff = ex.pop(TWIN_DIFF_INPUT)
        return grad_fn(weights, diff, {**shared, **ex}, loss_target)

    if N_MICROBATCH == 1:
        loss, (grad_w, grad_x) = one_microbatch(per_example, given["loss_target"])
    else:
        def body(carry, xs):
            loss_sum, grad_sum = carry
            l_k, (gw_k, gx_k) = one_microbatch(xs[0], xs[1])
            with _jax.named_scope("update"):
                return (loss_sum + l_k, _jax.tree.map(_jnp.add, grad_sum, gw_k)), gx_k

        init = (_jnp.zeros((), _jnp.float32), _jax.tree.map(_jnp.zeros_like, weights))
        (loss, grad_w), grad_x = _jax.lax.scan(body, init, (per_example, given["loss_target"]))
    with _jax.named_scope("update"):
        delta_w, new_m, new_v = {}, {}, {}
        for n in TWIN_WEIGHTS:
            delta_w[n], new_m[n], new_v[n] = _adamw(weights[n], grad_w[n], given["m_" + n], given["v_" + n])
    return (loss, grad_x, *[grad_w[n] for n in TWIN_WEIGHTS], *[delta_w[n] for n in TWIN_WEIGHTS],
            *[new_m[n] for n in TWIN_WEIGHTS], *[new_v[n] for n in TWIN_WEIGHTS])
```

```python
import jax
import jax.numpy as jnp
from jax import lax
from jax.experimental import pallas as pl
from jax.experimental.pallas import tpu as pltpu

F32 = jnp.float32
BF16 = jnp.bfloat16

N_DEV = 8
D_MODEL = 1024
N_META = 16
DEPTH = 2
CONV_W = 256
HG_W = 512
HG_D = 128
HG_HEADS = 4
POOL_W = 256
POOL_GROUP = 64
D_FF = 2816
ALPHA = (2 * DEPTH) ** 0.25
LN_EPS = 1e-5
RMS_EPS = 1e-6
F_FLOOR = 1e-30
Q_SCALE = HG_D ** -0.5
SUB = 16
SEQ_TILE = 192
FFN_TILE = 96
ROW_ALIGN = 192
LANES = 128
VMEM_LIMIT = 48 * 1024 * 1024

ADAM_LR = 0.001
ADAM_B1 = 0.9
ADAM_B2 = 0.999
ADAM_EPS = 1e-08
ADAM_WD = 0.01
ADAM_STEP = 10


def _tile(n, cap, mult):
    best = 0
    for t in range(mult, min(n, cap) + 1, mult):
        if n % t == 0:
            best = t
    assert best > 0, (n, cap, mult)
    return best


def _params(sem, vmem=VMEM_LIMIT):
    return pltpu.CompilerParams(dimension_semantics=sem, vmem_limit_bytes=vmem)


def _dnt(a, b):
    return lax.dot_general(a, b, (((1,), (1,)), ((), ())), preferred_element_type=F32)


def _dtn(a, b):
    return lax.dot_general(a, b, (((0,), (0,)), ((), ())), preferred_element_type=F32)


def _dnn(a, b):
    return jnp.dot(a, b, preferred_element_type=F32)


def _sigmoid(x):
    return 1.0 / (1.0 + jnp.exp(-x))


def _matmul(a, b, mode, out_dtype, name, res=None, alpha=1.0):
    if mode == "tn":
        K, M = a.shape
    else:
        M, K = a.shape
    N = b.shape[0] if mode == "nt" else b.shape[1]
    tm = _tile(M, 1536, 16)
    tn = _tile(N, 1536, LANES)
    tk = _tile(K, 1536, 16) if mode == "tn" else _tile(K, 1536, LANES)
    if mode == "tn":
        tk = _tile(K, 768, 16)
    nk = K // tk
    dims = {"nn": ((1,), (0,)), "nt": ((1,), (1,)), "tn": ((0,), (0,))}[mode]

    def body(*refs):
        if res is None:
            a_ref, b_ref, o_ref, acc = refs
            r_ref = None
        else:
            a_ref, b_ref, r_ref, o_ref, acc = refs
        k = pl.program_id(2)
        p = lax.dot_general(a_ref[...].astype(BF16), b_ref[...].astype(BF16), (dims, ((), ())),
                            preferred_element_type=F32)

        @pl.when(k == 0)
        def _():
            acc[...] = p

        @pl.when(k > 0)
        def _():
            acc[...] += p

        @pl.when(k == nk - 1)
        def _():
            r = acc[...]
            if r_ref is not None:
                r = r + alpha * r_ref[...]
            o_ref[...] = r.astype(out_dtype)

    if mode == "tn":
        a_spec = pl.BlockSpec((tk, tm), lambda i, j, k: (k, i))
    else:
        a_spec = pl.BlockSpec((tm, tk), lambda i, j, k: (i, k))
    if mode == "nt":
        b_spec = pl.BlockSpec((tn, tk), lambda i, j, k: (j, k))
    else:
        b_spec = pl.BlockSpec((tk, tn), lambda i, j, k: (k, j))
    in_specs = [a_spec, b_spec]
    args = [a, b]
    if res is not None:
        in_specs.append(pl.BlockSpec((tm, tn), lambda i, j, k: (i, j)))
        args.append(res)
    return pl.pallas_call(
        body, name=name,
        grid=(M // tm, N // tn, nk),
        in_specs=in_specs,
        out_specs=pl.BlockSpec((tm, tn), lambda i, j, k: (i, j)),
        out_shape=jax.ShapeDtypeStruct((M, N), out_dtype),
        scratch_shapes=[pltpu.VMEM((tm, tn), F32)],
        compiler_params=_params(("parallel", "parallel", "arbitrary")),
    )(*args)


def _ln_fwd(x, f, g, b, name):
    L, D = x.shape
    tr = _tile(L, 768, 16)

    def body(x_ref, f_ref, g_ref, b_ref, y_ref, yb_ref):
        z = ALPHA * x_ref[...] + f_ref[...]
        mu = jnp.mean(z, axis=-1, keepdims=True)
        zc = z - mu
        var = jnp.mean(zc * zc, axis=-1, keepdims=True)
        y = zc * lax.rsqrt(var + LN_EPS) * g_ref[...] + b_ref[...]
        y_ref[...] = y
        yb_ref[...] = y.astype(BF16)

    row = pl.BlockSpec((tr, D), lambda i: (i, 0))
    vec = pl.BlockSpec((1, D), lambda i: (0, 0))
    return pl.pallas_call(
        body, name=name, grid=(L // tr,),
        in_specs=[row, row, vec, vec], out_specs=[row, row],
        out_shape=[jax.ShapeDtypeStruct((L, D), F32), jax.ShapeDtypeStruct((L, D), BF16)],
        compiler_params=_params(("parallel",)),
    )(x, f, g, b)


def _ln_bwd(x, f, dy, g, name):
    L, D = x.shape
    tr = _tile(L, 768, 16)

    def body(x_ref, f_ref, dy_ref, g_ref, dz_ref, dzb_ref, dg_ref, db_ref):
        @pl.when(pl.program_id(0) == 0)
        def _():
            dg_ref[...] = jnp.zeros_like(dg_ref)
            db_ref[...] = jnp.zeros_like(db_ref)

        z = ALPHA * x_ref[...] + f_ref[...]
        mu = jnp.mean(z, axis=-1, keepdims=True)
        zc = z - mu
        var = jnp.mean(zc * zc, axis=-1, keepdims=True)
        rstd = lax.rsqrt(var + LN_EPS)
        xhat = zc * rstd
        dy = dy_ref[...]
        dxh = dy * g_ref[...]
        m1 = jnp.mean(dxh, axis=-1, keepdims=True)
        m2 = jnp.mean(dxh * xhat, axis=-1, keepdims=True)
        dz = rstd * (dxh - m1 - xhat * m2)
        dz_ref[...] = dz
        dzb_ref[...] = dz.astype(BF16)
        dg_ref[...] += jnp.sum(dy * xhat, axis=0, keepdims=True)
        db_ref[...] += jnp.sum(dy, axis=0, keepdims=True)

    row = pl.BlockSpec((tr, D), lambda i: (i, 0))
    vec = pl.BlockSpec((1, D), lambda i: (0, 0))
    return pl.pallas_call(
        body, name=name, grid=(L // tr,),
        in_specs=[row, row, row, vec], out_specs=[row, row, vec, vec],
        out_shape=[jax.ShapeDtypeStruct((L, D), F32), jax.ShapeDtypeStruct((L, D), BF16),
                   jax.ShapeDtypeStruct((1, D), F32), jax.ShapeDtypeStruct((1, D), F32)],
        compiler_params=_params(("arbitrary",)),
    )(x, f, dy, g)


def _loss_head(y, tgt, seq):
    L, D = y.shape
    tr = _tile(L, 768, 16)

    def body(y_ref, t_ref, dy_ref, loss_ref):
        i = pl.program_id(0)

        @pl.when(i == 0)
        def _():
            loss_ref[...] = jnp.zeros_like(loss_ref)

        r = i * tr + lax.broadcasted_iota(jnp.int32, (tr, D), 0)
        valid = (r >= N_META) & (r < N_META + seq)
        e = jnp.where(valid, y_ref[...] - t_ref[...], 0.0)
        dy_ref[...] = e * (1.0 / D)
        s = jnp.sum(jnp.sum(e * e, axis=-1, keepdims=True), axis=0, keepdims=True)
        loss_ref[...] += (0.5 / D) * s

    row = pl.BlockSpec((tr, D), lambda i: (i, 0))
    return pl.pallas_call(
        body, name="loss_head", grid=(L // tr,),
        in_specs=[row, row], out_specs=[row, pl.BlockSpec((1, 1), lambda i: (0, 0))],
        out_shape=[jax.ShapeDtypeStruct((L, D), F32), jax.ShapeDtypeStruct((1, 1), F32)],
        compiler_params=_params(("arbitrary",)),
    )(y, tgt)


def _shift_down(x, prev, k):
    T = x.shape[0]
    row = lax.broadcasted_iota(jnp.int32, x.shape, 0)
    out = pltpu.roll(x, k, 0)
    for r in range(k):
        out = jnp.where(row == r, prev[8 - k + r:8 - k + r + 1, :], out)
    return out


def _shift_up(x, nxt, k):
    T = x.shape[0]
    row = lax.broadcasted_iota(jnp.int32, x.shape, 0)
    out = pltpu.roll(x, T - k, 0)
    for r in range(k):
        out = jnp.where(row == T - k + r, nxt[r:r + 1, :], out)
    return out


def _ffn_act_fwd(up, w, b, name):
    L, C = up.shape
    F = C // 2
    ts = FFN_TILE
    n = L // ts

    def body(up_ref, pv_ref, w_ref, b_ref, u_ref, a_ref):
        i = pl.program_id(0)
        x = up_ref[...]
        prev = jnp.where(i > 0, pv_ref[...], 0.0)
        w = w_ref[...]
        u = w[2:3, :] * x + w[1:2, :] * _shift_down(x, prev, 1) + w[0:1, :] * _shift_down(x, prev, 2) + b_ref[...]
        u_ref[...] = u
        gate = u[:, :F]
        a_ref[...] = (gate * _sigmoid(gate) * u[:, F:]).astype(BF16)

    return pl.pallas_call(
        body, name=name, grid=(n,),
        in_specs=[pl.BlockSpec((ts, C), lambda i: (i, 0)),
                  pl.BlockSpec((8, C), lambda i: (jnp.maximum(i * (ts // 8) - 1, 0), 0)),
                  pl.BlockSpec((3, C), lambda i: (0, 0)), pl.BlockSpec((1, C), lambda i: (0, 0))],
        out_specs=[pl.BlockSpec((ts, C), lambda i: (i, 0)), pl.BlockSpec((ts, F), lambda i: (i, 0))],
        out_shape=[jax.ShapeDtypeStruct((L, C), F32), jax.ShapeDtypeStruct((L, F), BF16)],
        compiler_params=_params(("parallel",)),
    )(up, up, w, b)


def _ffn_act_bwd(up, u, da, w, name):
    L, C = up.shape
    F = C // 2
    ts = FFN_TILE
    n = L // ts
    last8 = L // 8 - 1

    def du_of(u, da):
        gate, val = u[:, :F], u[:, F:]
        sg = _sigmoid(gate)
        dgate = da * val * (sg * (1.0 + gate * (1.0 - sg)))
        dval = da * (gate * sg)
        return jnp.concatenate([dgate, dval], axis=1)

    def body(up_ref, u_ref, un_ref, da_ref, dan_ref, w_ref, dup_ref, dw_ref, db_ref):
        i = pl.program_id(0)

        @pl.when(i == 0)
        def _():
            dw_ref[...] = jnp.zeros_like(dw_ref)
            db_ref[...] = jnp.zeros_like(db_ref)

        du = du_of(u_ref[...], da_ref[...])
        dun = jnp.where(i < n - 1, du_of(un_ref[...], dan_ref[...]), 0.0)
        du1 = _shift_up(du, dun, 1)
        du2 = _shift_up(du, dun, 2)
        w = w_ref[...]
        dup_ref[...] = (w[2:3, :] * du + w[1:2, :] * du1 + w[0:1, :] * du2).astype(BF16)
        x = up_ref[...]
        dw_ref[...] += jnp.concatenate([jnp.sum(x * du2, axis=0, keepdims=True),
                                        jnp.sum(x * du1, axis=0, keepdims=True),
                                        jnp.sum(x * du, axis=0, keepdims=True)], axis=0)
        db_ref[...] += jnp.sum(du, axis=0, keepdims=True)

    nxt = lambda i: (jnp.minimum((i + 1) * (ts // 8), last8), 0)
    return pl.pallas_call(
        body, name=name, grid=(n,),
        in_specs=[pl.BlockSpec((ts, C), lambda i: (i, 0)),
                  pl.BlockSpec((ts, C), lambda i: (i, 0)), pl.BlockSpec((8, C), nxt),
                  pl.BlockSpec((ts, F), lambda i: (i, 0)), pl.BlockSpec((8, F), nxt),
                  pl.BlockSpec((3, C), lambda i: (0, 0))],
        out_specs=[pl.BlockSpec((ts, C), lambda i: (i, 0)), pl.BlockSpec((3, C), lambda i: (0, 0)),
                   pl.BlockSpec((1, C), lambda i: (0, 0))],
        out_shape=[jax.ShapeDtypeStruct((L, C), BF16), jax.ShapeDtypeStruct((3, C), F32),
                   jax.ShapeDtypeStruct((1, C), F32)],
        compiler_params=_params(("arbitrary",)),
    )(up, u, u, da, da, w)


def _pool_window(ext, tile_rows, first_row, lead):
    T = ext.shape[0]
    sh = (lambda x, k: pltpu.roll(x, T - k, 0)) if lead else (lambda x, k: pltpu.roll(x, k, 0))
    r2 = ext + sh(ext, 1)
    r4 = r2 + sh(r2, 2)
    r8 = r4 + sh(r4, 4)
    r16 = r8 + sh(r8, 8)
    lo = 0 if lead else 16
    grp = lax.broadcasted_iota(jnp.int32, (tile_rows, POOL_W), 1) // POOL_GROUP
    pick = lambda a, b, c, d: jnp.where(grp == 0, a, jnp.where(grp == 1, b, jnp.where(grp == 2, c, d)))
    win = pick(r2[lo:lo + tile_rows], r4[lo:lo + tile_rows], r8[lo:lo + tile_rows], r16[lo:lo + tile_rows])
    return win, pick(2.0, 4.0, 8.0, 16.0)


def _pool_count(first_row, rows, wlen):
    t1 = (first_row + lax.broadcasted_iota(jnp.int32, (rows, POOL_W), 0) + 1).astype(F32)
    return jnp.minimum(t1, wlen)


def _cp_fwd(hc, wc, wblk, pscale, name):
    L = hc.shape[0]
    ts = SEQ_TILE
    n = L // ts

    def body(h_ref, hp_ref, wc_ref, wb_ref, ps_ref, y_ref):
        i = pl.program_id(0)
        h = h_ref[...]
        hp = jnp.where(i > 0, hp_ref[...], 0.0)
        cb, cc, cv, pv = h[:, 0:256], h[:, 256:512], h[:, 512:768], h[:, 768:1024]
        p = cc * cv
        pp = hp[8:16, 256:512] * hp[8:16, 512:768]
        w = wc_ref[...]
        conv = w[2:3, :] * p + w[1:2, :] * _shift_down(p, pp, 1) + w[0:1, :] * _shift_down(p, pp, 2)
        y_conv = cb * conv
        ext = jnp.concatenate([hp[:, 768:1024], pv], axis=0)
        win, wlen = _pool_window(ext, ts, i * ts, False)
        d = win / _pool_count(i * ts, ts, wlen) - pv
        y_pool = _dnn(d.astype(BF16), wb_ref[...]) * ps_ref[...]
        y_ref[...] = jnp.concatenate([y_conv, y_pool], axis=1).astype(BF16)

    return pl.pallas_call(
        body, name=name, grid=(n,),
        in_specs=[pl.BlockSpec((ts, 1024), lambda i: (i, 0)),
                  pl.BlockSpec((16, 1024), lambda i: (jnp.maximum(i * (ts // 16) - 1, 0), 0)),
                  pl.BlockSpec((3, 256), lambda i: (0, 0)), pl.BlockSpec((256, 256), lambda i: (0, 0)),
                  pl.BlockSpec((1, 256), lambda i: (0, 0))],
        out_specs=pl.BlockSpec((ts, 512), lambda i: (i, 0)),
        out_shape=jax.ShapeDtypeStruct((L, 512), BF16),
        compiler_params=_params(("parallel",)),
    )(hc, hc, wc, wblk, pscale)


def _cp_bwd(hc, dcat, wc, wblk, pscale, name):
    L = hc.shape[0]
    ts = SEQ_TILE
    n = L // ts
    last16 = L // 16 - 1

    def body(h_ref, hp_ref, hn_ref, dy_ref, dyn_ref, wc_ref, wb_ref, ps_ref,
             dh_ref, dwc_ref, dwb_ref, dps_ref):
        i = pl.program_id(0)

        @pl.when(i == 0)
        def _():
            dwc_ref[...] = jnp.zeros_like(dwc_ref)
            dwb_ref[...] = jnp.zeros_like(dwb_ref)
            dps_ref[...] = jnp.zeros_like(dps_ref)

        h = h_ref[...]
        hp = jnp.where(i > 0, hp_ref[...], 0.0)
        hn = hn_ref[...]
        dy = dy_ref[...]
        dyn = jnp.where(i < n - 1, dyn_ref[...], 0.0)
        cb, cc, cv, pv = h[:, 0:256], h[:, 256:512], h[:, 512:768], h[:, 768:1024]
        w = wc_ref[...]
        p = cc * cv
        pp = hp[8:16, 256:512] * hp[8:16, 512:768]
        p1 = _shift_down(p, pp, 1)
        p2 = _shift_down(p, pp, 2)
        conv = w[2:3, :] * p + w[1:2, :] * p1 + w[0:1, :] * p2
        dyc = dy[:, 0:256]
        dcb = dyc * conv
        dconv = dyc * cb
        dconv_n = dyn[0:8, 0:256] * hn[0:8, 0:256]
        dc1 = _shift_up(dconv, dconv_n, 1)
        dc2 = _shift_up(dconv, dconv_n, 2)
        dp = w[2:3, :] * dconv + w[1:2, :] * dc1 + w[0:1, :] * dc2
        dwc_ref[...] += jnp.concatenate([jnp.sum(p * dc2, axis=0, keepdims=True),
                                         jnp.sum(p * dc1, axis=0, keepdims=True),
                                         jnp.sum(p * dconv, axis=0, keepdims=True)], axis=0)
        ps = ps_ref[...]
        wb = wb_ref[...]
        ext = jnp.concatenate([hp[:, 768:1024], pv], axis=0)
        win, wlen = _pool_window(ext, ts, i * ts, False)
        d = win / _pool_count(i * ts, ts, wlen) - pv
        db = d.astype(BF16)
        dyp = dy[:, 256:512]
        dps_ref[...] += jnp.sum(dyp * _dnn(db, wb), axis=0, keepdims=True)
        dypre = (dyp * ps).astype(BF16)
        dwb_ref[...] += _dtn(db, dypre)
        dd = _dnt(dypre, wb)
        ddn = _dnt((dyn[:, 256:512] * ps).astype(BF16), wb)
        e = dd / _pool_count(i * ts, ts, wlen)
        en = ddn / _pool_count((i + 1) * ts, 16, wlen[0:16])
        lead, _ = _pool_window(jnp.concatenate([e, en], axis=0), ts, i * ts, True)
        dpv = lead - dd
        dh_ref[...] = jnp.concatenate([dcb, dp * cv, dp * cc, dpv], axis=1).astype(BF16)

    return pl.pallas_call(
        body, name=name, grid=(n,),
        in_specs=[pl.BlockSpec((ts, 1024), lambda i: (i, 0)),
                  pl.BlockSpec((16, 1024), lambda i: (jnp.maximum(i * (ts // 16) - 1, 0), 0)),
                  pl.BlockSpec((16, 1024), lambda i: (jnp.minimum((i + 1) * (ts // 16), last16), 0)),
                  pl.BlockSpec((ts, 512), lambda i: (i, 1)),
                  pl.BlockSpec((16, 512), lambda i: (jnp.minimum((i + 1) * (ts // 16), last16), 1)),
                  pl.BlockSpec((3, 256), lambda i: (0, 0)), pl.BlockSpec((256, 256), lambda i: (0, 0)),
                  pl.BlockSpec((1, 256), lambda i: (0, 0))],
        out_specs=[pl.BlockSpec((ts, 1024), lambda i: (i, 0)), pl.BlockSpec((3, 256), lambda i: (0, 0)),
                   pl.BlockSpec((256, 256), lambda i: (0, 0)), pl.BlockSpec((1, 256), lambda i: (0, 0))],
        out_shape=[jax.ShapeDtypeStruct((L, 1024), BF16), jax.ShapeDtypeStruct((3, 256), F32),
                   jax.ShapeDtypeStruct((256, 256), F32), jax.ShapeDtypeStruct((1, 256), F32)],
        compiler_params=_params(("arbitrary",)),
    )(hc, hc, hc, dcat, dcat, wc, wblk, pscale)


def _lower_bound(lb_ref, layer):
    b0, b1 = lb_ref[0:1, :], lb_ref[1:2, :]
    m = jnp.maximum(b0, b1)
    e0, e1 = jnp.exp(b0 - m), jnp.exp(b1 - m)
    p0, p1 = e0 / (e0 + e1), e1 / (e0 + e1)
    lb = (p0 - p0) if layer == 0 else ((p0 + p1) - p0)
    return lb, p0, p1


def _cumsum_rows(x, reverse=False):
    row = lax.broadcasted_iota(jnp.int32, x.shape, 0)
    for sh in (1, 2, 4, 8):
        if reverse:
            x = x + jnp.where(row < SUB - sh, pltpu.roll(x, SUB - sh, 0), 0.0)
        else:
            x = x + jnp.where(row >= sh, pltpu.roll(x, sh, 0), 0.0)
    return x


def _gates(fz, lb):
    sig = _sigmoid(fz)
    f = lb + (1.0 - lb) * sig
    g = jnp.log(jnp.maximum(f, F_FLOOR))
    k = (1.0 - lb) * (1.0 - sig)
    return sig, f, g, k


def _head(h):
    return slice(h * HG_D, (h + 1) * HG_D)


def _hgrn_fwd(hh, lbp, gnorm, layer, name):
    L = hh.shape[0]
    ts = SEQ_TILE
    n = L // ts
    nsub = ts // SUB

    def body(q_ref, f_ref, i_ref, g_ref, lb_ref, gn_ref, y_ref, o_ref, s_ref, St):
        @pl.when(pl.program_id(0) == 0)
        def _():
            St[...] = jnp.zeros_like(St)

        lb, _, _ = _lower_bound(lb_ref, layer)
        gn = jnp.tile(gn_ref[...], (1, HG_HEADS))
        r16 = lax.broadcasted_iota(jnp.int32, (SUB, SUB), 0)
        c16 = lax.broadcasted_iota(jnp.int32, (SUB, SUB), 1)

        def block(j, carry):
            rows = pl.ds(pl.multiple_of(j * SUB, SUB), SUB)
            q = q_ref[rows, :] * Q_SCALE
            iv = i_ref[rows, :]
            gz = g_ref[rows, :]
            _, _, g, k = _gates(f_ref[rows, :], lb)
            G = _cumsum_rows(g)
            Gl = G[SUB - 1:SUB, :]
            qt = (q * jnp.exp(G)).astype(BF16)
            kd = (k * jnp.exp(Gl - G)).astype(BF16)
            eGl = jnp.exp(Gl)
            ib = iv.astype(BF16)
            A = [jnp.zeros((SUB, SUB), F32) for _ in range(HG_HEADS)]
            for s in range(SUB):
                P = q * jnp.exp(jnp.minimum(G - G[s:s + 1, :], 0.0)) * k[s:s + 1, :]
                for h in range(HG_HEADS):
                    A[h] = jnp.where(c16 == s, jnp.sum(P[:, _head(h)], axis=-1, keepdims=True), A[h])
            outs, ons = [], []
            for h in range(HG_HEADS):
                sl = _head(h)
                Sb = St[h].astype(BF16)
                s_ref[j, sl, :] = Sb
                Am = jnp.where(r16 >= c16, A[h], 0.0).astype(BF16)
                o = _dnt(qt[:, sl], Sb) + _dnn(Am, ib[:, sl])
                St[h] = eGl[:, sl] * St[h] + _dtn(ib[:, sl], kd[:, sl])
                outs.append(o)
                ons.append(o * lax.rsqrt(jnp.mean(o * o, axis=-1, keepdims=True) + RMS_EPS))
            o_ref[rows, :] = jnp.concatenate(outs, axis=1)
            y = jnp.concatenate(ons, axis=1) * gn * (gz * _sigmoid(gz))
            y_ref[rows, :] = y.astype(BF16)
            return carry

        lax.fori_loop(0, nsub, block, 0)

    col = lambda c: pl.BlockSpec((ts, HG_W), lambda i: (i, c))
    return pl.pallas_call(
        body, name=name, grid=(n,),
        in_specs=[col(0), col(1), col(2), col(3), pl.BlockSpec((2, HG_W), lambda i: (0, 0)),
                  pl.BlockSpec((1, HG_D), lambda i: (0, 0))],
        out_specs=[pl.BlockSpec((ts, HG_W), lambda i: (i, 0)), pl.BlockSpec((ts, HG_W), lambda i: (i, 0)),
                   pl.BlockSpec((nsub, HG_W, HG_D), lambda i: (i, 0, 0))],
        out_shape=[jax.ShapeDtypeStruct((L, HG_W), BF16), jax.ShapeDtypeStruct((L, HG_W), F32),
                   jax.ShapeDtypeStruct((L // SUB, HG_W, HG_D), BF16)],
        scratch_shapes=[pltpu.VMEM((HG_HEADS, HG_D, HG_D), F32)],
        compiler_params=_params(("arbitrary",)),
    )(hh, hh, hh, hh, lbp, gnorm)


def _hgrn_bwd(hh, o_raw, states, dcat, lbp, gnorm, layer, name):
    L = hh.shape[0]
    ts = SEQ_TILE
    n = L // ts
    nsub = ts // SUB

    def body(q_ref, f_ref, i_ref, g_ref, o_ref, s_ref, dy_ref, lb_ref, gn_ref,
             dh_ref, dlb_ref, dgn_ref, dSt, dlb_acc):
        step = pl.program_id(0)

        @pl.when(step == 0)
        def _():
            dSt[...] = jnp.zeros_like(dSt)
            dlb_acc[...] = jnp.zeros_like(dlb_acc)
            dgn_ref[...] = jnp.zeros_like(dgn_ref)

        lb, p0, p1 = _lower_bound(lb_ref, layer)
        gnh = gn_ref[...]
        gn = jnp.tile(gnh, (1, HG_HEADS))
        r16 = lax.broadcasted_iota(jnp.int32, (SUB, SUB), 0)
        c16 = lax.broadcasted_iota(jnp.int32, (SUB, SUB), 1)

        def block(jj, carry):
            j = nsub - 1 - jj
            rows = pl.ds(pl.multiple_of(j * SUB, SUB), SUB)
            q = q_ref[rows, :] * Q_SCALE
            iv = i_ref[rows, :]
            gz = g_ref[rows, :]
            o = o_ref[rows, :]
            dy = dy_ref[rows, :]
            sig, f, g, k = _gates(f_ref[rows, :], lb)
            G = _cumsum_rows(g)
            Gl = G[SUB - 1:SUB, :]
            eG = jnp.exp(G)
            edl = jnp.exp(Gl - G)
            eGl = jnp.exp(Gl)
            qt = (q * eG).astype(BF16)
            kd = (k * edl).astype(BF16)
            ib = iv.astype(BF16)
            sgz = _sigmoid(gz)
            sil = gz * sgz
            dyn = dy * sil
            on_parts, do_parts = [], []
            dgn = jnp.zeros((1, HG_D), F32)
            for h in range(HG_HEADS):
                sl = _head(h)
                oh = o[:, sl]
                rs = lax.rsqrt(jnp.mean(oh * oh, axis=-1, keepdims=True) + RMS_EPS)
                on = oh * rs
                dgn = dgn + jnp.sum(dyn[:, sl] * on, axis=0, keepdims=True)
                don = dyn[:, sl] * gnh
                do_parts.append(rs * (don - on * jnp.mean(don * on, axis=-1, keepdims=True)))
                on_parts.append(on)
            dgn_ref[...] += dgn
            on_all = jnp.concatenate(on_parts, axis=1)
            dgz = dy * on_all * gn * (sgz * (1.0 + gz * (1.0 - sgz)))
            do = jnp.concatenate(do_parts, axis=1)
            dob = do.astype(BF16)
            Ap = [jnp.where(r16 >= c16, _dnt(dob[:, _head(h)], ib[:, _head(h)]), 0.0) for h in range(HG_HEADS)]
            A = [jnp.zeros((SUB, SUB), F32) for _ in range(HG_HEADS)]
            dq = jnp.zeros((SUB, HG_W), F32)
            dk_rows = []
            for s in range(SUB):
                E = jnp.exp(jnp.minimum(G - G[s:s + 1, :], 0.0))
                P = q * E * k[s:s + 1, :]
                for h in range(HG_HEADS):
                    A[h] = jnp.where(c16 == s, jnp.sum(P[:, _head(h)], axis=-1, keepdims=True), A[h])
                t1 = jnp.concatenate([Ap[h][:, s:s + 1] * E[:, _head(h)] for h in range(HG_HEADS)], axis=1)
                dq = dq + t1 * k[s:s + 1, :]
                dk_rows.append(jnp.sum(t1 * q, axis=0, keepdims=True))
            dk = jnp.concatenate(dk_rows, axis=0)
            dq_p, dk_p, di_p, tail_p = [], [], [], []
            for h in range(HG_HEADS):
                sl = _head(h)
                Sb = s_ref[j, sl, :]
                dSb = dSt[h].astype(BF16)
                Am = jnp.where(r16 >= c16, A[h], 0.0).astype(BF16)
                dq_p.append(eG[:, sl] * _dnn(dob[:, sl], Sb))
                dk_p.append(edl[:, sl] * _dnn(ib[:, sl], dSb))
                di_p.append(_dtn(Am, dob[:, sl]) + _dnt(kd[:, sl], dSb))
                St_end = eGl[:, sl] * Sb.astype(F32) + _dtn(ib[:, sl], kd[:, sl])
                tail_p.append(jnp.sum(dSt[h] * St_end, axis=0, keepdims=True))
                dSt[h] = eGl[:, sl] * dSt[h] + _dtn(dob[:, sl], qt[:, sl])
            dq = dq + jnp.concatenate(dq_p, axis=1)
            dk = dk + jnp.concatenate(dk_p, axis=1)
            di = jnp.concatenate(di_p, axis=1)
            dg = _cumsum_rows(q * dq - k * dk, reverse=True) + jnp.concatenate(tail_p, axis=1)
            df = jnp.where(f > F_FLOOR, dg / f, 0.0)
            dfk = df - dk
            dfz = (1.0 - lb) * dfk * sig * (1.0 - sig)
            dlb_acc[...] += jnp.sum(dfk * (1.0 - sig), axis=0, keepdims=True)
            dh_ref[rows, :] = jnp.concatenate([dq * Q_SCALE, dfz, di, dgz], axis=1).astype(BF16)
            return carry

        lax.fori_loop(0, nsub, block, 0)

        @pl.when(step == n - 1)
        def _():
            if layer == 0:
                dlb_ref[...] = jnp.zeros_like(dlb_ref)
            else:
                dz1 = p0 * p1 * dlb_acc[...]
                dlb_ref[...] = jnp.concatenate([-dz1, dz1], axis=0)

    rev = lambda i: n - 1 - i
    col = lambda c: pl.BlockSpec((ts, HG_W), lambda i: (rev(i), c))
    return pl.pallas_call(
        body, name=name, grid=(n,),
        in_specs=[col(0), col(1), col(2), col(3), col(0),
                  pl.BlockSpec((nsub, HG_W, HG_D), lambda i: (rev(i), 0, 0)), col(0),
                  pl.BlockSpec((2, HG_W), lambda i: (0, 0)), pl.BlockSpec((1, HG_D), lambda i: (0, 0))],
        out_specs=[pl.BlockSpec((ts, 4 * HG_W), lambda i: (rev(i), 0)),
                   pl.BlockSpec((2, HG_W), lambda i: (0, 0)), pl.BlockSpec((1, HG_D), lambda i: (0, 0))],
        out_shape=[jax.ShapeDtypeStruct((L, 4 * HG_W), BF16), jax.ShapeDtypeStruct((2, HG_W), F32),
                   jax.ShapeDtypeStruct((1, HG_D), F32)],
        scratch_shapes=[pltpu.VMEM((HG_HEADS, HG_D, HG_D), F32), pltpu.VMEM((1, HG_W), F32)],
        compiler_params=_params(("arbitrary",)),
    )(hh, hh, hh, hh, o_raw, states, dcat, lbp, gnorm)


def _adamw(gparts, w, m, v, name):
    R = w.shape[0]
    tr = _tile(R, 1024, 8)
    c1 = 1.0 - ADAM_B1 ** ADAM_STEP
    c2 = 1.0 - ADAM_B2 ** ADAM_STEP

    def body(gp_ref, w_ref, m_ref, v_ref, g_ref, d_ref, mo_ref, vo_ref):
        g = gp_ref[0]
        for k in range(1, N_DEV):
            g = g + gp_ref[k]
        mn = ADAM_B1 * m_ref[...] + (1.0 - ADAM_B1) * g
        vn = ADAM_B2 * v_ref[...] + (1.0 - ADAM_B2) * (g * g)
        m_hat = mn / c1
        v_hat = vn / c2
        g_ref[...] = g
        d_ref[...] = -ADAM_LR * (m_hat / (jnp.sqrt(v_hat) + ADAM_EPS) + ADAM_WD * w_ref[...])
        mo_ref[...] = mn
        vo_ref[...] = vn

    row = pl.BlockSpec((tr, LANES), lambda i: (i, 0))
    shp = jax.ShapeDtypeStruct((R, LANES), F32)
    return pl.pallas_call(
        body, name=name, grid=(R // tr,),
        in_specs=[pl.BlockSpec((N_DEV, tr, LANES), lambda i: (0, i, 0)), row, row, row],
        out_specs=[row, row, row, row], out_shape=[shp, shp, shp, shp],
        compiler_params=_params(("parallel",)),
    )(gparts, w, m, v)


def _flip(coord, bit):
    return 1 - coord if bit else coord


def _all_gather(block, name):
    R, C = block.shape

    def body(x_ref, out_ref, send_sems, recv_sems, local_sem):
        x, y, c = lax.axis_index("x"), lax.axis_index("y"), lax.axis_index("c")
        me, sibling = (x, y, c), (x, y, 1 - c)
        chips = [(1 - x, y), (x, 1 - y), (1 - x, 1 - y)]

        def slot(px, py, pc):
            return out_ref.at[4 * px + 2 * py + pc]

        def copy(k, blk, to, src=None):
            return pltpu.make_async_remote_copy(
                src_ref=slot(*blk) if src is None else src, dst_ref=slot(*blk),
                send_sem=send_sems.at[k], recv_sem=recv_sems.at[k],
                device_id=to, device_id_type=pl.DeviceIdType.MESH)

        mine = pltpu.make_async_copy(x_ref, slot(*me), local_sem)
        mine.start()
        first = [copy(0, me, sibling, src=x_ref)]
        first += [copy(1 + j, me, (*chip, c), src=x_ref) for j, chip in enumerate(chips)]
        for cp in first:
            cp.start()
        passed = [copy(4 + j, (*chip, c), sibling) for j, chip in enumerate(chips)]
        for j, chip in enumerate(chips):
            copy(1 + j, (*chip, c), me).wait_recv()
            passed[j].start()
        copy(0, sibling, me).wait_recv()
        for j, chip in enumerate(chips):
            copy(4 + j, (*chip, 1 - c), me).wait_recv()
        for cp in first + passed:
            cp.wait_send()
        mine.wait()

    return pl.pallas_call(
        body, name=name,
        out_shape=jax.ShapeDtypeStruct((N_DEV, R, C), block.dtype),
        in_specs=[pl.BlockSpec(memory_space=pl.ANY)],
        out_specs=pl.BlockSpec(memory_space=pl.ANY),
        scratch_shapes=[pltpu.SemaphoreType.DMA((7,)), pltpu.SemaphoreType.DMA((7,)), pltpu.SemaphoreType.DMA],
    )(block)


def _all_to_all(chunks, name):
    _, R, C = chunks.shape

    def body(x_ref, out_ref, send_sems, recv_sems, local_sem):
        x, y, c = lax.axis_index("x"), lax.axis_index("y"), lax.axis_index("c")
        me = 4 * x + 2 * y + c

        mine = pltpu.make_async_copy(x_ref.at[me], out_ref.at[me], local_sem)
        mine.start()
        copies = []
        for k in range(1, N_DEV):
            px, py, pc = _flip(x, k & 4), _flip(y, k & 2), _flip(c, k & 1)
            peer = 4 * px + 2 * py + pc
            send = pltpu.make_async_remote_copy(
                src_ref=x_ref.at[peer], dst_ref=out_ref.at[me],
                send_sem=send_sems.at[k - 1], recv_sem=recv_sems.at[k - 1],
                device_id=(px, py, pc), device_id_type=pl.DeviceIdType.MESH)
            recv = pltpu.make_async_remote_copy(
                src_ref=x_ref.at[peer], dst_ref=out_ref.at[peer],
                send_sem=send_sems.at[k - 1], recv_sem=recv_sems.at[k - 1],
                device_id=(px, py, pc), device_id_type=pl.DeviceIdType.MESH)
            send.start()
            copies.append((send, recv))
        for send, recv in copies:
            recv.wait_recv()
        for send, recv in copies:
            send.wait_send()
        mine.wait()

    return pl.pallas_call(
        body, name=name,
        out_shape=jax.ShapeDtypeStruct(chunks.shape, chunks.dtype),
        in_specs=[pl.BlockSpec(memory_space=pl.ANY)],
        out_specs=pl.BlockSpec(memory_space=pl.ANY),
        scratch_shapes=[pltpu.SemaphoreType.DMA((7,)), pltpu.SemaphoreType.DMA((7,)), pltpu.SemaphoreType.DMA],
    )(chunks)


def _pack(arrs, dtype, row_mult=8):
    parts, offs, r = [], [], 0
    for a in arrs:
        flat = a.astype(dtype).reshape(-1)
        nrow = -(-flat.shape[0] // LANES)
        flat = jnp.pad(flat, (0, nrow * LANES - flat.shape[0]))
        parts.append(flat.reshape(nrow, LANES))
        offs.append((r, nrow))
        r += nrow
    pad = (-r) % row_mult
    if pad:
        parts.append(jnp.zeros((pad, LANES), dtype))
    return jnp.concatenate(parts, axis=0), offs


def _unpack(buf, offs, shapes, lead=()):
    outs = []
    for (r, nrow), shp in zip(offs, shapes):
        size = 1
        for s in shp:
            size *= s
        flat = buf[..., r:r + nrow, :].reshape(lead + (nrow * LANES,))
        outs.append(flat[..., :size].reshape(lead + tuple(shp)))
    return outs


def _cols_from_shards(g, axis):
    return jnp.concatenate([g[j] for j in range(N_DEV)], axis=axis)


def _shards_of(a, axis):
    return jnp.stack(jnp.split(a, N_DEV, axis=axis), axis=0)


BIG = ("w_in", "w_o", "w_up", "w_down")
BIG_AXIS = {"w_in": 2, "w_o": 1, "w_up": 2, "w_down": 1}
SMALL_SHARDED = ("meta_tokens", "w_conv", "w_ffn_conv")
SMALL_AXIS = {"meta_tokens": 1, "w_conv": 1, "w_ffn_conv": 1}
REPLICATED = ("hg_lower_bounds", "w_pool", "pool_scale", "hg_norm_g", "ln1_g", "ln1_b", "b_ffn_conv", "ln2_g", "ln2_b")
WEIGHTS = ("meta_tokens", "hg_lower_bounds", "w_in", "w_conv", "w_pool", "pool_scale", "hg_norm_g", "w_o",
           "ln1_g", "ln1_b", "w_up", "w_ffn_conv", "b_ffn_conv", "w_down", "ln2_g", "ln2_b")


def _pool_blockdiag(w_pool_l):
    z = jnp.zeros((POOL_GROUP, POOL_GROUP), w_pool_l.dtype)
    rows = [jnp.concatenate([w_pool_l[g] if h == g else z for h in range(4)], axis=1) for g in range(4)]
    return jnp.concatenate(rows, axis=0)


def _layer_weights(full, l):
    w_in = full["w_in"][l]
    w_o = full["w_o"][l]
    return dict(
        w_hg=w_in[:, 768:2816],
        w_cp=jnp.concatenate([w_in[:, 0:768], w_in[:, 2816:3072]], axis=1),
        w_o=jnp.concatenate([w_o[256:768], w_o[0:256], w_o[768:1024]], axis=0),
        w_up=full["w_up"][l], w_down=full["w_down"][l],
    )


def kernel(x, meta_tokens, hg_lower_bounds, w_in, w_conv, w_pool, pool_scale, hg_norm_g, w_o, ln1_g, ln1_b, w_up, w_ffn_conv, b_ffn_conv, w_down, ln2_g, ln2_b, loss_target, m_meta_tokens, m_hg_lower_bounds, m_w_in, m_w_conv, m_w_pool, m_pool_scale, m_hg_norm_g, m_w_o, m_ln1_g, m_ln1_b, m_w_up, m_w_ffn_conv, m_b_ffn_conv, m_w_down, m_ln2_g, m_ln2_b, v_meta_tokens, v_hg_lower_bounds, v_w_in, v_w_conv, v_w_pool, v_pool_scale, v_hg_norm_g, v_w_o, v_ln1_g, v_ln1_b, v_w_up, v_w_ffn_conv, v_b_ffn_conv, v_w_down, v_ln2_g, v_ln2_b):
    W = dict(meta_tokens=meta_tokens, hg_lower_bounds=hg_lower_bounds, w_in=w_in, w_conv=w_conv, w_pool=w_pool,
             pool_scale=pool_scale, hg_norm_g=hg_norm_g, w_o=w_o, ln1_g=ln1_g, ln1_b=ln1_b, w_up=w_up,
             w_ffn_conv=w_ffn_conv, b_ffn_conv=b_ffn_conv, w_down=w_down, ln2_g=ln2_g, ln2_b=ln2_b)
    M = dict(meta_tokens=m_meta_tokens, hg_lower_bounds=m_hg_lower_bounds, w_in=m_w_in, w_conv=m_w_conv,
             w_pool=m_w_pool, pool_scale=m_pool_scale, hg_norm_g=m_hg_norm_g, w_o=m_w_o, ln1_g=m_ln1_g,
             ln1_b=m_ln1_b, w_up=m_w_up, w_ffn_conv=m_w_ffn_conv, b_ffn_conv=m_b_ffn_conv, w_down=m_w_down,
             ln2_g=m_ln2_g, ln2_b=m_ln2_b)
    V = dict(meta_tokens=v_meta_tokens, hg_lower_bounds=v_hg_lower_bounds, w_in=v_w_in, w_conv=v_w_conv,
             w_pool=v_w_pool, pool_scale=v_pool_scale, hg_norm_g=v_hg_norm_g, w_o=v_w_o, ln1_g=v_ln1_g,
             ln1_b=v_ln1_b, w_up=v_w_up, w_ffn_conv=v_w_ffn_conv, b_ffn_conv=v_b_ffn_conv, w_down=v_w_down,
             ln2_g=v_ln2_g, ln2_b=v_ln2_b)
    assert x.shape[0] == 1 and x.shape[2] == D_MODEL and w_in.shape[0] == DEPTH
    seq = x.shape[1]
    L = -(-(seq + N_META) // ROW_ALIGN) * ROW_ALIGN

    big_pack, big_offs = _pack([W[n] for n in BIG], BF16, 16)
    small_pack, small_offs = _pack([W[n] for n in SMALL_SHARDED], F32)
    big_all = _all_gather(big_pack, "gather_weights")
    small_all = _all_gather(small_pack, "gather_small")
    full = {}
    for n, a in zip(BIG, _unpack(big_all, big_offs, [W[n].shape for n in BIG], (N_DEV,))):
        full[n] = _cols_from_shards(a, BIG_AXIS[n])
    for n, a in zip(SMALL_SHARDED, _unpack(small_all, small_offs, [W[n].shape for n in SMALL_SHARDED], (N_DEV,))):
        full[n] = _cols_from_shards(a, SMALL_AXIS[n])

    pad_rows = L - N_META - seq
    xp = jnp.concatenate([full["meta_tokens"], x[0], jnp.zeros((pad_rows, D_MODEL), F32)], axis=0)
    tgt = jnp.concatenate([jnp.zeros((N_META, D_MODEL), F32), loss_target[0], jnp.zeros((pad_rows, D_MODEL), F32)], axis=0)

    saved = []
    h_in, h_in_b = xp, xp.astype(BF16)
    for l in range(DEPTH):
        lw = _layer_weights(full, l)
        wc = full["w_conv"][l].T
        wblk = _pool_blockdiag(w_pool[l]).astype(BF16)
        ps = pool_scale[l][None, :]
        gn = hg_norm_g[l][None, :]
        wf = full["w_ffn_conv"][l].T
        bf = b_ffn_conv[l][None, :]
        hh = _matmul(h_in_b, lw["w_hg"], "nn", F32, f"fwd_hg_{l}")
        hc = _matmul(h_in_b, lw["w_cp"], "nn", F32, f"fwd_cp_{l}")
        y_hg, o_raw, states = _hgrn_fwd(hh, hg_lower_bounds, gn, l, f"hgrn_fwd_{l}")
        y_cp = _cp_fwd(hc, wc, wblk, ps, f"convpool_fwd_{l}")
        cat = jnp.concatenate([y_hg, y_cp], axis=1)
        mix = _matmul(cat, lw["w_o"], "nn", F32, f"fwd_o_{l}")
        x1, x1_b = _ln_fwd(h_in, mix, ln1_g[l][None, :], ln1_b[l][None, :], f"ln1_fwd_{l}")
        up = _matmul(x1_b, lw["w_up"], "nn", F32, f"fwd_up_{l}")
        u, a = _ffn_act_fwd(up, wf, bf, f"ffn_fwd_{l}")
        ffn = _matmul(a, lw["w_down"], "nn", F32, f"fwd_down_{l}")
        x2, x2_b = _ln_fwd(x1, ffn, ln2_g[l][None, :], ln2_b[l][None, :], f"ln2_fwd_{l}")
        saved.append(dict(lw=lw, wc=wc, wblk=wblk, ps=ps, gn=gn, wf=wf, x=h_in, x_b=h_in_b, hh=hh, hc=hc,
                          o_raw=o_raw, states=states, cat=cat, mix=mix, x1=x1, x1_b=x1_b, up=up, u=u, a=a, ffn=ffn))
        h_in, h_in_b = x2, x2_b

    dy, loss_part = _loss_head(h_in, tgt, seq)
    loss = lax.psum(loss_part[0, 0], ("x", "y", "c"))

    G = {}
    per_layer = {n: [None] * DEPTH for n in ("w_in", "w_o", "w_up", "w_down", "w_conv", "w_pool", "pool_scale",
                                             "hg_norm_g", "ln1_g", "ln1_b", "w_ffn_conv", "b_ffn_conv", "ln2_g", "ln2_b")}
    dlb_total = jnp.zeros((DEPTH, HG_W), F32)
    for l in reversed(range(DEPTH)):
        s = saved[l]
        lw = s["lw"]
        dz2, dz2_b, dg2, db2 = _ln_bwd(s["x1"], s["ffn"], dy, ln2_g[l][None, :], f"ln2_bwd_{l}")
        da = _matmul(dz2_b, lw["w_down"], "nt", F32, f"bwd_da_{l}")
        d_w_down = _matmul(s["a"], dz2_b, "tn", F32, f"wgrad_down_{l}")
        dup, dwf, dbf = _ffn_act_bwd(s["up"], s["u"], da, s["wf"], f"ffn_bwd_{l}")
        dx1 = _matmul(dup, lw["w_up"], "nt", F32, f"bwd_dx1_{l}", res=dz2, alpha=ALPHA)
        d_w_up = _matmul(s["x1_b"], dup, "tn", F32, f"wgrad_up_{l}")
        dz1, dz1_b, dg1, db1 = _ln_bwd(s["x"], s["mix"], dx1, ln1_g[l][None, :], f"ln1_bwd_{l}")
        dcat = _matmul(dz1_b, lw["w_o"], "nt", F32, f"bwd_dcat_{l}")
        d_w_o = _matmul(s["cat"], dz1_b, "tn", F32, f"wgrad_o_{l}")
        dhh, dlb, dgn = _hgrn_bwd(s["hh"], s["o_raw"], s["states"], dcat, hg_lower_bounds, s["gn"], l, f"hgrn_bwd_{l}")
        dhc, dwc, dwblk, dps = _cp_bwd(s["hc"], dcat, s["wc"], s["wblk"], s["ps"], f"convpool_bwd_{l}")
        dx_a = _matmul(dhh, lw["w_hg"], "nt", F32, f"bwd_dx_hg_{l}", res=dz1, alpha=ALPHA)
        dx = _matmul(dhc, lw["w_cp"], "nt", F32, f"bwd_dx_cp_{l}", res=dx_a, alpha=1.0)
        d_w_hg = _matmul(s["x_b"], dhh, "tn", F32, f"wgrad_hg_{l}")
        d_w_cp = _matmul(s["x_b"], dhc, "tn", F32, f"wgrad_cp_{l}")
        per_layer["w_in"][l] = jnp.concatenate([d_w_cp[:, 0:768], d_w_hg, d_w_cp[:, 768:1024]], axis=1)
        per_layer["w_o"][l] = jnp.concatenate([d_w_o[512:768], d_w_o[0:512], d_w_o[768:1024]], axis=0)
        per_layer["w_up"][l] = d_w_up
        per_layer["w_down"][l] = d_w_down
        per_layer["w_conv"][l] = dwc.T
        per_layer["w_ffn_conv"][l] = dwf.T
        per_layer["b_ffn_conv"][l] = dbf[0]
        per_layer["w_pool"][l] = jnp.stack([dwblk[g * 64:(g + 1) * 64, g * 64:(g + 1) * 64] for g in range(4)], axis=0)
        per_layer["pool_scale"][l] = dps[0]
        per_layer["hg_norm_g"][l] = dgn[0]
        per_layer["ln1_g"][l], per_layer["ln1_b"][l] = dg1[0], db1[0]
        per_layer["ln2_g"][l], per_layer["ln2_b"][l] = dg2[0], db2[0]
        dlb_total = dlb_total + dlb
        dy = dx
    for n, parts in per_layer.items():
        G[n] = jnp.stack(parts, axis=0)
    G["hg_lower_bounds"] = dlb_total
    G["meta_tokens"] = dy[0:N_META]
    grad_x = dy[N_META:N_META + seq][None]

    sharded = BIG + SMALL_SHARDED
    axis = {**BIG_AXIS, **SMALL_AXIS}
    chunk_pack, sh_offs = None, None
    packs = []
    for j in range(N_DEV):
        pj, sh_offs = _pack([jnp.split(G[n], N_DEV, axis=axis[n])[j] for n in sharded], F32)
        packs.append(pj)
    chunk_pack = jnp.stack(packs, axis=0)
    recv = _all_to_all(chunk_pack, "scatter_grads")
    w_pack, _ = _pack([W[n] for n in sharded], F32)
    m_pack, _ = _pack([M[n] for n in sharded], F32)
    v_pack, _ = _pack([V[n] for n in sharded], F32)
    outs_sh = _adamw(recv, w_pack, m_pack, v_pack, "adamw_sharded")

    rep_pack, rep_offs = _pack([G[n] for n in REPLICATED], F32)
    rep_all = _all_gather(rep_pack, "gather_small_grads")
    w_rep, _ = _pack([W[n] for n in REPLICATED], F32)
    m_rep, _ = _pack([M[n] for n in REPLICATED], F32)
    v_rep, _ = _pack([V[n] for n in REPLICATED], F32)
    outs_rep = _adamw(rep_all, w_rep, m_rep, v_rep, "adamw_replicated")

    res = {k: {} for k in ("grad", "delta", "new_m", "new_v")}
    for kind, b_sh, b_rep in zip(("grad", "delta", "new_m", "new_v"), outs_sh, outs_rep):
        for n, a in zip(sharded, _unpack(b_sh, sh_offs, [W[n].shape for n in sharded])):
            res[kind][n] = a
        for n, a in zip(REPLICATED, _unpack(b_rep, rep_offs, [W[n].shape for n in REPLICATED])):
            res[kind][n] = a

    return (loss, grad_x, *[res["grad"][n] for n in WEIGHTS], *[res["delta"][n] for n in WEIGHTS],
            *[res["new_m"][n] for n in WEIGHTS], *[res["new_v"][n] for n in WEIGHTS])
```

```python
import jax
import jax.numpy as jnp
from jax import lax
from jax.experimental import pallas as pl
from jax.experimental.pallas import tpu as pltpu

F32 = jnp.float32
BF16 = jnp.bfloat16

N_DEV = 8
D_MODEL = 1024
N_META = 16
DEPTH = 2
CONV_W = 256
HG_W = 512
HG_D = 128
HG_HEADS = 4
POOL_W = 256
POOL_GROUP = 64
D_FF = 2816
ALPHA = (2 * DEPTH) ** 0.25
LN_EPS = 1e-5
RMS_EPS = 1e-6
F_FLOOR = 1e-30
Q_SCALE = HG_D ** -0.5
SUB = 16
SEQ_TILE = 192
FFN_TILE = 96
ROW_ALIGN = 192
LANES = 128
PACK_ROWS = 512
VMEM_LIMIT = 48 * 1024 * 1024

ADAM_LR = 0.001
ADAM_B1 = 0.9
ADAM_B2 = 0.999
ADAM_EPS = 1e-08
ADAM_WD = 0.01
ADAM_STEP = 10


def _tile(n, cap, mult):
    best = 0
    for t in range(mult, min(n, cap) + 1, mult):
        if n % t == 0:
            best = t
    assert best > 0, (n, cap, mult)
    return best


def _params(sem, vmem=VMEM_LIMIT):
    return pltpu.CompilerParams(dimension_semantics=sem, vmem_limit_bytes=vmem)


def _dnt(a, b):
    return lax.dot_general(a, b, (((1,), (1,)), ((), ())), preferred_element_type=F32)


def _dtn(a, b):
    return lax.dot_general(a, b, (((0,), (0,)), ((), ())), preferred_element_type=F32)


def _dnn(a, b):
    return jnp.dot(a, b, preferred_element_type=F32)


def _sigmoid(x):
    return 1.0 / (1.0 + jnp.exp(-x))


def _matmul(a, b, mode, out_dtype, name, res=None, alpha=1.0):
    if mode == "tn":
        K, M = a.shape
    else:
        M, K = a.shape
    N = b.shape[0] if mode == "nt" else b.shape[1]
    tm = _tile(M, 1536, 16)
    tn = _tile(N, 1536, LANES)
    tk = _tile(K, 1536, 16) if mode == "tn" else _tile(K, 1536, LANES)
    if mode == "tn":
        tk = _tile(K, 768, 16)
    nk = K // tk
    dims = {"nn": ((1,), (0,)), "nt": ((1,), (1,)), "tn": ((0,), (0,))}[mode]

    def body(*refs):
        if res is None:
            a_ref, b_ref, o_ref, acc = refs
            r_ref = None
        else:
            a_ref, b_ref, r_ref, o_ref, acc = refs
        k = pl.program_id(2)
        p = lax.dot_general(a_ref[...].astype(BF16), b_ref[...].astype(BF16), (dims, ((), ())),
                            preferred_element_type=F32)

        @pl.when(k == 0)
        def _():
            acc[...] = p

        @pl.when(k > 0)
        def _():
            acc[...] += p

        @pl.when(k == nk - 1)
        def _():
            r = acc[...]
            if r_ref is not None:
                r = r + alpha * r_ref[...]
            o_ref[...] = r.astype(out_dtype)

    if mode == "tn":
        a_spec = pl.BlockSpec((tk, tm), lambda i, j, k: (k, i))
    else:
        a_spec = pl.BlockSpec((tm, tk), lambda i, j, k: (i, k))
    if mode == "nt":
        b_spec = pl.BlockSpec((tn, tk), lambda i, j, k: (j, k))
    else:
        b_spec = pl.BlockSpec((tk, tn), lambda i, j, k: (k, j))
    in_specs = [a_spec, b_spec]
    args = [a, b]
    if res is not None:
        in_specs.append(pl.BlockSpec((tm, tn), lambda i, j, k: (i, j)))
        args.append(res)
    return pl.pallas_call(
        body, name=name,
        grid=(M // tm, N // tn, nk),
        in_specs=in_specs,
        out_specs=pl.BlockSpec((tm, tn), lambda i, j, k: (i, j)),
        out_shape=jax.ShapeDtypeStruct((M, N), out_dtype),
        scratch_shapes=[pltpu.VMEM((tm, tn), F32)],
        compiler_params=_params(("parallel", "parallel", "arbitrary")),
    )(*args)


def _ln_fwd(x, f, g, b, name):
    L, D = x.shape
    tr = _tile(L, 768, 16)

    def body(x_ref, f_ref, g_ref, b_ref, y_ref, yb_ref):
        z = ALPHA * x_ref[...] + f_ref[...]
        mu = jnp.mean(z, axis=-1, keepdims=True)
        zc = z - mu
        var = jnp.mean(zc * zc, axis=-1, keepdims=True)
        y = zc * lax.rsqrt(var + LN_EPS) * g_ref[...] + b_ref[...]
        y_ref[...] = y
        yb_ref[...] = y.astype(BF16)

    row = pl.BlockSpec((tr, D), lambda i: (i, 0))
    vec = pl.BlockSpec((1, D), lambda i: (0, 0))
    return pl.pallas_call(
        body, name=name, grid=(L // tr,),
        in_specs=[row, row, vec, vec], out_specs=[row, row],
        out_shape=[jax.ShapeDtypeStruct((L, D), F32), jax.ShapeDtypeStruct((L, D), BF16)],
        compiler_params=_params(("parallel",)),
    )(x, f, g, b)


def _ln_bwd(x, f, dy, g, name):
    L, D = x.shape
    tr = _tile(L, 768, 16)

    def body(x_ref, f_ref, dy_ref, g_ref, dz_ref, dzb_ref, dg_ref, db_ref):
        @pl.when(pl.program_id(0) == 0)
        def _():
            dg_ref[...] = jnp.zeros_like(dg_ref)
            db_ref[...] = jnp.zeros_like(db_ref)

        z = ALPHA * x_ref[...] + f_ref[...]
        mu = jnp.mean(z, axis=-1, keepdims=True)
        zc = z - mu
        var = jnp.mean(zc * zc, axis=-1, keepdims=True)
        rstd = lax.rsqrt(var + LN_EPS)
        xhat = zc * rstd
        dy = dy_ref[...]
        dxh = dy * g_ref[...]
        m1 = jnp.mean(dxh, axis=-1, keepdims=True)
        m2 = jnp.mean(dxh * xhat, axis=-1, keepdims=True)
        dz = rstd * (dxh - m1 - xhat * m2)
        dz_ref[...] = dz
        dzb_ref[...] = dz.astype(BF16)
        dg_ref[...] += jnp.sum(dy * xhat, axis=0, keepdims=True)
        db_ref[...] += jnp.sum(dy, axis=0, keepdims=True)

    row = pl.BlockSpec((tr, D), lambda i: (i, 0))
    vec = pl.BlockSpec((1, D), lambda i: (0, 0))
    return pl.pallas_call(
        body, name=name, grid=(L // tr,),
        in_specs=[row, row, row, vec], out_specs=[row, row, vec, vec],
        out_shape=[jax.ShapeDtypeStruct((L, D), F32), jax.ShapeDtypeStruct((L, D), BF16),
                   jax.ShapeDtypeStruct((1, D), F32), jax.ShapeDtypeStruct((1, D), F32)],
        compiler_params=_params(("arbitrary",)),
    )(x, f, dy, g)


def _loss_head(y, tgt, seq):
    L, D = y.shape
    tr = _tile(L, 768, 16)

    def body(y_ref, t_ref, dy_ref, loss_ref):
        i = pl.program_id(0)

        @pl.when(i == 0)
        def _():
            loss_ref[...] = jnp.zeros_like(loss_ref)

        r = i * tr + lax.broadcasted_iota(jnp.int32, (tr, D), 0)
        valid = (r >= N_META) & (r < N_META + seq)
        e = jnp.where(valid, y_ref[...] - t_ref[...], 0.0)
        dy_ref[...] = e * (1.0 / D)
        s = jnp.sum(jnp.sum(e * e, axis=-1, keepdims=True), axis=0, keepdims=True)
        loss_ref[...] += (0.5 / D) * s

    row = pl.BlockSpec((tr, D), lambda i: (i, 0))
    return pl.pallas_call(
        body, name="loss_head", grid=(L // tr,),
        in_specs=[row, row], out_specs=[row, pl.BlockSpec((1, 1), lambda i: (0, 0))],
        out_shape=[jax.ShapeDtypeStruct((L, D), F32), jax.ShapeDtypeStruct((1, 1), F32)],
        compiler_params=_params(("arbitrary",)),
    )(y, tgt)


def _shift_down(x, prev, k):
    out = pltpu.roll(x, k, 0)
    row = lax.broadcasted_iota(jnp.int32, (8, x.shape[1]), 0)
    top = out[0:8]
    for r in range(k):
        top = jnp.where(row == r, prev[8 - k + r:8 - k + r + 1, :], top)
    return top if x.shape[0] == 8 else jnp.concatenate([top, out[8:]], axis=0)


def _shift_up(x, nxt, k):
    T = x.shape[0]
    out = pltpu.roll(x, T - k, 0)
    row = lax.broadcasted_iota(jnp.int32, (8, x.shape[1]), 0)
    bot = out[T - 8:T]
    for r in range(k):
        bot = jnp.where(row == 8 - k + r, nxt[r:r + 1, :], bot)
    return bot if T == 8 else jnp.concatenate([out[:T - 8], bot], axis=0)


def _conv3(x, prev, w, b):
    return w[2:3, :] * x + w[1:2, :] * _shift_down(x, prev, 1) + w[0:1, :] * _shift_down(x, prev, 2) + b


def _ffn_act_fwd(up, w, b, name):
    L, C = up.shape
    F = C // 2
    ts = FFN_TILE
    n = L // ts

    def body(up_ref, pv_ref, w_ref, b_ref, a_ref):
        i = pl.program_id(0)
        x = up_ref[...].astype(F32)
        prev = jnp.where(i > 0, pv_ref[...].astype(F32)[8:16], 0.0)
        u = _conv3(x, prev, w_ref[...], b_ref[...])
        gate = u[:, :F]
        a_ref[...] = (gate * _sigmoid(gate) * u[:, F:]).astype(BF16)

    return pl.pallas_call(
        body, name=name, grid=(n,),
        in_specs=[pl.BlockSpec((ts, C), lambda i: (i, 0)),
                  pl.BlockSpec((16, C), lambda i: (jnp.maximum(i * (ts // 16) - 1, 0), 0)),
                  pl.BlockSpec((3, C), lambda i: (0, 0)), pl.BlockSpec((1, C), lambda i: (0, 0))],
        out_specs=pl.BlockSpec((ts, F), lambda i: (i, 0)),
        out_shape=jax.ShapeDtypeStruct((L, F), BF16),
        compiler_params=_params(("parallel",)),
    )(up, up, w, b)


def _ffn_act_bwd(up, da, w, b, name):
    L, C = up.shape
    F = C // 2
    ts = FFN_TILE
    n = L // ts
    last16 = L // 16 - 1

    def du_of(u, da):
        gate, val = u[:, :F], u[:, F:]
        sg = _sigmoid(gate)
        dgate = da * val * (sg * (1.0 + gate * (1.0 - sg)))
        dval = da * (gate * sg)
        return jnp.concatenate([dgate, dval], axis=1)

    def body(up_ref, pv_ref, nx_ref, da_ref, dan_ref, w_ref, b_ref, dup_ref, dw_ref, db_ref):
        i = pl.program_id(0)

        @pl.when(i == 0)
        def _():
            dw_ref[...] = jnp.zeros_like(dw_ref)
            db_ref[...] = jnp.zeros_like(db_ref)

        w = w_ref[...]
        bias = b_ref[...]
        x = up_ref[...].astype(F32)
        prev = jnp.where(i > 0, pv_ref[...].astype(F32)[8:16], 0.0)
        u = _conv3(x, prev, w, bias)
        u_next = _conv3(nx_ref[...].astype(F32)[0:8], x[ts - 8:ts], w, bias)
        du = du_of(u, da_ref[...].astype(F32))
        dun = jnp.where(i < n - 1, du_of(u_next, dan_ref[...].astype(F32)[0:8]), 0.0)
        du1 = _shift_up(du, dun, 1)
        du2 = _shift_up(du, dun, 2)
        dup_ref[...] = (w[2:3, :] * du + w[1:2, :] * du1 + w[0:1, :] * du2).astype(BF16)
        dw_ref[...] += jnp.concatenate([jnp.sum(x * du2, axis=0, keepdims=True),
                                        jnp.sum(x * du1, axis=0, keepdims=True),
                                        jnp.sum(x * du, axis=0, keepdims=True)], axis=0)
        db_ref[...] += jnp.sum(du, axis=0, keepdims=True)

    prv = lambda i: (jnp.maximum(i * (ts // 16) - 1, 0), 0)
    nxt = lambda i: (jnp.minimum((i + 1) * (ts // 16), last16), 0)
    return pl.pallas_call(
        body, name=name, grid=(n,),
        in_specs=[pl.BlockSpec((ts, C), lambda i: (i, 0)), pl.BlockSpec((16, C), prv), pl.BlockSpec((16, C), nxt),
                  pl.BlockSpec((ts, F), lambda i: (i, 0)), pl.BlockSpec((16, F), nxt),
                  pl.BlockSpec((3, C), lambda i: (0, 0)), pl.BlockSpec((1, C), lambda i: (0, 0))],
        out_specs=[pl.BlockSpec((ts, C), lambda i: (i, 0)), pl.BlockSpec((3, C), lambda i: (0, 0)),
                   pl.BlockSpec((1, C), lambda i: (0, 0))],
        out_shape=[jax.ShapeDtypeStruct((L, C), BF16), jax.ShapeDtypeStruct((3, C), F32),
                   jax.ShapeDtypeStruct((1, C), F32)],
        compiler_params=_params(("arbitrary",)),
    )(up, up, up, da, da, w, b)


def _pool_window(ext, tile_rows, first_row, lead):
    T = ext.shape[0]
    sh = (lambda x, k: pltpu.roll(x, T - k, 0)) if lead else (lambda x, k: pltpu.roll(x, k, 0))
    r2 = ext + sh(ext, 1)
    r4 = r2 + sh(r2, 2)
    r8 = r4 + sh(r4, 4)
    r16 = r8 + sh(r8, 8)
    lo = 0 if lead else 16
    grp = lax.broadcasted_iota(jnp.int32, (tile_rows, POOL_W), 1) // POOL_GROUP
    pick = lambda a, b, c, d: jnp.where(grp == 0, a, jnp.where(grp == 1, b, jnp.where(grp == 2, c, d)))
    win = pick(r2[lo:lo + tile_rows], r4[lo:lo + tile_rows], r8[lo:lo + tile_rows], r16[lo:lo + tile_rows])
    return win, pick(2.0, 4.0, 8.0, 16.0)


def _pool_count(first_row, rows, wlen):
    t1 = (first_row + lax.broadcasted_iota(jnp.int32, (rows, POOL_W), 0) + 1).astype(F32)
    return jnp.minimum(t1, wlen)


def _cp_fwd(hc, wc, wblk, pscale, name):
    L = hc.shape[0]
    ts = SEQ_TILE
    n = L // ts

    def body(h_ref, hp_ref, wc_ref, wb_ref, ps_ref, y_ref):
        i = pl.program_id(0)
        h = h_ref[...]
        hp = jnp.where(i > 0, hp_ref[...], 0.0)
        cb, cc, cv, pv = h[:, 0:256], h[:, 256:512], h[:, 512:768], h[:, 768:1024]
        p = cc * cv
        pp = hp[8:16, 256:512] * hp[8:16, 512:768]
        w = wc_ref[...]
        conv = w[2:3, :] * p + w[1:2, :] * _shift_down(p, pp, 1) + w[0:1, :] * _shift_down(p, pp, 2)
        y_conv = cb * conv
        ext = jnp.concatenate([hp[:, 768:1024], pv], axis=0)
        win, wlen = _pool_window(ext, ts, i * ts, False)
        d = win / _pool_count(i * ts, ts, wlen) - pv
        y_pool = _dnn(d.astype(BF16), wb_ref[...]) * ps_ref[...]
        y_ref[...] = jnp.concatenate([y_conv, y_pool], axis=1).astype(BF16)

    return pl.pallas_call(
        body, name=name, grid=(n,),
        in_specs=[pl.BlockSpec((ts, 1024), lambda i: (i, 0)),
                  pl.BlockSpec((16, 1024), lambda i: (jnp.maximum(i * (ts // 16) - 1, 0), 0)),
                  pl.BlockSpec((3, 256), lambda i: (0, 0)), pl.BlockSpec((256, 256), lambda i: (0, 0)),
                  pl.BlockSpec((1, 256), lambda i: (0, 0))],
        out_specs=pl.BlockSpec((ts, 512), lambda i: (i, 0)),
        out_shape=jax.ShapeDtypeStruct((L, 512), BF16),
        compiler_params=_params(("parallel",)),
    )(hc, hc, wc, wblk, pscale)


def _cp_bwd(hc, dcat, wc, wblk, pscale, name):
    L = hc.shape[0]
    ts = SEQ_TILE
    n = L // ts
    last16 = L // 16 - 1

    def body(h_ref, hp_ref, hn_ref, dy_ref, dyn_ref, wc_ref, wb_ref, ps_ref,
             dh_ref, dwc_ref, dwb_ref, dps_ref):
        i = pl.program_id(0)

        @pl.when(i == 0)
        def _():
            dwc_ref[...] = jnp.zeros_like(dwc_ref)
            dwb_ref[...] = jnp.zeros_like(dwb_ref)
            dps_ref[...] = jnp.zeros_like(dps_ref)

        h = h_ref[...]
        hp = jnp.where(i > 0, hp_ref[...], 0.0)
        hn = hn_ref[...]
        dy = dy_ref[...]
        dyn = jnp.where(i < n - 1, dyn_ref[...], 0.0)
        cb, cc, cv, pv = h[:, 0:256], h[:, 256:512], h[:, 512:768], h[:, 768:1024]
        w = wc_ref[...]
        p = cc * cv
        pp = hp[8:16, 256:512] * hp[8:16, 512:768]
        p1 = _shift_down(p, pp, 1)
        p2 = _shift_down(p, pp, 2)
        conv = w[2:3, :] * p + w[1:2, :] * p1 + w[0:1, :] * p2
        dyc = dy[:, 0:256]
        dcb = dyc * conv
        dconv = dyc * cb
        dconv_n = dyn[0:8, 0:256] * hn[0:8, 0:256]
        dc1 = _shift_up(dconv, dconv_n, 1)
        dc2 = _shift_up(dconv, dconv_n, 2)
        dp = w[2:3, :] * dconv + w[1:2, :] * dc1 + w[0:1, :] * dc2
        dwc_ref[...] += jnp.concatenate([jnp.sum(p * dc2, axis=0, keepdims=True),
                                         jnp.sum(p * dc1, axis=0, keepdims=True),
                                         jnp.sum(p * dconv, axis=0, keepdims=True)], axis=0)
        ps = ps_ref[...]
        wb = wb_ref[...]
        ext = jnp.concatenate([hp[:, 768:1024], pv], axis=0)
        win, wlen = _pool_window(ext, ts, i * ts, False)
        d = win / _pool_count(i * ts, ts, wlen) - pv
        db = d.astype(BF16)
        dyp = dy[:, 256:512]
        dps_ref[...] += jnp.sum(dyp * _dnn(db, wb), axis=0, keepdims=True)
        dypre = (dyp * ps).astype(BF16)
        dwb_ref[...] += _dtn(db, dypre)
        dd = _dnt(dypre, wb)
        ddn = _dnt((dyn[:, 256:512] * ps).astype(BF16), wb)
        e = dd / _pool_count(i * ts, ts, wlen)
        en = ddn / _pool_count((i + 1) * ts, 16, wlen[0:16])
        lead, _ = _pool_window(jnp.concatenate([e, en], axis=0), ts, i * ts, True)
        dpv = lead - dd
        dh_ref[...] = jnp.concatenate([dcb, dp * cv, dp * cc, dpv], axis=1).astype(BF16)

    return pl.pallas_call(
        body, name=name, grid=(n,),
        in_specs=[pl.BlockSpec((ts, 1024), lambda i: (i, 0)),
                  pl.BlockSpec((16, 1024), lambda i: (jnp.maximum(i * (ts // 16) - 1, 0), 0)),
                  pl.BlockSpec((16, 1024), lambda i: (jnp.minimum((i + 1) * (ts // 16), last16), 0)),
                  pl.BlockSpec((ts, 512), lambda i: (i, 1)),
                  pl.BlockSpec((16, 512), lambda i: (jnp.minimum((i + 1) * (ts // 16), last16), 1)),
                  pl.BlockSpec((3, 256), lambda i: (0, 0)), pl.BlockSpec((256, 256), lambda i: (0, 0)),
                  pl.BlockSpec((1, 256), lambda i: (0, 0))],
        out_specs=[pl.BlockSpec((ts, 1024), lambda i: (i, 0)), pl.BlockSpec((3, 256), lambda i: (0, 0)),
                   pl.BlockSpec((256, 256), lambda i: (0, 0)), pl.BlockSpec((1, 256), lambda i: (0, 0))],
        out_shape=[jax.ShapeDtypeStruct((L, 1024), BF16), jax.ShapeDtypeStruct((3, 256), F32),
                   jax.ShapeDtypeStruct((256, 256), F32), jax.ShapeDtypeStruct((1, 256), F32)],
        compiler_params=_params(("arbitrary",)),
    )(hc, hc, hc, dcat, dcat, wc, wblk, pscale)


def _lower_bound(lb_ref, layer):
    b0, b1 = lb_ref[0:1, :], lb_ref[1:2, :]
    m = jnp.maximum(b0, b1)
    e0, e1 = jnp.exp(b0 - m), jnp.exp(b1 - m)
    p0, p1 = e0 / (e0 + e1), e1 / (e0 + e1)
    lb = (p0 - p0) if layer == 0 else ((p0 + p1) - p0)
    return lb, p0, p1


def _cumsum_rows(x, reverse=False):
    row = lax.broadcasted_iota(jnp.int32, x.shape, 0)
    for sh in (1, 2, 4, 8):
        if reverse:
            x = x + jnp.where(row < SUB - sh, pltpu.roll(x, SUB - sh, 0), 0.0)
        else:
            x = x + jnp.where(row >= sh, pltpu.roll(x, sh, 0), 0.0)
    return x


def _gates(fz, lb):
    sig = _sigmoid(fz)
    f = lb + (1.0 - lb) * sig
    g = jnp.log(jnp.maximum(f, F_FLOOR))
    k = (1.0 - lb) * (1.0 - sig)
    return sig, f, g, k


def _head(h):
    return slice(h * HG_D, (h + 1) * HG_D)


def _hgrn_fwd(hh, lbp, gnorm, layer, name):
    L = hh.shape[0]
    ts = SEQ_TILE
    n = L // ts
    nsub = ts // SUB

    def body(q_ref, f_ref, i_ref, g_ref, lb_ref, gn_ref, y_ref, o_ref, s_ref, St):
        @pl.when(pl.program_id(0) == 0)
        def _():
            St[...] = jnp.zeros_like(St)

        lb, _, _ = _lower_bound(lb_ref, layer)
        gn = jnp.tile(gn_ref[...], (1, HG_HEADS))
        r16 = lax.broadcasted_iota(jnp.int32, (SUB, SUB), 0)
        c16 = lax.broadcasted_iota(jnp.int32, (SUB, SUB), 1)

        def block(j, carry):
            rows = pl.ds(pl.multiple_of(j * SUB, SUB), SUB)
            q = q_ref[rows, :] * Q_SCALE
            iv = i_ref[rows, :]
            gz = g_ref[rows, :]
            _, _, g, k = _gates(f_ref[rows, :], lb)
            G = _cumsum_rows(g)
            Gl = G[SUB - 1:SUB, :]
            qt = (q * jnp.exp(G)).astype(BF16)
            kd = (k * jnp.exp(Gl - G)).astype(BF16)
            eGl = jnp.exp(Gl)
            ib = iv.astype(BF16)
            A = [jnp.zeros((SUB, SUB), F32) for _ in range(HG_HEADS)]
            for s in range(SUB):
                P = q * jnp.exp(jnp.minimum(G - G[s:s + 1, :], 0.0)) * k[s:s + 1, :]
                for h in range(HG_HEADS):
                    A[h] = jnp.where(c16 == s, jnp.sum(P[:, _head(h)], axis=-1, keepdims=True), A[h])
            outs, ons = [], []
            for h in range(HG_HEADS):
                sl = _head(h)
                Sb = St[h].astype(BF16)
                s_ref[j, sl, :] = Sb
                Am = jnp.where(r16 >= c16, A[h], 0.0).astype(BF16)
                o = _dnt(qt[:, sl], Sb) + _dnn(Am, ib[:, sl])
                St[h] = eGl[:, sl] * St[h] + _dtn(ib[:, sl], kd[:, sl])
                outs.append(o)
                ons.append(o * lax.rsqrt(jnp.mean(o * o, axis=-1, keepdims=True) + RMS_EPS))
            o_ref[rows, :] = jnp.concatenate(outs, axis=1)
            y = jnp.concatenate(ons, axis=1) * gn * (gz * _sigmoid(gz))
            y_ref[rows, :] = y.astype(BF16)
            return carry

        lax.fori_loop(0, nsub, block, 0, unroll=2)

    col = lambda c: pl.BlockSpec((ts, HG_W), lambda i: (i, c))
    return pl.pallas_call(
        body, name=name, grid=(n,),
        in_specs=[col(0), col(1), col(2), col(3), pl.BlockSpec((2, HG_W), lambda i: (0, 0)),
                  pl.BlockSpec((1, HG_D), lambda i: (0, 0))],
        out_specs=[pl.BlockSpec((ts, HG_W), lambda i: (i, 0)), pl.BlockSpec((ts, HG_W), lambda i: (i, 0)),
                   pl.BlockSpec((nsub, HG_W, HG_D), lambda i: (i, 0, 0))],
        out_shape=[jax.ShapeDtypeStruct((L, HG_W), BF16), jax.ShapeDtypeStruct((L, HG_W), F32),
                   jax.ShapeDtypeStruct((L // SUB, HG_W, HG_D), BF16)],
        scratch_shapes=[pltpu.VMEM((HG_HEADS, HG_D, HG_D), F32)],
        compiler_params=_params(("arbitrary",)),
    )(hh, hh, hh, hh, lbp, gnorm)


def _hgrn_bwd(hh, o_raw, states, dcat, lbp, gnorm, layer, name):
    L = hh.shape[0]
    ts = SEQ_TILE
    n = L // ts
    nsub = ts // SUB

    def body(q_ref, f_ref, i_ref, g_ref, o_ref, s_ref, dy_ref, lb_ref, gn_ref,
             dh_ref, dlb_ref, dgn_ref, dSt, dlb_acc):
        step = pl.program_id(0)

        @pl.when(step == 0)
        def _():
            dSt[...] = jnp.zeros_like(dSt)
            dlb_acc[...] = jnp.zeros_like(dlb_acc)
            dgn_ref[...] = jnp.zeros_like(dgn_ref)

        lb, p0, p1 = _lower_bound(lb_ref, layer)
        gnh = gn_ref[...]
        gn = jnp.tile(gnh, (1, HG_HEADS))
        r16 = lax.broadcasted_iota(jnp.int32, (SUB, SUB), 0)
        c16 = lax.broadcasted_iota(jnp.int32, (SUB, SUB), 1)

        def block(jj, carry):
            j = nsub - 1 - jj
            rows = pl.ds(pl.multiple_of(j * SUB, SUB), SUB)
            q = q_ref[rows, :] * Q_SCALE
            iv = i_ref[rows, :]
            gz = g_ref[rows, :]
            o = o_ref[rows, :]
            dy = dy_ref[rows, :]
            sig, f, g, k = _gates(f_ref[rows, :], lb)
            G = _cumsum_rows(g)
            Gl = G[SUB - 1:SUB, :]
            eG = jnp.exp(G)
            edl = jnp.exp(Gl - G)
            eGl = jnp.exp(Gl)
            qt = (q * eG).astype(BF16)
            kd = (k * edl).astype(BF16)
            ib = iv.astype(BF16)
            sgz = _sigmoid(gz)
            sil = gz * sgz
            dyn = dy * sil
            on_parts, do_parts = [], []
            dgn = jnp.zeros((1, HG_D), F32)
            for h in range(HG_HEADS):
                sl = _head(h)
                oh = o[:, sl]
                rs = lax.rsqrt(jnp.mean(oh * oh, axis=-1, keepdims=True) + RMS_EPS)
                on = oh * rs
                dgn = dgn + jnp.sum(dyn[:, sl] * on, axis=0, keepdims=True)
                don = dyn[:, sl] * gnh
                do_parts.append(rs * (don - on * jnp.mean(don * on, axis=-1, keepdims=True)))
                on_parts.append(on)
            dgn_ref[...] += dgn
            on_all = jnp.concatenate(on_parts, axis=1)
            dgz = dy * on_all * gn * (sgz * (1.0 + gz * (1.0 - sgz)))
            do = jnp.concatenate(do_parts, axis=1)
            dob = do.astype(BF16)
            Ap = [jnp.where(r16 >= c16, _dnt(dob[:, _head(h)], ib[:, _head(h)]), 0.0) for h in range(HG_HEADS)]
            A = [jnp.zeros((SUB, SUB), F32) for _ in range(HG_HEADS)]
            dq = jnp.zeros((SUB, HG_W), F32)
            dk_rows = []
            for s in range(SUB):
                E = jnp.exp(jnp.minimum(G - G[s:s + 1, :], 0.0))
                P = q * E * k[s:s + 1, :]
                for h in range(HG_HEADS):
                    A[h] = jnp.where(c16 == s, jnp.sum(P[:, _head(h)], axis=-1, keepdims=True), A[h])
                t1 = jnp.concatenate([Ap[h][:, s:s + 1] * E[:, _head(h)] for h in range(HG_HEADS)], axis=1)
                dq = dq + t1 * k[s:s + 1, :]
                dk_rows.append(jnp.sum(t1 * q, axis=0, keepdims=True))
            dk = jnp.concatenate(dk_rows, axis=0)
            dq_p, dk_p, di_p, tail_p = [], [], [], []
            for h in range(HG_HEADS):
                sl = _head(h)
                Sb = s_ref[j, sl, :]
                dSb = dSt[h].astype(BF16)
                Am = jnp.where(r16 >= c16, A[h], 0.0).astype(BF16)
                dq_p.append(eG[:, sl] * _dnn(dob[:, sl], Sb))
                dk_p.append(edl[:, sl] * _dnn(ib[:, sl], dSb))
                di_p.append(_dtn(Am, dob[:, sl]) + _dnt(kd[:, sl], dSb))
                St_end = eGl[:, sl] * Sb.astype(F32) + _dtn(ib[:, sl], kd[:, sl])
                tail_p.append(jnp.sum(dSt[h] * St_end, axis=0, keepdims=True))
                dSt[h] = eGl[:, sl] * dSt[h] + _dtn(dob[:, sl], qt[:, sl])
            dq = dq + jnp.concatenate(dq_p, axis=1)
            dk = dk + jnp.concatenate(dk_p, axis=1)
            di = jnp.concatenate(di_p, axis=1)
            dg = _cumsum_rows(q * dq - k * dk, reverse=True) + jnp.concatenate(tail_p, axis=1)
            df = jnp.where(f > F_FLOOR, dg / f, 0.0)
            dfk = df - dk
            dfz = (1.0 - lb) * dfk * sig * (1.0 - sig)
            dlb_acc[...] += jnp.sum(dfk * (1.0 - sig), axis=0, keepdims=True)
            dh_ref[rows, :] = jnp.concatenate([dq * Q_SCALE, dfz, di, dgz], axis=1).astype(BF16)
            return carry

        lax.fori_loop(0, nsub, block, 0)

        @pl.when(step == n - 1)
        def _():
            if layer == 0:
                dlb_ref[...] = jnp.zeros_like(dlb_ref)
            else:
                dz1 = p0 * p1 * dlb_acc[...]
                dlb_ref[...] = jnp.concatenate([-dz1, dz1], axis=0)

    rev = lambda i: n - 1 - i
    col = lambda c: pl.BlockSpec((ts, HG_W), lambda i: (rev(i), c))
    return pl.pallas_call(
        body, name=name, grid=(n,),
        in_specs=[col(0), col(1), col(2), col(3), col(0),
                  pl.BlockSpec((nsub, HG_W, HG_D), lambda i: (rev(i), 0, 0)), col(0),
                  pl.BlockSpec((2, HG_W), lambda i: (0, 0)), pl.BlockSpec((1, HG_D), lambda i: (0, 0))],
        out_specs=[pl.BlockSpec((ts, 4 * HG_W), lambda i: (rev(i), 0)),
                   pl.BlockSpec((2, HG_W), lambda i: (0, 0)), pl.BlockSpec((1, HG_D), lambda i: (0, 0))],
        out_shape=[jax.ShapeDtypeStruct((L, 4 * HG_W), BF16), jax.ShapeDtypeStruct((2, HG_W), F32),
                   jax.ShapeDtypeStruct((1, HG_D), F32)],
        scratch_shapes=[pltpu.VMEM((HG_HEADS, HG_D, HG_D), F32), pltpu.VMEM((1, HG_W), F32)],
        compiler_params=_params(("arbitrary",)),
    )(hh, hh, hh, hh, o_raw, states, dcat, lbp, gnorm)


def _adamw(gparts, w, m, v, name):
    R = w.shape[0]
    tr = _tile(R, 1024, 16) if R % 16 == 0 else R
    c1 = 1.0 - ADAM_B1 ** ADAM_STEP
    c2 = 1.0 - ADAM_B2 ** ADAM_STEP

    def body(gp_ref, w_ref, m_ref, v_ref, g_ref, d_ref, mo_ref, vo_ref):
        g = gp_ref[0].astype(F32)
        for k in range(1, N_DEV):
            g = g + gp_ref[k].astype(F32)
        mn = ADAM_B1 * m_ref[...] + (1.0 - ADAM_B1) * g
        vn = ADAM_B2 * v_ref[...] + (1.0 - ADAM_B2) * (g * g)
        m_hat = mn / c1
        v_hat = vn / c2
        g_ref[...] = g
        d_ref[...] = -ADAM_LR * (m_hat / (jnp.sqrt(v_hat) + ADAM_EPS) + ADAM_WD * w_ref[...])
        mo_ref[...] = mn
        vo_ref[...] = vn

    row = pl.BlockSpec((tr, LANES), lambda i: (i, 0))
    shp = jax.ShapeDtypeStruct((R, LANES), F32)
    return pl.pallas_call(
        body, name=name, grid=(R // tr,),
        in_specs=[pl.BlockSpec((N_DEV, tr, LANES), lambda i: (0, i, 0)), row, row, row],
        out_specs=[row, row, row, row], out_shape=[shp, shp, shp, shp],
        compiler_params=_params(("parallel",)),
    )(gparts, w, m, v)


def _flip(coord, bit):
    return 1 - coord if bit else coord


def _all_gather(block, name):
    R, C = block.shape

    def body(x_ref, out_ref, send_sems, recv_sems, local_sem):
        x, y, c = lax.axis_index("x"), lax.axis_index("y"), lax.axis_index("c")
        me, sibling = (x, y, c), (x, y, 1 - c)
        chips = [(1 - x, y), (x, 1 - y), (1 - x, 1 - y)]

        def slot(px, py, pc):
            return out_ref.at[4 * px + 2 * py + pc]

        def copy(k, blk, to, src=None):
            return pltpu.make_async_remote_copy(
                src_ref=slot(*blk) if src is None else src, dst_ref=slot(*blk),
                send_sem=send_sems.at[k], recv_sem=recv_sems.at[k],
                device_id=to, device_id_type=pl.DeviceIdType.MESH)

        mine = pltpu.make_async_copy(x_ref, slot(*me), local_sem)
        mine.start()
        first = [copy(0, me, sibling, src=x_ref)]
        first += [copy(1 + j, me, (*chip, c), src=x_ref) for j, chip in enumerate(chips)]
        for cp in first:
            cp.start()
        passed = [copy(4 + j, (*chip, c), sibling) for j, chip in enumerate(chips)]
        for j, chip in enumerate(chips):
            copy(1 + j, (*chip, c), me).wait_recv()
            passed[j].start()
        copy(0, sibling, me).wait_recv()
        for j, chip in enumerate(chips):
            copy(4 + j, (*chip, 1 - c), me).wait_recv()
        for cp in first + passed:
            cp.wait_send()
        mine.wait()

    return pl.pallas_call(
        body, name=name,
        out_shape=jax.ShapeDtypeStruct((N_DEV, R, C), block.dtype),
        in_specs=[pl.BlockSpec(memory_space=pl.ANY)],
        out_specs=pl.BlockSpec(memory_space=pl.ANY),
        scratch_shapes=[pltpu.SemaphoreType.DMA((7,)), pltpu.SemaphoreType.DMA((7,)), pltpu.SemaphoreType.DMA],
    )(block)


def _all_to_all(chunks, name):
    _, R, C = chunks.shape

    def body(x_ref, out_ref, send_sems, recv_sems, local_sem):
        x, y, c = lax.axis_index("x"), lax.axis_index("y"), lax.axis_index("c")
        me = 4 * x + 2 * y + c

        mine = pltpu.make_async_copy(x_ref.at[me], out_ref.at[me], local_sem)
        mine.start()
        copies = []
        for k in range(1, N_DEV):
            px, py, pc = _flip(x, k & 4), _flip(y, k & 2), _flip(c, k & 1)
            peer = 4 * px + 2 * py + pc
            send = pltpu.make_async_remote_copy(
                src_ref=x_ref.at[peer], dst_ref=out_ref.at[me],
                send_sem=send_sems.at[k - 1], recv_sem=recv_sems.at[k - 1],
                device_id=(px, py, pc), device_id_type=pl.DeviceIdType.MESH)
            recv = pltpu.make_async_remote_copy(
                src_ref=x_ref.at[peer], dst_ref=out_ref.at[peer],
                send_sem=send_sems.at[k - 1], recv_sem=recv_sems.at[k - 1],
                device_id=(px, py, pc), device_id_type=pl.DeviceIdType.MESH)
            send.start()
            copies.append((send, recv))
        for send, recv in copies:
            recv.wait_recv()
        for send, recv in copies:
            send.wait_send()
        mine.wait()

    return pl.pallas_call(
        body, name=name,
        out_shape=jax.ShapeDtypeStruct(chunks.shape, chunks.dtype),
        in_specs=[pl.BlockSpec(memory_space=pl.ANY)],
        out_specs=pl.BlockSpec(memory_space=pl.ANY),
        scratch_shapes=[pltpu.SemaphoreType.DMA((7,)), pltpu.SemaphoreType.DMA((7,)), pltpu.SemaphoreType.DMA],
    )(chunks)


def _pack(arrs, dtype, row_mult=8):
    parts, offs, r = [], [], 0
    for a in arrs:
        flat = a.astype(dtype).reshape(-1)
        nrow = -(-flat.shape[0] // LANES)
        flat = jnp.pad(flat, (0, nrow * LANES - flat.shape[0]))
        parts.append(flat.reshape(nrow, LANES))
        offs.append((r, nrow))
        r += nrow
    pad = (-r) % row_mult
    if pad:
        parts.append(jnp.zeros((pad, LANES), dtype))
    return jnp.concatenate(parts, axis=0), offs


def _unpack(buf, offs, shapes, lead=()):
    outs = []
    for (r, nrow), shp in zip(offs, shapes):
        size = 1
        for s in shp:
            size *= s
        flat = buf[..., r:r + nrow, :].reshape(lead + (nrow * LANES,))
        outs.append(flat[..., :size].reshape(lead + tuple(shp)))
    return outs


def _cols_from_shards(g, axis):
    return jnp.concatenate([g[j] for j in range(N_DEV)], axis=axis)


def _shards_of(a, axis):
    return jnp.stack(jnp.split(a, N_DEV, axis=axis), axis=0)


BIG = ("w_in", "w_o", "w_up", "w_down")
BIG_AXIS = {"w_in": 2, "w_o": 1, "w_up": 2, "w_down": 1}
SMALL_SHARDED = ("meta_tokens", "w_conv", "w_ffn_conv")
SMALL_AXIS = {"meta_tokens": 1, "w_conv": 1, "w_ffn_conv": 1}
REPLICATED = ("hg_lower_bounds", "w_pool", "pool_scale", "hg_norm_g", "ln1_g", "ln1_b", "b_ffn_conv", "ln2_g", "ln2_b")
WEIGHTS = ("meta_tokens", "hg_lower_bounds", "w_in", "w_conv", "w_pool", "pool_scale", "hg_norm_g", "w_o",
           "ln1_g", "ln1_b", "w_up", "w_ffn_conv", "b_ffn_conv", "w_down", "ln2_g", "ln2_b")


def _pool_blockdiag(w_pool_l):
    z = jnp.zeros((POOL_GROUP, POOL_GROUP), w_pool_l.dtype)
    rows = [jnp.concatenate([w_pool_l[g] if h == g else z for h in range(4)], axis=1) for g in range(4)]
    return jnp.concatenate(rows, axis=0)


def _layer_weights(full, l):
    w_in = full["w_in"][l]
    w_o = full["w_o"][l]
    return dict(
        w_hg=w_in[:, 768:2816],
        w_cp=jnp.concatenate([w_in[:, 0:768], w_in[:, 2816:3072]], axis=1),
        w_o=jnp.concatenate([w_o[256:768], w_o[0:256], w_o[768:1024]], axis=0),
        w_up=full["w_up"][l], w_down=full["w_down"][l],
    )


def kernel(x, meta_tokens, hg_lower_bounds, w_in, w_conv, w_pool, pool_scale, hg_norm_g, w_o, ln1_g, ln1_b, w_up, w_ffn_conv, b_ffn_conv, w_down, ln2_g, ln2_b, loss_target, m_meta_tokens, m_hg_lower_bounds, m_w_in, m_w_conv, m_w_pool, m_pool_scale, m_hg_norm_g, m_w_o, m_ln1_g, m_ln1_b, m_w_up, m_w_ffn_conv, m_b_ffn_conv, m_w_down, m_ln2_g, m_ln2_b, v_meta_tokens, v_hg_lower_bounds, v_w_in, v_w_conv, v_w_pool, v_pool_scale, v_hg_norm_g, v_w_o, v_ln1_g, v_ln1_b, v_w_up, v_w_ffn_conv, v_b_ffn_conv, v_w_down, v_ln2_g, v_ln2_b):
    W = dict(meta_tokens=meta_tokens, hg_lower_bounds=hg_lower_bounds, w_in=w_in, w_conv=w_conv, w_pool=w_pool,
             pool_scale=pool_scale, hg_norm_g=hg_norm_g, w_o=w_o, ln1_g=ln1_g, ln1_b=ln1_b, w_up=w_up,
             w_ffn_conv=w_ffn_conv, b_ffn_conv=b_ffn_conv, w_down=w_down, ln2_g=ln2_g, ln2_b=ln2_b)
    M = dict(meta_tokens=m_meta_tokens, hg_lower_bounds=m_hg_lower_bounds, w_in=m_w_in, w_conv=m_w_conv,
             w_pool=m_w_pool, pool_scale=m_pool_scale, hg_norm_g=m_hg_norm_g, w_o=m_w_o, ln1_g=m_ln1_g,
             ln1_b=m_ln1_b, w_up=m_w_up, w_ffn_conv=m_w_ffn_conv, b_ffn_conv=m_b_ffn_conv, w_down=m_w_down,
             ln2_g=m_ln2_g, ln2_b=m_ln2_b)
    V = dict(meta_tokens=v_meta_tokens, hg_lower_bounds=v_hg_lower_bounds, w_in=v_w_in, w_conv=v_w_conv,
             w_pool=v_w_pool, pool_scale=v_pool_scale, hg_norm_g=v_hg_norm_g, w_o=v_w_o, ln1_g=v_ln1_g,
             ln1_b=v_ln1_b, w_up=v_w_up, w_ffn_conv=v_w_ffn_conv, b_ffn_conv=v_b_ffn_conv, w_down=v_w_down,
             ln2_g=v_ln2_g, ln2_b=v_ln2_b)
    assert x.shape[0] == 1 and x.shape[2] == D_MODEL and w_in.shape[0] == DEPTH
    seq = x.shape[1]
    L = -(-(seq + N_META) // ROW_ALIGN) * ROW_ALIGN

    big_pack, big_offs = _pack([W[n] for n in BIG], BF16, 16)
    small_pack, small_offs = _pack([W[n] for n in SMALL_SHARDED], F32)
    big_all = _all_gather(big_pack, "gather_weights")
    small_all = _all_gather(small_pack, "gather_small")
    full = {}
    for n, a in zip(BIG, _unpack(big_all, big_offs, [W[n].shape for n in BIG], (N_DEV,))):
        full[n] = _cols_from_shards(a, BIG_AXIS[n])
    for n, a in zip(SMALL_SHARDED, _unpack(small_all, small_offs, [W[n].shape for n in SMALL_SHARDED], (N_DEV,))):
        full[n] = _cols_from_shards(a, SMALL_AXIS[n])

    pad_rows = L - N_META - seq
    xp = jnp.concatenate([full["meta_tokens"], x[0], jnp.zeros((pad_rows, D_MODEL), F32)], axis=0)
    tgt = jnp.concatenate([jnp.zeros((N_META, D_MODEL), F32), loss_target[0], jnp.zeros((pad_rows, D_MODEL), F32)], axis=0)

    saved = []
    h_in, h_in_b = xp, xp.astype(BF16)
    for l in range(DEPTH):
        lw = _layer_weights(full, l)
        wc = full["w_conv"][l].T
        wblk = _pool_blockdiag(w_pool[l]).astype(BF16)
        ps = pool_scale[l][None, :]
        gn = hg_norm_g[l][None, :]
        wf = full["w_ffn_conv"][l].T
        bf = b_ffn_conv[l][None, :]
        hh = _matmul(h_in_b, lw["w_hg"], "nn", F32, f"fwd_hg_{l}")
        hc = _matmul(h_in_b, lw["w_cp"], "nn", F32, f"fwd_cp_{l}")
        y_hg, o_raw, states = _hgrn_fwd(hh, hg_lower_bounds, gn, l, f"hgrn_fwd_{l}")
        y_cp = _cp_fwd(hc, wc, wblk, ps, f"convpool_fwd_{l}")
        cat = jnp.concatenate([y_hg, y_cp], axis=1)
        mix = _matmul(cat, lw["w_o"], "nn", F32, f"fwd_o_{l}")
        x1, x1_b = _ln_fwd(h_in, mix, ln1_g[l][None, :], ln1_b[l][None, :], f"ln1_fwd_{l}")
        up = _matmul(x1_b, lw["w_up"], "nn", BF16, f"fwd_up_{l}")
        a = _ffn_act_fwd(up, wf, bf, f"ffn_fwd_{l}")
        ffn = _matmul(a, lw["w_down"], "nn", F32, f"fwd_down_{l}")
        x2, x2_b = _ln_fwd(x1, ffn, ln2_g[l][None, :], ln2_b[l][None, :], f"ln2_fwd_{l}")
        saved.append(dict(lw=lw, wc=wc, wblk=wblk, ps=ps, gn=gn, wf=wf, bf=bf, x=h_in, x_b=h_in_b, hh=hh, hc=hc,
                          o_raw=o_raw, states=states, cat=cat, mix=mix, x1=x1, x1_b=x1_b, up=up, a=a, ffn=ffn))
        h_in, h_in_b = x2, x2_b

    dy, loss_part = _loss_head(h_in, tgt, seq)
    loss = lax.psum(loss_part[0, 0], ("x", "y", "c"))

    G = {}
    per_layer = {n: [None] * DEPTH for n in ("w_in", "w_o", "w_up", "w_down", "w_conv", "w_pool", "pool_scale",
                                             "hg_norm_g", "ln1_g", "ln1_b", "w_ffn_conv", "b_ffn_conv", "ln2_g", "ln2_b")}
    dlb_total = jnp.zeros((DEPTH, HG_W), F32)
    for l in reversed(range(DEPTH)):
        s = saved[l]
        lw = s["lw"]
        dz2, dz2_b, dg2, db2 = _ln_bwd(s["x1"], s["ffn"], dy, ln2_g[l][None, :], f"ln2_bwd_{l}")
        da = _matmul(dz2_b, lw["w_down"], "nt", BF16, f"bwd_da_{l}")
        d_w_down = _matmul(s["a"], dz2_b, "tn", F32, f"wgrad_down_{l}")
        dup, dwf, dbf = _ffn_act_bwd(s["up"], da, s["wf"], s["bf"], f"ffn_bwd_{l}")
        dx1 = _matmul(dup, lw["w_up"], "nt", F32, f"bwd_dx1_{l}", res=dz2, alpha=ALPHA)
        d_w_up = _matmul(s["x1_b"], dup, "tn", F32, f"wgrad_up_{l}")
        dz1, dz1_b, dg1, db1 = _ln_bwd(s["x"], s["mix"], dx1, ln1_g[l][None, :], f"ln1_bwd_{l}")
        dcat = _matmul(dz1_b, lw["w_o"], "nt", F32, f"bwd_dcat_{l}")
        d_w_o = _matmul(s["cat"], dz1_b, "tn", F32, f"wgrad_o_{l}")
        dhh, dlb, dgn = _hgrn_bwd(s["hh"], s["o_raw"], s["states"], dcat, hg_lower_bounds, s["gn"], l, f"hgrn_bwd_{l}")
        dhc, dwc, dwblk, dps = _cp_bwd(s["hc"], dcat, s["wc"], s["wblk"], s["ps"], f"convpool_bwd_{l}")
        dx_a = _matmul(dhh, lw["w_hg"], "nt", F32, f"bwd_dx_hg_{l}", res=dz1, alpha=ALPHA)
        dx = _matmul(dhc, lw["w_cp"], "nt", F32, f"bwd_dx_cp_{l}", res=dx_a, alpha=1.0)
        d_w_hg = _matmul(s["x_b"], dhh, "tn", F32, f"wgrad_hg_{l}")
        d_w_cp = _matmul(s["x_b"], dhc, "tn", F32, f"wgrad_cp_{l}")
        per_layer["w_in"][l] = jnp.concatenate([d_w_cp[:, 0:768], d_w_hg, d_w_cp[:, 768:1024]], axis=1)
        per_layer["w_o"][l] = jnp.concatenate([d_w_o[512:768], d_w_o[0:512], d_w_o[768:1024]], axis=0)
        per_layer["w_up"][l] = d_w_up
        per_layer["w_down"][l] = d_w_down
        per_layer["w_conv"][l] = dwc.T
        per_layer["w_ffn_conv"][l] = dwf.T
        per_layer["b_ffn_conv"][l] = dbf[0]
        per_layer["w_pool"][l] = jnp.stack([dwblk[g * 64:(g + 1) * 64, g * 64:(g + 1) * 64] for g in range(4)], axis=0)
        per_layer["pool_scale"][l] = dps[0]
        per_layer["hg_norm_g"][l] = dgn[0]
        per_layer["ln1_g"][l], per_layer["ln1_b"][l] = dg1[0], db1[0]
        per_layer["ln2_g"][l], per_layer["ln2_b"][l] = dg2[0], db2[0]
        dlb_total = dlb_total + dlb
        dy = dx
    for n, parts in per_layer.items():
        G[n] = jnp.stack(parts, axis=0)
    G["hg_lower_bounds"] = dlb_total
    G["meta_tokens"] = dy[0:N_META]
    grad_x = dy[N_META:N_META + seq][None]

    sharded = BIG + SMALL_SHARDED
    axis = {**BIG_AXIS, **SMALL_AXIS}
    chunk_pack, sh_offs = None, None
    packs = []
    for j in range(N_DEV):
        pj, sh_offs = _pack([jnp.split(G[n], N_DEV, axis=axis[n])[j] for n in sharded], BF16, PACK_ROWS)
        packs.append(pj)
    chunk_pack = jnp.stack(packs, axis=0)
    recv = _all_to_all(chunk_pack, "scatter_grads")
    w_pack, _ = _pack([W[n] for n in sharded], F32, PACK_ROWS)
    m_pack, _ = _pack([M[n] for n in sharded], F32, PACK_ROWS)
    v_pack, _ = _pack([V[n] for n in sharded], F32, PACK_ROWS)
    outs_sh = _adamw(recv, w_pack, m_pack, v_pack, "adamw_sharded")

    rep_pack, rep_offs = _pack([G[n] for n in REPLICATED], F32)
    rep_all = _all_gather(rep_pack, "gather_small_grads")
    w_rep, _ = _pack([W[n] for n in REPLICATED], F32)
    m_rep, _ = _pack([M[n] for n in REPLICATED], F32)
    v_rep, _ = _pack([V[n] for n in REPLICATED], F32)
    outs_rep = _adamw(rep_all, w_rep, m_rep, v_rep, "adamw_replicated")

    res = {k: {} for k in ("grad", "delta", "new_m", "new_v")}
    for kind, b_sh, b_rep in zip(("grad", "delta", "new_m", "new_v"), outs_sh, outs_rep):
        for n, a in zip(sharded, _unpack(b_sh, sh_offs, [W[n].shape for n in sharded])):
            res[kind][n] = a
        for n, a in zip(REPLICATED, _unpack(b_rep, rep_offs, [W[n].shape for n in REPLICATED])):
            res[kind][n] = a

    return (loss, grad_x, *[res["grad"][n] for n in WEIGHTS], *[res["delta"][n] for n in WEIGHTS],
            *[res["new_m"][n] for n in WEIGHTS], *[res["new_v"][n] for n in WEIGHTS])
```

```python
import jax
import jax.numpy as jnp
from jax import lax
from jax.experimental import pallas as pl
from jax.experimental.pallas import tpu as pltpu

F32 = jnp.float32
BF16 = jnp.bfloat16

N_DEV = 8
D_MODEL = 1024
N_META = 16
DEPTH = 2
CONV_W = 256
HG_W = 512
HG_D = 128
HG_HEADS = 4
POOL_W = 256
POOL_GROUP = 64
D_FF = 2816
ALPHA = (2 * DEPTH) ** 0.25
LN_EPS = 1e-5
RMS_EPS = 1e-6
F_FLOOR = 1e-30
Q_SCALE = HG_D ** -0.5
SUB = 16
SEQ_TILE = 192
FFN_TILE = 96
ROW_ALIGN = 192
LANES = 128
VMEM_LIMIT = 48 * 1024 * 1024

ADAM_LR = 0.001
ADAM_B1 = 0.9
ADAM_B2 = 0.999
ADAM_EPS = 1e-08
ADAM_WD = 0.01
ADAM_STEP = 10


def _tile(n, cap, mult):
    best = 0
    for t in range(mult, min(n, cap) + 1, mult):
        if n % t == 0:
            best = t
    assert best > 0, (n, cap, mult)
    return best


def _params(sem, vmem=VMEM_LIMIT):
    return pltpu.CompilerParams(dimension_semantics=sem, vmem_limit_bytes=vmem)


def _dnt(a, b):
    return lax.dot_general(a, b, (((1,), (1,)), ((), ())), preferred_element_type=F32)


def _dtn(a, b):
    return lax.dot_general(a, b, (((0,), (0,)), ((), ())), preferred_element_type=F32)


def _dnn(a, b):
    return jnp.dot(a, b, preferred_element_type=F32)


def _sigmoid(x):
    return 1.0 / (1.0 + jnp.exp(-x))


def _matmul(a, b, mode, out_dtype, name, res=None, alpha=1.0):
    if mode == "tn":
        K, M = a.shape
    else:
        M, K = a.shape
    N = b.shape[0] if mode == "nt" else b.shape[1]
    tm = _tile(M, 1536, 16)
    tn = _tile(N, 1536, LANES)
    tk = _tile(K, 768, 16) if mode == "tn" else _tile(K, 1536, LANES)
    nk = K // tk
    dims = {"nn": ((1,), (0,)), "nt": ((1,), (1,)), "tn": ((0,), (0,))}[mode]

    def body(*refs):
        if res is None:
            a_ref, b_ref, o_ref, acc = refs
            r_ref = None
        else:
            a_ref, b_ref, r_ref, o_ref, acc = refs
        k = pl.program_id(2)
        p = lax.dot_general(a_ref[...].astype(BF16), b_ref[...].astype(BF16), (dims, ((), ())),
                            preferred_element_type=F32)

        @pl.when(k == 0)
        def _():
            acc[...] = p

        @pl.when(k > 0)
        def _():
            acc[...] += p

        @pl.when(k == nk - 1)
        def _():
            r = acc[...]
            if r_ref is not None:
                r = r + alpha * r_ref[...]
            o_ref[...] = r.astype(out_dtype)

    if mode == "tn":
        a_spec = pl.BlockSpec((tk, tm), lambda i, j, k: (k, i))
    else:
        a_spec = pl.BlockSpec((tm, tk), lambda i, j, k: (i, k))
    if mode == "nt":
        b_spec = pl.BlockSpec((tn, tk), lambda i, j, k: (j, k))
    else:
        b_spec = pl.BlockSpec((tk, tn), lambda i, j, k: (k, j))
    in_specs = [a_spec, b_spec]
    args = [a, b]
    if res is not None:
        in_specs.append(pl.BlockSpec((tm, tn), lambda i, j, k: (i, j)))
        args.append(res)
    return pl.pallas_call(
        body, name=name,
        grid=(M // tm, N // tn, nk),
        in_specs=in_specs,
        out_specs=pl.BlockSpec((tm, tn), lambda i, j, k: (i, j)),
        out_shape=jax.ShapeDtypeStruct((M, N), out_dtype),
        scratch_shapes=[pltpu.VMEM((tm, tn), F32)],
        compiler_params=_params(("parallel", "parallel", "arbitrary")),
    )(*args)


def _ln_fwd(x, f, g, b, name):
    L, D = x.shape
    tr = _tile(L, 768, 16)

    def body(x_ref, f_ref, g_ref, b_ref, y_ref, yb_ref):
        z = ALPHA * x_ref[...] + f_ref[...]
        mu = jnp.mean(z, axis=-1, keepdims=True)
        zc = z - mu
        var = jnp.mean(zc * zc, axis=-1, keepdims=True)
        y = zc * lax.rsqrt(var + LN_EPS) * g_ref[...] + b_ref[...]
        y_ref[...] = y
        yb_ref[...] = y.astype(BF16)

    row = pl.BlockSpec((tr, D), lambda i: (i, 0))
    vec = pl.BlockSpec((1, D), lambda i: (0, 0))
    return pl.pallas_call(
        body, name=name, grid=(L // tr,),
        in_specs=[row, row, vec, vec], out_specs=[row, row],
        out_shape=[jax.ShapeDtypeStruct((L, D), F32), jax.ShapeDtypeStruct((L, D), BF16)],
        compiler_params=_params(("parallel",)),
    )(x, f, g, b)


def _ln_bwd(x, f, dy, g, name):
    L, D = x.shape
    tr = _tile(L, 768, 16)

    def body(x_ref, f_ref, dy_ref, g_ref, dz_ref, dzb_ref, dg_ref, db_ref):
        @pl.when(pl.program_id(0) == 0)
        def _():
            dg_ref[...] = jnp.zeros_like(dg_ref)
            db_ref[...] = jnp.zeros_like(db_ref)

        z = ALPHA * x_ref[...] + f_ref[...]
        mu = jnp.mean(z, axis=-1, keepdims=True)
        zc = z - mu
        var = jnp.mean(zc * zc, axis=-1, keepdims=True)
        rstd = lax.rsqrt(var + LN_EPS)
        xhat = zc * rstd
        dy = dy_ref[...]
        dxh = dy * g_ref[...]
        m1 = jnp.mean(dxh, axis=-1, keepdims=True)
        m2 = jnp.mean(dxh * xhat, axis=-1, keepdims=True)
        dz = rstd * (dxh - m1 - xhat * m2)
        dz_ref[...] = dz
        dzb_ref[...] = dz.astype(BF16)
        dg_ref[...] += jnp.sum(dy * xhat, axis=0, keepdims=True)
        db_ref[...] += jnp.sum(dy, axis=0, keepdims=True)

    row = pl.BlockSpec((tr, D), lambda i: (i, 0))
    vec = pl.BlockSpec((1, D), lambda i: (0, 0))
    return pl.pallas_call(
        body, name=name, grid=(L // tr,),
        in_specs=[row, row, row, vec], out_specs=[row, row, vec, vec],
        out_shape=[jax.ShapeDtypeStruct((L, D), F32), jax.ShapeDtypeStruct((L, D), BF16),
                   jax.ShapeDtypeStruct((1, D), F32), jax.ShapeDtypeStruct((1, D), F32)],
        compiler_params=_params(("arbitrary",)),
    )(x, f, dy, g)


def _loss_head(y, tgt, seq):
    L, D = y.shape
    tr = _tile(L, 768, 16)

    def body(y_ref, t_ref, dy_ref, loss_ref):
        i = pl.program_id(0)

        @pl.when(i == 0)
        def _():
            loss_ref[...] = jnp.zeros_like(loss_ref)

        r = i * tr + lax.broadcasted_iota(jnp.int32, (tr, D), 0)
        valid = (r >= N_META) & (r < N_META + seq)
        e = jnp.where(valid, y_ref[...] - t_ref[...], 0.0)
        dy_ref[...] = e * (1.0 / D)
        s = jnp.sum(jnp.sum(e * e, axis=-1, keepdims=True), axis=0, keepdims=True)
        loss_ref[...] += (0.5 / D) * s

    row = pl.BlockSpec((tr, D), lambda i: (i, 0))
    return pl.pallas_call(
        body, name="loss_head", grid=(L // tr,),
        in_specs=[row, row], out_specs=[row, pl.BlockSpec((1, 1), lambda i: (0, 0))],
        out_shape=[jax.ShapeDtypeStruct((L, D), F32), jax.ShapeDtypeStruct((1, 1), F32)],
        compiler_params=_params(("arbitrary",)),
    )(y, tgt)


def _shift_down(x, prev, k):
    out = pltpu.roll(x, k, 0)
    row = lax.broadcasted_iota(jnp.int32, (8, x.shape[1]), 0)
    top = out[0:8]
    for r in range(k):
        top = jnp.where(row == r, prev[8 - k + r:8 - k + r + 1, :], top)
    return top if x.shape[0] == 8 else jnp.concatenate([top, out[8:]], axis=0)


def _shift_up(x, nxt, k):
    T = x.shape[0]
    out = pltpu.roll(x, T - k, 0)
    row = lax.broadcasted_iota(jnp.int32, (8, x.shape[1]), 0)
    bot = out[T - 8:T]
    for r in range(k):
        bot = jnp.where(row == 8 - k + r, nxt[r:r + 1, :], bot)
    return bot if T == 8 else jnp.concatenate([out[:T - 8], bot], axis=0)


def _conv3(x, prev, w, b):
    return w[2:3, :] * x + w[1:2, :] * _shift_down(x, prev, 1) + w[0:1, :] * _shift_down(x, prev, 2) + b


def _ffn_act_fwd(up, w, b, name):
    L, C = up.shape
    F = C // 2
    ts = FFN_TILE
    n = L // ts

    def body(up_ref, pv_ref, w_ref, b_ref, a_ref):
        i = pl.program_id(0)
        x = up_ref[...].astype(F32)
        prev = jnp.where(i > 0, pv_ref[...].astype(F32)[8:16], 0.0)
        u = _conv3(x, prev, w_ref[...], b_ref[...])
        gate = u[:, :F]
        a_ref[...] = (gate * _sigmoid(gate) * u[:, F:]).astype(BF16)

    return pl.pallas_call(
        body, name=name, grid=(n,),
        in_specs=[pl.BlockSpec((ts, C), lambda i: (i, 0)),
                  pl.BlockSpec((16, C), lambda i: (jnp.maximum(i * (ts // 16) - 1, 0), 0)),
                  pl.BlockSpec((3, C), lambda i: (0, 0)), pl.BlockSpec((1, C), lambda i: (0, 0))],
        out_specs=pl.BlockSpec((ts, F), lambda i: (i, 0)),
        out_shape=jax.ShapeDtypeStruct((L, F), BF16),
        compiler_params=_params(("parallel",)),
    )(up, up, w, b)


def _ffn_act_bwd(up, da, w, b, name):
    L, C = up.shape
    F = C // 2
    ts = FFN_TILE
    n = L // ts
    last16 = L // 16 - 1

    def du_of(u, da):
        gate, val = u[:, :F], u[:, F:]
        sg = _sigmoid(gate)
        dgate = da * val * (sg * (1.0 + gate * (1.0 - sg)))
        dval = da * (gate * sg)
        return jnp.concatenate([dgate, dval], axis=1)

    def body(up_ref, pv_ref, nx_ref, da_ref, dan_ref, w_ref, b_ref, dup_ref, dw_ref, db_ref):
        i = pl.program_id(0)

        @pl.when(i == 0)
        def _():
            dw_ref[...] = jnp.zeros_like(dw_ref)
            db_ref[...] = jnp.zeros_like(db_ref)

        w = w_ref[...]
        bias = b_ref[...]
        x = up_ref[...].astype(F32)
        prev = jnp.where(i > 0, pv_ref[...].astype(F32)[8:16], 0.0)
        u = _conv3(x, prev, w, bias)
        u_next = _conv3(nx_ref[...].astype(F32)[0:8], x[ts - 8:ts], w, bias)
        du = du_of(u, da_ref[...].astype(F32))
        dun = jnp.where(i < n - 1, du_of(u_next, dan_ref[...].astype(F32)[0:8]), 0.0)
        du1 = _shift_up(du, dun, 1)
        du2 = _shift_up(du, dun, 2)
        dup_ref[...] = (w[2:3, :] * du + w[1:2, :] * du1 + w[0:1, :] * du2).astype(BF16)
        dw_ref[...] += jnp.concatenate([jnp.sum(x * du2, axis=0, keepdims=True),
                                        jnp.sum(x * du1, axis=0, keepdims=True),
                                        jnp.sum(x * du, axis=0, keepdims=True)], axis=0)
        db_ref[...] += jnp.sum(du, axis=0, keepdims=True)

    prv = lambda i: (jnp.maximum(i * (ts // 16) - 1, 0), 0)
    nxt = lambda i: (jnp.minimum((i + 1) * (ts // 16), last16), 0)
    return pl.pallas_call(
        body, name=name, grid=(n,),
        in_specs=[pl.BlockSpec((ts, C), lambda i: (i, 0)), pl.BlockSpec((16, C), prv), pl.BlockSpec((16, C), nxt),
                  pl.BlockSpec((ts, F), lambda i: (i, 0)), pl.BlockSpec((16, F), nxt),
                  pl.BlockSpec((3, C), lambda i: (0, 0)), pl.BlockSpec((1, C), lambda i: (0, 0))],
        out_specs=[pl.BlockSpec((ts, C), lambda i: (i, 0)), pl.BlockSpec((3, C), lambda i: (0, 0)),
                   pl.BlockSpec((1, C), lambda i: (0, 0))],
        out_shape=[jax.ShapeDtypeStruct((L, C), BF16), jax.ShapeDtypeStruct((3, C), F32),
                   jax.ShapeDtypeStruct((1, C), F32)],
        compiler_params=_params(("arbitrary",)),
    )(up, up, up, da, da, w, b)


def _pool_window(ext, tile_rows, first_row, lead):
    T = ext.shape[0]
    sh = (lambda x, k: pltpu.roll(x, T - k, 0)) if lead else (lambda x, k: pltpu.roll(x, k, 0))
    r2 = ext + sh(ext, 1)
    r4 = r2 + sh(r2, 2)
    r8 = r4 + sh(r4, 4)
    r16 = r8 + sh(r8, 8)
    lo = 0 if lead else 16
    grp = lax.broadcasted_iota(jnp.int32, (tile_rows, POOL_W), 1) // POOL_GROUP
    pick = lambda a, b, c, d: jnp.where(grp == 0, a, jnp.where(grp == 1, b, jnp.where(grp == 2, c, d)))
    win = pick(r2[lo:lo + tile_rows], r4[lo:lo + tile_rows], r8[lo:lo + tile_rows], r16[lo:lo + tile_rows])
    return win, pick(2.0, 4.0, 8.0, 16.0)


def _pool_count(first_row, rows, wlen):
    t1 = (first_row + lax.broadcasted_iota(jnp.int32, (rows, POOL_W), 0) + 1).astype(F32)
    return jnp.minimum(t1, wlen)


def _cp_fwd(hc, wc, wblk, pscale, name):
    L = hc.shape[0]
    ts = SEQ_TILE
    n = L // ts

    def body(h_ref, hp_ref, wc_ref, wb_ref, ps_ref, y_ref):
        i = pl.program_id(0)
        h = h_ref[...]
        hp = jnp.where(i > 0, hp_ref[...], 0.0)
        cb, cc, cv, pv = h[:, 0:256], h[:, 256:512], h[:, 512:768], h[:, 768:1024]
        p = cc * cv
        pp = hp[8:16, 256:512] * hp[8:16, 512:768]
        w = wc_ref[...]
        conv = w[2:3, :] * p + w[1:2, :] * _shift_down(p, pp, 1) + w[0:1, :] * _shift_down(p, pp, 2)
        y_conv = cb * conv
        ext = jnp.concatenate([hp[:, 768:1024], pv], axis=0)
        win, wlen = _pool_window(ext, ts, i * ts, False)
        d = win / _pool_count(i * ts, ts, wlen) - pv
        y_pool = _dnn(d.astype(BF16), wb_ref[...]) * ps_ref[...]
        y_ref[...] = jnp.concatenate([y_conv, y_pool], axis=1).astype(BF16)

    return pl.pallas_call(
        body, name=name, grid=(n,),
        in_specs=[pl.BlockSpec((ts, 1024), lambda i: (i, 0)),
                  pl.BlockSpec((16, 1024), lambda i: (jnp.maximum(i * (ts // 16) - 1, 0), 0)),
                  pl.BlockSpec((3, 256), lambda i: (0, 0)), pl.BlockSpec((256, 256), lambda i: (0, 0)),
                  pl.BlockSpec((1, 256), lambda i: (0, 0))],
        out_specs=pl.BlockSpec((ts, 512), lambda i: (i, 0)),
        out_shape=jax.ShapeDtypeStruct((L, 512), BF16),
        compiler_params=_params(("parallel",)),
    )(hc, hc, wc, wblk, pscale)


def _cp_bwd(hc, dcat, wc, wblk, pscale, name):
    L = hc.shape[0]
    ts = SEQ_TILE
    n = L // ts
    last16 = L // 16 - 1

    def body(h_ref, hp_ref, hn_ref, dy_ref, dyn_ref, wc_ref, wb_ref, ps_ref,
             dh_ref, dwc_ref, dwb_ref, dps_ref):
        i = pl.program_id(0)

        @pl.when(i == 0)
        def _():
            dwc_ref[...] = jnp.zeros_like(dwc_ref)
            dwb_ref[...] = jnp.zeros_like(dwb_ref)
            dps_ref[...] = jnp.zeros_like(dps_ref)

        h = h_ref[...]
        hp = jnp.where(i > 0, hp_ref[...], 0.0)
        hn = hn_ref[...]
        dy = dy_ref[...]
        dyn = jnp.where(i < n - 1, dyn_ref[...], 0.0)
        cb, cc, cv, pv = h[:, 0:256], h[:, 256:512], h[:, 512:768], h[:, 768:1024]
        w = wc_ref[...]
        p = cc * cv
        pp = hp[8:16, 256:512] * hp[8:16, 512:768]
        p1 = _shift_down(p, pp, 1)
        p2 = _shift_down(p, pp, 2)
        conv = w[2:3, :] * p + w[1:2, :] * p1 + w[0:1, :] * p2
        dyc = dy[:, 0:256]
        dcb = dyc * conv
        dconv = dyc * cb
        dconv_n = dyn[0:8, 0:256] * hn[0:8, 0:256]
        dc1 = _shift_up(dconv, dconv_n, 1)
        dc2 = _shift_up(dconv, dconv_n, 2)
        dp = w[2:3, :] * dconv + w[1:2, :] * dc1 + w[0:1, :] * dc2
        dwc_ref[...] += jnp.concatenate([jnp.sum(p * dc2, axis=0, keepdims=True),
                                         jnp.sum(p * dc1, axis=0, keepdims=True),
                                         jnp.sum(p * dconv, axis=0, keepdims=True)], axis=0)
        ps = ps_ref[...]
        wb = wb_ref[...]
        ext = jnp.concatenate([hp[:, 768:1024], pv], axis=0)
        win, wlen = _pool_window(ext, ts, i * ts, False)
        d = win / _pool_count(i * ts, ts, wlen) - pv
        db = d.astype(BF16)
        dyp = dy[:, 256:512]
        dps_ref[...] += jnp.sum(dyp * _dnn(db, wb), axis=0, keepdims=True)
        dypre = (dyp * ps).astype(BF16)
        dwb_ref[...] += _dtn(db, dypre)
        dd = _dnt(dypre, wb)
        ddn = _dnt((dyn[:, 256:512] * ps).astype(BF16), wb)
        e = dd / _pool_count(i * ts, ts, wlen)
        en = ddn / _pool_count((i + 1) * ts, 16, wlen[0:16])
        lead, _ = _pool_window(jnp.concatenate([e, en], axis=0), ts, i * ts, True)
        dpv = lead - dd
        dh_ref[...] = jnp.concatenate([dcb, dp * cv, dp * cc, dpv], axis=1).astype(BF16)

    return pl.pallas_call(
        body, name=name, grid=(n,),
        in_specs=[pl.BlockSpec((ts, 1024), lambda i: (i, 0)),
                  pl.BlockSpec((16, 1024), lambda i: (jnp.maximum(i * (ts // 16) - 1, 0), 0)),
                  pl.BlockSpec((16, 1024), lambda i: (jnp.minimum((i + 1) * (ts // 16), last16), 0)),
                  pl.BlockSpec((ts, 512), lambda i: (i, 1)),
                  pl.BlockSpec((16, 512), lambda i: (jnp.minimum((i + 1) * (ts // 16), last16), 1)),
                  pl.BlockSpec((3, 256), lambda i: (0, 0)), pl.BlockSpec((256, 256), lambda i: (0, 0)),
                  pl.BlockSpec((1, 256), lambda i: (0, 0))],
        out_specs=[pl.BlockSpec((ts, 1024), lambda i: (i, 0)), pl.BlockSpec((3, 256), lambda i: (0, 0)),
                   pl.BlockSpec((256, 256), lambda i: (0, 0)), pl.BlockSpec((1, 256), lambda i: (0, 0))],
        out_shape=[jax.ShapeDtypeStruct((L, 1024), BF16), jax.ShapeDtypeStruct((3, 256), F32),
                   jax.ShapeDtypeStruct((256, 256), F32), jax.ShapeDtypeStruct((1, 256), F32)],
        compiler_params=_params(("arbitrary",)),
    )(hc, hc, hc, dcat, dcat, wc, wblk, pscale)


def _lower_bound(lb_ref, layer):
    b0, b1 = lb_ref[0:1, :], lb_ref[1:2, :]
    m = jnp.maximum(b0, b1)
    e0, e1 = jnp.exp(b0 - m), jnp.exp(b1 - m)
    p0, p1 = e0 / (e0 + e1), e1 / (e0 + e1)
    lb = (p0 - p0) if layer == 0 else ((p0 + p1) - p0)
    return lb, p0, p1


def _cumsum_rows(x, reverse=False):
    row = lax.broadcasted_iota(jnp.int32, x.shape, 0)
    for sh in (1, 2, 4, 8):
        if reverse:
            x = x + jnp.where(row < SUB - sh, pltpu.roll(x, SUB - sh, 0), 0.0)
        else:
            x = x + jnp.where(row >= sh, pltpu.roll(x, sh, 0), 0.0)
    return x


def _gates(fz, lb):
    sig = _sigmoid(fz)
    f = lb + (1.0 - lb) * sig
    g = jnp.log(jnp.maximum(f, F_FLOOR))
    k = (1.0 - lb) * (1.0 - sig)
    return sig, f, g, k


def _head(h):
    return slice(h * HG_D, (h + 1) * HG_D)


def _hgrn_fwd(hh, lbp, gnorm, layer, name):
    L = hh.shape[0]
    ts = SEQ_TILE
    n = L // ts
    nsub = ts // SUB

    def body(q_ref, f_ref, i_ref, g_ref, lb_ref, gn_ref, y_ref, o_ref, s_ref, St):
        @pl.when(pl.program_id(0) == 0)
        def _():
            St[...] = jnp.zeros_like(St)

        lb, _, _ = _lower_bound(lb_ref, layer)
        gn = jnp.tile(gn_ref[...], (1, HG_HEADS))
        r16 = lax.broadcasted_iota(jnp.int32, (SUB, SUB), 0)
        c16 = lax.broadcasted_iota(jnp.int32, (SUB, SUB), 1)

        def block(j, carry):
            rows = pl.ds(pl.multiple_of(j * SUB, SUB), SUB)
            q = q_ref[rows, :] * Q_SCALE
            iv = i_ref[rows, :]
            gz = g_ref[rows, :]
            _, _, g, k = _gates(f_ref[rows, :], lb)
            G = _cumsum_rows(g)
            Gl = G[SUB - 1:SUB, :]
            qt = (q * jnp.exp(G)).astype(BF16)
            kd = (k * jnp.exp(Gl - G)).astype(BF16)
            eGl = jnp.exp(Gl)
            ib = iv.astype(BF16)
            A = [jnp.zeros((SUB, SUB), F32) for _ in range(HG_HEADS)]
            for s in range(SUB):
                P = q * jnp.exp(jnp.minimum(G - G[s:s + 1, :], 0.0)) * k[s:s + 1, :]
                for h in range(HG_HEADS):
                    A[h] = jnp.where(c16 == s, jnp.sum(P[:, _head(h)], axis=-1, keepdims=True), A[h])
            outs, ons = [], []
            for h in range(HG_HEADS):
                sl = _head(h)
                Sb = St[h].astype(BF16)
                s_ref[j, sl, :] = Sb
                Am = jnp.where(r16 >= c16, A[h], 0.0).astype(BF16)
                o = _dnt(qt[:, sl], Sb) + _dnn(Am, ib[:, sl])
                St[h] = eGl[:, sl] * St[h] + _dtn(ib[:, sl], kd[:, sl])
                outs.append(o)
                ons.append(o * lax.rsqrt(jnp.mean(o * o, axis=-1, keepdims=True) + RMS_EPS))
            o_ref[rows, :] = jnp.concatenate(outs, axis=1)
            y = jnp.concatenate(ons, axis=1) * gn * (gz * _sigmoid(gz))
            y_ref[rows, :] = y.astype(BF16)
            return carry

        lax.fori_loop(0, nsub, block, 0, unroll=2)

    col = lambda c: pl.BlockSpec((ts, HG_W), lambda i: (i, c))
    return pl.pallas_call(
        body, name=name, grid=(n,),
        in_specs=[col(0), col(1), col(2), col(3), pl.BlockSpec((2, HG_W), lambda i: (0, 0)),
                  pl.BlockSpec((1, HG_D), lambda i: (0, 0))],
        out_specs=[pl.BlockSpec((ts, HG_W), lambda i: (i, 0)), pl.BlockSpec((ts, HG_W), lambda i: (i, 0)),
                   pl.BlockSpec((nsub, HG_W, HG_D), lambda i: (i, 0, 0))],
        out_shape=[jax.ShapeDtypeStruct((L, HG_W), BF16), jax.ShapeDtypeStruct((L, HG_W), F32),
                   jax.ShapeDtypeStruct((L // SUB, HG_W, HG_D), BF16)],
        scratch_shapes=[pltpu.VMEM((HG_HEADS, HG_D, HG_D), F32)],
        compiler_params=_params(("arbitrary",)),
    )(hh, hh, hh, hh, lbp, gnorm)


def _hgrn_bwd(hh, o_raw, states, dcat, lbp, gnorm, layer, name):
    L = hh.shape[0]
    ts = SEQ_TILE
    n = L // ts
    nsub = ts // SUB

    def body(q_ref, f_ref, i_ref, g_ref, o_ref, s_ref, dy_ref, lb_ref, gn_ref,
             dh_ref, dlb_ref, dgn_ref, dSt, dlb_acc):
        step = pl.program_id(0)

        @pl.when(step == 0)
        def _():
            dSt[...] = jnp.zeros_like(dSt)
            dlb_acc[...] = jnp.zeros_like(dlb_acc)
            dgn_ref[...] = jnp.zeros_like(dgn_ref)

        lb, p0, p1 = _lower_bound(lb_ref, layer)
        gnh = gn_ref[...]
        gn = jnp.tile(gnh, (1, HG_HEADS))
        r16 = lax.broadcasted_iota(jnp.int32, (SUB, SUB), 0)
        c16 = lax.broadcasted_iota(jnp.int32, (SUB, SUB), 1)

        def block(jj, carry):
            j = nsub - 1 - jj
            rows = pl.ds(pl.multiple_of(j * SUB, SUB), SUB)
            q = q_ref[rows, :] * Q_SCALE
            iv = i_ref[rows, :]
            gz = g_ref[rows, :]
            o = o_ref[rows, :]
            dy = dy_ref[rows, :]
            sig, f, g, k = _gates(f_ref[rows, :], lb)
            G = _cumsum_rows(g)
            Gl = G[SUB - 1:SUB, :]
            eG = jnp.exp(G)
            edl = jnp.exp(Gl - G)
            eGl = jnp.exp(Gl)
            qt = (q * eG).astype(BF16)
            kd = (k * edl).astype(BF16)
            ib = iv.astype(BF16)
            sgz = _sigmoid(gz)
            sil = gz * sgz
            dyn = dy * sil
            on_parts, do_parts = [], []
            dgn = jnp.zeros((1, HG_D), F32)
            for h in range(HG_HEADS):
                sl = _head(h)
                oh = o[:, sl]
                rs = lax.rsqrt(jnp.mean(oh * oh, axis=-1, keepdims=True) + RMS_EPS)
                on = oh * rs
                dgn = dgn + jnp.sum(dyn[:, sl] * on, axis=0, keepdims=True)
                don = dyn[:, sl] * gnh
                do_parts.append(rs * (don - on * jnp.mean(don * on, axis=-1, keepdims=True)))
                on_parts.append(on)
            dgn_ref[...] += dgn
            on_all = jnp.concatenate(on_parts, axis=1)
            dgz = dy * on_all * gn * (sgz * (1.0 + gz * (1.0 - sgz)))
            do = jnp.concatenate(do_parts, axis=1)
            dob = do.astype(BF16)
            Ap = [jnp.where(r16 >= c16, _dnt(dob[:, _head(h)], ib[:, _head(h)]), 0.0) for h in range(HG_HEADS)]
            A = [jnp.zeros((SUB, SUB), F32) for _ in range(HG_HEADS)]
            dq = jnp.zeros((SUB, HG_W), F32)
            dk_rows = []
            for s in range(SUB):
                E = jnp.exp(jnp.minimum(G - G[s:s + 1, :], 0.0))
                P = q * E * k[s:s + 1, :]
                for h in range(HG_HEADS):
                    A[h] = jnp.where(c16 == s, jnp.sum(P[:, _head(h)], axis=-1, keepdims=True), A[h])
                t1 = jnp.concatenate([Ap[h][:, s:s + 1] * E[:, _head(h)] for h in range(HG_HEADS)], axis=1)
                dq = dq + t1 * k[s:s + 1, :]
                dk_rows.append(jnp.sum(t1 * q, axis=0, keepdims=True))
            dk = jnp.concatenate(dk_rows, axis=0)
            dq_p, dk_p, di_p, tail_p = [], [], [], []
            for h in range(HG_HEADS):
                sl = _head(h)
                Sb = s_ref[j, sl, :]
                dSb = dSt[h].astype(BF16)
                Am = jnp.where(r16 >= c16, A[h], 0.0).astype(BF16)
                dq_p.append(eG[:, sl] * _dnn(dob[:, sl], Sb))
                dk_p.append(edl[:, sl] * _dnn(ib[:, sl], dSb))
                di_p.append(_dtn(Am, dob[:, sl]) + _dnt(kd[:, sl], dSb))
                St_end = eGl[:, sl] * Sb.astype(F32) + _dtn(ib[:, sl], kd[:, sl])
                tail_p.append(jnp.sum(dSt[h] * St_end, axis=0, keepdims=True))
                dSt[h] = eGl[:, sl] * dSt[h] + _dtn(dob[:, sl], qt[:, sl])
            dq = dq + jnp.concatenate(dq_p, axis=1)
            dk = dk + jnp.concatenate(dk_p, axis=1)
            di = jnp.concatenate(di_p, axis=1)
            dg = _cumsum_rows(q * dq - k * dk, reverse=True) + jnp.concatenate(tail_p, axis=1)
            df = jnp.where(f > F_FLOOR, dg / f, 0.0)
            dfk = df - dk
            dfz = (1.0 - lb) * dfk * sig * (1.0 - sig)
            dlb_acc[...] += jnp.sum(dfk * (1.0 - sig), axis=0, keepdims=True)
            dh_ref[rows, :] = jnp.concatenate([dq * Q_SCALE, dfz, di, dgz], axis=1).astype(BF16)
            return carry

        lax.fori_loop(0, nsub, block, 0)

        @pl.when(step == n - 1)
        def _():
            if layer == 0:
                dlb_ref[...] = jnp.zeros_like(dlb_ref)
            else:
                dz1 = p0 * p1 * dlb_acc[...]
                dlb_ref[...] = jnp.concatenate([-dz1, dz1], axis=0)

    rev = lambda i: n - 1 - i
    col = lambda c: pl.BlockSpec((ts, HG_W), lambda i: (rev(i), c))
    return pl.pallas_call(
        body, name=name, grid=(n,),
        in_specs=[col(0), col(1), col(2), col(3), col(0),
                  pl.BlockSpec((nsub, HG_W, HG_D), lambda i: (rev(i), 0, 0)), col(0),
                  pl.BlockSpec((2, HG_W), lambda i: (0, 0)), pl.BlockSpec((1, HG_D), lambda i: (0, 0))],
        out_specs=[pl.BlockSpec((ts, 4 * HG_W), lambda i: (rev(i), 0)),
                   pl.BlockSpec((2, HG_W), lambda i: (0, 0)), pl.BlockSpec((1, HG_D), lambda i: (0, 0))],
        out_shape=[jax.ShapeDtypeStruct((L, 4 * HG_W), BF16), jax.ShapeDtypeStruct((2, HG_W), F32),
                   jax.ShapeDtypeStruct((1, HG_D), F32)],
        scratch_shapes=[pltpu.VMEM((HG_HEADS, HG_D, HG_D), F32), pltpu.VMEM((1, HG_W), F32)],
        compiler_params=_params(("arbitrary",)),
    )(hh, hh, hh, hh, o_raw, states, dcat, lbp, gnorm)


def _adamw_body(gp_ref, w_ref, m_ref, v_ref, g_ref, d_ref, mo_ref, vo_ref):
    c1 = 1.0 - ADAM_B1 ** ADAM_STEP
    c2 = 1.0 - ADAM_B2 ** ADAM_STEP
    g = gp_ref[0].astype(F32)
    for k in range(1, N_DEV):
        g = g + gp_ref[k].astype(F32)
    mn = ADAM_B1 * m_ref[...] + (1.0 - ADAM_B1) * g
    vn = ADAM_B2 * v_ref[...] + (1.0 - ADAM_B2) * (g * g)
    m_hat = mn / c1
    v_hat = vn / c2
    g_ref[...] = g
    d_ref[...] = -ADAM_LR * (m_hat / (jnp.sqrt(v_hat) + ADAM_EPS) + ADAM_WD * w_ref[...])
    mo_ref[...] = mn
    vo_ref[...] = vn


def _adamw_layers(gparts, w, m, v, name):
    _, depth, R, C = gparts.shape
    tr = _tile(R, 256, 16)

    def body(*refs):
        _adamw_body(*refs)

    blk = pl.BlockSpec((None, tr, C), lambda l, i: (l, i, 0))
    shp = jax.ShapeDtypeStruct((depth, R, C), F32)
    return pl.pallas_call(
        body, name=name, grid=(depth, R // tr),
        in_specs=[pl.BlockSpec((N_DEV, None, tr, C), lambda l, i: (0, l, i, 0)), blk, blk, blk],
        out_specs=[blk, blk, blk, blk], out_shape=[shp, shp, shp, shp],
        compiler_params=_params(("parallel", "parallel")),
    )(gparts, w, m, v)


def _adamw(gparts, w, m, v, name):
    R = w.shape[0]
    tr = _tile(R, 1024, 16) if R % 16 == 0 else R

    def body(*refs):
        _adamw_body(*refs)

    row = pl.BlockSpec((tr, LANES), lambda i: (i, 0))
    shp = jax.ShapeDtypeStruct((R, LANES), F32)
    return pl.pallas_call(
        body, name=name, grid=(R // tr,),
        in_specs=[pl.BlockSpec((N_DEV, tr, LANES), lambda i: (0, i, 0)), row, row, row],
        out_specs=[row, row, row, row], out_shape=[shp, shp, shp, shp],
        compiler_params=_params(("parallel",)),
    )(gparts, w, m, v)


def _flip(coord, bit):
    return 1 - coord if bit else coord


def _gather_many(blocks, name):
    n = len(blocks)

    def body(*refs):
        x_refs, out_refs = refs[:n], refs[n:2 * n]
        send_sems, recv_sems, local_sems = refs[2 * n:]
        x, y, c = lax.axis_index("x"), lax.axis_index("y"), lax.axis_index("c")
        me, sibling = (x, y, c), (x, y, 1 - c)
        chips = [(1 - x, y), (x, 1 - y), (1 - x, 1 - y)]

        def slot(a, px, py, pc):
            return out_refs[a].at[4 * px + 2 * py + pc]

        def copy(a, k, blk, to, src=None):
            return pltpu.make_async_remote_copy(
                src_ref=slot(a, *blk) if src is None else src, dst_ref=slot(a, *blk),
                send_sem=send_sems.at[7 * a + k], recv_sem=recv_sems.at[7 * a + k],
                device_id=to, device_id_type=pl.DeviceIdType.MESH)

        mine = [pltpu.make_async_copy(x_refs[a], slot(a, *me), local_sems.at[a]) for a in range(n)]
        for cp in mine:
            cp.start()
        first = [copy(a, 0, me, sibling, src=x_refs[a]) for a in range(n)]
        for j, chip in enumerate(chips):
            first += [copy(a, 1 + j, me, (*chip, c), src=x_refs[a]) for a in range(n)]
        for cp in first:
            cp.start()
        passed = []
        for j, chip in enumerate(chips):
            for a in range(n):
                copy(a, 1 + j, (*chip, c), me).wait_recv()
                fwd = copy(a, 4 + j, (*chip, c), sibling)
                fwd.start()
                passed.append(fwd)
        for a in range(n):
            copy(a, 0, sibling, me).wait_recv()
        for j, chip in enumerate(chips):
            for a in range(n):
                copy(a, 4 + j, (*chip, 1 - c), me).wait_recv()
        for cp in first + passed:
            cp.wait_send()
        for cp in mine:
            cp.wait()

    hbm = pl.BlockSpec(memory_space=pl.ANY)
    return pl.pallas_call(
        body, name=name,
        out_shape=[jax.ShapeDtypeStruct((N_DEV,) + b.shape, b.dtype) for b in blocks],
        in_specs=[hbm] * n, out_specs=[hbm] * n,
        scratch_shapes=[pltpu.SemaphoreType.DMA((7 * n,)), pltpu.SemaphoreType.DMA((7 * n,)),
                        pltpu.SemaphoreType.DMA((n,))],
    )(*blocks)


def _exchange_grads(layer_chunks, small_chunks, rep_block, name):
    flows, inputs = [], []
    for p, per_layer in enumerate(layer_chunks):
        for l, arr in enumerate(per_layer):
            flows.append(("param", p, l))
            inputs.append(arr)
    flows += [("small",), ("rep",)]
    inputs += [small_chunks, rep_block]
    n_par = len(layer_chunks)
    n_in, n_out, nf = len(inputs), n_par + 2, len(flows)

    def body(*refs):
        in_refs, out_refs = refs[:n_in], refs[n_in:n_in + n_out]
        send_sems, recv_sems, local_sems = refs[n_in + n_out:]
        x, y, c = lax.axis_index("x"), lax.axis_index("y"), lax.axis_index("c")
        me = 4 * x + 2 * y + c

        def src(f, dev):
            return in_refs[f] if flows[f][0] == "rep" else in_refs[f].at[dev]

        def dst(f, dev):
            if flows[f][0] == "param":
                _, p, l = flows[f]
                return out_refs[p].at[dev, l]
            return out_refs[n_par + (0 if flows[f][0] == "small" else 1)].at[dev]

        mine = [pltpu.make_async_copy(src(f, me), dst(f, me), local_sems.at[f]) for f in range(nf)]
        for cp in mine:
            cp.start()
        copies = []
        for k in range(1, N_DEV):
            px, py, pc = _flip(x, k & 4), _flip(y, k & 2), _flip(c, k & 1)
            peer = 4 * px + 2 * py + pc
            for f in range(nf):
                sems = dict(send_sem=send_sems.at[7 * f + k - 1], recv_sem=recv_sems.at[7 * f + k - 1],
                            device_id=(px, py, pc), device_id_type=pl.DeviceIdType.MESH)
                send = pltpu.make_async_remote_copy(src_ref=src(f, peer), dst_ref=dst(f, me), **sems)
                recv = pltpu.make_async_remote_copy(src_ref=src(f, peer), dst_ref=dst(f, peer), **sems)
                send.start()
                copies.append((send, recv))
        for send, recv in copies:
            recv.wait_recv()
        for send, recv in copies:
            send.wait_send()
        for cp in mine:
            cp.wait()

    out_shape = [jax.ShapeDtypeStruct((N_DEV, len(pl_)) + pl_[0].shape[1:], pl_[0].dtype) for pl_ in layer_chunks]
    out_shape += [jax.ShapeDtypeStruct(small_chunks.shape, small_chunks.dtype),
                  jax.ShapeDtypeStruct((N_DEV,) + rep_block.shape, rep_block.dtype)]
    hbm = pl.BlockSpec(memory_space=pl.ANY)
    return pl.pallas_call(
        body, name=name, out_shape=out_shape,
        in_specs=[hbm] * n_in, out_specs=[hbm] * n_out,
        scratch_shapes=[pltpu.SemaphoreType.DMA((7 * nf,)), pltpu.SemaphoreType.DMA((7 * nf,)),
                        pltpu.SemaphoreType.DMA((nf,))],
    )(*inputs)


def _pack(arrs, dtype, row_mult=8):
    parts, offs, r = [], [], 0
    for a in arrs:
        flat = a.astype(dtype).reshape(-1)
        nrow = -(-flat.shape[0] // LANES)
        flat = jnp.pad(flat, (0, nrow * LANES - flat.shape[0]))
        parts.append(flat.reshape(nrow, LANES))
        offs.append((r, nrow))
        r += nrow
    pad = (-r) % row_mult
    if pad:
        parts.append(jnp.zeros((pad, LANES), dtype))
    return jnp.concatenate(parts, axis=0), offs


def _unpack(buf, offs, shapes, lead=()):
    outs = []
    for (r, nrow), shp in zip(offs, shapes):
        size = 1
        for s in shp:
            size *= s
        flat = buf[..., r:r + nrow, :].reshape(lead + (nrow * LANES,))
        outs.append(flat[..., :size].reshape(lead + tuple(shp)))
    return outs


def _cols_from_shards(g, axis):
    return jnp.concatenate([g[j] for j in range(N_DEV)], axis=axis)


BIG = ("w_in", "w_o", "w_up", "w_down")
SMALL_SHARDED = ("meta_tokens", "w_conv", "w_ffn_conv")
REPLICATED = ("hg_lower_bounds", "w_pool", "pool_scale", "hg_norm_g", "ln1_g", "ln1_b", "b_ffn_conv", "ln2_g", "ln2_b")
WEIGHTS = ("meta_tokens", "hg_lower_bounds", "w_in", "w_conv", "w_pool", "pool_scale", "hg_norm_g", "w_o",
           "ln1_g", "ln1_b", "w_up", "w_ffn_conv", "b_ffn_conv", "w_down", "ln2_g", "ln2_b")


def _pool_blockdiag(w_pool_l):
    z = jnp.zeros((POOL_GROUP, POOL_GROUP), w_pool_l.dtype)
    rows = [jnp.concatenate([w_pool_l[g] if h == g else z for h in range(4)], axis=1) for g in range(4)]
    return jnp.concatenate(rows, axis=0)


def _layer_weights(full, l):
    w_in = full["w_in"][l]
    w_o = full["w_o"][l]
    return dict(
        w_hg=w_in[:, 768:2816],
        w_cp=jnp.concatenate([w_in[:, 0:768], w_in[:, 2816:3072]], axis=1),
        w_o=jnp.concatenate([w_o[256:768], w_o[0:256], w_o[768:1024]], axis=0),
        w_up=full["w_up"][l], w_down=full["w_down"][l],
    )


def kernel(x, meta_tokens, hg_lower_bounds, w_in, w_conv, w_pool, pool_scale, hg_norm_g, w_o, ln1_g, ln1_b, w_up, w_ffn_conv, b_ffn_conv, w_down, ln2_g, ln2_b, loss_target, m_meta_tokens, m_hg_lower_bounds, m_w_in, m_w_conv, m_w_pool, m_pool_scale, m_hg_norm_g, m_w_o, m_ln1_g, m_ln1_b, m_w_up, m_w_ffn_conv, m_b_ffn_conv, m_w_down, m_ln2_g, m_ln2_b, v_meta_tokens, v_hg_lower_bounds, v_w_in, v_w_conv, v_w_pool, v_pool_scale, v_hg_norm_g, v_w_o, v_ln1_g, v_ln1_b, v_w_up, v_w_ffn_conv, v_b_ffn_conv, v_w_down, v_ln2_g, v_ln2_b):
    W = dict(meta_tokens=meta_tokens, hg_lower_bounds=hg_lower_bounds, w_in=w_in, w_conv=w_conv, w_pool=w_pool,
             pool_scale=pool_scale, hg_norm_g=hg_norm_g, w_o=w_o, ln1_g=ln1_g, ln1_b=ln1_b, w_up=w_up,
             w_ffn_conv=w_ffn_conv, b_ffn_conv=b_ffn_conv, w_down=w_down, ln2_g=ln2_g, ln2_b=ln2_b)
    M = dict(meta_tokens=m_meta_tokens, hg_lower_bounds=m_hg_lower_bounds, w_in=m_w_in, w_conv=m_w_conv,
             w_pool=m_w_pool, pool_scale=m_pool_scale, hg_norm_g=m_hg_norm_g, w_o=m_w_o, ln1_g=m_ln1_g,
             ln1_b=m_ln1_b, w_up=m_w_up, w_ffn_conv=m_w_ffn_conv, b_ffn_conv=m_b_ffn_conv, w_down=m_w_down,
             ln2_g=m_ln2_g, ln2_b=m_ln2_b)
    V = dict(meta_tokens=v_meta_tokens, hg_lower_bounds=v_hg_lower_bounds, w_in=v_w_in, w_conv=v_w_conv,
             w_pool=v_w_pool, pool_scale=v_pool_scale, hg_norm_g=v_hg_norm_g, w_o=v_w_o, ln1_g=v_ln1_g,
             ln1_b=v_ln1_b, w_up=v_w_up, w_ffn_conv=v_w_ffn_conv, b_ffn_conv=v_b_ffn_conv, w_down=v_w_down,
             ln2_g=v_ln2_g, ln2_b=v_ln2_b)
    assert x.shape[0] == 1 and x.shape[2] == D_MODEL and w_in.shape[0] == DEPTH
    seq = x.shape[1]
    L = -(-(seq + N_META) // ROW_ALIGN) * ROW_ALIGN

    small_pack, small_offs = _pack([W[n] for n in SMALL_SHARDED], F32)
    g_in, g_o, g_up, g_down, small_all = _gather_many([W[n].astype(BF16) for n in BIG] + [small_pack], "gather_weights")
    full = dict(
        w_in=jnp.transpose(g_in, (1, 2, 0, 3)).reshape(DEPTH, D_MODEL, -1),
        w_o=jnp.transpose(g_o, (1, 0, 2, 3)).reshape(DEPTH, -1, D_MODEL),
        w_up=jnp.transpose(g_up, (1, 2, 0, 3)).reshape(DEPTH, D_MODEL, -1),
        w_down=jnp.transpose(g_down, (1, 0, 2, 3)).reshape(DEPTH, -1, D_MODEL))
    for n, a in zip(SMALL_SHARDED, _unpack(small_all, small_offs, [W[n].shape for n in SMALL_SHARDED], (N_DEV,))):
        full[n] = _cols_from_shards(a, 1)

    pad_rows = L - N_META - seq
    xp = jnp.concatenate([full["meta_tokens"], x[0], jnp.zeros((pad_rows, D_MODEL), F32)], axis=0)
    tgt = jnp.concatenate([jnp.zeros((N_META, D_MODEL), F32), loss_target[0], jnp.zeros((pad_rows, D_MODEL), F32)], axis=0)

    saved = []
    h_in, h_in_b = xp, xp.astype(BF16)
    for l in range(DEPTH):
        lw = _layer_weights(full, l)
        wc = full["w_conv"][l].T
        wblk = _pool_blockdiag(w_pool[l]).astype(BF16)
        ps = pool_scale[l][None, :]
        gn = hg_norm_g[l][None, :]
        wf = full["w_ffn_conv"][l].T
        bf = b_ffn_conv[l][None, :]
        hh = _matmul(h_in_b, lw["w_hg"], "nn", F32, f"fwd_hg_{l}")
        hc = _matmul(h_in_b, lw["w_cp"], "nn", F32, f"fwd_cp_{l}")
        y_hg, o_raw, states = _hgrn_fwd(hh, hg_lower_bounds, gn, l, f"hgrn_fwd_{l}")
        y_cp = _cp_fwd(hc, wc, wblk, ps, f"convpool_fwd_{l}")
        cat = jnp.concatenate([y_hg, y_cp], axis=1)
        mix = _matmul(cat, lw["w_o"], "nn", F32, f"fwd_o_{l}")
        x1, x1_b = _ln_fwd(h_in, mix, ln1_g[l][None, :], ln1_b[l][None, :], f"ln1_fwd_{l}")
        up = _matmul(x1_b, lw["w_up"], "nn", BF16, f"fwd_up_{l}")
        a = _ffn_act_fwd(up, wf, bf, f"ffn_fwd_{l}")
        ffn = _matmul(a, lw["w_down"], "nn", F32, f"fwd_down_{l}")
        x2, x2_b = _ln_fwd(x1, ffn, ln2_g[l][None, :], ln2_b[l][None, :], f"ln2_fwd_{l}")
        saved.append(dict(lw=lw, wc=wc, wblk=wblk, ps=ps, gn=gn, wf=wf, bf=bf, x=h_in, x_b=h_in_b, hh=hh, hc=hc,
                          o_raw=o_raw, states=states, cat=cat, mix=mix, x1=x1, x1_b=x1_b, up=up, a=a, ffn=ffn))
        h_in, h_in_b = x2, x2_b

    dy, loss_part = _loss_head(h_in, tgt, seq)
    loss = lax.psum(loss_part[0, 0], ("x", "y", "c"))

    G = {}
    per_layer = {n: [None] * DEPTH for n in ("w_conv", "w_pool", "pool_scale", "hg_norm_g", "ln1_g", "ln1_b",
                                             "w_ffn_conv", "b_ffn_conv", "ln2_g", "ln2_b")}
    chunks = {n: [None] * DEPTH for n in BIG}
    dlb_total = jnp.zeros((DEPTH, HG_W), F32)
    for l in reversed(range(DEPTH)):
        s = saved[l]
        lw = s["lw"]
        dz2, dz2_b, dg2, db2 = _ln_bwd(s["x1"], s["ffn"], dy, ln2_g[l][None, :], f"ln2_bwd_{l}")
        da = _matmul(dz2_b, lw["w_down"], "nt", BF16, f"bwd_da_{l}")
        d_w_down = _matmul(s["a"], dz2_b, "tn", BF16, f"wgrad_down_{l}")
        dup, dwf, dbf = _ffn_act_bwd(s["up"], da, s["wf"], s["bf"], f"ffn_bwd_{l}")
        dx1 = _matmul(dup, lw["w_up"], "nt", F32, f"bwd_dx1_{l}", res=dz2, alpha=ALPHA)
        d_w_up = _matmul(s["x1_b"], dup, "tn", BF16, f"wgrad_up_{l}")
        dz1, dz1_b, dg1, db1 = _ln_bwd(s["x"], s["mix"], dx1, ln1_g[l][None, :], f"ln1_bwd_{l}")
        dcat = _matmul(dz1_b, lw["w_o"], "nt", F32, f"bwd_dcat_{l}")
        d_w_o = _matmul(s["cat"], dz1_b, "tn", BF16, f"wgrad_o_{l}")
        dhh, dlb, dgn = _hgrn_bwd(s["hh"], s["o_raw"], s["states"], dcat, hg_lower_bounds, s["gn"], l, f"hgrn_bwd_{l}")
        dhc, dwc, dwblk, dps = _cp_bwd(s["hc"], dcat, s["wc"], s["wblk"], s["ps"], f"convpool_bwd_{l}")
        dx_a = _matmul(dhh, lw["w_hg"], "nt", F32, f"bwd_dx_hg_{l}", res=dz1, alpha=ALPHA)
        dx = _matmul(dhc, lw["w_cp"], "nt", F32, f"bwd_dx_cp_{l}", res=dx_a, alpha=1.0)
        d_w_hg = _matmul(s["x_b"], dhh, "tn", BF16, f"wgrad_hg_{l}")
        d_w_cp = _matmul(s["x_b"], dhc, "tn", BF16, f"wgrad_cp_{l}")
        d_w_in = jnp.concatenate([d_w_cp[:, 0:768], d_w_hg, d_w_cp[:, 768:1024]], axis=1)
        chunks["w_in"][l] = jnp.transpose(d_w_in.reshape(D_MODEL, N_DEV, -1), (1, 0, 2))
        chunks["w_o"][l] = jnp.concatenate([d_w_o[512:768], d_w_o[0:512], d_w_o[768:1024]], axis=0).reshape(N_DEV, -1, D_MODEL)
        chunks["w_up"][l] = jnp.transpose(d_w_up.reshape(D_MODEL, N_DEV, -1), (1, 0, 2))
        chunks["w_down"][l] = d_w_down.reshape(N_DEV, -1, D_MODEL)
        per_layer["w_conv"][l] = dwc.T
        per_layer["w_ffn_conv"][l] = dwf.T
        per_layer["b_ffn_conv"][l] = dbf[0]
        per_layer["w_pool"][l] = jnp.stack([dwblk[g * 64:(g + 1) * 64, g * 64:(g + 1) * 64] for g in range(4)], axis=0)
        per_layer["pool_scale"][l] = dps[0]
        per_layer["hg_norm_g"][l] = dgn[0]
        per_layer["ln1_g"][l], per_layer["ln1_b"][l] = dg1[0], db1[0]
        per_layer["ln2_g"][l], per_layer["ln2_b"][l] = dg2[0], db2[0]
        dlb_total = dlb_total + dlb
        dy = dx
    for n, parts in per_layer.items():
        G[n] = jnp.stack(parts, axis=0)
    G["hg_lower_bounds"] = dlb_total
    grad_x = dy[N_META:N_META + seq][None]

    def shard_major(g, lead):
        g = g.reshape(g.shape[:lead] + (N_DEV, -1) + g.shape[lead + 1:])
        g = jnp.moveaxis(g, lead, 0).reshape(N_DEV, -1)
        return jnp.pad(g, ((0, 0), (0, (-g.shape[1]) % LANES)))

    small_flat = jnp.concatenate([shard_major(dy[0:N_META], 1), shard_major(G["w_conv"], 1),
                                  shard_major(G["w_ffn_conv"], 1)], axis=1)
    w_small, _ = _pack([W[n] for n in SMALL_SHARDED], F32)
    small_rows = w_small.shape[0]
    small_chunks = jnp.pad(small_flat.reshape(N_DEV, -1, LANES),
                           ((0, 0), (0, small_rows - small_flat.shape[1] // LANES), (0, 0)))
    rep_pack, rep_offs = _pack([G[n] for n in REPLICATED], F32)
    received = _exchange_grads([chunks[n] for n in BIG], small_chunks, rep_pack, "exchange_grads")
    small_recv, rep_all = received[len(BIG)], received[len(BIG) + 1]

    res = {k: {} for k in ("grad", "delta", "new_m", "new_v")}
    kinds = ("grad", "delta", "new_m", "new_v")
    for n, gp in zip(BIG, received[:len(BIG)]):
        for kind, a in zip(kinds, _adamw_layers(gp, W[n], M[n], V[n], f"adamw_{n}")):
            res[kind][n] = a
    m_small, _ = _pack([M[n] for n in SMALL_SHARDED], F32)
    v_small, _ = _pack([V[n] for n in SMALL_SHARDED], F32)
    outs_small = _adamw(small_recv, w_small, m_small, v_small, "adamw_small_sharded")
    w_rep, _ = _pack([W[n] for n in REPLICATED], F32)
    m_rep, _ = _pack([M[n] for n in REPLICATED], F32)
    v_rep, _ = _pack([V[n] for n in REPLICATED], F32)
    outs_rep = _adamw(rep_all, w_rep, m_rep, v_rep, "adamw_replicated")
    for kind, b_sm, b_rep in zip(kinds, outs_small, outs_rep):
        for n, a in zip(SMALL_SHARDED, _unpack(b_sm, small_offs, [W[n].shape for n in SMALL_SHARDED])):
            res[kind][n] = a
        for n, a in zip(REPLICATED, _unpack(b_rep, rep_offs, [W[n].shape for n in REPLICATED])):
            res[kind][n] = a

    return (loss, grad_x, *[res["grad"][n] for n in WEIGHTS], *[res["delta"][n] for n in WEIGHTS],
            *[res["new_m"][n] for n in WEIGHTS], *[res["new_v"][n] for n in WEIGHTS])
```

```python
import jax
import jax.numpy as jnp
from jax import lax
from jax.experimental import pallas as pl
from jax.experimental.pallas import tpu as pltpu

F32 = jnp.float32
BF16 = jnp.bfloat16

N_DEV = 8
D_MODEL = 1024
N_META = 16
DEPTH = 2
CONV_W = 256
HG_W = 512
HG_D = 128
HG_HEADS = 4
POOL_W = 256
POOL_GROUP = 64
D_FF = 2816
ALPHA = (2 * DEPTH) ** 0.25
LN_EPS = 1e-5
RMS_EPS = 1e-6
F_FLOOR = 1e-30
Q_SCALE = HG_D ** -0.5
SUB = 16
SEQ_TILE = 192
FFN_TILE = 96
ROW_ALIGN = 192
LANES = 128
VMEM_LIMIT = 48 * 1024 * 1024
MATMUL_VMEM_BUDGET = 38 * 1024 * 1024

ADAM_LR = 0.001
ADAM_B1 = 0.9
ADAM_B2 = 0.999
ADAM_EPS = 1e-08
ADAM_WD = 0.01
ADAM_STEP = 10


def _tile(n, cap, mult):
    best = 0
    for t in range(mult, min(n, cap) + 1, mult):
        if n % t == 0:
            best = t
    assert best > 0, (n, cap, mult)
    return best


def _params(sem, vmem=VMEM_LIMIT):
    return pltpu.CompilerParams(dimension_semantics=sem, vmem_limit_bytes=vmem)


def _dnt(a, b):
    return lax.dot_general(a, b, (((1,), (1,)), ((), ())), preferred_element_type=F32)


def _dtn(a, b):
    return lax.dot_general(a, b, (((0,), (0,)), ((), ())), preferred_element_type=F32)


def _dnn(a, b):
    return jnp.dot(a, b, preferred_element_type=F32)


def _sigmoid(x):
    return 1.0 / (1.0 + jnp.exp(-x))


def _matmul(a, b, mode, out_dtype, name, res=None, alpha=1.0):
    if mode == "tn":
        K, M = a.shape
    else:
        M, K = a.shape
    N = b.shape[0] if mode == "nt" else b.shape[1]
    out_bytes = jnp.dtype(out_dtype).itemsize
    tn = _tile(N, 1536, LANES)
    tk = _tile(K, 1536, 16) if mode == "tn" else _tile(K, 2816, LANES)
    nk = K // tk
    use_acc = nk > 1 and out_dtype != F32
    tm = M
    for cap in (1536, 768, 384):
        tm = _tile(M, cap, 16)
        blocks = 2 * (a.dtype.itemsize * tm * tk + b.dtype.itemsize * tn * tk + out_bytes * tm * tn
                      + (4 * tm * tn if res is not None else 0)) + (4 * tm * tn if use_acc else 0)
        if blocks <= MATMUL_VMEM_BUDGET:
            break
    dims = {"nn": ((1,), (0,)), "nt": ((1,), (1,)), "tn": ((0,), (0,))}[mode]

    def body(*refs):
        a_ref, b_ref = refs[0], refs[1]
        r_ref = refs[2] if res is not None else None
        o_ref = refs[3] if res is not None else refs[2]
        acc = refs[-1] if use_acc else o_ref
        k = pl.program_id(2)
        p = lax.dot_general(a_ref[...].astype(BF16), b_ref[...].astype(BF16), (dims, ((), ())),
                            preferred_element_type=F32)

        def finish(r):
            if r_ref is not None:
                r = r + alpha * r_ref[...]
            o_ref[...] = r.astype(out_dtype)

        if nk == 1:
            finish(p)
        else:
            @pl.when(k == 0)
            def _():
                acc[...] = p

            @pl.when((k > 0) & (k < nk - 1))
            def _():
                acc[...] += p

            @pl.when(k == nk - 1)
            def _():
                finish(acc[...] + p)

    if mode == "tn":
        a_spec = pl.BlockSpec((tk, tm), lambda i, j, k: (k, i))
    else:
        a_spec = pl.BlockSpec((tm, tk), lambda i, j, k: (i, k))
    if mode == "nt":
        b_spec = pl.BlockSpec((tn, tk), lambda i, j, k: (j, k))
    else:
        b_spec = pl.BlockSpec((tk, tn), lambda i, j, k: (k, j))
    in_specs = [a_spec, b_spec]
    args = [a, b]
    if res is not None:
        in_specs.append(pl.BlockSpec((tm, tn), lambda i, j, k: (i, j)))
        args.append(res)
    return pl.pallas_call(
        body, name=name,
        grid=(M // tm, N // tn, nk),
        in_specs=in_specs,
        out_specs=pl.BlockSpec((tm, tn), lambda i, j, k: (i, j)),
        out_shape=jax.ShapeDtypeStruct((M, N), out_dtype),
        scratch_shapes=[pltpu.VMEM((tm, tn), F32)] if use_acc else [],
        compiler_params=_params(("parallel", "parallel", "arbitrary")),
    )(*args)


def _ln_fwd(x, f, g, b, name):
    L, D = x.shape
    tr = _tile(L, 768, 16)

    def body(x_ref, f_ref, g_ref, b_ref, y_ref, yb_ref):
        z = ALPHA * x_ref[...] + f_ref[...]
        mu = jnp.mean(z, axis=-1, keepdims=True)
        zc = z - mu
        var = jnp.mean(zc * zc, axis=-1, keepdims=True)
        y = zc * lax.rsqrt(var + LN_EPS) * g_ref[...] + b_ref[...]
        y_ref[...] = y
        yb_ref[...] = y.astype(BF16)

    row = pl.BlockSpec((tr, D), lambda i: (i, 0))
    vec = pl.BlockSpec((1, D), lambda i: (0, 0))
    return pl.pallas_call(
        body, name=name, grid=(L // tr,),
        in_specs=[row, row, vec, vec], out_specs=[row, row],
        out_shape=[jax.ShapeDtypeStruct((L, D), F32), jax.ShapeDtypeStruct((L, D), BF16)],
        compiler_params=_params(("parallel",)),
    )(x, f, g, b)


def _ln_bwd(x, f, dy, g, name):
    L, D = x.shape
    tr = _tile(L, 768, 16)

    def body(x_ref, f_ref, dy_ref, g_ref, dz_ref, dzb_ref, dg_ref, db_ref):
        @pl.when(pl.program_id(0) == 0)
        def _():
            dg_ref[...] = jnp.zeros_like(dg_ref)
            db_ref[...] = jnp.zeros_like(db_ref)

        z = ALPHA * x_ref[...] + f_ref[...]
        mu = jnp.mean(z, axis=-1, keepdims=True)
        zc = z - mu
        var = jnp.mean(zc * zc, axis=-1, keepdims=True)
        rstd = lax.rsqrt(var + LN_EPS)
        xhat = zc * rstd
        dy = dy_ref[...]
        dxh = dy * g_ref[...]
        m1 = jnp.mean(dxh, axis=-1, keepdims=True)
        m2 = jnp.mean(dxh * xhat, axis=-1, keepdims=True)
        dz = rstd * (dxh - m1 - xhat * m2)
        dz_ref[...] = dz
        dzb_ref[...] = dz.astype(BF16)
        dg_ref[...] += jnp.sum(dy * xhat, axis=0, keepdims=True)
        db_ref[...] += jnp.sum(dy, axis=0, keepdims=True)

    row = pl.BlockSpec((tr, D), lambda i: (i, 0))
    vec = pl.BlockSpec((1, D), lambda i: (0, 0))
    return pl.pallas_call(
        body, name=name, grid=(L // tr,),
        in_specs=[row, row, row, vec], out_specs=[row, row, vec, vec],
        out_shape=[jax.ShapeDtypeStruct((L, D), F32), jax.ShapeDtypeStruct((L, D), BF16),
                   jax.ShapeDtypeStruct((1, D), F32), jax.ShapeDtypeStruct((1, D), F32)],
        compiler_params=_params(("arbitrary",)),
    )(x, f, dy, g)


def _loss_head(y, tgt, seq):
    L, D = y.shape
    tr = _tile(L, 768, 16)

    def body(y_ref, t_ref, dy_ref, loss_ref):
        i = pl.program_id(0)

        @pl.when(i == 0)
        def _():
            loss_ref[...] = jnp.zeros_like(loss_ref)

        r = i * tr + lax.broadcasted_iota(jnp.int32, (tr, D), 0)
        valid = (r >= N_META) & (r < N_META + seq)
        e = jnp.where(valid, y_ref[...] - t_ref[...], 0.0)
        dy_ref[...] = e * (1.0 / D)
        s = jnp.sum(jnp.sum(e * e, axis=-1, keepdims=True), axis=0, keepdims=True)
        loss_ref[...] += (0.5 / D) * s

    row = pl.BlockSpec((tr, D), lambda i: (i, 0))
    return pl.pallas_call(
        body, name="loss_head", grid=(L // tr,),
        in_specs=[row, row], out_specs=[row, pl.BlockSpec((1, 1), lambda i: (0, 0))],
        out_shape=[jax.ShapeDtypeStruct((L, D), F32), jax.ShapeDtypeStruct((1, 1), F32)],
        compiler_params=_params(("arbitrary",)),
    )(y, tgt)


def _shift_down(x, prev, k):
    out = pltpu.roll(x, k, 0)
    row = lax.broadcasted_iota(jnp.int32, (8, x.shape[1]), 0)
    top = out[0:8]
    for r in range(k):
        top = jnp.where(row == r, prev[8 - k + r:8 - k + r + 1, :], top)
    return top if x.shape[0] == 8 else jnp.concatenate([top, out[8:]], axis=0)


def _shift_up(x, nxt, k):
    T = x.shape[0]
    out = pltpu.roll(x, T - k, 0)
    row = lax.broadcasted_iota(jnp.int32, (8, x.shape[1]), 0)
    bot = out[T - 8:T]
    for r in range(k):
        bot = jnp.where(row == 8 - k + r, nxt[r:r + 1, :], bot)
    return bot if T == 8 else jnp.concatenate([out[:T - 8], bot], axis=0)


def _conv3(x, prev, w, b):
    return w[2:3, :] * x + w[1:2, :] * _shift_down(x, prev, 1) + w[0:1, :] * _shift_down(x, prev, 2) + b


def _ffn_act_fwd(up, w, b, name):
    L, C = up.shape
    F = C // 2
    ts = FFN_TILE
    n = L // ts

    def body(up_ref, pv_ref, w_ref, b_ref, a_ref):
        i = pl.program_id(0)
        x = up_ref[...].astype(F32)
        prev = jnp.where(i > 0, pv_ref[...].astype(F32)[8:16], 0.0)
        u = _conv3(x, prev, w_ref[...], b_ref[...])
        gate = u[:, :F]
        a_ref[...] = (gate * _sigmoid(gate) * u[:, F:]).astype(BF16)

    return pl.pallas_call(
        body, name=name, grid=(n,),
        in_specs=[pl.BlockSpec((ts, C), lambda i: (i, 0)),
                  pl.BlockSpec((16, C), lambda i: (jnp.maximum(i * (ts // 16) - 1, 0), 0)),
                  pl.BlockSpec((3, C), lambda i: (0, 0)), pl.BlockSpec((1, C), lambda i: (0, 0))],
        out_specs=pl.BlockSpec((ts, F), lambda i: (i, 0)),
        out_shape=jax.ShapeDtypeStruct((L, F), BF16),
        compiler_params=_params(("parallel",)),
    )(up, up, w, b)


def _ffn_act_bwd(up, da, w, b, name):
    L, C = up.shape
    F = C // 2
    ts = FFN_TILE
    n = L // ts
    last16 = L // 16 - 1

    def du_of(u, da):
        gate, val = u[:, :F], u[:, F:]
        sg = _sigmoid(gate)
        dgate = da * val * (sg * (1.0 + gate * (1.0 - sg)))
        dval = da * (gate * sg)
        return jnp.concatenate([dgate, dval], axis=1)

    def body(up_ref, pv_ref, nx_ref, da_ref, dan_ref, w_ref, b_ref, dup_ref, dw_ref, db_ref):
        i = pl.program_id(0)

        @pl.when(i == 0)
        def _():
            dw_ref[...] = jnp.zeros_like(dw_ref)
            db_ref[...] = jnp.zeros_like(db_ref)

        w = w_ref[...]
        bias = b_ref[...]
        x = up_ref[...].astype(F32)
        prev = jnp.where(i > 0, pv_ref[...].astype(F32)[8:16], 0.0)
        u = _conv3(x, prev, w, bias)
        u_next = _conv3(nx_ref[...].astype(F32)[0:8], x[ts - 8:ts], w, bias)
        du = du_of(u, da_ref[...].astype(F32))
        dun = jnp.where(i < n - 1, du_of(u_next, dan_ref[...].astype(F32)[0:8]), 0.0)
        du1 = _shift_up(du, dun, 1)
        du2 = _shift_up(du, dun, 2)
        dup_ref[...] = (w[2:3, :] * du + w[1:2, :] * du1 + w[0:1, :] * du2).astype(BF16)
        dw_ref[...] += jnp.concatenate([jnp.sum(x * du2, axis=0, keepdims=True),
                                        jnp.sum(x * du1, axis=0, keepdims=True),
                                        jnp.sum(x * du, axis=0, keepdims=True)], axis=0)
        db_ref[...] += jnp.sum(du, axis=0, keepdims=True)

    prv = lambda i: (jnp.maximum(i * (ts // 16) - 1, 0), 0)
    nxt = lambda i: (jnp.minimum((i + 1) * (ts // 16), last16), 0)
    return pl.pallas_call(
        body, name=name, grid=(n,),
        in_specs=[pl.BlockSpec((ts, C), lambda i: (i, 0)), pl.BlockSpec((16, C), prv), pl.BlockSpec((16, C), nxt),
                  pl.BlockSpec((ts, F), lambda i: (i, 0)), pl.BlockSpec((16, F), nxt),
                  pl.BlockSpec((3, C), lambda i: (0, 0)), pl.BlockSpec((1, C), lambda i: (0, 0))],
        out_specs=[pl.BlockSpec((ts, C), lambda i: (i, 0)), pl.BlockSpec((3, C), lambda i: (0, 0)),
                   pl.BlockSpec((1, C), lambda i: (0, 0))],
        out_shape=[jax.ShapeDtypeStruct((L, C), BF16), jax.ShapeDtypeStruct((3, C), F32),
                   jax.ShapeDtypeStruct((1, C), F32)],
        compiler_params=_params(("arbitrary",)),
    )(up, up, up, da, da, w, b)


def _pool_window(ext, tile_rows, first_row, lead):
    T = ext.shape[0]
    sh = (lambda x, k: pltpu.roll(x, T - k, 0)) if lead else (lambda x, k: pltpu.roll(x, k, 0))
    r2 = ext + sh(ext, 1)
    r4 = r2 + sh(r2, 2)
    r8 = r4 + sh(r4, 4)
    r16 = r8 + sh(r8, 8)
    lo = 0 if lead else 16
    grp = lax.broadcasted_iota(jnp.int32, (tile_rows, POOL_W), 1) // POOL_GROUP
    pick = lambda a, b, c, d: jnp.where(grp == 0, a, jnp.where(grp == 1, b, jnp.where(grp == 2, c, d)))
    win = pick(r2[lo:lo + tile_rows], r4[lo:lo + tile_rows], r8[lo:lo + tile_rows], r16[lo:lo + tile_rows])
    return win, pick(2.0, 4.0, 8.0, 16.0)


def _pool_count(first_row, rows, wlen):
    t1 = (first_row + lax.broadcasted_iota(jnp.int32, (rows, POOL_W), 0) + 1).astype(F32)
    return jnp.minimum(t1, wlen)


def _cp_fwd(hc, wc, wblk, pscale, name):
    L = hc.shape[0]
    ts = SEQ_TILE
    n = L // ts

    def body(h_ref, hp_ref, wc_ref, wb_ref, ps_ref, y_ref):
        i = pl.program_id(0)
        h = h_ref[...]
        hp = jnp.where(i > 0, hp_ref[...], 0.0)
        cb, cc, cv, pv = h[:, 0:256], h[:, 256:512], h[:, 512:768], h[:, 768:1024]
        p = cc * cv
        pp = hp[8:16, 256:512] * hp[8:16, 512:768]
        w = wc_ref[...]
        conv = w[2:3, :] * p + w[1:2, :] * _shift_down(p, pp, 1) + w[0:1, :] * _shift_down(p, pp, 2)
        y_conv = cb * conv
        ext = jnp.concatenate([hp[:, 768:1024], pv], axis=0)
        win, wlen = _pool_window(ext, ts, i * ts, False)
        d = win / _pool_count(i * ts, ts, wlen) - pv
        y_pool = _dnn(d.astype(BF16), wb_ref[...]) * ps_ref[...]
        y_ref[...] = jnp.concatenate([y_conv, y_pool], axis=1).astype(BF16)

    return pl.pallas_call(
        body, name=name, grid=(n,),
        in_specs=[pl.BlockSpec((ts, 1024), lambda i: (i, 0)),
                  pl.BlockSpec((16, 1024), lambda i: (jnp.maximum(i * (ts // 16) - 1, 0), 0)),
                  pl.BlockSpec((3, 256), lambda i: (0, 0)), pl.BlockSpec((256, 256), lambda i: (0, 0)),
                  pl.BlockSpec((1, 256), lambda i: (0, 0))],
        out_specs=pl.BlockSpec((ts, 512), lambda i: (i, 0)),
        out_shape=jax.ShapeDtypeStruct((L, 512), BF16),
        compiler_params=_params(("parallel",)),
    )(hc, hc, wc, wblk, pscale)


def _cp_bwd(hc, dcat, wc, wblk, pscale, name):
    L = hc.shape[0]
    ts = SEQ_TILE
    n = L // ts
    last16 = L // 16 - 1

    def body(h_ref, hp_ref, hn_ref, dy_ref, dyn_ref, wc_ref, wb_ref, ps_ref,
             dh_ref, dwc_ref, dwb_ref, dps_ref):
        i = pl.program_id(0)

        @pl.when(i == 0)
        def _():
            dwc_ref[...] = jnp.zeros_like(dwc_ref)
            dwb_ref[...] = jnp.zeros_like(dwb_ref)
            dps_ref[...] = jnp.zeros_like(dps_ref)

        h = h_ref[...]
        hp = jnp.where(i > 0, hp_ref[...], 0.0)
        hn = hn_ref[...]
        dy = dy_ref[...]
        dyn = jnp.where(i < n - 1, dyn_ref[...], 0.0)
        cb, cc, cv, pv = h[:, 0:256], h[:, 256:512], h[:, 512:768], h[:, 768:1024]
        w = wc_ref[...]
        p = cc * cv
        pp = hp[8:16, 256:512] * hp[8:16, 512:768]
        p1 = _shift_down(p, pp, 1)
        p2 = _shift_down(p, pp, 2)
        conv = w[2:3, :] * p + w[1:2, :] * p1 + w[0:1, :] * p2
        dyc = dy[:, 0:256]
        dcb = dyc * conv
        dconv = dyc * cb
        dconv_n = dyn[0:8, 0:256] * hn[0:8, 0:256]
        dc1 = _shift_up(dconv, dconv_n, 1)
        dc2 = _shift_up(dconv, dconv_n, 2)
        dp = w[2:3, :] * dconv + w[1:2, :] * dc1 + w[0:1, :] * dc2
        dwc_ref[...] += jnp.concatenate([jnp.sum(p * dc2, axis=0, keepdims=True),
                                         jnp.sum(p * dc1, axis=0, keepdims=True),
                                         jnp.sum(p * dconv, axis=0, keepdims=True)], axis=0)
        ps = ps_ref[...]
        wb = wb_ref[...]
        ext = jnp.concatenate([hp[:, 768:1024], pv], axis=0)
        win, wlen = _pool_window(ext, ts, i * ts, False)
        d = win / _pool_count(i * ts, ts, wlen) - pv
        db = d.astype(BF16)
        dyp = dy[:, 256:512]
        dps_ref[...] += jnp.sum(dyp * _dnn(db, wb), axis=0, keepdims=True)
        dypre = (dyp * ps).astype(BF16)
        dwb_ref[...] += _dtn(db, dypre)
        dd = _dnt(dypre, wb)
        ddn = _dnt((dyn[:, 256:512] * ps).astype(BF16), wb)
        e = dd / _pool_count(i * ts, ts, wlen)
        en = ddn / _pool_count((i + 1) * ts, 16, wlen[0:16])
        lead, _ = _pool_window(jnp.concatenate([e, en], axis=0), ts, i * ts, True)
        dpv = lead - dd
        dh_ref[...] = jnp.concatenate([dcb, dp * cv, dp * cc, dpv], axis=1).astype(BF16)

    return pl.pallas_call(
        body, name=name, grid=(n,),
        in_specs=[pl.BlockSpec((ts, 1024), lambda i: (i, 0)),
                  pl.BlockSpec((16, 1024), lambda i: (jnp.maximum(i * (ts // 16) - 1, 0), 0)),
                  pl.BlockSpec((16, 1024), lambda i: (jnp.minimum((i + 1) * (ts // 16), last16), 0)),
                  pl.BlockSpec((ts, 512), lambda i: (i, 1)),
                  pl.BlockSpec((16, 512), lambda i: (jnp.minimum((i + 1) * (ts // 16), last16), 1)),
                  pl.BlockSpec((3, 256), lambda i: (0, 0)), pl.BlockSpec((256, 256), lambda i: (0, 0)),
                  pl.BlockSpec((1, 256), lambda i: (0, 0))],
        out_specs=[pl.BlockSpec((ts, 1024), lambda i: (i, 0)), pl.BlockSpec((3, 256), lambda i: (0, 0)),
                   pl.BlockSpec((256, 256), lambda i: (0, 0)), pl.BlockSpec((1, 256), lambda i: (0, 0))],
        out_shape=[jax.ShapeDtypeStruct((L, 1024), BF16), jax.ShapeDtypeStruct((3, 256), F32),
                   jax.ShapeDtypeStruct((256, 256), F32), jax.ShapeDtypeStruct((1, 256), F32)],
        compiler_params=_params(("arbitrary",)),
    )(hc, hc, hc, dcat, dcat, wc, wblk, pscale)


def _lower_bound(lb_ref, layer):
    b0, b1 = lb_ref[0:1, :], lb_ref[1:2, :]
    m = jnp.maximum(b0, b1)
    e0, e1 = jnp.exp(b0 - m), jnp.exp(b1 - m)
    p0, p1 = e0 / (e0 + e1), e1 / (e0 + e1)
    lb = (p0 - p0) if layer == 0 else ((p0 + p1) - p0)
    return lb, p0, p1


def _cumsum_rows(x, reverse=False):
    row = lax.broadcasted_iota(jnp.int32, x.shape, 0)
    for sh in (1, 2, 4, 8):
        if reverse:
            x = x + jnp.where(row < SUB - sh, pltpu.roll(x, SUB - sh, 0), 0.0)
        else:
            x = x + jnp.where(row >= sh, pltpu.roll(x, sh, 0), 0.0)
    return x


def _gates(fz, lb):
    sig = _sigmoid(fz)
    f = lb + (1.0 - lb) * sig
    g = jnp.log(jnp.maximum(f, F_FLOOR))
    k = (1.0 - lb) * (1.0 - sig)
    return sig, f, g, k


def _head(h):
    return slice(h * HG_D, (h + 1) * HG_D)


def _hgrn_fwd(hh, lbp, gnorm, layer, name):
    L = hh.shape[0]
    ts = SEQ_TILE
    n = L // ts
    nsub = ts // SUB

    def body(q_ref, f_ref, i_ref, g_ref, lb_ref, gn_ref, y_ref, o_ref, s_ref, St):
        @pl.when(pl.program_id(0) == 0)
        def _():
            St[...] = jnp.zeros_like(St)

        lb, _, _ = _lower_bound(lb_ref, layer)
        gn = jnp.tile(gn_ref[...], (1, HG_HEADS))
        r16 = lax.broadcasted_iota(jnp.int32, (SUB, SUB), 0)
        c16 = lax.broadcasted_iota(jnp.int32, (SUB, SUB), 1)

        def block(j, carry):
            rows = pl.ds(pl.multiple_of(j * SUB, SUB), SUB)
            q = q_ref[rows, :] * Q_SCALE
            iv = i_ref[rows, :]
            gz = g_ref[rows, :]
            _, _, g, k = _gates(f_ref[rows, :], lb)
            G = _cumsum_rows(g)
            Gl = G[SUB - 1:SUB, :]
            qt = (q * jnp.exp(G)).astype(BF16)
            kd = (k * jnp.exp(Gl - G)).astype(BF16)
            eGl = jnp.exp(Gl)
            ib = iv.astype(BF16)
            A = [jnp.zeros((SUB, SUB), F32) for _ in range(HG_HEADS)]
            for s in range(SUB):
                P = q * jnp.exp(jnp.minimum(G - G[s:s + 1, :], 0.0)) * k[s:s + 1, :]
                for h in range(HG_HEADS):
                    A[h] = jnp.where(c16 == s, jnp.sum(P[:, _head(h)], axis=-1, keepdims=True), A[h])
            outs, ons = [], []
            for h in range(HG_HEADS):
                sl = _head(h)
                Sb = St[h].astype(BF16)
                s_ref[j, sl, :] = Sb
                Am = jnp.where(r16 >= c16, A[h], 0.0).astype(BF16)
                o = _dnt(qt[:, sl], Sb) + _dnn(Am, ib[:, sl])
                St[h] = eGl[:, sl] * St[h] + _dtn(ib[:, sl], kd[:, sl])
                outs.append(o)
                ons.append(o * lax.rsqrt(jnp.mean(o * o, axis=-1, keepdims=True) + RMS_EPS))
            o_ref[rows, :] = jnp.concatenate(outs, axis=1)
            y = jnp.concatenate(ons, axis=1) * gn * (gz * _sigmoid(gz))
            y_ref[rows, :] = y.astype(BF16)
            return carry

        lax.fori_loop(0, nsub, block, 0, unroll=2)

    col = lambda c: pl.BlockSpec((ts, HG_W), lambda i: (i, c))
    return pl.pallas_call(
        body, name=name, grid=(n,),
        in_specs=[col(0), col(1), col(2), col(3), pl.BlockSpec((2, HG_W), lambda i: (0, 0)),
                  pl.BlockSpec((1, HG_D), lambda i: (0, 0))],
        out_specs=[pl.BlockSpec((ts, HG_W), lambda i: (i, 0)), pl.BlockSpec((ts, HG_W), lambda i: (i, 0)),
                   pl.BlockSpec((nsub, HG_W, HG_D), lambda i: (i, 0, 0))],
        out_shape=[jax.ShapeDtypeStruct((L, HG_W), BF16), jax.ShapeDtypeStruct((L, HG_W), F32),
                   jax.ShapeDtypeStruct((L // SUB, HG_W, HG_D), BF16)],
        scratch_shapes=[pltpu.VMEM((HG_HEADS, HG_D, HG_D), F32)],
        compiler_params=_params(("arbitrary",)),
    )(hh, hh, hh, hh, lbp, gnorm)


def _hgrn_bwd(hh, o_raw, states, dcat, lbp, gnorm, layer, name):
    L = hh.shape[0]
    ts = SEQ_TILE
    n = L // ts
    nsub = ts // SUB

    def body(q_ref, f_ref, i_ref, g_ref, o_ref, s_ref, dy_ref, lb_ref, gn_ref,
             dh_ref, dlb_ref, dgn_ref, dSt, dlb_acc):
        step = pl.program_id(0)

        @pl.when(step == 0)
        def _():
            dSt[...] = jnp.zeros_like(dSt)
            dlb_acc[...] = jnp.zeros_like(dlb_acc)
            dgn_ref[...] = jnp.zeros_like(dgn_ref)

        lb, p0, p1 = _lower_bound(lb_ref, layer)
        gnh = gn_ref[...]
        gn = jnp.tile(gnh, (1, HG_HEADS))
        r16 = lax.broadcasted_iota(jnp.int32, (SUB, SUB), 0)
        c16 = lax.broadcasted_iota(jnp.int32, (SUB, SUB), 1)

        def block(jj, carry):
            j = nsub - 1 - jj
            rows = pl.ds(pl.multiple_of(j * SUB, SUB), SUB)
            q = q_ref[rows, :] * Q_SCALE
            iv = i_ref[rows, :]
            gz = g_ref[rows, :]
            o = o_ref[rows, :]
            dy = dy_ref[rows, :]
            sig, f, g, k = _gates(f_ref[rows, :], lb)
            G = _cumsum_rows(g)
            Gl = G[SUB - 1:SUB, :]
            eG = jnp.exp(G)
            edl = jnp.exp(Gl - G)
            eGl = jnp.exp(Gl)
            qt = (q * eG).astype(BF16)
            kd = (k * edl).astype(BF16)
            ib = iv.astype(BF16)
            sgz = _sigmoid(gz)
            sil = gz * sgz
            dyn = dy * sil
            on_parts, do_parts = [], []
            dgn = jnp.zeros((1, HG_D), F32)
            for h in range(HG_HEADS):
                sl = _head(h)
                oh = o[:, sl]
                rs = lax.rsqrt(jnp.mean(oh * oh, axis=-1, keepdims=True) + RMS_EPS)
                on = oh * rs
                dgn = dgn + jnp.sum(dyn[:, sl] * on, axis=0, keepdims=True)
                don = dyn[:, sl] * gnh
                do_parts.append(rs * (don - on * jnp.mean(don * on, axis=-1, keepdims=True)))
                on_parts.append(on)
            dgn_ref[...] += dgn
            on_all = jnp.concatenate(on_parts, axis=1)
            dgz = dy * on_all * gn * (sgz * (1.0 + gz * (1.0 - sgz)))
            do = jnp.concatenate(do_parts, axis=1)
            dob = do.astype(BF16)
            Ap = [jnp.where(r16 >= c16, _dnt(dob[:, _head(h)], ib[:, _head(h)]), 0.0) for h in range(HG_HEADS)]
            A = [jnp.zeros((SUB, SUB), F32) for _ in range(HG_HEADS)]
            dq = jnp.zeros((SUB, HG_W), F32)
            dk_rows = []
            for s in range(SUB):
                E = jnp.exp(jnp.minimum(G - G[s:s + 1, :], 0.0))
                P = q * E * k[s:s + 1, :]
                for h in range(HG_HEADS):
                    A[h] = jnp.where(c16 == s, jnp.sum(P[:, _head(h)], axis=-1, keepdims=True), A[h])
                t1 = jnp.concatenate([Ap[h][:, s:s + 1] * E[:, _head(h)] for h in range(HG_HEADS)], axis=1)
                dq = dq + t1 * k[s:s + 1, :]
                dk_rows.append(jnp.sum(t1 * q, axis=0, keepdims=True))
            dk = jnp.concatenate(dk_rows, axis=0)
            dq_p, dk_p, di_p, tail_p = [], [], [], []
            for h in range(HG_HEADS):
                sl = _head(h)
                Sb = s_ref[j, sl, :]
                dSb = dSt[h].astype(BF16)
                Am = jnp.where(r16 >= c16, A[h], 0.0).astype(BF16)
                dq_p.append(eG[:, sl] * _dnn(dob[:, sl], Sb))
                dk_p.append(edl[:, sl] * _dnn(ib[:, sl], dSb))
                di_p.append(_dtn(Am, dob[:, sl]) + _dnt(kd[:, sl], dSb))
                St_end = eGl[:, sl] * Sb.astype(F32) + _dtn(ib[:, sl], kd[:, sl])
                tail_p.append(jnp.sum(dSt[h] * St_end, axis=0, keepdims=True))
                dSt[h] = eGl[:, sl] * dSt[h] + _dtn(dob[:, sl], qt[:, sl])
            dq = dq + jnp.concatenate(dq_p, axis=1)
            dk = dk + jnp.concatenate(dk_p, axis=1)
            di = jnp.concatenate(di_p, axis=1)
            dg = _cumsum_rows(q * dq - k * dk, reverse=True) + jnp.concatenate(tail_p, axis=1)
            df = jnp.where(f > F_FLOOR, dg / f, 0.0)
            dfk = df - dk
            dfz = (1.0 - lb) * dfk * sig * (1.0 - sig)
            dlb_acc[...] += jnp.sum(dfk * (1.0 - sig), axis=0, keepdims=True)
            dh_ref[rows, :] = jnp.concatenate([dq * Q_SCALE, dfz, di, dgz], axis=1).astype(BF16)
            return carry

        lax.fori_loop(0, nsub, block, 0)

        @pl.when(step == n - 1)
        def _():
            if layer == 0:
                dlb_ref[...] = jnp.zeros_like(dlb_ref)
            else:
                dz1 = p0 * p1 * dlb_acc[...]
                dlb_ref[...] = jnp.concatenate([-dz1, dz1], axis=0)

    rev = lambda i: n - 1 - i
    col = lambda c: pl.BlockSpec((ts, HG_W), lambda i: (rev(i), c))
    return pl.pallas_call(
        body, name=name, grid=(n,),
        in_specs=[col(0), col(1), col(2), col(3), col(0),
                  pl.BlockSpec((nsub, HG_W, HG_D), lambda i: (rev(i), 0, 0)), col(0),
                  pl.BlockSpec((2, HG_W), lambda i: (0, 0)), pl.BlockSpec((1, HG_D), lambda i: (0, 0))],
        out_specs=[pl.BlockSpec((ts, 4 * HG_W), lambda i: (rev(i), 0)),
                   pl.BlockSpec((2, HG_W), lambda i: (0, 0)), pl.BlockSpec((1, HG_D), lambda i: (0, 0))],
        out_shape=[jax.ShapeDtypeStruct((L, 4 * HG_W), BF16), jax.ShapeDtypeStruct((2, HG_W), F32),
                   jax.ShapeDtypeStruct((1, HG_D), F32)],
        scratch_shapes=[pltpu.VMEM((HG_HEADS, HG_D, HG_D), F32), pltpu.VMEM((1, HG_W), F32)],
        compiler_params=_params(("arbitrary",)),
    )(hh, hh, hh, hh, o_raw, states, dcat, lbp, gnorm)


def _adamw_body(gp_ref, w_ref, m_ref, v_ref, g_ref, d_ref, mo_ref, vo_ref):
    c1 = 1.0 - ADAM_B1 ** ADAM_STEP
    c2 = 1.0 - ADAM_B2 ** ADAM_STEP
    g = gp_ref[0].astype(F32)
    for k in range(1, N_DEV):
        g = g + gp_ref[k].astype(F32)
    mn = ADAM_B1 * m_ref[...] + (1.0 - ADAM_B1) * g
    vn = ADAM_B2 * v_ref[...] + (1.0 - ADAM_B2) * (g * g)
    m_hat = mn / c1
    v_hat = vn / c2
    g_ref[...] = g
    d_ref[...] = -ADAM_LR * (m_hat / (jnp.sqrt(v_hat) + ADAM_EPS) + ADAM_WD * w_ref[...])
    mo_ref[...] = mn
    vo_ref[...] = vn


def _adamw_layers(gparts, w, m, v, name):
    _, depth, R, C = gparts.shape
    tr = _tile(R, 256, 16)

    def body(*refs):
        _adamw_body(*refs)

    blk = pl.BlockSpec((None, tr, C), lambda l, i: (l, i, 0))
    shp = jax.ShapeDtypeStruct((depth, R, C), F32)
    return pl.pallas_call(
        body, name=name, grid=(depth, R // tr),
        in_specs=[pl.BlockSpec((N_DEV, None, tr, C), lambda l, i: (0, l, i, 0)), blk, blk, blk],
        out_specs=[blk, blk, blk, blk], out_shape=[shp, shp, shp, shp],
        compiler_params=_params(("parallel", "parallel")),
    )(gparts, w, m, v)


def _adamw(gparts, w, m, v, name):
    R = w.shape[0]
    tr = _tile(R, 1024, 16) if R % 16 == 0 else R

    def body(*refs):
        _adamw_body(*refs)

    row = pl.BlockSpec((tr, LANES), lambda i: (i, 0))
    shp = jax.ShapeDtypeStruct((R, LANES), F32)
    return pl.pallas_call(
        body, name=name, grid=(R // tr,),
        in_specs=[pl.BlockSpec((N_DEV, tr, LANES), lambda i: (0, i, 0)), row, row, row],
        out_specs=[row, row, row, row], out_shape=[shp, shp, shp, shp],
        compiler_params=_params(("parallel",)),
    )(gparts, w, m, v)


def _flip(coord, bit):
    return 1 - coord if bit else coord


def _gather_many(blocks, name):
    n = len(blocks)

    def body(*refs):
        x_refs, out_refs = refs[:n], refs[n:2 * n]
        send_sems, recv_sems, local_sems = refs[2 * n:]
        x, y, c = lax.axis_index("x"), lax.axis_index("y"), lax.axis_index("c")
        me, sibling = (x, y, c), (x, y, 1 - c)
        chips = [(1 - x, y), (x, 1 - y), (1 - x, 1 - y)]

        def slot(a, px, py, pc):
            return out_refs[a].at[4 * px + 2 * py + pc]

        def copy(a, k, blk, to, src=None):
            return pltpu.make_async_remote_copy(
                src_ref=slot(a, *blk) if src is None else src, dst_ref=slot(a, *blk),
                send_sem=send_sems.at[7 * a + k], recv_sem=recv_sems.at[7 * a + k],
                device_id=to, device_id_type=pl.DeviceIdType.MESH)

        mine = [pltpu.make_async_copy(x_refs[a], slot(a, *me), local_sems.at[a]) for a in range(n)]
        for cp in mine:
            cp.start()
        first = [copy(a, 0, me, sibling, src=x_refs[a]) for a in range(n)]
        for j, chip in enumerate(chips):
            first += [copy(a, 1 + j, me, (*chip, c), src=x_refs[a]) for a in range(n)]
        for cp in first:
            cp.start()
        passed = []
        for j, chip in enumerate(chips):
            for a in range(n):
                copy(a, 1 + j, (*chip, c), me).wait_recv()
                fwd = copy(a, 4 + j, (*chip, c), sibling)
                fwd.start()
                passed.append(fwd)
        for a in range(n):
            copy(a, 0, sibling, me).wait_recv()
        for j, chip in enumerate(chips):
            for a in range(n):
                copy(a, 4 + j, (*chip, 1 - c), me).wait_recv()
        for cp in first + passed:
            cp.wait_send()
        for cp in mine:
            cp.wait()

    hbm = pl.BlockSpec(memory_space=pl.ANY)
    return pl.pallas_call(
        body, name=name,
        out_shape=[jax.ShapeDtypeStruct((N_DEV,) + b.shape, b.dtype) for b in blocks],
        in_specs=[hbm] * n, out_specs=[hbm] * n,
        scratch_shapes=[pltpu.SemaphoreType.DMA((7 * n,)), pltpu.SemaphoreType.DMA((7 * n,)),
                        pltpu.SemaphoreType.DMA((n,))],
    )(*blocks)


def _exchange_grads(layer_chunks, small_chunks, rep_block, name):
    flows, inputs = [], []
    for p, per_layer in enumerate(layer_chunks):
        for l, arr in enumerate(per_layer):
            flows.append(("param", p, l))
            inputs.append(arr)
    flows += [("small",), ("rep",)]
    inputs += [small_chunks, rep_block]
    n_par = len(layer_chunks)
    n_in, n_out, nf = len(inputs), n_par + 2, len(flows)

    def body(*refs):
        in_refs, out_refs = refs[:n_in], refs[n_in:n_in + n_out]
        send_sems, recv_sems, local_sems = refs[n_in + n_out:]
        x, y, c = lax.axis_index("x"), lax.axis_index("y"), lax.axis_index("c")
        me = 4 * x + 2 * y + c

        def src(f, dev):
            return in_refs[f] if flows[f][0] == "rep" else in_refs[f].at[dev]

        def dst(f, dev):
            if flows[f][0] == "param":
                _, p, l = flows[f]
                return out_refs[p].at[dev, l]
            return out_refs[n_par + (0 if flows[f][0] == "small" else 1)].at[dev]

        mine = [pltpu.make_async_copy(src(f, me), dst(f, me), local_sems.at[f]) for f in range(nf)]
        for cp in mine:
            cp.start()
        copies = []
        for k in range(1, N_DEV):
            px, py, pc = _flip(x, k & 4), _flip(y, k & 2), _flip(c, k & 1)
            peer = 4 * px + 2 * py + pc
            for f in range(nf):
                sems = dict(send_sem=send_sems.at[7 * f + k - 1], recv_sem=recv_sems.at[7 * f + k - 1],
                            device_id=(px, py, pc), device_id_type=pl.DeviceIdType.MESH)
                send = pltpu.make_async_remote_copy(src_ref=src(f, peer), dst_ref=dst(f, me), **sems)
                recv = pltpu.make_async_remote_copy(src_ref=src(f, peer), dst_ref=dst(f, peer), **sems)
                send.start()
                copies.append((send, recv))
        for send, recv in copies:
            recv.wait_recv()
        for send, recv in copies:
            send.wait_send()
        for cp in mine:
            cp.wait()

    out_shape = [jax.ShapeDtypeStruct((N_DEV, len(pl_)) + pl_[0].shape[1:], pl_[0].dtype) for pl_ in layer_chunks]
    out_shape += [jax.ShapeDtypeStruct(small_chunks.shape, small_chunks.dtype),
                  jax.ShapeDtypeStruct((N_DEV,) + rep_block.shape, rep_block.dtype)]
    hbm = pl.BlockSpec(memory_space=pl.ANY)
    return pl.pallas_call(
        body, name=name, out_shape=out_shape,
        in_specs=[hbm] * n_in, out_specs=[hbm] * n_out,
        scratch_shapes=[pltpu.SemaphoreType.DMA((7 * nf,)), pltpu.SemaphoreType.DMA((7 * nf,)),
                        pltpu.SemaphoreType.DMA((nf,))],
    )(*inputs)


def _pack(arrs, dtype, row_mult=8):
    parts, offs, r = [], [], 0
    for a in arrs:
        flat = a.astype(dtype).reshape(-1)
        nrow = -(-flat.shape[0] // LANES)
        flat = jnp.pad(flat, (0, nrow * LANES - flat.shape[0]))
        parts.append(flat.reshape(nrow, LANES))
        offs.append((r, nrow))
        r += nrow
    pad = (-r) % row_mult
    if pad:
        parts.append(jnp.zeros((pad, LANES), dtype))
    return jnp.concatenate(parts, axis=0), offs


def _unpack(buf, offs, shapes, lead=()):
    outs = []
    for (r, nrow), shp in zip(offs, shapes):
        size = 1
        for s in shp:
            size *= s
        flat = buf[..., r:r + nrow, :].reshape(lead + (nrow * LANES,))
        outs.append(flat[..., :size].reshape(lead + tuple(shp)))
    return outs


def _cols_from_shards(g, axis):
    return jnp.concatenate([g[j] for j in range(N_DEV)], axis=axis)


BIG = ("w_in", "w_o", "w_up", "w_down")
SMALL_SHARDED = ("meta_tokens", "w_conv", "w_ffn_conv")
REPLICATED = ("hg_lower_bounds", "w_pool", "pool_scale", "hg_norm_g", "ln1_g", "ln1_b", "b_ffn_conv", "ln2_g", "ln2_b")
WEIGHTS = ("meta_tokens", "hg_lower_bounds", "w_in", "w_conv", "w_pool", "pool_scale", "hg_norm_g", "w_o",
           "ln1_g", "ln1_b", "w_up", "w_ffn_conv", "b_ffn_conv", "w_down", "ln2_g", "ln2_b")


def _pool_blockdiag(w_pool_l):
    z = jnp.zeros((POOL_GROUP, POOL_GROUP), w_pool_l.dtype)
    rows = [jnp.concatenate([w_pool_l[g] if h == g else z for h in range(4)], axis=1) for g in range(4)]
    return jnp.concatenate(rows, axis=0)


def _layer_weights(full, l):
    w_in = full["w_in"][l]
    w_o = full["w_o"][l]
    return dict(
        w_hg=w_in[:, 768:2816],
        w_cp=jnp.concatenate([w_in[:, 0:768], w_in[:, 2816:3072]], axis=1),
        w_o=jnp.concatenate([w_o[256:768], w_o[0:256], w_o[768:1024]], axis=0),
        w_up=full["w_up"][l], w_down=full["w_down"][l],
    )


def kernel(x, meta_tokens, hg_lower_bounds, w_in, w_conv, w_pool, pool_scale, hg_norm_g, w_o, ln1_g, ln1_b, w_up, w_ffn_conv, b_ffn_conv, w_down, ln2_g, ln2_b, loss_target, m_meta_tokens, m_hg_lower_bounds, m_w_in, m_w_conv, m_w_pool, m_pool_scale, m_hg_norm_g, m_w_o, m_ln1_g, m_ln1_b, m_w_up, m_w_ffn_conv, m_b_ffn_conv, m_w_down, m_ln2_g, m_ln2_b, v_meta_tokens, v_hg_lower_bounds, v_w_in, v_w_conv, v_w_pool, v_pool_scale, v_hg_norm_g, v_w_o, v_ln1_g, v_ln1_b, v_w_up, v_w_ffn_conv, v_b_ffn_conv, v_w_down, v_ln2_g, v_ln2_b):
    W = dict(meta_tokens=meta_tokens, hg_lower_bounds=hg_lower_bounds, w_in=w_in, w_conv=w_conv, w_pool=w_pool,
             pool_scale=pool_scale, hg_norm_g=hg_norm_g, w_o=w_o, ln1_g=ln1_g, ln1_b=ln1_b, w_up=w_up,
             w_ffn_conv=w_ffn_conv, b_ffn_conv=b_ffn_conv, w_down=w_down, ln2_g=ln2_g, ln2_b=ln2_b)
    M = dict(meta_tokens=m_meta_tokens, hg_lower_bounds=m_hg_lower_bounds, w_in=m_w_in, w_conv=m_w_conv,
             w_pool=m_w_pool, pool_scale=m_pool_scale, hg_norm_g=m_hg_norm_g, w_o=m_w_o, ln1_g=m_ln1_g,
             ln1_b=m_ln1_b, w_up=m_w_up, w_ffn_conv=m_w_ffn_conv, b_ffn_conv=m_b_ffn_conv, w_down=m_w_down,
             ln2_g=m_ln2_g, ln2_b=m_ln2_b)
    V = dict(meta_tokens=v_meta_tokens, hg_lower_bounds=v_hg_lower_bounds, w_in=v_w_in, w_conv=v_w_conv,
             w_pool=v_w_pool, pool_scale=v_pool_scale, hg_norm_g=v_hg_norm_g, w_o=v_w_o, ln1_g=v_ln1_g,
             ln1_b=v_ln1_b, w_up=v_w_up, w_ffn_conv=v_w_ffn_conv, b_ffn_conv=v_b_ffn_conv, w_down=v_w_down,
             ln2_g=v_ln2_g, ln2_b=v_ln2_b)
    assert x.shape[0] == 1 and x.shape[2] == D_MODEL and w_in.shape[0] == DEPTH
    seq = x.shape[1]
    L = -(-(seq + N_META) // ROW_ALIGN) * ROW_ALIGN

    small_pack, small_offs = _pack([W[n] for n in SMALL_SHARDED], F32)
    g_in, g_o, g_up, g_down, small_all = _gather_many([W[n].astype(BF16) for n in BIG] + [small_pack], "gather_weights")
    full = dict(
        w_in=jnp.transpose(g_in, (1, 2, 0, 3)).reshape(DEPTH, D_MODEL, -1),
        w_o=jnp.transpose(g_o, (1, 0, 2, 3)).reshape(DEPTH, -1, D_MODEL),
        w_up=jnp.transpose(g_up, (1, 2, 0, 3)).reshape(DEPTH, D_MODEL, -1),
        w_down=jnp.transpose(g_down, (1, 0, 2, 3)).reshape(DEPTH, -1, D_MODEL))
    for n, a in zip(SMALL_SHARDED, _unpack(small_all, small_offs, [W[n].shape for n in SMALL_SHARDED], (N_DEV,))):
        full[n] = _cols_from_shards(a, 1)

    pad_rows = L - N_META - seq
    xp = jnp.concatenate([full["meta_tokens"], x[0], jnp.zeros((pad_rows, D_MODEL), F32)], axis=0)
    tgt = jnp.concatenate([jnp.zeros((N_META, D_MODEL), F32), loss_target[0], jnp.zeros((pad_rows, D_MODEL), F32)], axis=0)

    saved = []
    h_in, h_in_b = xp, xp.astype(BF16)
    for l in range(DEPTH):
        lw = _layer_weights(full, l)
        wc = full["w_conv"][l].T
        wblk = _pool_blockdiag(w_pool[l]).astype(BF16)
        ps = pool_scale[l][None, :]
        gn = hg_norm_g[l][None, :]
        wf = full["w_ffn_conv"][l].T
        bf = b_ffn_conv[l][None, :]
        hh = _matmul(h_in_b, lw["w_hg"], "nn", F32, f"fwd_hg_{l}")
        hc = _matmul(h_in_b, lw["w_cp"], "nn", F32, f"fwd_cp_{l}")
        y_hg, o_raw, states = _hgrn_fwd(hh, hg_lower_bounds, gn, l, f"hgrn_fwd_{l}")
        y_cp = _cp_fwd(hc, wc, wblk, ps, f"convpool_fwd_{l}")
        cat = jnp.concatenate([y_hg, y_cp], axis=1)
        mix = _matmul(cat, lw["w_o"], "nn", F32, f"fwd_o_{l}")
        x1, x1_b = _ln_fwd(h_in, mix, ln1_g[l][None, :], ln1_b[l][None, :], f"ln1_fwd_{l}")
        up = _matmul(x1_b, lw["w_up"], "nn", BF16, f"fwd_up_{l}")
        a = _ffn_act_fwd(up, wf, bf, f"ffn_fwd_{l}")
        ffn = _matmul(a, lw["w_down"], "nn", F32, f"fwd_down_{l}")
        x2, x2_b = _ln_fwd(x1, ffn, ln2_g[l][None, :], ln2_b[l][None, :], f"ln2_fwd_{l}")
        saved.append(dict(lw=lw, wc=wc, wblk=wblk, ps=ps, gn=gn, wf=wf, bf=bf, x=h_in, x_b=h_in_b, hh=hh, hc=hc,
                          o_raw=o_raw, states=states, cat=cat, mix=mix, x1=x1, x1_b=x1_b, up=up, a=a, ffn=ffn))
        h_in, h_in_b = x2, x2_b

    dy, loss_part = _loss_head(h_in, tgt, seq)
    loss = lax.psum(loss_part[0, 0], ("x", "y", "c"))

    G = {}
    per_layer = {n: [None] * DEPTH for n in ("w_conv", "w_pool", "pool_scale", "hg_norm_g", "ln1_g", "ln1_b",
                                             "w_ffn_conv", "b_ffn_conv", "ln2_g", "ln2_b")}
    chunks = {n: [None] * DEPTH for n in BIG}
    dlb_total = jnp.zeros((DEPTH, HG_W), F32)
    for l in reversed(range(DEPTH)):
        s = saved[l]
        lw = s["lw"]
        dz2, dz2_b, dg2, db2 = _ln_bwd(s["x1"], s["ffn"], dy, ln2_g[l][None, :], f"ln2_bwd_{l}")
        da = _matmul(dz2_b, lw["w_down"], "nt", BF16, f"bwd_da_{l}")
        d_w_down = _matmul(s["a"], dz2_b, "tn", BF16, f"wgrad_down_{l}")
        dup, dwf, dbf = _ffn_act_bwd(s["up"], da, s["wf"], s["bf"], f"ffn_bwd_{l}")
        dx1 = _matmul(dup, lw["w_up"], "nt", F32, f"bwd_dx1_{l}", res=dz2, alpha=ALPHA)
        d_w_up = _matmul(s["x1_b"], dup, "tn", BF16, f"wgrad_up_{l}")
        dz1, dz1_b, dg1, db1 = _ln_bwd(s["x"], s["mix"], dx1, ln1_g[l][None, :], f"ln1_bwd_{l}")
        dcat = _matmul(dz1_b, lw["w_o"], "nt", F32, f"bwd_dcat_{l}")
        d_w_o = _matmul(s["cat"], dz1_b, "tn", BF16, f"wgrad_o_{l}")
        dhh, dlb, dgn = _hgrn_bwd(s["hh"], s["o_raw"], s["states"], dcat, hg_lower_bounds, s["gn"], l, f"hgrn_bwd_{l}")
        dhc, dwc, dwblk, dps = _cp_bwd(s["hc"], dcat, s["wc"], s["wblk"], s["ps"], f"convpool_bwd_{l}")
        dx_a = _matmul(dhh, lw["w_hg"], "nt", F32, f"bwd_dx_hg_{l}", res=dz1, alpha=ALPHA)
        dx = _matmul(dhc, lw["w_cp"], "nt", F32, f"bwd_dx_cp_{l}", res=dx_a, alpha=1.0)
        d_w_hg = _matmul(s["x_b"], dhh, "tn", BF16, f"wgrad_hg_{l}")
        d_w_cp = _matmul(s["x_b"], dhc, "tn", BF16, f"wgrad_cp_{l}")
        d_w_in = jnp.concatenate([d_w_cp[:, 0:768], d_w_hg, d_w_cp[:, 768:1024]], axis=1)
        chunks["w_in"][l] = jnp.transpose(d_w_in.reshape(D_MODEL, N_DEV, -1), (1, 0, 2))
        chunks["w_o"][l] = jnp.concatenate([d_w_o[512:768], d_w_o[0:512], d_w_o[768:1024]], axis=0).reshape(N_DEV, -1, D_MODEL)
        chunks["w_up"][l] = jnp.transpose(d_w_up.reshape(D_MODEL, N_DEV, -1), (1, 0, 2))
        chunks["w_down"][l] = d_w_down.reshape(N_DEV, -1, D_MODEL)
        per_layer["w_conv"][l] = dwc.T
        per_layer["w_ffn_conv"][l] = dwf.T
        per_layer["b_ffn_conv"][l] = dbf[0]
        per_layer["w_pool"][l] = jnp.stack([dwblk[g * 64:(g + 1) * 64, g * 64:(g + 1) * 64] for g in range(4)], axis=0)
        per_layer["pool_scale"][l] = dps[0]
        per_layer["hg_norm_g"][l] = dgn[0]
        per_layer["ln1_g"][l], per_layer["ln1_b"][l] = dg1[0], db1[0]
        per_layer["ln2_g"][l], per_layer["ln2_b"][l] = dg2[0], db2[0]
        dlb_total = dlb_total + dlb
        dy = dx
    for n, parts in per_layer.items():
        G[n] = jnp.stack(parts, axis=0)
    G["hg_lower_bounds"] = dlb_total
    grad_x = dy[N_META:N_META + seq][None]

    def shard_major(g, lead):
        g = g.reshape(g.shape[:lead] + (N_DEV, -1) + g.shape[lead + 1:])
        g = jnp.moveaxis(g, lead, 0).reshape(N_DEV, -1)
        return jnp.pad(g, ((0, 0), (0, (-g.shape[1]) % LANES)))

    small_flat = jnp.concatenate([shard_major(dy[0:N_META], 1), shard_major(G["w_conv"], 1),
                                  shard_major(G["w_ffn_conv"], 1)], axis=1)
    w_small, _ = _pack([W[n] for n in SMALL_SHARDED], F32)
    small_rows = w_small.shape[0]
    small_chunks = jnp.pad(small_flat.reshape(N_DEV, -1, LANES),
                           ((0, 0), (0, small_rows - small_flat.shape[1] // LANES), (0, 0)))
    rep_pack, rep_offs = _pack([G[n] for n in REPLICATED], F32)
    received = _exchange_grads([chunks[n] for n in BIG], small_chunks, rep_pack, "exchange_grads")
    small_recv, rep_all = received[len(BIG)], received[len(BIG) + 1]

    res = {k: {} for k in ("grad", "delta", "new_m", "new_v")}
    kinds = ("grad", "delta", "new_m", "new_v")
    for n, gp in zip(BIG, received[:len(BIG)]):
        for kind, a in zip(kinds, _adamw_layers(gp, W[n], M[n], V[n], f"adamw_{n}")):
            res[kind][n] = a
    m_small, _ = _pack([M[n] for n in SMALL_SHARDED], F32)
    v_small, _ = _pack([V[n] for n in SMALL_SHARDED], F32)
    outs_small = _adamw(small_recv, w_small, m_small, v_small, "adamw_small_sharded")
    w_rep, _ = _pack([W[n] for n in REPLICATED], F32)
    m_rep, _ = _pack([M[n] for n in REPLICATED], F32)
    v_rep, _ = _pack([V[n] for n in REPLICATED], F32)
    outs_rep = _adamw(rep_all, w_rep, m_rep, v_rep, "adamw_replicated")
    for kind, b_sm, b_rep in zip(kinds, outs_small, outs_rep):
        for n, a in zip(SMALL_SHARDED, _unpack(b_sm, small_offs, [W[n].shape for n in SMALL_SHARDED])):
            res[kind][n] = a
        for n, a in zip(REPLICATED, _unpack(b_rep, rep_offs, [W[n].shape for n in REPLICATED])):
            res[kind][n] = a

    return (loss, grad_x, *[res["grad"][n] for n in WEIGHTS], *[res["delta"][n] for n in WEIGHTS],
            *[res["new_m"][n] for n in WEIGHTS], *[res["new_v"][n] for n in WEIGHTS])
```

```python
import jax
import jax.numpy as jnp
from jax import lax
from jax.experimental import pallas as pl
from jax.experimental.pallas import tpu as pltpu

F32 = jnp.float32
BF16 = jnp.bfloat16

N_DEV = 8
D_MODEL = 1024
N_META = 16
DEPTH = 2
CONV_W = 256
HG_W = 512
HG_D = 128
HG_HEADS = 4
POOL_W = 256
POOL_GROUP = 64
D_FF = 2816
ALPHA = (2 * DEPTH) ** 0.25
LN_EPS = 1e-5
RMS_EPS = 1e-6
F_FLOOR = 1e-30
Q_SCALE = HG_D ** -0.5
SUB = 16
SEQ_TILE = 192
FFN_TILE = 96
ROW_ALIGN = 192
LANES = 128
VMEM_LIMIT = 48 * 1024 * 1024
MATMUL_VMEM_BUDGET = 38 * 1024 * 1024

ADAM_LR = 0.001
ADAM_B1 = 0.9
ADAM_B2 = 0.999
ADAM_EPS = 1e-08
ADAM_WD = 0.01
ADAM_STEP = 10


def _tile(n, cap, mult):
    best = 0
    for t in range(mult, min(n, cap) + 1, mult):
        if n % t == 0:
            best = t
    assert best > 0, (n, cap, mult)
    return best


def _params(sem, vmem=VMEM_LIMIT):
    return pltpu.CompilerParams(dimension_semantics=sem, vmem_limit_bytes=vmem)


def _dnt(a, b):
    return lax.dot_general(a, b, (((1,), (1,)), ((), ())), preferred_element_type=F32)


def _dtn(a, b):
    return lax.dot_general(a, b, (((0,), (0,)), ((), ())), preferred_element_type=F32)


def _dnn(a, b):
    return jnp.dot(a, b, preferred_element_type=F32)


def _sigmoid(x):
    return 1.0 / (1.0 + jnp.exp(-x))


def _matmul(a, b, mode, out_dtype, name, res=None, alpha=1.0):
    if mode == "tn":
        K, M = a.shape
    else:
        M, K = a.shape
    N = b.shape[0] if mode == "nt" else b.shape[1]
    out_bytes = jnp.dtype(out_dtype).itemsize
    tn = _tile(N, 1536, LANES)
    tk = _tile(K, 1536, 16) if mode == "tn" else _tile(K, 2816, LANES)
    nk = K // tk
    use_acc = nk > 1 and out_dtype != F32
    tm = M
    for cap in (1536, 768, 384):
        tm = _tile(M, cap, 16)
        blocks = 2 * (a.dtype.itemsize * tm * tk + b.dtype.itemsize * tn * tk + out_bytes * tm * tn
                      + (4 * tm * tn if res is not None else 0)) + (4 * tm * tn if use_acc else 0)
        if blocks <= MATMUL_VMEM_BUDGET:
            break
    dims = {"nn": ((1,), (0,)), "nt": ((1,), (1,)), "tn": ((0,), (0,))}[mode]

    def body(*refs):
        a_ref, b_ref = refs[0], refs[1]
        r_ref = refs[2] if res is not None else None
        o_ref = refs[3] if res is not None else refs[2]
        acc = refs[-1] if use_acc else o_ref
        k = pl.program_id(2)
        p = lax.dot_general(a_ref[...].astype(BF16), b_ref[...].astype(BF16), (dims, ((), ())),
                            preferred_element_type=F32)

        def finish(r):
            if r_ref is not None:
                r = r + alpha * r_ref[...]
            o_ref[...] = r.astype(out_dtype)

        if nk == 1:
            finish(p)
        else:
            @pl.when(k == 0)
            def _():
                acc[...] = p

            @pl.when((k > 0) & (k < nk - 1))
            def _():
                acc[...] += p

            @pl.when(k == nk - 1)
            def _():
                finish(acc[...] + p)

    if mode == "tn":
        a_spec = pl.BlockSpec((tk, tm), lambda i, j, k: (k, i))
    else:
        a_spec = pl.BlockSpec((tm, tk), lambda i, j, k: (i, k))
    if mode == "nt":
        b_spec = pl.BlockSpec((tn, tk), lambda i, j, k: (j, k))
    else:
        b_spec = pl.BlockSpec((tk, tn), lambda i, j, k: (k, j))
    in_specs = [a_spec, b_spec]
    args = [a, b]
    if res is not None:
        in_specs.append(pl.BlockSpec((tm, tn), lambda i, j, k: (i, j)))
        args.append(res)
    return pl.pallas_call(
        body, name=name,
        grid=(M // tm, N // tn, nk),
        in_specs=in_specs,
        out_specs=pl.BlockSpec((tm, tn), lambda i, j, k: (i, j)),
        out_shape=jax.ShapeDtypeStruct((M, N), out_dtype),
        scratch_shapes=[pltpu.VMEM((tm, tn), F32)] if use_acc else [],
        compiler_params=_params(("parallel", "parallel", "arbitrary")),
    )(*args)


def _ln_fwd(x, f, g, b, name):
    L, D = x.shape
    tr = _tile(L, 768, 16)

    def body(x_ref, f_ref, g_ref, b_ref, y_ref, yb_ref):
        z = ALPHA * x_ref[...] + f_ref[...]
        mu = jnp.mean(z, axis=-1, keepdims=True)
        zc = z - mu
        var = jnp.mean(zc * zc, axis=-1, keepdims=True)
        y = zc * lax.rsqrt(var + LN_EPS) * g_ref[...] + b_ref[...]
        y_ref[...] = y
        yb_ref[...] = y.astype(BF16)

    row = pl.BlockSpec((tr, D), lambda i: (i, 0))
    vec = pl.BlockSpec((1, D), lambda i: (0, 0))
    return pl.pallas_call(
        body, name=name, grid=(L // tr,),
        in_specs=[row, row, vec, vec], out_specs=[row, row],
        out_shape=[jax.ShapeDtypeStruct((L, D), F32), jax.ShapeDtypeStruct((L, D), BF16)],
        compiler_params=_params(("parallel",)),
    )(x, f, g, b)


def _ln_bwd(x, f, dy, g, name):
    L, D = x.shape
    tr = _tile(L, 768, 16)

    def body(x_ref, f_ref, dy_ref, g_ref, dz_ref, dzb_ref, dg_ref, db_ref):
        @pl.when(pl.program_id(0) == 0)
        def _():
            dg_ref[...] = jnp.zeros_like(dg_ref)
            db_ref[...] = jnp.zeros_like(db_ref)

        z = ALPHA * x_ref[...] + f_ref[...]
        mu = jnp.mean(z, axis=-1, keepdims=True)
        zc = z - mu
        var = jnp.mean(zc * zc, axis=-1, keepdims=True)
        rstd = lax.rsqrt(var + LN_EPS)
        xhat = zc * rstd
        dy = dy_ref[...]
        dxh = dy * g_ref[...]
        m1 = jnp.mean(dxh, axis=-1, keepdims=True)
        m2 = jnp.mean(dxh * xhat, axis=-1, keepdims=True)
        dz = rstd * (dxh - m1 - xhat * m2)
        dz_ref[...] = dz
        dzb_ref[...] = dz.astype(BF16)
        dg_ref[...] += jnp.sum(dy * xhat, axis=0, keepdims=True)
        db_ref[...] += jnp.sum(dy, axis=0, keepdims=True)

    row = pl.BlockSpec((tr, D), lambda i: (i, 0))
    vec = pl.BlockSpec((1, D), lambda i: (0, 0))
    return pl.pallas_call(
        body, name=name, grid=(L // tr,),
        in_specs=[row, row, row, vec], out_specs=[row, row, vec, vec],
        out_shape=[jax.ShapeDtypeStruct((L, D), F32), jax.ShapeDtypeStruct((L, D), BF16),
                   jax.ShapeDtypeStruct((1, D), F32), jax.ShapeDtypeStruct((1, D), F32)],
        compiler_params=_params(("arbitrary",)),
    )(x, f, dy, g)


def _loss_head(y, tgt, seq):
    L, D = y.shape
    tr = _tile(L, 768, 16)

    def body(y_ref, t_ref, dy_ref, loss_ref):
        i = pl.program_id(0)

        @pl.when(i == 0)
        def _():
            loss_ref[...] = jnp.zeros_like(loss_ref)

        r = i * tr + lax.broadcasted_iota(jnp.int32, (tr, D), 0)
        valid = (r >= N_META) & (r < N_META + seq)
        e = jnp.where(valid, y_ref[...] - t_ref[...], 0.0)
        dy_ref[...] = e * (1.0 / D)
        s = jnp.sum(jnp.sum(e * e, axis=-1, keepdims=True), axis=0, keepdims=True)
        loss_ref[...] += (0.5 / D) * s

    row = pl.BlockSpec((tr, D), lambda i: (i, 0))
    return pl.pallas_call(
        body, name="loss_head", grid=(L // tr,),
        in_specs=[row, row], out_specs=[row, pl.BlockSpec((1, 1), lambda i: (0, 0))],
        out_shape=[jax.ShapeDtypeStruct((L, D), F32), jax.ShapeDtypeStruct((1, 1), F32)],
        compiler_params=_params(("arbitrary",)),
    )(y, tgt)


def _shift_down(x, prev, k):
    out = pltpu.roll(x, k, 0)
    row = lax.broadcasted_iota(jnp.int32, (8, x.shape[1]), 0)
    top = out[0:8]
    for r in range(k):
        top = jnp.where(row == r, prev[8 - k + r:8 - k + r + 1, :], top)
    return top if x.shape[0] == 8 else jnp.concatenate([top, out[8:]], axis=0)


def _shift_up(x, nxt, k):
    T = x.shape[0]
    out = pltpu.roll(x, T - k, 0)
    row = lax.broadcasted_iota(jnp.int32, (8, x.shape[1]), 0)
    bot = out[T - 8:T]
    for r in range(k):
        bot = jnp.where(row == 8 - k + r, nxt[r:r + 1, :], bot)
    return bot if T == 8 else jnp.concatenate([out[:T - 8], bot], axis=0)


def _conv3(x, prev, w, b):
    return w[2:3, :] * x + w[1:2, :] * _shift_down(x, prev, 1) + w[0:1, :] * _shift_down(x, prev, 2) + b


def _ffn_act_fwd(up, w, b, name):
    L, C = up.shape
    F = C // 2
    ts = FFN_TILE
    n = L // ts

    def body(up_ref, pv_ref, w_ref, b_ref, a_ref):
        i = pl.program_id(0)
        x = up_ref[...].astype(F32)
        prev = jnp.where(i > 0, pv_ref[...].astype(F32)[8:16], 0.0)
        u = _conv3(x, prev, w_ref[...], b_ref[...])
        gate = u[:, :F]
        a_ref[...] = (gate * _sigmoid(gate) * u[:, F:]).astype(BF16)

    return pl.pallas_call(
        body, name=name, grid=(n,),
        in_specs=[pl.BlockSpec((ts, C), lambda i: (i, 0)),
                  pl.BlockSpec((16, C), lambda i: (jnp.maximum(i * (ts // 16) - 1, 0), 0)),
                  pl.BlockSpec((3, C), lambda i: (0, 0)), pl.BlockSpec((1, C), lambda i: (0, 0))],
        out_specs=pl.BlockSpec((ts, F), lambda i: (i, 0)),
        out_shape=jax.ShapeDtypeStruct((L, F), BF16),
        compiler_params=_params(("parallel",)),
    )(up, up, w, b)


def _ffn_act_bwd(up, da, w, b, name):
    L, C = up.shape
    F = C // 2
    ts = FFN_TILE
    n = L // ts
    last16 = L // 16 - 1

    def du_of(u, da):
        gate, val = u[:, :F], u[:, F:]
        sg = _sigmoid(gate)
        dgate = da * val * (sg * (1.0 + gate * (1.0 - sg)))
        dval = da * (gate * sg)
        return jnp.concatenate([dgate, dval], axis=1)

    def body(up_ref, pv_ref, nx_ref, da_ref, dan_ref, w_ref, b_ref, dup_ref, dw_ref, db_ref):
        i = pl.program_id(0)

        @pl.when(i == 0)
        def _():
            dw_ref[...] = jnp.zeros_like(dw_ref)
            db_ref[...] = jnp.zeros_like(db_ref)

        w = w_ref[...]
        bias = b_ref[...]
        x = up_ref[...].astype(F32)
        prev = jnp.where(i > 0, pv_ref[...].astype(F32)[8:16], 0.0)
        u = _conv3(x, prev, w, bias)
        u_next = _conv3(nx_ref[...].astype(F32)[0:8], x[ts - 8:ts], w, bias)
        du = du_of(u, da_ref[...].astype(F32))
        dun = jnp.where(i < n - 1, du_of(u_next, dan_ref[...].astype(F32)[0:8]), 0.0)
        du1 = _shift_up(du, dun, 1)
        du2 = _shift_up(du, dun, 2)
        dup_ref[...] = (w[2:3, :] * du + w[1:2, :] * du1 + w[0:1, :] * du2).astype(BF16)
        dw_ref[...] += jnp.concatenate([jnp.sum(x * du2, axis=0, keepdims=True),
                                        jnp.sum(x * du1, axis=0, keepdims=True),
                                        jnp.sum(x * du, axis=0, keepdims=True)], axis=0)
        db_ref[...] += jnp.sum(du, axis=0, keepdims=True)

    prv = lambda i: (jnp.maximum(i * (ts // 16) - 1, 0), 0)
    nxt = lambda i: (jnp.minimum((i + 1) * (ts // 16), last16), 0)
    return pl.pallas_call(
        body, name=name, grid=(n,),
        in_specs=[pl.BlockSpec((ts, C), lambda i: (i, 0)), pl.BlockSpec((16, C), prv), pl.BlockSpec((16, C), nxt),
                  pl.BlockSpec((ts, F), lambda i: (i, 0)), pl.BlockSpec((16, F), nxt),
                  pl.BlockSpec((3, C), lambda i: (0, 0)), pl.BlockSpec((1, C), lambda i: (0, 0))],
        out_specs=[pl.BlockSpec((ts, C), lambda i: (i, 0)), pl.BlockSpec((3, C), lambda i: (0, 0)),
                   pl.BlockSpec((1, C), lambda i: (0, 0))],
        out_shape=[jax.ShapeDtypeStruct((L, C), BF16), jax.ShapeDtypeStruct((3, C), F32),
                   jax.ShapeDtypeStruct((1, C), F32)],
        compiler_params=_params(("arbitrary",)),
    )(up, up, up, da, da, w, b)


def _pool_window(ext, tile_rows, first_row, lead):
    T = ext.shape[0]
    sh = (lambda x, k: pltpu.roll(x, T - k, 0)) if lead else (lambda x, k: pltpu.roll(x, k, 0))
    r2 = ext + sh(ext, 1)
    r4 = r2 + sh(r2, 2)
    r8 = r4 + sh(r4, 4)
    r16 = r8 + sh(r8, 8)
    lo = 0 if lead else 16
    grp = lax.broadcasted_iota(jnp.int32, (tile_rows, POOL_W), 1) // POOL_GROUP
    pick = lambda a, b, c, d: jnp.where(grp == 0, a, jnp.where(grp == 1, b, jnp.where(grp == 2, c, d)))
    win = pick(r2[lo:lo + tile_rows], r4[lo:lo + tile_rows], r8[lo:lo + tile_rows], r16[lo:lo + tile_rows])
    return win, pick(2.0, 4.0, 8.0, 16.0)


def _pool_count(first_row, rows, wlen):
    t1 = (first_row + lax.broadcasted_iota(jnp.int32, (rows, POOL_W), 0) + 1).astype(F32)
    return jnp.minimum(t1, wlen)


def _cp_fwd(hc, wc, wblk, pscale, name):
    L = hc.shape[0]
    ts = SEQ_TILE
    n = L // ts

    def body(h_ref, hp_ref, wc_ref, wb_ref, ps_ref, y_ref):
        i = pl.program_id(0)
        h = h_ref[...]
        hp = jnp.where(i > 0, hp_ref[...], 0.0)
        cb, cc, cv, pv = h[:, 0:256], h[:, 256:512], h[:, 512:768], h[:, 768:1024]
        p = cc * cv
        pp = hp[8:16, 256:512] * hp[8:16, 512:768]
        w = wc_ref[...]
        conv = w[2:3, :] * p + w[1:2, :] * _shift_down(p, pp, 1) + w[0:1, :] * _shift_down(p, pp, 2)
        y_conv = cb * conv
        ext = jnp.concatenate([hp[:, 768:1024], pv], axis=0)
        win, wlen = _pool_window(ext, ts, i * ts, False)
        d = win / _pool_count(i * ts, ts, wlen) - pv
        y_pool = _dnn(d.astype(BF16), wb_ref[...]) * ps_ref[...]
        y_ref[...] = jnp.concatenate([y_conv, y_pool], axis=1).astype(BF16)

    return pl.pallas_call(
        body, name=name, grid=(n,),
        in_specs=[pl.BlockSpec((ts, 1024), lambda i: (i, 0)),
                  pl.BlockSpec((16, 1024), lambda i: (jnp.maximum(i * (ts // 16) - 1, 0), 0)),
                  pl.BlockSpec((3, 256), lambda i: (0, 0)), pl.BlockSpec((256, 256), lambda i: (0, 0)),
                  pl.BlockSpec((1, 256), lambda i: (0, 0))],
        out_specs=pl.BlockSpec((ts, 512), lambda i: (i, 0)),
        out_shape=jax.ShapeDtypeStruct((L, 512), BF16),
        compiler_params=_params(("parallel",)),
    )(hc, hc, wc, wblk, pscale)


def _cp_bwd(hc, dcat, wc, wblk, pscale, name):
    L = hc.shape[0]
    ts = SEQ_TILE
    n = L // ts
    last16 = L // 16 - 1

    def body(h_ref, hp_ref, hn_ref, dy_ref, dyn_ref, wc_ref, wb_ref, ps_ref,
             dh_ref, dwc_ref, dwb_ref, dps_ref):
        i = pl.program_id(0)

        @pl.when(i == 0)
        def _():
            dwc_ref[...] = jnp.zeros_like(dwc_ref)
            dwb_ref[...] = jnp.zeros_like(dwb_ref)
            dps_ref[...] = jnp.zeros_like(dps_ref)

        h = h_ref[...]
        hp = jnp.where(i > 0, hp_ref[...], 0.0)
        hn = hn_ref[...]
        dy = dy_ref[...]
        dyn = jnp.where(i < n - 1, dyn_ref[...], 0.0)
        cb, cc, cv, pv = h[:, 0:256], h[:, 256:512], h[:, 512:768], h[:, 768:1024]
        w = wc_ref[...]
        p = cc * cv
        pp = hp[8:16, 256:512] * hp[8:16, 512:768]
        p1 = _shift_down(p, pp, 1)
        p2 = _shift_down(p, pp, 2)
        conv = w[2:3, :] * p + w[1:2, :] * p1 + w[0:1, :] * p2
        dyc = dy[:, 0:256]
        dcb = dyc * conv
        dconv = dyc * cb
        dconv_n = dyn[0:8, 0:256] * hn[0:8, 0:256]
        dc1 = _shift_up(dconv, dconv_n, 1)
        dc2 = _shift_up(dconv, dconv_n, 2)
        dp = w[2:3, :] * dconv + w[1:2, :] * dc1 + w[0:1, :] * dc2
        dwc_ref[...] += jnp.concatenate([jnp.sum(p * dc2, axis=0, keepdims=True),
                                         jnp.sum(p * dc1, axis=0, keepdims=True),
                                         jnp.sum(p * dconv, axis=0, keepdims=True)], axis=0)
        ps = ps_ref[...]
        wb = wb_ref[...]
        ext = jnp.concatenate([hp[:, 768:1024], pv], axis=0)
        win, wlen = _pool_window(ext, ts, i * ts, False)
        d = win / _pool_count(i * ts, ts, wlen) - pv
        db = d.astype(BF16)
        dyp = dy[:, 256:512]
        dps_ref[...] += jnp.sum(dyp * _dnn(db, wb), axis=0, keepdims=True)
        dypre = (dyp * ps).astype(BF16)
        dwb_ref[...] += _dtn(db, dypre)
        dd = _dnt(dypre, wb)
        ddn = _dnt((dyn[:, 256:512] * ps).astype(BF16), wb)
        e = dd / _pool_count(i * ts, ts, wlen)
        en = ddn / _pool_count((i + 1) * ts, 16, wlen[0:16])
        lead, _ = _pool_window(jnp.concatenate([e, en], axis=0), ts, i * ts, True)
        dpv = lead - dd
        dh_ref[...] = jnp.concatenate([dcb, dp * cv, dp * cc, dpv], axis=1).astype(BF16)

    return pl.pallas_call(
        body, name=name, grid=(n,),
        in_specs=[pl.BlockSpec((ts, 1024), lambda i: (i, 0)),
                  pl.BlockSpec((16, 1024), lambda i: (jnp.maximum(i * (ts // 16) - 1, 0), 0)),
                  pl.BlockSpec((16, 1024), lambda i: (jnp.minimum((i + 1) * (ts // 16), last16), 0)),
                  pl.BlockSpec((ts, 512), lambda i: (i, 1)),
                  pl.BlockSpec((16, 512), lambda i: (jnp.minimum((i + 1) * (ts // 16), last16), 1)),
                  pl.BlockSpec((3, 256), lambda i: (0, 0)), pl.BlockSpec((256, 256), lambda i: (0, 0)),
                  pl.BlockSpec((1, 256), lambda i: (0, 0))],
        out_specs=[pl.BlockSpec((ts, 1024), lambda i: (i, 0)), pl.BlockSpec((3, 256), lambda i: (0, 0)),
                   pl.BlockSpec((256, 256), lambda i: (0, 0)), pl.BlockSpec((1, 256), lambda i: (0, 0))],
        out_shape=[jax.ShapeDtypeStruct((L, 1024), BF16), jax.ShapeDtypeStruct((3, 256), F32),
                   jax.ShapeDtypeStruct((256, 256), F32), jax.ShapeDtypeStruct((1, 256), F32)],
        compiler_params=_params(("arbitrary",)),
    )(hc, hc, hc, dcat, dcat, wc, wblk, pscale)


def _lower_bound(lb_ref, layer):
    b0, b1 = lb_ref[0:1, :], lb_ref[1:2, :]
    m = jnp.maximum(b0, b1)
    e0, e1 = jnp.exp(b0 - m), jnp.exp(b1 - m)
    p0, p1 = e0 / (e0 + e1), e1 / (e0 + e1)
    lb = (p0 - p0) if layer == 0 else ((p0 + p1) - p0)
    return lb, p0, p1


def _cumsum_rows(x, reverse=False):
    row = lax.broadcasted_iota(jnp.int32, x.shape, 0)
    for sh in (1, 2, 4, 8):
        if reverse:
            x = x + jnp.where(row < SUB - sh, pltpu.roll(x, SUB - sh, 0), 0.0)
        else:
            x = x + jnp.where(row >= sh, pltpu.roll(x, sh, 0), 0.0)
    return x


def _gates(fz, lb):
    sig = _sigmoid(fz)
    f = lb + (1.0 - lb) * sig
    g = jnp.log(jnp.maximum(f, F_FLOOR))
    k = (1.0 - lb) * (1.0 - sig)
    return sig, f, g, k


def _head(h):
    return slice(h * HG_D, (h + 1) * HG_D)


def _hgrn_fwd(hh, lbp, gnorm, layer, name):
    L = hh.shape[0]
    ts = SEQ_TILE
    n = L // ts
    nsub = ts // SUB

    def body(q_ref, f_ref, i_ref, g_ref, lb_ref, gn_ref, y_ref, o_ref, s_ref, a_ref, St):
        @pl.when(pl.program_id(0) == 0)
        def _():
            St[...] = jnp.zeros_like(St)

        lb, _, _ = _lower_bound(lb_ref, layer)
        gn = jnp.tile(gn_ref[...], (1, HG_HEADS))
        r16 = lax.broadcasted_iota(jnp.int32, (SUB, SUB), 0)
        c16 = lax.broadcasted_iota(jnp.int32, (SUB, SUB), 1)

        def block(j, carry):
            rows = pl.ds(pl.multiple_of(j * SUB, SUB), SUB)
            q = q_ref[rows, :] * Q_SCALE
            iv = i_ref[rows, :]
            gz = g_ref[rows, :]
            _, _, g, k = _gates(f_ref[rows, :], lb)
            G = _cumsum_rows(g)
            Gl = G[SUB - 1:SUB, :]
            qt = (q * jnp.exp(G)).astype(BF16)
            kd = (k * jnp.exp(Gl - G)).astype(BF16)
            eGl = jnp.exp(Gl)
            ib = iv.astype(BF16)
            A = [jnp.zeros((SUB, SUB), F32) for _ in range(HG_HEADS)]
            for s in range(SUB):
                P = q * jnp.exp(jnp.minimum(G - G[s:s + 1, :], 0.0)) * k[s:s + 1, :]
                for h in range(HG_HEADS):
                    A[h] = jnp.where(c16 == s, jnp.sum(P[:, _head(h)], axis=-1, keepdims=True), A[h])
            outs, ons, amats = [], [], []
            for h in range(HG_HEADS):
                sl = _head(h)
                Sb = St[h].astype(BF16)
                s_ref[j, sl, :] = Sb
                Am = jnp.where(r16 >= c16, A[h], 0.0)
                amats.append(Am)
                o = _dnt(qt[:, sl], Sb) + _dnn(Am.astype(BF16), ib[:, sl])
                St[h] = eGl[:, sl] * St[h] + _dtn(ib[:, sl], kd[:, sl])
                outs.append(o)
                ons.append(o * lax.rsqrt(jnp.mean(o * o, axis=-1, keepdims=True) + RMS_EPS))
            a_ref[rows, :] = jnp.concatenate(amats, axis=1)
            o_ref[rows, :] = jnp.concatenate(outs, axis=1)
            y = jnp.concatenate(ons, axis=1) * gn * (gz * _sigmoid(gz))
            y_ref[rows, :] = y.astype(BF16)
            return carry

        lax.fori_loop(0, nsub, block, 0, unroll=2)

    col = lambda c: pl.BlockSpec((ts, HG_W), lambda i: (i, c))
    return pl.pallas_call(
        body, name=name, grid=(n,),
        in_specs=[col(0), col(1), col(2), col(3), pl.BlockSpec((2, HG_W), lambda i: (0, 0)),
                  pl.BlockSpec((1, HG_D), lambda i: (0, 0))],
        out_specs=[pl.BlockSpec((ts, HG_W), lambda i: (i, 0)), pl.BlockSpec((ts, HG_W), lambda i: (i, 0)),
                   pl.BlockSpec((nsub, HG_W, HG_D), lambda i: (i, 0, 0)),
                   pl.BlockSpec((ts, HG_HEADS * SUB), lambda i: (i, 0))],
        out_shape=[jax.ShapeDtypeStruct((L, HG_W), BF16), jax.ShapeDtypeStruct((L, HG_W), F32),
                   jax.ShapeDtypeStruct((L // SUB, HG_W, HG_D), BF16),
                   jax.ShapeDtypeStruct((L, HG_HEADS * SUB), F32)],
        scratch_shapes=[pltpu.VMEM((HG_HEADS, HG_D, HG_D), F32)],
        compiler_params=_params(("arbitrary",)),
    )(hh, hh, hh, hh, lbp, gnorm)


def _hgrn_bwd(hh, o_raw, states, amat, dcat, lbp, gnorm, layer, name):
    L = hh.shape[0]
    ts = SEQ_TILE
    n = L // ts
    nsub = ts // SUB

    def body(q_ref, f_ref, i_ref, g_ref, o_ref, s_ref, a_ref, dy_ref, lb_ref, gn_ref,
             dh_ref, dlb_ref, dgn_ref, dSt, dlb_acc):
        step = pl.program_id(0)

        @pl.when(step == 0)
        def _():
            dSt[...] = jnp.zeros_like(dSt)
            dlb_acc[...] = jnp.zeros_like(dlb_acc)
            dgn_ref[...] = jnp.zeros_like(dgn_ref)

        lb, p0, p1 = _lower_bound(lb_ref, layer)
        gnh = gn_ref[...]
        gn = jnp.tile(gnh, (1, HG_HEADS))
        r16 = lax.broadcasted_iota(jnp.int32, (SUB, SUB), 0)
        c16 = lax.broadcasted_iota(jnp.int32, (SUB, SUB), 1)

        def block(jj, carry):
            j = nsub - 1 - jj
            rows = pl.ds(pl.multiple_of(j * SUB, SUB), SUB)
            q = q_ref[rows, :] * Q_SCALE
            iv = i_ref[rows, :]
            gz = g_ref[rows, :]
            o = o_ref[rows, :]
            dy = dy_ref[rows, :]
            sig, f, g, k = _gates(f_ref[rows, :], lb)
            G = _cumsum_rows(g)
            Gl = G[SUB - 1:SUB, :]
            eG = jnp.exp(G)
            edl = jnp.exp(Gl - G)
            eGl = jnp.exp(Gl)
            qt = (q * eG).astype(BF16)
            kd = (k * edl).astype(BF16)
            ib = iv.astype(BF16)
            sgz = _sigmoid(gz)
            sil = gz * sgz
            dyn = dy * sil
            on_parts, do_parts = [], []
            dgn = jnp.zeros((1, HG_D), F32)
            for h in range(HG_HEADS):
                sl = _head(h)
                oh = o[:, sl]
                rs = lax.rsqrt(jnp.mean(oh * oh, axis=-1, keepdims=True) + RMS_EPS)
                on = oh * rs
                dgn = dgn + jnp.sum(dyn[:, sl] * on, axis=0, keepdims=True)
                don = dyn[:, sl] * gnh
                do_parts.append(rs * (don - on * jnp.mean(don * on, axis=-1, keepdims=True)))
                on_parts.append(on)
            dgn_ref[...] += dgn
            on_all = jnp.concatenate(on_parts, axis=1)
            dgz = dy * on_all * gn * (sgz * (1.0 + gz * (1.0 - sgz)))
            do = jnp.concatenate(do_parts, axis=1)
            dob = do.astype(BF16)
            amat = a_ref[rows, :]
            dq_p, dk_p, di_p, tail_p = [], [], [], []
            for h in range(HG_HEADS):
                sl = _head(h)
                qh, kh, Gh = q[:, sl], k[:, sl], G[:, sl]
                Ap = jnp.where(r16 >= c16, _dnt(dob[:, sl], ib[:, sl]), 0.0)
                ApT = jnp.where(r16 <= c16, _dnt(ib[:, sl], dob[:, sl]), 0.0)
                dqh = jnp.zeros((SUB, HG_D), F32)
                dkh = jnp.zeros((SUB, HG_D), F32)
                for s in range(SUB):
                    dGs = Gh - Gh[s:s + 1, :]
                    e = jnp.exp(jnp.minimum(dGs, -dGs))
                    dqh = dqh + Ap[:, s:s + 1] * (e * kh[s:s + 1, :])
                    dkh = dkh + ApT[:, s:s + 1] * (e * qh[s:s + 1, :])
                Sb = s_ref[j, sl, :]
                dSb = dSt[h].astype(BF16)
                Am = amat[:, h * SUB:(h + 1) * SUB].astype(BF16)
                dq_p.append(dqh + eG[:, sl] * _dnn(dob[:, sl], Sb))
                dk_p.append(dkh + edl[:, sl] * _dnn(ib[:, sl], dSb))
                di_p.append(_dtn(Am, dob[:, sl]) + _dnt(kd[:, sl], dSb))
                St_end = eGl[:, sl] * Sb.astype(F32) + _dtn(ib[:, sl], kd[:, sl])
                tail_p.append(jnp.sum(dSt[h] * St_end, axis=0, keepdims=True))
                dSt[h] = eGl[:, sl] * dSt[h] + _dtn(dob[:, sl], qt[:, sl])
            dq = jnp.concatenate(dq_p, axis=1)
            dk = jnp.concatenate(dk_p, axis=1)
            di = jnp.concatenate(di_p, axis=1)
            dg = _cumsum_rows(q * dq - k * dk, reverse=True) + jnp.concatenate(tail_p, axis=1)
            df = jnp.where(f > F_FLOOR, dg / f, 0.0)
            dfk = df - dk
            dfz = (1.0 - lb) * dfk * sig * (1.0 - sig)
            dlb_acc[...] += jnp.sum(dfk * (1.0 - sig), axis=0, keepdims=True)
            dh_ref[rows, :] = jnp.concatenate([dq * Q_SCALE, dfz, di, dgz], axis=1).astype(BF16)
            return carry

        lax.fori_loop(0, nsub, block, 0)

        @pl.when(step == n - 1)
        def _():
            if layer == 0:
                dlb_ref[...] = jnp.zeros_like(dlb_ref)
            else:
                dz1 = p0 * p1 * dlb_acc[...]
                dlb_ref[...] = jnp.concatenate([-dz1, dz1], axis=0)

    rev = lambda i: n - 1 - i
    col = lambda c: pl.BlockSpec((ts, HG_W), lambda i: (rev(i), c))
    return pl.pallas_call(
        body, name=name, grid=(n,),
        in_specs=[col(0), col(1), col(2), col(3), col(0),
                  pl.BlockSpec((nsub, HG_W, HG_D), lambda i: (rev(i), 0, 0)),
                  pl.BlockSpec((ts, HG_HEADS * SUB), lambda i: (rev(i), 0)), col(0),
                  pl.BlockSpec((2, HG_W), lambda i: (0, 0)), pl.BlockSpec((1, HG_D), lambda i: (0, 0))],
        out_specs=[pl.BlockSpec((ts, 4 * HG_W), lambda i: (rev(i), 0)),
                   pl.BlockSpec((2, HG_W), lambda i: (0, 0)), pl.BlockSpec((1, HG_D), lambda i: (0, 0))],
        out_shape=[jax.ShapeDtypeStruct((L, 4 * HG_W), BF16), jax.ShapeDtypeStruct((2, HG_W), F32),
                   jax.ShapeDtypeStruct((1, HG_D), F32)],
        scratch_shapes=[pltpu.VMEM((HG_HEADS, HG_D, HG_D), F32), pltpu.VMEM((1, HG_W), F32)],
        compiler_params=_params(("arbitrary",)),
    )(hh, hh, hh, hh, o_raw, states, amat, dcat, lbp, gnorm)


def _adamw_body(gp_ref, w_ref, m_ref, v_ref, g_ref, d_ref, mo_ref, vo_ref):
    c1 = 1.0 - ADAM_B1 ** ADAM_STEP
    c2 = 1.0 - ADAM_B2 ** ADAM_STEP
    g = gp_ref[0].astype(F32)
    for k in range(1, N_DEV):
        g = g + gp_ref[k].astype(F32)
    mn = ADAM_B1 * m_ref[...] + (1.0 - ADAM_B1) * g
    vn = ADAM_B2 * v_ref[...] + (1.0 - ADAM_B2) * (g * g)
    m_hat = mn / c1
    v_hat = vn / c2
    g_ref[...] = g
    d_ref[...] = -ADAM_LR * (m_hat / (jnp.sqrt(v_hat) + ADAM_EPS) + ADAM_WD * w_ref[...])
    mo_ref[...] = mn
    vo_ref[...] = vn


def _adamw_layers(gparts, w, m, v, name):
    _, depth, R, C = gparts.shape
    tr = _tile(R, 256, 16)

    def body(*refs):
        _adamw_body(*refs)

    blk = pl.BlockSpec((None, tr, C), lambda l, i: (l, i, 0))
    shp = jax.ShapeDtypeStruct((depth, R, C), F32)
    return pl.pallas_call(
        body, name=name, grid=(depth, R // tr),
        in_specs=[pl.BlockSpec((N_DEV, None, tr, C), lambda l, i: (0, l, i, 0)), blk, blk, blk],
        out_specs=[blk, blk, blk, blk], out_shape=[shp, shp, shp, shp],
        compiler_params=_params(("parallel", "parallel")),
    )(gparts, w, m, v)


def _adamw(gparts, w, m, v, name):
    R = w.shape[0]
    tr = _tile(R, 1024, 16) if R % 16 == 0 else R

    def body(*refs):
        _adamw_body(*refs)

    row = pl.BlockSpec((tr, LANES), lambda i: (i, 0))
    shp = jax.ShapeDtypeStruct((R, LANES), F32)
    return pl.pallas_call(
        body, name=name, grid=(R // tr,),
        in_specs=[pl.BlockSpec((N_DEV, tr, LANES), lambda i: (0, i, 0)), row, row, row],
        out_specs=[row, row, row, row], out_shape=[shp, shp, shp, shp],
        compiler_params=_params(("parallel",)),
    )(gparts, w, m, v)


def _flip(coord, bit):
    return 1 - coord if bit else coord


def _gather_many(blocks, name):
    n = len(blocks)

    def body(*refs):
        x_refs, out_refs = refs[:n], refs[n:2 * n]
        send_sems, recv_sems, local_sems = refs[2 * n:]
        x, y, c = lax.axis_index("x"), lax.axis_index("y"), lax.axis_index("c")
        me, sibling = (x, y, c), (x, y, 1 - c)
        chips = [(1 - x, y), (x, 1 - y), (1 - x, 1 - y)]

        def slot(a, px, py, pc):
            return out_refs[a].at[4 * px + 2 * py + pc]

        def copy(a, k, blk, to, src=None):
            return pltpu.make_async_remote_copy(
                src_ref=slot(a, *blk) if src is None else src, dst_ref=slot(a, *blk),
                send_sem=send_sems.at[7 * a + k], recv_sem=recv_sems.at[7 * a + k],
                device_id=to, device_id_type=pl.DeviceIdType.MESH)

        mine = [pltpu.make_async_copy(x_refs[a], slot(a, *me), local_sems.at[a]) for a in range(n)]
        for cp in mine:
            cp.start()
        first = [copy(a, 0, me, sibling, src=x_refs[a]) for a in range(n)]
        for j, chip in enumerate(chips):
            first += [copy(a, 1 + j, me, (*chip, c), src=x_refs[a]) for a in range(n)]
        for cp in first:
            cp.start()
        passed = []
        for j, chip in enumerate(chips):
            for a in range(n):
                copy(a, 1 + j, (*chip, c), me).wait_recv()
                fwd = copy(a, 4 + j, (*chip, c), sibling)
                fwd.start()
                passed.append(fwd)
        for a in range(n):
            copy(a, 0, sibling, me).wait_recv()
        for j, chip in enumerate(chips):
            for a in range(n):
                copy(a, 4 + j, (*chip, 1 - c), me).wait_recv()
        for cp in first + passed:
            cp.wait_send()
        for cp in mine:
            cp.wait()

    hbm = pl.BlockSpec(memory_space=pl.ANY)
    return pl.pallas_call(
        body, name=name,
        out_shape=[jax.ShapeDtypeStruct((N_DEV,) + b.shape, b.dtype) for b in blocks],
        in_specs=[hbm] * n, out_specs=[hbm] * n,
        scratch_shapes=[pltpu.SemaphoreType.DMA((7 * n,)), pltpu.SemaphoreType.DMA((7 * n,)),
                        pltpu.SemaphoreType.DMA((n,))],
    )(*blocks)


def _exchange_grads(layer_chunks, small_chunks, rep_block, name):
    flows, inputs = [], []
    for p, per_layer in enumerate(layer_chunks):
        for l, arr in enumerate(per_layer):
            flows.append(("param", p, l))
            inputs.append(arr)
    flows += [("small",), ("rep",)]
    inputs += [small_chunks, rep_block]
    n_par = len(layer_chunks)
    n_in, n_out, nf = len(inputs), n_par + 2, len(flows)

    def body(*refs):
        in_refs, out_refs = refs[:n_in], refs[n_in:n_in + n_out]
        send_sems, recv_sems, local_sems = refs[n_in + n_out:]
        x, y, c = lax.axis_index("x"), lax.axis_index("y"), lax.axis_index("c")
        me = 4 * x + 2 * y + c

        def src(f, dev):
            return in_refs[f] if flows[f][0] == "rep" else in_refs[f].at[dev]

        def dst(f, dev):
            if flows[f][0] == "param":
                _, p, l = flows[f]
                return out_refs[p].at[dev, l]
            return out_refs[n_par + (0 if flows[f][0] == "small" else 1)].at[dev]

        mine = [pltpu.make_async_copy(src(f, me), dst(f, me), local_sems.at[f]) for f in range(nf)]
        for cp in mine:
            cp.start()
        copies = []
        for k in range(1, N_DEV):
            px, py, pc = _flip(x, k & 4), _flip(y, k & 2), _flip(c, k & 1)
            peer = 4 * px + 2 * py + pc
            for f in range(nf):
                sems = dict(send_sem=send_sems.at[7 * f + k - 1], recv_sem=recv_sems.at[7 * f + k - 1],
                            device_id=(px, py, pc), device_id_type=pl.DeviceIdType.MESH)
                send = pltpu.make_async_remote_copy(src_ref=src(f, peer), dst_ref=dst(f, me), **sems)
                recv = pltpu.make_async_remote_copy(src_ref=src(f, peer), dst_ref=dst(f, peer), **sems)
                send.start()
                copies.append((send, recv))
        for send, recv in copies:
            recv.wait_recv()
        for send, recv in copies:
            send.wait_send()
        for cp in mine:
            cp.wait()

    out_shape = [jax.ShapeDtypeStruct((N_DEV, len(pl_)) + pl_[0].shape[1:], pl_[0].dtype) for pl_ in layer_chunks]
    out_shape += [jax.ShapeDtypeStruct(small_chunks.shape, small_chunks.dtype),
                  jax.ShapeDtypeStruct((N_DEV,) + rep_block.shape, rep_block.dtype)]
    hbm = pl.BlockSpec(memory_space=pl.ANY)
    return pl.pallas_call(
        body, name=name, out_shape=out_shape,
        in_specs=[hbm] * n_in, out_specs=[hbm] * n_out,
        scratch_shapes=[pltpu.SemaphoreType.DMA((7 * nf,)), pltpu.SemaphoreType.DMA((7 * nf,)),
                        pltpu.SemaphoreType.DMA((nf,))],
    )(*inputs)


def _pack(arrs, dtype, row_mult=8):
    parts, offs, r = [], [], 0
    for a in arrs:
        flat = a.astype(dtype).reshape(-1)
        nrow = -(-flat.shape[0] // LANES)
        flat = jnp.pad(flat, (0, nrow * LANES - flat.shape[0]))
        parts.append(flat.reshape(nrow, LANES))
        offs.append((r, nrow))
        r += nrow
    pad = (-r) % row_mult
    if pad:
        parts.append(jnp.zeros((pad, LANES), dtype))
    return jnp.concatenate(parts, axis=0), offs


def _unpack(buf, offs, shapes, lead=()):
    outs = []
    for (r, nrow), shp in zip(offs, shapes):
        size = 1
        for s in shp:
            size *= s
        flat = buf[..., r:r + nrow, :].reshape(lead + (nrow * LANES,))
        outs.append(flat[..., :size].reshape(lead + tuple(shp)))
    return outs


def _cols_from_shards(g, axis):
    return jnp.concatenate([g[j] for j in range(N_DEV)], axis=axis)


BIG = ("w_in", "w_o", "w_up", "w_down")
SMALL_SHARDED = ("meta_tokens", "w_conv", "w_ffn_conv")
REPLICATED = ("hg_lower_bounds", "w_pool", "pool_scale", "hg_norm_g", "ln1_g", "ln1_b", "b_ffn_conv", "ln2_g", "ln2_b")
WEIGHTS = ("meta_tokens", "hg_lower_bounds", "w_in", "w_conv", "w_pool", "pool_scale", "hg_norm_g", "w_o",
           "ln1_g", "ln1_b", "w_up", "w_ffn_conv", "b_ffn_conv", "w_down", "ln2_g", "ln2_b")


def _pool_blockdiag(w_pool_l):
    z = jnp.zeros((POOL_GROUP, POOL_GROUP), w_pool_l.dtype)
    rows = [jnp.concatenate([w_pool_l[g] if h == g else z for h in range(4)], axis=1) for g in range(4)]
    return jnp.concatenate(rows, axis=0)


def _layer_weights(full, l):
    w_in = full["w_in"][l]
    w_o = full["w_o"][l]
    return dict(
        w_hg=w_in[:, 768:2816],
        w_cp=jnp.concatenate([w_in[:, 0:768], w_in[:, 2816:3072]], axis=1),
        w_o=jnp.concatenate([w_o[256:768], w_o[0:256], w_o[768:1024]], axis=0),
        w_up=full["w_up"][l], w_down=full["w_down"][l],
    )


def kernel(x, meta_tokens, hg_lower_bounds, w_in, w_conv, w_pool, pool_scale, hg_norm_g, w_o, ln1_g, ln1_b, w_up, w_ffn_conv, b_ffn_conv, w_down, ln2_g, ln2_b, loss_target, m_meta_tokens, m_hg_lower_bounds, m_w_in, m_w_conv, m_w_pool, m_pool_scale, m_hg_norm_g, m_w_o, m_ln1_g, m_ln1_b, m_w_up, m_w_ffn_conv, m_b_ffn_conv, m_w_down, m_ln2_g, m_ln2_b, v_meta_tokens, v_hg_lower_bounds, v_w_in, v_w_conv, v_w_pool, v_pool_scale, v_hg_norm_g, v_w_o, v_ln1_g, v_ln1_b, v_w_up, v_w_ffn_conv, v_b_ffn_conv, v_w_down, v_ln2_g, v_ln2_b):
    W = dict(meta_tokens=meta_tokens, hg_lower_bounds=hg_lower_bounds, w_in=w_in, w_conv=w_conv, w_pool=w_pool,
             pool_scale=pool_scale, hg_norm_g=hg_norm_g, w_o=w_o, ln1_g=ln1_g, ln1_b=ln1_b, w_up=w_up,
             w_ffn_conv=w_ffn_conv, b_ffn_conv=b_ffn_conv, w_down=w_down, ln2_g=ln2_g, ln2_b=ln2_b)
    M = dict(meta_tokens=m_meta_tokens, hg_lower_bounds=m_hg_lower_bounds, w_in=m_w_in, w_conv=m_w_conv,
             w_pool=m_w_pool, pool_scale=m_pool_scale, hg_norm_g=m_hg_norm_g, w_o=m_w_o, ln1_g=m_ln1_g,
             ln1_b=m_ln1_b, w_up=m_w_up, w_ffn_conv=m_w_ffn_conv, b_ffn_conv=m_b_ffn_conv, w_down=m_w_down,
             ln2_g=m_ln2_g, ln2_b=m_ln2_b)
    V = dict(meta_tokens=v_meta_tokens, hg_lower_bounds=v_hg_lower_bounds, w_in=v_w_in, w_conv=v_w_conv,
             w_pool=v_w_pool, pool_scale=v_pool_scale, hg_norm_g=v_hg_norm_g, w_o=v_w_o, ln1_g=v_ln1_g,
             ln1_b=v_ln1_b, w_up=v_w_up, w_ffn_conv=v_w_ffn_conv, b_ffn_conv=v_b_ffn_conv, w_down=v_w_down,
             ln2_g=v_ln2_g, ln2_b=v_ln2_b)
    assert x.shape[0] == 1 and x.shape[2] == D_MODEL and w_in.shape[0] == DEPTH
    seq = x.shape[1]
    L = -(-(seq + N_META) // ROW_ALIGN) * ROW_ALIGN

    small_pack, small_offs = _pack([W[n] for n in SMALL_SHARDED], F32)
    g_in, g_o, g_up, g_down, small_all = _gather_many([W[n].astype(BF16) for n in BIG] + [small_pack], "gather_weights")
    full = dict(
        w_in=jnp.transpose(g_in, (1, 2, 0, 3)).reshape(DEPTH, D_MODEL, -1),
        w_o=jnp.transpose(g_o, (1, 0, 2, 3)).reshape(DEPTH, -1, D_MODEL),
        w_up=jnp.transpose(g_up, (1, 2, 0, 3)).reshape(DEPTH, D_MODEL, -1),
        w_down=jnp.transpose(g_down, (1, 0, 2, 3)).reshape(DEPTH, -1, D_MODEL))
    for n, a in zip(SMALL_SHARDED, _unpack(small_all, small_offs, [W[n].shape for n in SMALL_SHARDED], (N_DEV,))):
        full[n] = _cols_from_shards(a, 1)

    pad_rows = L - N_META - seq
    xp = jnp.concatenate([full["meta_tokens"], x[0], jnp.zeros((pad_rows, D_MODEL), F32)], axis=0)
    tgt = jnp.concatenate([jnp.zeros((N_META, D_MODEL), F32), loss_target[0], jnp.zeros((pad_rows, D_MODEL), F32)], axis=0)

    saved = []
    h_in, h_in_b = xp, xp.astype(BF16)
    for l in range(DEPTH):
        lw = _layer_weights(full, l)
        wc = full["w_conv"][l].T
        wblk = _pool_blockdiag(w_pool[l]).astype(BF16)
        ps = pool_scale[l][None, :]
        gn = hg_norm_g[l][None, :]
        wf = full["w_ffn_conv"][l].T
        bf = b_ffn_conv[l][None, :]
        hh = _matmul(h_in_b, lw["w_hg"], "nn", F32, f"fwd_hg_{l}")
        hc = _matmul(h_in_b, lw["w_cp"], "nn", F32, f"fwd_cp_{l}")
        y_hg, o_raw, states, amat = _hgrn_fwd(hh, hg_lower_bounds, gn, l, f"hgrn_fwd_{l}")
        y_cp = _cp_fwd(hc, wc, wblk, ps, f"convpool_fwd_{l}")
        cat = jnp.concatenate([y_hg, y_cp], axis=1)
        mix = _matmul(cat, lw["w_o"], "nn", F32, f"fwd_o_{l}")
        x1, x1_b = _ln_fwd(h_in, mix, ln1_g[l][None, :], ln1_b[l][None, :], f"ln1_fwd_{l}")
        up = _matmul(x1_b, lw["w_up"], "nn", BF16, f"fwd_up_{l}")
        a = _ffn_act_fwd(up, wf, bf, f"ffn_fwd_{l}")
        ffn = _matmul(a, lw["w_down"], "nn", F32, f"fwd_down_{l}")
        x2, x2_b = _ln_fwd(x1, ffn, ln2_g[l][None, :], ln2_b[l][None, :], f"ln2_fwd_{l}")
        saved.append(dict(lw=lw, wc=wc, wblk=wblk, ps=ps, gn=gn, wf=wf, bf=bf, x=h_in, x_b=h_in_b, hh=hh, hc=hc,
                          o_raw=o_raw, states=states, amat=amat, cat=cat, mix=mix, x1=x1, x1_b=x1_b, up=up, a=a, ffn=ffn))
        h_in, h_in_b = x2, x2_b

    dy, loss_part = _loss_head(h_in, tgt, seq)
    loss = lax.psum(loss_part[0, 0], ("x", "y", "c"))

    G = {}
    per_layer = {n: [None] * DEPTH for n in ("w_conv", "w_pool", "pool_scale", "hg_norm_g", "ln1_g", "ln1_b",
                                             "w_ffn_conv", "b_ffn_conv", "ln2_g", "ln2_b")}
    chunks = {n: [None] * DEPTH for n in BIG}
    dlb_total = jnp.zeros((DEPTH, HG_W), F32)
    for l in reversed(range(DEPTH)):
        s = saved[l]
        lw = s["lw"]
        dz2, dz2_b, dg2, db2 = _ln_bwd(s["x1"], s["ffn"], dy, ln2_g[l][None, :], f"ln2_bwd_{l}")
        da = _matmul(dz2_b, lw["w_down"], "nt", BF16, f"bwd_da_{l}")
        d_w_down = _matmul(s["a"], dz2_b, "tn", BF16, f"wgrad_down_{l}")
        dup, dwf, dbf = _ffn_act_bwd(s["up"], da, s["wf"], s["bf"], f"ffn_bwd_{l}")
        dx1 = _matmul(dup, lw["w_up"], "nt", F32, f"bwd_dx1_{l}", res=dz2, alpha=ALPHA)
        d_w_up = _matmul(s["x1_b"], dup, "tn", BF16, f"wgrad_up_{l}")
        dz1, dz1_b, dg1, db1 = _ln_bwd(s["x"], s["mix"], dx1, ln1_g[l][None, :], f"ln1_bwd_{l}")
        dcat = _matmul(dz1_b, lw["w_o"], "nt", F32, f"bwd_dcat_{l}")
        d_w_o = _matmul(s["cat"], dz1_b, "tn", BF16, f"wgrad_o_{l}")
        dhh, dlb, dgn = _hgrn_bwd(s["hh"], s["o_raw"], s["states"], s["amat"], dcat, hg_lower_bounds, s["gn"], l,
                                  f"hgrn_bwd_{l}")
        dhc, dwc, dwblk, dps = _cp_bwd(s["hc"], dcat, s["wc"], s["wblk"], s["ps"], f"convpool_bwd_{l}")
        dx_a = _matmul(dhh, lw["w_hg"], "nt", F32, f"bwd_dx_hg_{l}", res=dz1, alpha=ALPHA)
        dx = _matmul(dhc, lw["w_cp"], "nt", F32, f"bwd_dx_cp_{l}", res=dx_a, alpha=1.0)
        d_w_hg = _matmul(s["x_b"], dhh, "tn", BF16, f"wgrad_hg_{l}")
        d_w_cp = _matmul(s["x_b"], dhc, "tn", BF16, f"wgrad_cp_{l}")
        d_w_in = jnp.concatenate([d_w_cp[:, 0:768], d_w_hg, d_w_cp[:, 768:1024]], axis=1)
        chunks["w_in"][l] = jnp.transpose(d_w_in.reshape(D_MODEL, N_DEV, -1), (1, 0, 2))
        chunks["w_o"][l] = jnp.concatenate([d_w_o[512:768], d_w_o[0:512], d_w_o[768:1024]], axis=0).reshape(N_DEV, -1, D_MODEL)
        chunks["w_up"][l] = jnp.transpose(d_w_up.reshape(D_MODEL, N_DEV, -1), (1, 0, 2))
        chunks["w_down"][l] = d_w_down.reshape(N_DEV, -1, D_MODEL)
        per_layer["w_conv"][l] = dwc.T
        per_layer["w_ffn_conv"][l] = dwf.T
        per_layer["b_ffn_conv"][l] = dbf[0]
        per_layer["w_pool"][l] = jnp.stack([dwblk[g * 64:(g + 1) * 64, g * 64:(g + 1) * 64] for g in range(4)], axis=0)
        per_layer["pool_scale"][l] = dps[0]
        per_layer["hg_norm_g"][l] = dgn[0]
        per_layer["ln1_g"][l], per_layer["ln1_b"][l] = dg1[0], db1[0]
        per_layer["ln2_g"][l], per_layer["ln2_b"][l] = dg2[0], db2[0]
        dlb_total = dlb_total + dlb
        dy = dx
    for n, parts in per_layer.items():
        G[n] = jnp.stack(parts, axis=0)
    G["hg_lower_bounds"] = dlb_total
    grad_x = dy[N_META:N_META + seq][None]

    def shard_major(g, lead):
        g = g.reshape(g.shape[:lead] + (N_DEV, -1) + g.shape[lead + 1:])
        g = jnp.moveaxis(g, lead, 0).reshape(N_DEV, -1)
        return jnp.pad(g, ((0, 0), (0, (-g.shape[1]) % LANES)))

    small_flat = jnp.concatenate([shard_major(dy[0:N_META], 1), shard_major(G["w_conv"], 1),
                                  shard_major(G["w_ffn_conv"], 1)], axis=1)
    w_small, _ = _pack([W[n] for n in SMALL_SHARDED], F32)
    small_rows = w_small.shape[0]
    small_chunks = jnp.pad(small_flat.reshape(N_DEV, -1, LANES),
                           ((0, 0), (0, small_rows - small_flat.shape[1] // LANES), (0, 0)))
    rep_pack, rep_offs = _pack([G[n] for n in REPLICATED], F32)
    received = _exchange_grads([chunks[n] for n in BIG], small_chunks, rep_pack, "exchange_grads")
    small_recv, rep_all = received[len(BIG)], received[len(BIG) + 1]

    res = {k: {} for k in ("grad", "delta", "new_m", "new_v")}
    kinds = ("grad", "delta", "new_m", "new_v")
    for n, gp in zip(BIG, received[:len(BIG)]):
        for kind, a in zip(kinds, _adamw_layers(gp, W[n], M[n], V[n], f"adamw_{n}")):
            res[kind][n] = a
    m_small, _ = _pack([M[n] for n in SMALL_SHARDED], F32)
    v_small, _ = _pack([V[n] for n in SMALL_SHARDED], F32)
    outs_small = _adamw(small_recv, w_small, m_small, v_small, "adamw_small_sharded")
    w_rep, _ = _pack([W[n] for n in REPLICATED], F32)
    m_rep, _ = _pack([M[n] for n in REPLICATED], F32)
    v_rep, _ = _pack([V[n] for n in REPLICATED], F32)
    outs_rep = _adamw(rep_all, w_rep, m_rep, v_rep, "adamw_replicated")
    for kind, b_sm, b_rep in zip(kinds, outs_small, outs_rep):
        for n, a in zip(SMALL_SHARDED, _unpack(b_sm, small_offs, [W[n].shape for n in SMALL_SHARDED])):
            res[kind][n] = a
        for n, a in zip(REPLICATED, _unpack(b_rep, rep_offs, [W[n].shape for n in REPLICATED])):
            res[kind][n] = a

    return (loss, grad_x, *[res["grad"][n] for n in WEIGHTS], *[res["delta"][n] for n in WEIGHTS],
            *[res["new_m"][n] for n in WEIGHTS], *[res["new_v"][n] for n in WEIGHTS])
```

```python
import jax
import jax.numpy as jnp
from jax import lax
from jax.experimental import pallas as pl
from jax.experimental.pallas import tpu as pltpu

F32 = jnp.float32
BF16 = jnp.bfloat16

N_DEV = 8
D_MODEL = 1024
N_META = 16
DEPTH = 2
CONV_W = 256
HG_W = 512
HG_D = 128
HG_HEADS = 4
POOL_W = 256
POOL_GROUP = 64
D_FF = 2816
ALPHA = (2 * DEPTH) ** 0.25
LN_EPS = 1e-5
RMS_EPS = 1e-6
F_FLOOR = 1e-30
Q_SCALE = HG_D ** -0.5
SUB = 16
SEQ_TILE = 192
FFN_TILE = 96
ROW_ALIGN = 192
LANES = 128
VMEM_LIMIT = 48 * 1024 * 1024
MATMUL_VMEM_BUDGET = 38 * 1024 * 1024

ADAM_LR = 0.001
ADAM_B1 = 0.9
ADAM_B2 = 0.999
ADAM_EPS = 1e-08
ADAM_WD = 0.01
ADAM_STEP = 10


def _tile(n, cap, mult):
    best = 0
    for t in range(mult, min(n, cap) + 1, mult):
        if n % t == 0:
            best = t
    assert best > 0, (n, cap, mult)
    return best


def _params(sem, vmem=VMEM_LIMIT):
    return pltpu.CompilerParams(dimension_semantics=sem, vmem_limit_bytes=vmem)


def _dnt(a, b):
    return lax.dot_general(a, b, (((1,), (1,)), ((), ())), preferred_element_type=F32)


def _dtn(a, b):
    return lax.dot_general(a, b, (((0,), (0,)), ((), ())), preferred_element_type=F32)


def _dnn(a, b):
    return jnp.dot(a, b, preferred_element_type=F32)


def _sigmoid(x):
    return 1.0 / (1.0 + jnp.exp(-x))


def _matmul(a, b, mode, out_dtype, name, res=None, alpha=1.0):
    if mode == "tn":
        K, M = a.shape
    else:
        M, K = a.shape
    N = b.shape[0] if mode == "nt" else b.shape[1]
    out_bytes = jnp.dtype(out_dtype).itemsize
    tn = _tile(N, 1536, LANES)
    tk = _tile(K, 1536, 16) if mode == "tn" else _tile(K, 2816, LANES)
    nk = K // tk
    use_acc = nk > 1 and out_dtype != F32
    tm = M
    for cap in (1536, 768, 384):
        tm = _tile(M, cap, 16)
        blocks = 2 * (a.dtype.itemsize * tm * tk + b.dtype.itemsize * tn * tk + out_bytes * tm * tn
                      + (4 * tm * tn if res is not None else 0)) + (4 * tm * tn if use_acc else 0)
        if blocks <= MATMUL_VMEM_BUDGET:
            break
    dims = {"nn": ((1,), (0,)), "nt": ((1,), (1,)), "tn": ((0,), (0,))}[mode]

    def body(*refs):
        a_ref, b_ref = refs[0], refs[1]
        r_ref = refs[2] if res is not None else None
        o_ref = refs[3] if res is not None else refs[2]
        acc = refs[-1] if use_acc else o_ref
        k = pl.program_id(2)
        p = lax.dot_general(a_ref[...].astype(BF16), b_ref[...].astype(BF16), (dims, ((), ())),
                            preferred_element_type=F32)

        def finish(r):
            if r_ref is not None:
                r = r + alpha * r_ref[...]
            o_ref[...] = r.astype(out_dtype)

        if nk == 1:
            finish(p)
        else:
            @pl.when(k == 0)
            def _():
                acc[...] = p

            @pl.when((k > 0) & (k < nk - 1))
            def _():
                acc[...] += p

            @pl.when(k == nk - 1)
            def _():
                finish(acc[...] + p)

    if mode == "tn":
        a_spec = pl.BlockSpec((tk, tm), lambda i, j, k: (k, i))
    else:
        a_spec = pl.BlockSpec((tm, tk), lambda i, j, k: (i, k))
    if mode == "nt":
        b_spec = pl.BlockSpec((tn, tk), lambda i, j, k: (j, k))
    else:
        b_spec = pl.BlockSpec((tk, tn), lambda i, j, k: (k, j))
    in_specs = [a_spec, b_spec]
    args = [a, b]
    if res is not None:
        in_specs.append(pl.BlockSpec((tm, tn), lambda i, j, k: (i, j)))
        args.append(res)
    return pl.pallas_call(
        body, name=name,
        grid=(M // tm, N // tn, nk),
        in_specs=in_specs,
        out_specs=pl.BlockSpec((tm, tn), lambda i, j, k: (i, j)),
        out_shape=jax.ShapeDtypeStruct((M, N), out_dtype),
        scratch_shapes=[pltpu.VMEM((tm, tn), F32)] if use_acc else [],
        compiler_params=_params(("parallel", "parallel", "arbitrary")),
    )(*args)


def _ln_fwd(x, f, g, b, name):
    L, D = x.shape
    tr = _tile(L, 768, 16)

    def body(x_ref, f_ref, g_ref, b_ref, y_ref, yb_ref):
        z = ALPHA * x_ref[...] + f_ref[...]
        mu = jnp.mean(z, axis=-1, keepdims=True)
        zc = z - mu
        var = jnp.mean(zc * zc, axis=-1, keepdims=True)
        y = zc * lax.rsqrt(var + LN_EPS) * g_ref[...] + b_ref[...]
        y_ref[...] = y
        yb_ref[...] = y.astype(BF16)

    row = pl.BlockSpec((tr, D), lambda i: (i, 0))
    vec = pl.BlockSpec((1, D), lambda i: (0, 0))
    return pl.pallas_call(
        body, name=name, grid=(L // tr,),
        in_specs=[row, row, vec, vec], out_specs=[row, row],
        out_shape=[jax.ShapeDtypeStruct((L, D), F32), jax.ShapeDtypeStruct((L, D), BF16)],
        compiler_params=_params(("parallel",)),
    )(x, f, g, b)


def _ln_bwd(x, f, dy, g, name):
    L, D = x.shape
    tr = _tile(L, 768, 16)

    def body(x_ref, f_ref, dy_ref, g_ref, dz_ref, dzb_ref, dg_ref, db_ref):
        @pl.when(pl.program_id(0) == 0)
        def _():
            dg_ref[...] = jnp.zeros_like(dg_ref)
            db_ref[...] = jnp.zeros_like(db_ref)

        z = ALPHA * x_ref[...] + f_ref[...]
        mu = jnp.mean(z, axis=-1, keepdims=True)
        zc = z - mu
        var = jnp.mean(zc * zc, axis=-1, keepdims=True)
        rstd = lax.rsqrt(var + LN_EPS)
        xhat = zc * rstd
        dy = dy_ref[...]
        dxh = dy * g_ref[...]
        m1 = jnp.mean(dxh, axis=-1, keepdims=True)
        m2 = jnp.mean(dxh * xhat, axis=-1, keepdims=True)
        dz = rstd * (dxh - m1 - xhat * m2)
        dz_ref[...] = dz
        dzb_ref[...] = dz.astype(BF16)
        dg_ref[...] += jnp.sum(dy * xhat, axis=0, keepdims=True)
        db_ref[...] += jnp.sum(dy, axis=0, keepdims=True)

    row = pl.BlockSpec((tr, D), lambda i: (i, 0))
    vec = pl.BlockSpec((1, D), lambda i: (0, 0))
    return pl.pallas_call(
        body, name=name, grid=(L // tr,),
        in_specs=[row, row, row, vec], out_specs=[row, row, vec, vec],
        out_shape=[jax.ShapeDtypeStruct((L, D), F32), jax.ShapeDtypeStruct((L, D), BF16),
                   jax.ShapeDtypeStruct((1, D), F32), jax.ShapeDtypeStruct((1, D), F32)],
        compiler_params=_params(("arbitrary",)),
    )(x, f, dy, g)


def _loss_head(y, tgt, seq):
    L, D = y.shape
    tr = _tile(L, 768, 16)

    def body(y_ref, t_ref, dy_ref, loss_ref):
        i = pl.program_id(0)

        @pl.when(i == 0)
        def _():
            loss_ref[...] = jnp.zeros_like(loss_ref)

        r = i * tr + lax.broadcasted_iota(jnp.int32, (tr, D), 0)
        valid = (r >= N_META) & (r < N_META + seq)
        e = jnp.where(valid, y_ref[...] - t_ref[...], 0.0)
        dy_ref[...] = e * (1.0 / D)
        s = jnp.sum(jnp.sum(e * e, axis=-1, keepdims=True), axis=0, keepdims=True)
        loss_ref[...] += (0.5 / D) * s

    row = pl.BlockSpec((tr, D), lambda i: (i, 0))
    return pl.pallas_call(
        body, name="loss_head", grid=(L // tr,),
        in_specs=[row, row], out_specs=[row, pl.BlockSpec((1, 1), lambda i: (0, 0))],
        out_shape=[jax.ShapeDtypeStruct((L, D), F32), jax.ShapeDtypeStruct((1, 1), F32)],
        compiler_params=_params(("arbitrary",)),
    )(y, tgt)


def _shift_down(x, prev, k):
    out = pltpu.roll(x, k, 0)
    row = lax.broadcasted_iota(jnp.int32, (8, x.shape[1]), 0)
    top = out[0:8]
    for r in range(k):
        top = jnp.where(row == r, prev[8 - k + r:8 - k + r + 1, :], top)
    return top if x.shape[0] == 8 else jnp.concatenate([top, out[8:]], axis=0)


def _shift_up(x, nxt, k):
    T = x.shape[0]
    out = pltpu.roll(x, T - k, 0)
    row = lax.broadcasted_iota(jnp.int32, (8, x.shape[1]), 0)
    bot = out[T - 8:T]
    for r in range(k):
        bot = jnp.where(row == 8 - k + r, nxt[r:r + 1, :], bot)
    return bot if T == 8 else jnp.concatenate([out[:T - 8], bot], axis=0)


def _conv3(x, prev, w, b):
    return w[2:3, :] * x + w[1:2, :] * _shift_down(x, prev, 1) + w[0:1, :] * _shift_down(x, prev, 2) + b


def _ffn_act_fwd(up, w, b, name):
    L, C = up.shape
    F = C // 2
    ts = FFN_TILE
    n = L // ts

    def body(up_ref, pv_ref, w_ref, b_ref, a_ref):
        i = pl.program_id(0)
        x = up_ref[...].astype(F32)
        prev = jnp.where(i > 0, pv_ref[...].astype(F32)[8:16], 0.0)
        u = _conv3(x, prev, w_ref[...], b_ref[...])
        gate = u[:, :F]
        a_ref[...] = (gate * _sigmoid(gate) * u[:, F:]).astype(BF16)

    return pl.pallas_call(
        body, name=name, grid=(n,),
        in_specs=[pl.BlockSpec((ts, C), lambda i: (i, 0)),
                  pl.BlockSpec((16, C), lambda i: (jnp.maximum(i * (ts // 16) - 1, 0), 0)),
                  pl.BlockSpec((3, C), lambda i: (0, 0)), pl.BlockSpec((1, C), lambda i: (0, 0))],
        out_specs=pl.BlockSpec((ts, F), lambda i: (i, 0)),
        out_shape=jax.ShapeDtypeStruct((L, F), BF16),
        compiler_params=_params(("parallel",)),
    )(up, up, w, b)


def _ffn_act_bwd(up, da, w, b, name):
    L, C = up.shape
    F = C // 2
    ts = FFN_TILE
    n = L // ts
    last16 = L // 16 - 1

    def du_of(u, da):
        gate, val = u[:, :F], u[:, F:]
        sg = _sigmoid(gate)
        dgate = da * val * (sg * (1.0 + gate * (1.0 - sg)))
        dval = da * (gate * sg)
        return jnp.concatenate([dgate, dval], axis=1)

    def body(up_ref, pv_ref, nx_ref, da_ref, dan_ref, w_ref, b_ref, dup_ref, dw_ref, db_ref):
        i = pl.program_id(0)

        @pl.when(i == 0)
        def _():
            dw_ref[...] = jnp.zeros_like(dw_ref)
            db_ref[...] = jnp.zeros_like(db_ref)

        w = w_ref[...]
        bias = b_ref[...]
        x = up_ref[...].astype(F32)
        prev = jnp.where(i > 0, pv_ref[...].astype(F32)[8:16], 0.0)
        u = _conv3(x, prev, w, bias)
        u_next = _conv3(nx_ref[...].astype(F32)[0:8], x[ts - 8:ts], w, bias)
        du = du_of(u, da_ref[...].astype(F32))
        dun = jnp.where(i < n - 1, du_of(u_next, dan_ref[...].astype(F32)[0:8]), 0.0)
        du1 = _shift_up(du, dun, 1)
        du2 = _shift_up(du, dun, 2)
        dup_ref[...] = (w[2:3, :] * du + w[1:2, :] * du1 + w[0:1, :] * du2).astype(BF16)
        dw_ref[...] += jnp.concatenate([jnp.sum(x * du2, axis=0, keepdims=True),
                                        jnp.sum(x * du1, axis=0, keepdims=True),
                                        jnp.sum(x * du, axis=0, keepdims=True)], axis=0)
        db_ref[...] += jnp.sum(du, axis=0, keepdims=True)

    prv = lambda i: (jnp.maximum(i * (ts // 16) - 1, 0), 0)
    nxt = lambda i: (jnp.minimum((i + 1) * (ts // 16), last16), 0)
    return pl.pallas_call(
        body, name=name, grid=(n,),
        in_specs=[pl.BlockSpec((ts, C), lambda i: (i, 0)), pl.BlockSpec((16, C), prv), pl.BlockSpec((16, C), nxt),
                  pl.BlockSpec((ts, F), lambda i: (i, 0)), pl.BlockSpec((16, F), nxt),
                  pl.BlockSpec((3, C), lambda i: (0, 0)), pl.BlockSpec((1, C), lambda i: (0, 0))],
        out_specs=[pl.BlockSpec((ts, C), lambda i: (i, 0)), pl.BlockSpec((3, C), lambda i: (0, 0)),
                   pl.BlockSpec((1, C), lambda i: (0, 0))],
        out_shape=[jax.ShapeDtypeStruct((L, C), BF16), jax.ShapeDtypeStruct((3, C), F32),
                   jax.ShapeDtypeStruct((1, C), F32)],
        compiler_params=_params(("arbitrary",)),
    )(up, up, up, da, da, w, b)


def _pool_window(ext, tile_rows, first_row, lead):
    T = ext.shape[0]
    sh = (lambda x, k: pltpu.roll(x, T - k, 0)) if lead else (lambda x, k: pltpu.roll(x, k, 0))
    r2 = ext + sh(ext, 1)
    r4 = r2 + sh(r2, 2)
    r8 = r4 + sh(r4, 4)
    r16 = r8 + sh(r8, 8)
    lo = 0 if lead else 16
    grp = lax.broadcasted_iota(jnp.int32, (tile_rows, POOL_W), 1) // POOL_GROUP
    pick = lambda a, b, c, d: jnp.where(grp == 0, a, jnp.where(grp == 1, b, jnp.where(grp == 2, c, d)))
    win = pick(r2[lo:lo + tile_rows], r4[lo:lo + tile_rows], r8[lo:lo + tile_rows], r16[lo:lo + tile_rows])
    return win, pick(2.0, 4.0, 8.0, 16.0)


def _pool_count(first_row, rows, wlen):
    t1 = (first_row + lax.broadcasted_iota(jnp.int32, (rows, POOL_W), 0) + 1).astype(F32)
    return jnp.minimum(t1, wlen)


def _cp_fwd(hc, wc, wblk, pscale, name):
    L = hc.shape[0]
    ts = SEQ_TILE
    n = L // ts

    def body(h_ref, hp_ref, wc_ref, wb_ref, ps_ref, y_ref):
        i = pl.program_id(0)
        h = h_ref[...]
        hp = jnp.where(i > 0, hp_ref[...], 0.0)
        cb, cc, cv, pv = h[:, 0:256], h[:, 256:512], h[:, 512:768], h[:, 768:1024]
        p = cc * cv
        pp = hp[8:16, 256:512] * hp[8:16, 512:768]
        w = wc_ref[...]
        conv = w[2:3, :] * p + w[1:2, :] * _shift_down(p, pp, 1) + w[0:1, :] * _shift_down(p, pp, 2)
        y_conv = cb * conv
        ext = jnp.concatenate([hp[:, 768:1024], pv], axis=0)
        win, wlen = _pool_window(ext, ts, i * ts, False)
        d = win / _pool_count(i * ts, ts, wlen) - pv
        y_pool = _dnn(d.astype(BF16), wb_ref[...]) * ps_ref[...]
        y_ref[...] = jnp.concatenate([y_conv, y_pool], axis=1).astype(BF16)

    return pl.pallas_call(
        body, name=name, grid=(n,),
        in_specs=[pl.BlockSpec((ts, 1024), lambda i: (i, 0)),
                  pl.BlockSpec((16, 1024), lambda i: (jnp.maximum(i * (ts // 16) - 1, 0), 0)),
                  pl.BlockSpec((3, 256), lambda i: (0, 0)), pl.BlockSpec((256, 256), lambda i: (0, 0)),
                  pl.BlockSpec((1, 256), lambda i: (0, 0))],
        out_specs=pl.BlockSpec((ts, 512), lambda i: (i, 0)),
        out_shape=jax.ShapeDtypeStruct((L, 512), BF16),
        compiler_params=_params(("parallel",)),
    )(hc, hc, wc, wblk, pscale)


def _cp_bwd(hc, dcat, wc, wblk, pscale, name):
    L = hc.shape[0]
    ts = SEQ_TILE
    n = L // ts
    last16 = L // 16 - 1

    def body(h_ref, hp_ref, hn_ref, dy_ref, dyn_ref, wc_ref, wb_ref, ps_ref,
             dh_ref, dwc_ref, dwb_ref, dps_ref):
        i = pl.program_id(0)

        @pl.when(i == 0)
        def _():
            dwc_ref[...] = jnp.zeros_like(dwc_ref)
            dwb_ref[...] = jnp.zeros_like(dwb_ref)
            dps_ref[...] = jnp.zeros_like(dps_ref)

        h = h_ref[...]
        hp = jnp.where(i > 0, hp_ref[...], 0.0)
        hn = hn_ref[...]
        dy = dy_ref[...]
        dyn = jnp.where(i < n - 1, dyn_ref[...], 0.0)
        cb, cc, cv, pv = h[:, 0:256], h[:, 256:512], h[:, 512:768], h[:, 768:1024]
        w = wc_ref[...]
        p = cc * cv
        pp = hp[8:16, 256:512] * hp[8:16, 512:768]
        p1 = _shift_down(p, pp, 1)
        p2 = _shift_down(p, pp, 2)
        conv = w[2:3, :] * p + w[1:2, :] * p1 + w[0:1, :] * p2
        dyc = dy[:, 0:256]
        dcb = dyc * conv
        dconv = dyc * cb
        dconv_n = dyn[0:8, 0:256] * hn[0:8, 0:256]
        dc1 = _shift_up(dconv, dconv_n, 1)
        dc2 = _shift_up(dconv, dconv_n, 2)
        dp = w[2:3, :] * dconv + w[1:2, :] * dc1 + w[0:1, :] * dc2
        dwc_ref[...] += jnp.concatenate([jnp.sum(p * dc2, axis=0, keepdims=True),
                                         jnp.sum(p * dc1, axis=0, keepdims=True),
                                         jnp.sum(p * dconv, axis=0, keepdims=True)], axis=0)
        ps = ps_ref[...]
        wb = wb_ref[...]
        ext = jnp.concatenate([hp[:, 768:1024], pv], axis=0)
        win, wlen = _pool_window(ext, ts, i * ts, False)
        d = win / _pool_count(i * ts, ts, wlen) - pv
        db = d.astype(BF16)
        dyp = dy[:, 256:512]
        dps_ref[...] += jnp.sum(dyp * _dnn(db, wb), axis=0, keepdims=True)
        dypre = (dyp * ps).astype(BF16)
        dwb_ref[...] += _dtn(db, dypre)
        dd = _dnt(dypre, wb)
        ddn = _dnt((dyn[:, 256:512] * ps).astype(BF16), wb)
        e = dd / _pool_count(i * ts, ts, wlen)
        en = ddn / _pool_count((i + 1) * ts, 16, wlen[0:16])
        lead, _ = _pool_window(jnp.concatenate([e, en], axis=0), ts, i * ts, True)
        dpv = lead - dd
        dh_ref[...] = jnp.concatenate([dcb, dp * cv, dp * cc, dpv], axis=1).astype(BF16)

    return pl.pallas_call(
        body, name=name, grid=(n,),
        in_specs=[pl.BlockSpec((ts, 1024), lambda i: (i, 0)),
                  pl.BlockSpec((16, 1024), lambda i: (jnp.maximum(i * (ts // 16) - 1, 0), 0)),
                  pl.BlockSpec((16, 1024), lambda i: (jnp.minimum((i + 1) * (ts // 16), last16), 0)),
                  pl.BlockSpec((ts, 512), lambda i: (i, 1)),
                  pl.BlockSpec((16, 512), lambda i: (jnp.minimum((i + 1) * (ts // 16), last16), 1)),
                  pl.BlockSpec((3, 256), lambda i: (0, 0)), pl.BlockSpec((256, 256), lambda i: (0, 0)),
                  pl.BlockSpec((1, 256), lambda i: (0, 0))],
        out_specs=[pl.BlockSpec((ts, 1024), lambda i: (i, 0)), pl.BlockSpec((3, 256), lambda i: (0, 0)),
                   pl.BlockSpec((256, 256), lambda i: (0, 0)), pl.BlockSpec((1, 256), lambda i: (0, 0))],
        out_shape=[jax.ShapeDtypeStruct((L, 1024), BF16), jax.ShapeDtypeStruct((3, 256), F32),
                   jax.ShapeDtypeStruct((256, 256), F32), jax.ShapeDtypeStruct((1, 256), F32)],
        compiler_params=_params(("arbitrary",)),
    )(hc, hc, hc, dcat, dcat, wc, wblk, pscale)


def _lower_bound(lb_ref, layer):
    b0, b1 = lb_ref[0:1, :], lb_ref[1:2, :]
    m = jnp.maximum(b0, b1)
    e0, e1 = jnp.exp(b0 - m), jnp.exp(b1 - m)
    p0, p1 = e0 / (e0 + e1), e1 / (e0 + e1)
    lb = (p0 - p0) if layer == 0 else ((p0 + p1) - p0)
    return lb, p0, p1


def _cumsum_rows(x, reverse=False):
    row = lax.broadcasted_iota(jnp.int32, x.shape, 0)
    for sh in (1, 2, 4, 8):
        if reverse:
            x = x + jnp.where(row < SUB - sh, pltpu.roll(x, SUB - sh, 0), 0.0)
        else:
            x = x + jnp.where(row >= sh, pltpu.roll(x, sh, 0), 0.0)
    return x


def _gates(fz, lb):
    sig = _sigmoid(fz)
    f = lb + (1.0 - lb) * sig
    g = jnp.log(jnp.maximum(f, F_FLOOR))
    k = (1.0 - lb) * (1.0 - sig)
    return sig, f, g, k


def _head(h):
    return slice(h * HG_D, (h + 1) * HG_D)


def _hgrn_fwd(hh, lbp, gnorm, layer, name):
    L = hh.shape[0]
    ts = SEQ_TILE
    n = L // ts
    nsub = ts // SUB

    def body(q_ref, f_ref, i_ref, g_ref, lb_ref, gn_ref, y_ref, o_ref, s_ref, a_ref, St):
        @pl.when(pl.program_id(0) == 0)
        def _():
            St[...] = jnp.zeros_like(St)

        lb, _, _ = _lower_bound(lb_ref, layer)
        gn = jnp.tile(gn_ref[...], (1, HG_HEADS))
        r16 = lax.broadcasted_iota(jnp.int32, (SUB, SUB), 0)
        c16 = lax.broadcasted_iota(jnp.int32, (SUB, SUB), 1)

        def block(j, carry):
            rows = pl.ds(pl.multiple_of(j * SUB, SUB), SUB)
            q = q_ref[rows, :] * Q_SCALE
            iv = i_ref[rows, :]
            gz = g_ref[rows, :]
            _, _, g, k = _gates(f_ref[rows, :], lb)
            G = _cumsum_rows(g)
            Gl = G[SUB - 1:SUB, :]
            qt = (q * jnp.exp(G)).astype(BF16)
            kd = (k * jnp.exp(Gl - G)).astype(BF16)
            eGl = jnp.exp(Gl)
            ib = iv.astype(BF16)
            A = [jnp.zeros((SUB, SUB), F32) for _ in range(HG_HEADS)]
            for s in range(SUB):
                P = q * jnp.exp(jnp.minimum(G - G[s:s + 1, :], 0.0)) * k[s:s + 1, :]
                for h in range(HG_HEADS):
                    A[h] = jnp.where(c16 == s, jnp.sum(P[:, _head(h)], axis=-1, keepdims=True), A[h])
            outs, ons, amats = [], [], []
            for h in range(HG_HEADS):
                sl = _head(h)
                Sb = St[h].astype(BF16)
                s_ref[j, sl, :] = Sb
                Am = jnp.where(r16 >= c16, A[h], 0.0)
                amats.append(Am)
                o = _dnt(qt[:, sl], Sb) + _dnn(Am.astype(BF16), ib[:, sl])
                St[h] = eGl[:, sl] * St[h] + _dtn(ib[:, sl], kd[:, sl])
                outs.append(o)
                ons.append(o * lax.rsqrt(jnp.mean(o * o, axis=-1, keepdims=True) + RMS_EPS))
            a_ref[rows, :] = jnp.concatenate(amats, axis=1)
            o_ref[rows, :] = jnp.concatenate(outs, axis=1)
            y = jnp.concatenate(ons, axis=1) * gn * (gz * _sigmoid(gz))
            y_ref[rows, :] = y.astype(BF16)
            return carry

        lax.fori_loop(0, nsub, block, 0, unroll=2)

    col = lambda c: pl.BlockSpec((ts, HG_W), lambda i: (i, c))
    return pl.pallas_call(
        body, name=name, grid=(n,),
        in_specs=[col(0), col(1), col(2), col(3), pl.BlockSpec((2, HG_W), lambda i: (0, 0)),
                  pl.BlockSpec((1, HG_D), lambda i: (0, 0))],
        out_specs=[pl.BlockSpec((ts, HG_W), lambda i: (i, 0)), pl.BlockSpec((ts, HG_W), lambda i: (i, 0)),
                   pl.BlockSpec((nsub, HG_W, HG_D), lambda i: (i, 0, 0)),
                   pl.BlockSpec((ts, HG_HEADS * SUB), lambda i: (i, 0))],
        out_shape=[jax.ShapeDtypeStruct((L, HG_W), BF16), jax.ShapeDtypeStruct((L, HG_W), F32),
                   jax.ShapeDtypeStruct((L // SUB, HG_W, HG_D), BF16),
                   jax.ShapeDtypeStruct((L, HG_HEADS * SUB), F32)],
        scratch_shapes=[pltpu.VMEM((HG_HEADS, HG_D, HG_D), F32)],
        compiler_params=_params(("arbitrary",)),
    )(hh, hh, hh, hh, lbp, gnorm)


def _hgrn_bwd(hh, o_raw, states, amat, dcat, lbp, gnorm, layer, name):
    L = hh.shape[0]
    ts = SEQ_TILE
    n = L // ts
    nsub = ts // SUB

    def body(q_ref, f_ref, i_ref, g_ref, o_ref, s_ref, a_ref, dy_ref, lb_ref, gn_ref,
             dh_ref, dlb_ref, dgn_ref, dSt, dlb_acc):
        step = pl.program_id(0)

        @pl.when(step == 0)
        def _():
            dSt[...] = jnp.zeros_like(dSt)
            dlb_acc[...] = jnp.zeros_like(dlb_acc)
            dgn_ref[...] = jnp.zeros_like(dgn_ref)

        lb, p0, p1 = _lower_bound(lb_ref, layer)
        gnh = gn_ref[...]
        gn = jnp.tile(gnh, (1, HG_HEADS))
        r16 = lax.broadcasted_iota(jnp.int32, (SUB, SUB), 0)
        c16 = lax.broadcasted_iota(jnp.int32, (SUB, SUB), 1)

        def block(jj, carry):
            j = nsub - 1 - jj
            rows = pl.ds(pl.multiple_of(j * SUB, SUB), SUB)
            q = q_ref[rows, :] * Q_SCALE
            iv = i_ref[rows, :]
            gz = g_ref[rows, :]
            o = o_ref[rows, :]
            dy = dy_ref[rows, :]
            sig, f, g, k = _gates(f_ref[rows, :], lb)
            G = _cumsum_rows(g)
            Gl = G[SUB - 1:SUB, :]
            eG = jnp.exp(G)
            edl = jnp.exp(Gl - G)
            eGl = jnp.exp(Gl)
            qt = (q * eG).astype(BF16)
            kd = (k * edl).astype(BF16)
            ib = iv.astype(BF16)
            sgz = _sigmoid(gz)
            sil = gz * sgz
            dyn = dy * sil
            on_parts, do_parts = [], []
            dgn = jnp.zeros((1, HG_D), F32)
            for h in range(HG_HEADS):
                sl = _head(h)
                oh = o[:, sl]
                rs = lax.rsqrt(jnp.mean(oh * oh, axis=-1, keepdims=True) + RMS_EPS)
                on = oh * rs
                dgn = dgn + jnp.sum(dyn[:, sl] * on, axis=0, keepdims=True)
                don = dyn[:, sl] * gnh
                do_parts.append(rs * (don - on * jnp.mean(don * on, axis=-1, keepdims=True)))
                on_parts.append(on)
            dgn_ref[...] += dgn
            on_all = jnp.concatenate(on_parts, axis=1)
            dgz = dy * on_all * gn * (sgz * (1.0 + gz * (1.0 - sgz)))
            do = jnp.concatenate(do_parts, axis=1)
            dob = do.astype(BF16)
            amat = a_ref[rows, :]
            dq_p, dk_p, di_p, tail_p = [], [], [], []
            for h in range(HG_HEADS):
                sl = _head(h)
                qh, kh, Gh = q[:, sl], k[:, sl], G[:, sl]
                Ap = jnp.where(r16 >= c16, _dnt(dob[:, sl], ib[:, sl]), 0.0)
                ApT = jnp.where(r16 <= c16, _dnt(ib[:, sl], dob[:, sl]), 0.0)
                dqh = jnp.zeros((SUB, HG_D), F32)
                dkh = jnp.zeros((SUB, HG_D), F32)
                for s in range(SUB):
                    dGs = Gh - Gh[s:s + 1, :]
                    e = jnp.exp(jnp.minimum(dGs, -dGs))
                    dqh = dqh + Ap[:, s:s + 1] * (e * kh[s:s + 1, :])
                    dkh = dkh + ApT[:, s:s + 1] * (e * qh[s:s + 1, :])
                Sb = s_ref[j, sl, :]
                dSb = dSt[h].astype(BF16)
                Am = amat[:, h * SUB:(h + 1) * SUB].astype(BF16)
                dq_p.append(dqh + eG[:, sl] * _dnn(dob[:, sl], Sb))
                dk_p.append(dkh + edl[:, sl] * _dnn(ib[:, sl], dSb))
                di_p.append(_dtn(Am, dob[:, sl]) + _dnt(kd[:, sl], dSb))
                St_end = eGl[:, sl] * Sb.astype(F32) + _dtn(ib[:, sl], kd[:, sl])
                tail_p.append(jnp.sum(dSt[h] * St_end, axis=0, keepdims=True))
                dSt[h] = eGl[:, sl] * dSt[h] + _dtn(dob[:, sl], qt[:, sl])
            dq = jnp.concatenate(dq_p, axis=1)
            dk = jnp.concatenate(dk_p, axis=1)
            di = jnp.concatenate(di_p, axis=1)
            dg = _cumsum_rows(q * dq - k * dk, reverse=True) + jnp.concatenate(tail_p, axis=1)
            df = jnp.where(f > F_FLOOR, dg / f, 0.0)
            dfk = df - dk
            dfz = (1.0 - lb) * dfk * sig * (1.0 - sig)
            dlb_acc[...] += jnp.sum(dfk * (1.0 - sig), axis=0, keepdims=True)
            dh_ref[rows, :] = jnp.concatenate([dq * Q_SCALE, dfz, di, dgz], axis=1).astype(BF16)
            return carry

        lax.fori_loop(0, nsub, block, 0)

        @pl.when(step == n - 1)
        def _():
            if layer == 0:
                dlb_ref[...] = jnp.zeros_like(dlb_ref)
            else:
                dz1 = p0 * p1 * dlb_acc[...]
                dlb_ref[...] = jnp.concatenate([-dz1, dz1], axis=0)

    rev = lambda i: n - 1 - i
    col = lambda c: pl.BlockSpec((ts, HG_W), lambda i: (rev(i), c))
    return pl.pallas_call(
        body, name=name, grid=(n,),
        in_specs=[col(0), col(1), col(2), col(3), col(0),
                  pl.BlockSpec((nsub, HG_W, HG_D), lambda i: (rev(i), 0, 0)),
                  pl.BlockSpec((ts, HG_HEADS * SUB), lambda i: (rev(i), 0)), col(0),
                  pl.BlockSpec((2, HG_W), lambda i: (0, 0)), pl.BlockSpec((1, HG_D), lambda i: (0, 0))],
        out_specs=[pl.BlockSpec((ts, 4 * HG_W), lambda i: (rev(i), 0)),
                   pl.BlockSpec((2, HG_W), lambda i: (0, 0)), pl.BlockSpec((1, HG_D), lambda i: (0, 0))],
        out_shape=[jax.ShapeDtypeStruct((L, 4 * HG_W), BF16), jax.ShapeDtypeStruct((2, HG_W), F32),
                   jax.ShapeDtypeStruct((1, HG_D), F32)],
        scratch_shapes=[pltpu.VMEM((HG_HEADS, HG_D, HG_D), F32), pltpu.VMEM((1, HG_W), F32)],
        compiler_params=_params(("arbitrary",)),
    )(hh, hh, hh, hh, o_raw, states, amat, dcat, lbp, gnorm)


def _adamw_body(gp_ref, w_ref, m_ref, v_ref, g_ref, d_ref, mo_ref, vo_ref):
    c1 = 1.0 - ADAM_B1 ** ADAM_STEP
    c2 = 1.0 - ADAM_B2 ** ADAM_STEP
    g = gp_ref[0].astype(F32)
    for k in range(1, N_DEV):
        g = g + gp_ref[k].astype(F32)
    mn = ADAM_B1 * m_ref[...] + (1.0 - ADAM_B1) * g
    vn = ADAM_B2 * v_ref[...] + (1.0 - ADAM_B2) * (g * g)
    m_hat = mn / c1
    v_hat = vn / c2
    g_ref[...] = g
    d_ref[...] = -ADAM_LR * (m_hat / (jnp.sqrt(v_hat) + ADAM_EPS) + ADAM_WD * w_ref[...])
    mo_ref[...] = mn
    vo_ref[...] = vn


def _adamw_layers(gparts, w, m, v, name):
    _, depth, R, C = gparts.shape
    tr = _tile(R, 256, 16)

    def body(*refs):
        _adamw_body(*refs)

    blk = pl.BlockSpec((None, tr, C), lambda l, i: (l, i, 0))
    shp = jax.ShapeDtypeStruct((depth, R, C), F32)
    return pl.pallas_call(
        body, name=name, grid=(depth, R // tr),
        in_specs=[pl.BlockSpec((N_DEV, None, tr, C), lambda l, i: (0, l, i, 0)), blk, blk, blk],
        out_specs=[blk, blk, blk, blk], out_shape=[shp, shp, shp, shp],
        compiler_params=_params(("parallel", "parallel")),
    )(gparts, w, m, v)


def _adamw(gparts, w, m, v, name):
    R = w.shape[0]
    tr = _tile(R, 1024, 16) if R % 16 == 0 else R

    def body(*refs):
        _adamw_body(*refs)

    row = pl.BlockSpec((tr, LANES), lambda i: (i, 0))
    shp = jax.ShapeDtypeStruct((R, LANES), F32)
    return pl.pallas_call(
        body, name=name, grid=(R // tr,),
        in_specs=[pl.BlockSpec((N_DEV, tr, LANES), lambda i: (0, i, 0)), row, row, row],
        out_specs=[row, row, row, row], out_shape=[shp, shp, shp, shp],
        compiler_params=_params(("parallel",)),
    )(gparts, w, m, v)


def _flip(coord, bit):
    return 1 - coord if bit else coord


def _gather_many(blocks, name):
    n = len(blocks)

    def body(*refs):
        x_refs, out_refs = refs[:n], refs[n:2 * n]
        send_sems, recv_sems, local_sems = refs[2 * n:]
        x, y, c = lax.axis_index("x"), lax.axis_index("y"), lax.axis_index("c")
        me, sibling = (x, y, c), (x, y, 1 - c)
        chips = [(1 - x, y), (x, 1 - y), (1 - x, 1 - y)]

        def slot(a, px, py, pc):
            return out_refs[a].at[4 * px + 2 * py + pc]

        def copy(a, k, blk, to, src=None):
            return pltpu.make_async_remote_copy(
                src_ref=slot(a, *blk) if src is None else src, dst_ref=slot(a, *blk),
                send_sem=send_sems.at[7 * a + k], recv_sem=recv_sems.at[7 * a + k],
                device_id=to, device_id_type=pl.DeviceIdType.MESH)

        mine = [pltpu.make_async_copy(x_refs[a], slot(a, *me), local_sems.at[a]) for a in range(n)]
        for cp in mine:
            cp.start()
        first = [copy(a, 0, me, sibling, src=x_refs[a]) for a in range(n)]
        for j, chip in enumerate(chips):
            first += [copy(a, 1 + j, me, (*chip, c), src=x_refs[a]) for a in range(n)]
        for cp in first:
            cp.start()
        passed = []
        for j, chip in enumerate(chips):
            for a in range(n):
                copy(a, 1 + j, (*chip, c), me).wait_recv()
                fwd = copy(a, 4 + j, (*chip, c), sibling)
                fwd.start()
                passed.append(fwd)
        for a in range(n):
            copy(a, 0, sibling, me).wait_recv()
        for j, chip in enumerate(chips):
            for a in range(n):
                copy(a, 4 + j, (*chip, 1 - c), me).wait_recv()
        for cp in first + passed:
            cp.wait_send()
        for cp in mine:
            cp.wait()

    hbm = pl.BlockSpec(memory_space=pl.ANY)
    return pl.pallas_call(
        body, name=name,
        out_shape=[jax.ShapeDtypeStruct((N_DEV,) + b.shape, b.dtype) for b in blocks],
        in_specs=[hbm] * n, out_specs=[hbm] * n,
        scratch_shapes=[pltpu.SemaphoreType.DMA((7 * n,)), pltpu.SemaphoreType.DMA((7 * n,)),
                        pltpu.SemaphoreType.DMA((n,))],
    )(*blocks)


def _gather_start(blocks, name):
    n = len(blocks)
    lands = [lax.empty((N_DEV,) + b.shape, b.dtype) for b in blocks]

    def body(*refs):
        x_refs, land_refs = refs[:n], refs[n:2 * n]
        send_sems, recv_sems, token = refs[2 * n], refs[2 * n + 1], refs[-1]
        x, y, c = lax.axis_index("x"), lax.axis_index("y"), lax.axis_index("c")
        me = 4 * x + 2 * y + c
        for a in range(n):
            for k in range(1, N_DEV):
                pltpu.make_async_remote_copy(
                    src_ref=x_refs[a], dst_ref=land_refs[a].at[me],
                    send_sem=send_sems.at[7 * a + k - 1], recv_sem=recv_sems.at[7 * a + k - 1],
                    device_id=(_flip(x, k & 4), _flip(y, k & 2), _flip(c, k & 1)),
                    device_id_type=pl.DeviceIdType.MESH).start()
        token[...] = jnp.zeros_like(token)

    hbm = pl.BlockSpec(memory_space=pltpu.HBM)
    sem = pl.BlockSpec(memory_space=pltpu.SEMAPHORE)
    outs = pl.pallas_call(
        body, name=name,
        out_shape=(pltpu.SemaphoreType.DMA((7 * n,)), pltpu.SemaphoreType.DMA((7 * n,)),
                   *[pltpu.HBM(b.shape, b.dtype) for b in blocks], *[pltpu.HBM(l.shape, l.dtype) for l in lands],
                   jax.ShapeDtypeStruct((8, LANES), F32)),
        in_specs=[hbm] * (2 * n),
        out_specs=(sem, sem, *[hbm] * (2 * n), pl.BlockSpec(memory_space=pltpu.VMEM)),
        input_output_aliases={i: 2 + i for i in range(2 * n)},
        compiler_params=pltpu.CompilerParams(has_side_effects=pltpu.SideEffectType.DATAFLOW_SIDE_EFFECTING),
    )(*[pltpu.with_memory_space_constraint(b, pltpu.HBM) for b in blocks],
      *[pltpu.with_memory_space_constraint(l, pltpu.HBM) for l in lands])
    return outs[0], outs[1], list(outs[2:2 + n]), list(outs[2 + n:2 + 2 * n]), outs[-1]


def _gather_wait(send_sems, recv_sems, blocks, lands, after, name):
    n = len(blocks)

    def body(*refs):
        x_refs, land_refs = refs[:n], refs[n:2 * n]
        send_sems, recv_sems = refs[2 * n], refs[2 * n + 1]
        x, y, c = lax.axis_index("x"), lax.axis_index("y"), lax.axis_index("c")
        for a in range(n):
            for k in range(1, N_DEV):
                px, py, pc = _flip(x, k & 4), _flip(y, k & 2), _flip(c, k & 1)
                copy = pltpu.make_async_remote_copy(
                    src_ref=x_refs[a], dst_ref=land_refs[a].at[4 * px + 2 * py + pc],
                    send_sem=send_sems.at[7 * a + k - 1], recv_sem=recv_sems.at[7 * a + k - 1],
                    device_id=(px, py, pc), device_id_type=pl.DeviceIdType.MESH)
                copy.wait_send()
                copy.wait_recv()

    hbm = pl.BlockSpec(memory_space=pltpu.HBM)
    sem = pl.BlockSpec(memory_space=pltpu.SEMAPHORE)
    outs = pl.pallas_call(
        body, name=name,
        out_shape=(*[pltpu.HBM(b.shape, b.dtype) for b in blocks], *[pltpu.HBM(l.shape, l.dtype) for l in lands]),
        in_specs=[hbm] * (2 * n) + [sem, sem, pl.BlockSpec(memory_space=pl.ANY)],
        out_specs=[hbm] * (2 * n),
        input_output_aliases={i: i for i in range(2 * n)},
        compiler_params=pltpu.CompilerParams(has_side_effects=pltpu.SideEffectType.DATAFLOW_SIDE_EFFECTING),
    )(*blocks, *lands, send_sems, recv_sems, after)
    return list(outs[n:])


def _exchange_grads(layer_chunks, small_chunks, rep_block, name):
    flows, inputs = [], []
    for p, per_layer in enumerate(layer_chunks):
        for l, arr in enumerate(per_layer):
            flows.append(("param", p, l))
            inputs.append(arr)
    flows += [("small",), ("rep",)]
    inputs += [small_chunks, rep_block]
    n_par = len(layer_chunks)
    n_in, n_out, nf = len(inputs), n_par + 2, len(flows)

    def body(*refs):
        in_refs, out_refs = refs[:n_in], refs[n_in:n_in + n_out]
        send_sems, recv_sems, local_sems = refs[n_in + n_out:]
        x, y, c = lax.axis_index("x"), lax.axis_index("y"), lax.axis_index("c")
        me = 4 * x + 2 * y + c

        def src(f, dev):
            return in_refs[f] if flows[f][0] == "rep" else in_refs[f].at[dev]

        def dst(f, dev):
            if flows[f][0] == "param":
                _, p, l = flows[f]
                return out_refs[p].at[dev, l]
            return out_refs[n_par + (0 if flows[f][0] == "small" else 1)].at[dev]

        mine = [pltpu.make_async_copy(src(f, me), dst(f, me), local_sems.at[f]) for f in range(nf)]
        for cp in mine:
            cp.start()
        copies = []
        for k in range(1, N_DEV):
            px, py, pc = _flip(x, k & 4), _flip(y, k & 2), _flip(c, k & 1)
            peer = 4 * px + 2 * py + pc
            for f in range(nf):
                sems = dict(send_sem=send_sems.at[7 * f + k - 1], recv_sem=recv_sems.at[7 * f + k - 1],
                            device_id=(px, py, pc), device_id_type=pl.DeviceIdType.MESH)
                send = pltpu.make_async_remote_copy(src_ref=src(f, peer), dst_ref=dst(f, me), **sems)
                recv = pltpu.make_async_remote_copy(src_ref=src(f, peer), dst_ref=dst(f, peer), **sems)
                send.start()
                copies.append((send, recv))
        for send, recv in copies:
            recv.wait_recv()
        for send, recv in copies:
            send.wait_send()
        for cp in mine:
            cp.wait()

    out_shape = [jax.ShapeDtypeStruct((N_DEV, len(pl_)) + pl_[0].shape[1:], pl_[0].dtype) for pl_ in layer_chunks]
    out_shape += [jax.ShapeDtypeStruct(small_chunks.shape, small_chunks.dtype),
                  jax.ShapeDtypeStruct((N_DEV,) + rep_block.shape, rep_block.dtype)]
    hbm = pl.BlockSpec(memory_space=pl.ANY)
    return pl.pallas_call(
        body, name=name, out_shape=out_shape,
        in_specs=[hbm] * n_in, out_specs=[hbm] * n_out,
        scratch_shapes=[pltpu.SemaphoreType.DMA((7 * nf,)), pltpu.SemaphoreType.DMA((7 * nf,)),
                        pltpu.SemaphoreType.DMA((nf,))],
    )(*inputs)


def _pack_rows(size):
    return -(-size // (8 * LANES)) * 8


def _pack(arrs, dtype):
    parts, offs, r = [], [], 0
    for a in arrs:
        flat = a.astype(dtype).reshape(-1)
        nrow = _pack_rows(flat.shape[0])
        flat = jnp.pad(flat, (0, nrow * LANES - flat.shape[0]))
        parts.append(flat.reshape(nrow, LANES))
        offs.append((r, nrow))
        r += nrow
    return jnp.concatenate(parts, axis=0), offs


def _unpack(buf, offs, shapes, lead=()):
    outs = []
    for (r, nrow), shp in zip(offs, shapes):
        size = 1
        for s in shp:
            size *= s
        flat = buf[..., r:r + nrow, :].reshape(lead + (nrow * LANES,))
        outs.append(flat[..., :size].reshape(lead + tuple(shp)))
    return outs


def _cols_from_shards(g, axis):
    return jnp.concatenate([g[j] for j in range(N_DEV)], axis=axis)


BIG = ("w_in", "w_o", "w_up", "w_down")
SMALL_SHARDED = ("meta_tokens", "w_conv", "w_ffn_conv")
REPLICATED = ("hg_lower_bounds", "w_pool", "pool_scale", "hg_norm_g", "ln1_g", "ln1_b", "b_ffn_conv", "ln2_g", "ln2_b")
WEIGHTS = ("meta_tokens", "hg_lower_bounds", "w_in", "w_conv", "w_pool", "pool_scale", "hg_norm_g", "w_o",
           "ln1_g", "ln1_b", "w_up", "w_ffn_conv", "b_ffn_conv", "w_down", "ln2_g", "ln2_b")


def _pool_blockdiag(w_pool_l):
    z = jnp.zeros((POOL_GROUP, POOL_GROUP), w_pool_l.dtype)
    rows = [jnp.concatenate([w_pool_l[g] if h == g else z for h in range(4)], axis=1) for g in range(4)]
    return jnp.concatenate(rows, axis=0)


def _layer_weights(g_in, g_o, g_up, g_down):
    w_in = jnp.transpose(g_in, (1, 0, 2)).reshape(D_MODEL, -1)
    w_o = g_o.reshape(-1, D_MODEL)
    return dict(
        w_hg=w_in[:, 768:2816],
        w_cp=jnp.concatenate([w_in[:, 0:768], w_in[:, 2816:3072]], axis=1),
        w_o=jnp.concatenate([w_o[256:768], w_o[0:256], w_o[768:1024]], axis=0),
        w_up=jnp.transpose(g_up, (1, 0, 2)).reshape(D_MODEL, -1), w_down=g_down.reshape(-1, D_MODEL),
    )


def kernel(x, meta_tokens, hg_lower_bounds, w_in, w_conv, w_pool, pool_scale, hg_norm_g, w_o, ln1_g, ln1_b, w_up, w_ffn_conv, b_ffn_conv, w_down, ln2_g, ln2_b, loss_target, m_meta_tokens, m_hg_lower_bounds, m_w_in, m_w_conv, m_w_pool, m_pool_scale, m_hg_norm_g, m_w_o, m_ln1_g, m_ln1_b, m_w_up, m_w_ffn_conv, m_b_ffn_conv, m_w_down, m_ln2_g, m_ln2_b, v_meta_tokens, v_hg_lower_bounds, v_w_in, v_w_conv, v_w_pool, v_pool_scale, v_hg_norm_g, v_w_o, v_ln1_g, v_ln1_b, v_w_up, v_w_ffn_conv, v_b_ffn_conv, v_w_down, v_ln2_g, v_ln2_b):
    W = dict(meta_tokens=meta_tokens, hg_lower_bounds=hg_lower_bounds, w_in=w_in, w_conv=w_conv, w_pool=w_pool,
             pool_scale=pool_scale, hg_norm_g=hg_norm_g, w_o=w_o, ln1_g=ln1_g, ln1_b=ln1_b, w_up=w_up,
             w_ffn_conv=w_ffn_conv, b_ffn_conv=b_ffn_conv, w_down=w_down, ln2_g=ln2_g, ln2_b=ln2_b)
    M = dict(meta_tokens=m_meta_tokens, hg_lower_bounds=m_hg_lower_bounds, w_in=m_w_in, w_conv=m_w_conv,
             w_pool=m_w_pool, pool_scale=m_pool_scale, hg_norm_g=m_hg_norm_g, w_o=m_w_o, ln1_g=m_ln1_g,
             ln1_b=m_ln1_b, w_up=m_w_up, w_ffn_conv=m_w_ffn_conv, b_ffn_conv=m_b_ffn_conv, w_down=m_w_down,
             ln2_g=m_ln2_g, ln2_b=m_ln2_b)
    V = dict(meta_tokens=v_meta_tokens, hg_lower_bounds=v_hg_lower_bounds, w_in=v_w_in, w_conv=v_w_conv,
             w_pool=v_w_pool, pool_scale=v_pool_scale, hg_norm_g=v_hg_norm_g, w_o=v_w_o, ln1_g=v_ln1_g,
             ln1_b=v_ln1_b, w_up=v_w_up, w_ffn_conv=v_w_ffn_conv, b_ffn_conv=v_b_ffn_conv, w_down=v_w_down,
             ln2_g=v_ln2_g, ln2_b=v_ln2_b)
    assert x.shape[0] == 1 and x.shape[2] == D_MODEL and w_in.shape[0] == DEPTH
    seq = x.shape[1]
    L = -(-(seq + N_META) // ROW_ALIGN) * ROW_ALIGN

    me = 4 * lax.axis_index("x") + 2 * lax.axis_index("y") + lax.axis_index("c")
    small_pack, small_offs = _pack([W[n] for n in SMALL_SHARDED], F32)
    shards = [[W[n][l].astype(BF16) for n in BIG] for l in range(DEPTH)]
    send1, recv1, thru1, lands1, token = _gather_start(shards[1], "gather_layer1_start")
    *gathered0, small_all = _gather_many(shards[0] + [small_pack], "gather_weights")
    full = {}
    for n, a in zip(SMALL_SHARDED, _unpack(small_all, small_offs, [W[n].shape for n in SMALL_SHARDED], (N_DEV,))):
        full[n] = _cols_from_shards(a, 1)
    lb_in = hg_lower_bounds + token[0, 0]

    pad_rows = L - N_META - seq
    xp = jnp.concatenate([full["meta_tokens"], x[0], jnp.zeros((pad_rows, D_MODEL), F32)], axis=0)
    tgt = jnp.concatenate([jnp.zeros((N_META, D_MODEL), F32), loss_target[0], jnp.zeros((pad_rows, D_MODEL), F32)], axis=0)

    saved = []
    h_in, h_in_b = xp, xp.astype(BF16)
    for l in range(DEPTH):
        if l == 0:
            gathered = gathered0
        else:
            lands = _gather_wait(send1, recv1, thru1, lands1, h_in_b, "gather_layer1_wait")
            gathered = [lax.dynamic_update_index_in_dim(z, b, me, 0) for z, b in zip(lands, shards[1])]
        lw = _layer_weights(*gathered)
        wc = full["w_conv"][l].T
        wblk = _pool_blockdiag(w_pool[l]).astype(BF16)
        ps = pool_scale[l][None, :]
        gn = hg_norm_g[l][None, :]
        wf = full["w_ffn_conv"][l].T
        bf = b_ffn_conv[l][None, :]
        hh = _matmul(h_in_b, lw["w_hg"], "nn", F32, f"fwd_hg_{l}")
        hc = _matmul(h_in_b, lw["w_cp"], "nn", F32, f"fwd_cp_{l}")
        y_hg, o_raw, states, amat = _hgrn_fwd(hh, lb_in if l == 0 else hg_lower_bounds, gn, l, f"hgrn_fwd_{l}")
        y_cp = _cp_fwd(hc, wc, wblk, ps, f"convpool_fwd_{l}")
        cat = jnp.concatenate([y_hg, y_cp], axis=1)
        mix = _matmul(cat, lw["w_o"], "nn", F32, f"fwd_o_{l}")
        x1, x1_b = _ln_fwd(h_in, mix, ln1_g[l][None, :], ln1_b[l][None, :], f"ln1_fwd_{l}")
        up = _matmul(x1_b, lw["w_up"], "nn", BF16, f"fwd_up_{l}")
        a = _ffn_act_fwd(up, wf, bf, f"ffn_fwd_{l}")
        ffn = _matmul(a, lw["w_down"], "nn", F32, f"fwd_down_{l}")
        x2, x2_b = _ln_fwd(x1, ffn, ln2_g[l][None, :], ln2_b[l][None, :], f"ln2_fwd_{l}")
        saved.append(dict(lw=lw, wc=wc, wblk=wblk, ps=ps, gn=gn, wf=wf, bf=bf, x=h_in, x_b=h_in_b, hh=hh, hc=hc,
                          o_raw=o_raw, states=states, amat=amat, cat=cat, mix=mix, x1=x1, x1_b=x1_b, up=up, a=a, ffn=ffn))
        h_in, h_in_b = x2, x2_b

    dy, loss_part = _loss_head(h_in, tgt, seq)
    loss = lax.psum(loss_part[0, 0], ("x", "y", "c"))

    G = {}
    per_layer = {n: [None] * DEPTH for n in ("w_conv", "w_pool", "pool_scale", "hg_norm_g", "ln1_g", "ln1_b",
                                             "w_ffn_conv", "b_ffn_conv", "ln2_g", "ln2_b")}
    chunks = {n: [None] * DEPTH for n in BIG}
    dlb_total = jnp.zeros((DEPTH, HG_W), F32)
    for l in reversed(range(DEPTH)):
        s = saved[l]
        lw = s["lw"]
        dz2, dz2_b, dg2, db2 = _ln_bwd(s["x1"], s["ffn"], dy, ln2_g[l][None, :], f"ln2_bwd_{l}")
        da = _matmul(dz2_b, lw["w_down"], "nt", BF16, f"bwd_da_{l}")
        d_w_down = _matmul(s["a"], dz2_b, "tn", BF16, f"wgrad_down_{l}")
        dup, dwf, dbf = _ffn_act_bwd(s["up"], da, s["wf"], s["bf"], f"ffn_bwd_{l}")
        dx1 = _matmul(dup, lw["w_up"], "nt", F32, f"bwd_dx1_{l}", res=dz2, alpha=ALPHA)
        d_w_up = _matmul(s["x1_b"], dup, "tn", BF16, f"wgrad_up_{l}")
        dz1, dz1_b, dg1, db1 = _ln_bwd(s["x"], s["mix"], dx1, ln1_g[l][None, :], f"ln1_bwd_{l}")
        dcat = _matmul(dz1_b, lw["w_o"], "nt", F32, f"bwd_dcat_{l}")
        d_w_o = _matmul(s["cat"], dz1_b, "tn", BF16, f"wgrad_o_{l}")
        dhh, dlb, dgn = _hgrn_bwd(s["hh"], s["o_raw"], s["states"], s["amat"], dcat, hg_lower_bounds, s["gn"], l,
                                  f"hgrn_bwd_{l}")
        dhc, dwc, dwblk, dps = _cp_bwd(s["hc"], dcat, s["wc"], s["wblk"], s["ps"], f"convpool_bwd_{l}")
        dx_a = _matmul(dhh, lw["w_hg"], "nt", F32, f"bwd_dx_hg_{l}", res=dz1, alpha=ALPHA)
        dx = _matmul(dhc, lw["w_cp"], "nt", F32, f"bwd_dx_cp_{l}", res=dx_a, alpha=1.0)
        d_w_hg = _matmul(s["x_b"], dhh, "tn", BF16, f"wgrad_hg_{l}")
        d_w_cp = _matmul(s["x_b"], dhc, "tn", BF16, f"wgrad_cp_{l}")
        d_w_in = jnp.concatenate([d_w_cp[:, 0:768], d_w_hg, d_w_cp[:, 768:1024]], axis=1)
        chunks["w_in"][l] = jnp.transpose(d_w_in.reshape(D_MODEL, N_DEV, -1), (1, 0, 2))
        chunks["w_o"][l] = jnp.concatenate([d_w_o[512:768], d_w_o[0:512], d_w_o[768:1024]], axis=0).reshape(N_DEV, -1, D_MODEL)
        chunks["w_up"][l] = jnp.transpose(d_w_up.reshape(D_MODEL, N_DEV, -1), (1, 0, 2))
        chunks["w_down"][l] = d_w_down.reshape(N_DEV, -1, D_MODEL)
        per_layer["w_conv"][l] = dwc.T
        per_layer["w_ffn_conv"][l] = dwf.T
        per_layer["b_ffn_conv"][l] = dbf[0]
        per_layer["w_pool"][l] = jnp.stack([dwblk[g * 64:(g + 1) * 64, g * 64:(g + 1) * 64] for g in range(4)], axis=0)
        per_layer["pool_scale"][l] = dps[0]
        per_layer["hg_norm_g"][l] = dgn[0]
        per_layer["ln1_g"][l], per_layer["ln1_b"][l] = dg1[0], db1[0]
        per_layer["ln2_g"][l], per_layer["ln2_b"][l] = dg2[0], db2[0]
        dlb_total = dlb_total + dlb
        dy = dx
    for n, parts in per_layer.items():
        G[n] = jnp.stack(parts, axis=0)
    G["hg_lower_bounds"] = dlb_total
    grad_x = dy[N_META:N_META + seq][None]

    def shard_major(g, lead):
        g = g.reshape(g.shape[:lead] + (N_DEV, -1) + g.shape[lead + 1:])
        g = jnp.moveaxis(g, lead, 0).reshape(N_DEV, -1)
        nrow = _pack_rows(g.shape[1])
        return jnp.pad(g, ((0, 0), (0, nrow * LANES - g.shape[1]))).reshape(N_DEV, nrow, LANES)

    small_chunks = jnp.concatenate([shard_major(dy[0:N_META], 1), shard_major(G["w_conv"], 1),
                                    shard_major(G["w_ffn_conv"], 1)], axis=1)
    w_small, _ = _pack([W[n] for n in SMALL_SHARDED], F32)
    rep_pack, rep_offs = _pack([G[n] for n in REPLICATED], F32)
    received = _exchange_grads([chunks[n] for n in BIG], small_chunks, rep_pack, "exchange_grads")
    small_recv, rep_all = received[len(BIG)], received[len(BIG) + 1]

    res = {k: {} for k in ("grad", "delta", "new_m", "new_v")}
    kinds = ("grad", "delta", "new_m", "new_v")
    for n, gp in zip(BIG, received[:len(BIG)]):
        for kind, a in zip(kinds, _adamw_layers(gp, W[n], M[n], V[n], f"adamw_{n}")):
            res[kind][n] = a
    m_small, _ = _pack([M[n] for n in SMALL_SHARDED], F32)
    v_small, _ = _pack([V[n] for n in SMALL_SHARDED], F32)
    outs_small = _adamw(small_recv, w_small, m_small, v_small, "adamw_small_sharded")
    w_rep, _ = _pack([W[n] for n in REPLICATED], F32)
    m_rep, _ = _pack([M[n] for n in REPLICATED], F32)
    v_rep, _ = _pack([V[n] for n in REPLICATED], F32)
    outs_rep = _adamw(rep_all, w_rep, m_rep, v_rep, "adamw_replicated")
    for kind, b_sm, b_rep in zip(kinds, outs_small, outs_rep):
        for n, a in zip(SMALL_SHARDED, _unpack(b_sm, small_offs, [W[n].shape for n in SMALL_SHARDED])):
            res[kind][n] = a
        for n, a in zip(REPLICATED, _unpack(b_rep, rep_offs, [W[n].shape for n in REPLICATED])):
            res[kind][n] = a

    return (loss, grad_x, *[res["grad"][n] for n in WEIGHTS], *[res["delta"][n] for n in WEIGHTS],
            *[res["new_m"][n] for n in WEIGHTS], *[res["new_v"][n] for n in WEIGHTS])
```

```python
import jax
import jax.numpy as jnp
from jax import lax
from jax.experimental import pallas as pl
from jax.experimental.pallas import tpu as pltpu

F32 = jnp.float32
BF16 = jnp.bfloat16

N_DEV = 8
D_MODEL = 1024
N_META = 16
DEPTH = 2
CONV_W = 256
HG_W = 512
HG_D = 128
HG_HEADS = 4
POOL_W = 256
POOL_GROUP = 64
D_FF = 2816
ALPHA = (2 * DEPTH) ** 0.25
LN_EPS = 1e-5
RMS_EPS = 1e-6
F_FLOOR = 1e-30
Q_SCALE = HG_D ** -0.5
SUB = 16
SEQ_TILE = 192
FFN_TILE = 96
ROW_ALIGN = 192
LANES = 128
VMEM_LIMIT = 48 * 1024 * 1024
MATMUL_VMEM_BUDGET = 38 * 1024 * 1024

ADAM_LR = 0.001
ADAM_B1 = 0.9
ADAM_B2 = 0.999
ADAM_EPS = 1e-08
ADAM_WD = 0.01
ADAM_STEP = 10


def _tile(n, cap, mult):
    best = 0
    for t in range(mult, min(n, cap) + 1, mult):
        if n % t == 0:
            best = t
    assert best > 0, (n, cap, mult)
    return best


def _params(sem, vmem=VMEM_LIMIT):
    return pltpu.CompilerParams(dimension_semantics=sem, vmem_limit_bytes=vmem)


def _dnt(a, b):
    return lax.dot_general(a, b, (((1,), (1,)), ((), ())), preferred_element_type=F32)


def _dtn(a, b):
    return lax.dot_general(a, b, (((0,), (0,)), ((), ())), preferred_element_type=F32)


def _dnn(a, b):
    return jnp.dot(a, b, preferred_element_type=F32)


def _sigmoid(x):
    return 1.0 / (1.0 + jnp.exp(-x))


def _matmul(a, b, mode, out_dtype, name, res=None, alpha=1.0):
    if mode == "tn":
        K, M = a.shape
    else:
        M, K = a.shape
    N = b.shape[0] if mode == "nt" else b.shape[1]
    out_bytes = jnp.dtype(out_dtype).itemsize
    tn = _tile(N, 1536, LANES)
    tk = _tile(K, 1536, 16) if mode == "tn" else _tile(K, 2816, LANES)
    nk = K // tk
    use_acc = nk > 1 and out_dtype != F32
    tm = M
    for cap in (1536, 768, 384):
        tm = _tile(M, cap, 16)
        blocks = 2 * (a.dtype.itemsize * tm * tk + b.dtype.itemsize * tn * tk + out_bytes * tm * tn
                      + (4 * tm * tn if res is not None else 0)) + (4 * tm * tn if use_acc else 0)
        if blocks <= MATMUL_VMEM_BUDGET:
            break
    dims = {"nn": ((1,), (0,)), "nt": ((1,), (1,)), "tn": ((0,), (0,))}[mode]

    def body(*refs):
        a_ref, b_ref = refs[0], refs[1]
        r_ref = refs[2] if res is not None else None
        o_ref = refs[3] if res is not None else refs[2]
        acc = refs[-1] if use_acc else o_ref
        k = pl.program_id(2)
        p = lax.dot_general(a_ref[...].astype(BF16), b_ref[...].astype(BF16), (dims, ((), ())),
                            preferred_element_type=F32)

        def finish(r):
            if r_ref is not None:
                r = r + alpha * r_ref[...]
            o_ref[...] = r.astype(out_dtype)

        if nk == 1:
            finish(p)
        else:
            @pl.when(k == 0)
            def _():
                acc[...] = p

            @pl.when((k > 0) & (k < nk - 1))
            def _():
                acc[...] += p

            @pl.when(k == nk - 1)
            def _():
                finish(acc[...] + p)

    if mode == "tn":
        a_spec = pl.BlockSpec((tk, tm), lambda i, j, k: (k, i))
    else:
        a_spec = pl.BlockSpec((tm, tk), lambda i, j, k: (i, k))
    if mode == "nt":
        b_spec = pl.BlockSpec((tn, tk), lambda i, j, k: (j, k))
    else:
        b_spec = pl.BlockSpec((tk, tn), lambda i, j, k: (k, j))
    in_specs = [a_spec, b_spec]
    args = [a, b]
    if res is not None:
        in_specs.append(pl.BlockSpec((tm, tn), lambda i, j, k: (i, j)))
        args.append(res)
    return pl.pallas_call(
        body, name=name,
        grid=(M // tm, N // tn, nk),
        in_specs=in_specs,
        out_specs=pl.BlockSpec((tm, tn), lambda i, j, k: (i, j)),
        out_shape=jax.ShapeDtypeStruct((M, N), out_dtype),
        scratch_shapes=[pltpu.VMEM((tm, tn), F32)] if use_acc else [],
        compiler_params=_params(("parallel", "parallel", "arbitrary")),
    )(*args)


def _ln_fwd(x, f, g, b, name):
    L, D = x.shape
    tr = _tile(L, 768, 16)

    def body(x_ref, f_ref, g_ref, b_ref, y_ref, yb_ref):
        z = ALPHA * x_ref[...] + f_ref[...]
        mu = jnp.mean(z, axis=-1, keepdims=True)
        zc = z - mu
        var = jnp.mean(zc * zc, axis=-1, keepdims=True)
        y = zc * lax.rsqrt(var + LN_EPS) * g_ref[...] + b_ref[...]
        y_ref[...] = y
        yb_ref[...] = y.astype(BF16)

    row = pl.BlockSpec((tr, D), lambda i: (i, 0))
    vec = pl.BlockSpec((1, D), lambda i: (0, 0))
    return pl.pallas_call(
        body, name=name, grid=(L // tr,),
        in_specs=[row, row, vec, vec], out_specs=[row, row],
        out_shape=[jax.ShapeDtypeStruct((L, D), F32), jax.ShapeDtypeStruct((L, D), BF16)],
        compiler_params=_params(("parallel",)),
    )(x, f, g, b)


def _ln_bwd(x, f, dy, g, name):
    L, D = x.shape
    tr = _tile(L, 768, 16)

    def body(x_ref, f_ref, dy_ref, g_ref, dz_ref, dzb_ref, dg_ref, db_ref):
        @pl.when(pl.program_id(0) == 0)
        def _():
            dg_ref[...] = jnp.zeros_like(dg_ref)
            db_ref[...] = jnp.zeros_like(db_ref)

        z = ALPHA * x_ref[...] + f_ref[...]
        mu = jnp.mean(z, axis=-1, keepdims=True)
        zc = z - mu
        var = jnp.mean(zc * zc, axis=-1, keepdims=True)
        rstd = lax.rsqrt(var + LN_EPS)
        xhat = zc * rstd
        dy = dy_ref[...]
        dxh = dy * g_ref[...]
        m1 = jnp.mean(dxh, axis=-1, keepdims=True)
        m2 = jnp.mean(dxh * xhat, axis=-1, keepdims=True)
        dz = rstd * (dxh - m1 - xhat * m2)
        dz_ref[...] = dz
        dzb_ref[...] = dz.astype(BF16)
        dg_ref[...] += jnp.sum(dy * xhat, axis=0, keepdims=True)
        db_ref[...] += jnp.sum(dy, axis=0, keepdims=True)

    row = pl.BlockSpec((tr, D), lambda i: (i, 0))
    vec = pl.BlockSpec((1, D), lambda i: (0, 0))
    return pl.pallas_call(
        body, name=name, grid=(L // tr,),
        in_specs=[row, row, row, vec], out_specs=[row, row, vec, vec],
        out_shape=[jax.ShapeDtypeStruct((L, D), F32), jax.ShapeDtypeStruct((L, D), BF16),
                   jax.ShapeDtypeStruct((1, D), F32), jax.ShapeDtypeStruct((1, D), F32)],
        compiler_params=_params(("arbitrary",)),
    )(x, f, dy, g)


def _loss_head(y, tgt, seq):
    L, D = y.shape
    tr = _tile(L, 768, 16)

    def body(y_ref, t_ref, dy_ref, loss_ref):
        i = pl.program_id(0)

        @pl.when(i == 0)
        def _():
            loss_ref[...] = jnp.zeros_like(loss_ref)

        r = i * tr + lax.broadcasted_iota(jnp.int32, (tr, D), 0)
        valid = (r >= N_META) & (r < N_META + seq)
        e = jnp.where(valid, y_ref[...] - t_ref[...], 0.0)
        dy_ref[...] = e * (1.0 / D)
        s = jnp.sum(jnp.sum(e * e, axis=-1, keepdims=True), axis=0, keepdims=True)
        loss_ref[...] += (0.5 / D) * s

    row = pl.BlockSpec((tr, D), lambda i: (i, 0))
    return pl.pallas_call(
        body, name="loss_head", grid=(L // tr,),
        in_specs=[row, row], out_specs=[row, pl.BlockSpec((1, 1), lambda i: (0, 0))],
        out_shape=[jax.ShapeDtypeStruct((L, D), F32), jax.ShapeDtypeStruct((1, 1), F32)],
        compiler_params=_params(("arbitrary",)),
    )(y, tgt)


def _shift_down(x, prev, k):
    out = pltpu.roll(x, k, 0)
    row = lax.broadcasted_iota(jnp.int32, (8, x.shape[1]), 0)
    top = out[0:8]
    for r in range(k):
        top = jnp.where(row == r, prev[8 - k + r:8 - k + r + 1, :], top)
    return top if x.shape[0] == 8 else jnp.concatenate([top, out[8:]], axis=0)


def _shift_up(x, nxt, k):
    T = x.shape[0]
    out = pltpu.roll(x, T - k, 0)
    row = lax.broadcasted_iota(jnp.int32, (8, x.shape[1]), 0)
    bot = out[T - 8:T]
    for r in range(k):
        bot = jnp.where(row == 8 - k + r, nxt[r:r + 1, :], bot)
    return bot if T == 8 else jnp.concatenate([out[:T - 8], bot], axis=0)


def _conv3(x, prev, w, b):
    return w[2:3, :] * x + w[1:2, :] * _shift_down(x, prev, 1) + w[0:1, :] * _shift_down(x, prev, 2) + b


def _ffn_act_fwd(up, w, b, name):
    L, C = up.shape
    F = C // 2
    ts = FFN_TILE
    n = L // ts

    def body(up_ref, pv_ref, w_ref, b_ref, a_ref):
        i = pl.program_id(0)
        x = up_ref[...].astype(F32)
        prev = jnp.where(i > 0, pv_ref[...].astype(F32)[8:16], 0.0)
        u = _conv3(x, prev, w_ref[...], b_ref[...])
        gate = u[:, :F]
        a_ref[...] = (gate * _sigmoid(gate) * u[:, F:]).astype(BF16)

    return pl.pallas_call(
        body, name=name, grid=(n,),
        in_specs=[pl.BlockSpec((ts, C), lambda i: (i, 0)),
                  pl.BlockSpec((16, C), lambda i: (jnp.maximum(i * (ts // 16) - 1, 0), 0)),
                  pl.BlockSpec((3, C), lambda i: (0, 0)), pl.BlockSpec((1, C), lambda i: (0, 0))],
        out_specs=pl.BlockSpec((ts, F), lambda i: (i, 0)),
        out_shape=jax.ShapeDtypeStruct((L, F), BF16),
        compiler_params=_params(("parallel",)),
    )(up, up, w, b)


def _ffn_act_bwd(up, da, w, b, name):
    L, C = up.shape
    F = C // 2
    ts = FFN_TILE
    n = L // ts
    last16 = L // 16 - 1

    def du_of(u, da):
        gate, val = u[:, :F], u[:, F:]
        sg = _sigmoid(gate)
        dgate = da * val * (sg * (1.0 + gate * (1.0 - sg)))
        dval = da * (gate * sg)
        return jnp.concatenate([dgate, dval], axis=1)

    def body(up_ref, pv_ref, nx_ref, da_ref, dan_ref, w_ref, b_ref, dup_ref, dw_ref, db_ref):
        i = pl.program_id(0)

        @pl.when(i == 0)
        def _():
            dw_ref[...] = jnp.zeros_like(dw_ref)
            db_ref[...] = jnp.zeros_like(db_ref)

        w = w_ref[...]
        bias = b_ref[...]
        x = up_ref[...].astype(F32)
        prev = jnp.where(i > 0, pv_ref[...].astype(F32)[8:16], 0.0)
        u = _conv3(x, prev, w, bias)
        u_next = _conv3(nx_ref[...].astype(F32)[0:8], x[ts - 8:ts], w, bias)
        du = du_of(u, da_ref[...].astype(F32))
        dun = jnp.where(i < n - 1, du_of(u_next, dan_ref[...].astype(F32)[0:8]), 0.0)
        du1 = _shift_up(du, dun, 1)
        du2 = _shift_up(du, dun, 2)
        dup_ref[...] = (w[2:3, :] * du + w[1:2, :] * du1 + w[0:1, :] * du2).astype(BF16)
        dw_ref[...] += jnp.concatenate([jnp.sum(x * du2, axis=0, keepdims=True),
                                        jnp.sum(x * du1, axis=0, keepdims=True),
                                        jnp.sum(x * du, axis=0, keepdims=True)], axis=0)
        db_ref[...] += jnp.sum(du, axis=0, keepdims=True)

    prv = lambda i: (jnp.maximum(i * (ts // 16) - 1, 0), 0)
    nxt = lambda i: (jnp.minimum((i + 1) * (ts // 16), last16), 0)
    return pl.pallas_call(
        body, name=name, grid=(n,),
        in_specs=[pl.BlockSpec((ts, C), lambda i: (i, 0)), pl.BlockSpec((16, C), prv), pl.BlockSpec((16, C), nxt),
                  pl.BlockSpec((ts, F), lambda i: (i, 0)), pl.BlockSpec((16, F), nxt),
                  pl.BlockSpec((3, C), lambda i: (0, 0)), pl.BlockSpec((1, C), lambda i: (0, 0))],
        out_specs=[pl.BlockSpec((ts, C), lambda i: (i, 0)), pl.BlockSpec((3, C), lambda i: (0, 0)),
                   pl.BlockSpec((1, C), lambda i: (0, 0))],
        out_shape=[jax.ShapeDtypeStruct((L, C), BF16), jax.ShapeDtypeStruct((3, C), F32),
                   jax.ShapeDtypeStruct((1, C), F32)],
        compiler_params=_params(("arbitrary",)),
    )(up, up, up, da, da, w, b)


def _pool_window(ext, tile_rows, first_row, lead):
    T = ext.shape[0]
    sh = (lambda x, k: pltpu.roll(x, T - k, 0)) if lead else (lambda x, k: pltpu.roll(x, k, 0))
    r2 = ext + sh(ext, 1)
    r4 = r2 + sh(r2, 2)
    r8 = r4 + sh(r4, 4)
    r16 = r8 + sh(r8, 8)
    lo = 0 if lead else 16
    grp = lax.broadcasted_iota(jnp.int32, (tile_rows, POOL_W), 1) // POOL_GROUP
    pick = lambda a, b, c, d: jnp.where(grp == 0, a, jnp.where(grp == 1, b, jnp.where(grp == 2, c, d)))
    win = pick(r2[lo:lo + tile_rows], r4[lo:lo + tile_rows], r8[lo:lo + tile_rows], r16[lo:lo + tile_rows])
    return win, pick(2.0, 4.0, 8.0, 16.0)


def _pool_count(first_row, rows, wlen):
    t1 = (first_row + lax.broadcasted_iota(jnp.int32, (rows, POOL_W), 0) + 1).astype(F32)
    return jnp.minimum(t1, wlen)


def _cp_fwd(hc, wc, wblk, pscale, name):
    L = hc.shape[0]
    ts = SEQ_TILE
    n = L // ts

    def body(h_ref, hp_ref, wc_ref, wb_ref, ps_ref, y_ref):
        i = pl.program_id(0)
        h = h_ref[...]
        hp = jnp.where(i > 0, hp_ref[...], 0.0)
        cb, cc, cv, pv = h[:, 0:256], h[:, 256:512], h[:, 512:768], h[:, 768:1024]
        p = cc * cv
        pp = hp[8:16, 256:512] * hp[8:16, 512:768]
        w = wc_ref[...]
        conv = w[2:3, :] * p + w[1:2, :] * _shift_down(p, pp, 1) + w[0:1, :] * _shift_down(p, pp, 2)
        y_conv = cb * conv
        ext = jnp.concatenate([hp[:, 768:1024], pv], axis=0)
        win, wlen = _pool_window(ext, ts, i * ts, False)
        d = win / _pool_count(i * ts, ts, wlen) - pv
        y_pool = _dnn(d.astype(BF16), wb_ref[...]) * ps_ref[...]
        y_ref[...] = jnp.concatenate([y_conv, y_pool], axis=1).astype(BF16)

    return pl.pallas_call(
        body, name=name, grid=(n,),
        in_specs=[pl.BlockSpec((ts, 1024), lambda i: (i, 0)),
                  pl.BlockSpec((16, 1024), lambda i: (jnp.maximum(i * (ts // 16) - 1, 0), 0)),
                  pl.BlockSpec((3, 256), lambda i: (0, 0)), pl.BlockSpec((256, 256), lambda i: (0, 0)),
                  pl.BlockSpec((1, 256), lambda i: (0, 0))],
        out_specs=pl.BlockSpec((ts, 512), lambda i: (i, 0)),
        out_shape=jax.ShapeDtypeStruct((L, 512), BF16),
        compiler_params=_params(("parallel",)),
    )(hc, hc, wc, wblk, pscale)


def _cp_bwd(hc, dcat, wc, wblk, pscale, name):
    L = hc.shape[0]
    ts = SEQ_TILE
    n = L // ts
    last16 = L // 16 - 1

    def body(h_ref, hp_ref, hn_ref, dy_ref, dyn_ref, wc_ref, wb_ref, ps_ref,
             dh_ref, dwc_ref, dwb_ref, dps_ref):
        i = pl.program_id(0)

        @pl.when(i == 0)
        def _():
            dwc_ref[...] = jnp.zeros_like(dwc_ref)
            dwb_ref[...] = jnp.zeros_like(dwb_ref)
            dps_ref[...] = jnp.zeros_like(dps_ref)

        h = h_ref[...]
        hp = jnp.where(i > 0, hp_ref[...], 0.0)
        hn = hn_ref[...]
        dy = dy_ref[...]
        dyn = jnp.where(i < n - 1, dyn_ref[...], 0.0)
        cb, cc, cv, pv = h[:, 0:256], h[:, 256:512], h[:, 512:768], h[:, 768:1024]
        w = wc_ref[...]
        p = cc * cv
        pp = hp[8:16, 256:512] * hp[8:16, 512:768]
        p1 = _shift_down(p, pp, 1)
        p2 = _shift_down(p, pp, 2)
        conv = w[2:3, :] * p + w[1:2, :] * p1 + w[0:1, :] * p2
        dyc = dy[:, 0:256]
        dcb = dyc * conv
        dconv = dyc * cb
        dconv_n = dyn[0:8, 0:256] * hn[0:8, 0:256]
        dc1 = _shift_up(dconv, dconv_n, 1)
        dc2 = _shift_up(dconv, dconv_n, 2)
        dp = w[2:3, :] * dconv + w[1:2, :] * dc1 + w[0:1, :] * dc2
        dwc_ref[...] += jnp.concatenate([jnp.sum(p * dc2, axis=0, keepdims=True),
                                         jnp.sum(p * dc1, axis=0, keepdims=True),
                                         jnp.sum(p * dconv, axis=0, keepdims=True)], axis=0)
        ps = ps_ref[...]
        wb = wb_ref[...]
        ext = jnp.concatenate([hp[:, 768:1024], pv], axis=0)
        win, wlen = _pool_window(ext, ts, i * ts, False)
        d = win / _pool_count(i * ts, ts, wlen) - pv
        db = d.astype(BF16)
        dyp = dy[:, 256:512]
        dps_ref[...] += jnp.sum(dyp * _dnn(db, wb), axis=0, keepdims=True)
        dypre = (dyp * ps).astype(BF16)
        dwb_ref[...] += _dtn(db, dypre)
        dd = _dnt(dypre, wb)
        ddn = _dnt((dyn[:, 256:512] * ps).astype(BF16), wb)
        e = dd / _pool_count(i * ts, ts, wlen)
        en = ddn / _pool_count((i + 1) * ts, 16, wlen[0:16])
        lead, _ = _pool_window(jnp.concatenate([e, en], axis=0), ts, i * ts, True)
        dpv = lead - dd
        dh_ref[...] = jnp.concatenate([dcb, dp * cv, dp * cc, dpv], axis=1).astype(BF16)

    return pl.pallas_call(
        body, name=name, grid=(n,),
        in_specs=[pl.BlockSpec((ts, 1024), lambda i: (i, 0)),
                  pl.BlockSpec((16, 1024), lambda i: (jnp.maximum(i * (ts // 16) - 1, 0), 0)),
                  pl.BlockSpec((16, 1024), lambda i: (jnp.minimum((i + 1) * (ts // 16), last16), 0)),
                  pl.BlockSpec((ts, 512), lambda i: (i, 1)),
                  pl.BlockSpec((16, 512), lambda i: (jnp.minimum((i + 1) * (ts // 16), last16), 1)),
                  pl.BlockSpec((3, 256), lambda i: (0, 0)), pl.BlockSpec((256, 256), lambda i: (0, 0)),
                  pl.BlockSpec((1, 256), lambda i: (0, 0))],
        out_specs=[pl.BlockSpec((ts, 1024), lambda i: (i, 0)), pl.BlockSpec((3, 256), lambda i: (0, 0)),
                   pl.BlockSpec((256, 256), lambda i: (0, 0)), pl.BlockSpec((1, 256), lambda i: (0, 0))],
        out_shape=[jax.ShapeDtypeStruct((L, 1024), BF16), jax.ShapeDtypeStruct((3, 256), F32),
                   jax.ShapeDtypeStruct((256, 256), F32), jax.ShapeDtypeStruct((1, 256), F32)],
        compiler_params=_params(("arbitrary",)),
    )(hc, hc, hc, dcat, dcat, wc, wblk, pscale)


def _lower_bound(lb_ref, layer):
    b0, b1 = lb_ref[0:1, :], lb_ref[1:2, :]
    m = jnp.maximum(b0, b1)
    e0, e1 = jnp.exp(b0 - m), jnp.exp(b1 - m)
    p0, p1 = e0 / (e0 + e1), e1 / (e0 + e1)
    lb = (p0 - p0) if layer == 0 else ((p0 + p1) - p0)
    return lb, p0, p1


def _cumsum_rows(x, reverse=False):
    row = lax.broadcasted_iota(jnp.int32, x.shape, 0)
    for sh in (1, 2, 4, 8):
        if reverse:
            x = x + jnp.where(row < SUB - sh, pltpu.roll(x, SUB - sh, 0), 0.0)
        else:
            x = x + jnp.where(row >= sh, pltpu.roll(x, sh, 0), 0.0)
    return x


def _gates(fz, lb):
    sig = _sigmoid(fz)
    f = lb + (1.0 - lb) * sig
    g = jnp.log(jnp.maximum(f, F_FLOOR))
    k = (1.0 - lb) * (1.0 - sig)
    return sig, f, g, k


def _head(h):
    return slice(h * HG_D, (h + 1) * HG_D)


def _hgrn_fwd(hh, lbp, gnorm, layer, name):
    L = hh.shape[0]
    ts = SEQ_TILE
    n = L // ts
    nsub = ts // SUB

    def body(q_ref, f_ref, i_ref, g_ref, lb_ref, gn_ref, y_ref, o_ref, s_ref, a_ref, St):
        @pl.when(pl.program_id(0) == 0)
        def _():
            St[...] = jnp.zeros_like(St)

        lb, _, _ = _lower_bound(lb_ref, layer)
        gn = jnp.tile(gn_ref[...], (1, HG_HEADS))
        r16 = lax.broadcasted_iota(jnp.int32, (SUB, SUB), 0)
        c16 = lax.broadcasted_iota(jnp.int32, (SUB, SUB), 1)

        def block(j, carry):
            rows = pl.ds(pl.multiple_of(j * SUB, SUB), SUB)
            q = q_ref[rows, :] * Q_SCALE
            iv = i_ref[rows, :]
            gz = g_ref[rows, :]
            _, _, g, k = _gates(f_ref[rows, :], lb)
            G = _cumsum_rows(g)
            Gl = G[SUB - 1:SUB, :]
            qt = (q * jnp.exp(G)).astype(BF16)
            kd = (k * jnp.exp(Gl - G)).astype(BF16)
            eGl = jnp.exp(Gl)
            ib = iv.astype(BF16)
            A = [jnp.zeros((SUB, SUB), F32) for _ in range(HG_HEADS)]
            for s in range(SUB):
                P = q * jnp.exp(jnp.minimum(G - G[s:s + 1, :], 0.0)) * k[s:s + 1, :]
                for h in range(HG_HEADS):
                    A[h] = jnp.where(c16 == s, jnp.sum(P[:, _head(h)], axis=-1, keepdims=True), A[h])
            outs, ons, amats = [], [], []
            for h in range(HG_HEADS):
                sl = _head(h)
                Sb = St[h].astype(BF16)
                s_ref[j, sl, :] = Sb
                Am = jnp.where(r16 >= c16, A[h], 0.0)
                amats.append(Am)
                o = _dnt(qt[:, sl], Sb) + _dnn(Am.astype(BF16), ib[:, sl])
                St[h] = eGl[:, sl] * St[h] + _dtn(ib[:, sl], kd[:, sl])
                outs.append(o)
                ons.append(o * lax.rsqrt(jnp.mean(o * o, axis=-1, keepdims=True) + RMS_EPS))
            a_ref[rows, :] = jnp.concatenate(amats, axis=1)
            o_ref[rows, :] = jnp.concatenate(outs, axis=1)
            y = jnp.concatenate(ons, axis=1) * gn * (gz * _sigmoid(gz))
            y_ref[rows, :] = y.astype(BF16)
            return carry

        lax.fori_loop(0, nsub, block, 0, unroll=2)

    col = lambda c: pl.BlockSpec((ts, HG_W), lambda i: (i, c))
    return pl.pallas_call(
        body, name=name, grid=(n,),
        in_specs=[col(0), col(1), col(2), col(3), pl.BlockSpec((2, HG_W), lambda i: (0, 0)),
                  pl.BlockSpec((1, HG_D), lambda i: (0, 0))],
        out_specs=[pl.BlockSpec((ts, HG_W), lambda i: (i, 0)), pl.BlockSpec((ts, HG_W), lambda i: (i, 0)),
                   pl.BlockSpec((nsub, HG_W, HG_D), lambda i: (i, 0, 0)),
                   pl.BlockSpec((ts, HG_HEADS * SUB), lambda i: (i, 0))],
        out_shape=[jax.ShapeDtypeStruct((L, HG_W), BF16), jax.ShapeDtypeStruct((L, HG_W), F32),
                   jax.ShapeDtypeStruct((L // SUB, HG_W, HG_D), BF16),
                   jax.ShapeDtypeStruct((L, HG_HEADS * SUB), F32)],
        scratch_shapes=[pltpu.VMEM((HG_HEADS, HG_D, HG_D), F32)],
        compiler_params=_params(("arbitrary",)),
    )(hh, hh, hh, hh, lbp, gnorm)


def _hgrn_bwd(hh, o_raw, states, amat, dcat, lbp, gnorm, layer, name):
    L = hh.shape[0]
    ts = SEQ_TILE
    n = L // ts
    nsub = ts // SUB

    def body(q_ref, f_ref, i_ref, g_ref, o_ref, s_ref, a_ref, dy_ref, lb_ref, gn_ref,
             dh_ref, dlb_ref, dgn_ref, dSt, dlb_acc):
        step = pl.program_id(0)

        @pl.when(step == 0)
        def _():
            dSt[...] = jnp.zeros_like(dSt)
            dlb_acc[...] = jnp.zeros_like(dlb_acc)
            dgn_ref[...] = jnp.zeros_like(dgn_ref)

        lb, p0, p1 = _lower_bound(lb_ref, layer)
        gnh = gn_ref[...]
        gn = jnp.tile(gnh, (1, HG_HEADS))
        r16 = lax.broadcasted_iota(jnp.int32, (SUB, SUB), 0)
        c16 = lax.broadcasted_iota(jnp.int32, (SUB, SUB), 1)

        def block(jj, carry):
            j = nsub - 1 - jj
            rows = pl.ds(pl.multiple_of(j * SUB, SUB), SUB)
            q = q_ref[rows, :] * Q_SCALE
            iv = i_ref[rows, :]
            gz = g_ref[rows, :]
            o = o_ref[rows, :]
            dy = dy_ref[rows, :]
            sig, f, g, k = _gates(f_ref[rows, :], lb)
            G = _cumsum_rows(g)
            Gl = G[SUB - 1:SUB, :]
            eG = jnp.exp(G)
            edl = jnp.exp(Gl - G)
            eGl = jnp.exp(Gl)
            qt = (q * eG).astype(BF16)
            kd = (k * edl).astype(BF16)
            ib = iv.astype(BF16)
            sgz = _sigmoid(gz)
            sil = gz * sgz
            dyn = dy * sil
            on_parts, do_parts = [], []
            dgn = jnp.zeros((1, HG_D), F32)
            for h in range(HG_HEADS):
                sl = _head(h)
                oh = o[:, sl]
                rs = lax.rsqrt(jnp.mean(oh * oh, axis=-1, keepdims=True) + RMS_EPS)
                on = oh * rs
                dgn = dgn + jnp.sum(dyn[:, sl] * on, axis=0, keepdims=True)
                don = dyn[:, sl] * gnh
                do_parts.append(rs * (don - on * jnp.mean(don * on, axis=-1, keepdims=True)))
                on_parts.append(on)
            dgn_ref[...] += dgn
            on_all = jnp.concatenate(on_parts, axis=1)
            dgz = dy * on_all * gn * (sgz * (1.0 + gz * (1.0 - sgz)))
            do = jnp.concatenate(do_parts, axis=1)
            dob = do.astype(BF16)
            amat = a_ref[rows, :]
            dq_p, dk_p, di_p, tail_p = [], [], [], []
            for h in range(HG_HEADS):
                sl = _head(h)
                qh, kh, Gh = q[:, sl], k[:, sl], G[:, sl]
                Ap = jnp.where(r16 >= c16, _dnt(dob[:, sl], ib[:, sl]), 0.0)
                ApT = jnp.where(r16 <= c16, _dnt(ib[:, sl], dob[:, sl]), 0.0)
                dqh = jnp.zeros((SUB, HG_D), F32)
                dkh = jnp.zeros((SUB, HG_D), F32)
                for s in range(SUB):
                    dGs = Gh - Gh[s:s + 1, :]
                    e = jnp.exp(jnp.minimum(dGs, -dGs))
                    dqh = dqh + Ap[:, s:s + 1] * (e * kh[s:s + 1, :])
                    dkh = dkh + ApT[:, s:s + 1] * (e * qh[s:s + 1, :])
                Sb = s_ref[j, sl, :]
                dSb = dSt[h].astype(BF16)
                Am = amat[:, h * SUB:(h + 1) * SUB].astype(BF16)
                dq_p.append(dqh + eG[:, sl] * _dnn(dob[:, sl], Sb))
                dk_p.append(dkh + edl[:, sl] * _dnn(ib[:, sl], dSb))
                di_p.append(_dtn(Am, dob[:, sl]) + _dnt(kd[:, sl], dSb))
                St_end = eGl[:, sl] * Sb.astype(F32) + _dtn(ib[:, sl], kd[:, sl])
                tail_p.append(jnp.sum(dSt[h] * St_end, axis=0, keepdims=True))
                dSt[h] = eGl[:, sl] * dSt[h] + _dtn(dob[:, sl], qt[:, sl])
            dq = jnp.concatenate(dq_p, axis=1)
            dk = jnp.concatenate(dk_p, axis=1)
            di = jnp.concatenate(di_p, axis=1)
            dg = _cumsum_rows(q * dq - k * dk, reverse=True) + jnp.concatenate(tail_p, axis=1)
            df = jnp.where(f > F_FLOOR, dg / f, 0.0)
            dfk = df - dk
            dfz = (1.0 - lb) * dfk * sig * (1.0 - sig)
            dlb_acc[...] += jnp.sum(dfk * (1.0 - sig), axis=0, keepdims=True)
            dh_ref[rows, :] = jnp.concatenate([dq * Q_SCALE, dfz, di, dgz], axis=1).astype(BF16)
            return carry

        lax.fori_loop(0, nsub, block, 0)

        @pl.when(step == n - 1)
        def _():
            if layer == 0:
                dlb_ref[...] = jnp.zeros_like(dlb_ref)
            else:
                dz1 = p0 * p1 * dlb_acc[...]
                dlb_ref[...] = jnp.concatenate([-dz1, dz1], axis=0)

    rev = lambda i: n - 1 - i
    col = lambda c: pl.BlockSpec((ts, HG_W), lambda i: (rev(i), c))
    return pl.pallas_call(
        body, name=name, grid=(n,),
        in_specs=[col(0), col(1), col(2), col(3), col(0),
                  pl.BlockSpec((nsub, HG_W, HG_D), lambda i: (rev(i), 0, 0)),
                  pl.BlockSpec((ts, HG_HEADS * SUB), lambda i: (rev(i), 0)), col(0),
                  pl.BlockSpec((2, HG_W), lambda i: (0, 0)), pl.BlockSpec((1, HG_D), lambda i: (0, 0))],
        out_specs=[pl.BlockSpec((ts, 4 * HG_W), lambda i: (rev(i), 0)),
                   pl.BlockSpec((2, HG_W), lambda i: (0, 0)), pl.BlockSpec((1, HG_D), lambda i: (0, 0))],
        out_shape=[jax.ShapeDtypeStruct((L, 4 * HG_W), BF16), jax.ShapeDtypeStruct((2, HG_W), F32),
                   jax.ShapeDtypeStruct((1, HG_D), F32)],
        scratch_shapes=[pltpu.VMEM((HG_HEADS, HG_D, HG_D), F32), pltpu.VMEM((1, HG_W), F32)],
        compiler_params=_params(("arbitrary",)),
    )(hh, hh, hh, hh, o_raw, states, amat, dcat, lbp, gnorm)


def _adamw_body(gp_ref, w_ref, m_ref, v_ref, g_ref, d_ref, mo_ref, vo_ref):
    c1 = 1.0 - ADAM_B1 ** ADAM_STEP
    c2 = 1.0 - ADAM_B2 ** ADAM_STEP
    g = gp_ref[0].astype(F32)
    for k in range(1, N_DEV):
        g = g + gp_ref[k].astype(F32)
    mn = ADAM_B1 * m_ref[...] + (1.0 - ADAM_B1) * g
    vn = ADAM_B2 * v_ref[...] + (1.0 - ADAM_B2) * (g * g)
    m_hat = mn / c1
    v_hat = vn / c2
    g_ref[...] = g
    d_ref[...] = -ADAM_LR * (m_hat / (jnp.sqrt(v_hat) + ADAM_EPS) + ADAM_WD * w_ref[...])
    mo_ref[...] = mn
    vo_ref[...] = vn


def _adamw_layers(gparts, w, m, v, name):
    depth, R, C = w.shape
    tr = _tile(R, 256, 16)
    nr = R // tr

    def body(*refs):
        layer = pl.program_id(0)
        for d in range(depth):
            @pl.when(layer == d)
            def _(d=d):
                _adamw_body(refs[d], *refs[depth:])

    def parts_spec(d):
        return pl.BlockSpec((N_DEV, tr, C),
                            lambda l, i: (0, jnp.where(l == d, i, jnp.where(l < d, 0, nr - 1)), 0))

    blk = pl.BlockSpec((None, tr, C), lambda l, i: (l, i, 0))
    shp = jax.ShapeDtypeStruct((depth, R, C), F32)
    return pl.pallas_call(
        body, name=name, grid=(depth, nr),
        in_specs=[parts_spec(d) for d in range(depth)] + [blk, blk, blk],
        out_specs=[blk, blk, blk, blk], out_shape=[shp, shp, shp, shp],
        compiler_params=_params(("arbitrary", "arbitrary")),
    )(*gparts, w, m, v)


def _adamw(gparts, w, m, v, name):
    R = w.shape[0]
    tr = _tile(R, 1024, 16) if R % 16 == 0 else R

    def body(*refs):
        _adamw_body(*refs)

    row = pl.BlockSpec((tr, LANES), lambda i: (i, 0))
    shp = jax.ShapeDtypeStruct((R, LANES), F32)
    return pl.pallas_call(
        body, name=name, grid=(R // tr,),
        in_specs=[pl.BlockSpec((N_DEV, tr, LANES), lambda i: (0, i, 0)), row, row, row],
        out_specs=[row, row, row, row], out_shape=[shp, shp, shp, shp],
        compiler_params=_params(("parallel",)),
    )(gparts, w, m, v)


def _flip(coord, bit):
    return 1 - coord if bit else coord


def _gather_many(blocks, name):
    n = len(blocks)

    def body(*refs):
        x_refs, out_refs = refs[:n], refs[n:2 * n]
        send_sems, recv_sems, local_sems = refs[2 * n:]
        x, y, c = lax.axis_index("x"), lax.axis_index("y"), lax.axis_index("c")
        me, sibling = (x, y, c), (x, y, 1 - c)
        chips = [(1 - x, y), (x, 1 - y), (1 - x, 1 - y)]

        def slot(a, px, py, pc):
            return out_refs[a].at[4 * px + 2 * py + pc]

        def copy(a, k, blk, to, src=None):
            return pltpu.make_async_remote_copy(
                src_ref=slot(a, *blk) if src is None else src, dst_ref=slot(a, *blk),
                send_sem=send_sems.at[7 * a + k], recv_sem=recv_sems.at[7 * a + k],
                device_id=to, device_id_type=pl.DeviceIdType.MESH)

        mine = [pltpu.make_async_copy(x_refs[a], slot(a, *me), local_sems.at[a]) for a in range(n)]
        for cp in mine:
            cp.start()
        first = [copy(a, 0, me, sibling, src=x_refs[a]) for a in range(n)]
        for j, chip in enumerate(chips):
            first += [copy(a, 1 + j, me, (*chip, c), src=x_refs[a]) for a in range(n)]
        for cp in first:
            cp.start()
        passed = []
        for j, chip in enumerate(chips):
            for a in range(n):
                copy(a, 1 + j, (*chip, c), me).wait_recv()
                fwd = copy(a, 4 + j, (*chip, c), sibling)
                fwd.start()
                passed.append(fwd)
        for a in range(n):
            copy(a, 0, sibling, me).wait_recv()
        for j, chip in enumerate(chips):
            for a in range(n):
                copy(a, 4 + j, (*chip, 1 - c), me).wait_recv()
        for cp in first + passed:
            cp.wait_send()
        for cp in mine:
            cp.wait()

    hbm = pl.BlockSpec(memory_space=pl.ANY)
    return pl.pallas_call(
        body, name=name,
        out_shape=[jax.ShapeDtypeStruct((N_DEV,) + b.shape, b.dtype) for b in blocks],
        in_specs=[hbm] * n, out_specs=[hbm] * n,
        scratch_shapes=[pltpu.SemaphoreType.DMA((7 * n,)), pltpu.SemaphoreType.DMA((7 * n,)),
                        pltpu.SemaphoreType.DMA((n,))],
    )(*blocks)


def _split_start(blocks, chunked, name):
    n = len(blocks)
    lands = [lax.empty(b.shape if chunked else (N_DEV,) + b.shape, b.dtype) for b in blocks]

    def body(*refs):
        x_refs, land_refs = refs[:n], refs[n:2 * n]
        send_sems, recv_sems, token = refs[2 * n], refs[2 * n + 1], refs[-1]
        x, y, c = lax.axis_index("x"), lax.axis_index("y"), lax.axis_index("c")
        me = 4 * x + 2 * y + c
        for a in range(n):
            for k in range(1, N_DEV):
                px, py, pc = _flip(x, k & 4), _flip(y, k & 2), _flip(c, k & 1)
                pltpu.make_async_remote_copy(
                    src_ref=x_refs[a].at[4 * px + 2 * py + pc] if chunked else x_refs[a],
                    dst_ref=land_refs[a].at[me],
                    send_sem=send_sems.at[7 * a + k - 1], recv_sem=recv_sems.at[7 * a + k - 1],
                    device_id=(px, py, pc), device_id_type=pl.DeviceIdType.MESH).start()
        token[...] = jnp.zeros_like(token)

    hbm = pl.BlockSpec(memory_space=pltpu.HBM)
    sem = pl.BlockSpec(memory_space=pltpu.SEMAPHORE)
    outs = pl.pallas_call(
        body, name=name,
        out_shape=(pltpu.SemaphoreType.DMA((7 * n,)), pltpu.SemaphoreType.DMA((7 * n,)),
                   *[pltpu.HBM(b.shape, b.dtype) for b in blocks], *[pltpu.HBM(l.shape, l.dtype) for l in lands],
                   jax.ShapeDtypeStruct((8, LANES), F32)),
        in_specs=[hbm] * (2 * n),
        out_specs=(sem, sem, *[hbm] * (2 * n), pl.BlockSpec(memory_space=pltpu.VMEM)),
        input_output_aliases={i: 2 + i for i in range(2 * n)},
        compiler_params=pltpu.CompilerParams(has_side_effects=pltpu.SideEffectType.DATAFLOW_SIDE_EFFECTING),
    )(*[pltpu.with_memory_space_constraint(b, pltpu.HBM) for b in blocks],
      *[pltpu.with_memory_space_constraint(l, pltpu.HBM) for l in lands])
    return outs[0], outs[1], list(outs[2:2 + n]), list(outs[2 + n:2 + 2 * n]), outs[-1]


def _split_wait(started, chunked, after, name):
    send_sems, recv_sems, blocks, lands, _ = started
    n = len(blocks)

    def body(*refs):
        x_refs, land_refs = refs[:n], refs[n:2 * n]
        send_sems, recv_sems = refs[2 * n], refs[2 * n + 1]
        x, y, c = lax.axis_index("x"), lax.axis_index("y"), lax.axis_index("c")
        for a in range(n):
            for k in range(1, N_DEV):
                px, py, pc = _flip(x, k & 4), _flip(y, k & 2), _flip(c, k & 1)
                copy = pltpu.make_async_remote_copy(
                    src_ref=x_refs[a].at[4 * px + 2 * py + pc] if chunked else x_refs[a],
                    dst_ref=land_refs[a].at[4 * px + 2 * py + pc],
                    send_sem=send_sems.at[7 * a + k - 1], recv_sem=recv_sems.at[7 * a + k - 1],
                    device_id=(px, py, pc), device_id_type=pl.DeviceIdType.MESH)
                copy.wait_send()
                copy.wait_recv()

    hbm = pl.BlockSpec(memory_space=pltpu.HBM)
    sem = pl.BlockSpec(memory_space=pltpu.SEMAPHORE)
    outs = pl.pallas_call(
        body, name=name,
        out_shape=(*[pltpu.HBM(b.shape, b.dtype) for b in blocks], *[pltpu.HBM(l.shape, l.dtype) for l in lands]),
        in_specs=[hbm] * (2 * n) + [sem, sem, pl.BlockSpec(memory_space=pl.ANY)],
        out_specs=[hbm] * (2 * n),
        input_output_aliases={i: i for i in range(2 * n)},
        compiler_params=pltpu.CompilerParams(has_side_effects=pltpu.SideEffectType.DATAFLOW_SIDE_EFFECTING),
    )(*blocks, *lands, send_sems, recv_sems, after)
    me = 4 * lax.axis_index("x") + 2 * lax.axis_index("y") + lax.axis_index("c")
    own = [lax.dynamic_index_in_dim(b, me, 0, keepdims=False) if chunked else b for b in outs[:n]]
    return [lax.dynamic_update_index_in_dim(z, o, me, 0) for z, o in zip(outs[n:], own)]


def _exchange_grads(layer_chunks, small_chunks, rep_block, name):
    flows, inputs = [], []
    for p, per_layer in enumerate(layer_chunks):
        for l, arr in enumerate(per_layer):
            flows.append(("param", p, l))
            inputs.append(arr)
    flows += [("small",), ("rep",)]
    inputs += [small_chunks, rep_block]
    n_par = len(layer_chunks)
    n_in, n_out, nf = len(inputs), n_par + 2, len(flows)

    def body(*refs):
        in_refs, out_refs = refs[:n_in], refs[n_in:n_in + n_out]
        send_sems, recv_sems, local_sems = refs[n_in + n_out:]
        x, y, c = lax.axis_index("x"), lax.axis_index("y"), lax.axis_index("c")
        me = 4 * x + 2 * y + c

        def src(f, dev):
            return in_refs[f] if flows[f][0] == "rep" else in_refs[f].at[dev]

        def dst(f, dev):
            if flows[f][0] == "param":
                _, p, l = flows[f]
                return out_refs[p].at[dev, l]
            return out_refs[n_par + (0 if flows[f][0] == "small" else 1)].at[dev]

        mine = [pltpu.make_async_copy(src(f, me), dst(f, me), local_sems.at[f]) for f in range(nf)]
        for cp in mine:
            cp.start()
        copies = []
        for k in range(1, N_DEV):
            px, py, pc = _flip(x, k & 4), _flip(y, k & 2), _flip(c, k & 1)
            peer = 4 * px + 2 * py + pc
            for f in range(nf):
                sems = dict(send_sem=send_sems.at[7 * f + k - 1], recv_sem=recv_sems.at[7 * f + k - 1],
                            device_id=(px, py, pc), device_id_type=pl.DeviceIdType.MESH)
                send = pltpu.make_async_remote_copy(src_ref=src(f, peer), dst_ref=dst(f, me), **sems)
                recv = pltpu.make_async_remote_copy(src_ref=src(f, peer), dst_ref=dst(f, peer), **sems)
                send.start()
                copies.append((send, recv))
        for send, recv in copies:
            recv.wait_recv()
        for send, recv in copies:
            send.wait_send()
        for cp in mine:
            cp.wait()

    out_shape = [jax.ShapeDtypeStruct((N_DEV, len(pl_)) + pl_[0].shape[1:], pl_[0].dtype) for pl_ in layer_chunks]
    out_shape += [jax.ShapeDtypeStruct(small_chunks.shape, small_chunks.dtype),
                  jax.ShapeDtypeStruct((N_DEV,) + rep_block.shape, rep_block.dtype)]
    hbm = pl.BlockSpec(memory_space=pl.ANY)
    return pl.pallas_call(
        body, name=name, out_shape=out_shape,
        in_specs=[hbm] * n_in, out_specs=[hbm] * n_out,
        scratch_shapes=[pltpu.SemaphoreType.DMA((7 * nf,)), pltpu.SemaphoreType.DMA((7 * nf,)),
                        pltpu.SemaphoreType.DMA((nf,))],
    )(*inputs)


def _pack_rows(size):
    return -(-size // (8 * LANES)) * 8


def _pack(arrs, dtype):
    parts, offs, r = [], [], 0
    for a in arrs:
        flat = a.astype(dtype).reshape(-1)
        nrow = _pack_rows(flat.shape[0])
        flat = jnp.pad(flat, (0, nrow * LANES - flat.shape[0]))
        parts.append(flat.reshape(nrow, LANES))
        offs.append((r, nrow))
        r += nrow
    return jnp.concatenate(parts, axis=0), offs


def _unpack(buf, offs, shapes, lead=()):
    outs = []
    for (r, nrow), shp in zip(offs, shapes):
        size = 1
        for s in shp:
            size *= s
        flat = buf[..., r:r + nrow, :].reshape(lead + (nrow * LANES,))
        outs.append(flat[..., :size].reshape(lead + tuple(shp)))
    return outs


def _cols_from_shards(g, axis):
    return jnp.concatenate([g[j] for j in range(N_DEV)], axis=axis)


BIG = ("w_in", "w_o", "w_up", "w_down")
SMALL_SHARDED = ("meta_tokens", "w_conv", "w_ffn_conv")
REPLICATED = ("hg_lower_bounds", "w_pool", "pool_scale", "hg_norm_g", "ln1_g", "ln1_b", "b_ffn_conv", "ln2_g", "ln2_b")
WEIGHTS = ("meta_tokens", "hg_lower_bounds", "w_in", "w_conv", "w_pool", "pool_scale", "hg_norm_g", "w_o",
           "ln1_g", "ln1_b", "w_up", "w_ffn_conv", "b_ffn_conv", "w_down", "ln2_g", "ln2_b")


def _pool_blockdiag(w_pool_l):
    z = jnp.zeros((POOL_GROUP, POOL_GROUP), w_pool_l.dtype)
    rows = [jnp.concatenate([w_pool_l[g] if h == g else z for h in range(4)], axis=1) for g in range(4)]
    return jnp.concatenate(rows, axis=0)


def _mixer_weights(g_in, g_o):
    w_in = jnp.transpose(g_in, (1, 0, 2)).reshape(D_MODEL, -1)
    w_o = g_o.reshape(-1, D_MODEL)
    return dict(
        w_hg=w_in[:, 768:2816],
        w_cp=jnp.concatenate([w_in[:, 0:768], w_in[:, 2816:3072]], axis=1),
        w_o=jnp.concatenate([w_o[256:768], w_o[0:256], w_o[768:1024]], axis=0))


def _ffn_weights(g_up, g_down):
    return dict(w_up=jnp.transpose(g_up, (1, 0, 2)).reshape(D_MODEL, -1), w_down=g_down.reshape(-1, D_MODEL))


def kernel(x, meta_tokens, hg_lower_bounds, w_in, w_conv, w_pool, pool_scale, hg_norm_g, w_o, ln1_g, ln1_b, w_up, w_ffn_conv, b_ffn_conv, w_down, ln2_g, ln2_b, loss_target, m_meta_tokens, m_hg_lower_bounds, m_w_in, m_w_conv, m_w_pool, m_pool_scale, m_hg_norm_g, m_w_o, m_ln1_g, m_ln1_b, m_w_up, m_w_ffn_conv, m_b_ffn_conv, m_w_down, m_ln2_g, m_ln2_b, v_meta_tokens, v_hg_lower_bounds, v_w_in, v_w_conv, v_w_pool, v_pool_scale, v_hg_norm_g, v_w_o, v_ln1_g, v_ln1_b, v_w_up, v_w_ffn_conv, v_b_ffn_conv, v_w_down, v_ln2_g, v_ln2_b):
    W = dict(meta_tokens=meta_tokens, hg_lower_bounds=hg_lower_bounds, w_in=w_in, w_conv=w_conv, w_pool=w_pool,
             pool_scale=pool_scale, hg_norm_g=hg_norm_g, w_o=w_o, ln1_g=ln1_g, ln1_b=ln1_b, w_up=w_up,
             w_ffn_conv=w_ffn_conv, b_ffn_conv=b_ffn_conv, w_down=w_down, ln2_g=ln2_g, ln2_b=ln2_b)
    M = dict(meta_tokens=m_meta_tokens, hg_lower_bounds=m_hg_lower_bounds, w_in=m_w_in, w_conv=m_w_conv,
             w_pool=m_w_pool, pool_scale=m_pool_scale, hg_norm_g=m_hg_norm_g, w_o=m_w_o, ln1_g=m_ln1_g,
             ln1_b=m_ln1_b, w_up=m_w_up, w_ffn_conv=m_w_ffn_conv, b_ffn_conv=m_b_ffn_conv, w_down=m_w_down,
             ln2_g=m_ln2_g, ln2_b=m_ln2_b)
    V = dict(meta_tokens=v_meta_tokens, hg_lower_bounds=v_hg_lower_bounds, w_in=v_w_in, w_conv=v_w_conv,
             w_pool=v_w_pool, pool_scale=v_pool_scale, hg_norm_g=v_hg_norm_g, w_o=v_w_o, ln1_g=v_ln1_g,
             ln1_b=v_ln1_b, w_up=v_w_up, w_ffn_conv=v_w_ffn_conv, b_ffn_conv=v_b_ffn_conv, w_down=v_w_down,
             ln2_g=v_ln2_g, ln2_b=v_ln2_b)
    assert x.shape[0] == 1 and x.shape[2] == D_MODEL and w_in.shape[0] == DEPTH
    seq = x.shape[1]
    L = -(-(seq + N_META) // ROW_ALIGN) * ROW_ALIGN

    small_pack, small_offs = _pack([W[n] for n in SMALL_SHARDED], F32)
    shards = {n: [W[n][l].astype(BF16) for l in range(DEPTH)] for n in BIG}
    g_in0, g_o0, small_all = _gather_many([shards["w_in"][0], shards["w_o"][0], small_pack], "gather_weights")
    full = {}
    for n, a in zip(SMALL_SHARDED, _unpack(small_all, small_offs, [W[n].shape for n in SMALL_SHARDED], (N_DEV,))):
        full[n] = _cols_from_shards(a, 1)
    order = (small_all[0, 0, 0] * 0.0).astype(BF16)
    ffn0_started = _split_start([shards["w_up"][0] + order, shards["w_down"][0] + order], False, "gather_ffn0_start")
    order = ffn0_started[4][0, 0].astype(BF16)
    layer1_started = _split_start([shards[n][1] + order for n in BIG], False, "gather_layer1_start")
    lb_in = hg_lower_bounds + layer1_started[4][0, 0]

    pad_rows = L - N_META - seq
    xp = jnp.concatenate([full["meta_tokens"], x[0], jnp.zeros((pad_rows, D_MODEL), F32)], axis=0)
    tgt = jnp.concatenate([jnp.zeros((N_META, D_MODEL), F32), loss_target[0], jnp.zeros((pad_rows, D_MODEL), F32)], axis=0)

    saved = []
    h_in, h_in_b = xp, xp.astype(BF16)
    for l in range(DEPTH):
        if l == 0:
            lw = _mixer_weights(g_in0, g_o0)
        else:
            g_in, g_o, g_up, g_down = _split_wait(layer1_started, False, h_in_b, "gather_layer1_wait")
            lw = {**_mixer_weights(g_in, g_o), **_ffn_weights(g_up, g_down)}
        wc = full["w_conv"][l].T
        wblk = _pool_blockdiag(w_pool[l]).astype(BF16)
        ps = pool_scale[l][None, :]
        gn = hg_norm_g[l][None, :]
        wf = full["w_ffn_conv"][l].T
        bf = b_ffn_conv[l][None, :]
        hh = _matmul(h_in_b, lw["w_hg"], "nn", F32, f"fwd_hg_{l}")
        hc = _matmul(h_in_b, lw["w_cp"], "nn", F32, f"fwd_cp_{l}")
        y_hg, o_raw, states, amat = _hgrn_fwd(hh, lb_in if l == 0 else hg_lower_bounds, gn, l, f"hgrn_fwd_{l}")
        y_cp = _cp_fwd(hc, wc, wblk, ps, f"convpool_fwd_{l}")
        cat = jnp.concatenate([y_hg, y_cp], axis=1)
        mix = _matmul(cat, lw["w_o"], "nn", F32, f"fwd_o_{l}")
        x1, x1_b = _ln_fwd(h_in, mix, ln1_g[l][None, :], ln1_b[l][None, :], f"ln1_fwd_{l}")
        if l == 0:
            lw.update(_ffn_weights(*_split_wait(ffn0_started, False, x1_b, "gather_ffn0_wait")))
        up = _matmul(x1_b, lw["w_up"], "nn", BF16, f"fwd_up_{l}")
        a = _ffn_act_fwd(up, wf, bf, f"ffn_fwd_{l}")
        ffn = _matmul(a, lw["w_down"], "nn", F32, f"fwd_down_{l}")
        x2, x2_b = _ln_fwd(x1, ffn, ln2_g[l][None, :], ln2_b[l][None, :], f"ln2_fwd_{l}")
        saved.append(dict(lw=lw, wc=wc, wblk=wblk, ps=ps, gn=gn, wf=wf, bf=bf, x=h_in, x_b=h_in_b, hh=hh, hc=hc,
                          o_raw=o_raw, states=states, amat=amat, cat=cat, mix=mix, x1=x1, x1_b=x1_b, up=up, a=a, ffn=ffn))
        h_in, h_in_b = x2, x2_b

    dy, loss_part = _loss_head(h_in, tgt, seq)
    loss = lax.psum(loss_part[0, 0], ("x", "y", "c"))

    G = {}
    per_layer = {n: [None] * DEPTH for n in ("w_conv", "w_pool", "pool_scale", "hg_norm_g", "ln1_g", "ln1_b",
                                             "w_ffn_conv", "b_ffn_conv", "ln2_g", "ln2_b")}
    ffn_started, mix_started = [None] * DEPTH, [None] * DEPTH
    order = jnp.zeros((), F32)
    dlb_total = jnp.zeros((DEPTH, HG_W), F32)
    for l in reversed(range(DEPTH)):
        s = saved[l]
        lw = s["lw"]
        dz2, dz2_b, dg2, db2 = _ln_bwd(s["x1"], s["ffn"], dy, ln2_g[l][None, :] + order, f"ln2_bwd_{l}")
        da = _matmul(dz2_b, lw["w_down"], "nt", BF16, f"bwd_da_{l}")
        d_w_down = _matmul(s["a"], dz2_b, "tn", BF16, f"wgrad_down_{l}")
        dup, dwf, dbf = _ffn_act_bwd(s["up"], da, s["wf"], s["bf"], f"ffn_bwd_{l}")
        dx1 = _matmul(dup, lw["w_up"], "nt", F32, f"bwd_dx1_{l}", res=dz2, alpha=ALPHA)
        d_w_up = _matmul(s["x1_b"], dup, "tn", BF16, f"wgrad_up_{l}")
        ffn_started[l] = _split_start([jnp.transpose(d_w_up.reshape(D_MODEL, N_DEV, -1), (1, 0, 2)),
                                       d_w_down.reshape(N_DEV, -1, D_MODEL)], True, f"scatter_ffn{l}_start")
        order = ffn_started[l][4][0, 0]
        dz1, dz1_b, dg1, db1 = _ln_bwd(s["x"], s["mix"], dx1, ln1_g[l][None, :] + order, f"ln1_bwd_{l}")
        dcat = _matmul(dz1_b, lw["w_o"], "nt", F32, f"bwd_dcat_{l}")
        d_w_o = _matmul(s["cat"], dz1_b, "tn", BF16, f"wgrad_o_{l}")
        dhh, dlb, dgn = _hgrn_bwd(s["hh"], s["o_raw"], s["states"], s["amat"], dcat, hg_lower_bounds, s["gn"], l,
                                  f"hgrn_bwd_{l}")
        dhc, dwc, dwblk, dps = _cp_bwd(s["hc"], dcat, s["wc"], s["wblk"], s["ps"], f"convpool_bwd_{l}")
        dx_a = _matmul(dhh, lw["w_hg"], "nt", F32, f"bwd_dx_hg_{l}", res=dz1, alpha=ALPHA)
        dx = _matmul(dhc, lw["w_cp"], "nt", F32, f"bwd_dx_cp_{l}", res=dx_a, alpha=1.0)
        d_w_hg = _matmul(s["x_b"], dhh, "tn", BF16, f"wgrad_hg_{l}")
        d_w_cp = _matmul(s["x_b"], dhc, "tn", BF16, f"wgrad_cp_{l}")
        d_w_in = jnp.concatenate([d_w_cp[:, 0:768], d_w_hg, d_w_cp[:, 768:1024]], axis=1)
        mix_chunks = [jnp.transpose(d_w_in.reshape(D_MODEL, N_DEV, -1), (1, 0, 2)),
                      jnp.concatenate([d_w_o[512:768], d_w_o[0:512], d_w_o[768:1024]], axis=0).reshape(N_DEV, -1, D_MODEL)]
        if l > 0:
            mix_started[l] = _split_start(mix_chunks, True, f"scatter_mix{l}_start")
            order = mix_started[l][4][0, 0]
        per_layer["w_conv"][l] = dwc.T
        per_layer["w_ffn_conv"][l] = dwf.T
        per_layer["b_ffn_conv"][l] = dbf[0]
        per_layer["w_pool"][l] = jnp.stack([dwblk[g * 64:(g + 1) * 64, g * 64:(g + 1) * 64] for g in range(4)], axis=0)
        per_layer["pool_scale"][l] = dps[0]
        per_layer["hg_norm_g"][l] = dgn[0]
        per_layer["ln1_g"][l], per_layer["ln1_b"][l] = dg1[0], db1[0]
        per_layer["ln2_g"][l], per_layer["ln2_b"][l] = dg2[0], db2[0]
        dlb_total = dlb_total + dlb
        dy = dx
    for n, parts in per_layer.items():
        G[n] = jnp.stack(parts, axis=0)
    G["hg_lower_bounds"] = dlb_total
    grad_x = dy[N_META:N_META + seq][None]

    def shard_major(g, lead):
        g = g.reshape(g.shape[:lead] + (N_DEV, -1) + g.shape[lead + 1:])
        g = jnp.moveaxis(g, lead, 0).reshape(N_DEV, -1)
        nrow = _pack_rows(g.shape[1])
        return jnp.pad(g, ((0, 0), (0, nrow * LANES - g.shape[1]))).reshape(N_DEV, nrow, LANES)

    small_chunks = jnp.concatenate([shard_major(dy[0:N_META], 1), shard_major(G["w_conv"], 1),
                                    shard_major(G["w_ffn_conv"], 1)], axis=1)
    w_small, _ = _pack([W[n] for n in SMALL_SHARDED], F32)
    rep_pack, rep_offs = _pack([G[n] for n in REPLICATED], F32)
    in0, o0, small_recv, rep_all = _exchange_grads([[c] for c in mix_chunks], small_chunks, rep_pack, "exchange_grads")
    parts = {"w_in": [in0[:, 0]], "w_o": [o0[:, 0]], "w_up": [], "w_down": []}
    for l in range(DEPTH):
        up_l, down_l = _split_wait(ffn_started[l], True, rep_all, f"scatter_ffn{l}_wait")
        parts["w_up"].append(up_l)
        parts["w_down"].append(down_l)
        if l > 0:
            in_l, o_l = _split_wait(mix_started[l], True, rep_all, f"scatter_mix{l}_wait")
            parts["w_in"].append(in_l)
            parts["w_o"].append(o_l)

    res = {k: {} for k in ("grad", "delta", "new_m", "new_v")}
    kinds = ("grad", "delta", "new_m", "new_v")
    for n in BIG:
        for kind, a in zip(kinds, _adamw_layers(parts[n], W[n], M[n], V[n], f"adamw_{n}")):
            res[kind][n] = a
    m_small, _ = _pack([M[n] for n in SMALL_SHARDED], F32)
    v_small, _ = _pack([V[n] for n in SMALL_SHARDED], F32)
    outs_small = _adamw(small_recv, w_small, m_small, v_small, "adamw_small_sharded")
    w_rep, _ = _pack([W[n] for n in REPLICATED], F32)
    m_rep, _ = _pack([M[n] for n in REPLICATED], F32)
    v_rep, _ = _pack([V[n] for n in REPLICATED], F32)
    outs_rep = _adamw(rep_all, w_rep, m_rep, v_rep, "adamw_replicated")
    for kind, b_sm, b_rep in zip(kinds, outs_small, outs_rep):
        for n, a in zip(SMALL_SHARDED, _unpack(b_sm, small_offs, [W[n].shape for n in SMALL_SHARDED])):
            res[kind][n] = a
        for n, a in zip(REPLICATED, _unpack(b_rep, rep_offs, [W[n].shape for n in REPLICATED])):
            res[kind][n] = a

    return (loss, grad_x, *[res["grad"][n] for n in WEIGHTS], *[res["delta"][n] for n in WEIGHTS],
            *[res["new_m"][n] for n in WEIGHTS], *[res["new_v"][n] for n in WEIGHTS])
```

```python
import jax
import jax.numpy as jnp
from jax import lax
from jax.experimental import pallas as pl
from jax.experimental.pallas import tpu as pltpu

F32 = jnp.float32
BF16 = jnp.bfloat16

N_DEV = 8
D_MODEL = 1024
N_META = 16
DEPTH = 2
CONV_W = 256
HG_W = 512
HG_D = 128
HG_HEADS = 4
POOL_W = 256
POOL_GROUP = 64
D_FF = 2816
ALPHA = (2 * DEPTH) ** 0.25
LN_EPS = 1e-5
RMS_EPS = 1e-6
F_FLOOR = 1e-30
Q_SCALE = HG_D ** -0.5
SUB = 16
SEQ_TILE = 192
FFN_TILE = 96
ROW_ALIGN = 192
LANES = 128
VMEM_LIMIT = 48 * 1024 * 1024
MATMUL_VMEM_BUDGET = 38 * 1024 * 1024

ADAM_LR = 0.001
ADAM_B1 = 0.9
ADAM_B2 = 0.999
ADAM_EPS = 1e-08
ADAM_WD = 0.01
ADAM_STEP = 10


def _tile(n, cap, mult):
    best = 0
    for t in range(mult, min(n, cap) + 1, mult):
        if n % t == 0:
            best = t
    assert best > 0, (n, cap, mult)
    return best


def _params(sem, vmem=VMEM_LIMIT):
    return pltpu.CompilerParams(dimension_semantics=sem, vmem_limit_bytes=vmem)


def _dnt(a, b):
    return lax.dot_general(a, b, (((1,), (1,)), ((), ())), preferred_element_type=F32)


def _dtn(a, b):
    return lax.dot_general(a, b, (((0,), (0,)), ((), ())), preferred_element_type=F32)


def _dnn(a, b):
    return jnp.dot(a, b, preferred_element_type=F32)


def _sigmoid(x):
    return 1.0 / (1.0 + jnp.exp(-x))


def _matmul(a, b, mode, out_dtype, name, res=None, alpha=1.0):
    if mode == "tn":
        K, M = a.shape
    else:
        M, K = a.shape
    N = b.shape[0] if mode == "nt" else b.shape[1]
    out_bytes = jnp.dtype(out_dtype).itemsize
    tn = _tile(N, 1536, LANES)
    tk = _tile(K, 1536, 16) if mode == "tn" else _tile(K, 2816, LANES)
    nk = K // tk
    use_acc = nk > 1 and out_dtype != F32
    tm = M
    for cap in (1536, 768, 384):
        tm = _tile(M, cap, 16)
        blocks = 2 * (a.dtype.itemsize * tm * tk + b.dtype.itemsize * tn * tk + out_bytes * tm * tn
                      + (4 * tm * tn if res is not None else 0)) + (4 * tm * tn if use_acc else 0)
        if blocks <= MATMUL_VMEM_BUDGET:
            break
    dims = {"nn": ((1,), (0,)), "nt": ((1,), (1,)), "tn": ((0,), (0,))}[mode]

    def body(*refs):
        a_ref, b_ref = refs[0], refs[1]
        r_ref = refs[2] if res is not None else None
        o_ref = refs[3] if res is not None else refs[2]
        acc = refs[-1] if use_acc else o_ref
        k = pl.program_id(2)
        p = lax.dot_general(a_ref[...].astype(BF16), b_ref[...].astype(BF16), (dims, ((), ())),
                            preferred_element_type=F32)

        def finish(r):
            if r_ref is not None:
                r = r + alpha * r_ref[...]
            o_ref[...] = r.astype(out_dtype)

        if nk == 1:
            finish(p)
        else:
            @pl.when(k == 0)
            def _():
                acc[...] = p

            @pl.when((k > 0) & (k < nk - 1))
            def _():
                acc[...] += p

            @pl.when(k == nk - 1)
            def _():
                finish(acc[...] + p)

    if mode == "tn":
        a_spec = pl.BlockSpec((tk, tm), lambda i, j, k: (k, i))
    else:
        a_spec = pl.BlockSpec((tm, tk), lambda i, j, k: (i, k))
    if mode == "nt":
        b_spec = pl.BlockSpec((tn, tk), lambda i, j, k: (j, k))
    else:
        b_spec = pl.BlockSpec((tk, tn), lambda i, j, k: (k, j))
    in_specs = [a_spec, b_spec]
    args = [a, b]
    if res is not None:
        in_specs.append(pl.BlockSpec((tm, tn), lambda i, j, k: (i, j)))
        args.append(res)
    return pl.pallas_call(
        body, name=name,
        grid=(M // tm, N // tn, nk),
        in_specs=in_specs,
        out_specs=pl.BlockSpec((tm, tn), lambda i, j, k: (i, j)),
        out_shape=jax.ShapeDtypeStruct((M, N), out_dtype),
        scratch_shapes=[pltpu.VMEM((tm, tn), F32)] if use_acc else [],
        compiler_params=_params(("parallel", "parallel", "arbitrary")),
    )(*args)


def _matmul_ln(a, w, x, g, b, name, loss=None):
    L, K = a.shape
    D = w.shape[1]
    tr = L
    for cap in (1536, 768, 384):
        tr = _tile(L, cap, 16)
        if 2 * (2 * tr * K + 2 * K * D + 4 * tr * D * (4 if loss else 3) + 2 * tr * D) <= MATMUL_VMEM_BUDGET:
            break

    def body(*refs):
        a_ref, w_ref, x_ref, g_ref, b_ref = refs[:5]
        z = ALPHA * x_ref[...] + _dnn(a_ref[...], w_ref[...])
        mu = jnp.mean(z, axis=-1, keepdims=True)
        zc = z - mu
        var = jnp.mean(zc * zc, axis=-1, keepdims=True)
        y = zc * lax.rsqrt(var + LN_EPS) * g_ref[...] + b_ref[...]
        if loss is None:
            z_ref, y_ref, yb_ref = refs[5:]
            y_ref[...] = y
            yb_ref[...] = y.astype(BF16)
        else:
            t_ref, z_ref, dy_ref, loss_ref = refs[5:]
            i = pl.program_id(0)

            @pl.when(i == 0)
            def _():
                loss_ref[...] = jnp.zeros_like(loss_ref)

            r = i * tr + lax.broadcasted_iota(jnp.int32, (tr, D), 0)
            valid = (r >= N_META) & (r < N_META + loss[1])
            e = jnp.where(valid, y - t_ref[...], 0.0)
            dy_ref[...] = e * (1.0 / D)
            s = jnp.sum(jnp.sum(e * e, axis=-1, keepdims=True), axis=0, keepdims=True)
            loss_ref[...] += (0.5 / D) * s
        z_ref[...] = z

    row = pl.BlockSpec((tr, D), lambda i: (i, 0))
    vec = pl.BlockSpec((1, D), lambda i: (0, 0))
    in_specs = [pl.BlockSpec((tr, K), lambda i: (i, 0)), pl.BlockSpec((K, D), lambda i: (0, 0)), row, vec, vec]
    f32_rows = jax.ShapeDtypeStruct((L, D), F32)
    if loss is None:
        args, out_specs = [a, w, x, g, b], [row, row, row]
        out_shape = [f32_rows, f32_rows, jax.ShapeDtypeStruct((L, D), BF16)]
    else:
        args, in_specs = [a, w, x, g, b, loss[0]], in_specs + [row]
        out_specs = [row, row, pl.BlockSpec((1, 1), lambda i: (0, 0))]
        out_shape = [f32_rows, f32_rows, jax.ShapeDtypeStruct((1, 1), F32)]
    return pl.pallas_call(
        body, name=name, grid=(L // tr,), in_specs=in_specs, out_specs=out_specs, out_shape=out_shape,
        compiler_params=_params(("arbitrary",) if loss else ("parallel",)),
    )(*args)


def _ln_bwd(z, dy, g, name):
    L, D = z.shape
    tr = _tile(L, 768, 16)

    def body(z_ref, dy_ref, g_ref, dz_ref, dzb_ref, dg_ref, db_ref):
        @pl.when(pl.program_id(0) == 0)
        def _():
            dg_ref[...] = jnp.zeros_like(dg_ref)
            db_ref[...] = jnp.zeros_like(db_ref)

        z = z_ref[...]
        mu = jnp.mean(z, axis=-1, keepdims=True)
        zc = z - mu
        var = jnp.mean(zc * zc, axis=-1, keepdims=True)
        rstd = lax.rsqrt(var + LN_EPS)
        xhat = zc * rstd
        dy = dy_ref[...]
        dxh = dy * g_ref[...]
        m1 = jnp.mean(dxh, axis=-1, keepdims=True)
        m2 = jnp.mean(dxh * xhat, axis=-1, keepdims=True)
        dz = rstd * (dxh - m1 - xhat * m2)
        dz_ref[...] = dz
        dzb_ref[...] = dz.astype(BF16)
        dg_ref[...] += jnp.sum(dy * xhat, axis=0, keepdims=True)
        db_ref[...] += jnp.sum(dy, axis=0, keepdims=True)

    row = pl.BlockSpec((tr, D), lambda i: (i, 0))
    vec = pl.BlockSpec((1, D), lambda i: (0, 0))
    return pl.pallas_call(
        body, name=name, grid=(L // tr,),
        in_specs=[row, row, vec], out_specs=[row, row, vec, vec],
        out_shape=[jax.ShapeDtypeStruct((L, D), F32), jax.ShapeDtypeStruct((L, D), BF16),
                   jax.ShapeDtypeStruct((1, D), F32), jax.ShapeDtypeStruct((1, D), F32)],
        compiler_params=_params(("arbitrary",)),
    )(z, dy, g)


def _shift_down(x, prev, k):
    out = pltpu.roll(x, k, 0)
    row = lax.broadcasted_iota(jnp.int32, (8, x.shape[1]), 0)
    top = out[0:8]
    for r in range(k):
        top = jnp.where(row == r, prev[8 - k + r:8 - k + r + 1, :], top)
    return top if x.shape[0] == 8 else jnp.concatenate([top, out[8:]], axis=0)


def _shift_up(x, nxt, k):
    T = x.shape[0]
    out = pltpu.roll(x, T - k, 0)
    row = lax.broadcasted_iota(jnp.int32, (8, x.shape[1]), 0)
    bot = out[T - 8:T]
    for r in range(k):
        bot = jnp.where(row == 8 - k + r, nxt[r:r + 1, :], bot)
    return bot if T == 8 else jnp.concatenate([out[:T - 8], bot], axis=0)


def _conv3(x, prev, w, b):
    return w[2:3, :] * x + w[1:2, :] * _shift_down(x, prev, 1) + w[0:1, :] * _shift_down(x, prev, 2) + b


def _ffn_act_fwd(up, w, b, name):
    L, C = up.shape
    F = C // 2
    ts = FFN_TILE
    n = L // ts

    def body(up_ref, pv_ref, w_ref, b_ref, a_ref):
        i = pl.program_id(0)
        x = up_ref[...].astype(F32)
        prev = jnp.where(i > 0, pv_ref[...].astype(F32)[8:16], 0.0)
        u = _conv3(x, prev, w_ref[...], b_ref[...])
        gate = u[:, :F]
        a_ref[...] = (gate * _sigmoid(gate) * u[:, F:]).astype(BF16)

    return pl.pallas_call(
        body, name=name, grid=(n,),
        in_specs=[pl.BlockSpec((ts, C), lambda i: (i, 0)),
                  pl.BlockSpec((16, C), lambda i: (jnp.maximum(i * (ts // 16) - 1, 0), 0)),
                  pl.BlockSpec((3, C), lambda i: (0, 0)), pl.BlockSpec((1, C), lambda i: (0, 0))],
        out_specs=pl.BlockSpec((ts, F), lambda i: (i, 0)),
        out_shape=jax.ShapeDtypeStruct((L, F), BF16),
        compiler_params=_params(("parallel",)),
    )(up, up, w, b)


def _ffn_act_bwd(up, da, w, b, name):
    L, C = up.shape
    F = C // 2
    ts = FFN_TILE
    n = L // ts
    last16 = L // 16 - 1

    def du_of(u, da):
        gate, val = u[:, :F], u[:, F:]
        sg = _sigmoid(gate)
        dgate = da * val * (sg * (1.0 + gate * (1.0 - sg)))
        dval = da * (gate * sg)
        return jnp.concatenate([dgate, dval], axis=1)

    def body(up_ref, pv_ref, nx_ref, da_ref, dan_ref, w_ref, b_ref, dup_ref, dw_ref, db_ref):
        i = pl.program_id(0)

        @pl.when(i == 0)
        def _():
            dw_ref[...] = jnp.zeros_like(dw_ref)
            db_ref[...] = jnp.zeros_like(db_ref)

        w = w_ref[...]
        bias = b_ref[...]
        x = up_ref[...].astype(F32)
        prev = jnp.where(i > 0, pv_ref[...].astype(F32)[8:16], 0.0)
        u = _conv3(x, prev, w, bias)
        u_next = _conv3(nx_ref[...].astype(F32)[0:8], x[ts - 8:ts], w, bias)
        du = du_of(u, da_ref[...].astype(F32))
        dun = jnp.where(i < n - 1, du_of(u_next, dan_ref[...].astype(F32)[0:8]), 0.0)
        du1 = _shift_up(du, dun, 1)
        du2 = _shift_up(du, dun, 2)
        dup_ref[...] = (w[2:3, :] * du + w[1:2, :] * du1 + w[0:1, :] * du2).astype(BF16)
        dw_ref[...] += jnp.concatenate([jnp.sum(x * du2, axis=0, keepdims=True),
                                        jnp.sum(x * du1, axis=0, keepdims=True),
                                        jnp.sum(x * du, axis=0, keepdims=True)], axis=0)
        db_ref[...] += jnp.sum(du, axis=0, keepdims=True)

    prv = lambda i: (jnp.maximum(i * (ts // 16) - 1, 0), 0)
    nxt = lambda i: (jnp.minimum((i + 1) * (ts // 16), last16), 0)
    return pl.pallas_call(
        body, name=name, grid=(n,),
        in_specs=[pl.BlockSpec((ts, C), lambda i: (i, 0)), pl.BlockSpec((16, C), prv), pl.BlockSpec((16, C), nxt),
                  pl.BlockSpec((ts, F), lambda i: (i, 0)), pl.BlockSpec((16, F), nxt),
                  pl.BlockSpec((3, C), lambda i: (0, 0)), pl.BlockSpec((1, C), lambda i: (0, 0))],
        out_specs=[pl.BlockSpec((ts, C), lambda i: (i, 0)), pl.BlockSpec((3, C), lambda i: (0, 0)),
                   pl.BlockSpec((1, C), lambda i: (0, 0))],
        out_shape=[jax.ShapeDtypeStruct((L, C), BF16), jax.ShapeDtypeStruct((3, C), F32),
                   jax.ShapeDtypeStruct((1, C), F32)],
        compiler_params=_params(("arbitrary",)),
    )(up, up, up, da, da, w, b)


def _pool_window(ext, tile_rows, first_row, lead):
    T = ext.shape[0]
    sh = (lambda x, k: pltpu.roll(x, T - k, 0)) if lead else (lambda x, k: pltpu.roll(x, k, 0))
    r2 = ext + sh(ext, 1)
    r4 = r2 + sh(r2, 2)
    r8 = r4 + sh(r4, 4)
    r16 = r8 + sh(r8, 8)
    lo = 0 if lead else 16
    grp = lax.broadcasted_iota(jnp.int32, (tile_rows, POOL_W), 1) // POOL_GROUP
    pick = lambda a, b, c, d: jnp.where(grp == 0, a, jnp.where(grp == 1, b, jnp.where(grp == 2, c, d)))
    win = pick(r2[lo:lo + tile_rows], r4[lo:lo + tile_rows], r8[lo:lo + tile_rows], r16[lo:lo + tile_rows])
    return win, pick(2.0, 4.0, 8.0, 16.0)


def _pool_count(first_row, rows, wlen):
    t1 = (first_row + lax.broadcasted_iota(jnp.int32, (rows, POOL_W), 0) + 1).astype(F32)
    return jnp.minimum(t1, wlen)


def _cp_fwd(hc, wc, wblk, pscale, name):
    L = hc.shape[0]
    ts = SEQ_TILE
    n = L // ts

    def body(h_ref, hp_ref, wc_ref, wb_ref, ps_ref, y_ref):
        i = pl.program_id(0)
        h = h_ref[...]
        hp = jnp.where(i > 0, hp_ref[...], 0.0)
        cb, cc, cv, pv = h[:, 0:256], h[:, 256:512], h[:, 512:768], h[:, 768:1024]
        p = cc * cv
        pp = hp[8:16, 256:512] * hp[8:16, 512:768]
        w = wc_ref[...]
        conv = w[2:3, :] * p + w[1:2, :] * _shift_down(p, pp, 1) + w[0:1, :] * _shift_down(p, pp, 2)
        y_conv = cb * conv
        ext = jnp.concatenate([hp[:, 768:1024], pv], axis=0)
        win, wlen = _pool_window(ext, ts, i * ts, False)
        d = win / _pool_count(i * ts, ts, wlen) - pv
        y_pool = _dnn(d.astype(BF16), wb_ref[...]) * ps_ref[...]
        y_ref[...] = jnp.concatenate([y_conv, y_pool], axis=1).astype(BF16)

    return pl.pallas_call(
        body, name=name, grid=(n,),
        in_specs=[pl.BlockSpec((ts, 1024), lambda i: (i, 0)),
                  pl.BlockSpec((16, 1024), lambda i: (jnp.maximum(i * (ts // 16) - 1, 0), 0)),
                  pl.BlockSpec((3, 256), lambda i: (0, 0)), pl.BlockSpec((256, 256), lambda i: (0, 0)),
                  pl.BlockSpec((1, 256), lambda i: (0, 0))],
        out_specs=pl.BlockSpec((ts, 512), lambda i: (i, 0)),
        out_shape=jax.ShapeDtypeStruct((L, 512), BF16),
        compiler_params=_params(("parallel",)),
    )(hc, hc, wc, wblk, pscale)


def _cp_bwd(hc, dcat, wc, wblk, pscale, name):
    L = hc.shape[0]
    ts = SEQ_TILE
    n = L // ts
    last16 = L // 16 - 1

    def body(h_ref, hp_ref, hn_ref, dy_ref, dyn_ref, wc_ref, wb_ref, ps_ref,
             dh_ref, dwc_ref, dwb_ref, dps_ref):
        i = pl.program_id(0)

        @pl.when(i == 0)
        def _():
            dwc_ref[...] = jnp.zeros_like(dwc_ref)
            dwb_ref[...] = jnp.zeros_like(dwb_ref)
            dps_ref[...] = jnp.zeros_like(dps_ref)

        h = h_ref[...]
        hp = jnp.where(i > 0, hp_ref[...], 0.0)
        hn = hn_ref[...]
        dy = dy_ref[...]
        dyn = jnp.where(i < n - 1, dyn_ref[...], 0.0)
        cb, cc, cv, pv = h[:, 0:256], h[:, 256:512], h[:, 512:768], h[:, 768:1024]
        w = wc_ref[...]
        p = cc * cv
        pp = hp[8:16, 256:512] * hp[8:16, 512:768]
        p1 = _shift_down(p, pp, 1)
        p2 = _shift_down(p, pp, 2)
        conv = w[2:3, :] * p + w[1:2, :] * p1 + w[0:1, :] * p2
        dyc = dy[:, 0:256]
        dcb = dyc * conv
        dconv = dyc * cb
        dconv_n = dyn[0:8, 0:256] * hn[0:8, 0:256]
        dc1 = _shift_up(dconv, dconv_n, 1)
        dc2 = _shift_up(dconv, dconv_n, 2)
        dp = w[2:3, :] * dconv + w[1:2, :] * dc1 + w[0:1, :] * dc2
        dwc_ref[...] += jnp.concatenate([jnp.sum(p * dc2, axis=0, keepdims=True),
                                         jnp.sum(p * dc1, axis=0, keepdims=True),
                                         jnp.sum(p * dconv, axis=0, keepdims=True)], axis=0)
        ps = ps_ref[...]
        wb = wb_ref[...]
        ext = jnp.concatenate([hp[:, 768:1024], pv], axis=0)
        win, wlen = _pool_window(ext, ts, i * ts, False)
        d = win / _pool_count(i * ts, ts, wlen) - pv
        db = d.astype(BF16)
        dyp = dy[:, 256:512]
        dps_ref[...] += jnp.sum(dyp * _dnn(db, wb), axis=0, keepdims=True)
        dypre = (dyp * ps).astype(BF16)
        dwb_ref[...] += _dtn(db, dypre)
        dd = _dnt(dypre, wb)
        ddn = _dnt((dyn[:, 256:512] * ps).astype(BF16), wb)
        e = dd / _pool_count(i * ts, ts, wlen)
        en = ddn / _pool_count((i + 1) * ts, 16, wlen[0:16])
        lead, _ = _pool_window(jnp.concatenate([e, en], axis=0), ts, i * ts, True)
        dpv = lead - dd
        dh_ref[...] = jnp.concatenate([dcb, dp * cv, dp * cc, dpv], axis=1).astype(BF16)

    return pl.pallas_call(
        body, name=name, grid=(n,),
        in_specs=[pl.BlockSpec((ts, 1024), lambda i: (i, 0)),
                  pl.BlockSpec((16, 1024), lambda i: (jnp.maximum(i * (ts // 16) - 1, 0), 0)),
                  pl.BlockSpec((16, 1024), lambda i: (jnp.minimum((i + 1) * (ts // 16), last16), 0)),
                  pl.BlockSpec((ts, 512), lambda i: (i, 1)),
                  pl.BlockSpec((16, 512), lambda i: (jnp.minimum((i + 1) * (ts // 16), last16), 1)),
                  pl.BlockSpec((3, 256), lambda i: (0, 0)), pl.BlockSpec((256, 256), lambda i: (0, 0)),
                  pl.BlockSpec((1, 256), lambda i: (0, 0))],
        out_specs=[pl.BlockSpec((ts, 1024), lambda i: (i, 0)), pl.BlockSpec((3, 256), lambda i: (0, 0)),
                   pl.BlockSpec((256, 256), lambda i: (0, 0)), pl.BlockSpec((1, 256), lambda i: (0, 0))],
        out_shape=[jax.ShapeDtypeStruct((L, 1024), BF16), jax.ShapeDtypeStruct((3, 256), F32),
                   jax.ShapeDtypeStruct((256, 256), F32), jax.ShapeDtypeStruct((1, 256), F32)],
        compiler_params=_params(("arbitrary",)),
    )(hc, hc, hc, dcat, dcat, wc, wblk, pscale)


def _lower_bound(lb_ref, layer):
    b0, b1 = lb_ref[0:1, :], lb_ref[1:2, :]
    m = jnp.maximum(b0, b1)
    e0, e1 = jnp.exp(b0 - m), jnp.exp(b1 - m)
    p0, p1 = e0 / (e0 + e1), e1 / (e0 + e1)
    lb = (p0 - p0) if layer == 0 else ((p0 + p1) - p0)
    return lb, p0, p1


def _cumsum_rows(x, reverse=False):
    row = lax.broadcasted_iota(jnp.int32, x.shape, 0)
    for sh in (1, 2, 4, 8):
        if reverse:
            x = x + jnp.where(row < SUB - sh, pltpu.roll(x, SUB - sh, 0), 0.0)
        else:
            x = x + jnp.where(row >= sh, pltpu.roll(x, sh, 0), 0.0)
    return x


def _gates(fz, lb):
    sig = _sigmoid(fz)
    f = lb + (1.0 - lb) * sig
    g = jnp.log(jnp.maximum(f, F_FLOOR))
    k = (1.0 - lb) * (1.0 - sig)
    return sig, f, g, k


def _head(h):
    return slice(h * HG_D, (h + 1) * HG_D)


def _hgrn_fwd(hh, lbp, gnorm, layer, name):
    L = hh.shape[0]
    ts = SEQ_TILE
    n = L // ts
    nsub = ts // SUB

    def body(q_ref, f_ref, i_ref, g_ref, lb_ref, gn_ref, y_ref, o_ref, s_ref, a_ref, St):
        @pl.when(pl.program_id(0) == 0)
        def _():
            St[...] = jnp.zeros_like(St)

        lb, _, _ = _lower_bound(lb_ref, layer)
        gn = jnp.tile(gn_ref[...], (1, HG_HEADS))
        r16 = lax.broadcasted_iota(jnp.int32, (SUB, SUB), 0)
        c16 = lax.broadcasted_iota(jnp.int32, (SUB, SUB), 1)

        def block(j, carry):
            rows = pl.ds(pl.multiple_of(j * SUB, SUB), SUB)
            q = q_ref[rows, :] * Q_SCALE
            iv = i_ref[rows, :]
            gz = g_ref[rows, :]
            _, _, g, k = _gates(f_ref[rows, :], lb)
            G = _cumsum_rows(g)
            Gl = G[SUB - 1:SUB, :]
            qt = (q * jnp.exp(G)).astype(BF16)
            kd = (k * jnp.exp(Gl - G)).astype(BF16)
            eGl = jnp.exp(Gl)
            ib = iv.astype(BF16)
            A = [jnp.zeros((SUB, SUB), F32) for _ in range(HG_HEADS)]
            for s in range(SUB):
                P = q * jnp.exp(jnp.minimum(G - G[s:s + 1, :], 0.0)) * k[s:s + 1, :]
                for h in range(HG_HEADS):
                    A[h] = jnp.where(c16 == s, jnp.sum(P[:, _head(h)], axis=-1, keepdims=True), A[h])
            outs, ons, amats = [], [], []
            for h in range(HG_HEADS):
                sl = _head(h)
                Sb = St[h].astype(BF16)
                s_ref[j, sl, :] = Sb
                Am = jnp.where(r16 >= c16, A[h], 0.0)
                amats.append(Am)
                o = _dnt(qt[:, sl], Sb) + _dnn(Am.astype(BF16), ib[:, sl])
                St[h] = eGl[:, sl] * St[h] + _dtn(ib[:, sl], kd[:, sl])
                outs.append(o)
                ons.append(o * lax.rsqrt(jnp.mean(o * o, axis=-1, keepdims=True) + RMS_EPS))
            a_ref[rows, :] = jnp.concatenate(amats, axis=1)
            o_ref[rows, :] = jnp.concatenate(outs, axis=1)
            y = jnp.concatenate(ons, axis=1) * gn * (gz * _sigmoid(gz))
            y_ref[rows, :] = y.astype(BF16)
            return carry

        lax.fori_loop(0, nsub, block, 0, unroll=2)

    col = lambda c: pl.BlockSpec((ts, HG_W), lambda i: (i, c))
    return pl.pallas_call(
        body, name=name, grid=(n,),
        in_specs=[col(0), col(1), col(2), col(3), pl.BlockSpec((2, HG_W), lambda i: (0, 0)),
                  pl.BlockSpec((1, HG_D), lambda i: (0, 0))],
        out_specs=[pl.BlockSpec((ts, HG_W), lambda i: (i, 0)), pl.BlockSpec((ts, HG_W), lambda i: (i, 0)),
                   pl.BlockSpec((nsub, HG_W, HG_D), lambda i: (i, 0, 0)),
                   pl.BlockSpec((ts, HG_HEADS * SUB), lambda i: (i, 0))],
        out_shape=[jax.ShapeDtypeStruct((L, HG_W), BF16), jax.ShapeDtypeStruct((L, HG_W), F32),
                   jax.ShapeDtypeStruct((L // SUB, HG_W, HG_D), BF16),
                   jax.ShapeDtypeStruct((L, HG_HEADS * SUB), F32)],
        scratch_shapes=[pltpu.VMEM((HG_HEADS, HG_D, HG_D), F32)],
        compiler_params=_params(("arbitrary",)),
    )(hh, hh, hh, hh, lbp, gnorm)


def _hgrn_bwd(hh, o_raw, states, amat, dcat, lbp, gnorm, layer, name):
    L = hh.shape[0]
    ts = SEQ_TILE
    n = L // ts
    nsub = ts // SUB

    def body(q_ref, f_ref, i_ref, g_ref, o_ref, s_ref, a_ref, dy_ref, lb_ref, gn_ref,
             dh_ref, dlb_ref, dgn_ref, dSt, dlb_acc):
        step = pl.program_id(0)

        @pl.when(step == 0)
        def _():
            dSt[...] = jnp.zeros_like(dSt)
            dlb_acc[...] = jnp.zeros_like(dlb_acc)
            dgn_ref[...] = jnp.zeros_like(dgn_ref)

        lb, p0, p1 = _lower_bound(lb_ref, layer)
        gnh = gn_ref[...]
        gn = jnp.tile(gnh, (1, HG_HEADS))
        r16 = lax.broadcasted_iota(jnp.int32, (SUB, SUB), 0)
        c16 = lax.broadcasted_iota(jnp.int32, (SUB, SUB), 1)

        def block(jj, carry):
            j = nsub - 1 - jj
            rows = pl.ds(pl.multiple_of(j * SUB, SUB), SUB)
            q = q_ref[rows, :] * Q_SCALE
            iv = i_ref[rows, :]
            gz = g_ref[rows, :]
            o = o_ref[rows, :]
            dy = dy_ref[rows, :]
            sig, f, g, k = _gates(f_ref[rows, :], lb)
            G = _cumsum_rows(g)
            Gl = G[SUB - 1:SUB, :]
            eG = jnp.exp(G)
            edl = jnp.exp(Gl - G)
            eGl = jnp.exp(Gl)
            qt = (q * eG).astype(BF16)
            kd = (k * edl).astype(BF16)
            ib = iv.astype(BF16)
            sgz = _sigmoid(gz)
            sil = gz * sgz
            dyn = dy * sil
            on_parts, do_parts = [], []
            dgn = jnp.zeros((1, HG_D), F32)
            for h in range(HG_HEADS):
                sl = _head(h)
                oh = o[:, sl]
                rs = lax.rsqrt(jnp.mean(oh * oh, axis=-1, keepdims=True) + RMS_EPS)
                on = oh * rs
                dgn = dgn + jnp.sum(dyn[:, sl] * on, axis=0, keepdims=True)
                don = dyn[:, sl] * gnh
                do_parts.append(rs * (don - on * jnp.mean(don * on, axis=-1, keepdims=True)))
                on_parts.append(on)
            dgn_ref[...] += dgn
            on_all = jnp.concatenate(on_parts, axis=1)
            dgz = dy * on_all * gn * (sgz * (1.0 + gz * (1.0 - sgz)))
            do = jnp.concatenate(do_parts, axis=1)
            dob = do.astype(BF16)
            amat = a_ref[rows, :]
            dq_p, dk_p, di_p, tail_p = [], [], [], []
            for h in range(HG_HEADS):
                sl = _head(h)
                qh, kh, Gh = q[:, sl], k[:, sl], G[:, sl]
                Ap = jnp.where(r16 >= c16, _dnt(dob[:, sl], ib[:, sl]), 0.0)
                ApT = jnp.where(r16 <= c16, _dnt(ib[:, sl], dob[:, sl]), 0.0)
                dqh = jnp.zeros((SUB, HG_D), F32)
                dkh = jnp.zeros((SUB, HG_D), F32)
                for s in range(SUB):
                    dGs = Gh - Gh[s:s + 1, :]
                    e = jnp.exp(jnp.minimum(dGs, -dGs))
                    dqh = dqh + Ap[:, s:s + 1] * (e * kh[s:s + 1, :])
                    dkh = dkh + ApT[:, s:s + 1] * (e * qh[s:s + 1, :])
                Sb = s_ref[j, sl, :]
                dSb = dSt[h].astype(BF16)
                Am = amat[:, h * SUB:(h + 1) * SUB].astype(BF16)
                dq_p.append(dqh + eG[:, sl] * _dnn(dob[:, sl], Sb))
                dk_p.append(dkh + edl[:, sl] * _dnn(ib[:, sl], dSb))
                di_p.append(_dtn(Am, dob[:, sl]) + _dnt(kd[:, sl], dSb))
                St_end = eGl[:, sl] * Sb.astype(F32) + _dtn(ib[:, sl], kd[:, sl])
                tail_p.append(jnp.sum(dSt[h] * St_end, axis=0, keepdims=True))
                dSt[h] = eGl[:, sl] * dSt[h] + _dtn(dob[:, sl], qt[:, sl])
            dq = jnp.concatenate(dq_p, axis=1)
            dk = jnp.concatenate(dk_p, axis=1)
            di = jnp.concatenate(di_p, axis=1)
            dg = _cumsum_rows(q * dq - k * dk, reverse=True) + jnp.concatenate(tail_p, axis=1)
            df = jnp.where(f > F_FLOOR, dg / f, 0.0)
            dfk = df - dk
            dfz = (1.0 - lb) * dfk * sig * (1.0 - sig)
            dlb_acc[...] += jnp.sum(dfk * (1.0 - sig), axis=0, keepdims=True)
            dh_ref[rows, :] = jnp.concatenate([dq * Q_SCALE, dfz, di, dgz], axis=1).astype(BF16)
            return carry

        lax.fori_loop(0, nsub, block, 0)

        @pl.when(step == n - 1)
        def _():
            if layer == 0:
                dlb_ref[...] = jnp.zeros_like(dlb_ref)
            else:
                dz1 = p0 * p1 * dlb_acc[...]
                dlb_ref[...] = jnp.concatenate([-dz1, dz1], axis=0)

    rev = lambda i: n - 1 - i
    col = lambda c: pl.BlockSpec((ts, HG_W), lambda i: (rev(i), c))
    return pl.pallas_call(
        body, name=name, grid=(n,),
        in_specs=[col(0), col(1), col(2), col(3), col(0),
                  pl.BlockSpec((nsub, HG_W, HG_D), lambda i: (rev(i), 0, 0)),
                  pl.BlockSpec((ts, HG_HEADS * SUB), lambda i: (rev(i), 0)), col(0),
                  pl.BlockSpec((2, HG_W), lambda i: (0, 0)), pl.BlockSpec((1, HG_D), lambda i: (0, 0))],
        out_specs=[pl.BlockSpec((ts, 4 * HG_W), lambda i: (rev(i), 0)),
                   pl.BlockSpec((2, HG_W), lambda i: (0, 0)), pl.BlockSpec((1, HG_D), lambda i: (0, 0))],
        out_shape=[jax.ShapeDtypeStruct((L, 4 * HG_W), BF16), jax.ShapeDtypeStruct((2, HG_W), F32),
                   jax.ShapeDtypeStruct((1, HG_D), F32)],
        scratch_shapes=[pltpu.VMEM((HG_HEADS, HG_D, HG_D), F32), pltpu.VMEM((1, HG_W), F32)],
        compiler_params=_params(("arbitrary",)),
    )(hh, hh, hh, hh, o_raw, states, amat, dcat, lbp, gnorm)


def _adamw_body(gp_ref, w_ref, m_ref, v_ref, g_ref, d_ref, mo_ref, vo_ref):
    c1 = 1.0 - ADAM_B1 ** ADAM_STEP
    c2 = 1.0 - ADAM_B2 ** ADAM_STEP
    g = gp_ref[0].astype(F32)
    for k in range(1, N_DEV):
        g = g + gp_ref[k].astype(F32)
    mn = ADAM_B1 * m_ref[...] + (1.0 - ADAM_B1) * g
    vn = ADAM_B2 * v_ref[...] + (1.0 - ADAM_B2) * (g * g)
    m_hat = mn / c1
    v_hat = vn / c2
    g_ref[...] = g
    d_ref[...] = -ADAM_LR * (m_hat / (jnp.sqrt(v_hat) + ADAM_EPS) + ADAM_WD * w_ref[...])
    mo_ref[...] = mn
    vo_ref[...] = vn


def _adamw_layers(gparts, w, m, v, name):
    depth, R, C = w.shape
    tr = _tile(R, 256, 16)
    nr = R // tr

    def body(*refs):
        layer = pl.program_id(0)
        for d in range(depth):
            @pl.when(layer == d)
            def _(d=d):
                _adamw_body(refs[d], *refs[depth:])

    def parts_spec(d):
        return pl.BlockSpec((N_DEV, tr, C),
                            lambda l, i: (0, jnp.where(l == d, i, jnp.where(l < d, 0, nr - 1)), 0))

    blk = pl.BlockSpec((None, tr, C), lambda l, i: (l, i, 0))
    shp = jax.ShapeDtypeStruct((depth, R, C), F32)
    return pl.pallas_call(
        body, name=name, grid=(depth, nr),
        in_specs=[parts_spec(d) for d in range(depth)] + [blk, blk, blk],
        out_specs=[blk, blk, blk, blk], out_shape=[shp, shp, shp, shp],
        compiler_params=_params(("arbitrary", "arbitrary")),
    )(*gparts, w, m, v)


def _adamw(gparts, w, m, v, name):
    R = w.shape[0]
    tr = _tile(R, 1024, 16) if R % 16 == 0 else R

    def body(*refs):
        _adamw_body(*refs)

    row = pl.BlockSpec((tr, LANES), lambda i: (i, 0))
    shp = jax.ShapeDtypeStruct((R, LANES), F32)
    return pl.pallas_call(
        body, name=name, grid=(R // tr,),
        in_specs=[pl.BlockSpec((N_DEV, tr, LANES), lambda i: (0, i, 0)), row, row, row],
        out_specs=[row, row, row, row], out_shape=[shp, shp, shp, shp],
        compiler_params=_params(("parallel",)),
    )(gparts, w, m, v)


def _flip(coord, bit):
    return 1 - coord if bit else coord


def _gather_many(blocks, name):
    n = len(blocks)

    def body(*refs):
        x_refs, out_refs = refs[:n], refs[n:2 * n]
        send_sems, recv_sems, local_sems = refs[2 * n:]
        x, y, c = lax.axis_index("x"), lax.axis_index("y"), lax.axis_index("c")
        me, sibling = (x, y, c), (x, y, 1 - c)
        chips = [(1 - x, y), (x, 1 - y), (1 - x, 1 - y)]

        def slot(a, px, py, pc):
            return out_refs[a].at[4 * px + 2 * py + pc]

        def copy(a, k, blk, to, src=None):
            return pltpu.make_async_remote_copy(
                src_ref=slot(a, *blk) if src is None else src, dst_ref=slot(a, *blk),
                send_sem=send_sems.at[7 * a + k], recv_sem=recv_sems.at[7 * a + k],
                device_id=to, device_id_type=pl.DeviceIdType.MESH)

        mine = [pltpu.make_async_copy(x_refs[a], slot(a, *me), local_sems.at[a]) for a in range(n)]
        for cp in mine:
            cp.start()
        first = [copy(a, 0, me, sibling, src=x_refs[a]) for a in range(n)]
        for j, chip in enumerate(chips):
            first += [copy(a, 1 + j, me, (*chip, c), src=x_refs[a]) for a in range(n)]
        for cp in first:
            cp.start()
        passed = []
        for j, chip in enumerate(chips):
            for a in range(n):
                copy(a, 1 + j, (*chip, c), me).wait_recv()
                fwd = copy(a, 4 + j, (*chip, c), sibling)
                fwd.start()
                passed.append(fwd)
        for a in range(n):
            copy(a, 0, sibling, me).wait_recv()
        for j, chip in enumerate(chips):
            for a in range(n):
                copy(a, 4 + j, (*chip, 1 - c), me).wait_recv()
        for cp in first + passed:
            cp.wait_send()
        for cp in mine:
            cp.wait()

    hbm = pl.BlockSpec(memory_space=pl.ANY)
    return pl.pallas_call(
        body, name=name,
        out_shape=[jax.ShapeDtypeStruct((N_DEV,) + b.shape, b.dtype) for b in blocks],
        in_specs=[hbm] * n, out_specs=[hbm] * n,
        scratch_shapes=[pltpu.SemaphoreType.DMA((7 * n,)), pltpu.SemaphoreType.DMA((7 * n,)),
                        pltpu.SemaphoreType.DMA((n,))],
    )(*blocks)


def _split_start(blocks, chunked, name):
    n = len(blocks)
    lands = [lax.empty(b.shape if chunked else (N_DEV,) + b.shape, b.dtype) for b in blocks]

    def body(*refs):
        x_refs, land_refs = refs[:n], refs[n:2 * n]
        send_sems, recv_sems, token = refs[2 * n], refs[2 * n + 1], refs[-1]
        x, y, c = lax.axis_index("x"), lax.axis_index("y"), lax.axis_index("c")
        me = 4 * x + 2 * y + c
        for a in range(n):
            for k in range(1, N_DEV):
                px, py, pc = _flip(x, k & 4), _flip(y, k & 2), _flip(c, k & 1)
                pltpu.make_async_remote_copy(
                    src_ref=x_refs[a].at[4 * px + 2 * py + pc] if chunked else x_refs[a],
                    dst_ref=land_refs[a].at[me],
                    send_sem=send_sems.at[7 * a + k - 1], recv_sem=recv_sems.at[7 * a + k - 1],
                    device_id=(px, py, pc), device_id_type=pl.DeviceIdType.MESH).start()
        token[...] = jnp.zeros_like(token)

    hbm = pl.BlockSpec(memory_space=pltpu.HBM)
    sem = pl.BlockSpec(memory_space=pltpu.SEMAPHORE)
    outs = pl.pallas_call(
        body, name=name,
        out_shape=(pltpu.SemaphoreType.DMA((7 * n,)), pltpu.SemaphoreType.DMA((7 * n,)),
                   *[pltpu.HBM(b.shape, b.dtype) for b in blocks], *[pltpu.HBM(l.shape, l.dtype) for l in lands],
                   jax.ShapeDtypeStruct((8, LANES), F32)),
        in_specs=[hbm] * (2 * n),
        out_specs=(sem, sem, *[hbm] * (2 * n), pl.BlockSpec(memory_space=pltpu.VMEM)),
        input_output_aliases={i: 2 + i for i in range(2 * n)},
        compiler_params=pltpu.CompilerParams(has_side_effects=pltpu.SideEffectType.DATAFLOW_SIDE_EFFECTING),
    )(*[pltpu.with_memory_space_constraint(b, pltpu.HBM) for b in blocks],
      *[pltpu.with_memory_space_constraint(l, pltpu.HBM) for l in lands])
    return outs[0], outs[1], list(outs[2:2 + n]), list(outs[2 + n:2 + 2 * n]), outs[-1]


def _split_wait(started, chunked, after, name):
    send_sems, recv_sems, blocks, lands, _ = started
    n = len(blocks)

    def body(*refs):
        x_refs, land_refs = refs[:n], refs[n:2 * n]
        send_sems, recv_sems = refs[2 * n], refs[2 * n + 1]
        x, y, c = lax.axis_index("x"), lax.axis_index("y"), lax.axis_index("c")
        for a in range(n):
            for k in range(1, N_DEV):
                px, py, pc = _flip(x, k & 4), _flip(y, k & 2), _flip(c, k & 1)
                copy = pltpu.make_async_remote_copy(
                    src_ref=x_refs[a].at[4 * px + 2 * py + pc] if chunked else x_refs[a],
                    dst_ref=land_refs[a].at[4 * px + 2 * py + pc],
                    send_sem=send_sems.at[7 * a + k - 1], recv_sem=recv_sems.at[7 * a + k - 1],
                    device_id=(px, py, pc), device_id_type=pl.DeviceIdType.MESH)
                copy.wait_send()
                copy.wait_recv()

    hbm = pl.BlockSpec(memory_space=pltpu.HBM)
    sem = pl.BlockSpec(memory_space=pltpu.SEMAPHORE)
    outs = pl.pallas_call(
        body, name=name,
        out_shape=(*[pltpu.HBM(b.shape, b.dtype) for b in blocks], *[pltpu.HBM(l.shape, l.dtype) for l in lands]),
        in_specs=[hbm] * (2 * n) + [sem, sem, pl.BlockSpec(memory_space=pl.ANY)],
        out_specs=[hbm] * (2 * n),
        input_output_aliases={i: i for i in range(2 * n)},
        compiler_params=pltpu.CompilerParams(has_side_effects=pltpu.SideEffectType.DATAFLOW_SIDE_EFFECTING),
    )(*blocks, *lands, send_sems, recv_sems, after)
    me = 4 * lax.axis_index("x") + 2 * lax.axis_index("y") + lax.axis_index("c")
    own = [lax.dynamic_index_in_dim(b, me, 0, keepdims=False) if chunked else b for b in outs[:n]]
    return [lax.dynamic_update_index_in_dim(z, o, me, 0) for z, o in zip(outs[n:], own)]


def _exchange_grads(layer_chunks, small_chunks, rep_block, name):
    flows, inputs = [], []
    for p, per_layer in enumerate(layer_chunks):
        for l, arr in enumerate(per_layer):
            flows.append(("param", p, l))
            inputs.append(arr)
    flows += [("small",), ("rep",)]
    inputs += [small_chunks, rep_block]
    n_par = len(layer_chunks)
    n_in, n_out, nf = len(inputs), n_par + 2, len(flows)

    def body(*refs):
        in_refs, out_refs = refs[:n_in], refs[n_in:n_in + n_out]
        send_sems, recv_sems, local_sems = refs[n_in + n_out:]
        x, y, c = lax.axis_index("x"), lax.axis_index("y"), lax.axis_index("c")
        me = 4 * x + 2 * y + c

        def src(f, dev):
            return in_refs[f] if flows[f][0] == "rep" else in_refs[f].at[dev]

        def dst(f, dev):
            if flows[f][0] == "param":
                _, p, l = flows[f]
                return out_refs[p].at[dev, l]
            return out_refs[n_par + (0 if flows[f][0] == "small" else 1)].at[dev]

        mine = [pltpu.make_async_copy(src(f, me), dst(f, me), local_sems.at[f]) for f in range(nf)]
        for cp in mine:
            cp.start()
        copies = []
        for k in range(1, N_DEV):
            px, py, pc = _flip(x, k & 4), _flip(y, k & 2), _flip(c, k & 1)
            peer = 4 * px + 2 * py + pc
            for f in range(nf):
                sems = dict(send_sem=send_sems.at[7 * f + k - 1], recv_sem=recv_sems.at[7 * f + k - 1],
                            device_id=(px, py, pc), device_id_type=pl.DeviceIdType.MESH)
                send = pltpu.make_async_remote_copy(src_ref=src(f, peer), dst_ref=dst(f, me), **sems)
                recv = pltpu.make_async_remote_copy(src_ref=src(f, peer), dst_ref=dst(f, peer), **sems)
                send.start()
                copies.append((send, recv))
        for send, recv in copies:
            recv.wait_recv()
        for send, recv in copies:
            send.wait_send()
        for cp in mine:
            cp.wait()

    out_shape = [jax.ShapeDtypeStruct((N_DEV, len(pl_)) + pl_[0].shape[1:], pl_[0].dtype) for pl_ in layer_chunks]
    out_shape += [jax.ShapeDtypeStruct(small_chunks.shape, small_chunks.dtype),
                  jax.ShapeDtypeStruct((N_DEV,) + rep_block.shape, rep_block.dtype)]
    hbm = pl.BlockSpec(memory_space=pl.ANY)
    return pl.pallas_call(
        body, name=name, out_shape=out_shape,
        in_specs=[hbm] * n_in, out_specs=[hbm] * n_out,
        scratch_shapes=[pltpu.SemaphoreType.DMA((7 * nf,)), pltpu.SemaphoreType.DMA((7 * nf,)),
                        pltpu.SemaphoreType.DMA((nf,))],
    )(*inputs)


def _pack_rows(size):
    return -(-size // (8 * LANES)) * 8


def _pack(arrs, dtype):
    parts, offs, r = [], [], 0
    for a in arrs:
        flat = a.astype(dtype).reshape(-1)
        nrow = _pack_rows(flat.shape[0])
        flat = jnp.pad(flat, (0, nrow * LANES - flat.shape[0]))
        parts.append(flat.reshape(nrow, LANES))
        offs.append((r, nrow))
        r += nrow
    return jnp.concatenate(parts, axis=0), offs


def _unpack(buf, offs, shapes, lead=()):
    outs = []
    for (r, nrow), shp in zip(offs, shapes):
        size = 1
        for s in shp:
            size *= s
        flat = buf[..., r:r + nrow, :].reshape(lead + (nrow * LANES,))
        outs.append(flat[..., :size].reshape(lead + tuple(shp)))
    return outs


def _cols_from_shards(g, axis):
    return jnp.concatenate([g[j] for j in range(N_DEV)], axis=axis)


BIG = ("w_in", "w_o", "w_up", "w_down")
SMALL_SHARDED = ("meta_tokens", "w_conv", "w_ffn_conv")
REPLICATED = ("hg_lower_bounds", "w_pool", "pool_scale", "hg_norm_g", "ln1_g", "ln1_b", "b_ffn_conv", "ln2_g", "ln2_b")
WEIGHTS = ("meta_tokens", "hg_lower_bounds", "w_in", "w_conv", "w_pool", "pool_scale", "hg_norm_g", "w_o",
           "ln1_g", "ln1_b", "w_up", "w_ffn_conv", "b_ffn_conv", "w_down", "ln2_g", "ln2_b")


def _pool_blockdiag(w_pool_l):
    z = jnp.zeros((POOL_GROUP, POOL_GROUP), w_pool_l.dtype)
    rows = [jnp.concatenate([w_pool_l[g] if h == g else z for h in range(4)], axis=1) for g in range(4)]
    return jnp.concatenate(rows, axis=0)


def _mixer_weights(g_in, g_o):
    w_in = jnp.transpose(g_in, (1, 0, 2)).reshape(D_MODEL, -1)
    w_o = g_o.reshape(-1, D_MODEL)
    return dict(
        w_hg=w_in[:, 768:2816],
        w_cp=jnp.concatenate([w_in[:, 0:768], w_in[:, 2816:3072]], axis=1),
        w_o=jnp.concatenate([w_o[256:768], w_o[0:256], w_o[768:1024]], axis=0))


def _ffn_weights(g_up, g_down):
    return dict(w_up=jnp.transpose(g_up, (1, 0, 2)).reshape(D_MODEL, -1), w_down=g_down.reshape(-1, D_MODEL))


def kernel(x, meta_tokens, hg_lower_bounds, w_in, w_conv, w_pool, pool_scale, hg_norm_g, w_o, ln1_g, ln1_b, w_up, w_ffn_conv, b_ffn_conv, w_down, ln2_g, ln2_b, loss_target, m_meta_tokens, m_hg_lower_bounds, m_w_in, m_w_conv, m_w_pool, m_pool_scale, m_hg_norm_g, m_w_o, m_ln1_g, m_ln1_b, m_w_up, m_w_ffn_conv, m_b_ffn_conv, m_w_down, m_ln2_g, m_ln2_b, v_meta_tokens, v_hg_lower_bounds, v_w_in, v_w_conv, v_w_pool, v_pool_scale, v_hg_norm_g, v_w_o, v_ln1_g, v_ln1_b, v_w_up, v_w_ffn_conv, v_b_ffn_conv, v_w_down, v_ln2_g, v_ln2_b):
    W = dict(meta_tokens=meta_tokens, hg_lower_bounds=hg_lower_bounds, w_in=w_in, w_conv=w_conv, w_pool=w_pool,
             pool_scale=pool_scale, hg_norm_g=hg_norm_g, w_o=w_o, ln1_g=ln1_g, ln1_b=ln1_b, w_up=w_up,
             w_ffn_conv=w_ffn_conv, b_ffn_conv=b_ffn_conv, w_down=w_down, ln2_g=ln2_g, ln2_b=ln2_b)
    M = dict(meta_tokens=m_meta_tokens, hg_lower_bounds=m_hg_lower_bounds, w_in=m_w_in, w_conv=m_w_conv,
             w_pool=m_w_pool, pool_scale=m_pool_scale, hg_norm_g=m_hg_norm_g, w_o=m_w_o, ln1_g=m_ln1_g,
             ln1_b=m_ln1_b, w_up=m_w_up, w_ffn_conv=m_w_ffn_conv, b_ffn_conv=m_b_ffn_conv, w_down=m_w_down,
             ln2_g=m_ln2_g, ln2_b=m_ln2_b)
    V = dict(meta_tokens=v_meta_tokens, hg_lower_bounds=v_hg_lower_bounds, w_in=v_w_in, w_conv=v_w_conv,
             w_pool=v_w_pool, pool_scale=v_pool_scale, hg_norm_g=v_hg_norm_g, w_o=v_w_o, ln1_g=v_ln1_g,
             ln1_b=v_ln1_b, w_up=v_w_up, w_ffn_conv=v_w_ffn_conv, b_ffn_conv=v_b_ffn_conv, w_down=v_w_down,
             ln2_g=v_ln2_g, ln2_b=v_ln2_b)
    assert x.shape[0] == 1 and x.shape[2] == D_MODEL and w_in.shape[0] == DEPTH
    seq = x.shape[1]
    L = -(-(seq + N_META) // ROW_ALIGN) * ROW_ALIGN

    small_pack, small_offs = _pack([W[n] for n in SMALL_SHARDED], F32)
    shards = {n: [W[n][l].astype(BF16) for l in range(DEPTH)] for n in BIG}
    g_in0, g_o0, small_all = _gather_many([shards["w_in"][0], shards["w_o"][0], small_pack], "gather_weights")
    full = {}
    for n, a in zip(SMALL_SHARDED, _unpack(small_all, small_offs, [W[n].shape for n in SMALL_SHARDED], (N_DEV,))):
        full[n] = _cols_from_shards(a, 1)
    order = (small_all[0, 0, 0] * 0.0).astype(BF16)
    ffn0_started = _split_start([shards["w_up"][0] + order, shards["w_down"][0] + order], False, "gather_ffn0_start")
    order = ffn0_started[4][0, 0].astype(BF16)
    layer1_started = _split_start([shards[n][1] + order for n in BIG], False, "gather_layer1_start")
    lb_in = hg_lower_bounds + layer1_started[4][0, 0]

    pad_rows = L - N_META - seq
    xp = jnp.concatenate([full["meta_tokens"], x[0], jnp.zeros((pad_rows, D_MODEL), F32)], axis=0)
    tgt = jnp.concatenate([jnp.zeros((N_META, D_MODEL), F32), loss_target[0], jnp.zeros((pad_rows, D_MODEL), F32)], axis=0)

    saved = []
    h_in, h_in_b = xp, xp.astype(BF16)
    for l in range(DEPTH):
        if l == 0:
            lw = _mixer_weights(g_in0, g_o0)
        else:
            g_in, g_o, g_up, g_down = _split_wait(layer1_started, False, h_in_b, "gather_layer1_wait")
            lw = {**_mixer_weights(g_in, g_o), **_ffn_weights(g_up, g_down)}
        wc = full["w_conv"][l].T
        wblk = _pool_blockdiag(w_pool[l]).astype(BF16)
        ps = pool_scale[l][None, :]
        gn = hg_norm_g[l][None, :]
        wf = full["w_ffn_conv"][l].T
        bf = b_ffn_conv[l][None, :]
        hh = _matmul(h_in_b, lw["w_hg"], "nn", F32, f"fwd_hg_{l}")
        hc = _matmul(h_in_b, lw["w_cp"], "nn", F32, f"fwd_cp_{l}")
        y_hg, o_raw, states, amat = _hgrn_fwd(hh, lb_in if l == 0 else hg_lower_bounds, gn, l, f"hgrn_fwd_{l}")
        y_cp = _cp_fwd(hc, wc, wblk, ps, f"convpool_fwd_{l}")
        cat = jnp.concatenate([y_hg, y_cp], axis=1)
        z1, x1, x1_b = _matmul_ln(cat, lw["w_o"], h_in, ln1_g[l][None, :], ln1_b[l][None, :], f"fwd_o_ln1_{l}")
        if l == 0:
            lw.update(_ffn_weights(*_split_wait(ffn0_started, False, x1_b, "gather_ffn0_wait")))
        up = _matmul(x1_b, lw["w_up"], "nn", BF16, f"fwd_up_{l}")
        a = _ffn_act_fwd(up, wf, bf, f"ffn_fwd_{l}")
        saved.append(dict(lw=lw, wc=wc, wblk=wblk, ps=ps, gn=gn, wf=wf, bf=bf, x_b=h_in_b, hh=hh, hc=hc,
                          o_raw=o_raw, states=states, amat=amat, cat=cat, z1=z1, x1_b=x1_b, up=up, a=a))
        if l < DEPTH - 1:
            saved[l]["z2"], h_in, h_in_b = _matmul_ln(a, lw["w_down"], x1, ln2_g[l][None, :], ln2_b[l][None, :],
                                                     f"fwd_down_ln2_{l}")
        else:
            saved[l]["z2"], dy, loss_part = _matmul_ln(a, lw["w_down"], x1, ln2_g[l][None, :], ln2_b[l][None, :],
                                                       f"fwd_down_ln2_loss_{l}", loss=(tgt, seq))

    loss = lax.psum(loss_part[0, 0], ("x", "y", "c"))

    G = {}
    per_layer = {n: [None] * DEPTH for n in ("w_conv", "w_pool", "pool_scale", "hg_norm_g", "ln1_g", "ln1_b",
                                             "w_ffn_conv", "b_ffn_conv", "ln2_g", "ln2_b")}
    ffn_started, mix_started = [None] * DEPTH, [None] * DEPTH
    order = jnp.zeros((), F32)
    dlb_total = jnp.zeros((DEPTH, HG_W), F32)
    for l in reversed(range(DEPTH)):
        s = saved[l]
        lw = s["lw"]
        dz2, dz2_b, dg2, db2 = _ln_bwd(s["z2"], dy, ln2_g[l][None, :] + order, f"ln2_bwd_{l}")
        da = _matmul(dz2_b, lw["w_down"], "nt", BF16, f"bwd_da_{l}")
        d_w_down = _matmul(s["a"], dz2_b, "tn", BF16, f"wgrad_down_{l}")
        dup, dwf, dbf = _ffn_act_bwd(s["up"], da, s["wf"], s["bf"], f"ffn_bwd_{l}")
        dx1 = _matmul(dup, lw["w_up"], "nt", F32, f"bwd_dx1_{l}", res=dz2, alpha=ALPHA)
        d_w_up = _matmul(s["x1_b"], dup, "tn", BF16, f"wgrad_up_{l}")
        ffn_started[l] = _split_start([jnp.transpose(d_w_up.reshape(D_MODEL, N_DEV, -1), (1, 0, 2)),
                                       d_w_down.reshape(N_DEV, -1, D_MODEL)], True, f"scatter_ffn{l}_start")
        order = ffn_started[l][4][0, 0]
        dz1, dz1_b, dg1, db1 = _ln_bwd(s["z1"], dx1, ln1_g[l][None, :] + order, f"ln1_bwd_{l}")
        dcat = _matmul(dz1_b, lw["w_o"], "nt", F32, f"bwd_dcat_{l}")
        d_w_o = _matmul(s["cat"], dz1_b, "tn", BF16, f"wgrad_o_{l}")
        dhh, dlb, dgn = _hgrn_bwd(s["hh"], s["o_raw"], s["states"], s["amat"], dcat, hg_lower_bounds, s["gn"], l,
                                  f"hgrn_bwd_{l}")
        dhc, dwc, dwblk, dps = _cp_bwd(s["hc"], dcat, s["wc"], s["wblk"], s["ps"], f"convpool_bwd_{l}")
        dx_a = _matmul(dhh, lw["w_hg"], "nt", F32, f"bwd_dx_hg_{l}", res=dz1, alpha=ALPHA)
        dx = _matmul(dhc, lw["w_cp"], "nt", F32, f"bwd_dx_cp_{l}", res=dx_a, alpha=1.0)
        d_w_hg = _matmul(s["x_b"], dhh, "tn", BF16, f"wgrad_hg_{l}")
        d_w_cp = _matmul(s["x_b"], dhc, "tn", BF16, f"wgrad_cp_{l}")
        d_w_in = jnp.concatenate([d_w_cp[:, 0:768], d_w_hg, d_w_cp[:, 768:1024]], axis=1)
        mix_chunks = [jnp.transpose(d_w_in.reshape(D_MODEL, N_DEV, -1), (1, 0, 2)),
                      jnp.concatenate([d_w_o[512:768], d_w_o[0:512], d_w_o[768:1024]], axis=0).reshape(N_DEV, -1, D_MODEL)]
        if l > 0:
            mix_started[l] = _split_start(mix_chunks, True, f"scatter_mix{l}_start")
            order = mix_started[l][4][0, 0]
        per_layer["w_conv"][l] = dwc.T
        per_layer["w_ffn_conv"][l] = dwf.T
        per_layer["b_ffn_conv"][l] = dbf[0]
        per_layer["w_pool"][l] = jnp.stack([dwblk[g * 64:(g + 1) * 64, g * 64:(g + 1) * 64] for g in range(4)], axis=0)
        per_layer["pool_scale"][l] = dps[0]
        per_layer["hg_norm_g"][l] = dgn[0]
        per_layer["ln1_g"][l], per_layer["ln1_b"][l] = dg1[0], db1[0]
        per_layer["ln2_g"][l], per_layer["ln2_b"][l] = dg2[0], db2[0]
        dlb_total = dlb_total + dlb
        dy = dx
    for n, parts in per_layer.items():
        G[n] = jnp.stack(parts, axis=0)
    G["hg_lower_bounds"] = dlb_total
    grad_x = dy[N_META:N_META + seq][None]

    def shard_major(g, lead):
        g = g.reshape(g.shape[:lead] + (N_DEV, -1) + g.shape[lead + 1:])
        g = jnp.moveaxis(g, lead, 0).reshape(N_DEV, -1)
        nrow = _pack_rows(g.shape[1])
        return jnp.pad(g, ((0, 0), (0, nrow * LANES - g.shape[1]))).reshape(N_DEV, nrow, LANES)

    small_chunks = jnp.concatenate([shard_major(dy[0:N_META], 1), shard_major(G["w_conv"], 1),
                                    shard_major(G["w_ffn_conv"], 1)], axis=1)
    w_small, _ = _pack([W[n] for n in SMALL_SHARDED], F32)
    rep_pack, rep_offs = _pack([G[n] for n in REPLICATED], F32)
    in0, o0, small_recv, rep_all = _exchange_grads([[c] for c in mix_chunks], small_chunks, rep_pack, "exchange_grads")
    parts = {"w_in": [in0[:, 0]], "w_o": [o0[:, 0]], "w_up": [], "w_down": []}
    for l in range(DEPTH):
        up_l, down_l = _split_wait(ffn_started[l], True, rep_all, f"scatter_ffn{l}_wait")
        parts["w_up"].append(up_l)
        parts["w_down"].append(down_l)
        if l > 0:
            in_l, o_l = _split_wait(mix_started[l], True, rep_all, f"scatter_mix{l}_wait")
            parts["w_in"].append(in_l)
            parts["w_o"].append(o_l)

    res = {k: {} for k in ("grad", "delta", "new_m", "new_v")}
    kinds = ("grad", "delta", "new_m", "new_v")
    for n in BIG:
        for kind, a in zip(kinds, _adamw_layers(parts[n], W[n], M[n], V[n], f"adamw_{n}")):
            res[kind][n] = a
    m_small, _ = _pack([M[n] for n in SMALL_SHARDED], F32)
    v_small, _ = _pack([V[n] for n in SMALL_SHARDED], F32)
    outs_small = _adamw(small_recv, w_small, m_small, v_small, "adamw_small_sharded")
    w_rep, _ = _pack([W[n] for n in REPLICATED], F32)
    m_rep, _ = _pack([M[n] for n in REPLICATED], F32)
    v_rep, _ = _pack([V[n] for n in REPLICATED], F32)
    outs_rep = _adamw(rep_all, w_rep, m_rep, v_rep, "adamw_replicated")
    for kind, b_sm, b_rep in zip(kinds, outs_small, outs_rep):
        for n, a in zip(SMALL_SHARDED, _unpack(b_sm, small_offs, [W[n].shape for n in SMALL_SHARDED])):
            res[kind][n] = a
        for n, a in zip(REPLICATED, _unpack(b_rep, rep_offs, [W[n].shape for n in REPLICATED])):
            res[kind][n] = a

    return (loss, grad_x, *[res["grad"][n] for n in WEIGHTS], *[res["delta"][n] for n in WEIGHTS],
            *[res["new_m"][n] for n in WEIGHTS], *[res["new_v"][n] for n in WEIGHTS])
```

```python
import jax
import jax.numpy as jnp
from jax import lax
from jax.experimental import pallas as pl
from jax.experimental.pallas import tpu as pltpu

F32 = jnp.float32
BF16 = jnp.bfloat16

N_DEV = 8
D_MODEL = 1024
N_META = 16
DEPTH = 2
CONV_W = 256
HG_W = 512
HG_D = 128
HG_HEADS = 4
POOL_W = 256
POOL_GROUP = 64
D_FF = 2816
ALPHA = (2 * DEPTH) ** 0.25
LN_EPS = 1e-5
RMS_EPS = 1e-6
F_FLOOR = 1e-30
Q_SCALE = HG_D ** -0.5
SUB = 16
SEQ_TILE = 192
FFN_TILE = 96
ROW_ALIGN = 192
LANES = 128
VMEM_LIMIT = 48 * 1024 * 1024
MATMUL_VMEM_BUDGET = 38 * 1024 * 1024

ADAM_LR = 0.001
ADAM_B1 = 0.9
ADAM_B2 = 0.999
ADAM_EPS = 1e-08
ADAM_WD = 0.01
ADAM_STEP = 10


def _tile(n, cap, mult):
    best = 0
    for t in range(mult, min(n, cap) + 1, mult):
        if n % t == 0:
            best = t
    assert best > 0, (n, cap, mult)
    return best


def _params(sem, vmem=VMEM_LIMIT):
    return pltpu.CompilerParams(dimension_semantics=sem, vmem_limit_bytes=vmem)


def _dnt(a, b):
    return lax.dot_general(a, b, (((1,), (1,)), ((), ())), preferred_element_type=F32)


def _dtn(a, b):
    return lax.dot_general(a, b, (((0,), (0,)), ((), ())), preferred_element_type=F32)


def _dnn(a, b):
    return jnp.dot(a, b, preferred_element_type=F32)


def _sigmoid(x):
    return 1.0 / (1.0 + jnp.exp(-x))


def _matmul(a, b, mode, out_dtype, name, res=None, alpha=1.0):
    if mode == "tn":
        K, M = a.shape
    else:
        M, K = a.shape
    N = b.shape[0] if mode == "nt" else b.shape[1]
    out_bytes = jnp.dtype(out_dtype).itemsize
    tn = _tile(N, 1536, LANES)
    tk = _tile(K, 1536, 16) if mode == "tn" else _tile(K, 2816, LANES)
    nk = K // tk
    use_acc = nk > 1 and out_dtype != F32
    tm = M
    for cap in (1536, 768, 384):
        tm = _tile(M, cap, 16)
        blocks = 2 * (a.dtype.itemsize * tm * tk + b.dtype.itemsize * tn * tk + out_bytes * tm * tn
                      + (4 * tm * tn if res is not None else 0)) + (4 * tm * tn if use_acc else 0)
        if blocks <= MATMUL_VMEM_BUDGET:
            break
    dims = {"nn": ((1,), (0,)), "nt": ((1,), (1,)), "tn": ((0,), (0,))}[mode]

    def body(*refs):
        a_ref, b_ref = refs[0], refs[1]
        r_ref = refs[2] if res is not None else None
        o_ref = refs[3] if res is not None else refs[2]
        acc = refs[-1] if use_acc else o_ref
        k = pl.program_id(2)
        p = lax.dot_general(a_ref[...].astype(BF16), b_ref[...].astype(BF16), (dims, ((), ())),
                            preferred_element_type=F32)

        def finish(r):
            if r_ref is not None:
                r = r + alpha * r_ref[...]
            o_ref[...] = r.astype(out_dtype)

        if nk == 1:
            finish(p)
        else:
            @pl.when(k == 0)
            def _():
                acc[...] = p

            @pl.when((k > 0) & (k < nk - 1))
            def _():
                acc[...] += p

            @pl.when(k == nk - 1)
            def _():
                finish(acc[...] + p)

    if mode == "tn":
        a_spec = pl.BlockSpec((tk, tm), lambda i, j, k: (k, i))
    else:
        a_spec = pl.BlockSpec((tm, tk), lambda i, j, k: (i, k))
    if mode == "nt":
        b_spec = pl.BlockSpec((tn, tk), lambda i, j, k: (j, k))
    else:
        b_spec = pl.BlockSpec((tk, tn), lambda i, j, k: (k, j))
    in_specs = [a_spec, b_spec]
    args = [a, b]
    if res is not None:
        in_specs.append(pl.BlockSpec((tm, tn), lambda i, j, k: (i, j)))
        args.append(res)
    return pl.pallas_call(
        body, name=name,
        grid=(M // tm, N // tn, nk),
        in_specs=in_specs,
        out_specs=pl.BlockSpec((tm, tn), lambda i, j, k: (i, j)),
        out_shape=jax.ShapeDtypeStruct((M, N), out_dtype),
        scratch_shapes=[pltpu.VMEM((tm, tn), F32)] if use_acc else [],
        compiler_params=_params(("parallel", "parallel", "arbitrary")),
    )(*args)


def _matmul_ln(a, w, x, g, b, name, loss=None):
    L, K = a.shape
    D = w.shape[1]
    tr = L
    for cap in (1536, 768, 384):
        tr = _tile(L, cap, 16)
        if 2 * (2 * tr * K + 2 * K * D + 4 * tr * D * (4 if loss else 3) + 2 * tr * D) <= MATMUL_VMEM_BUDGET:
            break

    def body(*refs):
        a_ref, w_ref, x_ref, g_ref, b_ref = refs[:5]
        z = ALPHA * x_ref[...] + _dnn(a_ref[...], w_ref[...])
        mu = jnp.mean(z, axis=-1, keepdims=True)
        zc = z - mu
        var = jnp.mean(zc * zc, axis=-1, keepdims=True)
        y = zc * lax.rsqrt(var + LN_EPS) * g_ref[...] + b_ref[...]
        if loss is None:
            z_ref, y_ref, yb_ref = refs[5:]
            y_ref[...] = y
            yb_ref[...] = y.astype(BF16)
        else:
            t_ref, z_ref, dy_ref, loss_ref = refs[5:]
            i = pl.program_id(0)

            @pl.when(i == 0)
            def _():
                loss_ref[...] = jnp.zeros_like(loss_ref)

            r = i * tr + lax.broadcasted_iota(jnp.int32, (tr, D), 0)
            valid = (r >= N_META) & (r < N_META + loss[1])
            e = jnp.where(valid, y - t_ref[...], 0.0)
            dy_ref[...] = e * (1.0 / D)
            s = jnp.sum(jnp.sum(e * e, axis=-1, keepdims=True), axis=0, keepdims=True)
            loss_ref[...] += (0.5 / D) * s
        z_ref[...] = z

    row = pl.BlockSpec((tr, D), lambda i: (i, 0))
    vec = pl.BlockSpec((1, D), lambda i: (0, 0))
    in_specs = [pl.BlockSpec((tr, K), lambda i: (i, 0)), pl.BlockSpec((K, D), lambda i: (0, 0)), row, vec, vec]
    f32_rows = jax.ShapeDtypeStruct((L, D), F32)
    if loss is None:
        args, out_specs = [a, w, x, g, b], [row, row, row]
        out_shape = [f32_rows, f32_rows, jax.ShapeDtypeStruct((L, D), BF16)]
    else:
        args, in_specs = [a, w, x, g, b, loss[0]], in_specs + [row]
        out_specs = [row, row, pl.BlockSpec((1, 1), lambda i: (0, 0))]
        out_shape = [f32_rows, f32_rows, jax.ShapeDtypeStruct((1, 1), F32)]
    return pl.pallas_call(
        body, name=name, grid=(L // tr,), in_specs=in_specs, out_specs=out_specs, out_shape=out_shape,
        compiler_params=_params(("arbitrary",) if loss else ("parallel",)),
    )(*args)


def _ln_bwd(z, dy, g, name):
    L, D = z.shape
    tr = _tile(L, 768, 16)

    def body(z_ref, dy_ref, g_ref, dz_ref, dzb_ref, dg_ref, db_ref):
        @pl.when(pl.program_id(0) == 0)
        def _():
            dg_ref[...] = jnp.zeros_like(dg_ref)
            db_ref[...] = jnp.zeros_like(db_ref)

        z = z_ref[...]
        mu = jnp.mean(z, axis=-1, keepdims=True)
        zc = z - mu
        var = jnp.mean(zc * zc, axis=-1, keepdims=True)
        rstd = lax.rsqrt(var + LN_EPS)
        xhat = zc * rstd
        dy = dy_ref[...]
        dxh = dy * g_ref[...]
        m1 = jnp.mean(dxh, axis=-1, keepdims=True)
        m2 = jnp.mean(dxh * xhat, axis=-1, keepdims=True)
        dz = rstd * (dxh - m1 - xhat * m2)
        dz_ref[...] = dz
        dzb_ref[...] = dz.astype(BF16)
        dg_ref[...] += jnp.sum(dy * xhat, axis=0, keepdims=True)
        db_ref[...] += jnp.sum(dy, axis=0, keepdims=True)

    row = pl.BlockSpec((tr, D), lambda i: (i, 0))
    vec = pl.BlockSpec((1, D), lambda i: (0, 0))
    return pl.pallas_call(
        body, name=name, grid=(L // tr,),
        in_specs=[row, row, vec], out_specs=[row, row, vec, vec],
        out_shape=[jax.ShapeDtypeStruct((L, D), F32), jax.ShapeDtypeStruct((L, D), BF16),
                   jax.ShapeDtypeStruct((1, D), F32), jax.ShapeDtypeStruct((1, D), F32)],
        compiler_params=_params(("arbitrary",)),
    )(z, dy, g)


def _shift_down(x, prev, k):
    out = pltpu.roll(x, k, 0)
    row = lax.broadcasted_iota(jnp.int32, (8, x.shape[1]), 0)
    top = out[0:8]
    for r in range(k):
        top = jnp.where(row == r, prev[8 - k + r:8 - k + r + 1, :], top)
    return top if x.shape[0] == 8 else jnp.concatenate([top, out[8:]], axis=0)


def _shift_up(x, nxt, k):
    T = x.shape[0]
    out = pltpu.roll(x, T - k, 0)
    row = lax.broadcasted_iota(jnp.int32, (8, x.shape[1]), 0)
    bot = out[T - 8:T]
    for r in range(k):
        bot = jnp.where(row == 8 - k + r, nxt[r:r + 1, :], bot)
    return bot if T == 8 else jnp.concatenate([out[:T - 8], bot], axis=0)


def _conv3(x, prev, w, b):
    return w[2:3, :] * x + w[1:2, :] * _shift_down(x, prev, 1) + w[0:1, :] * _shift_down(x, prev, 2) + b


def _ffn_act_fwd(up, w, b, name):
    L, C = up.shape
    F = C // 2
    ts = FFN_TILE
    n = L // ts

    def body(up_ref, pv_ref, w_ref, b_ref, a_ref, u_ref):
        i = pl.program_id(0)
        x = up_ref[...].astype(F32)
        prev = jnp.where(i > 0, pv_ref[...].astype(F32)[8:16], 0.0)
        u = _conv3(x, prev, w_ref[...], b_ref[...])
        u_ref[...] = u.astype(BF16)
        gate = u[:, :F]
        a_ref[...] = (gate * _sigmoid(gate) * u[:, F:]).astype(BF16)

    return pl.pallas_call(
        body, name=name, grid=(n,),
        in_specs=[pl.BlockSpec((ts, C), lambda i: (i, 0)),
                  pl.BlockSpec((16, C), lambda i: (jnp.maximum(i * (ts // 16) - 1, 0), 0)),
                  pl.BlockSpec((3, C), lambda i: (0, 0)), pl.BlockSpec((1, C), lambda i: (0, 0))],
        out_specs=[pl.BlockSpec((ts, F), lambda i: (i, 0)), pl.BlockSpec((ts, C), lambda i: (i, 0))],
        out_shape=[jax.ShapeDtypeStruct((L, F), BF16), jax.ShapeDtypeStruct((L, C), BF16)],
        compiler_params=_params(("parallel",)),
    )(up, up, w, b)


def _ffn_act_bwd(up, u, da, w, name):
    L, C = up.shape
    F = C // 2
    ts = FFN_TILE
    n = L // ts
    last16 = L // 16 - 1

    def du_of(u, da):
        gate, val = u[:, :F], u[:, F:]
        sg = _sigmoid(gate)
        dgate = da * val * (sg * (1.0 + gate * (1.0 - sg)))
        dval = da * (gate * sg)
        return jnp.concatenate([dgate, dval], axis=1)

    def body(up_ref, u_ref, un_ref, da_ref, dan_ref, w_ref, dup_ref, dw_ref, db_ref):
        i = pl.program_id(0)

        @pl.when(i == 0)
        def _():
            dw_ref[...] = jnp.zeros_like(dw_ref)
            db_ref[...] = jnp.zeros_like(db_ref)

        w = w_ref[...]
        x = up_ref[...].astype(F32)
        du = du_of(u_ref[...].astype(F32), da_ref[...].astype(F32))
        dun = jnp.where(i < n - 1, du_of(un_ref[...].astype(F32)[0:8], dan_ref[...].astype(F32)[0:8]), 0.0)
        du1 = _shift_up(du, dun, 1)
        du2 = _shift_up(du, dun, 2)
        dup_ref[...] = (w[2:3, :] * du + w[1:2, :] * du1 + w[0:1, :] * du2).astype(BF16)
        dw_ref[...] += jnp.concatenate([jnp.sum(x * du2, axis=0, keepdims=True),
                                        jnp.sum(x * du1, axis=0, keepdims=True),
                                        jnp.sum(x * du, axis=0, keepdims=True)], axis=0)
        db_ref[...] += jnp.sum(du, axis=0, keepdims=True)

    nxt = lambda i: (jnp.minimum((i + 1) * (ts // 16), last16), 0)
    return pl.pallas_call(
        body, name=name, grid=(n,),
        in_specs=[pl.BlockSpec((ts, C), lambda i: (i, 0)),
                  pl.BlockSpec((ts, C), lambda i: (i, 0)), pl.BlockSpec((16, C), nxt),
                  pl.BlockSpec((ts, F), lambda i: (i, 0)), pl.BlockSpec((16, F), nxt),
                  pl.BlockSpec((3, C), lambda i: (0, 0))],
        out_specs=[pl.BlockSpec((ts, C), lambda i: (i, 0)), pl.BlockSpec((3, C), lambda i: (0, 0)),
                   pl.BlockSpec((1, C), lambda i: (0, 0))],
        out_shape=[jax.ShapeDtypeStruct((L, C), BF16), jax.ShapeDtypeStruct((3, C), F32),
                   jax.ShapeDtypeStruct((1, C), F32)],
        compiler_params=_params(("arbitrary",)),
    )(up, u, u, da, da, w)


def _pool_window(ext, tile_rows, first_row, lead):
    T = ext.shape[0]
    sh = (lambda x, k: pltpu.roll(x, T - k, 0)) if lead else (lambda x, k: pltpu.roll(x, k, 0))
    r2 = ext + sh(ext, 1)
    r4 = r2 + sh(r2, 2)
    r8 = r4 + sh(r4, 4)
    r16 = r8 + sh(r8, 8)
    lo = 0 if lead else 16
    grp = lax.broadcasted_iota(jnp.int32, (tile_rows, POOL_W), 1) // POOL_GROUP
    pick = lambda a, b, c, d: jnp.where(grp == 0, a, jnp.where(grp == 1, b, jnp.where(grp == 2, c, d)))
    win = pick(r2[lo:lo + tile_rows], r4[lo:lo + tile_rows], r8[lo:lo + tile_rows], r16[lo:lo + tile_rows])
    return win, pick(2.0, 4.0, 8.0, 16.0)


def _pool_count(first_row, rows, wlen):
    t1 = (first_row + lax.broadcasted_iota(jnp.int32, (rows, POOL_W), 0) + 1).astype(F32)
    return jnp.minimum(t1, wlen)


def _cp_fwd(hc, wc, wblk, pscale, name):
    L = hc.shape[0]
    ts = SEQ_TILE
    n = L // ts

    def body(h_ref, hp_ref, wc_ref, wb_ref, ps_ref, y_ref):
        i = pl.program_id(0)
        h = h_ref[...]
        hp = jnp.where(i > 0, hp_ref[...], 0.0)
        cb, cc, cv, pv = h[:, 0:256], h[:, 256:512], h[:, 512:768], h[:, 768:1024]
        p = cc * cv
        pp = hp[8:16, 256:512] * hp[8:16, 512:768]
        w = wc_ref[...]
        conv = w[2:3, :] * p + w[1:2, :] * _shift_down(p, pp, 1) + w[0:1, :] * _shift_down(p, pp, 2)
        y_conv = cb * conv
        ext = jnp.concatenate([hp[:, 768:1024], pv], axis=0)
        win, wlen = _pool_window(ext, ts, i * ts, False)
        d = win / _pool_count(i * ts, ts, wlen) - pv
        y_pool = _dnn(d.astype(BF16), wb_ref[...]) * ps_ref[...]
        y_ref[...] = jnp.concatenate([y_conv, y_pool], axis=1).astype(BF16)

    return pl.pallas_call(
        body, name=name, grid=(n,),
        in_specs=[pl.BlockSpec((ts, 1024), lambda i: (i, 0)),
                  pl.BlockSpec((16, 1024), lambda i: (jnp.maximum(i * (ts // 16) - 1, 0), 0)),
                  pl.BlockSpec((3, 256), lambda i: (0, 0)), pl.BlockSpec((256, 256), lambda i: (0, 0)),
                  pl.BlockSpec((1, 256), lambda i: (0, 0))],
        out_specs=pl.BlockSpec((ts, 512), lambda i: (i, 0)),
        out_shape=jax.ShapeDtypeStruct((L, 512), BF16),
        compiler_params=_params(("parallel",)),
    )(hc, hc, wc, wblk, pscale)


def _cp_bwd(hc, dcat, wc, wblk, pscale, name):
    L = hc.shape[0]
    ts = SEQ_TILE
    n = L // ts
    last16 = L // 16 - 1

    def body(h_ref, hp_ref, hn_ref, dy_ref, dyn_ref, wc_ref, wb_ref, ps_ref,
             dh_ref, dwc_ref, dwb_ref, dps_ref):
        i = pl.program_id(0)

        @pl.when(i == 0)
        def _():
            dwc_ref[...] = jnp.zeros_like(dwc_ref)
            dwb_ref[...] = jnp.zeros_like(dwb_ref)
            dps_ref[...] = jnp.zeros_like(dps_ref)

        h = h_ref[...]
        hp = jnp.where(i > 0, hp_ref[...], 0.0)
        hn = hn_ref[...]
        dy = dy_ref[...]
        dyn = jnp.where(i < n - 1, dyn_ref[...], 0.0)
        cb, cc, cv, pv = h[:, 0:256], h[:, 256:512], h[:, 512:768], h[:, 768:1024]
        w = wc_ref[...]
        p = cc * cv
        pp = hp[8:16, 256:512] * hp[8:16, 512:768]
        p1 = _shift_down(p, pp, 1)
        p2 = _shift_down(p, pp, 2)
        conv = w[2:3, :] * p + w[1:2, :] * p1 + w[0:1, :] * p2
        dyc = dy[:, 0:256]
        dcb = dyc * conv
        dconv = dyc * cb
        dconv_n = dyn[0:8, 0:256] * hn[0:8, 0:256]
        dc1 = _shift_up(dconv, dconv_n, 1)
        dc2 = _shift_up(dconv, dconv_n, 2)
        dp = w[2:3, :] * dconv + w[1:2, :] * dc1 + w[0:1, :] * dc2
        dwc_ref[...] += jnp.concatenate([jnp.sum(p * dc2, axis=0, keepdims=True),
                                         jnp.sum(p * dc1, axis=0, keepdims=True),
                                         jnp.sum(p * dconv, axis=0, keepdims=True)], axis=0)
        ps = ps_ref[...]
        wb = wb_ref[...]
        ext = jnp.concatenate([hp[:, 768:1024], pv], axis=0)
        win, wlen = _pool_window(ext, ts, i * ts, False)
        d = win / _pool_count(i * ts, ts, wlen) - pv
        db = d.astype(BF16)
        dyp = dy[:, 256:512]
        dps_ref[...] += jnp.sum(dyp * _dnn(db, wb), axis=0, keepdims=True)
        dypre = (dyp * ps).astype(BF16)
        dwb_ref[...] += _dtn(db, dypre)
        dd = _dnt(dypre, wb)
        ddn = _dnt((dyn[:, 256:512] * ps).astype(BF16), wb)
        e = dd / _pool_count(i * ts, ts, wlen)
        en = ddn / _pool_count((i + 1) * ts, 16, wlen[0:16])
        lead, _ = _pool_window(jnp.concatenate([e, en], axis=0), ts, i * ts, True)
        dpv = lead - dd
        dh_ref[...] = jnp.concatenate([dcb, dp * cv, dp * cc, dpv], axis=1).astype(BF16)

    return pl.pallas_call(
        body, name=name, grid=(n,),
        in_specs=[pl.BlockSpec((ts, 1024), lambda i: (i, 0)),
                  pl.BlockSpec((16, 1024), lambda i: (jnp.maximum(i * (ts // 16) - 1, 0), 0)),
                  pl.BlockSpec((16, 1024), lambda i: (jnp.minimum((i + 1) * (ts // 16), last16), 0)),
                  pl.BlockSpec((ts, 512), lambda i: (i, 1)),
                  pl.BlockSpec((16, 512), lambda i: (jnp.minimum((i + 1) * (ts // 16), last16), 1)),
                  pl.BlockSpec((3, 256), lambda i: (0, 0)), pl.BlockSpec((256, 256), lambda i: (0, 0)),
                  pl.BlockSpec((1, 256), lambda i: (0, 0))],
        out_specs=[pl.BlockSpec((ts, 1024), lambda i: (i, 0)), pl.BlockSpec((3, 256), lambda i: (0, 0)),
                   pl.BlockSpec((256, 256), lambda i: (0, 0)), pl.BlockSpec((1, 256), lambda i: (0, 0))],
        out_shape=[jax.ShapeDtypeStruct((L, 1024), BF16), jax.ShapeDtypeStruct((3, 256), F32),
                   jax.ShapeDtypeStruct((256, 256), F32), jax.ShapeDtypeStruct((1, 256), F32)],
        compiler_params=_params(("arbitrary",)),
    )(hc, hc, hc, dcat, dcat, wc, wblk, pscale)


def _lower_bound(lb_ref, layer):
    b0, b1 = lb_ref[0:1, :], lb_ref[1:2, :]
    m = jnp.maximum(b0, b1)
    e0, e1 = jnp.exp(b0 - m), jnp.exp(b1 - m)
    p0, p1 = e0 / (e0 + e1), e1 / (e0 + e1)
    lb = (p0 - p0) if layer == 0 else ((p0 + p1) - p0)
    return lb, p0, p1


def _cumsum_rows(x, reverse=False):
    row = lax.broadcasted_iota(jnp.int32, x.shape, 0)
    for sh in (1, 2, 4, 8):
        if reverse:
            x = x + jnp.where(row < SUB - sh, pltpu.roll(x, SUB - sh, 0), 0.0)
        else:
            x = x + jnp.where(row >= sh, pltpu.roll(x, sh, 0), 0.0)
    return x


def _gates(fz, lb):
    sig = _sigmoid(fz)
    f = lb + (1.0 - lb) * sig
    g = jnp.log(jnp.maximum(f, F_FLOOR))
    k = (1.0 - lb) * (1.0 - sig)
    return sig, f, g, k


def _head(h):
    return slice(h * HG_D, (h + 1) * HG_D)


def _hgrn_fwd(hh, lbp, gnorm, layer, name):
    L = hh.shape[0]
    ts = SEQ_TILE
    n = L // ts
    nsub = ts // SUB

    def body(q_ref, f_ref, i_ref, g_ref, lb_ref, gn_ref, y_ref, o_ref, s_ref, a_ref, St):
        @pl.when(pl.program_id(0) == 0)
        def _():
            St[...] = jnp.zeros_like(St)

        lb, _, _ = _lower_bound(lb_ref, layer)
        gn = jnp.tile(gn_ref[...], (1, HG_HEADS))
        r16 = lax.broadcasted_iota(jnp.int32, (SUB, SUB), 0)
        c16 = lax.broadcasted_iota(jnp.int32, (SUB, SUB), 1)

        def block(j, carry):
            rows = pl.ds(pl.multiple_of(j * SUB, SUB), SUB)
            q = q_ref[rows, :] * Q_SCALE
            iv = i_ref[rows, :]
            gz = g_ref[rows, :]
            _, _, g, k = _gates(f_ref[rows, :], lb)
            G = _cumsum_rows(g)
            Gl = G[SUB - 1:SUB, :]
            qt = (q * jnp.exp(G)).astype(BF16)
            kd = (k * jnp.exp(Gl - G)).astype(BF16)
            eGl = jnp.exp(Gl)
            ib = iv.astype(BF16)
            A = [jnp.zeros((SUB, SUB), F32) for _ in range(HG_HEADS)]
            for s in range(SUB):
                P = q * jnp.exp(jnp.minimum(G - G[s:s + 1, :], 0.0)) * k[s:s + 1, :]
                for h in range(HG_HEADS):
                    A[h] = jnp.where(c16 == s, jnp.sum(P[:, _head(h)], axis=-1, keepdims=True), A[h])
            outs, ons, amats = [], [], []
            for h in range(HG_HEADS):
                sl = _head(h)
                Sb = St[h].astype(BF16)
                s_ref[j, sl, :] = Sb
                Am = jnp.where(r16 >= c16, A[h], 0.0)
                amats.append(Am)
                o = _dnt(qt[:, sl], Sb) + _dnn(Am.astype(BF16), ib[:, sl])
                St[h] = eGl[:, sl] * St[h] + _dtn(ib[:, sl], kd[:, sl])
                outs.append(o)
                ons.append(o * lax.rsqrt(jnp.mean(o * o, axis=-1, keepdims=True) + RMS_EPS))
            a_ref[rows, :] = jnp.concatenate(amats, axis=1)
            o_ref[rows, :] = jnp.concatenate(outs, axis=1)
            y = jnp.concatenate(ons, axis=1) * gn * (gz * _sigmoid(gz))
            y_ref[rows, :] = y.astype(BF16)
            return carry

        lax.fori_loop(0, nsub, block, 0, unroll=2)

    col = lambda c: pl.BlockSpec((ts, HG_W), lambda i: (i, c))
    return pl.pallas_call(
        body, name=name, grid=(n,),
        in_specs=[col(0), col(1), col(2), col(3), pl.BlockSpec((2, HG_W), lambda i: (0, 0)),
                  pl.BlockSpec((1, HG_D), lambda i: (0, 0))],
        out_specs=[pl.BlockSpec((ts, HG_W), lambda i: (i, 0)), pl.BlockSpec((ts, HG_W), lambda i: (i, 0)),
                   pl.BlockSpec((nsub, HG_W, HG_D), lambda i: (i, 0, 0)),
                   pl.BlockSpec((ts, HG_HEADS * SUB), lambda i: (i, 0))],
        out_shape=[jax.ShapeDtypeStruct((L, HG_W), BF16), jax.ShapeDtypeStruct((L, HG_W), F32),
                   jax.ShapeDtypeStruct((L // SUB, HG_W, HG_D), BF16),
                   jax.ShapeDtypeStruct((L, HG_HEADS * SUB), F32)],
        scratch_shapes=[pltpu.VMEM((HG_HEADS, HG_D, HG_D), F32)],
        compiler_params=_params(("arbitrary",)),
    )(hh, hh, hh, hh, lbp, gnorm)


def _hgrn_bwd(hh, o_raw, states, amat, dcat, lbp, gnorm, layer, name):
    L = hh.shape[0]
    ts = SEQ_TILE
    n = L // ts
    nsub = ts // SUB

    def body(q_ref, f_ref, i_ref, g_ref, o_ref, s_ref, a_ref, dy_ref, lb_ref, gn_ref,
             dh_ref, dlb_ref, dgn_ref, dSt, dlb_acc, S_next):
        step = pl.program_id(0)

        @pl.when(step == 0)
        def _():
            dSt[...] = jnp.zeros_like(dSt)
            S_next[...] = jnp.zeros_like(S_next)
            dlb_acc[...] = jnp.zeros_like(dlb_acc)
            dgn_ref[...] = jnp.zeros_like(dgn_ref)

        lb, p0, p1 = _lower_bound(lb_ref, layer)
        gnh = gn_ref[...]
        gn = jnp.tile(gnh, (1, HG_HEADS))
        r16 = lax.broadcasted_iota(jnp.int32, (SUB, SUB), 0)
        c16 = lax.broadcasted_iota(jnp.int32, (SUB, SUB), 1)

        def block(jj, carry):
            j = nsub - 1 - jj
            rows = pl.ds(pl.multiple_of(j * SUB, SUB), SUB)
            q = q_ref[rows, :] * Q_SCALE
            iv = i_ref[rows, :]
            gz = g_ref[rows, :]
            o = o_ref[rows, :]
            dy = dy_ref[rows, :]
            sig, f, g, k = _gates(f_ref[rows, :], lb)
            G = _cumsum_rows(g)
            Gl = G[SUB - 1:SUB, :]
            eG = jnp.exp(G)
            edl = jnp.exp(Gl - G)
            eGl = jnp.exp(Gl)
            qt = (q * eG).astype(BF16)
            kd = (k * edl).astype(BF16)
            ib = iv.astype(BF16)
            sgz = _sigmoid(gz)
            sil = gz * sgz
            dyn = dy * sil
            on_parts, do_parts = [], []
            dgn = jnp.zeros((1, HG_D), F32)
            for h in range(HG_HEADS):
                sl = _head(h)
                oh = o[:, sl]
                rs = lax.rsqrt(jnp.mean(oh * oh, axis=-1, keepdims=True) + RMS_EPS)
                on = oh * rs
                dgn = dgn + jnp.sum(dyn[:, sl] * on, axis=0, keepdims=True)
                don = dyn[:, sl] * gnh
                do_parts.append(rs * (don - on * jnp.mean(don * on, axis=-1, keepdims=True)))
                on_parts.append(on)
            dgn_ref[...] += dgn
            on_all = jnp.concatenate(on_parts, axis=1)
            dgz = dy * on_all * gn * (sgz * (1.0 + gz * (1.0 - sgz)))
            do = jnp.concatenate(do_parts, axis=1)
            dob = do.astype(BF16)
            amat = a_ref[rows, :]
            dq_p, dk_p, di_p, tail_p = [], [], [], []
            for h in range(HG_HEADS):
                sl = _head(h)
                qh, kh, Gh = q[:, sl], k[:, sl], G[:, sl]
                Ap = jnp.where(r16 >= c16, _dnt(dob[:, sl], ib[:, sl]), 0.0)
                ApT = jnp.where(r16 <= c16, _dnt(ib[:, sl], dob[:, sl]), 0.0)
                dqh = jnp.zeros((SUB, HG_D), F32)
                dkh = jnp.zeros((SUB, HG_D), F32)
                for s in range(SUB):
                    dGs = Gh - Gh[s:s + 1, :]
                    e = jnp.exp(jnp.minimum(dGs, -dGs))
                    dqh = dqh + Ap[:, s:s + 1] * (e * kh[s:s + 1, :])
                    dkh = dkh + ApT[:, s:s + 1] * (e * qh[s:s + 1, :])
                Sb = s_ref[j, sl, :]
                dSb = dSt[h].astype(BF16)
                Am = amat[:, h * SUB:(h + 1) * SUB].astype(BF16)
                dq_p.append(dqh + eG[:, sl] * _dnn(dob[:, sl], Sb))
                dk_p.append(dkh + edl[:, sl] * _dnn(ib[:, sl], dSb))
                di_p.append(_dtn(Am, dob[:, sl]) + _dnt(kd[:, sl], dSb))
                tail_p.append(jnp.sum(dSt[h] * S_next[h].astype(F32), axis=0, keepdims=True))
                S_next[h] = Sb
                dSt[h] = eGl[:, sl] * dSt[h] + _dtn(dob[:, sl], qt[:, sl])
            dq = jnp.concatenate(dq_p, axis=1)
            dk = jnp.concatenate(dk_p, axis=1)
            di = jnp.concatenate(di_p, axis=1)
            dg = _cumsum_rows(q * dq - k * dk, reverse=True) + jnp.concatenate(tail_p, axis=1)
            df = jnp.where(f > F_FLOOR, dg / f, 0.0)
            dfk = df - dk
            dfz = (1.0 - lb) * dfk * sig * (1.0 - sig)
            dlb_acc[...] += jnp.sum(dfk * (1.0 - sig), axis=0, keepdims=True)
            dh_ref[rows, :] = jnp.concatenate([dq * Q_SCALE, dfz, di, dgz], axis=1).astype(BF16)
            return carry

        lax.fori_loop(0, nsub, block, 0)

        @pl.when(step == n - 1)
        def _():
            if layer == 0:
                dlb_ref[...] = jnp.zeros_like(dlb_ref)
            else:
                dz1 = p0 * p1 * dlb_acc[...]
                dlb_ref[...] = jnp.concatenate([-dz1, dz1], axis=0)

    rev = lambda i: n - 1 - i
    col = lambda c: pl.BlockSpec((ts, HG_W), lambda i: (rev(i), c))
    return pl.pallas_call(
        body, name=name, grid=(n,),
        in_specs=[col(0), col(1), col(2), col(3), col(0),
                  pl.BlockSpec((nsub, HG_W, HG_D), lambda i: (rev(i), 0, 0)),
                  pl.BlockSpec((ts, HG_HEADS * SUB), lambda i: (rev(i), 0)), col(0),
                  pl.BlockSpec((2, HG_W), lambda i: (0, 0)), pl.BlockSpec((1, HG_D), lambda i: (0, 0))],
        out_specs=[pl.BlockSpec((ts, 4 * HG_W), lambda i: (rev(i), 0)),
                   pl.BlockSpec((2, HG_W), lambda i: (0, 0)), pl.BlockSpec((1, HG_D), lambda i: (0, 0))],
        out_shape=[jax.ShapeDtypeStruct((L, 4 * HG_W), BF16), jax.ShapeDtypeStruct((2, HG_W), F32),
                   jax.ShapeDtypeStruct((1, HG_D), F32)],
        scratch_shapes=[pltpu.VMEM((HG_HEADS, HG_D, HG_D), F32), pltpu.VMEM((1, HG_W), F32),
                        pltpu.VMEM((HG_HEADS, HG_D, HG_D), BF16)],
        compiler_params=_params(("arbitrary",)),
    )(hh, hh, hh, hh, o_raw, states, amat, dcat, lbp, gnorm)


def _adamw_body(gp_ref, w_ref, m_ref, v_ref, g_ref, d_ref, mo_ref, vo_ref):
    c1 = 1.0 - ADAM_B1 ** ADAM_STEP
    c2 = 1.0 - ADAM_B2 ** ADAM_STEP
    g = gp_ref[0].astype(F32)
    for k in range(1, N_DEV):
        g = g + gp_ref[k].astype(F32)
    mn = ADAM_B1 * m_ref[...] + (1.0 - ADAM_B1) * g
    vn = ADAM_B2 * v_ref[...] + (1.0 - ADAM_B2) * (g * g)
    m_hat = mn / c1
    v_hat = vn / c2
    g_ref[...] = g
    d_ref[...] = -ADAM_LR * (m_hat / (jnp.sqrt(v_hat) + ADAM_EPS) + ADAM_WD * w_ref[...])
    mo_ref[...] = mn
    vo_ref[...] = vn


def _adamw_layers(gparts, w, m, v, name):
    depth, R, C = w.shape
    tr = _tile(R, 256, 16)
    nr = R // tr

    def body(*refs):
        layer = pl.program_id(0)
        for d in range(depth):
            @pl.when(layer == d)
            def _(d=d):
                _adamw_body(refs[d], *refs[depth:])

    def parts_spec(d):
        return pl.BlockSpec((N_DEV, tr, C),
                            lambda l, i: (0, jnp.where(l == d, i, jnp.where(l < d, 0, nr - 1)), 0))

    blk = pl.BlockSpec((None, tr, C), lambda l, i: (l, i, 0))
    shp = jax.ShapeDtypeStruct((depth, R, C), F32)
    return pl.pallas_call(
        body, name=name, grid=(depth, nr),
        in_specs=[parts_spec(d) for d in range(depth)] + [blk, blk, blk],
        out_specs=[blk, blk, blk, blk], out_shape=[shp, shp, shp, shp],
        compiler_params=_params(("arbitrary", "arbitrary")),
    )(*gparts, w, m, v)


def _adamw(gparts, w, m, v, name):
    R = w.shape[0]
    tr = _tile(R, 1024, 16) if R % 16 == 0 else R

    def body(*refs):
        _adamw_body(*refs)

    row = pl.BlockSpec((tr, LANES), lambda i: (i, 0))
    shp = jax.ShapeDtypeStruct((R, LANES), F32)
    return pl.pallas_call(
        body, name=name, grid=(R // tr,),
        in_specs=[pl.BlockSpec((N_DEV, tr, LANES), lambda i: (0, i, 0)), row, row, row],
        out_specs=[row, row, row, row], out_shape=[shp, shp, shp, shp],
        compiler_params=_params(("parallel",)),
    )(gparts, w, m, v)


def _flip(coord, bit):
    return 1 - coord if bit else coord


def _gather_many(blocks, name):
    n = len(blocks)

    def body(*refs):
        x_refs, out_refs = refs[:n], refs[n:2 * n]
        send_sems, recv_sems, local_sems = refs[2 * n:]
        x, y, c = lax.axis_index("x"), lax.axis_index("y"), lax.axis_index("c")
        me, sibling = (x, y, c), (x, y, 1 - c)
        chips = [(1 - x, y), (x, 1 - y), (1 - x, 1 - y)]

        def slot(a, px, py, pc):
            return out_refs[a].at[4 * px + 2 * py + pc]

        def copy(a, k, blk, to, src=None):
            return pltpu.make_async_remote_copy(
                src_ref=slot(a, *blk) if src is None else src, dst_ref=slot(a, *blk),
                send_sem=send_sems.at[7 * a + k], recv_sem=recv_sems.at[7 * a + k],
                device_id=to, device_id_type=pl.DeviceIdType.MESH)

        mine = [pltpu.make_async_copy(x_refs[a], slot(a, *me), local_sems.at[a]) for a in range(n)]
        for cp in mine:
            cp.start()
        first = [copy(a, 0, me, sibling, src=x_refs[a]) for a in range(n)]
        for j, chip in enumerate(chips):
            first += [copy(a, 1 + j, me, (*chip, c), src=x_refs[a]) for a in range(n)]
        for cp in first:
            cp.start()
        passed = []
        for j, chip in enumerate(chips):
            for a in range(n):
                copy(a, 1 + j, (*chip, c), me).wait_recv()
                fwd = copy(a, 4 + j, (*chip, c), sibling)
                fwd.start()
                passed.append(fwd)
        for a in range(n):
            copy(a, 0, sibling, me).wait_recv()
        for j, chip in enumerate(chips):
            for a in range(n):
                copy(a, 4 + j, (*chip, 1 - c), me).wait_recv()
        for cp in first + passed:
            cp.wait_send()
        for cp in mine:
            cp.wait()

    hbm = pl.BlockSpec(memory_space=pl.ANY)
    return pl.pallas_call(
        body, name=name,
        out_shape=[jax.ShapeDtypeStruct((N_DEV,) + b.shape, b.dtype) for b in blocks],
        in_specs=[hbm] * n, out_specs=[hbm] * n,
        scratch_shapes=[pltpu.SemaphoreType.DMA((7 * n,)), pltpu.SemaphoreType.DMA((7 * n,)),
                        pltpu.SemaphoreType.DMA((n,))],
    )(*blocks)


def _split_start(blocks, chunked, name):
    n = len(blocks)
    lands = [lax.empty(b.shape if chunked else (N_DEV,) + b.shape, b.dtype) for b in blocks]

    def body(*refs):
        x_refs, land_refs = refs[:n], refs[n:2 * n]
        send_sems, recv_sems, token = refs[2 * n], refs[2 * n + 1], refs[-1]
        x, y, c = lax.axis_index("x"), lax.axis_index("y"), lax.axis_index("c")
        me = 4 * x + 2 * y + c
        for a in range(n):
            for k in range(1, N_DEV):
                px, py, pc = _flip(x, k & 4), _flip(y, k & 2), _flip(c, k & 1)
                pltpu.make_async_remote_copy(
                    src_ref=x_refs[a].at[4 * px + 2 * py + pc] if chunked else x_refs[a],
                    dst_ref=land_refs[a].at[me],
                    send_sem=send_sems.at[7 * a + k - 1], recv_sem=recv_sems.at[7 * a + k - 1],
                    device_id=(px, py, pc), device_id_type=pl.DeviceIdType.MESH).start()
        token[...] = jnp.zeros_like(token)

    hbm = pl.BlockSpec(memory_space=pltpu.HBM)
    sem = pl.BlockSpec(memory_space=pltpu.SEMAPHORE)
    outs = pl.pallas_call(
        body, name=name,
        out_shape=(pltpu.SemaphoreType.DMA((7 * n,)), pltpu.SemaphoreType.DMA((7 * n,)),
                   *[pltpu.HBM(b.shape, b.dtype) for b in blocks], *[pltpu.HBM(l.shape, l.dtype) for l in lands],
                   jax.ShapeDtypeStruct((8, LANES), F32)),
        in_specs=[hbm] * (2 * n),
        out_specs=(sem, sem, *[hbm] * (2 * n), pl.BlockSpec(memory_space=pltpu.VMEM)),
        input_output_aliases={i: 2 + i for i in range(2 * n)},
        compiler_params=pltpu.CompilerParams(has_side_effects=pltpu.SideEffectType.DATAFLOW_SIDE_EFFECTING),
    )(*[pltpu.with_memory_space_constraint(b, pltpu.HBM) for b in blocks],
      *[pltpu.with_memory_space_constraint(l, pltpu.HBM) for l in lands])
    return outs[0], outs[1], list(outs[2:2 + n]), list(outs[2 + n:2 + 2 * n]), outs[-1]


def _split_wait(started, chunked, after, name):
    send_sems, recv_sems, blocks, lands, _ = started
    n = len(blocks)

    def body(*refs):
        x_refs, land_refs = refs[:n], refs[n:2 * n]
        send_sems, recv_sems = refs[2 * n], refs[2 * n + 1]
        x, y, c = lax.axis_index("x"), lax.axis_index("y"), lax.axis_index("c")
        for a in range(n):
            for k in range(1, N_DEV):
                px, py, pc = _flip(x, k & 4), _flip(y, k & 2), _flip(c, k & 1)
                copy = pltpu.make_async_remote_copy(
                    src_ref=x_refs[a].at[4 * px + 2 * py + pc] if chunked else x_refs[a],
                    dst_ref=land_refs[a].at[4 * px + 2 * py + pc],
                    send_sem=send_sems.at[7 * a + k - 1], recv_sem=recv_sems.at[7 * a + k - 1],
                    device_id=(px, py, pc), device_id_type=pl.DeviceIdType.MESH)
                copy.wait_send()
                copy.wait_recv()

    hbm = pl.BlockSpec(memory_space=pltpu.HBM)
    sem = pl.BlockSpec(memory_space=pltpu.SEMAPHORE)
    outs = pl.pallas_call(
        body, name=name,
        out_shape=(*[pltpu.HBM(b.shape, b.dtype) for b in blocks], *[pltpu.HBM(l.shape, l.dtype) for l in lands]),
        in_specs=[hbm] * (2 * n) + [sem, sem, pl.BlockSpec(memory_space=pl.ANY)],
        out_specs=[hbm] * (2 * n),
        input_output_aliases={i: i for i in range(2 * n)},
        compiler_params=pltpu.CompilerParams(has_side_effects=pltpu.SideEffectType.DATAFLOW_SIDE_EFFECTING),
    )(*blocks, *lands, send_sems, recv_sems, after)
    me = 4 * lax.axis_index("x") + 2 * lax.axis_index("y") + lax.axis_index("c")
    own = [lax.dynamic_index_in_dim(b, me, 0, keepdims=False) if chunked else b for b in outs[:n]]
    return [lax.dynamic_update_index_in_dim(z, o, me, 0) for z, o in zip(outs[n:], own)]


def _exchange_grads(layer_chunks, small_chunks, rep_block, name):
    flows, inputs = [], []
    for p, per_layer in enumerate(layer_chunks):
        for l, arr in enumerate(per_layer):
            flows.append(("param", p, l))
            inputs.append(arr)
    flows += [("small",), ("rep",)]
    inputs += [small_chunks, rep_block]
    n_par = len(layer_chunks)
    n_in, n_out, nf = len(inputs), n_par + 2, len(flows)

    def body(*refs):
        in_refs, out_refs = refs[:n_in], refs[n_in:n_in + n_out]
        send_sems, recv_sems, local_sems = refs[n_in + n_out:]
        x, y, c = lax.axis_index("x"), lax.axis_index("y"), lax.axis_index("c")
        me = 4 * x + 2 * y + c

        def src(f, dev):
            return in_refs[f] if flows[f][0] == "rep" else in_refs[f].at[dev]

        def dst(f, dev):
            if flows[f][0] == "param":
                _, p, l = flows[f]
                return out_refs[p].at[dev, l]
            return out_refs[n_par + (0 if flows[f][0] == "small" else 1)].at[dev]

        mine = [pltpu.make_async_copy(src(f, me), dst(f, me), local_sems.at[f]) for f in range(nf)]
        for cp in mine:
            cp.start()
        copies = []
        for k in range(1, N_DEV):
            px, py, pc = _flip(x, k & 4), _flip(y, k & 2), _flip(c, k & 1)
            peer = 4 * px + 2 * py + pc
            for f in range(nf):
                sems = dict(send_sem=send_sems.at[7 * f + k - 1], recv_sem=recv_sems.at[7 * f + k - 1],
                            device_id=(px, py, pc), device_id_type=pl.DeviceIdType.MESH)
                send = pltpu.make_async_remote_copy(src_ref=src(f, peer), dst_ref=dst(f, me), **sems)
                recv = pltpu.make_async_remote_copy(src_ref=src(f, peer), dst_ref=dst(f, peer), **sems)
                send.start()
                copies.append((send, recv))
        for send, recv in copies:
            recv.wait_recv()
        for send, recv in copies:
            send.wait_send()
        for cp in mine:
            cp.wait()

    out_shape = [jax.ShapeDtypeStruct((N_DEV, len(pl_)) + pl_[0].shape[1:], pl_[0].dtype) for pl_ in layer_chunks]
    out_shape += [jax.ShapeDtypeStruct(small_chunks.shape, small_chunks.dtype),
                  jax.ShapeDtypeStruct((N_DEV,) + rep_block.shape, rep_block.dtype)]
    hbm = pl.BlockSpec(memory_space=pl.ANY)
    return pl.pallas_call(
        body, name=name, out_shape=out_shape,
        in_specs=[hbm] * n_in, out_specs=[hbm] * n_out,
        scratch_shapes=[pltpu.SemaphoreType.DMA((7 * nf,)), pltpu.SemaphoreType.DMA((7 * nf,)),
                        pltpu.SemaphoreType.DMA((nf,))],
    )(*inputs)


def _pack_rows(size):
    return -(-size // (8 * LANES)) * 8


def _pack(arrs, dtype):
    parts, offs, r = [], [], 0
    for a in arrs:
        flat = a.astype(dtype).reshape(-1)
        nrow = _pack_rows(flat.shape[0])
        flat = jnp.pad(flat, (0, nrow * LANES - flat.shape[0]))
        parts.append(flat.reshape(nrow, LANES))
        offs.append((r, nrow))
        r += nrow
    return jnp.concatenate(parts, axis=0), offs


def _unpack(buf, offs, shapes, lead=()):
    outs = []
    for (r, nrow), shp in zip(offs, shapes):
        size = 1
        for s in shp:
            size *= s
        flat = buf[..., r:r + nrow, :].reshape(lead + (nrow * LANES,))
        outs.append(flat[..., :size].reshape(lead + tuple(shp)))
    return outs


def _cols_from_shards(g, axis):
    return jnp.concatenate([g[j] for j in range(N_DEV)], axis=axis)


BIG = ("w_in", "w_o", "w_up", "w_down")
SMALL_SHARDED = ("meta_tokens", "w_conv", "w_ffn_conv")
REPLICATED = ("hg_lower_bounds", "w_pool", "pool_scale", "hg_norm_g", "ln1_g", "ln1_b", "b_ffn_conv", "ln2_g", "ln2_b")
WEIGHTS = ("meta_tokens", "hg_lower_bounds", "w_in", "w_conv", "w_pool", "pool_scale", "hg_norm_g", "w_o",
           "ln1_g", "ln1_b", "w_up", "w_ffn_conv", "b_ffn_conv", "w_down", "ln2_g", "ln2_b")


def _pool_blockdiag(w_pool_l):
    z = jnp.zeros((POOL_GROUP, POOL_GROUP), w_pool_l.dtype)
    rows = [jnp.concatenate([w_pool_l[g] if h == g else z for h in range(4)], axis=1) for g in range(4)]
    return jnp.concatenate(rows, axis=0)


def _mixer_weights(g_in, g_o):
    w_in = jnp.transpose(g_in, (1, 0, 2)).reshape(D_MODEL, -1)
    w_o = g_o.reshape(-1, D_MODEL)
    return dict(
        w_hg=w_in[:, 768:2816],
        w_cp=jnp.concatenate([w_in[:, 0:768], w_in[:, 2816:3072]], axis=1),
        w_o=jnp.concatenate([w_o[256:768], w_o[0:256], w_o[768:1024]], axis=0))


def _ffn_weights(g_up, g_down):
    return dict(w_up=jnp.transpose(g_up, (1, 0, 2)).reshape(D_MODEL, -1), w_down=g_down.reshape(-1, D_MODEL))


def kernel(x, meta_tokens, hg_lower_bounds, w_in, w_conv, w_pool, pool_scale, hg_norm_g, w_o, ln1_g, ln1_b, w_up, w_ffn_conv, b_ffn_conv, w_down, ln2_g, ln2_b, loss_target, m_meta_tokens, m_hg_lower_bounds, m_w_in, m_w_conv, m_w_pool, m_pool_scale, m_hg_norm_g, m_w_o, m_ln1_g, m_ln1_b, m_w_up, m_w_ffn_conv, m_b_ffn_conv, m_w_down, m_ln2_g, m_ln2_b, v_meta_tokens, v_hg_lower_bounds, v_w_in, v_w_conv, v_w_pool, v_pool_scale, v_hg_norm_g, v_w_o, v_ln1_g, v_ln1_b, v_w_up, v_w_ffn_conv, v_b_ffn_conv, v_w_down, v_ln2_g, v_ln2_b):
    W = dict(meta_tokens=meta_tokens, hg_lower_bounds=hg_lower_bounds, w_in=w_in, w_conv=w_conv, w_pool=w_pool,
             pool_scale=pool_scale, hg_norm_g=hg_norm_g, w_o=w_o, ln1_g=ln1_g, ln1_b=ln1_b, w_up=w_up,
             w_ffn_conv=w_ffn_conv, b_ffn_conv=b_ffn_conv, w_down=w_down, ln2_g=ln2_g, ln2_b=ln2_b)
    M = dict(meta_tokens=m_meta_tokens, hg_lower_bounds=m_hg_lower_bounds, w_in=m_w_in, w_conv=m_w_conv,
             w_pool=m_w_pool, pool_scale=m_pool_scale, hg_norm_g=m_hg_norm_g, w_o=m_w_o, ln1_g=m_ln1_g,
             ln1_b=m_ln1_b, w_up=m_w_up, w_ffn_conv=m_w_ffn_conv, b_ffn_conv=m_b_ffn_conv, w_down=m_w_down,
             ln2_g=m_ln2_g, ln2_b=m_ln2_b)
    V = dict(meta_tokens=v_meta_tokens, hg_lower_bounds=v_hg_lower_bounds, w_in=v_w_in, w_conv=v_w_conv,
             w_pool=v_w_pool, pool_scale=v_pool_scale, hg_norm_g=v_hg_norm_g, w_o=v_w_o, ln1_g=v_ln1_g,
             ln1_b=v_ln1_b, w_up=v_w_up, w_ffn_conv=v_w_ffn_conv, b_ffn_conv=v_b_ffn_conv, w_down=v_w_down,
             ln2_g=v_ln2_g, ln2_b=v_ln2_b)
    assert x.shape[0] == 1 and x.shape[2] == D_MODEL and w_in.shape[0] == DEPTH
    seq = x.shape[1]
    L = -(-(seq + N_META) // ROW_ALIGN) * ROW_ALIGN

    small_pack, small_offs = _pack([W[n] for n in SMALL_SHARDED], F32)
    shards = {n: [W[n][l].astype(BF16) for l in range(DEPTH)] for n in BIG}
    g_in0, g_o0, small_all = _gather_many([shards["w_in"][0], shards["w_o"][0], small_pack], "gather_weights")
    full = {}
    for n, a in zip(SMALL_SHARDED, _unpack(small_all, small_offs, [W[n].shape for n in SMALL_SHARDED], (N_DEV,))):
        full[n] = _cols_from_shards(a, 1)
    order = (small_all[0, 0, 0] * 0.0).astype(BF16)
    ffn0_started = _split_start([shards["w_up"][0] + order, shards["w_down"][0] + order], False, "gather_ffn0_start")
    order = ffn0_started[4][0, 0].astype(BF16)
    layer1_started = _split_start([shards[n][1] + order for n in BIG], False, "gather_layer1_start")
    lb_in = hg_lower_bounds + layer1_started[4][0, 0]

    pad_rows = L - N_META - seq
    xp = jnp.concatenate([full["meta_tokens"], x[0], jnp.zeros((pad_rows, D_MODEL), F32)], axis=0)
    tgt = jnp.concatenate([jnp.zeros((N_META, D_MODEL), F32), loss_target[0], jnp.zeros((pad_rows, D_MODEL), F32)], axis=0)

    saved = []
    h_in, h_in_b = xp, xp.astype(BF16)
    for l in range(DEPTH):
        if l == 0:
            lw = _mixer_weights(g_in0, g_o0)
        else:
            g_in, g_o, g_up, g_down = _split_wait(layer1_started, False, h_in_b, "gather_layer1_wait")
            lw = {**_mixer_weights(g_in, g_o), **_ffn_weights(g_up, g_down)}
        wc = full["w_conv"][l].T
        wblk = _pool_blockdiag(w_pool[l]).astype(BF16)
        ps = pool_scale[l][None, :]
        gn = hg_norm_g[l][None, :]
        wf = full["w_ffn_conv"][l].T
        bf = b_ffn_conv[l][None, :]
        hh = _matmul(h_in_b, lw["w_hg"], "nn", F32, f"fwd_hg_{l}")
        hc = _matmul(h_in_b, lw["w_cp"], "nn", F32, f"fwd_cp_{l}")
        y_hg, o_raw, states, amat = _hgrn_fwd(hh, lb_in if l == 0 else hg_lower_bounds, gn, l, f"hgrn_fwd_{l}")
        y_cp = _cp_fwd(hc, wc, wblk, ps, f"convpool_fwd_{l}")
        cat = jnp.concatenate([y_hg, y_cp], axis=1)
        z1, x1, x1_b = _matmul_ln(cat, lw["w_o"], h_in, ln1_g[l][None, :], ln1_b[l][None, :], f"fwd_o_ln1_{l}")
        if l == 0:
            lw.update(_ffn_weights(*_split_wait(ffn0_started, False, x1_b, "gather_ffn0_wait")))
        up = _matmul(x1_b, lw["w_up"], "nn", BF16, f"fwd_up_{l}")
        a, u = _ffn_act_fwd(up, wf, bf, f"ffn_fwd_{l}")
        saved.append(dict(lw=lw, wc=wc, wblk=wblk, ps=ps, gn=gn, wf=wf, bf=bf, x_b=h_in_b, hh=hh, hc=hc,
                          o_raw=o_raw, states=states, amat=amat, cat=cat, z1=z1, x1_b=x1_b, up=up, u=u, a=a))
        if l < DEPTH - 1:
            saved[l]["z2"], h_in, h_in_b = _matmul_ln(a, lw["w_down"], x1, ln2_g[l][None, :], ln2_b[l][None, :],
                                                     f"fwd_down_ln2_{l}")
        else:
            saved[l]["z2"], dy, loss_part = _matmul_ln(a, lw["w_down"], x1, ln2_g[l][None, :], ln2_b[l][None, :],
                                                       f"fwd_down_ln2_loss_{l}", loss=(tgt, seq))

    loss = lax.psum(loss_part[0, 0], ("x", "y", "c"))

    G = {}
    per_layer = {n: [None] * DEPTH for n in ("w_conv", "w_pool", "pool_scale", "hg_norm_g", "ln1_g", "ln1_b",
                                             "w_ffn_conv", "b_ffn_conv", "ln2_g", "ln2_b")}
    ffn_started, mix_started = [None] * DEPTH, [None] * DEPTH
    order = jnp.zeros((), F32)
    dlb_total = jnp.zeros((DEPTH, HG_W), F32)
    for l in reversed(range(DEPTH)):
        s = saved[l]
        lw = s["lw"]
        dz2, dz2_b, dg2, db2 = _ln_bwd(s["z2"], dy, ln2_g[l][None, :] + order, f"ln2_bwd_{l}")
        da = _matmul(dz2_b, lw["w_down"], "nt", BF16, f"bwd_da_{l}")
        d_w_down = _matmul(s["a"], dz2_b, "tn", BF16, f"wgrad_down_{l}")
        dup, dwf, dbf = _ffn_act_bwd(s["up"], s["u"], da, s["wf"], f"ffn_bwd_{l}")
        dx1 = _matmul(dup, lw["w_up"], "nt", F32, f"bwd_dx1_{l}", res=dz2, alpha=ALPHA)
        d_w_up = _matmul(s["x1_b"], dup, "tn", BF16, f"wgrad_up_{l}")
        ffn_started[l] = _split_start([jnp.transpose(d_w_up.reshape(D_MODEL, N_DEV, -1), (1, 0, 2)),
                                       d_w_down.reshape(N_DEV, -1, D_MODEL)], True, f"scatter_ffn{l}_start")
        order = ffn_started[l][4][0, 0]
        dz1, dz1_b, dg1, db1 = _ln_bwd(s["z1"], dx1, ln1_g[l][None, :] + order, f"ln1_bwd_{l}")
        dcat = _matmul(dz1_b, lw["w_o"], "nt", F32, f"bwd_dcat_{l}")
        d_w_o = _matmul(s["cat"], dz1_b, "tn", BF16, f"wgrad_o_{l}")
        dhh, dlb, dgn = _hgrn_bwd(s["hh"], s["o_raw"], s["states"], s["amat"], dcat, hg_lower_bounds, s["gn"], l,
                                  f"hgrn_bwd_{l}")
        dhc, dwc, dwblk, dps = _cp_bwd(s["hc"], dcat, s["wc"], s["wblk"], s["ps"], f"convpool_bwd_{l}")
        dx_a = _matmul(dhh, lw["w_hg"], "nt", F32, f"bwd_dx_hg_{l}", res=dz1, alpha=ALPHA)
        dx = _matmul(dhc, lw["w_cp"], "nt", F32, f"bwd_dx_cp_{l}", res=dx_a, alpha=1.0)
        d_w_hg = _matmul(s["x_b"], dhh, "tn", BF16, f"wgrad_hg_{l}")
        d_w_cp = _matmul(s["x_b"], dhc, "tn", BF16, f"wgrad_cp_{l}")
        d_w_in = jnp.concatenate([d_w_cp[:, 0:768], d_w_hg, d_w_cp[:, 768:1024]], axis=1)
        mix_chunks = [jnp.transpose(d_w_in.reshape(D_MODEL, N_DEV, -1), (1, 0, 2)),
                      jnp.concatenate([d_w_o[512:768], d_w_o[0:512], d_w_o[768:1024]], axis=0).reshape(N_DEV, -1, D_MODEL)]
        if l > 0:
            mix_started[l] = _split_start(mix_chunks, True, f"scatter_mix{l}_start")
            order = mix_started[l][4][0, 0]
        per_layer["w_conv"][l] = dwc.T
        per_layer["w_ffn_conv"][l] = dwf.T
        per_layer["b_ffn_conv"][l] = dbf[0]
        per_layer["w_pool"][l] = jnp.stack([dwblk[g * 64:(g + 1) * 64, g * 64:(g + 1) * 64] for g in range(4)], axis=0)
        per_layer["pool_scale"][l] = dps[0]
        per_layer["hg_norm_g"][l] = dgn[0]
        per_layer["ln1_g"][l], per_layer["ln1_b"][l] = dg1[0], db1[0]
        per_layer["ln2_g"][l], per_layer["ln2_b"][l] = dg2[0], db2[0]
        dlb_total = dlb_total + dlb
        dy = dx
    for n, parts in per_layer.items():
        G[n] = jnp.stack(parts, axis=0)
    G["hg_lower_bounds"] = dlb_total
    grad_x = dy[N_META:N_META + seq][None]

    def shard_major(g, lead):
        g = g.reshape(g.shape[:lead] + (N_DEV, -1) + g.shape[lead + 1:])
        g = jnp.moveaxis(g, lead, 0).reshape(N_DEV, -1)
        nrow = _pack_rows(g.shape[1])
        return jnp.pad(g, ((0, 0), (0, nrow * LANES - g.shape[1]))).reshape(N_DEV, nrow, LANES)

    small_chunks = jnp.concatenate([shard_major(dy[0:N_META], 1), shard_major(G["w_conv"], 1),
                                    shard_major(G["w_ffn_conv"], 1)], axis=1)
    w_small, _ = _pack([W[n] for n in SMALL_SHARDED], F32)
    rep_pack, rep_offs = _pack([G[n] for n in REPLICATED], F32)
    in0, o0, small_recv, rep_all = _exchange_grads([[c] for c in mix_chunks], small_chunks, rep_pack, "exchange_grads")
    parts = {"w_in": [in0[:, 0]], "w_o": [o0[:, 0]], "w_up": [], "w_down": []}
    for l in range(DEPTH):
        up_l, down_l = _split_wait(ffn_started[l], True, rep_all, f"scatter_ffn{l}_wait")
        parts["w_up"].append(up_l)
        parts["w_down"].append(down_l)
        if l > 0:
            in_l, o_l = _split_wait(mix_started[l], True, rep_all, f"scatter_mix{l}_wait")
            parts["w_in"].append(in_l)
            parts["w_o"].append(o_l)

    res = {k: {} for k in ("grad", "delta", "new_m", "new_v")}
    kinds = ("grad", "delta", "new_m", "new_v")
    for n in BIG:
        for kind, a in zip(kinds, _adamw_layers(parts[n], W[n], M[n], V[n], f"adamw_{n}")):
            res[kind][n] = a
    m_small, _ = _pack([M[n] for n in SMALL_SHARDED], F32)
    v_small, _ = _pack([V[n] for n in SMALL_SHARDED], F32)
    outs_small = _adamw(small_recv, w_small, m_small, v_small, "adamw_small_sharded")
    w_rep, _ = _pack([W[n] for n in REPLICATED], F32)
    m_rep, _ = _pack([M[n] for n in REPLICATED], F32)
    v_rep, _ = _pack([V[n] for n in REPLICATED], F32)
    outs_rep = _adamw(rep_all, w_rep, m_rep, v_rep, "adamw_replicated")
    for kind, b_sm, b_rep in zip(kinds, outs_small, outs_rep):
        for n, a in zip(SMALL_SHARDED, _unpack(b_sm, small_offs, [W[n].shape for n in SMALL_SHARDED])):
            res[kind][n] = a
        for n, a in zip(REPLICATED, _unpack(b_rep, rep_offs, [W[n].shape for n in REPLICATED])):
            res[kind][n] = a

    return (loss, grad_x, *[res["grad"][n] for n in WEIGHTS], *[res["delta"][n] for n in WEIGHTS],
            *[res["new_m"][n] for n in WEIGHTS], *[res["new_v"][n] for n in WEIGHTS])
```

```python
import jax
import jax.numpy as jnp
from jax import lax
from jax.experimental import pallas as pl
from jax.experimental.pallas import tpu as pltpu

F32 = jnp.float32
BF16 = jnp.bfloat16

N_DEV = 8
D_MODEL = 1024
N_META = 16
DEPTH = 2
CONV_W = 256
HG_W = 512
HG_D = 128
HG_HEADS = 4
POOL_W = 256
POOL_GROUP = 64
D_FF = 2816
ALPHA = (2 * DEPTH) ** 0.25
LN_EPS = 1e-5
RMS_EPS = 1e-6
F_FLOOR = 1e-30
Q_SCALE = HG_D ** -0.5
SUB = 16
SEQ_TILE = 192
FFN_TILE = 96
ROW_ALIGN = 192
LANES = 128
VMEM_LIMIT = 48 * 1024 * 1024
MATMUL_VMEM_BUDGET = 38 * 1024 * 1024

ADAM_LR = 0.001
ADAM_B1 = 0.9
ADAM_B2 = 0.999
ADAM_EPS = 1e-08
ADAM_WD = 0.01
ADAM_STEP = 10


def _tile(n, cap, mult):
    best = 0
    for t in range(mult, min(n, cap) + 1, mult):
        if n % t == 0:
            best = t
    assert best > 0, (n, cap, mult)
    return best


def _params(sem, vmem=VMEM_LIMIT):
    return pltpu.CompilerParams(dimension_semantics=sem, vmem_limit_bytes=vmem)


def _dnt(a, b):
    return lax.dot_general(a, b, (((1,), (1,)), ((), ())), preferred_element_type=F32)


def _dtn(a, b):
    return lax.dot_general(a, b, (((0,), (0,)), ((), ())), preferred_element_type=F32)


def _dnn(a, b):
    return jnp.dot(a, b, preferred_element_type=F32)


def _sigmoid(x):
    return 1.0 / (1.0 + jnp.exp(-x))


def _matmul(a, b, mode, out_dtype, name, res=None, alpha=1.0):
    if mode == "tn":
        K, M = a.shape
    else:
        M, K = a.shape
    N = b.shape[0] if mode == "nt" else b.shape[1]
    out_bytes = jnp.dtype(out_dtype).itemsize
    tn = _tile(N, 1536, LANES)
    tk = _tile(K, 1536, 16) if mode == "tn" else _tile(K, 2816, LANES)
    nk = K // tk
    use_acc = nk > 1 and out_dtype != F32
    tm = M
    for cap in (1536, 768, 384):
        tm = _tile(M, cap, 16)
        blocks = 2 * (a.dtype.itemsize * tm * tk + b.dtype.itemsize * tn * tk + out_bytes * tm * tn
                      + (4 * tm * tn if res is not None else 0)) + (4 * tm * tn if use_acc else 0)
        if blocks <= MATMUL_VMEM_BUDGET:
            break
    dims = {"nn": ((1,), (0,)), "nt": ((1,), (1,)), "tn": ((0,), (0,))}[mode]

    def body(*refs):
        a_ref, b_ref = refs[0], refs[1]
        r_ref = refs[2] if res is not None else None
        o_ref = refs[3] if res is not None else refs[2]
        acc = refs[-1] if use_acc else o_ref
        k = pl.program_id(2)
        p = lax.dot_general(a_ref[...].astype(BF16), b_ref[...].astype(BF16), (dims, ((), ())),
                            preferred_element_type=F32)

        def finish(r):
            if r_ref is not None:
                r = r + alpha * r_ref[...]
            o_ref[...] = r.astype(out_dtype)

        if nk == 1:
            finish(p)
        else:
            @pl.when(k == 0)
            def _():
                acc[...] = p

            @pl.when((k > 0) & (k < nk - 1))
            def _():
                acc[...] += p

            @pl.when(k == nk - 1)
            def _():
                finish(acc[...] + p)

    if mode == "tn":
        a_spec = pl.BlockSpec((tk, tm), lambda i, j, k: (k, i))
    else:
        a_spec = pl.BlockSpec((tm, tk), lambda i, j, k: (i, k))
    if mode == "nt":
        b_spec = pl.BlockSpec((tn, tk), lambda i, j, k: (j, k))
    else:
        b_spec = pl.BlockSpec((tk, tn), lambda i, j, k: (k, j))
    in_specs = [a_spec, b_spec]
    args = [a, b]
    if res is not None:
        in_specs.append(pl.BlockSpec((tm, tn), lambda i, j, k: (i, j)))
        args.append(res)
    return pl.pallas_call(
        body, name=name,
        grid=(M // tm, N // tn, nk),
        in_specs=in_specs,
        out_specs=pl.BlockSpec((tm, tn), lambda i, j, k: (i, j)),
        out_shape=jax.ShapeDtypeStruct((M, N), out_dtype),
        scratch_shapes=[pltpu.VMEM((tm, tn), F32)] if use_acc else [],
        compiler_params=_params(("parallel", "parallel", "arbitrary")),
    )(*args)


def _matmul_ln(a, w, x, g, b, name, loss=None):
    L, K = a.shape
    D = w.shape[1]
    tr = L
    for cap in (1536, 768, 384):
        tr = _tile(L, cap, 16)
        if 2 * (2 * tr * K + 2 * K * D + 4 * tr * D * (4 if loss else 3) + 2 * tr * D) <= MATMUL_VMEM_BUDGET:
            break

    def body(*refs):
        a_ref, w_ref, x_ref, g_ref, b_ref = refs[:5]
        z = ALPHA * x_ref[...] + _dnn(a_ref[...], w_ref[...])
        mu = jnp.mean(z, axis=-1, keepdims=True)
        zc = z - mu
        var = jnp.mean(zc * zc, axis=-1, keepdims=True)
        y = zc * lax.rsqrt(var + LN_EPS) * g_ref[...] + b_ref[...]
        if loss is None:
            z_ref, y_ref, yb_ref = refs[5:]
            y_ref[...] = y
            yb_ref[...] = y.astype(BF16)
        else:
            t_ref, z_ref, dy_ref, loss_ref = refs[5:]
            i = pl.program_id(0)

            @pl.when(i == 0)
            def _():
                loss_ref[...] = jnp.zeros_like(loss_ref)

            r = i * tr + lax.broadcasted_iota(jnp.int32, (tr, D), 0)
            valid = (r >= N_META) & (r < N_META + loss[1])
            e = jnp.where(valid, y - t_ref[...], 0.0)
            dy_ref[...] = e * (1.0 / D)
            s = jnp.sum(jnp.sum(e * e, axis=-1, keepdims=True), axis=0, keepdims=True)
            loss_ref[...] += (0.5 / D) * s
        z_ref[...] = z

    row = pl.BlockSpec((tr, D), lambda i: (i, 0))
    vec = pl.BlockSpec((1, D), lambda i: (0, 0))
    in_specs = [pl.BlockSpec((tr, K), lambda i: (i, 0)), pl.BlockSpec((K, D), lambda i: (0, 0)), row, vec, vec]
    f32_rows = jax.ShapeDtypeStruct((L, D), F32)
    if loss is None:
        args, out_specs = [a, w, x, g, b], [row, row, row]
        out_shape = [f32_rows, f32_rows, jax.ShapeDtypeStruct((L, D), BF16)]
    else:
        args, in_specs = [a, w, x, g, b, loss[0]], in_specs + [row]
        out_specs = [row, row, pl.BlockSpec((1, 1), lambda i: (0, 0))]
        out_shape = [f32_rows, f32_rows, jax.ShapeDtypeStruct((1, 1), F32)]
    return pl.pallas_call(
        body, name=name, grid=(L // tr,), in_specs=in_specs, out_specs=out_specs, out_shape=out_shape,
        compiler_params=_params(("arbitrary",) if loss else ("parallel",)),
    )(*args)


def _ln_bwd(z, dy, g, name):
    L, D = z.shape
    tr = _tile(L, 768, 16)

    def body(z_ref, dy_ref, g_ref, dz_ref, dzb_ref, dg_ref, db_ref):
        @pl.when(pl.program_id(0) == 0)
        def _():
            dg_ref[...] = jnp.zeros_like(dg_ref)
            db_ref[...] = jnp.zeros_like(db_ref)

        z = z_ref[...]
        mu = jnp.mean(z, axis=-1, keepdims=True)
        zc = z - mu
        var = jnp.mean(zc * zc, axis=-1, keepdims=True)
        rstd = lax.rsqrt(var + LN_EPS)
        xhat = zc * rstd
        dy = dy_ref[...]
        dxh = dy * g_ref[...]
        m1 = jnp.mean(dxh, axis=-1, keepdims=True)
        m2 = jnp.mean(dxh * xhat, axis=-1, keepdims=True)
        dz = rstd * (dxh - m1 - xhat * m2)
        dz_ref[...] = dz
        dzb_ref[...] = dz.astype(BF16)
        dg_ref[...] += jnp.sum(dy * xhat, axis=0, keepdims=True)
        db_ref[...] += jnp.sum(dy, axis=0, keepdims=True)

    row = pl.BlockSpec((tr, D), lambda i: (i, 0))
    vec = pl.BlockSpec((1, D), lambda i: (0, 0))
    return pl.pallas_call(
        body, name=name, grid=(L // tr,),
        in_specs=[row, row, vec], out_specs=[row, row, vec, vec],
        out_shape=[jax.ShapeDtypeStruct((L, D), F32), jax.ShapeDtypeStruct((L, D), BF16),
                   jax.ShapeDtypeStruct((1, D), F32), jax.ShapeDtypeStruct((1, D), F32)],
        compiler_params=_params(("arbitrary",)),
    )(z, dy, g)


def _shift_down(x, prev, k):
    T, C = x.shape
    rot = pltpu.roll(jnp.concatenate([prev, x], axis=0).reshape(T // 8 + 1, 8, C), k, 1)
    sub = lax.broadcasted_iota(jnp.int32, (T // 8, 8, C), 1)
    return jnp.where(sub < k, rot[:-1], rot[1:]).reshape(T, C)


def _shift_up(x, nxt, k):
    T, C = x.shape
    rot = pltpu.roll(jnp.concatenate([x, nxt], axis=0).reshape(T // 8 + 1, 8, C), 8 - k, 1)
    sub = lax.broadcasted_iota(jnp.int32, (T // 8, 8, C), 1)
    return jnp.where(sub >= 8 - k, rot[1:], rot[:-1]).reshape(T, C)


def _conv3(x, prev, w, b):
    return w[2:3, :] * x + w[1:2, :] * _shift_down(x, prev, 1) + w[0:1, :] * _shift_down(x, prev, 2) + b


def _ffn_act_fwd(up, w, b, name):
    L, C = up.shape
    F = C // 2
    ts = FFN_TILE
    n = L // ts

    def body(up_ref, pv_ref, w_ref, b_ref, a_ref, u_ref):
        i = pl.program_id(0)
        x = up_ref[...].astype(F32)
        prev = jnp.where(i > 0, pv_ref[...].astype(F32)[8:16], 0.0)
        u = _conv3(x, prev, w_ref[...], b_ref[...])
        u_ref[...] = u.astype(BF16)
        gate = u[:, :F]
        a_ref[...] = (gate * _sigmoid(gate) * u[:, F:]).astype(BF16)

    return pl.pallas_call(
        body, name=name, grid=(n,),
        in_specs=[pl.BlockSpec((ts, C), lambda i: (i, 0)),
                  pl.BlockSpec((16, C), lambda i: (jnp.maximum(i * (ts // 16) - 1, 0), 0)),
                  pl.BlockSpec((3, C), lambda i: (0, 0)), pl.BlockSpec((1, C), lambda i: (0, 0))],
        out_specs=[pl.BlockSpec((ts, F), lambda i: (i, 0)), pl.BlockSpec((ts, C), lambda i: (i, 0))],
        out_shape=[jax.ShapeDtypeStruct((L, F), BF16), jax.ShapeDtypeStruct((L, C), BF16)],
        compiler_params=_params(("parallel",)),
    )(up, up, w, b)


def _ffn_act_bwd(up, u, da, w, name):
    L, C = up.shape
    F = C // 2
    ts = FFN_TILE
    n = L // ts
    last16 = L // 16 - 1

    def du_of(u, da):
        gate, val = u[:, :F], u[:, F:]
        sg = _sigmoid(gate)
        dgate = da * val * (sg * (1.0 + gate * (1.0 - sg)))
        dval = da * (gate * sg)
        return jnp.concatenate([dgate, dval], axis=1)

    def body(up_ref, u_ref, un_ref, da_ref, dan_ref, w_ref, dup_ref, dw_ref, db_ref):
        i = pl.program_id(0)

        @pl.when(i == 0)
        def _():
            dw_ref[...] = jnp.zeros_like(dw_ref)
            db_ref[...] = jnp.zeros_like(db_ref)

        w = w_ref[...]
        x = up_ref[...].astype(F32)
        du = du_of(u_ref[...].astype(F32), da_ref[...].astype(F32))
        dun = jnp.where(i < n - 1, du_of(un_ref[...].astype(F32)[0:8], dan_ref[...].astype(F32)[0:8]), 0.0)
        du1 = _shift_up(du, dun, 1)
        du2 = _shift_up(du, dun, 2)
        dup_ref[...] = (w[2:3, :] * du + w[1:2, :] * du1 + w[0:1, :] * du2).astype(BF16)
        dw_ref[...] += jnp.concatenate([jnp.sum(x * du2, axis=0, keepdims=True),
                                        jnp.sum(x * du1, axis=0, keepdims=True),
                                        jnp.sum(x * du, axis=0, keepdims=True)], axis=0)
        db_ref[...] += jnp.sum(du, axis=0, keepdims=True)

    nxt = lambda i: (jnp.minimum((i + 1) * (ts // 16), last16), 0)
    return pl.pallas_call(
        body, name=name, grid=(n,),
        in_specs=[pl.BlockSpec((ts, C), lambda i: (i, 0)),
                  pl.BlockSpec((ts, C), lambda i: (i, 0)), pl.BlockSpec((16, C), nxt),
                  pl.BlockSpec((ts, F), lambda i: (i, 0)), pl.BlockSpec((16, F), nxt),
                  pl.BlockSpec((3, C), lambda i: (0, 0))],
        out_specs=[pl.BlockSpec((ts, C), lambda i: (i, 0)), pl.BlockSpec((3, C), lambda i: (0, 0)),
                   pl.BlockSpec((1, C), lambda i: (0, 0))],
        out_shape=[jax.ShapeDtypeStruct((L, C), BF16), jax.ShapeDtypeStruct((3, C), F32),
                   jax.ShapeDtypeStruct((1, C), F32)],
        compiler_params=_params(("arbitrary",)),
    )(up, u, u, da, da, w)


def _pool_window(ext, tile_rows, first_row, lead):
    T = ext.shape[0]
    sh = (lambda x, k: pltpu.roll(x, T - k, 0)) if lead else (lambda x, k: pltpu.roll(x, k, 0))
    r2 = ext + sh(ext, 1)
    r4 = r2 + sh(r2, 2)
    r8 = r4 + sh(r4, 4)
    r16 = r8 + sh(r8, 8)
    lo = 0 if lead else 16
    grp = lax.broadcasted_iota(jnp.int32, (tile_rows, POOL_W), 1) // POOL_GROUP
    pick = lambda a, b, c, d: jnp.where(grp == 0, a, jnp.where(grp == 1, b, jnp.where(grp == 2, c, d)))
    win = pick(r2[lo:lo + tile_rows], r4[lo:lo + tile_rows], r8[lo:lo + tile_rows], r16[lo:lo + tile_rows])
    return win, pick(2.0, 4.0, 8.0, 16.0)


def _pool_count(first_row, rows, wlen):
    t1 = (first_row + lax.broadcasted_iota(jnp.int32, (rows, POOL_W), 0) + 1).astype(F32)
    return jnp.minimum(t1, wlen)


def _cp_fwd(hc, wc, wblk, pscale, name):
    L = hc.shape[0]
    ts = SEQ_TILE
    n = L // ts

    def body(h_ref, hp_ref, wc_ref, wb_ref, ps_ref, y_ref):
        i = pl.program_id(0)
        h = h_ref[...]
        hp = jnp.where(i > 0, hp_ref[...], 0.0)
        cb, cc, cv, pv = h[:, 0:256], h[:, 256:512], h[:, 512:768], h[:, 768:1024]
        p = cc * cv
        pp = hp[8:16, 256:512] * hp[8:16, 512:768]
        w = wc_ref[...]
        conv = w[2:3, :] * p + w[1:2, :] * _shift_down(p, pp, 1) + w[0:1, :] * _shift_down(p, pp, 2)
        y_conv = cb * conv
        ext = jnp.concatenate([hp[:, 768:1024], pv], axis=0)
        win, wlen = _pool_window(ext, ts, i * ts, False)
        d = win / _pool_count(i * ts, ts, wlen) - pv
        y_pool = _dnn(d.astype(BF16), wb_ref[...]) * ps_ref[...]
        y_ref[...] = jnp.concatenate([y_conv, y_pool], axis=1).astype(BF16)

    return pl.pallas_call(
        body, name=name, grid=(n,),
        in_specs=[pl.BlockSpec((ts, 1024), lambda i: (i, 0)),
                  pl.BlockSpec((16, 1024), lambda i: (jnp.maximum(i * (ts // 16) - 1, 0), 0)),
                  pl.BlockSpec((3, 256), lambda i: (0, 0)), pl.BlockSpec((256, 256), lambda i: (0, 0)),
                  pl.BlockSpec((1, 256), lambda i: (0, 0))],
        out_specs=pl.BlockSpec((ts, 512), lambda i: (i, 0)),
        out_shape=jax.ShapeDtypeStruct((L, 512), BF16),
        compiler_params=_params(("parallel",)),
    )(hc, hc, wc, wblk, pscale)


def _cp_bwd(hc, dcat, wc, wblk, pscale, name):
    L = hc.shape[0]
    ts = SEQ_TILE
    n = L // ts
    last16 = L // 16 - 1

    def body(h_ref, hp_ref, hn_ref, dy_ref, dyn_ref, wc_ref, wb_ref, ps_ref,
             dh_ref, dwc_ref, dwb_ref, dps_ref):
        i = pl.program_id(0)

        @pl.when(i == 0)
        def _():
            dwc_ref[...] = jnp.zeros_like(dwc_ref)
            dwb_ref[...] = jnp.zeros_like(dwb_ref)
            dps_ref[...] = jnp.zeros_like(dps_ref)

        h = h_ref[...]
        hp = jnp.where(i > 0, hp_ref[...], 0.0)
        hn = hn_ref[...]
        dy = dy_ref[...]
        dyn = jnp.where(i < n - 1, dyn_ref[...], 0.0)
        cb, cc, cv, pv = h[:, 0:256], h[:, 256:512], h[:, 512:768], h[:, 768:1024]
        w = wc_ref[...]
        p = cc * cv
        pp = hp[8:16, 256:512] * hp[8:16, 512:768]
        p1 = _shift_down(p, pp, 1)
        p2 = _shift_down(p, pp, 2)
        conv = w[2:3, :] * p + w[1:2, :] * p1 + w[0:1, :] * p2
        dyc = dy[:, 0:256]
        dcb = dyc * conv
        dconv = dyc * cb
        dconv_n = dyn[0:8, 0:256] * hn[0:8, 0:256]
        dc1 = _shift_up(dconv, dconv_n, 1)
        dc2 = _shift_up(dconv, dconv_n, 2)
        dp = w[2:3, :] * dconv + w[1:2, :] * dc1 + w[0:1, :] * dc2
        dwc_ref[...] += jnp.concatenate([jnp.sum(p * dc2, axis=0, keepdims=True),
                                         jnp.sum(p * dc1, axis=0, keepdims=True),
                                         jnp.sum(p * dconv, axis=0, keepdims=True)], axis=0)
        ps = ps_ref[...]
        wb = wb_ref[...]
        ext = jnp.concatenate([hp[:, 768:1024], pv], axis=0)
        win, wlen = _pool_window(ext, ts, i * ts, False)
        d = win / _pool_count(i * ts, ts, wlen) - pv
        db = d.astype(BF16)
        dyp = dy[:, 256:512]
        dps_ref[...] += jnp.sum(dyp * _dnn(db, wb), axis=0, keepdims=True)
        dypre = (dyp * ps).astype(BF16)
        dwb_ref[...] += _dtn(db, dypre)
        dd = _dnt(dypre, wb)
        ddn = _dnt((dyn[:, 256:512] * ps).astype(BF16), wb)
        e = dd / _pool_count(i * ts, ts, wlen)
        en = ddn / _pool_count((i + 1) * ts, 16, wlen[0:16])
        lead, _ = _pool_window(jnp.concatenate([e, en], axis=0), ts, i * ts, True)
        dpv = lead - dd
        dh_ref[...] = jnp.concatenate([dcb, dp * cv, dp * cc, dpv], axis=1).astype(BF16)

    return pl.pallas_call(
        body, name=name, grid=(n,),
        in_specs=[pl.BlockSpec((ts, 1024), lambda i: (i, 0)),
                  pl.BlockSpec((16, 1024), lambda i: (jnp.maximum(i * (ts // 16) - 1, 0), 0)),
                  pl.BlockSpec((16, 1024), lambda i: (jnp.minimum((i + 1) * (ts // 16), last16), 0)),
                  pl.BlockSpec((ts, 512), lambda i: (i, 1)),
                  pl.BlockSpec((16, 512), lambda i: (jnp.minimum((i + 1) * (ts // 16), last16), 1)),
                  pl.BlockSpec((3, 256), lambda i: (0, 0)), pl.BlockSpec((256, 256), lambda i: (0, 0)),
                  pl.BlockSpec((1, 256), lambda i: (0, 0))],
        out_specs=[pl.BlockSpec((ts, 1024), lambda i: (i, 0)), pl.BlockSpec((3, 256), lambda i: (0, 0)),
                   pl.BlockSpec((256, 256), lambda i: (0, 0)), pl.BlockSpec((1, 256), lambda i: (0, 0))],
        out_shape=[jax.ShapeDtypeStruct((L, 1024), BF16), jax.ShapeDtypeStruct((3, 256), F32),
                   jax.ShapeDtypeStruct((256, 256), F32), jax.ShapeDtypeStruct((1, 256), F32)],
        compiler_params=_params(("arbitrary",)),
    )(hc, hc, hc, dcat, dcat, wc, wblk, pscale)


def _lower_bound(lb_ref, layer):
    b0, b1 = lb_ref[0:1, :], lb_ref[1:2, :]
    m = jnp.maximum(b0, b1)
    e0, e1 = jnp.exp(b0 - m), jnp.exp(b1 - m)
    p0, p1 = e0 / (e0 + e1), e1 / (e0 + e1)
    lb = (p0 - p0) if layer == 0 else ((p0 + p1) - p0)
    return lb, p0, p1


def _cumsum_rows(x, reverse=False):
    row = lax.broadcasted_iota(jnp.int32, x.shape, 0)
    for sh in (1, 2, 4, 8):
        if reverse:
            x = x + jnp.where(row < SUB - sh, pltpu.roll(x, SUB - sh, 0), 0.0)
        else:
            x = x + jnp.where(row >= sh, pltpu.roll(x, sh, 0), 0.0)
    return x


def _gates(fz, lb):
    sig = _sigmoid(fz)
    f = lb + (1.0 - lb) * sig
    g = jnp.log(jnp.maximum(f, F_FLOOR))
    k = (1.0 - lb) * (1.0 - sig)
    return sig, f, g, k


def _head(h):
    return slice(h * HG_D, (h + 1) * HG_D)


def _hgrn_fwd(hh, lbp, gnorm, layer, name):
    L = hh.shape[0]
    ts = SEQ_TILE
    n = L // ts
    nsub = ts // SUB

    def body(q_ref, f_ref, i_ref, g_ref, lb_ref, gn_ref, y_ref, o_ref, s_ref, a_ref, St):
        @pl.when(pl.program_id(0) == 0)
        def _():
            St[...] = jnp.zeros_like(St)

        lb, _, _ = _lower_bound(lb_ref, layer)
        gn = jnp.tile(gn_ref[...], (1, HG_HEADS))
        r16 = lax.broadcasted_iota(jnp.int32, (SUB, SUB), 0)
        c16 = lax.broadcasted_iota(jnp.int32, (SUB, SUB), 1)

        def block(j, carry):
            rows = pl.ds(pl.multiple_of(j * SUB, SUB), SUB)
            q = q_ref[rows, :] * Q_SCALE
            iv = i_ref[rows, :]
            gz = g_ref[rows, :]
            _, _, g, k = _gates(f_ref[rows, :], lb)
            G = _cumsum_rows(g)
            Gl = G[SUB - 1:SUB, :]
            qt = (q * jnp.exp(G)).astype(BF16)
            kd = (k * jnp.exp(Gl - G)).astype(BF16)
            eGl = jnp.exp(Gl)
            ib = iv.astype(BF16)
            A = [jnp.zeros((SUB, SUB), F32) for _ in range(HG_HEADS)]
            for s in range(SUB):
                P = q * jnp.exp(jnp.minimum(G - G[s:s + 1, :], 0.0)) * k[s:s + 1, :]
                for h in range(HG_HEADS):
                    A[h] = jnp.where(c16 == s, jnp.sum(P[:, _head(h)], axis=-1, keepdims=True), A[h])
            outs, ons, amats = [], [], []
            for h in range(HG_HEADS):
                sl = _head(h)
                Sb = St[h].astype(BF16)
                s_ref[j, sl, :] = Sb
                Am = jnp.where(r16 >= c16, A[h], 0.0)
                amats.append(Am)
                o = _dnt(qt[:, sl], Sb) + _dnn(Am.astype(BF16), ib[:, sl])
                St[h] = eGl[:, sl] * St[h] + _dtn(ib[:, sl], kd[:, sl])
                outs.append(o)
                ons.append(o * lax.rsqrt(jnp.mean(o * o, axis=-1, keepdims=True) + RMS_EPS))
            a_ref[rows, :] = jnp.concatenate(amats, axis=1)
            o_ref[rows, :] = jnp.concatenate(outs, axis=1)
            y = jnp.concatenate(ons, axis=1) * gn * (gz * _sigmoid(gz))
            y_ref[rows, :] = y.astype(BF16)
            return carry

        lax.fori_loop(0, nsub, block, 0, unroll=2)

    col = lambda c: pl.BlockSpec((ts, HG_W), lambda i: (i, c))
    return pl.pallas_call(
        body, name=name, grid=(n,),
        in_specs=[col(0), col(1), col(2), col(3), pl.BlockSpec((2, HG_W), lambda i: (0, 0)),
                  pl.BlockSpec((1, HG_D), lambda i: (0, 0))],
        out_specs=[pl.BlockSpec((ts, HG_W), lambda i: (i, 0)), pl.BlockSpec((ts, HG_W), lambda i: (i, 0)),
                   pl.BlockSpec((nsub, HG_W, HG_D), lambda i: (i, 0, 0)),
                   pl.BlockSpec((ts, HG_HEADS * SUB), lambda i: (i, 0))],
        out_shape=[jax.ShapeDtypeStruct((L, HG_W), BF16), jax.ShapeDtypeStruct((L, HG_W), F32),
                   jax.ShapeDtypeStruct((L // SUB, HG_W, HG_D), BF16),
                   jax.ShapeDtypeStruct((L, HG_HEADS * SUB), F32)],
        scratch_shapes=[pltpu.VMEM((HG_HEADS, HG_D, HG_D), F32)],
        compiler_params=_params(("arbitrary",)),
    )(hh, hh, hh, hh, lbp, gnorm)


def _hgrn_bwd(hh, o_raw, states, amat, dcat, lbp, gnorm, layer, name):
    L = hh.shape[0]
    ts = SEQ_TILE
    n = L // ts
    nsub = ts // SUB

    def body(q_ref, f_ref, i_ref, g_ref, o_ref, s_ref, a_ref, dy_ref, lb_ref, gn_ref,
             dh_ref, dlb_ref, dgn_ref, dSt, dlb_acc, S_next):
        step = pl.program_id(0)

        @pl.when(step == 0)
        def _():
            dSt[...] = jnp.zeros_like(dSt)
            S_next[...] = jnp.zeros_like(S_next)
            dlb_acc[...] = jnp.zeros_like(dlb_acc)
            dgn_ref[...] = jnp.zeros_like(dgn_ref)

        lb, p0, p1 = _lower_bound(lb_ref, layer)
        gnh = gn_ref[...]
        gn = jnp.tile(gnh, (1, HG_HEADS))
        r16 = lax.broadcasted_iota(jnp.int32, (SUB, SUB), 0)
        c16 = lax.broadcasted_iota(jnp.int32, (SUB, SUB), 1)

        def block(jj, carry):
            j = nsub - 1 - jj
            rows = pl.ds(pl.multiple_of(j * SUB, SUB), SUB)
            q = q_ref[rows, :] * Q_SCALE
            iv = i_ref[rows, :]
            gz = g_ref[rows, :]
            o = o_ref[rows, :]
            dy = dy_ref[rows, :]
            sig, f, g, k = _gates(f_ref[rows, :], lb)
            G = _cumsum_rows(g)
            Gl = G[SUB - 1:SUB, :]
            eG = jnp.exp(G)
            edl = jnp.exp(Gl - G)
            eGl = jnp.exp(Gl)
            qt = (q * eG).astype(BF16)
            kd = (k * edl).astype(BF16)
            ib = iv.astype(BF16)
            sgz = _sigmoid(gz)
            sil = gz * sgz
            dyn = dy * sil
            on_parts, do_parts = [], []
            dgn = jnp.zeros((1, HG_D), F32)
            for h in range(HG_HEADS):
                sl = _head(h)
                oh = o[:, sl]
                rs = lax.rsqrt(jnp.mean(oh * oh, axis=-1, keepdims=True) + RMS_EPS)
                on = oh * rs
                dgn = dgn + jnp.sum(dyn[:, sl] * on, axis=0, keepdims=True)
                don = dyn[:, sl] * gnh
                do_parts.append(rs * (don - on * jnp.mean(don * on, axis=-1, keepdims=True)))
                on_parts.append(on)
            dgn_ref[...] += dgn
            on_all = jnp.concatenate(on_parts, axis=1)
            dgz = dy * on_all * gn * (sgz * (1.0 + gz * (1.0 - sgz)))
            do = jnp.concatenate(do_parts, axis=1)
            dob = do.astype(BF16)
            amat = a_ref[rows, :]
            dq_p, dk_p, di_p, tail_p = [], [], [], []
            for h in range(HG_HEADS):
                sl = _head(h)
                qh, kh, Gh = q[:, sl], k[:, sl], G[:, sl]
                Ap = jnp.where(r16 >= c16, _dnt(dob[:, sl], ib[:, sl]), 0.0)
                ApT = jnp.where(r16 <= c16, _dnt(ib[:, sl], dob[:, sl]), 0.0)
                dqh = jnp.zeros((SUB, HG_D), F32)
                dkh = jnp.zeros((SUB, HG_D), F32)
                for s in range(SUB):
                    dGs = Gh - Gh[s:s + 1, :]
                    e = jnp.exp(jnp.minimum(dGs, -dGs))
                    dqh = dqh + Ap[:, s:s + 1] * (e * kh[s:s + 1, :])
                    dkh = dkh + ApT[:, s:s + 1] * (e * qh[s:s + 1, :])
                Sb = s_ref[j, sl, :]
                dSb = dSt[h].astype(BF16)
                Am = amat[:, h * SUB:(h + 1) * SUB].astype(BF16)
                dq_p.append(dqh + eG[:, sl] * _dnn(dob[:, sl], Sb))
                dk_p.append(dkh + edl[:, sl] * _dnn(ib[:, sl], dSb))
                di_p.append(_dtn(Am, dob[:, sl]) + _dnt(kd[:, sl], dSb))
                tail_p.append(jnp.sum(dSt[h] * S_next[h].astype(F32), axis=0, keepdims=True))
                S_next[h] = Sb
                dSt[h] = eGl[:, sl] * dSt[h] + _dtn(dob[:, sl], qt[:, sl])
            dq = jnp.concatenate(dq_p, axis=1)
            dk = jnp.concatenate(dk_p, axis=1)
            di = jnp.concatenate(di_p, axis=1)
            dg = _cumsum_rows(q * dq - k * dk, reverse=True) + jnp.concatenate(tail_p, axis=1)
            df = jnp.where(f > F_FLOOR, dg / f, 0.0)
            dfk = df - dk
            dfz = (1.0 - lb) * dfk * sig * (1.0 - sig)
            dlb_acc[...] += jnp.sum(dfk * (1.0 - sig), axis=0, keepdims=True)
            dh_ref[rows, :] = jnp.concatenate([dq * Q_SCALE, dfz, di, dgz], axis=1).astype(BF16)
            return carry

        lax.fori_loop(0, nsub, block, 0)

        @pl.when(step == n - 1)
        def _():
            if layer == 0:
                dlb_ref[...] = jnp.zeros_like(dlb_ref)
            else:
                dz1 = p0 * p1 * dlb_acc[...]
                dlb_ref[...] = jnp.concatenate([-dz1, dz1], axis=0)

    rev = lambda i: n - 1 - i
    col = lambda c: pl.BlockSpec((ts, HG_W), lambda i: (rev(i), c))
    return pl.pallas_call(
        body, name=name, grid=(n,),
        in_specs=[col(0), col(1), col(2), col(3), col(0),
                  pl.BlockSpec((nsub, HG_W, HG_D), lambda i: (rev(i), 0, 0)),
                  pl.BlockSpec((ts, HG_HEADS * SUB), lambda i: (rev(i), 0)), col(0),
                  pl.BlockSpec((2, HG_W), lambda i: (0, 0)), pl.BlockSpec((1, HG_D), lambda i: (0, 0))],
        out_specs=[pl.BlockSpec((ts, 4 * HG_W), lambda i: (rev(i), 0)),
                   pl.BlockSpec((2, HG_W), lambda i: (0, 0)), pl.BlockSpec((1, HG_D), lambda i: (0, 0))],
        out_shape=[jax.ShapeDtypeStruct((L, 4 * HG_W), BF16), jax.ShapeDtypeStruct((2, HG_W), F32),
                   jax.ShapeDtypeStruct((1, HG_D), F32)],
        scratch_shapes=[pltpu.VMEM((HG_HEADS, HG_D, HG_D), F32), pltpu.VMEM((1, HG_W), F32),
                        pltpu.VMEM((HG_HEADS, HG_D, HG_D), BF16)],
        compiler_params=_params(("arbitrary",)),
    )(hh, hh, hh, hh, o_raw, states, amat, dcat, lbp, gnorm)


def _adamw_body(gp_ref, w_ref, m_ref, v_ref, g_ref, d_ref, mo_ref, vo_ref):
    c1 = 1.0 - ADAM_B1 ** ADAM_STEP
    c2 = 1.0 - ADAM_B2 ** ADAM_STEP
    g = gp_ref[0].astype(F32)
    for k in range(1, N_DEV):
        g = g + gp_ref[k].astype(F32)
    mn = ADAM_B1 * m_ref[...] + (1.0 - ADAM_B1) * g
    vn = ADAM_B2 * v_ref[...] + (1.0 - ADAM_B2) * (g * g)
    m_hat = mn / c1
    v_hat = vn / c2
    g_ref[...] = g
    d_ref[...] = -ADAM_LR * (m_hat / (jnp.sqrt(v_hat) + ADAM_EPS) + ADAM_WD * w_ref[...])
    mo_ref[...] = mn
    vo_ref[...] = vn


def _adamw_layers(gparts, w, m, v, name):
    depth, R, C = w.shape
    tr = _tile(R, 256, 16)
    nr = R // tr

    def body(*refs):
        layer = pl.program_id(0)
        for d in range(depth):
            @pl.when(layer == d)
            def _(d=d):
                _adamw_body(refs[d], *refs[depth:])

    def parts_spec(d):
        return pl.BlockSpec((N_DEV, tr, C),
                            lambda l, i: (0, jnp.where(l == d, i, jnp.where(l < d, 0, nr - 1)), 0))

    blk = pl.BlockSpec((None, tr, C), lambda l, i: (l, i, 0))
    shp = jax.ShapeDtypeStruct((depth, R, C), F32)
    return pl.pallas_call(
        body, name=name, grid=(depth, nr),
        in_specs=[parts_spec(d) for d in range(depth)] + [blk, blk, blk],
        out_specs=[blk, blk, blk, blk], out_shape=[shp, shp, shp, shp],
        compiler_params=_params(("arbitrary", "arbitrary")),
    )(*gparts, w, m, v)


def _adamw(gparts, w, m, v, name):
    R = w.shape[0]
    tr = _tile(R, 1024, 16) if R % 16 == 0 else R

    def body(*refs):
        _adamw_body(*refs)

    row = pl.BlockSpec((tr, LANES), lambda i: (i, 0))
    shp = jax.ShapeDtypeStruct((R, LANES), F32)
    return pl.pallas_call(
        body, name=name, grid=(R // tr,),
        in_specs=[pl.BlockSpec((N_DEV, tr, LANES), lambda i: (0, i, 0)), row, row, row],
        out_specs=[row, row, row, row], out_shape=[shp, shp, shp, shp],
        compiler_params=_params(("parallel",)),
    )(gparts, w, m, v)


def _flip(coord, bit):
    return 1 - coord if bit else coord


def _gather_many(blocks, name):
    n = len(blocks)

    def body(*refs):
        x_refs, out_refs = refs[:n], refs[n:2 * n]
        send_sems, recv_sems, local_sems = refs[2 * n:]
        x, y, c = lax.axis_index("x"), lax.axis_index("y"), lax.axis_index("c")
        me, sibling = (x, y, c), (x, y, 1 - c)
        chips = [(1 - x, y), (x, 1 - y), (1 - x, 1 - y)]

        def slot(a, px, py, pc):
            return out_refs[a].at[4 * px + 2 * py + pc]

        def copy(a, k, blk, to, src=None):
            return pltpu.make_async_remote_copy(
                src_ref=slot(a, *blk) if src is None else src, dst_ref=slot(a, *blk),
                send_sem=send_sems.at[7 * a + k], recv_sem=recv_sems.at[7 * a + k],
                device_id=to, device_id_type=pl.DeviceIdType.MESH)

        mine = [pltpu.make_async_copy(x_refs[a], slot(a, *me), local_sems.at[a]) for a in range(n)]
        for cp in mine:
            cp.start()
        first = [copy(a, 0, me, sibling, src=x_refs[a]) for a in range(n)]
        for j, chip in enumerate(chips):
            first += [copy(a, 1 + j, me, (*chip, c), src=x_refs[a]) for a in range(n)]
        for cp in first:
            cp.start()
        passed = []
        for j, chip in enumerate(chips):
            for a in range(n):
                copy(a, 1 + j, (*chip, c), me).wait_recv()
                fwd = copy(a, 4 + j, (*chip, c), sibling)
                fwd.start()
                passed.append(fwd)
        for a in range(n):
            copy(a, 0, sibling, me).wait_recv()
        for j, chip in enumerate(chips):
            for a in range(n):
                copy(a, 4 + j, (*chip, 1 - c), me).wait_recv()
        for cp in first + passed:
            cp.wait_send()
        for cp in mine:
            cp.wait()

    hbm = pl.BlockSpec(memory_space=pl.ANY)
    return pl.pallas_call(
        body, name=name,
        out_shape=[jax.ShapeDtypeStruct((N_DEV,) + b.shape, b.dtype) for b in blocks],
        in_specs=[hbm] * n, out_specs=[hbm] * n,
        scratch_shapes=[pltpu.SemaphoreType.DMA((7 * n,)), pltpu.SemaphoreType.DMA((7 * n,)),
                        pltpu.SemaphoreType.DMA((n,))],
    )(*blocks)


def _split_start(blocks, chunked, name):
    n = len(blocks)
    lands = [lax.empty(b.shape if chunked else (N_DEV,) + b.shape, b.dtype) for b in blocks]

    def body(*refs):
        x_refs, land_refs = refs[:n], refs[n:2 * n]
        send_sems, recv_sems, token = refs[2 * n], refs[2 * n + 1], refs[-1]
        x, y, c = lax.axis_index("x"), lax.axis_index("y"), lax.axis_index("c")
        me = 4 * x + 2 * y + c
        for a in range(n):
            for k in range(1, N_DEV):
                px, py, pc = _flip(x, k & 4), _flip(y, k & 2), _flip(c, k & 1)
                pltpu.make_async_remote_copy(
                    src_ref=x_refs[a].at[4 * px + 2 * py + pc] if chunked else x_refs[a],
                    dst_ref=land_refs[a].at[me],
                    send_sem=send_sems.at[7 * a + k - 1], recv_sem=recv_sems.at[7 * a + k - 1],
                    device_id=(px, py, pc), device_id_type=pl.DeviceIdType.MESH).start()
        token[...] = jnp.zeros_like(token)

    hbm = pl.BlockSpec(memory_space=pltpu.HBM)
    sem = pl.BlockSpec(memory_space=pltpu.SEMAPHORE)
    outs = pl.pallas_call(
        body, name=name,
        out_shape=(pltpu.SemaphoreType.DMA((7 * n,)), pltpu.SemaphoreType.DMA((7 * n,)),
                   *[pltpu.HBM(b.shape, b.dtype) for b in blocks], *[pltpu.HBM(l.shape, l.dtype) for l in lands],
                   jax.ShapeDtypeStruct((8, LANES), F32)),
        in_specs=[hbm] * (2 * n),
        out_specs=(sem, sem, *[hbm] * (2 * n), pl.BlockSpec(memory_space=pltpu.VMEM)),
        input_output_aliases={i: 2 + i for i in range(2 * n)},
        compiler_params=pltpu.CompilerParams(has_side_effects=pltpu.SideEffectType.DATAFLOW_SIDE_EFFECTING),
    )(*[pltpu.with_memory_space_constraint(b, pltpu.HBM) for b in blocks],
      *[pltpu.with_memory_space_constraint(l, pltpu.HBM) for l in lands])
    return outs[0], outs[1], list(outs[2:2 + n]), list(outs[2 + n:2 + 2 * n]), outs[-1]


def _split_wait(started, chunked, after, name):
    send_sems, recv_sems, blocks, lands, _ = started
    n = len(blocks)

    def body(*refs):
        x_refs, land_refs = refs[:n], refs[n:2 * n]
        send_sems, recv_sems = refs[2 * n], refs[2 * n + 1]
        x, y, c = lax.axis_index("x"), lax.axis_index("y"), lax.axis_index("c")
        for a in range(n):
            for k in range(1, N_DEV):
                px, py, pc = _flip(x, k & 4), _flip(y, k & 2), _flip(c, k & 1)
                copy = pltpu.make_async_remote_copy(
                    src_ref=x_refs[a].at[4 * px + 2 * py + pc] if chunked else x_refs[a],
                    dst_ref=land_refs[a].at[4 * px + 2 * py + pc],
                    send_sem=send_sems.at[7 * a + k - 1], recv_sem=recv_sems.at[7 * a + k - 1],
                    device_id=(px, py, pc), device_id_type=pl.DeviceIdType.MESH)
                copy.wait_send()
                copy.wait_recv()

    hbm = pl.BlockSpec(memory_space=pltpu.HBM)
    sem = pl.BlockSpec(memory_space=pltpu.SEMAPHORE)
    outs = pl.pallas_call(
        body, name=name,
        out_shape=(*[pltpu.HBM(b.shape, b.dtype) for b in blocks], *[pltpu.HBM(l.shape, l.dtype) for l in lands]),
        in_specs=[hbm] * (2 * n) + [sem, sem, pl.BlockSpec(memory_space=pl.ANY)],
        out_specs=[hbm] * (2 * n),
        input_output_aliases={i: i for i in range(2 * n)},
        compiler_params=pltpu.CompilerParams(has_side_effects=pltpu.SideEffectType.DATAFLOW_SIDE_EFFECTING),
    )(*blocks, *lands, send_sems, recv_sems, after)
    me = 4 * lax.axis_index("x") + 2 * lax.axis_index("y") + lax.axis_index("c")
    own = [lax.dynamic_index_in_dim(b, me, 0, keepdims=False) if chunked else b for b in outs[:n]]
    return [lax.dynamic_update_index_in_dim(z, o, me, 0) for z, o in zip(outs[n:], own)]


def _exchange_grads(layer_chunks, small_chunks, rep_block, name):
    flows, inputs = [], []
    for p, per_layer in enumerate(layer_chunks):
        for l, arr in enumerate(per_layer):
            flows.append(("param", p, l))
            inputs.append(arr)
    flows += [("small",), ("rep",)]
    inputs += [small_chunks, rep_block]
    n_par = len(layer_chunks)
    n_in, n_out, nf = len(inputs), n_par + 2, len(flows)

    def body(*refs):
        in_refs, out_refs = refs[:n_in], refs[n_in:n_in + n_out]
        send_sems, recv_sems, local_sems = refs[n_in + n_out:]
        x, y, c = lax.axis_index("x"), lax.axis_index("y"), lax.axis_index("c")
        me = 4 * x + 2 * y + c

        def src(f, dev):
            return in_refs[f] if flows[f][0] == "rep" else in_refs[f].at[dev]

        def dst(f, dev):
            if flows[f][0] == "param":
                _, p, l = flows[f]
                return out_refs[p].at[dev, l]
            return out_refs[n_par + (0 if flows[f][0] == "small" else 1)].at[dev]

        mine = [pltpu.make_async_copy(src(f, me), dst(f, me), local_sems.at[f]) for f in range(nf)]
        for cp in mine:
            cp.start()
        copies = []
        for k in range(1, N_DEV):
            px, py, pc = _flip(x, k & 4), _flip(y, k & 2), _flip(c, k & 1)
            peer = 4 * px + 2 * py + pc
            for f in range(nf):
                sems = dict(send_sem=send_sems.at[7 * f + k - 1], recv_sem=recv_sems.at[7 * f + k - 1],
                            device_id=(px, py, pc), device_id_type=pl.DeviceIdType.MESH)
                send = pltpu.make_async_remote_copy(src_ref=src(f, peer), dst_ref=dst(f, me), **sems)
                recv = pltpu.make_async_remote_copy(src_ref=src(f, peer), dst_ref=dst(f, peer), **sems)
                send.start()
                copies.append((send, recv))
        for send, recv in copies:
            recv.wait_recv()
        for send, recv in copies:
            send.wait_send()
        for cp in mine:
            cp.wait()

    out_shape = [jax.ShapeDtypeStruct((N_DEV, len(pl_)) + pl_[0].shape[1:], pl_[0].dtype) for pl_ in layer_chunks]
    out_shape += [jax.ShapeDtypeStruct(small_chunks.shape, small_chunks.dtype),
                  jax.ShapeDtypeStruct((N_DEV,) + rep_block.shape, rep_block.dtype)]
    hbm = pl.BlockSpec(memory_space=pl.ANY)
    return pl.pallas_call(
        body, name=name, out_shape=out_shape,
        in_specs=[hbm] * n_in, out_specs=[hbm] * n_out,
        scratch_shapes=[pltpu.SemaphoreType.DMA((7 * nf,)), pltpu.SemaphoreType.DMA((7 * nf,)),
                        pltpu.SemaphoreType.DMA((nf,))],
    )(*inputs)


def _pack_rows(size):
    return -(-size // (8 * LANES)) * 8


def _pack(arrs, dtype):
    parts, offs, r = [], [], 0
    for a in arrs:
        flat = a.astype(dtype).reshape(-1)
        nrow = _pack_rows(flat.shape[0])
        flat = jnp.pad(flat, (0, nrow * LANES - flat.shape[0]))
        parts.append(flat.reshape(nrow, LANES))
        offs.append((r, nrow))
        r += nrow
    return jnp.concatenate(parts, axis=0), offs


def _unpack(buf, offs, shapes, lead=()):
    outs = []
    for (r, nrow), shp in zip(offs, shapes):
        size = 1
        for s in shp:
            size *= s
        flat = buf[..., r:r + nrow, :].reshape(lead + (nrow * LANES,))
        outs.append(flat[..., :size].reshape(lead + tuple(shp)))
    return outs


def _cols_from_shards(g, axis):
    return jnp.concatenate([g[j] for j in range(N_DEV)], axis=axis)


BIG = ("w_in", "w_o", "w_up", "w_down")
SMALL_SHARDED = ("meta_tokens", "w_conv", "w_ffn_conv")
REPLICATED = ("hg_lower_bounds", "w_pool", "pool_scale", "hg_norm_g", "ln1_g", "ln1_b", "b_ffn_conv", "ln2_g", "ln2_b")
WEIGHTS = ("meta_tokens", "hg_lower_bounds", "w_in", "w_conv", "w_pool", "pool_scale", "hg_norm_g", "w_o",
           "ln1_g", "ln1_b", "w_up", "w_ffn_conv", "b_ffn_conv", "w_down", "ln2_g", "ln2_b")


def _pool_blockdiag(w_pool_l):
    z = jnp.zeros((POOL_GROUP, POOL_GROUP), w_pool_l.dtype)
    rows = [jnp.concatenate([w_pool_l[g] if h == g else z for h in range(4)], axis=1) for g in range(4)]
    return jnp.concatenate(rows, axis=0)


def _mixer_weights(g_in, g_o):
    w_in = jnp.transpose(g_in, (1, 0, 2)).reshape(D_MODEL, -1)
    w_o = g_o.reshape(-1, D_MODEL)
    return dict(
        w_hg=w_in[:, 768:2816],
        w_cp=jnp.concatenate([w_in[:, 0:768], w_in[:, 2816:3072]], axis=1),
        w_o=jnp.concatenate([w_o[256:768], w_o[0:256], w_o[768:1024]], axis=0))


def _ffn_weights(g_up, g_down):
    return dict(w_up=jnp.transpose(g_up, (1, 0, 2)).reshape(D_MODEL, -1), w_down=g_down.reshape(-1, D_MODEL))


def kernel(x, meta_tokens, hg_lower_bounds, w_in, w_conv, w_pool, pool_scale, hg_norm_g, w_o, ln1_g, ln1_b, w_up, w_ffn_conv, b_ffn_conv, w_down, ln2_g, ln2_b, loss_target, m_meta_tokens, m_hg_lower_bounds, m_w_in, m_w_conv, m_w_pool, m_pool_scale, m_hg_norm_g, m_w_o, m_ln1_g, m_ln1_b, m_w_up, m_w_ffn_conv, m_b_ffn_conv, m_w_down, m_ln2_g, m_ln2_b, v_meta_tokens, v_hg_lower_bounds, v_w_in, v_w_conv, v_w_pool, v_pool_scale, v_hg_norm_g, v_w_o, v_ln1_g, v_ln1_b, v_w_up, v_w_ffn_conv, v_b_ffn_conv, v_w_down, v_ln2_g, v_ln2_b):
    W = dict(meta_tokens=meta_tokens, hg_lower_bounds=hg_lower_bounds, w_in=w_in, w_conv=w_conv, w_pool=w_pool,
             pool_scale=pool_scale, hg_norm_g=hg_norm_g, w_o=w_o, ln1_g=ln1_g, ln1_b=ln1_b, w_up=w_up,
             w_ffn_conv=w_ffn_conv, b_ffn_conv=b_ffn_conv, w_down=w_down, ln2_g=ln2_g, ln2_b=ln2_b)
    M = dict(meta_tokens=m_meta_tokens, hg_lower_bounds=m_hg_lower_bounds, w_in=m_w_in, w_conv=m_w_conv,
             w_pool=m_w_pool, pool_scale=m_pool_scale, hg_norm_g=m_hg_norm_g, w_o=m_w_o, ln1_g=m_ln1_g,
             ln1_b=m_ln1_b, w_up=m_w_up, w_ffn_conv=m_w_ffn_conv, b_ffn_conv=m_b_ffn_conv, w_down=m_w_down,
             ln2_g=m_ln2_g, ln2_b=m_ln2_b)
    V = dict(meta_tokens=v_meta_tokens, hg_lower_bounds=v_hg_lower_bounds, w_in=v_w_in, w_conv=v_w_conv,
             w_pool=v_w_pool, pool_scale=v_pool_scale, hg_norm_g=v_hg_norm_g, w_o=v_w_o, ln1_g=v_ln1_g,
             ln1_b=v_ln1_b, w_up=v_w_up, w_ffn_conv=v_w_ffn_conv, b_ffn_conv=v_b_ffn_conv, w_down=v_w_down,
             ln2_g=v_ln2_g, ln2_b=v_ln2_b)
    assert x.shape[0] == 1 and x.shape[2] == D_MODEL and w_in.shape[0] == DEPTH
    seq = x.shape[1]
    L = -(-(seq + N_META) // ROW_ALIGN) * ROW_ALIGN

    small_pack, small_offs = _pack([W[n] for n in SMALL_SHARDED], F32)
    shards = {n: [W[n][l].astype(BF16) for l in range(DEPTH)] for n in BIG}
    g_in0, g_o0, small_all = _gather_many([shards["w_in"][0], shards["w_o"][0], small_pack], "gather_weights")
    full = {}
    for n, a in zip(SMALL_SHARDED, _unpack(small_all, small_offs, [W[n].shape for n in SMALL_SHARDED], (N_DEV,))):
        full[n] = _cols_from_shards(a, 1)
    order = (small_all[0, 0, 0] * 0.0).astype(BF16)
    ffn0_started = _split_start([shards["w_up"][0] + order, shards["w_down"][0] + order], False, "gather_ffn0_start")
    order = ffn0_started[4][0, 0].astype(BF16)
    layer1_started = _split_start([shards[n][1] + order for n in BIG], False, "gather_layer1_start")
    lb_in = hg_lower_bounds + layer1_started[4][0, 0]

    pad_rows = L - N_META - seq
    xp = jnp.concatenate([full["meta_tokens"], x[0], jnp.zeros((pad_rows, D_MODEL), F32)], axis=0)
    tgt = jnp.concatenate([jnp.zeros((N_META, D_MODEL), F32), loss_target[0], jnp.zeros((pad_rows, D_MODEL), F32)], axis=0)

    saved = []
    h_in, h_in_b = xp, xp.astype(BF16)
    for l in range(DEPTH):
        if l == 0:
            lw = _mixer_weights(g_in0, g_o0)
        else:
            g_in, g_o, g_up, g_down = _split_wait(layer1_started, False, h_in_b, "gather_layer1_wait")
            lw = {**_mixer_weights(g_in, g_o), **_ffn_weights(g_up, g_down)}
        wc = full["w_conv"][l].T
        wblk = _pool_blockdiag(w_pool[l]).astype(BF16)
        ps = pool_scale[l][None, :]
        gn = hg_norm_g[l][None, :]
        wf = full["w_ffn_conv"][l].T
        bf = b_ffn_conv[l][None, :]
        hh = _matmul(h_in_b, lw["w_hg"], "nn", F32, f"fwd_hg_{l}")
        hc = _matmul(h_in_b, lw["w_cp"], "nn", F32, f"fwd_cp_{l}")
        y_hg, o_raw, states, amat = _hgrn_fwd(hh, lb_in if l == 0 else hg_lower_bounds, gn, l, f"hgrn_fwd_{l}")
        y_cp = _cp_fwd(hc, wc, wblk, ps, f"convpool_fwd_{l}")
        cat = jnp.concatenate([y_hg, y_cp], axis=1)
        z1, x1, x1_b = _matmul_ln(cat, lw["w_o"], h_in, ln1_g[l][None, :], ln1_b[l][None, :], f"fwd_o_ln1_{l}")
        if l == 0:
            lw.update(_ffn_weights(*_split_wait(ffn0_started, False, x1_b, "gather_ffn0_wait")))
        up = _matmul(x1_b, lw["w_up"], "nn", BF16, f"fwd_up_{l}")
        a, u = _ffn_act_fwd(up, wf, bf, f"ffn_fwd_{l}")
        saved.append(dict(lw=lw, wc=wc, wblk=wblk, ps=ps, gn=gn, wf=wf, bf=bf, x_b=h_in_b, hh=hh, hc=hc,
                          o_raw=o_raw, states=states, amat=amat, cat=cat, z1=z1, x1_b=x1_b, up=up, u=u, a=a))
        if l < DEPTH - 1:
            saved[l]["z2"], h_in, h_in_b = _matmul_ln(a, lw["w_down"], x1, ln2_g[l][None, :], ln2_b[l][None, :],
                                                     f"fwd_down_ln2_{l}")
        else:
            saved[l]["z2"], dy, loss_part = _matmul_ln(a, lw["w_down"], x1, ln2_g[l][None, :], ln2_b[l][None, :],
                                                       f"fwd_down_ln2_loss_{l}", loss=(tgt, seq))

    loss = lax.psum(loss_part[0, 0], ("x", "y", "c"))

    G = {}
    per_layer = {n: [None] * DEPTH for n in ("w_conv", "w_pool", "pool_scale", "hg_norm_g", "ln1_g", "ln1_b",
                                             "w_ffn_conv", "b_ffn_conv", "ln2_g", "ln2_b")}
    ffn_started, mix_started = [None] * DEPTH, [None] * DEPTH
    order = jnp.zeros((), F32)
    dlb_total = jnp.zeros((DEPTH, HG_W), F32)
    for l in reversed(range(DEPTH)):
        s = saved[l]
        lw = s["lw"]
        dz2, dz2_b, dg2, db2 = _ln_bwd(s["z2"], dy, ln2_g[l][None, :] + order, f"ln2_bwd_{l}")
        da = _matmul(dz2_b, lw["w_down"], "nt", BF16, f"bwd_da_{l}")
        d_w_down = _matmul(s["a"], dz2_b, "tn", BF16, f"wgrad_down_{l}")
        dup, dwf, dbf = _ffn_act_bwd(s["up"], s["u"], da, s["wf"], f"ffn_bwd_{l}")
        dx1 = _matmul(dup, lw["w_up"], "nt", F32, f"bwd_dx1_{l}", res=dz2, alpha=ALPHA)
        d_w_up = _matmul(s["x1_b"], dup, "tn", BF16, f"wgrad_up_{l}")
        ffn_started[l] = _split_start([jnp.transpose(d_w_up.reshape(D_MODEL, N_DEV, -1), (1, 0, 2)),
                                       d_w_down.reshape(N_DEV, -1, D_MODEL)], True, f"scatter_ffn{l}_start")
        order = ffn_started[l][4][0, 0]
        dz1, dz1_b, dg1, db1 = _ln_bwd(s["z1"], dx1, ln1_g[l][None, :] + order, f"ln1_bwd_{l}")
        dcat = _matmul(dz1_b, lw["w_o"], "nt", F32, f"bwd_dcat_{l}")
        d_w_o = _matmul(s["cat"], dz1_b, "tn", BF16, f"wgrad_o_{l}")
        dhh, dlb, dgn = _hgrn_bwd(s["hh"], s["o_raw"], s["states"], s["amat"], dcat, hg_lower_bounds, s["gn"], l,
                                  f"hgrn_bwd_{l}")
        dhc, dwc, dwblk, dps = _cp_bwd(s["hc"], dcat, s["wc"], s["wblk"], s["ps"], f"convpool_bwd_{l}")
        d_w_hg = _matmul(s["x_b"], dhh, "tn", BF16, f"wgrad_hg_{l}")
        d_w_cp = _matmul(s["x_b"], dhc, "tn", BF16, f"wgrad_cp_{l}")
        d_w_in = jnp.concatenate([d_w_cp[:, 0:768], d_w_hg, d_w_cp[:, 768:1024]], axis=1)
        mix_chunks = [jnp.transpose(d_w_in.reshape(D_MODEL, N_DEV, -1), (1, 0, 2)),
                      jnp.concatenate([d_w_o[512:768], d_w_o[0:512], d_w_o[768:1024]], axis=0).reshape(N_DEV, -1, D_MODEL)]
        mix_started[l] = _split_start(mix_chunks, True, f"scatter_mix{l}_start")
        order = mix_started[l][4][0, 0]
        dx_a = _matmul(dhh, lw["w_hg"] + order.astype(BF16), "nt", F32, f"bwd_dx_hg_{l}", res=dz1, alpha=ALPHA)
        dx = _matmul(dhc, lw["w_cp"], "nt", F32, f"bwd_dx_cp_{l}", res=dx_a, alpha=1.0)
        per_layer["w_conv"][l] = dwc.T
        per_layer["w_ffn_conv"][l] = dwf.T
        per_layer["b_ffn_conv"][l] = dbf[0]
        per_layer["w_pool"][l] = jnp.stack([dwblk[g * 64:(g + 1) * 64, g * 64:(g + 1) * 64] for g in range(4)], axis=0)
        per_layer["pool_scale"][l] = dps[0]
        per_layer["hg_norm_g"][l] = dgn[0]
        per_layer["ln1_g"][l], per_layer["ln1_b"][l] = dg1[0], db1[0]
        per_layer["ln2_g"][l], per_layer["ln2_b"][l] = dg2[0], db2[0]
        dlb_total = dlb_total + dlb
        dy = dx
    for n, parts in per_layer.items():
        G[n] = jnp.stack(parts, axis=0)
    G["hg_lower_bounds"] = dlb_total
    grad_x = dy[N_META:N_META + seq][None]

    def shard_major(g, lead):
        g = g.reshape(g.shape[:lead] + (N_DEV, -1) + g.shape[lead + 1:])
        g = jnp.moveaxis(g, lead, 0).reshape(N_DEV, -1)
        nrow = _pack_rows(g.shape[1])
        return jnp.pad(g, ((0, 0), (0, nrow * LANES - g.shape[1]))).reshape(N_DEV, nrow, LANES)

    small_chunks = jnp.concatenate([shard_major(dy[0:N_META], 1), shard_major(G["w_conv"], 1),
                                    shard_major(G["w_ffn_conv"], 1)], axis=1)
    w_small, _ = _pack([W[n] for n in SMALL_SHARDED], F32)
    rep_pack, rep_offs = _pack([G[n] for n in REPLICATED], F32)
    small_recv, rep_all = _exchange_grads([], small_chunks, rep_pack, "exchange_grads")
    parts = {n: [] for n in BIG}
    for l in range(DEPTH):
        up_l, down_l = _split_wait(ffn_started[l], True, rep_all, f"scatter_ffn{l}_wait")
        in_l, o_l = _split_wait(mix_started[l], True, rep_all, f"scatter_mix{l}_wait")
        for n, a in zip(BIG, (in_l, o_l, up_l, down_l)):
            parts[n].append(a)

    res = {k: {} for k in ("grad", "delta", "new_m", "new_v")}
    kinds = ("grad", "delta", "new_m", "new_v")
    for n in BIG:
        for kind, a in zip(kinds, _adamw_layers(parts[n], W[n], M[n], V[n], f"adamw_{n}")):
            res[kind][n] = a
    m_small, _ = _pack([M[n] for n in SMALL_SHARDED], F32)
    v_small, _ = _pack([V[n] for n in SMALL_SHARDED], F32)
    outs_small = _adamw(small_recv, w_small, m_small, v_small, "adamw_small_sharded")
    w_rep, _ = _pack([W[n] for n in REPLICATED], F32)
    m_rep, _ = _pack([M[n] for n in REPLICATED], F32)
    v_rep, _ = _pack([V[n] for n in REPLICATED], F32)
    outs_rep = _adamw(rep_all, w_rep, m_rep, v_rep, "adamw_replicated")
    for kind, b_sm, b_rep in zip(kinds, outs_small, outs_rep):
        for n, a in zip(SMALL_SHARDED, _unpack(b_sm, small_offs, [W[n].shape for n in SMALL_SHARDED])):
            res[kind][n] = a
        for n, a in zip(REPLICATED, _unpack(b_rep, rep_offs, [W[n].shape for n in REPLICATED])):
            res[kind][n] = a

    return (loss, grad_x, *[res["grad"][n] for n in WEIGHTS], *[res["delta"][n] for n in WEIGHTS],
            *[res["new_m"][n] for n in WEIGHTS], *[res["new_v"][n] for n in WEIGHTS])
```

```python
import jax
import jax.numpy as jnp
from jax import lax
from jax.experimental import pallas as pl
from jax.experimental.pallas import tpu as pltpu

F32 = jnp.float32
BF16 = jnp.bfloat16

N_DEV = 8
D_MODEL = 1024
N_META = 16
DEPTH = 2
CONV_W = 256
HG_W = 512
HG_D = 128
HG_HEADS = 4
POOL_W = 256
POOL_GROUP = 64
D_FF = 2816
ALPHA = (2 * DEPTH) ** 0.25
LN_EPS = 1e-5
RMS_EPS = 1e-6
F_FLOOR = 1e-30
Q_SCALE = HG_D ** -0.5
SUB = 16
SEQ_TILE = 192
FFN_TILE = 96
ROW_ALIGN = 192
LANES = 128
VMEM_LIMIT = 48 * 1024 * 1024
MATMUL_VMEM_BUDGET = 38 * 1024 * 1024

ADAM_LR = 0.001
ADAM_B1 = 0.9
ADAM_B2 = 0.999
ADAM_EPS = 1e-08
ADAM_WD = 0.01
ADAM_STEP = 10


def _tile(n, cap, mult):
    best = 0
    for t in range(mult, min(n, cap) + 1, mult):
        if n % t == 0:
            best = t
    assert best > 0, (n, cap, mult)
    return best


def _params(sem, vmem=VMEM_LIMIT):
    return pltpu.CompilerParams(dimension_semantics=sem, vmem_limit_bytes=vmem)


def _dnt(a, b):
    return lax.dot_general(a, b, (((1,), (1,)), ((), ())), preferred_element_type=F32)


def _dtn(a, b):
    return lax.dot_general(a, b, (((0,), (0,)), ((), ())), preferred_element_type=F32)


def _dnn(a, b):
    return jnp.dot(a, b, preferred_element_type=F32)


def _sigmoid(x):
    return 1.0 / (1.0 + jnp.exp(-x))


def _matmul(a, b, mode, out_dtype, name, res=None, alpha=1.0):
    if mode == "tn":
        K, M = a.shape
    else:
        M, K = a.shape
    N = b.shape[0] if mode == "nt" else b.shape[1]
    out_bytes = jnp.dtype(out_dtype).itemsize
    tn = _tile(N, 1536, LANES)
    tk = _tile(K, 1536, 16) if mode == "tn" else _tile(K, 2816, LANES)
    nk = K // tk
    use_acc = nk > 1 and out_dtype != F32
    tm = M
    for cap in (1536, 768, 384):
        tm = _tile(M, cap, 16)
        blocks = 2 * (a.dtype.itemsize * tm * tk + b.dtype.itemsize * tn * tk + out_bytes * tm * tn
                      + (res.dtype.itemsize * tm * tn if res is not None else 0)) + (4 * tm * tn if use_acc else 0)
        if blocks <= MATMUL_VMEM_BUDGET:
            break
    dims = {"nn": ((1,), (0,)), "nt": ((1,), (1,)), "tn": ((0,), (0,))}[mode]

    def body(*refs):
        a_ref, b_ref = refs[0], refs[1]
        r_ref = refs[2] if res is not None else None
        o_ref = refs[3] if res is not None else refs[2]
        acc = refs[-1] if use_acc else o_ref
        k = pl.program_id(2)
        p = lax.dot_general(a_ref[...].astype(BF16), b_ref[...].astype(BF16), (dims, ((), ())),
                            preferred_element_type=F32)

        def finish(r):
            if r_ref is not None:
                r = r + alpha * r_ref[...].astype(F32)
            o_ref[...] = r.astype(out_dtype)

        if nk == 1:
            finish(p)
        else:
            @pl.when(k == 0)
            def _():
                acc[...] = p

            @pl.when((k > 0) & (k < nk - 1))
            def _():
                acc[...] += p

            @pl.when(k == nk - 1)
            def _():
                finish(acc[...] + p)

    if mode == "tn":
        a_spec = pl.BlockSpec((tk, tm), lambda i, j, k: (k, i))
    else:
        a_spec = pl.BlockSpec((tm, tk), lambda i, j, k: (i, k))
    if mode == "nt":
        b_spec = pl.BlockSpec((tn, tk), lambda i, j, k: (j, k))
    else:
        b_spec = pl.BlockSpec((tk, tn), lambda i, j, k: (k, j))
    in_specs = [a_spec, b_spec]
    args = [a, b]
    if res is not None:
        in_specs.append(pl.BlockSpec((tm, tn), lambda i, j, k: (i, j)))
        args.append(res)
    return pl.pallas_call(
        body, name=name,
        grid=(M // tm, N // tn, nk),
        in_specs=in_specs,
        out_specs=pl.BlockSpec((tm, tn), lambda i, j, k: (i, j)),
        out_shape=jax.ShapeDtypeStruct((M, N), out_dtype),
        scratch_shapes=[pltpu.VMEM((tm, tn), F32)] if use_acc else [],
        compiler_params=_params(("parallel", "parallel", "arbitrary")),
    )(*args)


def _matmul_ln(a, w, x, g, b, name, loss=None):
    L, K = a.shape
    D = w.shape[1]
    tr = L
    for cap in (1536, 768, 384):
        tr = _tile(L, cap, 16)
        if 2 * (2 * tr * K + 2 * K * D + 4 * tr * D * (4 if loss else 3) + 2 * tr * D) <= MATMUL_VMEM_BUDGET:
            break

    def body(*refs):
        a_ref, w_ref, x_ref, g_ref, b_ref = refs[:5]
        z = ALPHA * x_ref[...] + _dnn(a_ref[...], w_ref[...])
        mu = jnp.mean(z, axis=-1, keepdims=True)
        zc = z - mu
        var = jnp.mean(zc * zc, axis=-1, keepdims=True)
        y = zc * lax.rsqrt(var + LN_EPS) * g_ref[...] + b_ref[...]
        if loss is None:
            z_ref, y_ref, yb_ref = refs[5:]
            y_ref[...] = y
            yb_ref[...] = y.astype(BF16)
        else:
            t_ref, z_ref, dy_ref, loss_ref = refs[5:]
            i = pl.program_id(0)

            @pl.when(i == 0)
            def _():
                loss_ref[...] = jnp.zeros_like(loss_ref)

            r = i * tr + lax.broadcasted_iota(jnp.int32, (tr, D), 0)
            valid = (r >= N_META) & (r < N_META + loss[1])
            e = jnp.where(valid, y - t_ref[...], 0.0)
            dy_ref[...] = e * (1.0 / D)
            s = jnp.sum(jnp.sum(e * e, axis=-1, keepdims=True), axis=0, keepdims=True)
            loss_ref[...] += (0.5 / D) * s
        z_ref[...] = z

    row = pl.BlockSpec((tr, D), lambda i: (i, 0))
    vec = pl.BlockSpec((1, D), lambda i: (0, 0))
    in_specs = [pl.BlockSpec((tr, K), lambda i: (i, 0)), pl.BlockSpec((K, D), lambda i: (0, 0)), row, vec, vec]
    f32_rows = jax.ShapeDtypeStruct((L, D), F32)
    if loss is None:
        args, out_specs = [a, w, x, g, b], [row, row, row]
        out_shape = [f32_rows, f32_rows, jax.ShapeDtypeStruct((L, D), BF16)]
    else:
        args, in_specs = [a, w, x, g, b, loss[0]], in_specs + [row]
        out_specs = [row, row, pl.BlockSpec((1, 1), lambda i: (0, 0))]
        out_shape = [f32_rows, f32_rows, jax.ShapeDtypeStruct((1, 1), F32)]
    return pl.pallas_call(
        body, name=name, grid=(L // tr,), in_specs=in_specs, out_specs=out_specs, out_shape=out_shape,
        compiler_params=_params(("arbitrary",) if loss else ("parallel",)),
    )(*args)


def _ln_bwd(z, dy, g, name):
    L, D = z.shape
    tr = _tile(L, 768, 16)

    def body(z_ref, dy_ref, g_ref, dzb_ref, dg_ref, db_ref):
        @pl.when(pl.program_id(0) == 0)
        def _():
            dg_ref[...] = jnp.zeros_like(dg_ref)
            db_ref[...] = jnp.zeros_like(db_ref)

        z = z_ref[...]
        mu = jnp.mean(z, axis=-1, keepdims=True)
        zc = z - mu
        var = jnp.mean(zc * zc, axis=-1, keepdims=True)
        rstd = lax.rsqrt(var + LN_EPS)
        xhat = zc * rstd
        dy = dy_ref[...].astype(F32)
        dxh = dy * g_ref[...]
        m1 = jnp.mean(dxh, axis=-1, keepdims=True)
        m2 = jnp.mean(dxh * xhat, axis=-1, keepdims=True)
        dz = rstd * (dxh - m1 - xhat * m2)
        dzb_ref[...] = dz.astype(BF16)
        dg_ref[...] += jnp.sum(dy * xhat, axis=0, keepdims=True)
        db_ref[...] += jnp.sum(dy, axis=0, keepdims=True)

    row = pl.BlockSpec((tr, D), lambda i: (i, 0))
    vec = pl.BlockSpec((1, D), lambda i: (0, 0))
    return pl.pallas_call(
        body, name=name, grid=(L // tr,),
        in_specs=[row, row, vec], out_specs=[row, vec, vec],
        out_shape=[jax.ShapeDtypeStruct((L, D), BF16),
                   jax.ShapeDtypeStruct((1, D), F32), jax.ShapeDtypeStruct((1, D), F32)],
        compiler_params=_params(("arbitrary",)),
    )(z, dy, g)


def _shift_down(x, prev, k):
    T, C = x.shape
    rot = pltpu.roll(jnp.concatenate([prev, x], axis=0).reshape(T // 8 + 1, 8, C), k, 1)
    sub = lax.broadcasted_iota(jnp.int32, (T // 8, 8, C), 1)
    return jnp.where(sub < k, rot[:-1], rot[1:]).reshape(T, C)


def _shift_up(x, nxt, k):
    T, C = x.shape
    rot = pltpu.roll(jnp.concatenate([x, nxt], axis=0).reshape(T // 8 + 1, 8, C), 8 - k, 1)
    sub = lax.broadcasted_iota(jnp.int32, (T // 8, 8, C), 1)
    return jnp.where(sub >= 8 - k, rot[1:], rot[:-1]).reshape(T, C)


def _conv3(x, prev, w, b):
    return w[2:3, :] * x + w[1:2, :] * _shift_down(x, prev, 1) + w[0:1, :] * _shift_down(x, prev, 2) + b


def _ffn_act_fwd(up, w, b, name):
    L, C = up.shape
    F = C // 2
    ts = FFN_TILE
    n = L // ts

    def body(up_ref, pv_ref, w_ref, b_ref, a_ref, u_ref):
        i = pl.program_id(0)
        x = up_ref[...].astype(F32)
        prev = jnp.where(i > 0, pv_ref[...].astype(F32)[8:16], 0.0)
        u = _conv3(x, prev, w_ref[...], b_ref[...])
        u_ref[...] = u.astype(BF16)
        gate = u[:, :F]
        a_ref[...] = (gate * _sigmoid(gate) * u[:, F:]).astype(BF16)

    return pl.pallas_call(
        body, name=name, grid=(n,),
        in_specs=[pl.BlockSpec((ts, C), lambda i: (i, 0)),
                  pl.BlockSpec((16, C), lambda i: (jnp.maximum(i * (ts // 16) - 1, 0), 0)),
                  pl.BlockSpec((3, C), lambda i: (0, 0)), pl.BlockSpec((1, C), lambda i: (0, 0))],
        out_specs=[pl.BlockSpec((ts, F), lambda i: (i, 0)), pl.BlockSpec((ts, C), lambda i: (i, 0))],
        out_shape=[jax.ShapeDtypeStruct((L, F), BF16), jax.ShapeDtypeStruct((L, C), BF16)],
        compiler_params=_params(("parallel",)),
    )(up, up, w, b)


def _ffn_act_bwd(up, u, da, w, name):
    L, C = up.shape
    F = C // 2
    ts = FFN_TILE
    n = L // ts
    last16 = L // 16 - 1

    def du_of(u, da):
        gate, val = u[:, :F], u[:, F:]
        sg = _sigmoid(gate)
        dgate = da * val * (sg * (1.0 + gate * (1.0 - sg)))
        dval = da * (gate * sg)
        return jnp.concatenate([dgate, dval], axis=1)

    def body(up_ref, u_ref, un_ref, da_ref, dan_ref, w_ref, dup_ref, dw_ref, db_ref):
        i = pl.program_id(0)

        @pl.when(i == 0)
        def _():
            dw_ref[...] = jnp.zeros_like(dw_ref)
            db_ref[...] = jnp.zeros_like(db_ref)

        w = w_ref[...]
        x = up_ref[...].astype(F32)
        du = du_of(u_ref[...].astype(F32), da_ref[...].astype(F32))
        dun = jnp.where(i < n - 1, du_of(un_ref[...].astype(F32)[0:8], dan_ref[...].astype(F32)[0:8]), 0.0)
        du1 = _shift_up(du, dun, 1)
        du2 = _shift_up(du, dun, 2)
        dup_ref[...] = (w[2:3, :] * du + w[1:2, :] * du1 + w[0:1, :] * du2).astype(BF16)
        dw_ref[...] += jnp.concatenate([jnp.sum(x * du2, axis=0, keepdims=True),
                                        jnp.sum(x * du1, axis=0, keepdims=True),
                                        jnp.sum(x * du, axis=0, keepdims=True)], axis=0)
        db_ref[...] += jnp.sum(du, axis=0, keepdims=True)

    nxt = lambda i: (jnp.minimum((i + 1) * (ts // 16), last16), 0)
    return pl.pallas_call(
        body, name=name, grid=(n,),
        in_specs=[pl.BlockSpec((ts, C), lambda i: (i, 0)),
                  pl.BlockSpec((ts, C), lambda i: (i, 0)), pl.BlockSpec((16, C), nxt),
                  pl.BlockSpec((ts, F), lambda i: (i, 0)), pl.BlockSpec((16, F), nxt),
                  pl.BlockSpec((3, C), lambda i: (0, 0))],
        out_specs=[pl.BlockSpec((ts, C), lambda i: (i, 0)), pl.BlockSpec((3, C), lambda i: (0, 0)),
                   pl.BlockSpec((1, C), lambda i: (0, 0))],
        out_shape=[jax.ShapeDtypeStruct((L, C), BF16), jax.ShapeDtypeStruct((3, C), F32),
                   jax.ShapeDtypeStruct((1, C), F32)],
        compiler_params=_params(("arbitrary",)),
    )(up, u, u, da, da, w)


def _pool_window(ext, tile_rows, first_row, lead):
    T = ext.shape[0]
    sh = (lambda x, k: pltpu.roll(x, T - k, 0)) if lead else (lambda x, k: pltpu.roll(x, k, 0))
    r2 = ext + sh(ext, 1)
    r4 = r2 + sh(r2, 2)
    r8 = r4 + sh(r4, 4)
    r16 = r8 + sh(r8, 8)
    lo = 0 if lead else 16
    grp = lax.broadcasted_iota(jnp.int32, (tile_rows, POOL_W), 1) // POOL_GROUP
    pick = lambda a, b, c, d: jnp.where(grp == 0, a, jnp.where(grp == 1, b, jnp.where(grp == 2, c, d)))
    win = pick(r2[lo:lo + tile_rows], r4[lo:lo + tile_rows], r8[lo:lo + tile_rows], r16[lo:lo + tile_rows])
    return win, pick(2.0, 4.0, 8.0, 16.0)


def _pool_count(first_row, rows, wlen):
    t1 = (first_row + lax.broadcasted_iota(jnp.int32, (rows, POOL_W), 0) + 1).astype(F32)
    return jnp.minimum(t1, wlen)


def _cp_fwd(hc, wc, wblk, pscale, name):
    L = hc.shape[0]
    ts = SEQ_TILE
    n = L // ts

    def body(h_ref, hp_ref, wc_ref, wb_ref, ps_ref, y_ref):
        i = pl.program_id(0)
        h = h_ref[...]
        hp = jnp.where(i > 0, hp_ref[...], 0.0)
        cb, cc, cv, pv = h[:, 0:256], h[:, 256:512], h[:, 512:768], h[:, 768:1024]
        p = cc * cv
        pp = hp[8:16, 256:512] * hp[8:16, 512:768]
        w = wc_ref[...]
        conv = w[2:3, :] * p + w[1:2, :] * _shift_down(p, pp, 1) + w[0:1, :] * _shift_down(p, pp, 2)
        y_conv = cb * conv
        ext = jnp.concatenate([hp[:, 768:1024], pv], axis=0)
        win, wlen = _pool_window(ext, ts, i * ts, False)
        d = win / _pool_count(i * ts, ts, wlen) - pv
        y_pool = _dnn(d.astype(BF16), wb_ref[...]) * ps_ref[...]
        y_ref[...] = jnp.concatenate([y_conv, y_pool], axis=1).astype(BF16)

    return pl.pallas_call(
        body, name=name, grid=(n,),
        in_specs=[pl.BlockSpec((ts, 1024), lambda i: (i, 0)),
                  pl.BlockSpec((16, 1024), lambda i: (jnp.maximum(i * (ts // 16) - 1, 0), 0)),
                  pl.BlockSpec((3, 256), lambda i: (0, 0)), pl.BlockSpec((256, 256), lambda i: (0, 0)),
                  pl.BlockSpec((1, 256), lambda i: (0, 0))],
        out_specs=pl.BlockSpec((ts, 512), lambda i: (i, 0)),
        out_shape=jax.ShapeDtypeStruct((L, 512), BF16),
        compiler_params=_params(("parallel",)),
    )(hc, hc, wc, wblk, pscale)


def _cp_bwd(hc, dcat, wc, wblk, pscale, name):
    L = hc.shape[0]
    ts = SEQ_TILE
    n = L // ts
    last16 = L // 16 - 1

    def body(h_ref, hp_ref, hn_ref, dy_ref, dyn_ref, wc_ref, wb_ref, ps_ref,
             dh_ref, dwc_ref, dwb_ref, dps_ref):
        i = pl.program_id(0)

        @pl.when(i == 0)
        def _():
            dwc_ref[...] = jnp.zeros_like(dwc_ref)
            dwb_ref[...] = jnp.zeros_like(dwb_ref)
            dps_ref[...] = jnp.zeros_like(dps_ref)

        h = h_ref[...]
        hp = jnp.where(i > 0, hp_ref[...], 0.0)
        hn = hn_ref[...]
        dy = dy_ref[...].astype(F32)
        dyn = jnp.where(i < n - 1, dyn_ref[...].astype(F32), 0.0)
        cb, cc, cv, pv = h[:, 0:256], h[:, 256:512], h[:, 512:768], h[:, 768:1024]
        w = wc_ref[...]
        p = cc * cv
        pp = hp[8:16, 256:512] * hp[8:16, 512:768]
        p1 = _shift_down(p, pp, 1)
        p2 = _shift_down(p, pp, 2)
        conv = w[2:3, :] * p + w[1:2, :] * p1 + w[0:1, :] * p2
        dyc = dy[:, 0:256]
        dcb = dyc * conv
        dconv = dyc * cb
        dconv_n = dyn[0:8, 0:256] * hn[0:8, 0:256]
        dc1 = _shift_up(dconv, dconv_n, 1)
        dc2 = _shift_up(dconv, dconv_n, 2)
        dp = w[2:3, :] * dconv + w[1:2, :] * dc1 + w[0:1, :] * dc2
        dwc_ref[...] += jnp.concatenate([jnp.sum(p * dc2, axis=0, keepdims=True),
                                         jnp.sum(p * dc1, axis=0, keepdims=True),
                                         jnp.sum(p * dconv, axis=0, keepdims=True)], axis=0)
        ps = ps_ref[...]
        wb = wb_ref[...]
        ext = jnp.concatenate([hp[:, 768:1024], pv], axis=0)
        win, wlen = _pool_window(ext, ts, i * ts, False)
        d = win / _pool_count(i * ts, ts, wlen) - pv
        db = d.astype(BF16)
        dyp = dy[:, 256:512]
        dps_ref[...] += jnp.sum(dyp * _dnn(db, wb), axis=0, keepdims=True)
        dypre = (dyp * ps).astype(BF16)
        dwb_ref[...] += _dtn(db, dypre)
        dd = _dnt(dypre, wb)
        ddn = _dnt((dyn[:, 256:512] * ps).astype(BF16), wb)
        e = dd / _pool_count(i * ts, ts, wlen)
        en = ddn / _pool_count((i + 1) * ts, 16, wlen[0:16])
        lead, _ = _pool_window(jnp.concatenate([e, en], axis=0), ts, i * ts, True)
        dpv = lead - dd
        dh_ref[...] = jnp.concatenate([dcb, dp * cv, dp * cc, dpv], axis=1).astype(BF16)

    return pl.pallas_call(
        body, name=name, grid=(n,),
        in_specs=[pl.BlockSpec((ts, 1024), lambda i: (i, 0)),
                  pl.BlockSpec((16, 1024), lambda i: (jnp.maximum(i * (ts // 16) - 1, 0), 0)),
                  pl.BlockSpec((16, 1024), lambda i: (jnp.minimum((i + 1) * (ts // 16), last16), 0)),
                  pl.BlockSpec((ts, 512), lambda i: (i, 1)),
                  pl.BlockSpec((16, 512), lambda i: (jnp.minimum((i + 1) * (ts // 16), last16), 1)),
                  pl.BlockSpec((3, 256), lambda i: (0, 0)), pl.BlockSpec((256, 256), lambda i: (0, 0)),
                  pl.BlockSpec((1, 256), lambda i: (0, 0))],
        out_specs=[pl.BlockSpec((ts, 1024), lambda i: (i, 0)), pl.BlockSpec((3, 256), lambda i: (0, 0)),
                   pl.BlockSpec((256, 256), lambda i: (0, 0)), pl.BlockSpec((1, 256), lambda i: (0, 0))],
        out_shape=[jax.ShapeDtypeStruct((L, 1024), BF16), jax.ShapeDtypeStruct((3, 256), F32),
                   jax.ShapeDtypeStruct((256, 256), F32), jax.ShapeDtypeStruct((1, 256), F32)],
        compiler_params=_params(("arbitrary",)),
    )(hc, hc, hc, dcat, dcat, wc, wblk, pscale)


def _lower_bound(lb_ref, layer):
    b0, b1 = lb_ref[0:1, :], lb_ref[1:2, :]
    m = jnp.maximum(b0, b1)
    e0, e1 = jnp.exp(b0 - m), jnp.exp(b1 - m)
    p0, p1 = e0 / (e0 + e1), e1 / (e0 + e1)
    lb = (p0 - p0) if layer == 0 else ((p0 + p1) - p0)
    return lb, p0, p1


def _cumsum_rows(x, reverse=False):
    row = lax.broadcasted_iota(jnp.int32, x.shape, 0)
    for sh in (1, 2, 4, 8):
        if reverse:
            x = x + jnp.where(row < SUB - sh, pltpu.roll(x, SUB - sh, 0), 0.0)
        else:
            x = x + jnp.where(row >= sh, pltpu.roll(x, sh, 0), 0.0)
    return x


def _gates(fz, lb):
    sig = _sigmoid(fz)
    f = lb + (1.0 - lb) * sig
    g = jnp.log(jnp.maximum(f, F_FLOOR))
    k = (1.0 - lb) * (1.0 - sig)
    return sig, f, g, k


def _head(h):
    return slice(h * HG_D, (h + 1) * HG_D)


def _hgrn_fwd(hh, lbp, gnorm, layer, name):
    L = hh.shape[0]
    ts = SEQ_TILE
    n = L // ts
    nsub = ts // SUB

    def body(q_ref, f_ref, i_ref, g_ref, lb_ref, gn_ref, y_ref, o_ref, s_ref, a_ref, St):
        @pl.when(pl.program_id(0) == 0)
        def _():
            St[...] = jnp.zeros_like(St)

        lb, _, _ = _lower_bound(lb_ref, layer)
        gn = jnp.tile(gn_ref[...], (1, HG_HEADS))
        r16 = lax.broadcasted_iota(jnp.int32, (SUB, SUB), 0)
        c16 = lax.broadcasted_iota(jnp.int32, (SUB, SUB), 1)

        def block(j, carry):
            rows = pl.ds(pl.multiple_of(j * SUB, SUB), SUB)
            q = q_ref[rows, :] * Q_SCALE
            iv = i_ref[rows, :]
            gz = g_ref[rows, :]
            _, _, g, k = _gates(f_ref[rows, :], lb)
            G = _cumsum_rows(g)
            Gl = G[SUB - 1:SUB, :]
            qt = (q * jnp.exp(G)).astype(BF16)
            kd = (k * jnp.exp(Gl - G)).astype(BF16)
            eGl = jnp.exp(Gl)
            ib = iv.astype(BF16)
            A = [jnp.zeros((SUB, SUB), F32) for _ in range(HG_HEADS)]
            for s in range(SUB):
                P = q * jnp.exp(jnp.minimum(G - G[s:s + 1, :], 0.0)) * k[s:s + 1, :]
                for h in range(HG_HEADS):
                    A[h] = jnp.where(c16 == s, jnp.sum(P[:, _head(h)], axis=-1, keepdims=True), A[h])
            outs, ons, amats = [], [], []
            for h in range(HG_HEADS):
                sl = _head(h)
                Sb = St[h].astype(BF16)
                s_ref[j, sl, :] = Sb
                Am = jnp.where(r16 >= c16, A[h], 0.0)
                amats.append(Am)
                o = _dnt(qt[:, sl], Sb) + _dnn(Am.astype(BF16), ib[:, sl])
                St[h] = eGl[:, sl] * St[h] + _dtn(ib[:, sl], kd[:, sl])
                outs.append(o)
                ons.append(o * lax.rsqrt(jnp.mean(o * o, axis=-1, keepdims=True) + RMS_EPS))
            a_ref[rows, :] = jnp.concatenate(amats, axis=1)
            o_ref[rows, :] = jnp.concatenate(outs, axis=1)
            y = jnp.concatenate(ons, axis=1) * gn * (gz * _sigmoid(gz))
            y_ref[rows, :] = y.astype(BF16)
            return carry

        lax.fori_loop(0, nsub, block, 0, unroll=2)

    col = lambda c: pl.BlockSpec((ts, HG_W), lambda i: (i, c))
    return pl.pallas_call(
        body, name=name, grid=(n,),
        in_specs=[col(0), col(1), col(2), col(3), pl.BlockSpec((2, HG_W), lambda i: (0, 0)),
                  pl.BlockSpec((1, HG_D), lambda i: (0, 0))],
        out_specs=[pl.BlockSpec((ts, HG_W), lambda i: (i, 0)), pl.BlockSpec((ts, HG_W), lambda i: (i, 0)),
                   pl.BlockSpec((nsub, HG_W, HG_D), lambda i: (i, 0, 0)),
                   pl.BlockSpec((ts, HG_HEADS * SUB), lambda i: (i, 0))],
        out_shape=[jax.ShapeDtypeStruct((L, HG_W), BF16), jax.ShapeDtypeStruct((L, HG_W), F32),
                   jax.ShapeDtypeStruct((L // SUB, HG_W, HG_D), BF16),
                   jax.ShapeDtypeStruct((L, HG_HEADS * SUB), F32)],
        scratch_shapes=[pltpu.VMEM((HG_HEADS, HG_D, HG_D), F32)],
        compiler_params=_params(("arbitrary",)),
    )(hh, hh, hh, hh, lbp, gnorm)


def _hgrn_bwd(hh, o_raw, states, amat, dcat, lbp, gnorm, layer, name):
    L = hh.shape[0]
    ts = SEQ_TILE
    n = L // ts
    nsub = ts // SUB

    def body(q_ref, f_ref, i_ref, g_ref, o_ref, s_ref, a_ref, dy_ref, lb_ref, gn_ref,
             dh_ref, dlb_ref, dgn_ref, dSt, dlb_acc, S_next):
        step = pl.program_id(0)

        @pl.when(step == 0)
        def _():
            dSt[...] = jnp.zeros_like(dSt)
            S_next[...] = jnp.zeros_like(S_next)
            dlb_acc[...] = jnp.zeros_like(dlb_acc)
            dgn_ref[...] = jnp.zeros_like(dgn_ref)

        lb, p0, p1 = _lower_bound(lb_ref, layer)
        gnh = gn_ref[...]
        gn = jnp.tile(gnh, (1, HG_HEADS))
        r16 = lax.broadcasted_iota(jnp.int32, (SUB, SUB), 0)
        c16 = lax.broadcasted_iota(jnp.int32, (SUB, SUB), 1)

        def block(jj, carry):
            j = nsub - 1 - jj
            rows = pl.ds(pl.multiple_of(j * SUB, SUB), SUB)
            q = q_ref[rows, :] * Q_SCALE
            iv = i_ref[rows, :]
            gz = g_ref[rows, :]
            o = o_ref[rows, :]
            dy = dy_ref[rows, :].astype(F32)
            sig, f, g, k = _gates(f_ref[rows, :], lb)
            G = _cumsum_rows(g)
            Gl = G[SUB - 1:SUB, :]
            eG = jnp.exp(G)
            edl = jnp.exp(Gl - G)
            eGl = jnp.exp(Gl)
            qt = (q * eG).astype(BF16)
            kd = (k * edl).astype(BF16)
            ib = iv.astype(BF16)
            sgz = _sigmoid(gz)
            sil = gz * sgz
            dyn = dy * sil
            on_parts, do_parts = [], []
            dgn = jnp.zeros((1, HG_D), F32)
            for h in range(HG_HEADS):
                sl = _head(h)
                oh = o[:, sl]
                rs = lax.rsqrt(jnp.mean(oh * oh, axis=-1, keepdims=True) + RMS_EPS)
                on = oh * rs
                dgn = dgn + jnp.sum(dyn[:, sl] * on, axis=0, keepdims=True)
                don = dyn[:, sl] * gnh
                do_parts.append(rs * (don - on * jnp.mean(don * on, axis=-1, keepdims=True)))
                on_parts.append(on)
            dgn_ref[...] += dgn
            on_all = jnp.concatenate(on_parts, axis=1)
            dgz = dy * on_all * gn * (sgz * (1.0 + gz * (1.0 - sgz)))
            do = jnp.concatenate(do_parts, axis=1)
            dob = do.astype(BF16)
            amat = a_ref[rows, :]
            dq_p, dk_p, di_p, tail_p = [], [], [], []
            for h in range(HG_HEADS):
                sl = _head(h)
                qh, kh, Gh = q[:, sl], k[:, sl], G[:, sl]
                Ap = jnp.where(r16 >= c16, _dnt(dob[:, sl], ib[:, sl]), 0.0)
                ApT = jnp.where(r16 <= c16, _dnt(ib[:, sl], dob[:, sl]), 0.0)
                dqh = jnp.zeros((SUB, HG_D), F32)
                dkh = jnp.zeros((SUB, HG_D), F32)
                for s in range(SUB):
                    dGs = Gh - Gh[s:s + 1, :]
                    e = jnp.exp(jnp.minimum(dGs, -dGs))
                    dqh = dqh + Ap[:, s:s + 1] * (e * kh[s:s + 1, :])
                    dkh = dkh + ApT[:, s:s + 1] * (e * qh[s:s + 1, :])
                Sb = s_ref[j, sl, :]
                dSb = dSt[h].astype(BF16)
                Am = amat[:, h * SUB:(h + 1) * SUB].astype(BF16)
                dq_p.append(dqh + eG[:, sl] * _dnn(dob[:, sl], Sb))
                dk_p.append(dkh + edl[:, sl] * _dnn(ib[:, sl], dSb))
                di_p.append(_dtn(Am, dob[:, sl]) + _dnt(kd[:, sl], dSb))
                tail_p.append(jnp.sum(dSt[h] * S_next[h].astype(F32), axis=0, keepdims=True))
                S_next[h] = Sb
                dSt[h] = eGl[:, sl] * dSt[h] + _dtn(dob[:, sl], qt[:, sl])
            dq = jnp.concatenate(dq_p, axis=1)
            dk = jnp.concatenate(dk_p, axis=1)
            di = jnp.concatenate(di_p, axis=1)
            dg = _cumsum_rows(q * dq - k * dk, reverse=True) + jnp.concatenate(tail_p, axis=1)
            df = jnp.where(f > F_FLOOR, dg / f, 0.0)
            dfk = df - dk
            dfz = (1.0 - lb) * dfk * sig * (1.0 - sig)
            dlb_acc[...] += jnp.sum(dfk * (1.0 - sig), axis=0, keepdims=True)
            dh_ref[rows, :] = jnp.concatenate([dq * Q_SCALE, dfz, di, dgz], axis=1).astype(BF16)
            return carry

        lax.fori_loop(0, nsub, block, 0)

        @pl.when(step == n - 1)
        def _():
            if layer == 0:
                dlb_ref[...] = jnp.zeros_like(dlb_ref)
            else:
                dz1 = p0 * p1 * dlb_acc[...]
                dlb_ref[...] = jnp.concatenate([-dz1, dz1], axis=0)

    rev = lambda i: n - 1 - i
    col = lambda c: pl.BlockSpec((ts, HG_W), lambda i: (rev(i), c))
    return pl.pallas_call(
        body, name=name, grid=(n,),
        in_specs=[col(0), col(1), col(2), col(3), col(0),
                  pl.BlockSpec((nsub, HG_W, HG_D), lambda i: (rev(i), 0, 0)),
                  pl.BlockSpec((ts, HG_HEADS * SUB), lambda i: (rev(i), 0)), col(0),
                  pl.BlockSpec((2, HG_W), lambda i: (0, 0)), pl.BlockSpec((1, HG_D), lambda i: (0, 0))],
        out_specs=[pl.BlockSpec((ts, 4 * HG_W), lambda i: (rev(i), 0)),
                   pl.BlockSpec((2, HG_W), lambda i: (0, 0)), pl.BlockSpec((1, HG_D), lambda i: (0, 0))],
        out_shape=[jax.ShapeDtypeStruct((L, 4 * HG_W), BF16), jax.ShapeDtypeStruct((2, HG_W), F32),
                   jax.ShapeDtypeStruct((1, HG_D), F32)],
        scratch_shapes=[pltpu.VMEM((HG_HEADS, HG_D, HG_D), F32), pltpu.VMEM((1, HG_W), F32),
                        pltpu.VMEM((HG_HEADS, HG_D, HG_D), BF16)],
        compiler_params=_params(("arbitrary",)),
    )(hh, hh, hh, hh, o_raw, states, amat, dcat, lbp, gnorm)


def _adamw_body(gp_ref, w_ref, m_ref, v_ref, g_ref, d_ref, mo_ref, vo_ref):
    c1 = 1.0 - ADAM_B1 ** ADAM_STEP
    c2 = 1.0 - ADAM_B2 ** ADAM_STEP
    g = gp_ref[0].astype(F32)
    for k in range(1, N_DEV):
        g = g + gp_ref[k].astype(F32)
    mn = ADAM_B1 * m_ref[...] + (1.0 - ADAM_B1) * g
    vn = ADAM_B2 * v_ref[...] + (1.0 - ADAM_B2) * (g * g)
    m_hat = mn / c1
    v_hat = vn / c2
    g_ref[...] = g
    d_ref[...] = -ADAM_LR * (m_hat / (jnp.sqrt(v_hat) + ADAM_EPS) + ADAM_WD * w_ref[...])
    mo_ref[...] = mn
    vo_ref[...] = vn


def _adamw_layers(gparts, w, m, v, name):
    depth, R, C = w.shape
    tr = _tile(R, 256, 16)
    nr = R // tr

    def body(*refs):
        layer = pl.program_id(0)
        for d in range(depth):
            @pl.when(layer == d)
            def _(d=d):
                _adamw_body(refs[d], *refs[depth:])

    def parts_spec(d):
        return pl.BlockSpec((N_DEV, tr, C),
                            lambda l, i: (0, jnp.where(l == d, i, jnp.where(l < d, 0, nr - 1)), 0))

    blk = pl.BlockSpec((None, tr, C), lambda l, i: (l, i, 0))
    shp = jax.ShapeDtypeStruct((depth, R, C), F32)
    return pl.pallas_call(
        body, name=name, grid=(depth, nr),
        in_specs=[parts_spec(d) for d in range(depth)] + [blk, blk, blk],
        out_specs=[blk, blk, blk, blk], out_shape=[shp, shp, shp, shp],
        compiler_params=_params(("arbitrary", "arbitrary")),
    )(*gparts, w, m, v)


def _adamw(gparts, w, m, v, name):
    R = w.shape[0]
    tr = _tile(R, 1024, 16) if R % 16 == 0 else R

    def body(*refs):
        _adamw_body(*refs)

    row = pl.BlockSpec((tr, LANES), lambda i: (i, 0))
    shp = jax.ShapeDtypeStruct((R, LANES), F32)
    return pl.pallas_call(
        body, name=name, grid=(R // tr,),
        in_specs=[pl.BlockSpec((N_DEV, tr, LANES), lambda i: (0, i, 0)), row, row, row],
        out_specs=[row, row, row, row], out_shape=[shp, shp, shp, shp],
        compiler_params=_params(("parallel",)),
    )(gparts, w, m, v)


def _flip(coord, bit):
    return 1 - coord if bit else coord


def _gather_many(blocks, name):
    n = len(blocks)

    def body(*refs):
        x_refs, out_refs = refs[:n], refs[n:2 * n]
        send_sems, recv_sems, local_sems = refs[2 * n:]
        x, y, c = lax.axis_index("x"), lax.axis_index("y"), lax.axis_index("c")
        me, sibling = (x, y, c), (x, y, 1 - c)
        chips = [(1 - x, y), (x, 1 - y), (1 - x, 1 - y)]

        def slot(a, px, py, pc):
            return out_refs[a].at[4 * px + 2 * py + pc]

        def copy(a, k, blk, to, src=None):
            return pltpu.make_async_remote_copy(
                src_ref=slot(a, *blk) if src is None else src, dst_ref=slot(a, *blk),
                send_sem=send_sems.at[7 * a + k], recv_sem=recv_sems.at[7 * a + k],
                device_id=to, device_id_type=pl.DeviceIdType.MESH)

        mine = [pltpu.make_async_copy(x_refs[a], slot(a, *me), local_sems.at[a]) for a in range(n)]
        for cp in mine:
            cp.start()
        first = [copy(a, 0, me, sibling, src=x_refs[a]) for a in range(n)]
        for j, chip in enumerate(chips):
            first += [copy(a, 1 + j, me, (*chip, c), src=x_refs[a]) for a in range(n)]
        for cp in first:
            cp.start()
        passed = []
        for j, chip in enumerate(chips):
            for a in range(n):
                copy(a, 1 + j, (*chip, c), me).wait_recv()
                fwd = copy(a, 4 + j, (*chip, c), sibling)
                fwd.start()
                passed.append(fwd)
        for a in range(n):
            copy(a, 0, sibling, me).wait_recv()
        for j, chip in enumerate(chips):
            for a in range(n):
                copy(a, 4 + j, (*chip, 1 - c), me).wait_recv()
        for cp in first + passed:
            cp.wait_send()
        for cp in mine:
            cp.wait()

    hbm = pl.BlockSpec(memory_space=pl.ANY)
    return pl.pallas_call(
        body, name=name,
        out_shape=[jax.ShapeDtypeStruct((N_DEV,) + b.shape, b.dtype) for b in blocks],
        in_specs=[hbm] * n, out_specs=[hbm] * n,
        scratch_shapes=[pltpu.SemaphoreType.DMA((7 * n,)), pltpu.SemaphoreType.DMA((7 * n,)),
                        pltpu.SemaphoreType.DMA((n,))],
    )(*blocks)


def _split_start(blocks, chunked, name):
    n = len(blocks)
    lands = [lax.empty(b.shape if chunked else (N_DEV,) + b.shape, b.dtype) for b in blocks]

    def body(*refs):
        x_refs, land_refs = refs[:n], refs[n:2 * n]
        send_sems, recv_sems, token = refs[2 * n], refs[2 * n + 1], refs[-1]
        x, y, c = lax.axis_index("x"), lax.axis_index("y"), lax.axis_index("c")
        me = 4 * x + 2 * y + c
        for a in range(n):
            for k in range(1, N_DEV):
                px, py, pc = _flip(x, k & 4), _flip(y, k & 2), _flip(c, k & 1)
                pltpu.make_async_remote_copy(
                    src_ref=x_refs[a].at[4 * px + 2 * py + pc] if chunked else x_refs[a],
                    dst_ref=land_refs[a].at[me],
                    send_sem=send_sems.at[7 * a + k - 1], recv_sem=recv_sems.at[7 * a + k - 1],
                    device_id=(px, py, pc), device_id_type=pl.DeviceIdType.MESH).start()
        token[...] = jnp.zeros_like(token)

    hbm = pl.BlockSpec(memory_space=pltpu.HBM)
    sem = pl.BlockSpec(memory_space=pltpu.SEMAPHORE)
    outs = pl.pallas_call(
        body, name=name,
        out_shape=(pltpu.SemaphoreType.DMA((7 * n,)), pltpu.SemaphoreType.DMA((7 * n,)),
                   *[pltpu.HBM(b.shape, b.dtype) for b in blocks], *[pltpu.HBM(l.shape, l.dtype) for l in lands],
                   jax.ShapeDtypeStruct((8, LANES), F32)),
        in_specs=[hbm] * (2 * n),
        out_specs=(sem, sem, *[hbm] * (2 * n), pl.BlockSpec(memory_space=pltpu.VMEM)),
        input_output_aliases={i: 2 + i for i in range(2 * n)},
        compiler_params=pltpu.CompilerParams(has_side_effects=pltpu.SideEffectType.DATAFLOW_SIDE_EFFECTING),
    )(*[pltpu.with_memory_space_constraint(b, pltpu.HBM) for b in blocks],
      *[pltpu.with_memory_space_constraint(l, pltpu.HBM) for l in lands])
    return outs[0], outs[1], list(outs[2:2 + n]), list(outs[2 + n:2 + 2 * n]), outs[-1]


def _split_wait(started, chunked, after, name):
    send_sems, recv_sems, blocks, lands, _ = started
    n = len(blocks)

    def body(*refs):
        x_refs, land_refs = refs[:n], refs[n:2 * n]
        send_sems, recv_sems = refs[2 * n], refs[2 * n + 1]
        x, y, c = lax.axis_index("x"), lax.axis_index("y"), lax.axis_index("c")
        for a in range(n):
            for k in range(1, N_DEV):
                px, py, pc = _flip(x, k & 4), _flip(y, k & 2), _flip(c, k & 1)
                copy = pltpu.make_async_remote_copy(
                    src_ref=x_refs[a].at[4 * px + 2 * py + pc] if chunked else x_refs[a],
                    dst_ref=land_refs[a].at[4 * px + 2 * py + pc],
                    send_sem=send_sems.at[7 * a + k - 1], recv_sem=recv_sems.at[7 * a + k - 1],
                    device_id=(px, py, pc), device_id_type=pl.DeviceIdType.MESH)
                copy.wait_send()
                copy.wait_recv()

    hbm = pl.BlockSpec(memory_space=pltpu.HBM)
    sem = pl.BlockSpec(memory_space=pltpu.SEMAPHORE)
    outs = pl.pallas_call(
        body, name=name,
        out_shape=(*[pltpu.HBM(b.shape, b.dtype) for b in blocks], *[pltpu.HBM(l.shape, l.dtype) for l in lands]),
        in_specs=[hbm] * (2 * n) + [sem, sem, pl.BlockSpec(memory_space=pl.ANY)],
        out_specs=[hbm] * (2 * n),
        input_output_aliases={i: i for i in range(2 * n)},
        compiler_params=pltpu.CompilerParams(has_side_effects=pltpu.SideEffectType.DATAFLOW_SIDE_EFFECTING),
    )(*blocks, *lands, send_sems, recv_sems, after)
    me = 4 * lax.axis_index("x") + 2 * lax.axis_index("y") + lax.axis_index("c")
    own = [lax.dynamic_index_in_dim(b, me, 0, keepdims=False) if chunked else b for b in outs[:n]]
    return [lax.dynamic_update_index_in_dim(z, o, me, 0) for z, o in zip(outs[n:], own)]


def _exchange_grads(layer_chunks, small_chunks, rep_block, name):
    flows, inputs = [], []
    for p, per_layer in enumerate(layer_chunks):
        for l, arr in enumerate(per_layer):
            flows.append(("param", p, l))
            inputs.append(arr)
    flows += [("small",), ("rep",)]
    inputs += [small_chunks, rep_block]
    n_par = len(layer_chunks)
    n_in, n_out, nf = len(inputs), n_par + 2, len(flows)

    def body(*refs):
        in_refs, out_refs = refs[:n_in], refs[n_in:n_in + n_out]
        send_sems, recv_sems, local_sems = refs[n_in + n_out:]
        x, y, c = lax.axis_index("x"), lax.axis_index("y"), lax.axis_index("c")
        me = 4 * x + 2 * y + c

        def src(f, dev):
            return in_refs[f] if flows[f][0] == "rep" else in_refs[f].at[dev]

        def dst(f, dev):
            if flows[f][0] == "param":
                _, p, l = flows[f]
                return out_refs[p].at[dev, l]
            return out_refs[n_par + (0 if flows[f][0] == "small" else 1)].at[dev]

        mine = [pltpu.make_async_copy(src(f, me), dst(f, me), local_sems.at[f]) for f in range(nf)]
        for cp in mine:
            cp.start()
        copies = []
        for k in range(1, N_DEV):
            px, py, pc = _flip(x, k & 4), _flip(y, k & 2), _flip(c, k & 1)
            peer = 4 * px + 2 * py + pc
            for f in range(nf):
                sems = dict(send_sem=send_sems.at[7 * f + k - 1], recv_sem=recv_sems.at[7 * f + k - 1],
                            device_id=(px, py, pc), device_id_type=pl.DeviceIdType.MESH)
                send = pltpu.make_async_remote_copy(src_ref=src(f, peer), dst_ref=dst(f, me), **sems)
                recv = pltpu.make_async_remote_copy(src_ref=src(f, peer), dst_ref=dst(f, peer), **sems)
                send.start()
                copies.append((send, recv))
        for send, recv in copies:
            recv.wait_recv()
        for send, recv in copies:
            send.wait_send()
        for cp in mine:
            cp.wait()

    out_shape = [jax.ShapeDtypeStruct((N_DEV, len(pl_)) + pl_[0].shape[1:], pl_[0].dtype) for pl_ in layer_chunks]
    out_shape += [jax.ShapeDtypeStruct(small_chunks.shape, small_chunks.dtype),
                  jax.ShapeDtypeStruct((N_DEV,) + rep_block.shape, rep_block.dtype)]
    hbm = pl.BlockSpec(memory_space=pl.ANY)
    return pl.pallas_call(
        body, name=name, out_shape=out_shape,
        in_specs=[hbm] * n_in, out_specs=[hbm] * n_out,
        scratch_shapes=[pltpu.SemaphoreType.DMA((7 * nf,)), pltpu.SemaphoreType.DMA((7 * nf,)),
                        pltpu.SemaphoreType.DMA((nf,))],
    )(*inputs)


def _pack_rows(size):
    return -(-size // (8 * LANES)) * 8


def _pack(arrs, dtype):
    parts, offs, r = [], [], 0
    for a in arrs:
        flat = a.astype(dtype).reshape(-1)
        nrow = _pack_rows(flat.shape[0])
        flat = jnp.pad(flat, (0, nrow * LANES - flat.shape[0]))
        parts.append(flat.reshape(nrow, LANES))
        offs.append((r, nrow))
        r += nrow
    return jnp.concatenate(parts, axis=0), offs


def _unpack(buf, offs, shapes, lead=()):
    outs = []
    for (r, nrow), shp in zip(offs, shapes):
        size = 1
        for s in shp:
            size *= s
        flat = buf[..., r:r + nrow, :].reshape(lead + (nrow * LANES,))
        outs.append(flat[..., :size].reshape(lead + tuple(shp)))
    return outs


def _cols_from_shards(g, axis):
    return jnp.concatenate([g[j] for j in range(N_DEV)], axis=axis)


BIG = ("w_in", "w_o", "w_up", "w_down")
SMALL_SHARDED = ("meta_tokens", "w_conv", "w_ffn_conv")
REPLICATED = ("hg_lower_bounds", "w_pool", "pool_scale", "hg_norm_g", "ln1_g", "ln1_b", "b_ffn_conv", "ln2_g", "ln2_b")
WEIGHTS = ("meta_tokens", "hg_lower_bounds", "w_in", "w_conv", "w_pool", "pool_scale", "hg_norm_g", "w_o",
           "ln1_g", "ln1_b", "w_up", "w_ffn_conv", "b_ffn_conv", "w_down", "ln2_g", "ln2_b")


def _pool_blockdiag(w_pool_l):
    z = jnp.zeros((POOL_GROUP, POOL_GROUP), w_pool_l.dtype)
    rows = [jnp.concatenate([w_pool_l[g] if h == g else z for h in range(4)], axis=1) for g in range(4)]
    return jnp.concatenate(rows, axis=0)


def _mixer_weights(g_in, g_o):
    w_in = jnp.transpose(g_in, (1, 0, 2)).reshape(D_MODEL, -1)
    w_o = g_o.reshape(-1, D_MODEL)
    return dict(
        w_hg=w_in[:, 768:2816],
        w_cp=jnp.concatenate([w_in[:, 0:768], w_in[:, 2816:3072]], axis=1),
        w_o=jnp.concatenate([w_o[256:768], w_o[0:256], w_o[768:1024]], axis=0))


def _ffn_weights(g_up, g_down):
    return dict(w_up=jnp.transpose(g_up, (1, 0, 2)).reshape(D_MODEL, -1), w_down=g_down.reshape(-1, D_MODEL))


def kernel(x, meta_tokens, hg_lower_bounds, w_in, w_conv, w_pool, pool_scale, hg_norm_g, w_o, ln1_g, ln1_b, w_up, w_ffn_conv, b_ffn_conv, w_down, ln2_g, ln2_b, loss_target, m_meta_tokens, m_hg_lower_bounds, m_w_in, m_w_conv, m_w_pool, m_pool_scale, m_hg_norm_g, m_w_o, m_ln1_g, m_ln1_b, m_w_up, m_w_ffn_conv, m_b_ffn_conv, m_w_down, m_ln2_g, m_ln2_b, v_meta_tokens, v_hg_lower_bounds, v_w_in, v_w_conv, v_w_pool, v_pool_scale, v_hg_norm_g, v_w_o, v_ln1_g, v_ln1_b, v_w_up, v_w_ffn_conv, v_b_ffn_conv, v_w_down, v_ln2_g, v_ln2_b):
    W = dict(meta_tokens=meta_tokens, hg_lower_bounds=hg_lower_bounds, w_in=w_in, w_conv=w_conv, w_pool=w_pool,
             pool_scale=pool_scale, hg_norm_g=hg_norm_g, w_o=w_o, ln1_g=ln1_g, ln1_b=ln1_b, w_up=w_up,
             w_ffn_conv=w_ffn_conv, b_ffn_conv=b_ffn_conv, w_down=w_down, ln2_g=ln2_g, ln2_b=ln2_b)
    M = dict(meta_tokens=m_meta_tokens, hg_lower_bounds=m_hg_lower_bounds, w_in=m_w_in, w_conv=m_w_conv,
             w_pool=m_w_pool, pool_scale=m_pool_scale, hg_norm_g=m_hg_norm_g, w_o=m_w_o, ln1_g=m_ln1_g,
             ln1_b=m_ln1_b, w_up=m_w_up, w_ffn_conv=m_w_ffn_conv, b_ffn_conv=m_b_ffn_conv, w_down=m_w_down,
             ln2_g=m_ln2_g, ln2_b=m_ln2_b)
    V = dict(meta_tokens=v_meta_tokens, hg_lower_bounds=v_hg_lower_bounds, w_in=v_w_in, w_conv=v_w_conv,
             w_pool=v_w_pool, pool_scale=v_pool_scale, hg_norm_g=v_hg_norm_g, w_o=v_w_o, ln1_g=v_ln1_g,
             ln1_b=v_ln1_b, w_up=v_w_up, w_ffn_conv=v_w_ffn_conv, b_ffn_conv=v_b_ffn_conv, w_down=v_w_down,
             ln2_g=v_ln2_g, ln2_b=v_ln2_b)
    assert x.shape[0] == 1 and x.shape[2] == D_MODEL and w_in.shape[0] == DEPTH
    seq = x.shape[1]
    L = -(-(seq + N_META) // ROW_ALIGN) * ROW_ALIGN

    small_pack, small_offs = _pack([W[n] for n in SMALL_SHARDED], F32)
    shards = {n: [W[n][l].astype(BF16) for l in range(DEPTH)] for n in BIG}
    g_in0, g_o0, small_all = _gather_many([shards["w_in"][0], shards["w_o"][0], small_pack], "gather_weights")
    full = {}
    for n, a in zip(SMALL_SHARDED, _unpack(small_all, small_offs, [W[n].shape for n in SMALL_SHARDED], (N_DEV,))):
        full[n] = _cols_from_shards(a, 1)
    order = (small_all[0, 0, 0] * 0.0).astype(BF16)
    ffn0_started = _split_start([shards["w_up"][0] + order, shards["w_down"][0] + order], False, "gather_ffn0_start")
    order = ffn0_started[4][0, 0].astype(BF16)
    layer1_started = _split_start([shards[n][1] + order for n in BIG], False, "gather_layer1_start")
    lb_in = hg_lower_bounds + layer1_started[4][0, 0]

    pad_rows = L - N_META - seq
    xp = jnp.concatenate([full["meta_tokens"], x[0], jnp.zeros((pad_rows, D_MODEL), F32)], axis=0)
    tgt = jnp.concatenate([jnp.zeros((N_META, D_MODEL), F32), loss_target[0], jnp.zeros((pad_rows, D_MODEL), F32)], axis=0)

    saved = []
    h_in, h_in_b = xp, xp.astype(BF16)
    for l in range(DEPTH):
        if l == 0:
            lw = _mixer_weights(g_in0, g_o0)
        else:
            g_in, g_o, g_up, g_down = _split_wait(layer1_started, False, h_in_b, "gather_layer1_wait")
            lw = {**_mixer_weights(g_in, g_o), **_ffn_weights(g_up, g_down)}
        wc = full["w_conv"][l].T
        wblk = _pool_blockdiag(w_pool[l]).astype(BF16)
        ps = pool_scale[l][None, :]
        gn = hg_norm_g[l][None, :]
        wf = full["w_ffn_conv"][l].T
        bf = b_ffn_conv[l][None, :]
        hh = _matmul(h_in_b, lw["w_hg"], "nn", F32, f"fwd_hg_{l}")
        hc = _matmul(h_in_b, lw["w_cp"], "nn", F32, f"fwd_cp_{l}")
        y_hg, o_raw, states, amat = _hgrn_fwd(hh, lb_in if l == 0 else hg_lower_bounds, gn, l, f"hgrn_fwd_{l}")
        y_cp = _cp_fwd(hc, wc, wblk, ps, f"convpool_fwd_{l}")
        cat = jnp.concatenate([y_hg, y_cp], axis=1)
        z1, x1, x1_b = _matmul_ln(cat, lw["w_o"], h_in, ln1_g[l][None, :], ln1_b[l][None, :], f"fwd_o_ln1_{l}")
        if l == 0:
            lw.update(_ffn_weights(*_split_wait(ffn0_started, False, x1_b, "gather_ffn0_wait")))
        up = _matmul(x1_b, lw["w_up"], "nn", BF16, f"fwd_up_{l}")
        a, u = _ffn_act_fwd(up, wf, bf, f"ffn_fwd_{l}")
        saved.append(dict(lw=lw, wc=wc, wblk=wblk, ps=ps, gn=gn, wf=wf, bf=bf, x_b=h_in_b, hh=hh, hc=hc,
                          o_raw=o_raw, states=states, amat=amat, cat=cat, z1=z1, x1_b=x1_b, up=up, u=u, a=a))
        if l < DEPTH - 1:
            saved[l]["z2"], h_in, h_in_b = _matmul_ln(a, lw["w_down"], x1, ln2_g[l][None, :], ln2_b[l][None, :],
                                                     f"fwd_down_ln2_{l}")
        else:
            saved[l]["z2"], dy, loss_part = _matmul_ln(a, lw["w_down"], x1, ln2_g[l][None, :], ln2_b[l][None, :],
                                                       f"fwd_down_ln2_loss_{l}", loss=(tgt, seq))

    loss = lax.psum(loss_part[0, 0], ("x", "y", "c"))

    G = {}
    per_layer = {n: [None] * DEPTH for n in ("w_conv", "w_pool", "pool_scale", "hg_norm_g", "ln1_g", "ln1_b",
                                             "w_ffn_conv", "b_ffn_conv", "ln2_g", "ln2_b")}
    ffn_started, mix_started = [None] * DEPTH, [None] * DEPTH
    order = jnp.zeros((), F32)
    dlb_total = jnp.zeros((DEPTH, HG_W), F32)
    for l in reversed(range(DEPTH)):
        s = saved[l]
        lw = s["lw"]
        dz2_b, dg2, db2 = _ln_bwd(s["z2"], dy, ln2_g[l][None, :] + order, f"ln2_bwd_{l}")
        da = _matmul(dz2_b, lw["w_down"], "nt", BF16, f"bwd_da_{l}")
        d_w_down = _matmul(s["a"], dz2_b, "tn", BF16, f"wgrad_down_{l}")
        dup, dwf, dbf = _ffn_act_bwd(s["up"], s["u"], da, s["wf"], f"ffn_bwd_{l}")
        dx1 = _matmul(dup, lw["w_up"], "nt", BF16, f"bwd_dx1_{l}", res=dz2_b, alpha=ALPHA)
        d_w_up = _matmul(s["x1_b"], dup, "tn", BF16, f"wgrad_up_{l}")
        ffn_started[l] = _split_start([jnp.transpose(d_w_up.reshape(D_MODEL, N_DEV, -1), (1, 0, 2)),
                                       d_w_down.reshape(N_DEV, -1, D_MODEL)], True, f"scatter_ffn{l}_start")
        order = ffn_started[l][4][0, 0]
        dz1_b, dg1, db1 = _ln_bwd(s["z1"], dx1, ln1_g[l][None, :] + order, f"ln1_bwd_{l}")
        dcat = _matmul(dz1_b, lw["w_o"], "nt", BF16, f"bwd_dcat_{l}")
        d_w_o = _matmul(s["cat"], dz1_b, "tn", BF16, f"wgrad_o_{l}")
        dhh, dlb, dgn = _hgrn_bwd(s["hh"], s["o_raw"], s["states"], s["amat"], dcat, hg_lower_bounds, s["gn"], l,
                                  f"hgrn_bwd_{l}")
        dhc, dwc, dwblk, dps = _cp_bwd(s["hc"], dcat, s["wc"], s["wblk"], s["ps"], f"convpool_bwd_{l}")
        d_w_hg = _matmul(s["x_b"], dhh, "tn", BF16, f"wgrad_hg_{l}")
        d_w_cp = _matmul(s["x_b"], dhc, "tn", BF16, f"wgrad_cp_{l}")
        d_w_in = jnp.concatenate([d_w_cp[:, 0:768], d_w_hg, d_w_cp[:, 768:1024]], axis=1)
        mix_chunks = [jnp.transpose(d_w_in.reshape(D_MODEL, N_DEV, -1), (1, 0, 2)),
                      jnp.concatenate([d_w_o[512:768], d_w_o[0:512], d_w_o[768:1024]], axis=0).reshape(N_DEV, -1, D_MODEL)]
        mix_started[l] = _split_start(mix_chunks, True, f"scatter_mix{l}_start")
        order = mix_started[l][4][0, 0]
        dx_a = _matmul(dhh, lw["w_hg"] + order.astype(BF16), "nt", F32, f"bwd_dx_hg_{l}", res=dz1_b, alpha=ALPHA)
        dx = _matmul(dhc, lw["w_cp"], "nt", F32, f"bwd_dx_cp_{l}", res=dx_a, alpha=1.0)
        per_layer["w_conv"][l] = dwc.T
        per_layer["w_ffn_conv"][l] = dwf.T
        per_layer["b_ffn_conv"][l] = dbf[0]
        per_layer["w_pool"][l] = jnp.stack([dwblk[g * 64:(g + 1) * 64, g * 64:(g + 1) * 64] for g in range(4)], axis=0)
        per_layer["pool_scale"][l] = dps[0]
        per_layer["hg_norm_g"][l] = dgn[0]
        per_layer["ln1_g"][l], per_layer["ln1_b"][l] = dg1[0], db1[0]
        per_layer["ln2_g"][l], per_layer["ln2_b"][l] = dg2[0], db2[0]
        dlb_total = dlb_total + dlb
        dy = dx
    for n, parts in per_layer.items():
        G[n] = jnp.stack(parts, axis=0)
    G["hg_lower_bounds"] = dlb_total
    grad_x = dy[N_META:N_META + seq][None]

    def shard_major(g, lead):
        g = g.reshape(g.shape[:lead] + (N_DEV, -1) + g.shape[lead + 1:])
        g = jnp.moveaxis(g, lead, 0).reshape(N_DEV, -1)
        nrow = _pack_rows(g.shape[1])
        return jnp.pad(g, ((0, 0), (0, nrow * LANES - g.shape[1]))).reshape(N_DEV, nrow, LANES)

    small_chunks = jnp.concatenate([shard_major(dy[0:N_META], 1), shard_major(G["w_conv"], 1),
                                    shard_major(G["w_ffn_conv"], 1)], axis=1)
    w_small, _ = _pack([W[n] for n in SMALL_SHARDED], F32)
    rep_pack, rep_offs = _pack([G[n] for n in REPLICATED], F32)
    small_recv, rep_all = _exchange_grads([], small_chunks, rep_pack, "exchange_grads")
    parts = {n: [] for n in BIG}
    for l in range(DEPTH):
        up_l, down_l = _split_wait(ffn_started[l], True, rep_all, f"scatter_ffn{l}_wait")
        in_l, o_l = _split_wait(mix_started[l], True, rep_all, f"scatter_mix{l}_wait")
        for n, a in zip(BIG, (in_l, o_l, up_l, down_l)):
            parts[n].append(a)

    res = {k: {} for k in ("grad", "delta", "new_m", "new_v")}
    kinds = ("grad", "delta", "new_m", "new_v")
    for n in BIG:
        for kind, a in zip(kinds, _adamw_layers(parts[n], W[n], M[n], V[n], f"adamw_{n}")):
            res[kind][n] = a
    m_small, _ = _pack([M[n] for n in SMALL_SHARDED], F32)
    v_small, _ = _pack([V[n] for n in SMALL_SHARDED], F32)
    outs_small = _adamw(small_recv, w_small, m_small, v_small, "adamw_small_sharded")
    w_rep, _ = _pack([W[n] for n in REPLICATED], F32)
    m_rep, _ = _pack([M[n] for n in REPLICATED], F32)
    v_rep, _ = _pack([V[n] for n in REPLICATED], F32)
    outs_rep = _adamw(rep_all, w_rep, m_rep, v_rep, "adamw_replicated")
    for kind, b_sm, b_rep in zip(kinds, outs_small, outs_rep):
        for n, a in zip(SMALL_SHARDED, _unpack(b_sm, small_offs, [W[n].shape for n in SMALL_SHARDED])):
            res[kind][n] = a
        for n, a in zip(REPLICATED, _unpack(b_rep, rep_offs, [W[n].shape for n in REPLICATED])):
            res[kind][n] = a

    return (loss, grad_x, *[res["grad"][n] for n in WEIGHTS], *[res["delta"][n] for n in WEIGHTS],
            *[res["new_m"][n] for n in WEIGHTS], *[res["new_v"][n] for n in WEIGHTS])
```

```python
import jax
import jax.numpy as jnp
from jax import lax
from jax.experimental import pallas as pl
from jax.experimental.pallas import tpu as pltpu

F32 = jnp.float32
BF16 = jnp.bfloat16

N_DEV = 8
D_MODEL = 1024
N_META = 16
DEPTH = 2
CONV_W = 256
HG_W = 512
HG_D = 128
HG_HEADS = 4
POOL_W = 256
POOL_GROUP = 64
D_FF = 2816
ALPHA = (2 * DEPTH) ** 0.25
LN_EPS = 1e-5
RMS_EPS = 1e-6
F_FLOOR = 1e-30
Q_SCALE = HG_D ** -0.5
SUB = 16
SEQ_TILE = 192
FFN_TILE = 96
ROW_ALIGN = 192
LANES = 128
VMEM_LIMIT = 48 * 1024 * 1024
MATMUL_VMEM_BUDGET = 38 * 1024 * 1024

ADAM_LR = 0.001
ADAM_B1 = 0.9
ADAM_B2 = 0.999
ADAM_EPS = 1e-08
ADAM_WD = 0.01
ADAM_STEP = 10


def _tile(n, cap, mult):
    best = 0
    for t in range(mult, min(n, cap) + 1, mult):
        if n % t == 0:
            best = t
    assert best > 0, (n, cap, mult)
    return best


def _params(sem, vmem=VMEM_LIMIT):
    return pltpu.CompilerParams(dimension_semantics=sem, vmem_limit_bytes=vmem)


def _dnt(a, b):
    return lax.dot_general(a, b, (((1,), (1,)), ((), ())), preferred_element_type=F32)


def _dtn(a, b):
    return lax.dot_general(a, b, (((0,), (0,)), ((), ())), preferred_element_type=F32)


def _dnn(a, b):
    return jnp.dot(a, b, preferred_element_type=F32)


def _sigmoid(x):
    return 1.0 / (1.0 + jnp.exp(-x))


def _matmul(a, b, mode, out_dtype, name, res=None, alpha=1.0):
    if mode == "tn":
        K, M = a.shape
    else:
        M, K = a.shape
    N = b.shape[0] if mode == "nt" else b.shape[1]
    out_bytes = jnp.dtype(out_dtype).itemsize
    if mode == "tn":
        tk, tm, tn = K, _tile(M, 512, LANES), _tile(N, 512, LANES)
        nk, use_acc = 1, False
    else:
        tn = _tile(N, 1536, LANES)
        tk = _tile(K, 2816, LANES)
        nk = K // tk
        use_acc = nk > 1 and out_dtype != F32
        tm = M
        for cap in (1536, 768, 384):
            tm = _tile(M, cap, 16)
            blocks = 2 * (a.dtype.itemsize * tm * tk + b.dtype.itemsize * tn * tk + out_bytes * tm * tn
                          + (res.dtype.itemsize * tm * tn if res is not None else 0)) + (4 * tm * tn if use_acc else 0)
            if blocks <= MATMUL_VMEM_BUDGET:
                break
    dims = {"nn": ((1,), (0,)), "nt": ((1,), (1,)), "tn": ((0,), (0,))}[mode]

    def body(*refs):
        a_ref, b_ref = refs[0], refs[1]
        r_ref = refs[2] if res is not None else None
        o_ref = refs[3] if res is not None else refs[2]
        acc = refs[-1] if use_acc else o_ref
        k = pl.program_id(2)
        p = lax.dot_general(a_ref[...].astype(BF16), b_ref[...].astype(BF16), (dims, ((), ())),
                            preferred_element_type=F32)

        def finish(r):
            if r_ref is not None:
                r = r + alpha * r_ref[...].astype(F32)
            o_ref[...] = r.astype(out_dtype)

        if nk == 1:
            finish(p)
        else:
            @pl.when(k == 0)
            def _():
                acc[...] = p

            @pl.when((k > 0) & (k < nk - 1))
            def _():
                acc[...] += p

            @pl.when(k == nk - 1)
            def _():
                finish(acc[...] + p)

    if mode == "tn":
        a_spec = pl.BlockSpec((tk, tm), lambda i, j, k: (k, i))
    else:
        a_spec = pl.BlockSpec((tm, tk), lambda i, j, k: (i, k))
    if mode == "nt":
        b_spec = pl.BlockSpec((tn, tk), lambda i, j, k: (j, k))
    else:
        b_spec = pl.BlockSpec((tk, tn), lambda i, j, k: (k, j))
    in_specs = [a_spec, b_spec]
    args = [a, b]
    if res is not None:
        in_specs.append(pl.BlockSpec((tm, tn), lambda i, j, k: (i, j)))
        args.append(res)
    return pl.pallas_call(
        body, name=name,
        grid=(M // tm, N // tn, nk),
        in_specs=in_specs,
        out_specs=pl.BlockSpec((tm, tn), lambda i, j, k: (i, j)),
        out_shape=jax.ShapeDtypeStruct((M, N), out_dtype),
        scratch_shapes=[pltpu.VMEM((tm, tn), F32)] if use_acc else [],
        compiler_params=_params(("parallel", "parallel", "arbitrary")),
    )(*args)


def _matmul_ln(a, w, x, g, b, name, loss=None):
    L, K = a.shape
    D = w.shape[1]
    tr = L
    for cap in (1536, 768, 384):
        tr = _tile(L, cap, 16)
        if 2 * (2 * tr * K + 2 * K * D + 4 * tr * D * (4 if loss else 3) + 2 * tr * D) <= MATMUL_VMEM_BUDGET:
            break

    def body(*refs):
        a_ref, w_ref, x_ref, g_ref, b_ref = refs[:5]
        z = ALPHA * x_ref[...] + _dnn(a_ref[...], w_ref[...])
        mu = jnp.mean(z, axis=-1, keepdims=True)
        zc = z - mu
        var = jnp.mean(zc * zc, axis=-1, keepdims=True)
        y = zc * lax.rsqrt(var + LN_EPS) * g_ref[...] + b_ref[...]
        if loss is None:
            z_ref, y_ref, yb_ref = refs[5:]
            y_ref[...] = y
            yb_ref[...] = y.astype(BF16)
        else:
            t_ref, z_ref, dy_ref, loss_ref = refs[5:]
            i = pl.program_id(0)

            @pl.when(i == 0)
            def _():
                loss_ref[...] = jnp.zeros_like(loss_ref)

            r = i * tr + lax.broadcasted_iota(jnp.int32, (tr, D), 0)
            valid = (r >= N_META) & (r < N_META + loss[1])
            e = jnp.where(valid, y - t_ref[...], 0.0)
            dy_ref[...] = e * (1.0 / D)
            s = jnp.sum(jnp.sum(e * e, axis=-1, keepdims=True), axis=0, keepdims=True)
            loss_ref[...] += (0.5 / D) * s
        z_ref[...] = z

    row = pl.BlockSpec((tr, D), lambda i: (i, 0))
    vec = pl.BlockSpec((1, D), lambda i: (0, 0))
    in_specs = [pl.BlockSpec((tr, K), lambda i: (i, 0)), pl.BlockSpec((K, D), lambda i: (0, 0)), row, vec, vec]
    f32_rows = jax.ShapeDtypeStruct((L, D), F32)
    if loss is None:
        args, out_specs = [a, w, x, g, b], [row, row, row]
        out_shape = [f32_rows, f32_rows, jax.ShapeDtypeStruct((L, D), BF16)]
    else:
        args, in_specs = [a, w, x, g, b, loss[0]], in_specs + [row]
        out_specs = [row, row, pl.BlockSpec((1, 1), lambda i: (0, 0))]
        out_shape = [f32_rows, f32_rows, jax.ShapeDtypeStruct((1, 1), F32)]
    return pl.pallas_call(
        body, name=name, grid=(L // tr,), in_specs=in_specs, out_specs=out_specs, out_shape=out_shape,
        compiler_params=_params(("arbitrary",) if loss else ("parallel",)),
    )(*args)


def _ln_bwd(z, dy, g, name):
    L, D = z.shape
    tr = _tile(L, 768, 16)

    def body(z_ref, dy_ref, g_ref, dzb_ref, dg_ref, db_ref):
        @pl.when(pl.program_id(0) == 0)
        def _():
            dg_ref[...] = jnp.zeros_like(dg_ref)
            db_ref[...] = jnp.zeros_like(db_ref)

        z = z_ref[...]
        mu = jnp.mean(z, axis=-1, keepdims=True)
        zc = z - mu
        var = jnp.mean(zc * zc, axis=-1, keepdims=True)
        rstd = lax.rsqrt(var + LN_EPS)
        xhat = zc * rstd
        dy = dy_ref[...].astype(F32)
        dxh = dy * g_ref[...]
        m1 = jnp.mean(dxh, axis=-1, keepdims=True)
        m2 = jnp.mean(dxh * xhat, axis=-1, keepdims=True)
        dz = rstd * (dxh - m1 - xhat * m2)
        dzb_ref[...] = dz.astype(BF16)
        dg_ref[...] += jnp.sum(dy * xhat, axis=0, keepdims=True)
        db_ref[...] += jnp.sum(dy, axis=0, keepdims=True)

    row = pl.BlockSpec((tr, D), lambda i: (i, 0))
    vec = pl.BlockSpec((1, D), lambda i: (0, 0))
    return pl.pallas_call(
        body, name=name, grid=(L // tr,),
        in_specs=[row, row, vec], out_specs=[row, vec, vec],
        out_shape=[jax.ShapeDtypeStruct((L, D), BF16),
                   jax.ShapeDtypeStruct((1, D), F32), jax.ShapeDtypeStruct((1, D), F32)],
        compiler_params=_params(("arbitrary",)),
    )(z, dy, g)


def _shift_down(x, prev, k):
    T, C = x.shape
    rot = pltpu.roll(jnp.concatenate([prev, x], axis=0).reshape(T // 8 + 1, 8, C), k, 1)
    sub = lax.broadcasted_iota(jnp.int32, (T // 8, 8, C), 1)
    return jnp.where(sub < k, rot[:-1], rot[1:]).reshape(T, C)


def _shift_up(x, nxt, k):
    T, C = x.shape
    rot = pltpu.roll(jnp.concatenate([x, nxt], axis=0).reshape(T // 8 + 1, 8, C), 8 - k, 1)
    sub = lax.broadcasted_iota(jnp.int32, (T // 8, 8, C), 1)
    return jnp.where(sub >= 8 - k, rot[1:], rot[:-1]).reshape(T, C)


def _conv3(x, prev, w, b):
    return w[2:3, :] * x + w[1:2, :] * _shift_down(x, prev, 1) + w[0:1, :] * _shift_down(x, prev, 2) + b


def _ffn_act_fwd(up, w, b, name):
    L, C = up.shape
    F = C // 2
    ts = FFN_TILE
    n = L // ts

    def body(up_ref, pv_ref, w_ref, b_ref, a_ref, u_ref):
        i = pl.program_id(0)
        x = up_ref[...].astype(F32)
        prev = jnp.where(i > 0, pv_ref[...].astype(F32)[8:16], 0.0)
        u = _conv3(x, prev, w_ref[...], b_ref[...])
        u_ref[...] = u.astype(BF16)
        gate = u[:, :F]
        a_ref[...] = (gate * _sigmoid(gate) * u[:, F:]).astype(BF16)

    return pl.pallas_call(
        body, name=name, grid=(n,),
        in_specs=[pl.BlockSpec((ts, C), lambda i: (i, 0)),
                  pl.BlockSpec((16, C), lambda i: (jnp.maximum(i * (ts // 16) - 1, 0), 0)),
                  pl.BlockSpec((3, C), lambda i: (0, 0)), pl.BlockSpec((1, C), lambda i: (0, 0))],
        out_specs=[pl.BlockSpec((ts, F), lambda i: (i, 0)), pl.BlockSpec((ts, C), lambda i: (i, 0))],
        out_shape=[jax.ShapeDtypeStruct((L, F), BF16), jax.ShapeDtypeStruct((L, C), BF16)],
        compiler_params=_params(("parallel",)),
    )(up, up, w, b)


def _ffn_act_bwd(up, u, da, w, name):
    L, C = up.shape
    F = C // 2
    ts = FFN_TILE
    n = L // ts
    last16 = L // 16 - 1

    def du_of(u, da):
        gate, val = u[:, :F], u[:, F:]
        sg = _sigmoid(gate)
        dgate = da * val * (sg * (1.0 + gate * (1.0 - sg)))
        dval = da * (gate * sg)
        return jnp.concatenate([dgate, dval], axis=1)

    def body(up_ref, u_ref, un_ref, da_ref, dan_ref, w_ref, dup_ref, dw_ref, db_ref):
        i = pl.program_id(0)

        @pl.when(i == 0)
        def _():
            dw_ref[...] = jnp.zeros_like(dw_ref)
            db_ref[...] = jnp.zeros_like(db_ref)

        w = w_ref[...]
        x = up_ref[...].astype(F32)
        du = du_of(u_ref[...].astype(F32), da_ref[...].astype(F32))
        dun = jnp.where(i < n - 1, du_of(un_ref[...].astype(F32)[0:8], dan_ref[...].astype(F32)[0:8]), 0.0)
        du1 = _shift_up(du, dun, 1)
        du2 = _shift_up(du, dun, 2)
        dup_ref[...] = (w[2:3, :] * du + w[1:2, :] * du1 + w[0:1, :] * du2).astype(BF16)
        dw_ref[...] += jnp.concatenate([jnp.sum(x * du2, axis=0, keepdims=True),
                                        jnp.sum(x * du1, axis=0, keepdims=True),
                                        jnp.sum(x * du, axis=0, keepdims=True)], axis=0)
        db_ref[...] += jnp.sum(du, axis=0, keepdims=True)

    nxt = lambda i: (jnp.minimum((i + 1) * (ts // 16), last16), 0)
    return pl.pallas_call(
        body, name=name, grid=(n,),
        in_specs=[pl.BlockSpec((ts, C), lambda i: (i, 0)),
                  pl.BlockSpec((ts, C), lambda i: (i, 0)), pl.BlockSpec((16, C), nxt),
                  pl.BlockSpec((ts, F), lambda i: (i, 0)), pl.BlockSpec((16, F), nxt),
                  pl.BlockSpec((3, C), lambda i: (0, 0))],
        out_specs=[pl.BlockSpec((ts, C), lambda i: (i, 0)), pl.BlockSpec((3, C), lambda i: (0, 0)),
                   pl.BlockSpec((1, C), lambda i: (0, 0))],
        out_shape=[jax.ShapeDtypeStruct((L, C), BF16), jax.ShapeDtypeStruct((3, C), F32),
                   jax.ShapeDtypeStruct((1, C), F32)],
        compiler_params=_params(("arbitrary",)),
    )(up, u, u, da, da, w)


def _pool_window(ext, tile_rows, first_row, lead):
    T = ext.shape[0]
    sh = (lambda x, k: pltpu.roll(x, T - k, 0)) if lead else (lambda x, k: pltpu.roll(x, k, 0))
    r2 = ext + sh(ext, 1)
    r4 = r2 + sh(r2, 2)
    r8 = r4 + sh(r4, 4)
    r16 = r8 + sh(r8, 8)
    lo = 0 if lead else 16
    grp = lax.broadcasted_iota(jnp.int32, (tile_rows, POOL_W), 1) // POOL_GROUP
    pick = lambda a, b, c, d: jnp.where(grp == 0, a, jnp.where(grp == 1, b, jnp.where(grp == 2, c, d)))
    win = pick(r2[lo:lo + tile_rows], r4[lo:lo + tile_rows], r8[lo:lo + tile_rows], r16[lo:lo + tile_rows])
    return win, pick(2.0, 4.0, 8.0, 16.0)


def _pool_count(first_row, rows, wlen):
    t1 = (first_row + lax.broadcasted_iota(jnp.int32, (rows, POOL_W), 0) + 1).astype(F32)
    return jnp.minimum(t1, wlen)


def _cp_fwd(hc, wc, wblk, pscale, name):
    L = hc.shape[0]
    ts = SEQ_TILE
    n = L // ts

    def body(h_ref, hp_ref, wc_ref, wb_ref, ps_ref, y_ref):
        i = pl.program_id(0)
        h = h_ref[...]
        hp = jnp.where(i > 0, hp_ref[...], 0.0)
        cb, cc, cv, pv = h[:, 0:256], h[:, 256:512], h[:, 512:768], h[:, 768:1024]
        p = cc * cv
        pp = hp[8:16, 256:512] * hp[8:16, 512:768]
        w = wc_ref[...]
        conv = w[2:3, :] * p + w[1:2, :] * _shift_down(p, pp, 1) + w[0:1, :] * _shift_down(p, pp, 2)
        y_conv = cb * conv
        ext = jnp.concatenate([hp[:, 768:1024], pv], axis=0)
        win, wlen = _pool_window(ext, ts, i * ts, False)
        d = win / _pool_count(i * ts, ts, wlen) - pv
        y_pool = _dnn(d.astype(BF16), wb_ref[...]) * ps_ref[...]
        y_ref[...] = jnp.concatenate([y_conv, y_pool], axis=1).astype(BF16)

    return pl.pallas_call(
        body, name=name, grid=(n,),
        in_specs=[pl.BlockSpec((ts, 1024), lambda i: (i, 0)),
                  pl.BlockSpec((16, 1024), lambda i: (jnp.maximum(i * (ts // 16) - 1, 0), 0)),
                  pl.BlockSpec((3, 256), lambda i: (0, 0)), pl.BlockSpec((256, 256), lambda i: (0, 0)),
                  pl.BlockSpec((1, 256), lambda i: (0, 0))],
        out_specs=pl.BlockSpec((ts, 512), lambda i: (i, 0)),
        out_shape=jax.ShapeDtypeStruct((L, 512), BF16),
        compiler_params=_params(("parallel",)),
    )(hc, hc, wc, wblk, pscale)


def _cp_bwd(hc, dcat, wc, wblk, pscale, name):
    L = hc.shape[0]
    ts = SEQ_TILE
    n = L // ts
    last16 = L // 16 - 1

    def body(h_ref, hp_ref, hn_ref, dy_ref, dyn_ref, wc_ref, wb_ref, ps_ref,
             dh_ref, dwc_ref, dwb_ref, dps_ref):
        i = pl.program_id(0)

        @pl.when(i == 0)
        def _():
            dwc_ref[...] = jnp.zeros_like(dwc_ref)
            dwb_ref[...] = jnp.zeros_like(dwb_ref)
            dps_ref[...] = jnp.zeros_like(dps_ref)

        h = h_ref[...]
        hp = jnp.where(i > 0, hp_ref[...], 0.0)
        hn = hn_ref[...]
        dy = dy_ref[...].astype(F32)
        dyn = jnp.where(i < n - 1, dyn_ref[...].astype(F32), 0.0)
        cb, cc, cv, pv = h[:, 0:256], h[:, 256:512], h[:, 512:768], h[:, 768:1024]
        w = wc_ref[...]
        p = cc * cv
        pp = hp[8:16, 256:512] * hp[8:16, 512:768]
        p1 = _shift_down(p, pp, 1)
        p2 = _shift_down(p, pp, 2)
        conv = w[2:3, :] * p + w[1:2, :] * p1 + w[0:1, :] * p2
        dyc = dy[:, 0:256]
        dcb = dyc * conv
        dconv = dyc * cb
        dconv_n = dyn[0:8, 0:256] * hn[0:8, 0:256]
        dc1 = _shift_up(dconv, dconv_n, 1)
        dc2 = _shift_up(dconv, dconv_n, 2)
        dp = w[2:3, :] * dconv + w[1:2, :] * dc1 + w[0:1, :] * dc2
        dwc_ref[...] += jnp.concatenate([jnp.sum(p * dc2, axis=0, keepdims=True),
                                         jnp.sum(p * dc1, axis=0, keepdims=True),
                                         jnp.sum(p * dconv, axis=0, keepdims=True)], axis=0)
        ps = ps_ref[...]
        wb = wb_ref[...]
        ext = jnp.concatenate([hp[:, 768:1024], pv], axis=0)
        win, wlen = _pool_window(ext, ts, i * ts, False)
        d = win / _pool_count(i * ts, ts, wlen) - pv
        db = d.astype(BF16)
        dyp = dy[:, 256:512]
        dps_ref[...] += jnp.sum(dyp * _dnn(db, wb), axis=0, keepdims=True)
        dypre = (dyp * ps).astype(BF16)
        dwb_ref[...] += _dtn(db, dypre)
        dd = _dnt(dypre, wb)
        ddn = _dnt((dyn[:, 256:512] * ps).astype(BF16), wb)
        e = dd / _pool_count(i * ts, ts, wlen)
        en = ddn / _pool_count((i + 1) * ts, 16, wlen[0:16])
        lead, _ = _pool_window(jnp.concatenate([e, en], axis=0), ts, i * ts, True)
        dpv = lead - dd
        dh_ref[...] = jnp.concatenate([dcb, dp * cv, dp * cc, dpv], axis=1).astype(BF16)

    return pl.pallas_call(
        body, name=name, grid=(n,),
        in_specs=[pl.BlockSpec((ts, 1024), lambda i: (i, 0)),
                  pl.BlockSpec((16, 1024), lambda i: (jnp.maximum(i * (ts // 16) - 1, 0), 0)),
                  pl.BlockSpec((16, 1024), lambda i: (jnp.minimum((i + 1) * (ts // 16), last16), 0)),
                  pl.BlockSpec((ts, 512), lambda i: (i, 1)),
                  pl.BlockSpec((16, 512), lambda i: (jnp.minimum((i + 1) * (ts // 16), last16), 1)),
                  pl.BlockSpec((3, 256), lambda i: (0, 0)), pl.BlockSpec((256, 256), lambda i: (0, 0)),
                  pl.BlockSpec((1, 256), lambda i: (0, 0))],
        out_specs=[pl.BlockSpec((ts, 1024), lambda i: (i, 0)), pl.BlockSpec((3, 256), lambda i: (0, 0)),
                   pl.BlockSpec((256, 256), lambda i: (0, 0)), pl.BlockSpec((1, 256), lambda i: (0, 0))],
        out_shape=[jax.ShapeDtypeStruct((L, 1024), BF16), jax.ShapeDtypeStruct((3, 256), F32),
                   jax.ShapeDtypeStruct((256, 256), F32), jax.ShapeDtypeStruct((1, 256), F32)],
        compiler_params=_params(("arbitrary",)),
    )(hc, hc, hc, dcat, dcat, wc, wblk, pscale)


def _lower_bound(lb_ref, layer):
    b0, b1 = lb_ref[0:1, :], lb_ref[1:2, :]
    m = jnp.maximum(b0, b1)
    e0, e1 = jnp.exp(b0 - m), jnp.exp(b1 - m)
    p0, p1 = e0 / (e0 + e1), e1 / (e0 + e1)
    lb = (p0 - p0) if layer == 0 else ((p0 + p1) - p0)
    return lb, p0, p1


def _cumsum_rows(x, reverse=False):
    row = lax.broadcasted_iota(jnp.int32, x.shape, 0)
    for sh in (1, 2, 4, 8):
        if reverse:
            x = x + jnp.where(row < SUB - sh, pltpu.roll(x, SUB - sh, 0), 0.0)
        else:
            x = x + jnp.where(row >= sh, pltpu.roll(x, sh, 0), 0.0)
    return x


def _gates(fz, lb):
    sig = _sigmoid(fz)
    f = lb + (1.0 - lb) * sig
    g = jnp.log(jnp.maximum(f, F_FLOOR))
    k = (1.0 - lb) * (1.0 - sig)
    return sig, f, g, k


def _head(h):
    return slice(h * HG_D, (h + 1) * HG_D)


def _hgrn_fwd(hh, lbp, gnorm, layer, name):
    L = hh.shape[0]
    ts = SEQ_TILE
    n = L // ts
    nsub = ts // SUB

    def body(q_ref, f_ref, i_ref, g_ref, lb_ref, gn_ref, y_ref, o_ref, s_ref, a_ref, St):
        @pl.when(pl.program_id(0) == 0)
        def _():
            St[...] = jnp.zeros_like(St)

        lb, _, _ = _lower_bound(lb_ref, layer)
        gn = jnp.tile(gn_ref[...], (1, HG_HEADS))
        r16 = lax.broadcasted_iota(jnp.int32, (SUB, SUB), 0)
        c16 = lax.broadcasted_iota(jnp.int32, (SUB, SUB), 1)

        def block(j, carry):
            rows = pl.ds(pl.multiple_of(j * SUB, SUB), SUB)
            q = q_ref[rows, :] * Q_SCALE
            iv = i_ref[rows, :]
            gz = g_ref[rows, :]
            _, _, g, k = _gates(f_ref[rows, :], lb)
            G = _cumsum_rows(g)
            Gl = G[SUB - 1:SUB, :]
            qt = (q * jnp.exp(G)).astype(BF16)
            kd = (k * jnp.exp(Gl - G)).astype(BF16)
            eGl = jnp.exp(Gl)
            ib = iv.astype(BF16)
            A = [jnp.zeros((SUB, SUB), F32) for _ in range(HG_HEADS)]
            for s in range(SUB):
                P = q * jnp.exp(jnp.minimum(G - G[s:s + 1, :], 0.0)) * k[s:s + 1, :]
                for h in range(HG_HEADS):
                    A[h] = jnp.where(c16 == s, jnp.sum(P[:, _head(h)], axis=-1, keepdims=True), A[h])
            outs, ons, amats = [], [], []
            for h in range(HG_HEADS):
                sl = _head(h)
                Sb = St[h].astype(BF16)
                s_ref[j, sl, :] = Sb
                Am = jnp.where(r16 >= c16, A[h], 0.0)
                amats.append(Am)
                o = _dnt(qt[:, sl], Sb) + _dnn(Am.astype(BF16), ib[:, sl])
                St[h] = eGl[:, sl] * St[h] + _dtn(ib[:, sl], kd[:, sl])
                outs.append(o)
                ons.append(o * lax.rsqrt(jnp.mean(o * o, axis=-1, keepdims=True) + RMS_EPS))
            a_ref[rows, :] = jnp.concatenate(amats, axis=1)
            o_ref[rows, :] = jnp.concatenate(outs, axis=1)
            y = jnp.concatenate(ons, axis=1) * gn * (gz * _sigmoid(gz))
            y_ref[rows, :] = y.astype(BF16)
            return carry

        lax.fori_loop(0, nsub, block, 0, unroll=2)

    col = lambda c: pl.BlockSpec((ts, HG_W), lambda i: (i, c))
    return pl.pallas_call(
        body, name=name, grid=(n,),
        in_specs=[col(0), col(1), col(2), col(3), pl.BlockSpec((2, HG_W), lambda i: (0, 0)),
                  pl.BlockSpec((1, HG_D), lambda i: (0, 0))],
        out_specs=[pl.BlockSpec((ts, HG_W), lambda i: (i, 0)), pl.BlockSpec((ts, HG_W), lambda i: (i, 0)),
                   pl.BlockSpec((nsub, HG_W, HG_D), lambda i: (i, 0, 0)),
                   pl.BlockSpec((ts, HG_HEADS * SUB), lambda i: (i, 0))],
        out_shape=[jax.ShapeDtypeStruct((L, HG_W), BF16), jax.ShapeDtypeStruct((L, HG_W), F32),
                   jax.ShapeDtypeStruct((L // SUB, HG_W, HG_D), BF16),
                   jax.ShapeDtypeStruct((L, HG_HEADS * SUB), F32)],
        scratch_shapes=[pltpu.VMEM((HG_HEADS, HG_D, HG_D), F32)],
        compiler_params=_params(("arbitrary",)),
    )(hh, hh, hh, hh, lbp, gnorm)


def _hgrn_bwd(hh, o_raw, states, amat, dcat, lbp, gnorm, layer, name):
    L = hh.shape[0]
    ts = SEQ_TILE
    n = L // ts
    nsub = ts // SUB

    def body(q_ref, f_ref, i_ref, g_ref, o_ref, s_ref, a_ref, dy_ref, lb_ref, gn_ref,
             dh_ref, dlb_ref, dgn_ref, dSt, dlb_acc, S_next):
        step = pl.program_id(0)

        @pl.when(step == 0)
        def _():
            dSt[...] = jnp.zeros_like(dSt)
            S_next[...] = jnp.zeros_like(S_next)
            dlb_acc[...] = jnp.zeros_like(dlb_acc)
            dgn_ref[...] = jnp.zeros_like(dgn_ref)

        lb, p0, p1 = _lower_bound(lb_ref, layer)
        gnh = gn_ref[...]
        gn = jnp.tile(gnh, (1, HG_HEADS))
        r16 = lax.broadcasted_iota(jnp.int32, (SUB, SUB), 0)
        c16 = lax.broadcasted_iota(jnp.int32, (SUB, SUB), 1)

        def block(jj, carry):
            j = nsub - 1 - jj
            rows = pl.ds(pl.multiple_of(j * SUB, SUB), SUB)
            q = q_ref[rows, :] * Q_SCALE
            iv = i_ref[rows, :]
            gz = g_ref[rows, :]
            o = o_ref[rows, :]
            dy = dy_ref[rows, :].astype(F32)
            sig, f, g, k = _gates(f_ref[rows, :], lb)
            G = _cumsum_rows(g)
            Gl = G[SUB - 1:SUB, :]
            eG = jnp.exp(G)
            edl = jnp.exp(Gl - G)
            eGl = jnp.exp(Gl)
            qt = (q * eG).astype(BF16)
            kd = (k * edl).astype(BF16)
            ib = iv.astype(BF16)
            sgz = _sigmoid(gz)
            sil = gz * sgz
            dyn = dy * sil
            on_parts, do_parts = [], []
            dgn = jnp.zeros((1, HG_D), F32)
            for h in range(HG_HEADS):
                sl = _head(h)
                oh = o[:, sl]
                rs = lax.rsqrt(jnp.mean(oh * oh, axis=-1, keepdims=True) + RMS_EPS)
                on = oh * rs
                dgn = dgn + jnp.sum(dyn[:, sl] * on, axis=0, keepdims=True)
                don = dyn[:, sl] * gnh
                do_parts.append(rs * (don - on * jnp.mean(don * on, axis=-1, keepdims=True)))
                on_parts.append(on)
            dgn_ref[...] += dgn
            on_all = jnp.concatenate(on_parts, axis=1)
            dgz = dy * on_all * gn * (sgz * (1.0 + gz * (1.0 - sgz)))
            do = jnp.concatenate(do_parts, axis=1)
            dob = do.astype(BF16)
            amat = a_ref[rows, :]
            dq_p, dk_p, di_p, tail_p = [], [], [], []
            for h in range(HG_HEADS):
                sl = _head(h)
                qh, kh, Gh = q[:, sl], k[:, sl], G[:, sl]
                Ap = jnp.where(r16 >= c16, _dnt(dob[:, sl], ib[:, sl]), 0.0)
                ApT = jnp.where(r16 <= c16, _dnt(ib[:, sl], dob[:, sl]), 0.0)
                dqh = jnp.zeros((SUB, HG_D), F32)
                dkh = jnp.zeros((SUB, HG_D), F32)
                for s in range(SUB):
                    dGs = Gh - Gh[s:s + 1, :]
                    e = jnp.exp(jnp.minimum(dGs, -dGs))
                    dqh = dqh + Ap[:, s:s + 1] * (e * kh[s:s + 1, :])
                    dkh = dkh + ApT[:, s:s + 1] * (e * qh[s:s + 1, :])
                Sb = s_ref[j, sl, :]
                dSb = dSt[h].astype(BF16)
                Am = amat[:, h * SUB:(h + 1) * SUB].astype(BF16)
                dq_p.append(dqh + eG[:, sl] * _dnn(dob[:, sl], Sb))
                dk_p.append(dkh + edl[:, sl] * _dnn(ib[:, sl], dSb))
                di_p.append(_dtn(Am, dob[:, sl]) + _dnt(kd[:, sl], dSb))
                tail_p.append(jnp.sum(dSt[h] * S_next[h].astype(F32), axis=0, keepdims=True))
                S_next[h] = Sb
                dSt[h] = eGl[:, sl] * dSt[h] + _dtn(dob[:, sl], qt[:, sl])
            dq = jnp.concatenate(dq_p, axis=1)
            dk = jnp.concatenate(dk_p, axis=1)
            di = jnp.concatenate(di_p, axis=1)
            dg = _cumsum_rows(q * dq - k * dk, reverse=True) + jnp.concatenate(tail_p, axis=1)
            df = jnp.where(f > F_FLOOR, dg / f, 0.0)
            dfk = df - dk
            dfz = (1.0 - lb) * dfk * sig * (1.0 - sig)
            dlb_acc[...] += jnp.sum(dfk * (1.0 - sig), axis=0, keepdims=True)
            dh_ref[rows, :] = jnp.concatenate([dq * Q_SCALE, dfz, di, dgz], axis=1).astype(BF16)
            return carry

        lax.fori_loop(0, nsub, block, 0)

        @pl.when(step == n - 1)
        def _():
            if layer == 0:
                dlb_ref[...] = jnp.zeros_like(dlb_ref)
            else:
                dz1 = p0 * p1 * dlb_acc[...]
                dlb_ref[...] = jnp.concatenate([-dz1, dz1], axis=0)

    rev = lambda i: n - 1 - i
    col = lambda c: pl.BlockSpec((ts, HG_W), lambda i: (rev(i), c))
    return pl.pallas_call(
        body, name=name, grid=(n,),
        in_specs=[col(0), col(1), col(2), col(3), col(0),
                  pl.BlockSpec((nsub, HG_W, HG_D), lambda i: (rev(i), 0, 0)),
                  pl.BlockSpec((ts, HG_HEADS * SUB), lambda i: (rev(i), 0)), col(0),
                  pl.BlockSpec((2, HG_W), lambda i: (0, 0)), pl.BlockSpec((1, HG_D), lambda i: (0, 0))],
        out_specs=[pl.BlockSpec((ts, 4 * HG_W), lambda i: (rev(i), 0)),
                   pl.BlockSpec((2, HG_W), lambda i: (0, 0)), pl.BlockSpec((1, HG_D), lambda i: (0, 0))],
        out_shape=[jax.ShapeDtypeStruct((L, 4 * HG_W), BF16), jax.ShapeDtypeStruct((2, HG_W), F32),
                   jax.ShapeDtypeStruct((1, HG_D), F32)],
        scratch_shapes=[pltpu.VMEM((HG_HEADS, HG_D, HG_D), F32), pltpu.VMEM((1, HG_W), F32),
                        pltpu.VMEM((HG_HEADS, HG_D, HG_D), BF16)],
        compiler_params=_params(("arbitrary",)),
    )(hh, hh, hh, hh, o_raw, states, amat, dcat, lbp, gnorm)


def _adamw_body(gp_ref, w_ref, m_ref, v_ref, g_ref, d_ref, mo_ref, vo_ref):
    c1 = 1.0 - ADAM_B1 ** ADAM_STEP
    c2 = 1.0 - ADAM_B2 ** ADAM_STEP
    g = gp_ref[0].astype(F32)
    for k in range(1, N_DEV):
        g = g + gp_ref[k].astype(F32)
    mn = ADAM_B1 * m_ref[...] + (1.0 - ADAM_B1) * g
    vn = ADAM_B2 * v_ref[...] + (1.0 - ADAM_B2) * (g * g)
    m_hat = mn / c1
    v_hat = vn / c2
    g_ref[...] = g
    d_ref[...] = -ADAM_LR * (m_hat / (jnp.sqrt(v_hat) + ADAM_EPS) + ADAM_WD * w_ref[...])
    mo_ref[...] = mn
    vo_ref[...] = vn


def _adamw_layers(gparts, w, m, v, name):
    depth, R, C = w.shape
    tr = _tile(R, 256, 16)
    nr = R // tr

    def body(*refs):
        layer = pl.program_id(0)
        for d in range(depth):
            @pl.when(layer == d)
            def _(d=d):
                _adamw_body(refs[d], *refs[depth:])

    def parts_spec(d):
        return pl.BlockSpec((N_DEV, tr, C),
                            lambda l, i: (0, jnp.where(l == d, i, jnp.where(l < d, 0, nr - 1)), 0))

    blk = pl.BlockSpec((None, tr, C), lambda l, i: (l, i, 0))
    shp = jax.ShapeDtypeStruct((depth, R, C), F32)
    return pl.pallas_call(
        body, name=name, grid=(depth, nr),
        in_specs=[parts_spec(d) for d in range(depth)] + [blk, blk, blk],
        out_specs=[blk, blk, blk, blk], out_shape=[shp, shp, shp, shp],
        compiler_params=_params(("arbitrary", "arbitrary")),
    )(*gparts, w, m, v)


def _adamw(gparts, w, m, v, name):
    R = w.shape[0]
    tr = _tile(R, 1024, 16) if R % 16 == 0 else R

    def body(*refs):
        _adamw_body(*refs)

    row = pl.BlockSpec((tr, LANES), lambda i: (i, 0))
    shp = jax.ShapeDtypeStruct((R, LANES), F32)
    return pl.pallas_call(
        body, name=name, grid=(R // tr,),
        in_specs=[pl.BlockSpec((N_DEV, tr, LANES), lambda i: (0, i, 0)), row, row, row],
        out_specs=[row, row, row, row], out_shape=[shp, shp, shp, shp],
        compiler_params=_params(("parallel",)),
    )(gparts, w, m, v)


def _flip(coord, bit):
    return 1 - coord if bit else coord


def _gather_many(blocks, name):
    n = len(blocks)

    def body(*refs):
        x_refs, out_refs = refs[:n], refs[n:2 * n]
        send_sems, recv_sems, local_sems = refs[2 * n:]
        x, y, c = lax.axis_index("x"), lax.axis_index("y"), lax.axis_index("c")
        me, sibling = (x, y, c), (x, y, 1 - c)
        chips = [(1 - x, y), (x, 1 - y), (1 - x, 1 - y)]

        def slot(a, px, py, pc):
            return out_refs[a].at[4 * px + 2 * py + pc]

        def copy(a, k, blk, to, src=None):
            return pltpu.make_async_remote_copy(
                src_ref=slot(a, *blk) if src is None else src, dst_ref=slot(a, *blk),
                send_sem=send_sems.at[7 * a + k], recv_sem=recv_sems.at[7 * a + k],
                device_id=to, device_id_type=pl.DeviceIdType.MESH)

        mine = [pltpu.make_async_copy(x_refs[a], slot(a, *me), local_sems.at[a]) for a in range(n)]
        for cp in mine:
            cp.start()
        first = [copy(a, 0, me, sibling, src=x_refs[a]) for a in range(n)]
        for j, chip in enumerate(chips):
            first += [copy(a, 1 + j, me, (*chip, c), src=x_refs[a]) for a in range(n)]
        for cp in first:
            cp.start()
        passed = []
        for j, chip in enumerate(chips):
            for a in range(n):
                copy(a, 1 + j, (*chip, c), me).wait_recv()
                fwd = copy(a, 4 + j, (*chip, c), sibling)
                fwd.start()
                passed.append(fwd)
        for a in range(n):
            copy(a, 0, sibling, me).wait_recv()
        for j, chip in enumerate(chips):
            for a in range(n):
                copy(a, 4 + j, (*chip, 1 - c), me).wait_recv()
        for cp in first + passed:
            cp.wait_send()
        for cp in mine:
            cp.wait()

    hbm = pl.BlockSpec(memory_space=pl.ANY)
    return pl.pallas_call(
        body, name=name,
        out_shape=[jax.ShapeDtypeStruct((N_DEV,) + b.shape, b.dtype) for b in blocks],
        in_specs=[hbm] * n, out_specs=[hbm] * n,
        scratch_shapes=[pltpu.SemaphoreType.DMA((7 * n,)), pltpu.SemaphoreType.DMA((7 * n,)),
                        pltpu.SemaphoreType.DMA((n,))],
    )(*blocks)


def _split_start(blocks, chunked, name):
    n = len(blocks)
    lands = [lax.empty(b.shape if chunked else (N_DEV,) + b.shape, b.dtype) for b in blocks]

    def body(*refs):
        x_refs, land_refs = refs[:n], refs[n:2 * n]
        send_sems, recv_sems, token = refs[2 * n], refs[2 * n + 1], refs[-1]
        x, y, c = lax.axis_index("x"), lax.axis_index("y"), lax.axis_index("c")
        me = 4 * x + 2 * y + c
        for a in range(n):
            for k in range(1, N_DEV):
                px, py, pc = _flip(x, k & 4), _flip(y, k & 2), _flip(c, k & 1)
                pltpu.make_async_remote_copy(
                    src_ref=x_refs[a].at[4 * px + 2 * py + pc] if chunked else x_refs[a],
                    dst_ref=land_refs[a].at[me],
                    send_sem=send_sems.at[7 * a + k - 1], recv_sem=recv_sems.at[7 * a + k - 1],
                    device_id=(px, py, pc), device_id_type=pl.DeviceIdType.MESH).start()
        token[...] = jnp.zeros_like(token)

    hbm = pl.BlockSpec(memory_space=pltpu.HBM)
    sem = pl.BlockSpec(memory_space=pltpu.SEMAPHORE)
    outs = pl.pallas_call(
        body, name=name,
        out_shape=(pltpu.SemaphoreType.DMA((7 * n,)), pltpu.SemaphoreType.DMA((7 * n,)),
                   *[pltpu.HBM(b.shape, b.dtype) for b in blocks], *[pltpu.HBM(l.shape, l.dtype) for l in lands],
                   jax.ShapeDtypeStruct((8, LANES), F32)),
        in_specs=[hbm] * (2 * n),
        out_specs=(sem, sem, *[hbm] * (2 * n), pl.BlockSpec(memory_space=pltpu.VMEM)),
        input_output_aliases={i: 2 + i for i in range(2 * n)},
        compiler_params=pltpu.CompilerParams(has_side_effects=pltpu.SideEffectType.DATAFLOW_SIDE_EFFECTING),
    )(*[pltpu.with_memory_space_constraint(b, pltpu.HBM) for b in blocks],
      *[pltpu.with_memory_space_constraint(l, pltpu.HBM) for l in lands])
    return outs[0], outs[1], list(outs[2:2 + n]), list(outs[2 + n:2 + 2 * n]), outs[-1]


def _split_wait(started, chunked, after, name):
    send_sems, recv_sems, blocks, lands, _ = started
    n = len(blocks)

    def body(*refs):
        x_refs, land_refs = refs[:n], refs[n:2 * n]
        send_sems, recv_sems = refs[2 * n], refs[2 * n + 1]
        x, y, c = lax.axis_index("x"), lax.axis_index("y"), lax.axis_index("c")
        for a in range(n):
            for k in range(1, N_DEV):
                px, py, pc = _flip(x, k & 4), _flip(y, k & 2), _flip(c, k & 1)
                copy = pltpu.make_async_remote_copy(
                    src_ref=x_refs[a].at[4 * px + 2 * py + pc] if chunked else x_refs[a],
                    dst_ref=land_refs[a].at[4 * px + 2 * py + pc],
                    send_sem=send_sems.at[7 * a + k - 1], recv_sem=recv_sems.at[7 * a + k - 1],
                    device_id=(px, py, pc), device_id_type=pl.DeviceIdType.MESH)
                copy.wait_send()
                copy.wait_recv()

    hbm = pl.BlockSpec(memory_space=pltpu.HBM)
    sem = pl.BlockSpec(memory_space=pltpu.SEMAPHORE)
    outs = pl.pallas_call(
        body, name=name,
        out_shape=(*[pltpu.HBM(b.shape, b.dtype) for b in blocks], *[pltpu.HBM(l.shape, l.dtype) for l in lands]),
        in_specs=[hbm] * (2 * n) + [sem, sem, pl.BlockSpec(memory_space=pl.ANY)],
        out_specs=[hbm] * (2 * n),
        input_output_aliases={i: i for i in range(2 * n)},
        compiler_params=pltpu.CompilerParams(has_side_effects=pltpu.SideEffectType.DATAFLOW_SIDE_EFFECTING),
    )(*blocks, *lands, send_sems, recv_sems, after)
    me = 4 * lax.axis_index("x") + 2 * lax.axis_index("y") + lax.axis_index("c")
    own = [lax.dynamic_index_in_dim(b, me, 0, keepdims=False) if chunked else b for b in outs[:n]]
    return [lax.dynamic_update_index_in_dim(z, o, me, 0) for z, o in zip(outs[n:], own)]


def _exchange_grads(layer_chunks, small_chunks, rep_block, name):
    flows, inputs = [], []
    for p, per_layer in enumerate(layer_chunks):
        for l, arr in enumerate(per_layer):
            flows.append(("param", p, l))
            inputs.append(arr)
    flows += [("small",), ("rep",)]
    inputs += [small_chunks, rep_block]
    n_par = len(layer_chunks)
    n_in, n_out, nf = len(inputs), n_par + 2, len(flows)

    def body(*refs):
        in_refs, out_refs = refs[:n_in], refs[n_in:n_in + n_out]
        send_sems, recv_sems, local_sems = refs[n_in + n_out:]
        x, y, c = lax.axis_index("x"), lax.axis_index("y"), lax.axis_index("c")
        me = 4 * x + 2 * y + c

        def src(f, dev):
            return in_refs[f] if flows[f][0] == "rep" else in_refs[f].at[dev]

        def dst(f, dev):
            if flows[f][0] == "param":
                _, p, l = flows[f]
                return out_refs[p].at[dev, l]
            return out_refs[n_par + (0 if flows[f][0] == "small" else 1)].at[dev]

        mine = [pltpu.make_async_copy(src(f, me), dst(f, me), local_sems.at[f]) for f in range(nf)]
        for cp in mine:
            cp.start()
        copies = []
        for k in range(1, N_DEV):
            px, py, pc = _flip(x, k & 4), _flip(y, k & 2), _flip(c, k & 1)
            peer = 4 * px + 2 * py + pc
            for f in range(nf):
                sems = dict(send_sem=send_sems.at[7 * f + k - 1], recv_sem=recv_sems.at[7 * f + k - 1],
                            device_id=(px, py, pc), device_id_type=pl.DeviceIdType.MESH)
                send = pltpu.make_async_remote_copy(src_ref=src(f, peer), dst_ref=dst(f, me), **sems)
                recv = pltpu.make_async_remote_copy(src_ref=src(f, peer), dst_ref=dst(f, peer), **sems)
                send.start()
                copies.append((send, recv))
        for send, recv in copies:
            recv.wait_recv()
        for send, recv in copies:
            send.wait_send()
        for cp in mine:
            cp.wait()

    out_shape = [jax.ShapeDtypeStruct((N_DEV, len(pl_)) + pl_[0].shape[1:], pl_[0].dtype) for pl_ in layer_chunks]
    out_shape += [jax.ShapeDtypeStruct(small_chunks.shape, small_chunks.dtype),
                  jax.ShapeDtypeStruct((N_DEV,) + rep_block.shape, rep_block.dtype)]
    hbm = pl.BlockSpec(memory_space=pl.ANY)
    return pl.pallas_call(
        body, name=name, out_shape=out_shape,
        in_specs=[hbm] * n_in, out_specs=[hbm] * n_out,
        scratch_shapes=[pltpu.SemaphoreType.DMA((7 * nf,)), pltpu.SemaphoreType.DMA((7 * nf,)),
                        pltpu.SemaphoreType.DMA((nf,))],
    )(*inputs)


def _pack_rows(size):
    return -(-size // (8 * LANES)) * 8


def _pack(arrs, dtype):
    parts, offs, r = [], [], 0
    for a in arrs:
        flat = a.astype(dtype).reshape(-1)
        nrow = _pack_rows(flat.shape[0])
        flat = jnp.pad(flat, (0, nrow * LANES - flat.shape[0]))
        parts.append(flat.reshape(nrow, LANES))
        offs.append((r, nrow))
        r += nrow
    return jnp.concatenate(parts, axis=0), offs


def _unpack(buf, offs, shapes, lead=()):
    outs = []
    for (r, nrow), shp in zip(offs, shapes):
        size = 1
        for s in shp:
            size *= s
        flat = buf[..., r:r + nrow, :].reshape(lead + (nrow * LANES,))
        outs.append(flat[..., :size].reshape(lead + tuple(shp)))
    return outs


def _cols_from_shards(g, axis):
    return jnp.concatenate([g[j] for j in range(N_DEV)], axis=axis)


BIG = ("w_in", "w_o", "w_up", "w_down")
SMALL_SHARDED = ("meta_tokens", "w_conv", "w_ffn_conv")
REPLICATED = ("hg_lower_bounds", "w_pool", "pool_scale", "hg_norm_g", "ln1_g", "ln1_b", "b_ffn_conv", "ln2_g", "ln2_b")
WEIGHTS = ("meta_tokens", "hg_lower_bounds", "w_in", "w_conv", "w_pool", "pool_scale", "hg_norm_g", "w_o",
           "ln1_g", "ln1_b", "w_up", "w_ffn_conv", "b_ffn_conv", "w_down", "ln2_g", "ln2_b")


def _pool_blockdiag(w_pool_l):
    z = jnp.zeros((POOL_GROUP, POOL_GROUP), w_pool_l.dtype)
    rows = [jnp.concatenate([w_pool_l[g] if h == g else z for h in range(4)], axis=1) for g in range(4)]
    return jnp.concatenate(rows, axis=0)


def _mixer_weights(g_in, g_o):
    w_in = jnp.transpose(g_in, (1, 0, 2)).reshape(D_MODEL, -1)
    w_o = g_o.reshape(-1, D_MODEL)
    return dict(
        w_hg=w_in[:, 768:2816],
        w_cp=jnp.concatenate([w_in[:, 0:768], w_in[:, 2816:3072]], axis=1),
        w_o=jnp.concatenate([w_o[256:768], w_o[0:256], w_o[768:1024]], axis=0))


def _ffn_weights(g_up, g_down):
    return dict(w_up=jnp.transpose(g_up, (1, 0, 2)).reshape(D_MODEL, -1), w_down=g_down.reshape(-1, D_MODEL))


def kernel(x, meta_tokens, hg_lower_bounds, w_in, w_conv, w_pool, pool_scale, hg_norm_g, w_o, ln1_g, ln1_b, w_up, w_ffn_conv, b_ffn_conv, w_down, ln2_g, ln2_b, loss_target, m_meta_tokens, m_hg_lower_bounds, m_w_in, m_w_conv, m_w_pool, m_pool_scale, m_hg_norm_g, m_w_o, m_ln1_g, m_ln1_b, m_w_up, m_w_ffn_conv, m_b_ffn_conv, m_w_down, m_ln2_g, m_ln2_b, v_meta_tokens, v_hg_lower_bounds, v_w_in, v_w_conv, v_w_pool, v_pool_scale, v_hg_norm_g, v_w_o, v_ln1_g, v_ln1_b, v_w_up, v_w_ffn_conv, v_b_ffn_conv, v_w_down, v_ln2_g, v_ln2_b):
    W = dict(meta_tokens=meta_tokens, hg_lower_bounds=hg_lower_bounds, w_in=w_in, w_conv=w_conv, w_pool=w_pool,
             pool_scale=pool_scale, hg_norm_g=hg_norm_g, w_o=w_o, ln1_g=ln1_g, ln1_b=ln1_b, w_up=w_up,
             w_ffn_conv=w_ffn_conv, b_ffn_conv=b_ffn_conv, w_down=w_down, ln2_g=ln2_g, ln2_b=ln2_b)
    M = dict(meta_tokens=m_meta_tokens, hg_lower_bounds=m_hg_lower_bounds, w_in=m_w_in, w_conv=m_w_conv,
             w_pool=m_w_pool, pool_scale=m_pool_scale, hg_norm_g=m_hg_norm_g, w_o=m_w_o, ln1_g=m_ln1_g,
             ln1_b=m_ln1_b, w_up=m_w_up, w_ffn_conv=m_w_ffn_conv, b_ffn_conv=m_b_ffn_conv, w_down=m_w_down,
             ln2_g=m_ln2_g, ln2_b=m_ln2_b)
    V = dict(meta_tokens=v_meta_tokens, hg_lower_bounds=v_hg_lower_bounds, w_in=v_w_in, w_conv=v_w_conv,
             w_pool=v_w_pool, pool_scale=v_pool_scale, hg_norm_g=v_hg_norm_g, w_o=v_w_o, ln1_g=v_ln1_g,
             ln1_b=v_ln1_b, w_up=v_w_up, w_ffn_conv=v_w_ffn_conv, b_ffn_conv=v_b_ffn_conv, w_down=v_w_down,
             ln2_g=v_ln2_g, ln2_b=v_ln2_b)
    assert x.shape[0] == 1 and x.shape[2] == D_MODEL and w_in.shape[0] == DEPTH
    seq = x.shape[1]
    L = -(-(seq + N_META) // ROW_ALIGN) * ROW_ALIGN

    small_pack, small_offs = _pack([W[n] for n in SMALL_SHARDED], F32)
    shards = {n: [W[n][l].astype(BF16) for l in range(DEPTH)] for n in BIG}
    g_in0, g_o0, small_all = _gather_many([shards["w_in"][0], shards["w_o"][0], small_pack], "gather_weights")
    full = {}
    for n, a in zip(SMALL_SHARDED, _unpack(small_all, small_offs, [W[n].shape for n in SMALL_SHARDED], (N_DEV,))):
        full[n] = _cols_from_shards(a, 1)
    order = (small_all[0, 0, 0] * 0.0).astype(BF16)
    ffn0_started = _split_start([shards["w_up"][0] + order, shards["w_down"][0] + order], False, "gather_ffn0_start")
    order = ffn0_started[4][0, 0].astype(BF16)
    layer1_started = _split_start([shards[n][1] + order for n in BIG], False, "gather_layer1_start")
    lb_in = hg_lower_bounds + layer1_started[4][0, 0]

    pad_rows = L - N_META - seq
    xp = jnp.concatenate([full["meta_tokens"], x[0], jnp.zeros((pad_rows, D_MODEL), F32)], axis=0)
    tgt = jnp.concatenate([jnp.zeros((N_META, D_MODEL), F32), loss_target[0], jnp.zeros((pad_rows, D_MODEL), F32)], axis=0)

    saved = []
    h_in, h_in_b = xp, xp.astype(BF16)
    for l in range(DEPTH):
        if l == 0:
            lw = _mixer_weights(g_in0, g_o0)
        else:
            g_in, g_o, g_up, g_down = _split_wait(layer1_started, False, h_in_b, "gather_layer1_wait")
            lw = {**_mixer_weights(g_in, g_o), **_ffn_weights(g_up, g_down)}
        wc = full["w_conv"][l].T
        wblk = _pool_blockdiag(w_pool[l]).astype(BF16)
        ps = pool_scale[l][None, :]
        gn = hg_norm_g[l][None, :]
        wf = full["w_ffn_conv"][l].T
        bf = b_ffn_conv[l][None, :]
        hh = _matmul(h_in_b, lw["w_hg"], "nn", F32, f"fwd_hg_{l}")
        hc = _matmul(h_in_b, lw["w_cp"], "nn", F32, f"fwd_cp_{l}")
        y_hg, o_raw, states, amat = _hgrn_fwd(hh, lb_in if l == 0 else hg_lower_bounds, gn, l, f"hgrn_fwd_{l}")
        y_cp = _cp_fwd(hc, wc, wblk, ps, f"convpool_fwd_{l}")
        cat = jnp.concatenate([y_hg, y_cp], axis=1)
        z1, x1, x1_b = _matmul_ln(cat, lw["w_o"], h_in, ln1_g[l][None, :], ln1_b[l][None, :], f"fwd_o_ln1_{l}")
        if l == 0:
            lw.update(_ffn_weights(*_split_wait(ffn0_started, False, x1_b, "gather_ffn0_wait")))
        up = _matmul(x1_b, lw["w_up"], "nn", BF16, f"fwd_up_{l}")
        a, u = _ffn_act_fwd(up, wf, bf, f"ffn_fwd_{l}")
        saved.append(dict(lw=lw, wc=wc, wblk=wblk, ps=ps, gn=gn, wf=wf, bf=bf, x_b=h_in_b, hh=hh, hc=hc,
                          o_raw=o_raw, states=states, amat=amat, cat=cat, z1=z1, x1_b=x1_b, up=up, u=u, a=a))
        if l < DEPTH - 1:
            saved[l]["z2"], h_in, h_in_b = _matmul_ln(a, lw["w_down"], x1, ln2_g[l][None, :], ln2_b[l][None, :],
                                                     f"fwd_down_ln2_{l}")
        else:
            saved[l]["z2"], dy, loss_part = _matmul_ln(a, lw["w_down"], x1, ln2_g[l][None, :], ln2_b[l][None, :],
                                                       f"fwd_down_ln2_loss_{l}", loss=(tgt, seq))

    loss = lax.psum(loss_part[0, 0], ("x", "y", "c"))

    G = {}
    per_layer = {n: [None] * DEPTH for n in ("w_conv", "w_pool", "pool_scale", "hg_norm_g", "ln1_g", "ln1_b",
                                             "w_ffn_conv", "b_ffn_conv", "ln2_g", "ln2_b")}
    ffn_started, mix_started = [None] * DEPTH, [None] * DEPTH
    order = jnp.zeros((), F32)
    dlb_total = jnp.zeros((DEPTH, HG_W), F32)
    for l in reversed(range(DEPTH)):
        s = saved[l]
        lw = s["lw"]
        dz2_b, dg2, db2 = _ln_bwd(s["z2"], dy, ln2_g[l][None, :] + order, f"ln2_bwd_{l}")
        da = _matmul(dz2_b, lw["w_down"], "nt", BF16, f"bwd_da_{l}")
        d_w_down = _matmul(s["a"], dz2_b, "tn", BF16, f"wgrad_down_{l}")
        dup, dwf, dbf = _ffn_act_bwd(s["up"], s["u"], da, s["wf"], f"ffn_bwd_{l}")
        dx1 = _matmul(dup, lw["w_up"], "nt", BF16, f"bwd_dx1_{l}", res=dz2_b, alpha=ALPHA)
        d_w_up = _matmul(s["x1_b"], dup, "tn", BF16, f"wgrad_up_{l}")
        ffn_started[l] = _split_start([jnp.transpose(d_w_up.reshape(D_MODEL, N_DEV, -1), (1, 0, 2)),
                                       d_w_down.reshape(N_DEV, -1, D_MODEL)], True, f"scatter_ffn{l}_start")
        order = ffn_started[l][4][0, 0]
        dz1_b, dg1, db1 = _ln_bwd(s["z1"], dx1, ln1_g[l][None, :] + order, f"ln1_bwd_{l}")
        dcat = _matmul(dz1_b, lw["w_o"], "nt", BF16, f"bwd_dcat_{l}")
        d_w_o = _matmul(s["cat"], dz1_b, "tn", BF16, f"wgrad_o_{l}")
        dhh, dlb, dgn = _hgrn_bwd(s["hh"], s["o_raw"], s["states"], s["amat"], dcat, hg_lower_bounds, s["gn"], l,
                                  f"hgrn_bwd_{l}")
        dhc, dwc, dwblk, dps = _cp_bwd(s["hc"], dcat, s["wc"], s["wblk"], s["ps"], f"convpool_bwd_{l}")
        d_w_hg = _matmul(s["x_b"], dhh, "tn", BF16, f"wgrad_hg_{l}")
        d_w_cp = _matmul(s["x_b"], dhc, "tn", BF16, f"wgrad_cp_{l}")
        d_w_in = jnp.concatenate([d_w_cp[:, 0:768], d_w_hg, d_w_cp[:, 768:1024]], axis=1)
        mix_chunks = [jnp.transpose(d_w_in.reshape(D_MODEL, N_DEV, -1), (1, 0, 2)),
                      jnp.concatenate([d_w_o[512:768], d_w_o[0:512], d_w_o[768:1024]], axis=0).reshape(N_DEV, -1, D_MODEL)]
        mix_started[l] = _split_start(mix_chunks, True, f"scatter_mix{l}_start")
        order = mix_started[l][4][0, 0]
        dx_a = _matmul(dhh, lw["w_hg"] + order.astype(BF16), "nt", F32, f"bwd_dx_hg_{l}", res=dz1_b, alpha=ALPHA)
        dx = _matmul(dhc, lw["w_cp"], "nt", F32, f"bwd_dx_cp_{l}", res=dx_a, alpha=1.0)
        per_layer["w_conv"][l] = dwc.T
        per_layer["w_ffn_conv"][l] = dwf.T
        per_layer["b_ffn_conv"][l] = dbf[0]
        per_layer["w_pool"][l] = jnp.stack([dwblk[g * 64:(g + 1) * 64, g * 64:(g + 1) * 64] for g in range(4)], axis=0)
        per_layer["pool_scale"][l] = dps[0]
        per_layer["hg_norm_g"][l] = dgn[0]
        per_layer["ln1_g"][l], per_layer["ln1_b"][l] = dg1[0], db1[0]
        per_layer["ln2_g"][l], per_layer["ln2_b"][l] = dg2[0], db2[0]
        dlb_total = dlb_total + dlb
        dy = dx
    for n, parts in per_layer.items():
        G[n] = jnp.stack(parts, axis=0)
    G["hg_lower_bounds"] = dlb_total
    grad_x = dy[N_META:N_META + seq][None]

    def shard_major(g, lead):
        g = g.reshape(g.shape[:lead] + (N_DEV, -1) + g.shape[lead + 1:])
        g = jnp.moveaxis(g, lead, 0).reshape(N_DEV, -1)
        nrow = _pack_rows(g.shape[1])
        return jnp.pad(g, ((0, 0), (0, nrow * LANES - g.shape[1]))).reshape(N_DEV, nrow, LANES)

    small_chunks = jnp.concatenate([shard_major(dy[0:N_META], 1), shard_major(G["w_conv"], 1),
                                    shard_major(G["w_ffn_conv"], 1)], axis=1)
    w_small, _ = _pack([W[n] for n in SMALL_SHARDED], F32)
    rep_pack, rep_offs = _pack([G[n] for n in REPLICATED], F32)
    small_recv, rep_all = _exchange_grads([], small_chunks, rep_pack, "exchange_grads")
    parts = {n: [] for n in BIG}
    for l in range(DEPTH):
        up_l, down_l = _split_wait(ffn_started[l], True, rep_all, f"scatter_ffn{l}_wait")
        in_l, o_l = _split_wait(mix_started[l], True, rep_all, f"scatter_mix{l}_wait")
        for n, a in zip(BIG, (in_l, o_l, up_l, down_l)):
            parts[n].append(a)

    res = {k: {} for k in ("grad", "delta", "new_m", "new_v")}
    kinds = ("grad", "delta", "new_m", "new_v")
    for n in BIG:
        for kind, a in zip(kinds, _adamw_layers(parts[n], W[n], M[n], V[n], f"adamw_{n}")):
            res[kind][n] = a
    m_small, _ = _pack([M[n] for n in SMALL_SHARDED], F32)
    v_small, _ = _pack([V[n] for n in SMALL_SHARDED], F32)
    outs_small = _adamw(small_recv, w_small, m_small, v_small, "adamw_small_sharded")
    w_rep, _ = _pack([W[n] for n in REPLICATED], F32)
    m_rep, _ = _pack([M[n] for n in REPLICATED], F32)
    v_rep, _ = _pack([V[n] for n in REPLICATED], F32)
    outs_rep = _adamw(rep_all, w_rep, m_rep, v_rep, "adamw_replicated")
    for kind, b_sm, b_rep in zip(kinds, outs_small, outs_rep):
        for n, a in zip(SMALL_SHARDED, _unpack(b_sm, small_offs, [W[n].shape for n in SMALL_SHARDED])):
            res[kind][n] = a
        for n, a in zip(REPLICATED, _unpack(b_rep, rep_offs, [W[n].shape for n in REPLICATED])):
            res[kind][n] = a

    return (loss, grad_x, *[res["grad"][n] for n in WEIGHTS], *[res["delta"][n] for n in WEIGHTS],
            *[res["new_m"][n] for n in WEIGHTS], *[res["new_v"][n] for n in WEIGHTS])
```

```python
import jax
import jax.numpy as jnp
from jax import lax
from jax.experimental import pallas as pl
from jax.experimental.pallas import tpu as pltpu

F32 = jnp.float32
BF16 = jnp.bfloat16

N_DEV = 8
D_MODEL = 1024
N_META = 16
DEPTH = 2
CONV_W = 256
HG_W = 512
HG_D = 128
HG_HEADS = 4
POOL_W = 256
POOL_GROUP = 64
D_FF = 2816
ALPHA = (2 * DEPTH) ** 0.25
LN_EPS = 1e-5
RMS_EPS = 1e-6
F_FLOOR = 1e-30
Q_SCALE = HG_D ** -0.5
SUB = 16
SEQ_TILE = 192
FFN_TILE = 96
ROW_ALIGN = 192
LANES = 128
VMEM_LIMIT = 48 * 1024 * 1024
MATMUL_VMEM_BUDGET = 38 * 1024 * 1024

ADAM_LR = 0.001
ADAM_B1 = 0.9
ADAM_B2 = 0.999
ADAM_EPS = 1e-08
ADAM_WD = 0.01
ADAM_STEP = 10


def _tile(n, cap, mult):
    best = 0
    for t in range(mult, min(n, cap) + 1, mult):
        if n % t == 0:
            best = t
    assert best > 0, (n, cap, mult)
    return best


def _params(sem, vmem=VMEM_LIMIT):
    return pltpu.CompilerParams(dimension_semantics=sem, vmem_limit_bytes=vmem)


def _dnt(a, b):
    return lax.dot_general(a, b, (((1,), (1,)), ((), ())), preferred_element_type=F32)


def _dtn(a, b):
    return lax.dot_general(a, b, (((0,), (0,)), ((), ())), preferred_element_type=F32)


def _dnn(a, b):
    return jnp.dot(a, b, preferred_element_type=F32)


def _sigmoid(x):
    return 1.0 / (1.0 + jnp.exp(-x))


def _matmul(a, b, mode, out_dtype, name, res=None, alpha=1.0):
    if mode == "tn":
        K, M = a.shape
    else:
        M, K = a.shape
    N = b.shape[0] if mode == "nt" else b.shape[1]
    out_bytes = jnp.dtype(out_dtype).itemsize
    if mode == "tn" and M % 512 == 0 and N % 512 == 0:
        tk, tm, tn = K, 512, 512
        nk, use_acc = 1, False
    else:
        tn = _tile(N, 1536, LANES)
        tk = _tile(K, 1536, 16) if mode == "tn" else _tile(K, 2816, LANES)
        nk = K // tk
        use_acc = nk > 1 and out_dtype != F32
        tm = M
        for cap in (1536, 768, 384):
            tm = _tile(M, cap, 16)
            blocks = 2 * (a.dtype.itemsize * tm * tk + b.dtype.itemsize * tn * tk + out_bytes * tm * tn
                          + (res.dtype.itemsize * tm * tn if res is not None else 0)) + (4 * tm * tn if use_acc else 0)
            if blocks <= MATMUL_VMEM_BUDGET:
                break
    dims = {"nn": ((1,), (0,)), "nt": ((1,), (1,)), "tn": ((0,), (0,))}[mode]

    def body(*refs):
        a_ref, b_ref = refs[0], refs[1]
        r_ref = refs[2] if res is not None else None
        o_ref = refs[3] if res is not None else refs[2]
        acc = refs[-1] if use_acc else o_ref
        k = pl.program_id(2)
        p = lax.dot_general(a_ref[...].astype(BF16), b_ref[...].astype(BF16), (dims, ((), ())),
                            preferred_element_type=F32)

        def finish(r):
            if r_ref is not None:
                r = r + alpha * r_ref[...].astype(F32)
            o_ref[...] = r.astype(out_dtype)

        if nk == 1:
            finish(p)
        else:
            @pl.when(k == 0)
            def _():
                acc[...] = p

            @pl.when((k > 0) & (k < nk - 1))
            def _():
                acc[...] += p

            @pl.when(k == nk - 1)
            def _():
                finish(acc[...] + p)

    if mode == "tn":
        a_spec = pl.BlockSpec((tk, tm), lambda i, j, k: (k, i))
    else:
        a_spec = pl.BlockSpec((tm, tk), lambda i, j, k: (i, k))
    if mode == "nt":
        b_spec = pl.BlockSpec((tn, tk), lambda i, j, k: (j, k))
    else:
        b_spec = pl.BlockSpec((tk, tn), lambda i, j, k: (k, j))
    in_specs = [a_spec, b_spec]
    args = [a, b]
    if res is not None:
        in_specs.append(pl.BlockSpec((tm, tn), lambda i, j, k: (i, j)))
        args.append(res)
    return pl.pallas_call(
        body, name=name,
        grid=(M // tm, N // tn, nk),
        in_specs=in_specs,
        out_specs=pl.BlockSpec((tm, tn), lambda i, j, k: (i, j)),
        out_shape=jax.ShapeDtypeStruct((M, N), out_dtype),
        scratch_shapes=[pltpu.VMEM((tm, tn), F32)] if use_acc else [],
        compiler_params=_params(("parallel", "parallel", "arbitrary")),
    )(*args)


def _matmul_ln(a, w, x, g, b, name, loss=None):
    L, K = a.shape
    D = w.shape[1]
    tr = L
    for cap in (1536, 768, 384):
        tr = _tile(L, cap, 16)
        if 2 * (2 * tr * K + 2 * K * D + 4 * tr * D * (4 if loss else 3) + 2 * tr * D) <= MATMUL_VMEM_BUDGET:
            break

    def body(*refs):
        a_ref, w_ref, x_ref, g_ref, b_ref = refs[:5]
        z = ALPHA * x_ref[...] + _dnn(a_ref[...], w_ref[...])
        mu = jnp.mean(z, axis=-1, keepdims=True)
        zc = z - mu
        var = jnp.mean(zc * zc, axis=-1, keepdims=True)
        y = zc * lax.rsqrt(var + LN_EPS) * g_ref[...] + b_ref[...]
        if loss is None:
            z_ref, y_ref, yb_ref = refs[5:]
            y_ref[...] = y
            yb_ref[...] = y.astype(BF16)
        else:
            t_ref, z_ref, dy_ref, loss_ref = refs[5:]
            i = pl.program_id(0)

            @pl.when(i == 0)
            def _():
                loss_ref[...] = jnp.zeros_like(loss_ref)

            r = i * tr + lax.broadcasted_iota(jnp.int32, (tr, D), 0)
            valid = (r >= N_META) & (r < N_META + loss[1])
            e = jnp.where(valid, y - t_ref[...], 0.0)
            dy_ref[...] = e * (1.0 / D)
            s = jnp.sum(jnp.sum(e * e, axis=-1, keepdims=True), axis=0, keepdims=True)
            loss_ref[...] += (0.5 / D) * s
        z_ref[...] = z

    row = pl.BlockSpec((tr, D), lambda i: (i, 0))
    vec = pl.BlockSpec((1, D), lambda i: (0, 0))
    in_specs = [pl.BlockSpec((tr, K), lambda i: (i, 0)), pl.BlockSpec((K, D), lambda i: (0, 0)), row, vec, vec]
    f32_rows = jax.ShapeDtypeStruct((L, D), F32)
    if loss is None:
        args, out_specs = [a, w, x, g, b], [row, row, row]
        out_shape = [f32_rows, f32_rows, jax.ShapeDtypeStruct((L, D), BF16)]
    else:
        args, in_specs = [a, w, x, g, b, loss[0]], in_specs + [row]
        out_specs = [row, row, pl.BlockSpec((1, 1), lambda i: (0, 0))]
        out_shape = [f32_rows, f32_rows, jax.ShapeDtypeStruct((1, 1), F32)]
    return pl.pallas_call(
        body, name=name, grid=(L // tr,), in_specs=in_specs, out_specs=out_specs, out_shape=out_shape,
        compiler_params=_params(("arbitrary",) if loss else ("parallel",)),
    )(*args)


def _ln_bwd(z, dy, g, name):
    L, D = z.shape
    tr = _tile(L, 768, 16)

    def body(z_ref, dy_ref, g_ref, dzb_ref, dg_ref, db_ref):
        @pl.when(pl.program_id(0) == 0)
        def _():
            dg_ref[...] = jnp.zeros_like(dg_ref)
            db_ref[...] = jnp.zeros_like(db_ref)

        z = z_ref[...]
        mu = jnp.mean(z, axis=-1, keepdims=True)
        zc = z - mu
        var = jnp.mean(zc * zc, axis=-1, keepdims=True)
        rstd = lax.rsqrt(var + LN_EPS)
        xhat = zc * rstd
        dy = dy_ref[...].astype(F32)
        dxh = dy * g_ref[...]
        m1 = jnp.mean(dxh, axis=-1, keepdims=True)
        m2 = jnp.mean(dxh * xhat, axis=-1, keepdims=True)
        dz = rstd * (dxh - m1 - xhat * m2)
        dzb_ref[...] = dz.astype(BF16)
        dg_ref[...] += jnp.sum(dy * xhat, axis=0, keepdims=True)
        db_ref[...] += jnp.sum(dy, axis=0, keepdims=True)

    row = pl.BlockSpec((tr, D), lambda i: (i, 0))
    vec = pl.BlockSpec((1, D), lambda i: (0, 0))
    return pl.pallas_call(
        body, name=name, grid=(L // tr,),
        in_specs=[row, row, vec], out_specs=[row, vec, vec],
        out_shape=[jax.ShapeDtypeStruct((L, D), BF16),
                   jax.ShapeDtypeStruct((1, D), F32), jax.ShapeDtypeStruct((1, D), F32)],
        compiler_params=_params(("arbitrary",)),
    )(z, dy, g)


def _shift_down(x, prev, k):
    T, C = x.shape
    rot = pltpu.roll(jnp.concatenate([prev, x], axis=0).reshape(T // 8 + 1, 8, C), k, 1)
    sub = lax.broadcasted_iota(jnp.int32, (T // 8, 8, C), 1)
    return jnp.where(sub < k, rot[:-1], rot[1:]).reshape(T, C)


def _shift_up(x, nxt, k):
    T, C = x.shape
    rot = pltpu.roll(jnp.concatenate([x, nxt], axis=0).reshape(T // 8 + 1, 8, C), 8 - k, 1)
    sub = lax.broadcasted_iota(jnp.int32, (T // 8, 8, C), 1)
    return jnp.where(sub >= 8 - k, rot[1:], rot[:-1]).reshape(T, C)


def _conv3(x, prev, w, b):
    return w[2:3, :] * x + w[1:2, :] * _shift_down(x, prev, 1) + w[0:1, :] * _shift_down(x, prev, 2) + b


def _ffn_act_fwd(up, w, b, name):
    L, C = up.shape
    F = C // 2
    ts = 2 * FFN_TILE
    n = L // ts

    def body(up_ref, pv_ref, w_ref, b_ref, a_ref, u_ref):
        i = pl.program_id(0)
        x = up_ref[...].astype(F32)
        prev = jnp.where(i > 0, pv_ref[...].astype(F32)[8:16], 0.0)
        u = _conv3(x, prev, w_ref[...], b_ref[...])
        u_ref[...] = u.astype(BF16)
        gate = u[:, :F]
        a_ref[...] = (gate * _sigmoid(gate) * u[:, F:]).astype(BF16)

    return pl.pallas_call(
        body, name=name, grid=(n,),
        in_specs=[pl.BlockSpec((ts, C), lambda i: (i, 0)),
                  pl.BlockSpec((16, C), lambda i: (jnp.maximum(i * (ts // 16) - 1, 0), 0)),
                  pl.BlockSpec((3, C), lambda i: (0, 0)), pl.BlockSpec((1, C), lambda i: (0, 0))],
        out_specs=[pl.BlockSpec((ts, F), lambda i: (i, 0)), pl.BlockSpec((ts, C), lambda i: (i, 0))],
        out_shape=[jax.ShapeDtypeStruct((L, F), BF16), jax.ShapeDtypeStruct((L, C), BF16)],
        compiler_params=_params(("parallel",)),
    )(up, up, w, b)


def _ffn_act_bwd(up, u, da, w, name):
    L, C = up.shape
    F = C // 2
    ts = FFN_TILE
    n = L // ts
    last16 = L // 16 - 1

    def du_of(u, da):
        gate, val = u[:, :F], u[:, F:]
        sg = _sigmoid(gate)
        dgate = da * val * (sg * (1.0 + gate * (1.0 - sg)))
        dval = da * (gate * sg)
        return jnp.concatenate([dgate, dval], axis=1)

    def body(up_ref, u_ref, un_ref, da_ref, dan_ref, w_ref, dup_ref, dw_ref, db_ref):
        i = pl.program_id(0)

        @pl.when(i == 0)
        def _():
            dw_ref[...] = jnp.zeros_like(dw_ref)
            db_ref[...] = jnp.zeros_like(db_ref)

        w = w_ref[...]
        x = up_ref[...].astype(F32)
        du = du_of(u_ref[...].astype(F32), da_ref[...].astype(F32))
        dun = jnp.where(i < n - 1, du_of(un_ref[...].astype(F32)[0:8], dan_ref[...].astype(F32)[0:8]), 0.0)
        du1 = _shift_up(du, dun, 1)
        du2 = _shift_up(du, dun, 2)
        dup_ref[...] = (w[2:3, :] * du + w[1:2, :] * du1 + w[0:1, :] * du2).astype(BF16)
        dw_ref[...] += jnp.concatenate([jnp.sum(x * du2, axis=0, keepdims=True),
                                        jnp.sum(x * du1, axis=0, keepdims=True),
                                        jnp.sum(x * du, axis=0, keepdims=True)], axis=0)
        db_ref[...] += jnp.sum(du, axis=0, keepdims=True)

    nxt = lambda i: (jnp.minimum((i + 1) * (ts // 16), last16), 0)
    return pl.pallas_call(
        body, name=name, grid=(n,),
        in_specs=[pl.BlockSpec((ts, C), lambda i: (i, 0)),
                  pl.BlockSpec((ts, C), lambda i: (i, 0)), pl.BlockSpec((16, C), nxt),
                  pl.BlockSpec((ts, F), lambda i: (i, 0)), pl.BlockSpec((16, F), nxt),
                  pl.BlockSpec((3, C), lambda i: (0, 0))],
        out_specs=[pl.BlockSpec((ts, C), lambda i: (i, 0)), pl.BlockSpec((3, C), lambda i: (0, 0)),
                   pl.BlockSpec((1, C), lambda i: (0, 0))],
        out_shape=[jax.ShapeDtypeStruct((L, C), BF16), jax.ShapeDtypeStruct((3, C), F32),
                   jax.ShapeDtypeStruct((1, C), F32)],
        compiler_params=_params(("arbitrary",)),
    )(up, u, u, da, da, w)


def _pool_window(ext, tile_rows, first_row, lead):
    T = ext.shape[0]
    sh = (lambda x, k: pltpu.roll(x, T - k, 0)) if lead else (lambda x, k: pltpu.roll(x, k, 0))
    r2 = ext + sh(ext, 1)
    r4 = r2 + sh(r2, 2)
    r8 = r4 + sh(r4, 4)
    r16 = r8 + sh(r8, 8)
    lo = 0 if lead else 16
    grp = lax.broadcasted_iota(jnp.int32, (tile_rows, POOL_W), 1) // POOL_GROUP
    pick = lambda a, b, c, d: jnp.where(grp == 0, a, jnp.where(grp == 1, b, jnp.where(grp == 2, c, d)))
    win = pick(r2[lo:lo + tile_rows], r4[lo:lo + tile_rows], r8[lo:lo + tile_rows], r16[lo:lo + tile_rows])
    return win, pick(2.0, 4.0, 8.0, 16.0)


def _pool_count(first_row, rows, wlen):
    t1 = (first_row + lax.broadcasted_iota(jnp.int32, (rows, POOL_W), 0) + 1).astype(F32)
    return jnp.minimum(t1, wlen)


def _cp_fwd(hc, wc, wblk, pscale, name):
    L = hc.shape[0]
    ts = SEQ_TILE
    n = L // ts

    def body(h_ref, hp_ref, wc_ref, wb_ref, ps_ref, y_ref):
        i = pl.program_id(0)
        h = h_ref[...]
        hp = jnp.where(i > 0, hp_ref[...], 0.0)
        cb, cc, cv, pv = h[:, 0:256], h[:, 256:512], h[:, 512:768], h[:, 768:1024]
        p = cc * cv
        pp = hp[8:16, 256:512] * hp[8:16, 512:768]
        w = wc_ref[...]
        conv = w[2:3, :] * p + w[1:2, :] * _shift_down(p, pp, 1) + w[0:1, :] * _shift_down(p, pp, 2)
        y_conv = cb * conv
        ext = jnp.concatenate([hp[:, 768:1024], pv], axis=0)
        win, wlen = _pool_window(ext, ts, i * ts, False)
        d = win / _pool_count(i * ts, ts, wlen) - pv
        y_pool = _dnn(d.astype(BF16), wb_ref[...]) * ps_ref[...]
        y_ref[...] = jnp.concatenate([y_conv, y_pool], axis=1).astype(BF16)

    return pl.pallas_call(
        body, name=name, grid=(n,),
        in_specs=[pl.BlockSpec((ts, 1024), lambda i: (i, 0)),
                  pl.BlockSpec((16, 1024), lambda i: (jnp.maximum(i * (ts // 16) - 1, 0), 0)),
                  pl.BlockSpec((3, 256), lambda i: (0, 0)), pl.BlockSpec((256, 256), lambda i: (0, 0)),
                  pl.BlockSpec((1, 256), lambda i: (0, 0))],
        out_specs=pl.BlockSpec((ts, 512), lambda i: (i, 0)),
        out_shape=jax.ShapeDtypeStruct((L, 512), BF16),
        compiler_params=_params(("parallel",)),
    )(hc, hc, wc, wblk, pscale)


def _cp_bwd(hc, dcat, wc, wblk, pscale, name):
    L = hc.shape[0]
    ts = SEQ_TILE
    n = L // ts
    last16 = L // 16 - 1

    def body(h_ref, hp_ref, hn_ref, dy_ref, dyn_ref, wc_ref, wb_ref, ps_ref,
             dh_ref, dwc_ref, dwb_ref, dps_ref):
        i = pl.program_id(0)

        @pl.when(i == 0)
        def _():
            dwc_ref[...] = jnp.zeros_like(dwc_ref)
            dwb_ref[...] = jnp.zeros_like(dwb_ref)
            dps_ref[...] = jnp.zeros_like(dps_ref)

        h = h_ref[...]
        hp = jnp.where(i > 0, hp_ref[...], 0.0)
        hn = hn_ref[...]
        dy = dy_ref[...].astype(F32)
        dyn = jnp.where(i < n - 1, dyn_ref[...].astype(F32), 0.0)
        cb, cc, cv, pv = h[:, 0:256], h[:, 256:512], h[:, 512:768], h[:, 768:1024]
        w = wc_ref[...]
        p = cc * cv
        pp = hp[8:16, 256:512] * hp[8:16, 512:768]
        p1 = _shift_down(p, pp, 1)
        p2 = _shift_down(p, pp, 2)
        conv = w[2:3, :] * p + w[1:2, :] * p1 + w[0:1, :] * p2
        dyc = dy[:, 0:256]
        dcb = dyc * conv
        dconv = dyc * cb
        dconv_n = dyn[0:8, 0:256] * hn[0:8, 0:256]
        dc1 = _shift_up(dconv, dconv_n, 1)
        dc2 = _shift_up(dconv, dconv_n, 2)
        dp = w[2:3, :] * dconv + w[1:2, :] * dc1 + w[0:1, :] * dc2
        dwc_ref[...] += jnp.concatenate([jnp.sum(p * dc2, axis=0, keepdims=True),
                                         jnp.sum(p * dc1, axis=0, keepdims=True),
                                         jnp.sum(p * dconv, axis=0, keepdims=True)], axis=0)
        ps = ps_ref[...]
        wb = wb_ref[...]
        ext = jnp.concatenate([hp[:, 768:1024], pv], axis=0)
        win, wlen = _pool_window(ext, ts, i * ts, False)
        d = win / _pool_count(i * ts, ts, wlen) - pv
        db = d.astype(BF16)
        dyp = dy[:, 256:512]
        dps_ref[...] += jnp.sum(dyp * _dnn(db, wb), axis=0, keepdims=True)
        dypre = (dyp * ps).astype(BF16)
        dwb_ref[...] += _dtn(db, dypre)
        dd = _dnt(dypre, wb)
        ddn = _dnt((dyn[:, 256:512] * ps).astype(BF16), wb)
        e = dd / _pool_count(i * ts, ts, wlen)
        en = ddn / _pool_count((i + 1) * ts, 16, wlen[0:16])
        lead, _ = _pool_window(jnp.concatenate([e, en], axis=0), ts, i * ts, True)
        dpv = lead - dd
        dh_ref[...] = jnp.concatenate([dcb, dp * cv, dp * cc, dpv], axis=1).astype(BF16)

    return pl.pallas_call(
        body, name=name, grid=(n,),
        in_specs=[pl.BlockSpec((ts, 1024), lambda i: (i, 0)),
                  pl.BlockSpec((16, 1024), lambda i: (jnp.maximum(i * (ts // 16) - 1, 0), 0)),
                  pl.BlockSpec((16, 1024), lambda i: (jnp.minimum((i + 1) * (ts // 16), last16), 0)),
                  pl.BlockSpec((ts, 512), lambda i: (i, 1)),
                  pl.BlockSpec((16, 512), lambda i: (jnp.minimum((i + 1) * (ts // 16), last16), 1)),
                  pl.BlockSpec((3, 256), lambda i: (0, 0)), pl.BlockSpec((256, 256), lambda i: (0, 0)),
                  pl.BlockSpec((1, 256), lambda i: (0, 0))],
        out_specs=[pl.BlockSpec((ts, 1024), lambda i: (i, 0)), pl.BlockSpec((3, 256), lambda i: (0, 0)),
                   pl.BlockSpec((256, 256), lambda i: (0, 0)), pl.BlockSpec((1, 256), lambda i: (0, 0))],
        out_shape=[jax.ShapeDtypeStruct((L, 1024), BF16), jax.ShapeDtypeStruct((3, 256), F32),
                   jax.ShapeDtypeStruct((256, 256), F32), jax.ShapeDtypeStruct((1, 256), F32)],
        compiler_params=_params(("arbitrary",)),
    )(hc, hc, hc, dcat, dcat, wc, wblk, pscale)


def _lower_bound(lb_ref, layer):
    b0, b1 = lb_ref[0:1, :], lb_ref[1:2, :]
    m = jnp.maximum(b0, b1)
    e0, e1 = jnp.exp(b0 - m), jnp.exp(b1 - m)
    p0, p1 = e0 / (e0 + e1), e1 / (e0 + e1)
    lb = (p0 - p0) if layer == 0 else ((p0 + p1) - p0)
    return lb, p0, p1


def _cumsum_rows(x, reverse=False):
    row = lax.broadcasted_iota(jnp.int32, x.shape, 0)
    for sh in (1, 2, 4, 8):
        if reverse:
            x = x + jnp.where(row < SUB - sh, pltpu.roll(x, SUB - sh, 0), 0.0)
        else:
            x = x + jnp.where(row >= sh, pltpu.roll(x, sh, 0), 0.0)
    return x


def _gates(fz, lb):
    sig = _sigmoid(fz)
    f = lb + (1.0 - lb) * sig
    g = jnp.log(jnp.maximum(f, F_FLOOR))
    k = (1.0 - lb) * (1.0 - sig)
    return sig, f, g, k


def _head(h):
    return slice(h * HG_D, (h + 1) * HG_D)


def _hgrn_fwd(hh, lbp, gnorm, layer, name):
    L = hh.shape[0]
    ts = SEQ_TILE
    n = L // ts
    nsub = ts // SUB

    def body(q_ref, f_ref, i_ref, g_ref, lb_ref, gn_ref, y_ref, o_ref, s_ref, a_ref, St):
        @pl.when(pl.program_id(0) == 0)
        def _():
            St[...] = jnp.zeros_like(St)

        lb, _, _ = _lower_bound(lb_ref, layer)
        gn = jnp.tile(gn_ref[...], (1, HG_HEADS))
        r16 = lax.broadcasted_iota(jnp.int32, (SUB, SUB), 0)
        c16 = lax.broadcasted_iota(jnp.int32, (SUB, SUB), 1)

        def block(j, carry):
            rows = pl.ds(pl.multiple_of(j * SUB, SUB), SUB)
            q = q_ref[rows, :] * Q_SCALE
            iv = i_ref[rows, :]
            gz = g_ref[rows, :]
            _, _, g, k = _gates(f_ref[rows, :], lb)
            G = _cumsum_rows(g)
            Gl = G[SUB - 1:SUB, :]
            qt = (q * jnp.exp(G)).astype(BF16)
            kd = (k * jnp.exp(Gl - G)).astype(BF16)
            eGl = jnp.exp(Gl)
            ib = iv.astype(BF16)
            A = [jnp.zeros((SUB, SUB), F32) for _ in range(HG_HEADS)]
            for s in range(SUB):
                P = q * jnp.exp(jnp.minimum(G - G[s:s + 1, :], 0.0)) * k[s:s + 1, :]
                for h in range(HG_HEADS):
                    A[h] = jnp.where(c16 == s, jnp.sum(P[:, _head(h)], axis=-1, keepdims=True), A[h])
            outs, ons, amats = [], [], []
            for h in range(HG_HEADS):
                sl = _head(h)
                Sb = St[h].astype(BF16)
                s_ref[j, sl, :] = Sb
                Am = jnp.where(r16 >= c16, A[h], 0.0)
                amats.append(Am)
                o = _dnt(qt[:, sl], Sb) + _dnn(Am.astype(BF16), ib[:, sl])
                St[h] = eGl[:, sl] * St[h] + _dtn(ib[:, sl], kd[:, sl])
                outs.append(o)
                ons.append(o * lax.rsqrt(jnp.mean(o * o, axis=-1, keepdims=True) + RMS_EPS))
            a_ref[rows, :] = jnp.concatenate(amats, axis=1)
            o_ref[rows, :] = jnp.concatenate(outs, axis=1)
            y = jnp.concatenate(ons, axis=1) * gn * (gz * _sigmoid(gz))
            y_ref[rows, :] = y.astype(BF16)
            return carry

        lax.fori_loop(0, nsub, block, 0, unroll=2)

    col = lambda c: pl.BlockSpec((ts, HG_W), lambda i: (i, c))
    return pl.pallas_call(
        body, name=name, grid=(n,),
        in_specs=[col(0), col(1), col(2), col(3), pl.BlockSpec((2, HG_W), lambda i: (0, 0)),
                  pl.BlockSpec((1, HG_D), lambda i: (0, 0))],
        out_specs=[pl.BlockSpec((ts, HG_W), lambda i: (i, 0)), pl.BlockSpec((ts, HG_W), lambda i: (i, 0)),
                   pl.BlockSpec((nsub, HG_W, HG_D), lambda i: (i, 0, 0)),
                   pl.BlockSpec((ts, HG_HEADS * SUB), lambda i: (i, 0))],
        out_shape=[jax.ShapeDtypeStruct((L, HG_W), BF16), jax.ShapeDtypeStruct((L, HG_W), F32),
                   jax.ShapeDtypeStruct((L // SUB, HG_W, HG_D), BF16),
                   jax.ShapeDtypeStruct((L, HG_HEADS * SUB), F32)],
        scratch_shapes=[pltpu.VMEM((HG_HEADS, HG_D, HG_D), F32)],
        compiler_params=_params(("arbitrary",)),
    )(hh, hh, hh, hh, lbp, gnorm)


def _hgrn_bwd(hh, o_raw, states, amat, dcat, lbp, gnorm, layer, name):
    L = hh.shape[0]
    ts = SEQ_TILE
    n = L // ts
    nsub = ts // SUB

    def body(q_ref, f_ref, i_ref, g_ref, o_ref, s_ref, a_ref, dy_ref, lb_ref, gn_ref,
             dh_ref, dlb_ref, dgn_ref, dSt, dlb_acc, S_next):
        step = pl.program_id(0)

        @pl.when(step == 0)
        def _():
            dSt[...] = jnp.zeros_like(dSt)
            S_next[...] = jnp.zeros_like(S_next)
            dlb_acc[...] = jnp.zeros_like(dlb_acc)
            dgn_ref[...] = jnp.zeros_like(dgn_ref)

        lb, p0, p1 = _lower_bound(lb_ref, layer)
        gnh = gn_ref[...]
        gn = jnp.tile(gnh, (1, HG_HEADS))
        r16 = lax.broadcasted_iota(jnp.int32, (SUB, SUB), 0)
        c16 = lax.broadcasted_iota(jnp.int32, (SUB, SUB), 1)

        def block(jj, carry):
            j = nsub - 1 - jj
            rows = pl.ds(pl.multiple_of(j * SUB, SUB), SUB)
            q = q_ref[rows, :] * Q_SCALE
            iv = i_ref[rows, :]
            gz = g_ref[rows, :]
            o = o_ref[rows, :]
            dy = dy_ref[rows, :].astype(F32)
            sig, f, g, k = _gates(f_ref[rows, :], lb)
            G = _cumsum_rows(g)
            Gl = G[SUB - 1:SUB, :]
            eG = jnp.exp(G)
            edl = jnp.exp(Gl - G)
            eGl = jnp.exp(Gl)
            qt = (q * eG).astype(BF16)
            kd = (k * edl).astype(BF16)
            ib = iv.astype(BF16)
            sgz = _sigmoid(gz)
            sil = gz * sgz
            dyn = dy * sil
            on_parts, do_parts = [], []
            dgn = jnp.zeros((1, HG_D), F32)
            for h in range(HG_HEADS):
                sl = _head(h)
                oh = o[:, sl]
                rs = lax.rsqrt(jnp.mean(oh * oh, axis=-1, keepdims=True) + RMS_EPS)
                on = oh * rs
                dgn = dgn + jnp.sum(dyn[:, sl] * on, axis=0, keepdims=True)
                don = dyn[:, sl] * gnh
                do_parts.append(rs * (don - on * jnp.mean(don * on, axis=-1, keepdims=True)))
                on_parts.append(on)
            dgn_ref[...] += dgn
            on_all = jnp.concatenate(on_parts, axis=1)
            dgz = dy * on_all * gn * (sgz * (1.0 + gz * (1.0 - sgz)))
            do = jnp.concatenate(do_parts, axis=1)
            dob = do.astype(BF16)
            amat = a_ref[rows, :]
            dq_p, dk_p, di_p, tail_p = [], [], [], []
            for h in range(HG_HEADS):
                sl = _head(h)
                qh, kh, Gh = q[:, sl], k[:, sl], G[:, sl]
                Ap = jnp.where(r16 >= c16, _dnt(dob[:, sl], ib[:, sl]), 0.0)
                ApT = jnp.where(r16 <= c16, _dnt(ib[:, sl], dob[:, sl]), 0.0)
                dqh = jnp.zeros((SUB, HG_D), F32)
                dkh = jnp.zeros((SUB, HG_D), F32)
                for s in range(SUB):
                    dGs = Gh - Gh[s:s + 1, :]
                    e = jnp.exp(jnp.minimum(dGs, -dGs))
                    dqh = dqh + Ap[:, s:s + 1] * (e * kh[s:s + 1, :])
                    dkh = dkh + ApT[:, s:s + 1] * (e * qh[s:s + 1, :])
                Sb = s_ref[j, sl, :]
                dSb = dSt[h].astype(BF16)
                Am = amat[:, h * SUB:(h + 1) * SUB].astype(BF16)
                dq_p.append(dqh + eG[:, sl] * _dnn(dob[:, sl], Sb))
                dk_p.append(dkh + edl[:, sl] * _dnn(ib[:, sl], dSb))
                di_p.append(_dtn(Am, dob[:, sl]) + _dnt(kd[:, sl], dSb))
                tail_p.append(jnp.sum(dSt[h] * S_next[h].astype(F32), axis=0, keepdims=True))
                S_next[h] = Sb
                dSt[h] = eGl[:, sl] * dSt[h] + _dtn(dob[:, sl], qt[:, sl])
            dq = jnp.concatenate(dq_p, axis=1)
            dk = jnp.concatenate(dk_p, axis=1)
            di = jnp.concatenate(di_p, axis=1)
            dg = _cumsum_rows(q * dq - k * dk, reverse=True) + jnp.concatenate(tail_p, axis=1)
            df = jnp.where(f > F_FLOOR, dg / f, 0.0)
            dfk = df - dk
            dfz = (1.0 - lb) * dfk * sig * (1.0 - sig)
            dlb_acc[...] += jnp.sum(dfk * (1.0 - sig), axis=0, keepdims=True)
            dh_ref[rows, :] = jnp.concatenate([dq * Q_SCALE, dfz, di, dgz], axis=1).astype(BF16)
            return carry

        lax.fori_loop(0, nsub, block, 0)

        @pl.when(step == n - 1)
        def _():
            if layer == 0:
                dlb_ref[...] = jnp.zeros_like(dlb_ref)
            else:
                dz1 = p0 * p1 * dlb_acc[...]
                dlb_ref[...] = jnp.concatenate([-dz1, dz1], axis=0)

    rev = lambda i: n - 1 - i
    col = lambda c: pl.BlockSpec((ts, HG_W), lambda i: (rev(i), c))
    return pl.pallas_call(
        body, name=name, grid=(n,),
        in_specs=[col(0), col(1), col(2), col(3), col(0),
                  pl.BlockSpec((nsub, HG_W, HG_D), lambda i: (rev(i), 0, 0)),
                  pl.BlockSpec((ts, HG_HEADS * SUB), lambda i: (rev(i), 0)), col(0),
                  pl.BlockSpec((2, HG_W), lambda i: (0, 0)), pl.BlockSpec((1, HG_D), lambda i: (0, 0))],
        out_specs=[pl.BlockSpec((ts, 4 * HG_W), lambda i: (rev(i), 0)),
                   pl.BlockSpec((2, HG_W), lambda i: (0, 0)), pl.BlockSpec((1, HG_D), lambda i: (0, 0))],
        out_shape=[jax.ShapeDtypeStruct((L, 4 * HG_W), BF16), jax.ShapeDtypeStruct((2, HG_W), F32),
                   jax.ShapeDtypeStruct((1, HG_D), F32)],
        scratch_shapes=[pltpu.VMEM((HG_HEADS, HG_D, HG_D), F32), pltpu.VMEM((1, HG_W), F32),
                        pltpu.VMEM((HG_HEADS, HG_D, HG_D), BF16)],
        compiler_params=_params(("arbitrary",)),
    )(hh, hh, hh, hh, o_raw, states, amat, dcat, lbp, gnorm)


def _adamw_body(gp_ref, w_ref, m_ref, v_ref, g_ref, d_ref, mo_ref, vo_ref):
    c1 = 1.0 - ADAM_B1 ** ADAM_STEP
    c2 = 1.0 - ADAM_B2 ** ADAM_STEP
    g = gp_ref[0].astype(F32)
    for k in range(1, N_DEV):
        g = g + gp_ref[k].astype(F32)
    mn = ADAM_B1 * m_ref[...] + (1.0 - ADAM_B1) * g
    vn = ADAM_B2 * v_ref[...] + (1.0 - ADAM_B2) * (g * g)
    m_hat = mn / c1
    v_hat = vn / c2
    g_ref[...] = g
    d_ref[...] = -ADAM_LR * (m_hat / (jnp.sqrt(v_hat) + ADAM_EPS) + ADAM_WD * w_ref[...])
    mo_ref[...] = mn
    vo_ref[...] = vn


def _adamw_layers(gparts, w, m, v, name):
    depth, R, C = w.shape
    tr = _tile(R, 256, 16)
    nr = R // tr

    def body(*refs):
        layer = pl.program_id(0)
        for d in range(depth):
            @pl.when(layer == d)
            def _(d=d):
                _adamw_body(refs[d], *refs[depth:])

    def parts_spec(d):
        return pl.BlockSpec((N_DEV, tr, C),
                            lambda l, i: (0, jnp.where(l == d, i, jnp.where(l < d, 0, nr - 1)), 0))

    blk = pl.BlockSpec((None, tr, C), lambda l, i: (l, i, 0))
    shp = jax.ShapeDtypeStruct((depth, R, C), F32)
    return pl.pallas_call(
        body, name=name, grid=(depth, nr),
        in_specs=[parts_spec(d) for d in range(depth)] + [blk, blk, blk],
        out_specs=[blk, blk, blk, blk], out_shape=[shp, shp, shp, shp],
        compiler_params=_params(("arbitrary", "arbitrary")),
    )(*gparts, w, m, v)


def _adamw(gparts, w, m, v, name):
    R = w.shape[0]
    tr = _tile(R, 1024, 16) if R % 16 == 0 else R

    def body(*refs):
        _adamw_body(*refs)

    row = pl.BlockSpec((tr, LANES), lambda i: (i, 0))
    shp = jax.ShapeDtypeStruct((R, LANES), F32)
    return pl.pallas_call(
        body, name=name, grid=(R // tr,),
        in_specs=[pl.BlockSpec((N_DEV, tr, LANES), lambda i: (0, i, 0)), row, row, row],
        out_specs=[row, row, row, row], out_shape=[shp, shp, shp, shp],
        compiler_params=_params(("parallel",)),
    )(gparts, w, m, v)


def _flip(coord, bit):
    return 1 - coord if bit else coord


def _gather_many(blocks, name):
    n = len(blocks)

    def body(*refs):
        x_refs, out_refs = refs[:n], refs[n:2 * n]
        send_sems, recv_sems, local_sems = refs[2 * n:]
        x, y, c = lax.axis_index("x"), lax.axis_index("y"), lax.axis_index("c")
        me, sibling = (x, y, c), (x, y, 1 - c)
        chips = [(1 - x, y), (x, 1 - y), (1 - x, 1 - y)]

        def slot(a, px, py, pc):
            return out_refs[a].at[4 * px + 2 * py + pc]

        def copy(a, k, blk, to, src=None):
            return pltpu.make_async_remote_copy(
                src_ref=slot(a, *blk) if src is None else src, dst_ref=slot(a, *blk),
                send_sem=send_sems.at[7 * a + k], recv_sem=recv_sems.at[7 * a + k],
                device_id=to, device_id_type=pl.DeviceIdType.MESH)

        mine = [pltpu.make_async_copy(x_refs[a], slot(a, *me), local_sems.at[a]) for a in range(n)]
        for cp in mine:
            cp.start()
        first = [copy(a, 0, me, sibling, src=x_refs[a]) for a in range(n)]
        for j, chip in enumerate(chips):
            first += [copy(a, 1 + j, me, (*chip, c), src=x_refs[a]) for a in range(n)]
        for cp in first:
            cp.start()
        passed = []
        for j, chip in enumerate(chips):
            for a in range(n):
                copy(a, 1 + j, (*chip, c), me).wait_recv()
                fwd = copy(a, 4 + j, (*chip, c), sibling)
                fwd.start()
                passed.append(fwd)
        for a in range(n):
            copy(a, 0, sibling, me).wait_recv()
        for j, chip in enumerate(chips):
            for a in range(n):
                copy(a, 4 + j, (*chip, 1 - c), me).wait_recv()
        for cp in first + passed:
            cp.wait_send()
        for cp in mine:
            cp.wait()

    hbm = pl.BlockSpec(memory_space=pl.ANY)
    return pl.pallas_call(
        body, name=name,
        out_shape=[jax.ShapeDtypeStruct((N_DEV,) + b.shape, b.dtype) for b in blocks],
        in_specs=[hbm] * n, out_specs=[hbm] * n,
        scratch_shapes=[pltpu.SemaphoreType.DMA((7 * n,)), pltpu.SemaphoreType.DMA((7 * n,)),
                        pltpu.SemaphoreType.DMA((n,))],
    )(*blocks)


def _split_start(blocks, chunked, name):
    n = len(blocks)
    lands = [lax.empty(b.shape if chunked else (N_DEV,) + b.shape, b.dtype) for b in blocks]

    def body(*refs):
        x_refs, land_refs = refs[:n], refs[n:2 * n]
        send_sems, recv_sems, token = refs[2 * n], refs[2 * n + 1], refs[-1]
        x, y, c = lax.axis_index("x"), lax.axis_index("y"), lax.axis_index("c")
        me = 4 * x + 2 * y + c
        for a in range(n):
            for k in range(1, N_DEV):
                px, py, pc = _flip(x, k & 4), _flip(y, k & 2), _flip(c, k & 1)
                pltpu.make_async_remote_copy(
                    src_ref=x_refs[a].at[4 * px + 2 * py + pc] if chunked else x_refs[a],
                    dst_ref=land_refs[a].at[me],
                    send_sem=send_sems.at[7 * a + k - 1], recv_sem=recv_sems.at[7 * a + k - 1],
                    device_id=(px, py, pc), device_id_type=pl.DeviceIdType.MESH).start()
        token[...] = jnp.zeros_like(token)

    hbm = pl.BlockSpec(memory_space=pltpu.HBM)
    sem = pl.BlockSpec(memory_space=pltpu.SEMAPHORE)
    outs = pl.pallas_call(
        body, name=name,
        out_shape=(pltpu.SemaphoreType.DMA((7 * n,)), pltpu.SemaphoreType.DMA((7 * n,)),
                   *[pltpu.HBM(b.shape, b.dtype) for b in blocks], *[pltpu.HBM(l.shape, l.dtype) for l in lands],
                   jax.ShapeDtypeStruct((8, LANES), F32)),
        in_specs=[hbm] * (2 * n),
        out_specs=(sem, sem, *[hbm] * (2 * n), pl.BlockSpec(memory_space=pltpu.VMEM)),
        input_output_aliases={i: 2 + i for i in range(2 * n)},
        compiler_params=pltpu.CompilerParams(has_side_effects=pltpu.SideEffectType.DATAFLOW_SIDE_EFFECTING),
    )(*[pltpu.with_memory_space_constraint(b, pltpu.HBM) for b in blocks],
      *[pltpu.with_memory_space_constraint(l, pltpu.HBM) for l in lands])
    return outs[0], outs[1], list(outs[2:2 + n]), list(outs[2 + n:2 + 2 * n]), outs[-1]


def _split_wait(started, chunked, after, name):
    send_sems, recv_sems, blocks, lands, _ = started
    n = len(blocks)

    def body(*refs):
        x_refs, land_refs = refs[:n], refs[n:2 * n]
        send_sems, recv_sems = refs[2 * n], refs[2 * n + 1]
        x, y, c = lax.axis_index("x"), lax.axis_index("y"), lax.axis_index("c")
        for a in range(n):
            for k in range(1, N_DEV):
                px, py, pc = _flip(x, k & 4), _flip(y, k & 2), _flip(c, k & 1)
                copy = pltpu.make_async_remote_copy(
                    src_ref=x_refs[a].at[4 * px + 2 * py + pc] if chunked else x_refs[a],
                    dst_ref=land_refs[a].at[4 * px + 2 * py + pc],
                    send_sem=send_sems.at[7 * a + k - 1], recv_sem=recv_sems.at[7 * a + k - 1],
                    device_id=(px, py, pc), device_id_type=pl.DeviceIdType.MESH)
                copy.wait_send()
                copy.wait_recv()

    hbm = pl.BlockSpec(memory_space=pltpu.HBM)
    sem = pl.BlockSpec(memory_space=pltpu.SEMAPHORE)
    outs = pl.pallas_call(
        body, name=name,
        out_shape=(*[pltpu.HBM(b.shape, b.dtype) for b in blocks], *[pltpu.HBM(l.shape, l.dtype) for l in lands]),
        in_specs=[hbm] * (2 * n) + [sem, sem, pl.BlockSpec(memory_space=pl.ANY)],
        out_specs=[hbm] * (2 * n),
        input_output_aliases={i: i for i in range(2 * n)},
        compiler_params=pltpu.CompilerParams(has_side_effects=pltpu.SideEffectType.DATAFLOW_SIDE_EFFECTING),
    )(*blocks, *lands, send_sems, recv_sems, after)
    me = 4 * lax.axis_index("x") + 2 * lax.axis_index("y") + lax.axis_index("c")
    own = [lax.dynamic_index_in_dim(b, me, 0, keepdims=False) if chunked else b for b in outs[:n]]
    return [lax.dynamic_update_index_in_dim(z, o, me, 0) for z, o in zip(outs[n:], own)]


def _exchange_grads(layer_chunks, small_chunks, rep_block, name):
    flows, inputs = [], []
    for p, per_layer in enumerate(layer_chunks):
        for l, arr in enumerate(per_layer):
            flows.append(("param", p, l))
            inputs.append(arr)
    flows += [("small",), ("rep",)]
    inputs += [small_chunks, rep_block]
    n_par = len(layer_chunks)
    n_in, n_out, nf = len(inputs), n_par + 2, len(flows)

    def body(*refs):
        in_refs, out_refs = refs[:n_in], refs[n_in:n_in + n_out]
        send_sems, recv_sems, local_sems = refs[n_in + n_out:]
        x, y, c = lax.axis_index("x"), lax.axis_index("y"), lax.axis_index("c")
        me = 4 * x + 2 * y + c

        def src(f, dev):
            return in_refs[f] if flows[f][0] == "rep" else in_refs[f].at[dev]

        def dst(f, dev):
            if flows[f][0] == "param":
                _, p, l = flows[f]
                return out_refs[p].at[dev, l]
            return out_refs[n_par + (0 if flows[f][0] == "small" else 1)].at[dev]

        mine = [pltpu.make_async_copy(src(f, me), dst(f, me), local_sems.at[f]) for f in range(nf)]
        for cp in mine:
            cp.start()
        copies = []
        for k in range(1, N_DEV):
            px, py, pc = _flip(x, k & 4), _flip(y, k & 2), _flip(c, k & 1)
            peer = 4 * px + 2 * py + pc
            for f in range(nf):
                sems = dict(send_sem=send_sems.at[7 * f + k - 1], recv_sem=recv_sems.at[7 * f + k - 1],
                            device_id=(px, py, pc), device_id_type=pl.DeviceIdType.MESH)
                send = pltpu.make_async_remote_copy(src_ref=src(f, peer), dst_ref=dst(f, me), **sems)
                recv = pltpu.make_async_remote_copy(src_ref=src(f, peer), dst_ref=dst(f, peer), **sems)
                send.start()
                copies.append((send, recv))
        for send, recv in copies:
            recv.wait_recv()
        for send, recv in copies:
            send.wait_send()
        for cp in mine:
            cp.wait()

    out_shape = [jax.ShapeDtypeStruct((N_DEV, len(pl_)) + pl_[0].shape[1:], pl_[0].dtype) for pl_ in layer_chunks]
    out_shape += [jax.ShapeDtypeStruct(small_chunks.shape, small_chunks.dtype),
                  jax.ShapeDtypeStruct((N_DEV,) + rep_block.shape, rep_block.dtype)]
    hbm = pl.BlockSpec(memory_space=pl.ANY)
    return pl.pallas_call(
        body, name=name, out_shape=out_shape,
        in_specs=[hbm] * n_in, out_specs=[hbm] * n_out,
        scratch_shapes=[pltpu.SemaphoreType.DMA((7 * nf,)), pltpu.SemaphoreType.DMA((7 * nf,)),
                        pltpu.SemaphoreType.DMA((nf,))],
    )(*inputs)


def _pack_rows(size):
    return -(-size // (8 * LANES)) * 8


def _pack(arrs, dtype):
    parts, offs, r = [], [], 0
    for a in arrs:
        flat = a.astype(dtype).reshape(-1)
        nrow = _pack_rows(flat.shape[0])
        flat = jnp.pad(flat, (0, nrow * LANES - flat.shape[0]))
        parts.append(flat.reshape(nrow, LANES))
        offs.append((r, nrow))
        r += nrow
    return jnp.concatenate(parts, axis=0), offs


def _unpack(buf, offs, shapes, lead=()):
    outs = []
    for (r, nrow), shp in zip(offs, shapes):
        size = 1
        for s in shp:
            size *= s
        flat = buf[..., r:r + nrow, :].reshape(lead + (nrow * LANES,))
        outs.append(flat[..., :size].reshape(lead + tuple(shp)))
    return outs


def _cols_from_shards(g, axis):
    return jnp.concatenate([g[j] for j in range(N_DEV)], axis=axis)


BIG = ("w_in", "w_o", "w_up", "w_down")
SMALL_SHARDED = ("meta_tokens", "w_conv", "w_ffn_conv")
REPLICATED = ("hg_lower_bounds", "w_pool", "pool_scale", "hg_norm_g", "ln1_g", "ln1_b", "b_ffn_conv", "ln2_g", "ln2_b")
WEIGHTS = ("meta_tokens", "hg_lower_bounds", "w_in", "w_conv", "w_pool", "pool_scale", "hg_norm_g", "w_o",
           "ln1_g", "ln1_b", "w_up", "w_ffn_conv", "b_ffn_conv", "w_down", "ln2_g", "ln2_b")


def _pool_blockdiag(w_pool_l):
    z = jnp.zeros((POOL_GROUP, POOL_GROUP), w_pool_l.dtype)
    rows = [jnp.concatenate([w_pool_l[g] if h == g else z for h in range(4)], axis=1) for g in range(4)]
    return jnp.concatenate(rows, axis=0)


def _mixer_weights(g_in, g_o):
    w_in = jnp.transpose(g_in, (1, 0, 2)).reshape(D_MODEL, -1)
    w_o = g_o.reshape(-1, D_MODEL)
    return dict(
        w_hg=w_in[:, 768:2816],
        w_cp=jnp.concatenate([w_in[:, 0:768], w_in[:, 2816:3072]], axis=1),
        w_o=jnp.concatenate([w_o[256:768], w_o[0:256], w_o[768:1024]], axis=0))


def _ffn_weights(g_up, g_down):
    return dict(w_up=jnp.transpose(g_up, (1, 0, 2)).reshape(D_MODEL, -1), w_down=g_down.reshape(-1, D_MODEL))


def kernel(x, meta_tokens, hg_lower_bounds, w_in, w_conv, w_pool, pool_scale, hg_norm_g, w_o, ln1_g, ln1_b, w_up, w_ffn_conv, b_ffn_conv, w_down, ln2_g, ln2_b, loss_target, m_meta_tokens, m_hg_lower_bounds, m_w_in, m_w_conv, m_w_pool, m_pool_scale, m_hg_norm_g, m_w_o, m_ln1_g, m_ln1_b, m_w_up, m_w_ffn_conv, m_b_ffn_conv, m_w_down, m_ln2_g, m_ln2_b, v_meta_tokens, v_hg_lower_bounds, v_w_in, v_w_conv, v_w_pool, v_pool_scale, v_hg_norm_g, v_w_o, v_ln1_g, v_ln1_b, v_w_up, v_w_ffn_conv, v_b_ffn_conv, v_w_down, v_ln2_g, v_ln2_b):
    W = dict(meta_tokens=meta_tokens, hg_lower_bounds=hg_lower_bounds, w_in=w_in, w_conv=w_conv, w_pool=w_pool,
             pool_scale=pool_scale, hg_norm_g=hg_norm_g, w_o=w_o, ln1_g=ln1_g, ln1_b=ln1_b, w_up=w_up,
             w_ffn_conv=w_ffn_conv, b_ffn_conv=b_ffn_conv, w_down=w_down, ln2_g=ln2_g, ln2_b=ln2_b)
    M = dict(meta_tokens=m_meta_tokens, hg_lower_bounds=m_hg_lower_bounds, w_in=m_w_in, w_conv=m_w_conv,
             w_pool=m_w_pool, pool_scale=m_pool_scale, hg_norm_g=m_hg_norm_g, w_o=m_w_o, ln1_g=m_ln1_g,
             ln1_b=m_ln1_b, w_up=m_w_up, w_ffn_conv=m_w_ffn_conv, b_ffn_conv=m_b_ffn_conv, w_down=m_w_down,
             ln2_g=m_ln2_g, ln2_b=m_ln2_b)
    V = dict(meta_tokens=v_meta_tokens, hg_lower_bounds=v_hg_lower_bounds, w_in=v_w_in, w_conv=v_w_conv,
             w_pool=v_w_pool, pool_scale=v_pool_scale, hg_norm_g=v_hg_norm_g, w_o=v_w_o, ln1_g=v_ln1_g,
             ln1_b=v_ln1_b, w_up=v_w_up, w_ffn_conv=v_w_ffn_conv, b_ffn_conv=v_b_ffn_conv, w_down=v_w_down,
             ln2_g=v_ln2_g, ln2_b=v_ln2_b)
    assert x.shape[0] == 1 and x.shape[2] == D_MODEL and w_in.shape[0] == DEPTH
    seq = x.shape[1]
    L = -(-(seq + N_META) // ROW_ALIGN) * ROW_ALIGN

    small_pack, small_offs = _pack([W[n] for n in SMALL_SHARDED], F32)
    shards = {n: [W[n][l].astype(BF16) for l in range(DEPTH)] for n in BIG}
    g_in0, g_o0, small_all = _gather_many([shards["w_in"][0], shards["w_o"][0], small_pack], "gather_weights")
    full = {}
    for n, a in zip(SMALL_SHARDED, _unpack(small_all, small_offs, [W[n].shape for n in SMALL_SHARDED], (N_DEV,))):
        full[n] = _cols_from_shards(a, 1)
    order = (small_all[0, 0, 0] * 0.0).astype(BF16)
    ffn0_started = _split_start([shards["w_up"][0] + order, shards["w_down"][0] + order], False, "gather_ffn0_start")
    order = ffn0_started[4][0, 0].astype(BF16)
    layer1_started = _split_start([shards[n][1] + order for n in BIG], False, "gather_layer1_start")
    lb_in = hg_lower_bounds + layer1_started[4][0, 0]

    pad_rows = L - N_META - seq
    xp = jnp.concatenate([full["meta_tokens"], x[0], jnp.zeros((pad_rows, D_MODEL), F32)], axis=0)
    tgt = jnp.concatenate([jnp.zeros((N_META, D_MODEL), F32), loss_target[0], jnp.zeros((pad_rows, D_MODEL), F32)], axis=0)

    saved = []
    h_in, h_in_b = xp, xp.astype(BF16)
    for l in range(DEPTH):
        if l == 0:
            lw = _mixer_weights(g_in0, g_o0)
        else:
            g_in, g_o, g_up, g_down = _split_wait(layer1_started, False, h_in_b, "gather_layer1_wait")
            lw = {**_mixer_weights(g_in, g_o), **_ffn_weights(g_up, g_down)}
        wc = full["w_conv"][l].T
        wblk = _pool_blockdiag(w_pool[l]).astype(BF16)
        ps = pool_scale[l][None, :]
        gn = hg_norm_g[l][None, :]
        wf = full["w_ffn_conv"][l].T
        bf = b_ffn_conv[l][None, :]
        hh = _matmul(h_in_b, lw["w_hg"], "nn", F32, f"fwd_hg_{l}")
        hc = _matmul(h_in_b, lw["w_cp"], "nn", F32, f"fwd_cp_{l}")
        y_hg, o_raw, states, amat = _hgrn_fwd(hh, lb_in if l == 0 else hg_lower_bounds, gn, l, f"hgrn_fwd_{l}")
        y_cp = _cp_fwd(hc, wc, wblk, ps, f"convpool_fwd_{l}")
        cat = jnp.concatenate([y_hg, y_cp], axis=1)
        z1, x1, x1_b = _matmul_ln(cat, lw["w_o"], h_in, ln1_g[l][None, :], ln1_b[l][None, :], f"fwd_o_ln1_{l}")
        if l == 0:
            lw.update(_ffn_weights(*_split_wait(ffn0_started, False, x1_b, "gather_ffn0_wait")))
        up = _matmul(x1_b, lw["w_up"], "nn", BF16, f"fwd_up_{l}")
        a, u = _ffn_act_fwd(up, wf, bf, f"ffn_fwd_{l}")
        saved.append(dict(lw=lw, wc=wc, wblk=wblk, ps=ps, gn=gn, wf=wf, bf=bf, x_b=h_in_b, hh=hh, hc=hc,
                          o_raw=o_raw, states=states, amat=amat, cat=cat, z1=z1, x1_b=x1_b, up=up, u=u, a=a))
        if l < DEPTH - 1:
            saved[l]["z2"], h_in, h_in_b = _matmul_ln(a, lw["w_down"], x1, ln2_g[l][None, :], ln2_b[l][None, :],
                                                     f"fwd_down_ln2_{l}")
        else:
            saved[l]["z2"], dy, loss_part = _matmul_ln(a, lw["w_down"], x1, ln2_g[l][None, :], ln2_b[l][None, :],
                                                       f"fwd_down_ln2_loss_{l}", loss=(tgt, seq))

    loss = lax.psum(loss_part[0, 0], ("x", "y", "c"))

    G = {}
    per_layer = {n: [None] * DEPTH for n in ("w_conv", "w_pool", "pool_scale", "hg_norm_g", "ln1_g", "ln1_b",
                                             "w_ffn_conv", "b_ffn_conv", "ln2_g", "ln2_b")}
    ffn_started, mix_started = [None] * DEPTH, [None] * DEPTH
    order = jnp.zeros((), F32)
    dlb_total = jnp.zeros((DEPTH, HG_W), F32)
    for l in reversed(range(DEPTH)):
        s = saved[l]
        lw = s["lw"]
        dz2_b, dg2, db2 = _ln_bwd(s["z2"], dy, ln2_g[l][None, :] + order, f"ln2_bwd_{l}")
        da = _matmul(dz2_b, lw["w_down"], "nt", BF16, f"bwd_da_{l}")
        d_w_down = _matmul(s["a"], dz2_b, "tn", BF16, f"wgrad_down_{l}")
        dup, dwf, dbf = _ffn_act_bwd(s["up"], s["u"], da, s["wf"], f"ffn_bwd_{l}")
        dx1 = _matmul(dup, lw["w_up"], "nt", BF16, f"bwd_dx1_{l}", res=dz2_b, alpha=ALPHA)
        d_w_up = _matmul(s["x1_b"], dup, "tn", BF16, f"wgrad_up_{l}")
        ffn_started[l] = _split_start([jnp.transpose(d_w_up.reshape(D_MODEL, N_DEV, -1), (1, 0, 2)),
                                       d_w_down.reshape(N_DEV, -1, D_MODEL)], True, f"scatter_ffn{l}_start")
        order = ffn_started[l][4][0, 0]
        dz1_b, dg1, db1 = _ln_bwd(s["z1"], dx1, ln1_g[l][None, :] + order, f"ln1_bwd_{l}")
        dcat = _matmul(dz1_b, lw["w_o"], "nt", BF16, f"bwd_dcat_{l}")
        d_w_o = _matmul(s["cat"], dz1_b, "tn", BF16, f"wgrad_o_{l}")
        dhh, dlb, dgn = _hgrn_bwd(s["hh"], s["o_raw"], s["states"], s["amat"], dcat, hg_lower_bounds, s["gn"], l,
                                  f"hgrn_bwd_{l}")
        dhc, dwc, dwblk, dps = _cp_bwd(s["hc"], dcat, s["wc"], s["wblk"], s["ps"], f"convpool_bwd_{l}")
        d_w_hg = _matmul(s["x_b"], dhh, "tn", BF16, f"wgrad_hg_{l}")
        d_w_cp = _matmul(s["x_b"], dhc, "tn", BF16, f"wgrad_cp_{l}")
        d_w_in = jnp.concatenate([d_w_cp[:, 0:768], d_w_hg, d_w_cp[:, 768:1024]], axis=1)
        mix_chunks = [jnp.transpose(d_w_in.reshape(D_MODEL, N_DEV, -1), (1, 0, 2)),
                      jnp.concatenate([d_w_o[512:768], d_w_o[0:512], d_w_o[768:1024]], axis=0).reshape(N_DEV, -1, D_MODEL)]
        mix_started[l] = _split_start(mix_chunks, True, f"scatter_mix{l}_start")
        order = mix_started[l][4][0, 0]
        dx_a = _matmul(dhh, lw["w_hg"] + order.astype(BF16), "nt", F32, f"bwd_dx_hg_{l}", res=dz1_b, alpha=ALPHA)
        dx = _matmul(dhc, lw["w_cp"], "nt", F32, f"bwd_dx_cp_{l}", res=dx_a, alpha=1.0)
        per_layer["w_conv"][l] = dwc.T
        per_layer["w_ffn_conv"][l] = dwf.T
        per_layer["b_ffn_conv"][l] = dbf[0]
        per_layer["w_pool"][l] = jnp.stack([dwblk[g * 64:(g + 1) * 64, g * 64:(g + 1) * 64] for g in range(4)], axis=0)
        per_layer["pool_scale"][l] = dps[0]
        per_layer["hg_norm_g"][l] = dgn[0]
        per_layer["ln1_g"][l], per_layer["ln1_b"][l] = dg1[0], db1[0]
        per_layer["ln2_g"][l], per_layer["ln2_b"][l] = dg2[0], db2[0]
        dlb_total = dlb_total + dlb
        dy = dx
    for n, parts in per_layer.items():
        G[n] = jnp.stack(parts, axis=0)
    G["hg_lower_bounds"] = dlb_total
    grad_x = dy[N_META:N_META + seq][None]

    def shard_major(g, lead):
        g = g.reshape(g.shape[:lead] + (N_DEV, -1) + g.shape[lead + 1:])
        g = jnp.moveaxis(g, lead, 0).reshape(N_DEV, -1)
        nrow = _pack_rows(g.shape[1])
        return jnp.pad(g, ((0, 0), (0, nrow * LANES - g.shape[1]))).reshape(N_DEV, nrow, LANES)

    small_chunks = jnp.concatenate([shard_major(dy[0:N_META], 1), shard_major(G["w_conv"], 1),
                                    shard_major(G["w_ffn_conv"], 1)], axis=1)
    w_small, _ = _pack([W[n] for n in SMALL_SHARDED], F32)
    rep_pack, rep_offs = _pack([G[n] for n in REPLICATED], F32)
    small_recv, rep_all = _exchange_grads([], small_chunks, rep_pack, "exchange_grads")
    parts = {n: [] for n in BIG}
    for l in range(DEPTH):
        up_l, down_l = _split_wait(ffn_started[l], True, rep_all, f"scatter_ffn{l}_wait")
        in_l, o_l = _split_wait(mix_started[l], True, rep_all, f"scatter_mix{l}_wait")
        for n, a in zip(BIG, (in_l, o_l, up_l, down_l)):
            parts[n].append(a)

    res = {k: {} for k in ("grad", "delta", "new_m", "new_v")}
    kinds = ("grad", "delta", "new_m", "new_v")
    for n in BIG:
        for kind, a in zip(kinds, _adamw_layers(parts[n], W[n], M[n], V[n], f"adamw_{n}")):
            res[kind][n] = a
    m_small, _ = _pack([M[n] for n in SMALL_SHARDED], F32)
    v_small, _ = _pack([V[n] for n in SMALL_SHARDED], F32)
    outs_small = _adamw(small_recv, w_small, m_small, v_small, "adamw_small_sharded")
    w_rep, _ = _pack([W[n] for n in REPLICATED], F32)
    m_rep, _ = _pack([M[n] for n in REPLICATED], F32)
    v_rep, _ = _pack([V[n] for n in REPLICATED], F32)
    outs_rep = _adamw(rep_all, w_rep, m_rep, v_rep, "adamw_replicated")
    for kind, b_sm, b_rep in zip(kinds, outs_small, outs_rep):
        for n, a in zip(SMALL_SHARDED, _unpack(b_sm, small_offs, [W[n].shape for n in SMALL_SHARDED])):
            res[kind][n] = a
        for n, a in zip(REPLICATED, _unpack(b_rep, rep_offs, [W[n].shape for n in REPLICATED])):
            res[kind][n] = a

    return (loss, grad_x, *[res["grad"][n] for n in WEIGHTS], *[res["delta"][n] for n in WEIGHTS],
            *[res["new_m"][n] for n in WEIGHTS], *[res["new_v"][n] for n in WEIGHTS])
```

```python
import jax
import jax.numpy as jnp
from jax import lax
from jax.experimental import pallas as pl
from jax.experimental.pallas import tpu as pltpu

F32 = jnp.float32
BF16 = jnp.bfloat16

N_DEV = 8
D_MODEL = 1024
N_META = 16
DEPTH = 2
CONV_W = 256
HG_W = 512
HG_D = 128
HG_HEADS = 4
POOL_W = 256
POOL_GROUP = 64
D_FF = 2816
ALPHA = (2 * DEPTH) ** 0.25
LN_EPS = 1e-5
RMS_EPS = 1e-6
F_FLOOR = 1e-30
Q_SCALE = HG_D ** -0.5
SUB = 16
SEQ_TILE = 192
FFN_TILE = 192
CP_TILE_CAP = 768
ROW_ALIGN = 192
LANES = 128
VMEM_LIMIT = 48 * 1024 * 1024
MATMUL_VMEM_BUDGET = 38 * 1024 * 1024

ADAM_LR = 0.001
ADAM_B1 = 0.9
ADAM_B2 = 0.999
ADAM_EPS = 1e-08
ADAM_WD = 0.01
ADAM_STEP = 10


def _tile(n, cap, mult):
    best = 0
    for t in range(mult, min(n, cap) + 1, mult):
        if n % t == 0:
            best = t
    assert best > 0, (n, cap, mult)
    return best


def _params(sem, vmem=VMEM_LIMIT):
    return pltpu.CompilerParams(dimension_semantics=sem, vmem_limit_bytes=vmem)


def _dnt(a, b):
    return lax.dot_general(a, b, (((1,), (1,)), ((), ())), preferred_element_type=F32)


def _dtn(a, b):
    return lax.dot_general(a, b, (((0,), (0,)), ((), ())), preferred_element_type=F32)


def _dnn(a, b):
    return jnp.dot(a, b, preferred_element_type=F32)


def _sigmoid(x):
    return 1.0 / (1.0 + jnp.exp(-x))


def _matmul(a, b, mode, out_dtype, name, res=None, alpha=1.0):
    if mode == "tn":
        K, M = a.shape
    else:
        M, K = a.shape
    N = b.shape[0] if mode == "nt" else b.shape[1]
    out_bytes = jnp.dtype(out_dtype).itemsize
    if mode == "tn" and M % 512 == 0 and N % 512 == 0:
        tk, tm, tn = K, 512, 512
        nk, use_acc = 1, False
    else:
        tn = _tile(N, 1536, LANES)
        tk = _tile(K, 1536, 16) if mode == "tn" else _tile(K, 2816, LANES)
        nk = K // tk
        use_acc = nk > 1 and out_dtype != F32
        tm = M
        for cap in (1536, 768, 384):
            tm = _tile(M, cap, 16)
            blocks = 2 * (a.dtype.itemsize * tm * tk + b.dtype.itemsize * tn * tk + out_bytes * tm * tn
                          + (res.dtype.itemsize * tm * tn if res is not None else 0)) + (4 * tm * tn if use_acc else 0)
            if blocks <= MATMUL_VMEM_BUDGET:
                break
    dims = {"nn": ((1,), (0,)), "nt": ((1,), (1,)), "tn": ((0,), (0,))}[mode]

    def body(*refs):
        a_ref, b_ref = refs[0], refs[1]
        r_ref = refs[2] if res is not None else None
        o_ref = refs[3] if res is not None else refs[2]
        acc = refs[-1] if use_acc else o_ref
        k = pl.program_id(2)
        p = lax.dot_general(a_ref[...].astype(BF16), b_ref[...].astype(BF16), (dims, ((), ())),
                            preferred_element_type=F32)

        def finish(r):
            if r_ref is not None:
                r = r + alpha * r_ref[...].astype(F32)
            o_ref[...] = r.astype(out_dtype)

        if nk == 1:
            finish(p)
        else:
            @pl.when(k == 0)
            def _():
                acc[...] = p

            @pl.when((k > 0) & (k < nk - 1))
            def _():
                acc[...] += p

            @pl.when(k == nk - 1)
            def _():
                finish(acc[...] + p)

    if mode == "tn":
        a_spec = pl.BlockSpec((tk, tm), lambda i, j, k: (k, i))
    else:
        a_spec = pl.BlockSpec((tm, tk), lambda i, j, k: (i, k))
    if mode == "nt":
        b_spec = pl.BlockSpec((tn, tk), lambda i, j, k: (j, k))
    else:
        b_spec = pl.BlockSpec((tk, tn), lambda i, j, k: (k, j))
    in_specs = [a_spec, b_spec]
    args = [a, b]
    if res is not None:
        in_specs.append(pl.BlockSpec((tm, tn), lambda i, j, k: (i, j)))
        args.append(res)
    return pl.pallas_call(
        body, name=name,
        grid=(M // tm, N // tn, nk),
        in_specs=in_specs,
        out_specs=pl.BlockSpec((tm, tn), lambda i, j, k: (i, j)),
        out_shape=jax.ShapeDtypeStruct((M, N), out_dtype),
        scratch_shapes=[pltpu.VMEM((tm, tn), F32)] if use_acc else [],
        compiler_params=_params(("parallel", "parallel", "arbitrary")),
    )(*args)


def _matmul_ln(a, w, x, g, b, name, loss=None):
    L, K = a.shape
    D = w.shape[1]
    tr = L
    for cap in (1536, 768, 384):
        tr = _tile(L, cap, 16)
        if 2 * (2 * tr * K + 2 * K * D + 4 * tr * D * (4 if loss else 3) + 2 * tr * D) <= MATMUL_VMEM_BUDGET:
            break

    def body(*refs):
        a_ref, w_ref, x_ref, g_ref, b_ref = refs[:5]
        z = ALPHA * x_ref[...] + _dnn(a_ref[...], w_ref[...])
        mu = jnp.mean(z, axis=-1, keepdims=True)
        zc = z - mu
        var = jnp.mean(zc * zc, axis=-1, keepdims=True)
        y = zc * lax.rsqrt(var + LN_EPS) * g_ref[...] + b_ref[...]
        if loss is None:
            z_ref, y_ref, yb_ref = refs[5:]
            y_ref[...] = y
            yb_ref[...] = y.astype(BF16)
        else:
            t_ref, z_ref, dy_ref, loss_ref = refs[5:]
            i = pl.program_id(0)

            @pl.when(i == 0)
            def _():
                loss_ref[...] = jnp.zeros_like(loss_ref)

            r = i * tr + lax.broadcasted_iota(jnp.int32, (tr, D), 0)
            valid = (r >= N_META) & (r < N_META + loss[1])
            e = jnp.where(valid, y - t_ref[...], 0.0)
            dy_ref[...] = e * (1.0 / D)
            s = jnp.sum(jnp.sum(e * e, axis=-1, keepdims=True), axis=0, keepdims=True)
            loss_ref[...] += (0.5 / D) * s
        z_ref[...] = z

    row = pl.BlockSpec((tr, D), lambda i: (i, 0))
    vec = pl.BlockSpec((1, D), lambda i: (0, 0))
    in_specs = [pl.BlockSpec((tr, K), lambda i: (i, 0)), pl.BlockSpec((K, D), lambda i: (0, 0)), row, vec, vec]
    f32_rows = jax.ShapeDtypeStruct((L, D), F32)
    if loss is None:
        args, out_specs = [a, w, x, g, b], [row, row, row]
        out_shape = [f32_rows, f32_rows, jax.ShapeDtypeStruct((L, D), BF16)]
    else:
        args, in_specs = [a, w, x, g, b, loss[0]], in_specs + [row]
        out_specs = [row, row, pl.BlockSpec((1, 1), lambda i: (0, 0))]
        out_shape = [f32_rows, f32_rows, jax.ShapeDtypeStruct((1, 1), F32)]
    return pl.pallas_call(
        body, name=name, grid=(L // tr,), in_specs=in_specs, out_specs=out_specs, out_shape=out_shape,
        compiler_params=_params(("arbitrary",) if loss else ("parallel",)),
    )(*args)


def _ln_bwd(z, dy, g, name):
    L, D = z.shape
    tr = _tile(L, 768, 16)

    def body(z_ref, dy_ref, g_ref, dzb_ref, dg_ref, db_ref):
        @pl.when(pl.program_id(0) == 0)
        def _():
            dg_ref[...] = jnp.zeros_like(dg_ref)
            db_ref[...] = jnp.zeros_like(db_ref)

        z = z_ref[...]
        mu = jnp.mean(z, axis=-1, keepdims=True)
        zc = z - mu
        var = jnp.mean(zc * zc, axis=-1, keepdims=True)
        rstd = lax.rsqrt(var + LN_EPS)
        xhat = zc * rstd
        dy = dy_ref[...].astype(F32)
        dxh = dy * g_ref[...]
        m1 = jnp.mean(dxh, axis=-1, keepdims=True)
        m2 = jnp.mean(dxh * xhat, axis=-1, keepdims=True)
        dz = rstd * (dxh - m1 - xhat * m2)
        dzb_ref[...] = dz.astype(BF16)
        dg_ref[...] += jnp.sum(dy * xhat, axis=0, keepdims=True)
        db_ref[...] += jnp.sum(dy, axis=0, keepdims=True)

    row = pl.BlockSpec((tr, D), lambda i: (i, 0))
    vec = pl.BlockSpec((1, D), lambda i: (0, 0))
    return pl.pallas_call(
        body, name=name, grid=(L // tr,),
        in_specs=[row, row, vec], out_specs=[row, vec, vec],
        out_shape=[jax.ShapeDtypeStruct((L, D), BF16),
                   jax.ShapeDtypeStruct((1, D), F32), jax.ShapeDtypeStruct((1, D), F32)],
        compiler_params=_params(("arbitrary",)),
    )(z, dy, g)


def _shift_down(x, prev, k):
    T, C = x.shape
    rot = pltpu.roll(jnp.concatenate([prev, x], axis=0).reshape(T // 8 + 1, 8, C), k, 1)
    sub = lax.broadcasted_iota(jnp.int32, (T // 8, 8, C), 1)
    return jnp.where(sub < k, rot[:-1], rot[1:]).reshape(T, C)


def _shift_up(x, nxt, k):
    T, C = x.shape
    rot = pltpu.roll(jnp.concatenate([x, nxt], axis=0).reshape(T // 8 + 1, 8, C), 8 - k, 1)
    sub = lax.broadcasted_iota(jnp.int32, (T // 8, 8, C), 1)
    return jnp.where(sub >= 8 - k, rot[1:], rot[:-1]).reshape(T, C)


def _conv3(x, prev, w, b):
    return w[2:3, :] * x + w[1:2, :] * _shift_down(x, prev, 1) + w[0:1, :] * _shift_down(x, prev, 2) + b


def _ffn_act_fwd(up, w, b, name):
    L, C = up.shape
    F = C // 2
    ts = FFN_TILE
    n = L // ts

    def body(up_ref, pv_ref, w_ref, b_ref, a_ref, u_ref):
        i = pl.program_id(0)
        x = up_ref[...].astype(F32)
        prev = jnp.where(i > 0, pv_ref[...].astype(F32)[8:16], 0.0)
        u = _conv3(x, prev, w_ref[...], b_ref[...])
        u_ref[...] = u.astype(BF16)
        gate = u[:, :F]
        a_ref[...] = (gate * _sigmoid(gate) * u[:, F:]).astype(BF16)

    return pl.pallas_call(
        body, name=name, grid=(n,),
        in_specs=[pl.BlockSpec((ts, C), lambda i: (i, 0)),
                  pl.BlockSpec((16, C), lambda i: (jnp.maximum(i * (ts // 16) - 1, 0), 0)),
                  pl.BlockSpec((3, C), lambda i: (0, 0)), pl.BlockSpec((1, C), lambda i: (0, 0))],
        out_specs=[pl.BlockSpec((ts, F), lambda i: (i, 0)), pl.BlockSpec((ts, C), lambda i: (i, 0))],
        out_shape=[jax.ShapeDtypeStruct((L, F), BF16), jax.ShapeDtypeStruct((L, C), BF16)],
        compiler_params=_params(("parallel",)),
    )(up, up, w, b)


def _ffn_act_bwd(up, u, da, w, name):
    L, C = up.shape
    F = C // 2
    ts = FFN_TILE
    n = L // ts
    last16 = L // 16 - 1

    def du_of(u, da):
        gate, val = u[:, :F], u[:, F:]
        sg = _sigmoid(gate)
        dgate = da * val * (sg * (1.0 + gate * (1.0 - sg)))
        dval = da * (gate * sg)
        return jnp.concatenate([dgate, dval], axis=1)

    def body(up_ref, u_ref, un_ref, da_ref, dan_ref, w_ref, dup_ref, dw_ref, db_ref):
        i = pl.program_id(0)

        @pl.when(i == 0)
        def _():
            dw_ref[...] = jnp.zeros_like(dw_ref)
            db_ref[...] = jnp.zeros_like(db_ref)

        w = w_ref[...]
        x = up_ref[...].astype(F32)
        du = du_of(u_ref[...].astype(F32), da_ref[...].astype(F32))
        dun = jnp.where(i < n - 1, du_of(un_ref[...].astype(F32)[0:8], dan_ref[...].astype(F32)[0:8]), 0.0)
        du1 = _shift_up(du, dun, 1)
        du2 = _shift_up(du, dun, 2)
        dup_ref[...] = (w[2:3, :] * du + w[1:2, :] * du1 + w[0:1, :] * du2).astype(BF16)
        dw_ref[...] += jnp.concatenate([jnp.sum(x * du2, axis=0, keepdims=True),
                                        jnp.sum(x * du1, axis=0, keepdims=True),
                                        jnp.sum(x * du, axis=0, keepdims=True)], axis=0)
        db_ref[...] += jnp.sum(du, axis=0, keepdims=True)

    nxt = lambda i: (jnp.minimum((i + 1) * (ts // 16), last16), 0)
    return pl.pallas_call(
        body, name=name, grid=(n,),
        in_specs=[pl.BlockSpec((ts, C), lambda i: (i, 0)),
                  pl.BlockSpec((ts, C), lambda i: (i, 0)), pl.BlockSpec((16, C), nxt),
                  pl.BlockSpec((ts, F), lambda i: (i, 0)), pl.BlockSpec((16, F), nxt),
                  pl.BlockSpec((3, C), lambda i: (0, 0))],
        out_specs=[pl.BlockSpec((ts, C), lambda i: (i, 0)), pl.BlockSpec((3, C), lambda i: (0, 0)),
                   pl.BlockSpec((1, C), lambda i: (0, 0))],
        out_shape=[jax.ShapeDtypeStruct((L, C), BF16), jax.ShapeDtypeStruct((3, C), F32),
                   jax.ShapeDtypeStruct((1, C), F32)],
        compiler_params=_params(("arbitrary",)),
    )(up, u, u, da, da, w)


def _pool_window(ext, tile_rows, first_row, lead):
    T = ext.shape[0]
    sh = (lambda x, k: pltpu.roll(x, T - k, 0)) if lead else (lambda x, k: pltpu.roll(x, k, 0))
    r2 = ext + sh(ext, 1)
    r4 = r2 + sh(r2, 2)
    r8 = r4 + sh(r4, 4)
    r16 = r8 + sh(r8, 8)
    lo = 0 if lead else 16
    grp = lax.broadcasted_iota(jnp.int32, (tile_rows, POOL_W), 1) // POOL_GROUP
    pick = lambda a, b, c, d: jnp.where(grp == 0, a, jnp.where(grp == 1, b, jnp.where(grp == 2, c, d)))
    win = pick(r2[lo:lo + tile_rows], r4[lo:lo + tile_rows], r8[lo:lo + tile_rows], r16[lo:lo + tile_rows])
    return win, pick(2.0, 4.0, 8.0, 16.0)


def _pool_count(first_row, rows, wlen):
    t1 = (first_row + lax.broadcasted_iota(jnp.int32, (rows, POOL_W), 0) + 1).astype(F32)
    return jnp.minimum(t1, wlen)


def _cp_fwd(hc, wc, wblk, pscale, name):
    L = hc.shape[0]
    ts = _tile(L, CP_TILE_CAP, 16)
    n = L // ts

    def body(h_ref, hp_ref, wc_ref, wb_ref, ps_ref, y_ref):
        i = pl.program_id(0)
        h = h_ref[...]
        hp = jnp.where(i > 0, hp_ref[...], 0.0)
        cb, cc, cv, pv = h[:, 0:256], h[:, 256:512], h[:, 512:768], h[:, 768:1024]
        p = cc * cv
        pp = hp[8:16, 256:512] * hp[8:16, 512:768]
        w = wc_ref[...]
        conv = w[2:3, :] * p + w[1:2, :] * _shift_down(p, pp, 1) + w[0:1, :] * _shift_down(p, pp, 2)
        y_conv = cb * conv
        ext = jnp.concatenate([hp[:, 768:1024], pv], axis=0)
        win, wlen = _pool_window(ext, ts, i * ts, False)
        d = win / _pool_count(i * ts, ts, wlen) - pv
        y_pool = _dnn(d.astype(BF16), wb_ref[...]) * ps_ref[...]
        y_ref[...] = jnp.concatenate([y_conv, y_pool], axis=1).astype(BF16)

    return pl.pallas_call(
        body, name=name, grid=(n,),
        in_specs=[pl.BlockSpec((ts, 1024), lambda i: (i, 0)),
                  pl.BlockSpec((16, 1024), lambda i: (jnp.maximum(i * (ts // 16) - 1, 0), 0)),
                  pl.BlockSpec((3, 256), lambda i: (0, 0)), pl.BlockSpec((256, 256), lambda i: (0, 0)),
                  pl.BlockSpec((1, 256), lambda i: (0, 0))],
        out_specs=pl.BlockSpec((ts, 512), lambda i: (i, 0)),
        out_shape=jax.ShapeDtypeStruct((L, 512), BF16),
        compiler_params=_params(("parallel",)),
    )(hc, hc, wc, wblk, pscale)


def _cp_bwd(hc, dcat, wc, wblk, pscale, name):
    L = hc.shape[0]
    ts = _tile(L, CP_TILE_CAP, 16)
    n = L // ts
    last16 = L // 16 - 1

    def body(h_ref, hp_ref, hn_ref, dy_ref, dyn_ref, wc_ref, wb_ref, ps_ref,
             dh_ref, dwc_ref, dwb_ref, dps_ref):
        i = pl.program_id(0)

        @pl.when(i == 0)
        def _():
            dwc_ref[...] = jnp.zeros_like(dwc_ref)
            dwb_ref[...] = jnp.zeros_like(dwb_ref)
            dps_ref[...] = jnp.zeros_like(dps_ref)

        h = h_ref[...]
        hp = jnp.where(i > 0, hp_ref[...], 0.0)
        hn = hn_ref[...]
        dy = dy_ref[...].astype(F32)
        dyn = jnp.where(i < n - 1, dyn_ref[...].astype(F32), 0.0)
        cb, cc, cv, pv = h[:, 0:256], h[:, 256:512], h[:, 512:768], h[:, 768:1024]
        w = wc_ref[...]
        p = cc * cv
        pp = hp[8:16, 256:512] * hp[8:16, 512:768]
        p1 = _shift_down(p, pp, 1)
        p2 = _shift_down(p, pp, 2)
        conv = w[2:3, :] * p + w[1:2, :] * p1 + w[0:1, :] * p2
        dyc = dy[:, 0:256]
        dcb = dyc * conv
        dconv = dyc * cb
        dconv_n = dyn[0:8, 0:256] * hn[0:8, 0:256]
        dc1 = _shift_up(dconv, dconv_n, 1)
        dc2 = _shift_up(dconv, dconv_n, 2)
        dp = w[2:3, :] * dconv + w[1:2, :] * dc1 + w[0:1, :] * dc2
        dwc_ref[...] += jnp.concatenate([jnp.sum(p * dc2, axis=0, keepdims=True),
                                         jnp.sum(p * dc1, axis=0, keepdims=True),
                                         jnp.sum(p * dconv, axis=0, keepdims=True)], axis=0)
        ps = ps_ref[...]
        wb = wb_ref[...]
        ext = jnp.concatenate([hp[:, 768:1024], pv], axis=0)
        win, wlen = _pool_window(ext, ts, i * ts, False)
        d = win / _pool_count(i * ts, ts, wlen) - pv
        db = d.astype(BF16)
        dyp = dy[:, 256:512]
        dps_ref[...] += jnp.sum(dyp * _dnn(db, wb), axis=0, keepdims=True)
        dypre = (dyp * ps).astype(BF16)
        dwb_ref[...] += _dtn(db, dypre)
        dd = _dnt(dypre, wb)
        ddn = _dnt((dyn[:, 256:512] * ps).astype(BF16), wb)
        e = dd / _pool_count(i * ts, ts, wlen)
        en = ddn / _pool_count((i + 1) * ts, 16, wlen[0:16])
        lead, _ = _pool_window(jnp.concatenate([e, en], axis=0), ts, i * ts, True)
        dpv = lead - dd
        dh_ref[...] = jnp.concatenate([dcb, dp * cv, dp * cc, dpv], axis=1).astype(BF16)

    return pl.pallas_call(
        body, name=name, grid=(n,),
        in_specs=[pl.BlockSpec((ts, 1024), lambda i: (i, 0)),
                  pl.BlockSpec((16, 1024), lambda i: (jnp.maximum(i * (ts // 16) - 1, 0), 0)),
                  pl.BlockSpec((16, 1024), lambda i: (jnp.minimum((i + 1) * (ts // 16), last16), 0)),
                  pl.BlockSpec((ts, 512), lambda i: (i, 1)),
                  pl.BlockSpec((16, 512), lambda i: (jnp.minimum((i + 1) * (ts // 16), last16), 1)),
                  pl.BlockSpec((3, 256), lambda i: (0, 0)), pl.BlockSpec((256, 256), lambda i: (0, 0)),
                  pl.BlockSpec((1, 256), lambda i: (0, 0))],
        out_specs=[pl.BlockSpec((ts, 1024), lambda i: (i, 0)), pl.BlockSpec((3, 256), lambda i: (0, 0)),
                   pl.BlockSpec((256, 256), lambda i: (0, 0)), pl.BlockSpec((1, 256), lambda i: (0, 0))],
        out_shape=[jax.ShapeDtypeStruct((L, 1024), BF16), jax.ShapeDtypeStruct((3, 256), F32),
                   jax.ShapeDtypeStruct((256, 256), F32), jax.ShapeDtypeStruct((1, 256), F32)],
        compiler_params=_params(("arbitrary",)),
    )(hc, hc, hc, dcat, dcat, wc, wblk, pscale)


def _lower_bound(lb_ref, layer):
    b0, b1 = lb_ref[0:1, :], lb_ref[1:2, :]
    m = jnp.maximum(b0, b1)
    e0, e1 = jnp.exp(b0 - m), jnp.exp(b1 - m)
    p0, p1 = e0 / (e0 + e1), e1 / (e0 + e1)
    lb = (p0 - p0) if layer == 0 else ((p0 + p1) - p0)
    return lb, p0, p1


def _cumsum_rows(x, reverse=False):
    row = lax.broadcasted_iota(jnp.int32, x.shape, 0)
    for sh in (1, 2, 4, 8):
        if reverse:
            x = x + jnp.where(row < SUB - sh, pltpu.roll(x, SUB - sh, 0), 0.0)
        else:
            x = x + jnp.where(row >= sh, pltpu.roll(x, sh, 0), 0.0)
    return x


def _gates(fz, lb):
    sig = _sigmoid(fz)
    f = lb + (1.0 - lb) * sig
    g = jnp.log(jnp.maximum(f, F_FLOOR))
    k = (1.0 - lb) * (1.0 - sig)
    return sig, f, g, k


def _head(h):
    return slice(h * HG_D, (h + 1) * HG_D)


def _hgrn_fwd(hh, lbp, gnorm, layer, name):
    L = hh.shape[0]
    ts = SEQ_TILE
    n = L // ts
    nsub = ts // SUB

    def body(q_ref, f_ref, i_ref, g_ref, lb_ref, gn_ref, y_ref, o_ref, s_ref, a_ref, St):
        @pl.when(pl.program_id(0) == 0)
        def _():
            St[...] = jnp.zeros_like(St)

        lb, _, _ = _lower_bound(lb_ref, layer)
        gn = jnp.tile(gn_ref[...], (1, HG_HEADS))
        r16 = lax.broadcasted_iota(jnp.int32, (SUB, SUB), 0)
        c16 = lax.broadcasted_iota(jnp.int32, (SUB, SUB), 1)

        def block(j, carry):
            rows = pl.ds(pl.multiple_of(j * SUB, SUB), SUB)
            q = q_ref[rows, :] * Q_SCALE
            iv = i_ref[rows, :]
            gz = g_ref[rows, :]
            _, _, g, k = _gates(f_ref[rows, :], lb)
            G = _cumsum_rows(g)
            Gl = G[SUB - 1:SUB, :]
            qt = (q * jnp.exp(G)).astype(BF16)
            kd = (k * jnp.exp(Gl - G)).astype(BF16)
            eGl = jnp.exp(Gl)
            ib = iv.astype(BF16)
            A = [jnp.zeros((SUB, SUB), F32) for _ in range(HG_HEADS)]
            for s in range(SUB):
                P = q * jnp.exp(jnp.minimum(G - G[s:s + 1, :], 0.0)) * k[s:s + 1, :]
                for h in range(HG_HEADS):
                    A[h] = jnp.where(c16 == s, jnp.sum(P[:, _head(h)], axis=-1, keepdims=True), A[h])
            outs, ons, amats = [], [], []
            for h in range(HG_HEADS):
                sl = _head(h)
                Sb = St[h].astype(BF16)
                s_ref[j, sl, :] = Sb
                Am = jnp.where(r16 >= c16, A[h], 0.0)
                amats.append(Am)
                o = _dnt(qt[:, sl], Sb) + _dnn(Am.astype(BF16), ib[:, sl])
                St[h] = eGl[:, sl] * St[h] + _dtn(ib[:, sl], kd[:, sl])
                outs.append(o)
                ons.append(o * lax.rsqrt(jnp.mean(o * o, axis=-1, keepdims=True) + RMS_EPS))
            a_ref[rows, :] = jnp.concatenate(amats, axis=1)
            o_ref[rows, :] = jnp.concatenate(outs, axis=1)
            y = jnp.concatenate(ons, axis=1) * gn * (gz * _sigmoid(gz))
            y_ref[rows, :] = y.astype(BF16)
            return carry

        lax.fori_loop(0, nsub, block, 0, unroll=2)

    col = lambda c: pl.BlockSpec((ts, HG_W), lambda i: (i, c))
    return pl.pallas_call(
        body, name=name, grid=(n,),
        in_specs=[col(0), col(1), col(2), col(3), pl.BlockSpec((2, HG_W), lambda i: (0, 0)),
                  pl.BlockSpec((1, HG_D), lambda i: (0, 0))],
        out_specs=[pl.BlockSpec((ts, HG_W), lambda i: (i, 0)), pl.BlockSpec((ts, HG_W), lambda i: (i, 0)),
                   pl.BlockSpec((nsub, HG_W, HG_D), lambda i: (i, 0, 0)),
                   pl.BlockSpec((ts, HG_HEADS * SUB), lambda i: (i, 0))],
        out_shape=[jax.ShapeDtypeStruct((L, HG_W), BF16), jax.ShapeDtypeStruct((L, HG_W), F32),
                   jax.ShapeDtypeStruct((L // SUB, HG_W, HG_D), BF16),
                   jax.ShapeDtypeStruct((L, HG_HEADS * SUB), F32)],
        scratch_shapes=[pltpu.VMEM((HG_HEADS, HG_D, HG_D), F32)],
        compiler_params=_params(("arbitrary",)),
    )(hh, hh, hh, hh, lbp, gnorm)


def _hgrn_bwd(hh, o_raw, states, amat, dcat, lbp, gnorm, layer, name):
    L = hh.shape[0]
    ts = SEQ_TILE
    n = L // ts
    nsub = ts // SUB

    def body(q_ref, f_ref, i_ref, g_ref, o_ref, s_ref, a_ref, dy_ref, lb_ref, gn_ref,
             dh_ref, dlb_ref, dgn_ref, dSt, dlb_acc, S_next):
        step = pl.program_id(0)

        @pl.when(step == 0)
        def _():
            dSt[...] = jnp.zeros_like(dSt)
            S_next[...] = jnp.zeros_like(S_next)
            dlb_acc[...] = jnp.zeros_like(dlb_acc)
            dgn_ref[...] = jnp.zeros_like(dgn_ref)

        lb, p0, p1 = _lower_bound(lb_ref, layer)
        gnh = gn_ref[...]
        gn = jnp.tile(gnh, (1, HG_HEADS))
        r16 = lax.broadcasted_iota(jnp.int32, (SUB, SUB), 0)
        c16 = lax.broadcasted_iota(jnp.int32, (SUB, SUB), 1)

        def block(jj, carry):
            j = nsub - 1 - jj
            rows = pl.ds(pl.multiple_of(j * SUB, SUB), SUB)
            q = q_ref[rows, :] * Q_SCALE
            iv = i_ref[rows, :]
            gz = g_ref[rows, :]
            o = o_ref[rows, :]
            dy = dy_ref[rows, :].astype(F32)
            sig, f, g, k = _gates(f_ref[rows, :], lb)
            G = _cumsum_rows(g)
            Gl = G[SUB - 1:SUB, :]
            eG = jnp.exp(G)
            edl = jnp.exp(Gl - G)
            eGl = jnp.exp(Gl)
            qt = (q * eG).astype(BF16)
            kd = (k * edl).astype(BF16)
            ib = iv.astype(BF16)
            sgz = _sigmoid(gz)
            sil = gz * sgz
            dyn = dy * sil
            on_parts, do_parts = [], []
            dgn = jnp.zeros((1, HG_D), F32)
            for h in range(HG_HEADS):
                sl = _head(h)
                oh = o[:, sl]
                rs = lax.rsqrt(jnp.mean(oh * oh, axis=-1, keepdims=True) + RMS_EPS)
                on = oh * rs
                dgn = dgn + jnp.sum(dyn[:, sl] * on, axis=0, keepdims=True)
                don = dyn[:, sl] * gnh
                do_parts.append(rs * (don - on * jnp.mean(don * on, axis=-1, keepdims=True)))
                on_parts.append(on)
            dgn_ref[...] += dgn
            on_all = jnp.concatenate(on_parts, axis=1)
            dgz = dy * on_all * gn * (sgz * (1.0 + gz * (1.0 - sgz)))
            do = jnp.concatenate(do_parts, axis=1)
            dob = do.astype(BF16)
            amat = a_ref[rows, :]
            dq_p, dk_p, di_p, tail_p = [], [], [], []
            for h in range(HG_HEADS):
                sl = _head(h)
                qh, kh, Gh = q[:, sl], k[:, sl], G[:, sl]
                Ap = jnp.where(r16 >= c16, _dnt(dob[:, sl], ib[:, sl]), 0.0)
                ApT = jnp.where(r16 <= c16, _dnt(ib[:, sl], dob[:, sl]), 0.0)
                dqh = jnp.zeros((SUB, HG_D), F32)
                dkh = jnp.zeros((SUB, HG_D), F32)
                for s in range(SUB):
                    dGs = Gh - Gh[s:s + 1, :]
                    e = jnp.exp(jnp.minimum(dGs, -dGs))
                    dqh = dqh + Ap[:, s:s + 1] * (e * kh[s:s + 1, :])
                    dkh = dkh + ApT[:, s:s + 1] * (e * qh[s:s + 1, :])
                Sb = s_ref[j, sl, :]
                dSb = dSt[h].astype(BF16)
                Am = amat[:, h * SUB:(h + 1) * SUB].astype(BF16)
                dq_p.append(dqh + eG[:, sl] * _dnn(dob[:, sl], Sb))
                dk_p.append(dkh + edl[:, sl] * _dnn(ib[:, sl], dSb))
                di_p.append(_dtn(Am, dob[:, sl]) + _dnt(kd[:, sl], dSb))
                tail_p.append(jnp.sum(dSt[h] * S_next[h].astype(F32), axis=0, keepdims=True))
                S_next[h] = Sb
                dSt[h] = eGl[:, sl] * dSt[h] + _dtn(dob[:, sl], qt[:, sl])
            dq = jnp.concatenate(dq_p, axis=1)
            dk = jnp.concatenate(dk_p, axis=1)
            di = jnp.concatenate(di_p, axis=1)
            dg = _cumsum_rows(q * dq - k * dk, reverse=True) + jnp.concatenate(tail_p, axis=1)
            df = jnp.where(f > F_FLOOR, dg / f, 0.0)
            dfk = df - dk
            dfz = (1.0 - lb) * dfk * sig * (1.0 - sig)
            dlb_acc[...] += jnp.sum(dfk * (1.0 - sig), axis=0, keepdims=True)
            dh_ref[rows, :] = jnp.concatenate([dq * Q_SCALE, dfz, di, dgz], axis=1).astype(BF16)
            return carry

        lax.fori_loop(0, nsub, block, 0)

        @pl.when(step == n - 1)
        def _():
            if layer == 0:
                dlb_ref[...] = jnp.zeros_like(dlb_ref)
            else:
                dz1 = p0 * p1 * dlb_acc[...]
                dlb_ref[...] = jnp.concatenate([-dz1, dz1], axis=0)

    rev = lambda i: n - 1 - i
    col = lambda c: pl.BlockSpec((ts, HG_W), lambda i: (rev(i), c))
    return pl.pallas_call(
        body, name=name, grid=(n,),
        in_specs=[col(0), col(1), col(2), col(3), col(0),
                  pl.BlockSpec((nsub, HG_W, HG_D), lambda i: (rev(i), 0, 0)),
                  pl.BlockSpec((ts, HG_HEADS * SUB), lambda i: (rev(i), 0)), col(0),
                  pl.BlockSpec((2, HG_W), lambda i: (0, 0)), pl.BlockSpec((1, HG_D), lambda i: (0, 0))],
        out_specs=[pl.BlockSpec((ts, 4 * HG_W), lambda i: (rev(i), 0)),
                   pl.BlockSpec((2, HG_W), lambda i: (0, 0)), pl.BlockSpec((1, HG_D), lambda i: (0, 0))],
        out_shape=[jax.ShapeDtypeStruct((L, 4 * HG_W), BF16), jax.ShapeDtypeStruct((2, HG_W), F32),
                   jax.ShapeDtypeStruct((1, HG_D), F32)],
        scratch_shapes=[pltpu.VMEM((HG_HEADS, HG_D, HG_D), F32), pltpu.VMEM((1, HG_W), F32),
                        pltpu.VMEM((HG_HEADS, HG_D, HG_D), BF16)],
        compiler_params=_params(("arbitrary",)),
    )(hh, hh, hh, hh, o_raw, states, amat, dcat, lbp, gnorm)


def _adamw_body(gp_ref, w_ref, m_ref, v_ref, g_ref, d_ref, mo_ref, vo_ref):
    c1 = 1.0 - ADAM_B1 ** ADAM_STEP
    c2 = 1.0 - ADAM_B2 ** ADAM_STEP
    g = gp_ref[0].astype(F32)
    for k in range(1, N_DEV):
        g = g + gp_ref[k].astype(F32)
    mn = ADAM_B1 * m_ref[...] + (1.0 - ADAM_B1) * g
    vn = ADAM_B2 * v_ref[...] + (1.0 - ADAM_B2) * (g * g)
    m_hat = mn / c1
    v_hat = vn / c2
    g_ref[...] = g
    d_ref[...] = -ADAM_LR * (m_hat / (jnp.sqrt(v_hat) + ADAM_EPS) + ADAM_WD * w_ref[...])
    mo_ref[...] = mn
    vo_ref[...] = vn


def _adamw_layers(gparts, w, m, v, name):
    depth, R, C = w.shape
    tr = _tile(R, 256, 16)
    nr = R // tr

    def body(*refs):
        layer = pl.program_id(0)
        for d in range(depth):
            @pl.when(layer == d)
            def _(d=d):
                _adamw_body(refs[d], *refs[depth:])

    def parts_spec(d):
        return pl.BlockSpec((N_DEV, tr, C),
                            lambda l, i: (0, jnp.where(l == d, i, jnp.where(l < d, 0, nr - 1)), 0))

    blk = pl.BlockSpec((None, tr, C), lambda l, i: (l, i, 0))
    shp = jax.ShapeDtypeStruct((depth, R, C), F32)
    return pl.pallas_call(
        body, name=name, grid=(depth, nr),
        in_specs=[parts_spec(d) for d in range(depth)] + [blk, blk, blk],
        out_specs=[blk, blk, blk, blk], out_shape=[shp, shp, shp, shp],
        compiler_params=_params(("arbitrary", "arbitrary")),
    )(*gparts, w, m, v)


def _adamw(gparts, w, m, v, name):
    R = w.shape[0]
    tr = _tile(R, 1024, 16) if R % 16 == 0 else R

    def body(*refs):
        _adamw_body(*refs)

    row = pl.BlockSpec((tr, LANES), lambda i: (i, 0))
    shp = jax.ShapeDtypeStruct((R, LANES), F32)
    return pl.pallas_call(
        body, name=name, grid=(R // tr,),
        in_specs=[pl.BlockSpec((N_DEV, tr, LANES), lambda i: (0, i, 0)), row, row, row],
        out_specs=[row, row, row, row], out_shape=[shp, shp, shp, shp],
        compiler_params=_params(("parallel",)),
    )(gparts, w, m, v)


def _flip(coord, bit):
    return 1 - coord if bit else coord


def _gather_many(blocks, name):
    n = len(blocks)

    def body(*refs):
        x_refs, out_refs = refs[:n], refs[n:2 * n]
        send_sems, recv_sems, local_sems = refs[2 * n:]
        x, y, c = lax.axis_index("x"), lax.axis_index("y"), lax.axis_index("c")
        me, sibling = (x, y, c), (x, y, 1 - c)
        chips = [(1 - x, y), (x, 1 - y), (1 - x, 1 - y)]

        def slot(a, px, py, pc):
            return out_refs[a].at[4 * px + 2 * py + pc]

        def copy(a, k, blk, to, src=None):
            return pltpu.make_async_remote_copy(
                src_ref=slot(a, *blk) if src is None else src, dst_ref=slot(a, *blk),
                send_sem=send_sems.at[7 * a + k], recv_sem=recv_sems.at[7 * a + k],
                device_id=to, device_id_type=pl.DeviceIdType.MESH)

        mine = [pltpu.make_async_copy(x_refs[a], slot(a, *me), local_sems.at[a]) for a in range(n)]
        for cp in mine:
            cp.start()
        first = [copy(a, 0, me, sibling, src=x_refs[a]) for a in range(n)]
        for j, chip in enumerate(chips):
            first += [copy(a, 1 + j, me, (*chip, c), src=x_refs[a]) for a in range(n)]
        for cp in first:
            cp.start()
        passed = []
        for j, chip in enumerate(chips):
            for a in range(n):
                copy(a, 1 + j, (*chip, c), me).wait_recv()
                fwd = copy(a, 4 + j, (*chip, c), sibling)
                fwd.start()
                passed.append(fwd)
        for a in range(n):
            copy(a, 0, sibling, me).wait_recv()
        for j, chip in enumerate(chips):
            for a in range(n):
                copy(a, 4 + j, (*chip, 1 - c), me).wait_recv()
        for cp in first + passed:
            cp.wait_send()
        for cp in mine:
            cp.wait()

    hbm = pl.BlockSpec(memory_space=pl.ANY)
    return pl.pallas_call(
        body, name=name,
        out_shape=[jax.ShapeDtypeStruct((N_DEV,) + b.shape, b.dtype) for b in blocks],
        in_specs=[hbm] * n, out_specs=[hbm] * n,
        scratch_shapes=[pltpu.SemaphoreType.DMA((7 * n,)), pltpu.SemaphoreType.DMA((7 * n,)),
                        pltpu.SemaphoreType.DMA((n,))],
    )(*blocks)


def _split_start(blocks, chunked, name):
    n = len(blocks)
    lands = [lax.empty(b.shape if chunked else (N_DEV,) + b.shape, b.dtype) for b in blocks]

    def body(*refs):
        x_refs, land_refs = refs[:n], refs[n:2 * n]
        send_sems, recv_sems, token = refs[2 * n], refs[2 * n + 1], refs[-1]
        x, y, c = lax.axis_index("x"), lax.axis_index("y"), lax.axis_index("c")
        me = 4 * x + 2 * y + c
        for a in range(n):
            for k in range(1, N_DEV):
                px, py, pc = _flip(x, k & 4), _flip(y, k & 2), _flip(c, k & 1)
                pltpu.make_async_remote_copy(
                    src_ref=x_refs[a].at[4 * px + 2 * py + pc] if chunked else x_refs[a],
                    dst_ref=land_refs[a].at[me],
                    send_sem=send_sems.at[7 * a + k - 1], recv_sem=recv_sems.at[7 * a + k - 1],
                    device_id=(px, py, pc), device_id_type=pl.DeviceIdType.MESH).start()
        token[...] = jnp.zeros_like(token)

    hbm = pl.BlockSpec(memory_space=pltpu.HBM)
    sem = pl.BlockSpec(memory_space=pltpu.SEMAPHORE)
    outs = pl.pallas_call(
        body, name=name,
        out_shape=(pltpu.SemaphoreType.DMA((7 * n,)), pltpu.SemaphoreType.DMA((7 * n,)),
                   *[pltpu.HBM(b.shape, b.dtype) for b in blocks], *[pltpu.HBM(l.shape, l.dtype) for l in lands],
                   jax.ShapeDtypeStruct((8, LANES), F32)),
        in_specs=[hbm] * (2 * n),
        out_specs=(sem, sem, *[hbm] * (2 * n), pl.BlockSpec(memory_space=pltpu.VMEM)),
        input_output_aliases={i: 2 + i for i in range(2 * n)},
        compiler_params=pltpu.CompilerParams(has_side_effects=pltpu.SideEffectType.DATAFLOW_SIDE_EFFECTING),
    )(*[pltpu.with_memory_space_constraint(b, pltpu.HBM) for b in blocks],
      *[pltpu.with_memory_space_constraint(l, pltpu.HBM) for l in lands])
    return outs[0], outs[1], list(outs[2:2 + n]), list(outs[2 + n:2 + 2 * n]), outs[-1]


def _split_wait(started, chunked, after, name):
    send_sems, recv_sems, blocks, lands, _ = started
    n = len(blocks)

    def body(*refs):
        x_refs, land_refs = refs[:n], refs[n:2 * n]
        send_sems, recv_sems = refs[2 * n], refs[2 * n + 1]
        x, y, c = lax.axis_index("x"), lax.axis_index("y"), lax.axis_index("c")
        for a in range(n):
            for k in range(1, N_DEV):
                px, py, pc = _flip(x, k & 4), _flip(y, k & 2), _flip(c, k & 1)
                copy = pltpu.make_async_remote_copy(
                    src_ref=x_refs[a].at[4 * px + 2 * py + pc] if chunked else x_refs[a],
                    dst_ref=land_refs[a].at[4 * px + 2 * py + pc],
                    send_sem=send_sems.at[7 * a + k - 1], recv_sem=recv_sems.at[7 * a + k - 1],
                    device_id=(px, py, pc), device_id_type=pl.DeviceIdType.MESH)
                copy.wait_send()
                copy.wait_recv()

    hbm = pl.BlockSpec(memory_space=pltpu.HBM)
    sem = pl.BlockSpec(memory_space=pltpu.SEMAPHORE)
    outs = pl.pallas_call(
        body, name=name,
        out_shape=(*[pltpu.HBM(b.shape, b.dtype) for b in blocks], *[pltpu.HBM(l.shape, l.dtype) for l in lands]),
        in_specs=[hbm] * (2 * n) + [sem, sem, pl.BlockSpec(memory_space=pl.ANY)],
        out_specs=[hbm] * (2 * n),
        input_output_aliases={i: i for i in range(2 * n)},
        compiler_params=pltpu.CompilerParams(has_side_effects=pltpu.SideEffectType.DATAFLOW_SIDE_EFFECTING),
    )(*blocks, *lands, send_sems, recv_sems, after)
    me = 4 * lax.axis_index("x") + 2 * lax.axis_index("y") + lax.axis_index("c")
    own = [lax.dynamic_index_in_dim(b, me, 0, keepdims=False) if chunked else b for b in outs[:n]]
    return [lax.dynamic_update_index_in_dim(z, o, me, 0) for z, o in zip(outs[n:], own)]


def _exchange_grads(layer_chunks, small_chunks, rep_block, name):
    flows, inputs = [], []
    for p, per_layer in enumerate(layer_chunks):
        for l, arr in enumerate(per_layer):
            flows.append(("param", p, l))
            inputs.append(arr)
    flows += [("small",), ("rep",)]
    inputs += [small_chunks, rep_block]
    n_par = len(layer_chunks)
    n_in, n_out, nf = len(inputs), n_par + 2, len(flows)

    def body(*refs):
        in_refs, out_refs = refs[:n_in], refs[n_in:n_in + n_out]
        send_sems, recv_sems, local_sems = refs[n_in + n_out:]
        x, y, c = lax.axis_index("x"), lax.axis_index("y"), lax.axis_index("c")
        me = 4 * x + 2 * y + c

        def src(f, dev):
            return in_refs[f] if flows[f][0] == "rep" else in_refs[f].at[dev]

        def dst(f, dev):
            if flows[f][0] == "param":
                _, p, l = flows[f]
                return out_refs[p].at[dev, l]
            return out_refs[n_par + (0 if flows[f][0] == "small" else 1)].at[dev]

        mine = [pltpu.make_async_copy(src(f, me), dst(f, me), local_sems.at[f]) for f in range(nf)]
        for cp in mine:
            cp.start()
        copies = []
        for k in range(1, N_DEV):
            px, py, pc = _flip(x, k & 4), _flip(y, k & 2), _flip(c, k & 1)
            peer = 4 * px + 2 * py + pc
            for f in range(nf):
                sems = dict(send_sem=send_sems.at[7 * f + k - 1], recv_sem=recv_sems.at[7 * f + k - 1],
                            device_id=(px, py, pc), device_id_type=pl.DeviceIdType.MESH)
                send = pltpu.make_async_remote_copy(src_ref=src(f, peer), dst_ref=dst(f, me), **sems)
                recv = pltpu.make_async_remote_copy(src_ref=src(f, peer), dst_ref=dst(f, peer), **sems)
                send.start()
                copies.append((send, recv))
        for send, recv in copies:
            recv.wait_recv()
        for send, recv in copies:
            send.wait_send()
        for cp in mine:
            cp.wait()

    out_shape = [jax.ShapeDtypeStruct((N_DEV, len(pl_)) + pl_[0].shape[1:], pl_[0].dtype) for pl_ in layer_chunks]
    out_shape += [jax.ShapeDtypeStruct(small_chunks.shape, small_chunks.dtype),
                  jax.ShapeDtypeStruct((N_DEV,) + rep_block.shape, rep_block.dtype)]
    hbm = pl.BlockSpec(memory_space=pl.ANY)
    return pl.pallas_call(
        body, name=name, out_shape=out_shape,
        in_specs=[hbm] * n_in, out_specs=[hbm] * n_out,
        scratch_shapes=[pltpu.SemaphoreType.DMA((7 * nf,)), pltpu.SemaphoreType.DMA((7 * nf,)),
                        pltpu.SemaphoreType.DMA((nf,))],
    )(*inputs)


def _pack_rows(size):
    return -(-size // (8 * LANES)) * 8


def _pack(arrs, dtype):
    parts, offs, r = [], [], 0
    for a in arrs:
        flat = a.astype(dtype).reshape(-1)
        nrow = _pack_rows(flat.shape[0])
        flat = jnp.pad(flat, (0, nrow * LANES - flat.shape[0]))
        parts.append(flat.reshape(nrow, LANES))
        offs.append((r, nrow))
        r += nrow
    return jnp.concatenate(parts, axis=0), offs


def _unpack(buf, offs, shapes, lead=()):
    outs = []
    for (r, nrow), shp in zip(offs, shapes):
        size = 1
        for s in shp:
            size *= s
        flat = buf[..., r:r + nrow, :].reshape(lead + (nrow * LANES,))
        outs.append(flat[..., :size].reshape(lead + tuple(shp)))
    return outs


def _cols_from_shards(g, axis):
    return jnp.concatenate([g[j] for j in range(N_DEV)], axis=axis)


BIG = ("w_in", "w_o", "w_up", "w_down")
SMALL_SHARDED = ("meta_tokens", "w_conv", "w_ffn_conv")
REPLICATED = ("hg_lower_bounds", "w_pool", "pool_scale", "hg_norm_g", "ln1_g", "ln1_b", "b_ffn_conv", "ln2_g", "ln2_b")
WEIGHTS = ("meta_tokens", "hg_lower_bounds", "w_in", "w_conv", "w_pool", "pool_scale", "hg_norm_g", "w_o",
           "ln1_g", "ln1_b", "w_up", "w_ffn_conv", "b_ffn_conv", "w_down", "ln2_g", "ln2_b")


def _pool_blockdiag(w_pool_l):
    z = jnp.zeros((POOL_GROUP, POOL_GROUP), w_pool_l.dtype)
    rows = [jnp.concatenate([w_pool_l[g] if h == g else z for h in range(4)], axis=1) for g in range(4)]
    return jnp.concatenate(rows, axis=0)


def _mixer_weights(g_in, g_o):
    w_in = jnp.transpose(g_in, (1, 0, 2)).reshape(D_MODEL, -1)
    w_o = g_o.reshape(-1, D_MODEL)
    return dict(
        w_hg=w_in[:, 768:2816],
        w_cp=jnp.concatenate([w_in[:, 0:768], w_in[:, 2816:3072]], axis=1),
        w_o=jnp.concatenate([w_o[256:768], w_o[0:256], w_o[768:1024]], axis=0))


def _ffn_weights(g_up, g_down):
    return dict(w_up=jnp.transpose(g_up, (1, 0, 2)).reshape(D_MODEL, -1), w_down=g_down.reshape(-1, D_MODEL))


def kernel(x, meta_tokens, hg_lower_bounds, w_in, w_conv, w_pool, pool_scale, hg_norm_g, w_o, ln1_g, ln1_b, w_up, w_ffn_conv, b_ffn_conv, w_down, ln2_g, ln2_b, loss_target, m_meta_tokens, m_hg_lower_bounds, m_w_in, m_w_conv, m_w_pool, m_pool_scale, m_hg_norm_g, m_w_o, m_ln1_g, m_ln1_b, m_w_up, m_w_ffn_conv, m_b_ffn_conv, m_w_down, m_ln2_g, m_ln2_b, v_meta_tokens, v_hg_lower_bounds, v_w_in, v_w_conv, v_w_pool, v_pool_scale, v_hg_norm_g, v_w_o, v_ln1_g, v_ln1_b, v_w_up, v_w_ffn_conv, v_b_ffn_conv, v_w_down, v_ln2_g, v_ln2_b):
    W = dict(meta_tokens=meta_tokens, hg_lower_bounds=hg_lower_bounds, w_in=w_in, w_conv=w_conv, w_pool=w_pool,
             pool_scale=pool_scale, hg_norm_g=hg_norm_g, w_o=w_o, ln1_g=ln1_g, ln1_b=ln1_b, w_up=w_up,
             w_ffn_conv=w_ffn_conv, b_ffn_conv=b_ffn_conv, w_down=w_down, ln2_g=ln2_g, ln2_b=ln2_b)
    M = dict(meta_tokens=m_meta_tokens, hg_lower_bounds=m_hg_lower_bounds, w_in=m_w_in, w_conv=m_w_conv,
             w_pool=m_w_pool, pool_scale=m_pool_scale, hg_norm_g=m_hg_norm_g, w_o=m_w_o, ln1_g=m_ln1_g,
             ln1_b=m_ln1_b, w_up=m_w_up, w_ffn_conv=m_w_ffn_conv, b_ffn_conv=m_b_ffn_conv, w_down=m_w_down,
             ln2_g=m_ln2_g, ln2_b=m_ln2_b)
    V = dict(meta_tokens=v_meta_tokens, hg_lower_bounds=v_hg_lower_bounds, w_in=v_w_in, w_conv=v_w_conv,
             w_pool=v_w_pool, pool_scale=v_pool_scale, hg_norm_g=v_hg_norm_g, w_o=v_w_o, ln1_g=v_ln1_g,
             ln1_b=v_ln1_b, w_up=v_w_up, w_ffn_conv=v_w_ffn_conv, b_ffn_conv=v_b_ffn_conv, w_down=v_w_down,
             ln2_g=v_ln2_g, ln2_b=v_ln2_b)
    assert x.shape[0] == 1 and x.shape[2] == D_MODEL and w_in.shape[0] == DEPTH
    seq = x.shape[1]
    L = -(-(seq + N_META) // ROW_ALIGN) * ROW_ALIGN

    small_pack, small_offs = _pack([W[n] for n in SMALL_SHARDED], F32)
    shards = {n: [W[n][l].astype(BF16) for l in range(DEPTH)] for n in BIG}
    g_in0, g_o0, small_all = _gather_many([shards["w_in"][0], shards["w_o"][0], small_pack], "gather_weights")
    full = {}
    for n, a in zip(SMALL_SHARDED, _unpack(small_all, small_offs, [W[n].shape for n in SMALL_SHARDED], (N_DEV,))):
        full[n] = _cols_from_shards(a, 1)
    order = (small_all[0, 0, 0] * 0.0).astype(BF16)
    ffn0_started = _split_start([shards["w_up"][0] + order, shards["w_down"][0] + order], False, "gather_ffn0_start")
    order = ffn0_started[4][0, 0].astype(BF16)
    layer1_started = _split_start([shards[n][1] + order for n in BIG], False, "gather_layer1_start")
    lb_in = hg_lower_bounds + layer1_started[4][0, 0]

    pad_rows = L - N_META - seq
    xp = jnp.concatenate([full["meta_tokens"], x[0], jnp.zeros((pad_rows, D_MODEL), F32)], axis=0)
    tgt = jnp.concatenate([jnp.zeros((N_META, D_MODEL), F32), loss_target[0], jnp.zeros((pad_rows, D_MODEL), F32)], axis=0)

    saved = []
    h_in, h_in_b = xp, xp.astype(BF16)
    for l in range(DEPTH):
        if l == 0:
            lw = _mixer_weights(g_in0, g_o0)
        else:
            g_in, g_o, g_up, g_down = _split_wait(layer1_started, False, h_in_b, "gather_layer1_wait")
            lw = {**_mixer_weights(g_in, g_o), **_ffn_weights(g_up, g_down)}
        wc = full["w_conv"][l].T
        wblk = _pool_blockdiag(w_pool[l]).astype(BF16)
        ps = pool_scale[l][None, :]
        gn = hg_norm_g[l][None, :]
        wf = full["w_ffn_conv"][l].T
        bf = b_ffn_conv[l][None, :]
        hh = _matmul(h_in_b, lw["w_hg"], "nn", F32, f"fwd_hg_{l}")
        hc = _matmul(h_in_b, lw["w_cp"], "nn", F32, f"fwd_cp_{l}")
        y_hg, o_raw, states, amat = _hgrn_fwd(hh, lb_in if l == 0 else hg_lower_bounds, gn, l, f"hgrn_fwd_{l}")
        y_cp = _cp_fwd(hc, wc, wblk, ps, f"convpool_fwd_{l}")
        cat = jnp.concatenate([y_hg, y_cp], axis=1)
        z1, x1, x1_b = _matmul_ln(cat, lw["w_o"], h_in, ln1_g[l][None, :], ln1_b[l][None, :], f"fwd_o_ln1_{l}")
        if l == 0:
            lw.update(_ffn_weights(*_split_wait(ffn0_started, False, x1_b, "gather_ffn0_wait")))
        up = _matmul(x1_b, lw["w_up"], "nn", BF16, f"fwd_up_{l}")
        a, u = _ffn_act_fwd(up, wf, bf, f"ffn_fwd_{l}")
        saved.append(dict(lw=lw, wc=wc, wblk=wblk, ps=ps, gn=gn, wf=wf, bf=bf, x_b=h_in_b, hh=hh, hc=hc,
                          o_raw=o_raw, states=states, amat=amat, cat=cat, z1=z1, x1_b=x1_b, up=up, u=u, a=a))
        if l < DEPTH - 1:
            saved[l]["z2"], h_in, h_in_b = _matmul_ln(a, lw["w_down"], x1, ln2_g[l][None, :], ln2_b[l][None, :],
                                                     f"fwd_down_ln2_{l}")
        else:
            saved[l]["z2"], dy, loss_part = _matmul_ln(a, lw["w_down"], x1, ln2_g[l][None, :], ln2_b[l][None, :],
                                                       f"fwd_down_ln2_loss_{l}", loss=(tgt, seq))

    loss = lax.psum(loss_part[0, 0], ("x", "y", "c"))

    G = {}
    per_layer = {n: [None] * DEPTH for n in ("w_conv", "w_pool", "pool_scale", "hg_norm_g", "ln1_g", "ln1_b",
                                             "w_ffn_conv", "b_ffn_conv", "ln2_g", "ln2_b")}
    ffn_started, mix_started = [None] * DEPTH, [None] * DEPTH
    order = jnp.zeros((), F32)
    dlb_total = jnp.zeros((DEPTH, HG_W), F32)
    for l in reversed(range(DEPTH)):
        s = saved[l]
        lw = s["lw"]
        dz2_b, dg2, db2 = _ln_bwd(s["z2"], dy, ln2_g[l][None, :] + order, f"ln2_bwd_{l}")
        da = _matmul(dz2_b, lw["w_down"], "nt", BF16, f"bwd_da_{l}")
        d_w_down = _matmul(s["a"], dz2_b, "tn", BF16, f"wgrad_down_{l}")
        dup, dwf, dbf = _ffn_act_bwd(s["up"], s["u"], da, s["wf"], f"ffn_bwd_{l}")
        dx1 = _matmul(dup, lw["w_up"], "nt", BF16, f"bwd_dx1_{l}", res=dz2_b, alpha=ALPHA)
        d_w_up = _matmul(s["x1_b"], dup, "tn", BF16, f"wgrad_up_{l}")
        ffn_started[l] = _split_start([jnp.transpose(d_w_up.reshape(D_MODEL, N_DEV, -1), (1, 0, 2)),
                                       d_w_down.reshape(N_DEV, -1, D_MODEL)], True, f"scatter_ffn{l}_start")
        order = ffn_started[l][4][0, 0]
        dz1_b, dg1, db1 = _ln_bwd(s["z1"], dx1, ln1_g[l][None, :] + order, f"ln1_bwd_{l}")
        dcat = _matmul(dz1_b, lw["w_o"], "nt", BF16, f"bwd_dcat_{l}")
        d_w_o = _matmul(s["cat"], dz1_b, "tn", BF16, f"wgrad_o_{l}")
        dhh, dlb, dgn = _hgrn_bwd(s["hh"], s["o_raw"], s["states"], s["amat"], dcat, hg_lower_bounds, s["gn"], l,
                                  f"hgrn_bwd_{l}")
        dhc, dwc, dwblk, dps = _cp_bwd(s["hc"], dcat, s["wc"], s["wblk"], s["ps"], f"convpool_bwd_{l}")
        d_w_hg = _matmul(s["x_b"], dhh, "tn", BF16, f"wgrad_hg_{l}")
        d_w_cp = _matmul(s["x_b"], dhc, "tn", BF16, f"wgrad_cp_{l}")
        d_w_in = jnp.concatenate([d_w_cp[:, 0:768], d_w_hg, d_w_cp[:, 768:1024]], axis=1)
        mix_chunks = [jnp.transpose(d_w_in.reshape(D_MODEL, N_DEV, -1), (1, 0, 2)),
                      jnp.concatenate([d_w_o[512:768], d_w_o[0:512], d_w_o[768:1024]], axis=0).reshape(N_DEV, -1, D_MODEL)]
        mix_started[l] = _split_start(mix_chunks, True, f"scatter_mix{l}_start")
        order = mix_started[l][4][0, 0]
        dx_a = _matmul(dhh, lw["w_hg"] + order.astype(BF16), "nt", F32, f"bwd_dx_hg_{l}", res=dz1_b, alpha=ALPHA)
        dx = _matmul(dhc, lw["w_cp"], "nt", F32, f"bwd_dx_cp_{l}", res=dx_a, alpha=1.0)
        per_layer["w_conv"][l] = dwc.T
        per_layer["w_ffn_conv"][l] = dwf.T
        per_layer["b_ffn_conv"][l] = dbf[0]
        per_layer["w_pool"][l] = jnp.stack([dwblk[g * 64:(g + 1) * 64, g * 64:(g + 1) * 64] for g in range(4)], axis=0)
        per_layer["pool_scale"][l] = dps[0]
        per_layer["hg_norm_g"][l] = dgn[0]
        per_layer["ln1_g"][l], per_layer["ln1_b"][l] = dg1[0], db1[0]
        per_layer["ln2_g"][l], per_layer["ln2_b"][l] = dg2[0], db2[0]
        dlb_total = dlb_total + dlb
        dy = dx
    for n, parts in per_layer.items():
        G[n] = jnp.stack(parts, axis=0)
    G["hg_lower_bounds"] = dlb_total
    grad_x = dy[N_META:N_META + seq][None]

    def shard_major(g, lead):
        g = g.reshape(g.shape[:lead] + (N_DEV, -1) + g.shape[lead + 1:])
        g = jnp.moveaxis(g, lead, 0).reshape(N_DEV, -1)
        nrow = _pack_rows(g.shape[1])
        return jnp.pad(g, ((0, 0), (0, nrow * LANES - g.shape[1]))).reshape(N_DEV, nrow, LANES)

    small_chunks = jnp.concatenate([shard_major(dy[0:N_META], 1), shard_major(G["w_conv"], 1),
                                    shard_major(G["w_ffn_conv"], 1)], axis=1)
    w_small, _ = _pack([W[n] for n in SMALL_SHARDED], F32)
    rep_pack, rep_offs = _pack([G[n] for n in REPLICATED], F32)
    small_recv, rep_all = _exchange_grads([], small_chunks, rep_pack, "exchange_grads")
    parts = {n: [] for n in BIG}
    for l in range(DEPTH):
        up_l, down_l = _split_wait(ffn_started[l], True, rep_all, f"scatter_ffn{l}_wait")
        in_l, o_l = _split_wait(mix_started[l], True, rep_all, f"scatter_mix{l}_wait")
        for n, a in zip(BIG, (in_l, o_l, up_l, down_l)):
            parts[n].append(a)

    res = {k: {} for k in ("grad", "delta", "new_m", "new_v")}
    kinds = ("grad", "delta", "new_m", "new_v")
    for n in BIG:
        for kind, a in zip(kinds, _adamw_layers(parts[n], W[n], M[n], V[n], f"adamw_{n}")):
            res[kind][n] = a
    m_small, _ = _pack([M[n] for n in SMALL_SHARDED], F32)
    v_small, _ = _pack([V[n] for n in SMALL_SHARDED], F32)
    outs_small = _adamw(small_recv, w_small, m_small, v_small, "adamw_small_sharded")
    w_rep, _ = _pack([W[n] for n in REPLICATED], F32)
    m_rep, _ = _pack([M[n] for n in REPLICATED], F32)
    v_rep, _ = _pack([V[n] for n in REPLICATED], F32)
    outs_rep = _adamw(rep_all, w_rep, m_rep, v_rep, "adamw_replicated")
    for kind, b_sm, b_rep in zip(kinds, outs_small, outs_rep):
        for n, a in zip(SMALL_SHARDED, _unpack(b_sm, small_offs, [W[n].shape for n in SMALL_SHARDED])):
            res[kind][n] = a
        for n, a in zip(REPLICATED, _unpack(b_rep, rep_offs, [W[n].shape for n in REPLICATED])):
            res[kind][n] = a

    return (loss, grad_x, *[res["grad"][n] for n in WEIGHTS], *[res["delta"][n] for n in WEIGHTS],
            *[res["new_m"][n] for n in WEIGHTS], *[res["new_v"][n] for n in WEIGHTS])
```

```python
import jax
import jax.numpy as jnp
from jax import lax
from jax.experimental import pallas as pl
from jax.experimental.pallas import tpu as pltpu

F32 = jnp.float32
BF16 = jnp.bfloat16

N_DEV = 8
D_MODEL = 1024
N_META = 16
DEPTH = 2
CONV_W = 256
HG_W = 512
HG_D = 128
HG_HEADS = 4
POOL_W = 256
POOL_GROUP = 64
D_FF = 2816
ALPHA = (2 * DEPTH) ** 0.25
LN_EPS = 1e-5
RMS_EPS = 1e-6
F_FLOOR = 1e-30
Q_SCALE = HG_D ** -0.5
SUB = 16
SEQ_TILE = 192
FFN_TILE = 192
CP_TILE_CAP = 768
ROW_ALIGN = 192
LANES = 128
VMEM_LIMIT = 48 * 1024 * 1024
MATMUL_VMEM_BUDGET = 38 * 1024 * 1024

ADAM_LR = 0.001
ADAM_B1 = 0.9
ADAM_B2 = 0.999
ADAM_EPS = 1e-08
ADAM_WD = 0.01
ADAM_STEP = 10


def _tile(n, cap, mult):
    best = 0
    for t in range(mult, min(n, cap) + 1, mult):
        if n % t == 0:
            best = t
    assert best > 0, (n, cap, mult)
    return best


def _params(sem, vmem=VMEM_LIMIT):
    return pltpu.CompilerParams(dimension_semantics=sem, vmem_limit_bytes=vmem)


def _dnt(a, b):
    return lax.dot_general(a, b, (((1,), (1,)), ((), ())), preferred_element_type=F32)


def _dtn(a, b):
    return lax.dot_general(a, b, (((0,), (0,)), ((), ())), preferred_element_type=F32)


def _dnn(a, b):
    return jnp.dot(a, b, preferred_element_type=F32)


def _sigmoid(x):
    return jax.nn.sigmoid(x)


def _matmul(a, b, mode, out_dtype, name, res=None, alpha=1.0):
    if mode == "tn":
        K, M = a.shape
    else:
        M, K = a.shape
    N = b.shape[0] if mode == "nt" else b.shape[1]
    out_bytes = jnp.dtype(out_dtype).itemsize
    if mode == "tn" and M % 512 == 0 and N % 512 == 0:
        tk, tm, tn = K, 512, 512
        nk, use_acc = 1, False
    else:
        tn = _tile(N, 1536, LANES)
        tk = _tile(K, 1536, 16) if mode == "tn" else _tile(K, 2816, LANES)
        nk = K // tk
        use_acc = nk > 1 and out_dtype != F32
        tm = M
        for cap in (1536, 768, 384):
            tm = _tile(M, cap, 16)
            blocks = 2 * (a.dtype.itemsize * tm * tk + b.dtype.itemsize * tn * tk + out_bytes * tm * tn
                          + (res.dtype.itemsize * tm * tn if res is not None else 0)) + (4 * tm * tn if use_acc else 0)
            if blocks <= MATMUL_VMEM_BUDGET:
                break
    dims = {"nn": ((1,), (0,)), "nt": ((1,), (1,)), "tn": ((0,), (0,))}[mode]

    def body(*refs):
        a_ref, b_ref = refs[0], refs[1]
        r_ref = refs[2] if res is not None else None
        o_ref = refs[3] if res is not None else refs[2]
        acc = refs[-1] if use_acc else o_ref
        k = pl.program_id(2)
        p = lax.dot_general(a_ref[...].astype(BF16), b_ref[...].astype(BF16), (dims, ((), ())),
                            preferred_element_type=F32)

        def finish(r):
            if r_ref is not None:
                r = r + alpha * r_ref[...].astype(F32)
            o_ref[...] = r.astype(out_dtype)

        if nk == 1:
            finish(p)
        else:
            @pl.when(k == 0)
            def _():
                acc[...] = p

            @pl.when((k > 0) & (k < nk - 1))
            def _():
                acc[...] += p

            @pl.when(k == nk - 1)
            def _():
                finish(acc[...] + p)

    if mode == "tn":
        a_spec = pl.BlockSpec((tk, tm), lambda i, j, k: (k, i))
    else:
        a_spec = pl.BlockSpec((tm, tk), lambda i, j, k: (i, k))
    if mode == "nt":
        b_spec = pl.BlockSpec((tn, tk), lambda i, j, k: (j, k))
    else:
        b_spec = pl.BlockSpec((tk, tn), lambda i, j, k: (k, j))
    in_specs = [a_spec, b_spec]
    args = [a, b]
    if res is not None:
        in_specs.append(pl.BlockSpec((tm, tn), lambda i, j, k: (i, j)))
        args.append(res)
    return pl.pallas_call(
        body, name=name,
        grid=(M // tm, N // tn, nk),
        in_specs=in_specs,
        out_specs=pl.BlockSpec((tm, tn), lambda i, j, k: (i, j)),
        out_shape=jax.ShapeDtypeStruct((M, N), out_dtype),
        scratch_shapes=[pltpu.VMEM((tm, tn), F32)] if use_acc else [],
        compiler_params=_params(("parallel", "parallel", "arbitrary")),
    )(*args)


def _matmul_ln(a, w, x, g, b, name, loss=None):
    L, K = a.shape
    D = w.shape[1]
    tr = L
    for cap in (1536, 768, 384):
        tr = _tile(L, cap, 16)
        if 2 * (2 * tr * K + 2 * K * D + 4 * tr * D * (4 if loss else 3) + 2 * tr * D) <= MATMUL_VMEM_BUDGET:
            break

    def body(*refs):
        a_ref, w_ref, x_ref, g_ref, b_ref = refs[:5]
        z = ALPHA * x_ref[...] + _dnn(a_ref[...], w_ref[...])
        mu = jnp.mean(z, axis=-1, keepdims=True)
        zc = z - mu
        var = jnp.mean(zc * zc, axis=-1, keepdims=True)
        y = zc * lax.rsqrt(var + LN_EPS) * g_ref[...] + b_ref[...]
        if loss is None:
            z_ref, y_ref, yb_ref = refs[5:]
            y_ref[...] = y
            yb_ref[...] = y.astype(BF16)
        else:
            t_ref, z_ref, dy_ref, loss_ref = refs[5:]
            i = pl.program_id(0)

            @pl.when(i == 0)
            def _():
                loss_ref[...] = jnp.zeros_like(loss_ref)

            r = i * tr + lax.broadcasted_iota(jnp.int32, (tr, D), 0)
            valid = (r >= N_META) & (r < N_META + loss[1])
            e = jnp.where(valid, y - t_ref[...], 0.0)
            dy_ref[...] = e * (1.0 / D)
            s = jnp.sum(jnp.sum(e * e, axis=-1, keepdims=True), axis=0, keepdims=True)
            loss_ref[...] += (0.5 / D) * s
        z_ref[...] = z.astype(BF16)

    row = pl.BlockSpec((tr, D), lambda i: (i, 0))
    vec = pl.BlockSpec((1, D), lambda i: (0, 0))
    in_specs = [pl.BlockSpec((tr, K), lambda i: (i, 0)), pl.BlockSpec((K, D), lambda i: (0, 0)), row, vec, vec]
    f32_rows = jax.ShapeDtypeStruct((L, D), F32)
    b16_rows = jax.ShapeDtypeStruct((L, D), BF16)
    if loss is None:
        args, out_specs = [a, w, x, g, b], [row, row, row]
        out_shape = [b16_rows, f32_rows, b16_rows]
    else:
        args, in_specs = [a, w, x, g, b, loss[0]], in_specs + [row]
        out_specs = [row, row, pl.BlockSpec((1, 1), lambda i: (0, 0))]
        out_shape = [b16_rows, f32_rows, jax.ShapeDtypeStruct((1, 1), F32)]
    return pl.pallas_call(
        body, name=name, grid=(L // tr,), in_specs=in_specs, out_specs=out_specs, out_shape=out_shape,
        compiler_params=_params(("arbitrary",) if loss else ("parallel",)),
    )(*args)


def _ln_bwd(z, dy, g, name):
    L, D = z.shape
    tr = _tile(L, 768, 16)

    def body(z_ref, dy_ref, g_ref, dzb_ref, dg_ref, db_ref):
        @pl.when(pl.program_id(0) == 0)
        def _():
            dg_ref[...] = jnp.zeros_like(dg_ref)
            db_ref[...] = jnp.zeros_like(db_ref)

        z = z_ref[...].astype(F32)
        mu = jnp.mean(z, axis=-1, keepdims=True)
        zc = z - mu
        var = jnp.mean(zc * zc, axis=-1, keepdims=True)
        rstd = lax.rsqrt(var + LN_EPS)
        xhat = zc * rstd
        dy = dy_ref[...].astype(F32)
        dxh = dy * g_ref[...]
        m1 = jnp.mean(dxh, axis=-1, keepdims=True)
        m2 = jnp.mean(dxh * xhat, axis=-1, keepdims=True)
        dz = rstd * (dxh - m1 - xhat * m2)
        dzb_ref[...] = dz.astype(BF16)
        dg_ref[...] += jnp.sum(dy * xhat, axis=0, keepdims=True)
        db_ref[...] += jnp.sum(dy, axis=0, keepdims=True)

    row = pl.BlockSpec((tr, D), lambda i: (i, 0))
    vec = pl.BlockSpec((1, D), lambda i: (0, 0))
    return pl.pallas_call(
        body, name=name, grid=(L // tr,),
        in_specs=[row, row, vec], out_specs=[row, vec, vec],
        out_shape=[jax.ShapeDtypeStruct((L, D), BF16),
                   jax.ShapeDtypeStruct((1, D), F32), jax.ShapeDtypeStruct((1, D), F32)],
        compiler_params=_params(("arbitrary",)),
    )(z, dy, g)


def _shift_down(x, prev, k):
    T, C = x.shape
    rot = pltpu.roll(jnp.concatenate([prev, x], axis=0).reshape(T // 8 + 1, 8, C), k, 1)
    sub = lax.broadcasted_iota(jnp.int32, (T // 8, 8, C), 1)
    return jnp.where(sub < k, rot[:-1], rot[1:]).reshape(T, C)


def _shift_up(x, nxt, k):
    T, C = x.shape
    rot = pltpu.roll(jnp.concatenate([x, nxt], axis=0).reshape(T // 8 + 1, 8, C), 8 - k, 1)
    sub = lax.broadcasted_iota(jnp.int32, (T // 8, 8, C), 1)
    return jnp.where(sub >= 8 - k, rot[1:], rot[:-1]).reshape(T, C)


def _conv3(x, prev, w, b):
    return w[2:3, :] * x + w[1:2, :] * _shift_down(x, prev, 1) + w[0:1, :] * _shift_down(x, prev, 2) + b


def _ffn_act_fwd(up, w, b, name):
    L, C = up.shape
    F = C // 2
    ts = FFN_TILE
    n = L // ts

    def body(up_ref, pv_ref, w_ref, b_ref, a_ref, u_ref):
        i = pl.program_id(0)
        x = up_ref[...].astype(F32)
        prev = jnp.where(i > 0, pv_ref[...].astype(F32)[8:16], 0.0)
        u = _conv3(x, prev, w_ref[...], b_ref[...])
        u_ref[...] = u.astype(BF16)
        gate = u[:, :F]
        a_ref[...] = (gate * _sigmoid(gate) * u[:, F:]).astype(BF16)

    return pl.pallas_call(
        body, name=name, grid=(n,),
        in_specs=[pl.BlockSpec((ts, C), lambda i: (i, 0)),
                  pl.BlockSpec((16, C), lambda i: (jnp.maximum(i * (ts // 16) - 1, 0), 0)),
                  pl.BlockSpec((3, C), lambda i: (0, 0)), pl.BlockSpec((1, C), lambda i: (0, 0))],
        out_specs=[pl.BlockSpec((ts, F), lambda i: (i, 0)), pl.BlockSpec((ts, C), lambda i: (i, 0))],
        out_shape=[jax.ShapeDtypeStruct((L, F), BF16), jax.ShapeDtypeStruct((L, C), BF16)],
        compiler_params=_params(("parallel",)),
    )(up, up, w, b)


def _ffn_act_bwd(up, u, da, w, name):
    L, C = up.shape
    F = C // 2
    ts = FFN_TILE
    n = L // ts
    last16 = L // 16 - 1

    def du_of(u, da):
        gate, val = u[:, :F], u[:, F:]
        sg = _sigmoid(gate)
        dgate = da * val * (sg * (1.0 + gate * (1.0 - sg)))
        dval = da * (gate * sg)
        return jnp.concatenate([dgate, dval], axis=1)

    def body(up_ref, u_ref, un_ref, da_ref, dan_ref, w_ref, dup_ref, dw_ref, db_ref):
        i = pl.program_id(0)

        @pl.when(i == 0)
        def _():
            dw_ref[...] = jnp.zeros_like(dw_ref)
            db_ref[...] = jnp.zeros_like(db_ref)

        w = w_ref[...]
        x = up_ref[...].astype(F32)
        du = du_of(u_ref[...].astype(F32), da_ref[...].astype(F32))
        dun = jnp.where(i < n - 1, du_of(un_ref[...].astype(F32)[0:8], dan_ref[...].astype(F32)[0:8]), 0.0)
        du1 = _shift_up(du, dun, 1)
        du2 = _shift_up(du, dun, 2)
        dup_ref[...] = (w[2:3, :] * du + w[1:2, :] * du1 + w[0:1, :] * du2).astype(BF16)
        dw_ref[...] += jnp.concatenate([jnp.sum(x * du2, axis=0, keepdims=True),
                                        jnp.sum(x * du1, axis=0, keepdims=True),
                                        jnp.sum(x * du, axis=0, keepdims=True)], axis=0)
        db_ref[...] += jnp.sum(du, axis=0, keepdims=True)

    nxt = lambda i: (jnp.minimum((i + 1) * (ts // 16), last16), 0)
    return pl.pallas_call(
        body, name=name, grid=(n,),
        in_specs=[pl.BlockSpec((ts, C), lambda i: (i, 0)),
                  pl.BlockSpec((ts, C), lambda i: (i, 0)), pl.BlockSpec((16, C), nxt),
                  pl.BlockSpec((ts, F), lambda i: (i, 0)), pl.BlockSpec((16, F), nxt),
                  pl.BlockSpec((3, C), lambda i: (0, 0))],
        out_specs=[pl.BlockSpec((ts, C), lambda i: (i, 0)), pl.BlockSpec((3, C), lambda i: (0, 0)),
                   pl.BlockSpec((1, C), lambda i: (0, 0))],
        out_shape=[jax.ShapeDtypeStruct((L, C), BF16), jax.ShapeDtypeStruct((3, C), F32),
                   jax.ShapeDtypeStruct((1, C), F32)],
        compiler_params=_params(("arbitrary",)),
    )(up, u, u, da, da, w)


def _pool_window(ext, tile_rows, first_row, lead):
    T = ext.shape[0]
    sh = (lambda x, k: pltpu.roll(x, T - k, 0)) if lead else (lambda x, k: pltpu.roll(x, k, 0))
    r2 = ext + sh(ext, 1)
    r4 = r2 + sh(r2, 2)
    r8 = r4 + sh(r4, 4)
    r16 = r8 + sh(r8, 8)
    lo = 0 if lead else 16
    grp = lax.broadcasted_iota(jnp.int32, (tile_rows, POOL_W), 1) // POOL_GROUP
    pick = lambda a, b, c, d: jnp.where(grp == 0, a, jnp.where(grp == 1, b, jnp.where(grp == 2, c, d)))
    win = pick(r2[lo:lo + tile_rows], r4[lo:lo + tile_rows], r8[lo:lo + tile_rows], r16[lo:lo + tile_rows])
    return win, pick(2.0, 4.0, 8.0, 16.0)


def _pool_count(first_row, rows, wlen):
    t1 = (first_row + lax.broadcasted_iota(jnp.int32, (rows, POOL_W), 0) + 1).astype(F32)
    return jnp.minimum(t1, wlen)


def _cp_fwd(hc, wc, wblk, pscale, name):
    L = hc.shape[0]
    ts = _tile(L, CP_TILE_CAP, 16)
    n = L // ts

    def body(h_ref, hp_ref, wc_ref, wb_ref, ps_ref, y_ref):
        i = pl.program_id(0)
        h = h_ref[...]
        hp = jnp.where(i > 0, hp_ref[...], 0.0)
        cb, cc, cv, pv = h[:, 0:256], h[:, 256:512], h[:, 512:768], h[:, 768:1024]
        p = cc * cv
        pp = hp[8:16, 256:512] * hp[8:16, 512:768]
        w = wc_ref[...]
        conv = w[2:3, :] * p + w[1:2, :] * _shift_down(p, pp, 1) + w[0:1, :] * _shift_down(p, pp, 2)
        y_conv = cb * conv
        ext = jnp.concatenate([hp[:, 768:1024], pv], axis=0)
        win, wlen = _pool_window(ext, ts, i * ts, False)
        d = win / _pool_count(i * ts, ts, wlen) - pv
        y_pool = _dnn(d.astype(BF16), wb_ref[...]) * ps_ref[...]
        y_ref[...] = jnp.concatenate([y_conv, y_pool], axis=1).astype(BF16)

    return pl.pallas_call(
        body, name=name, grid=(n,),
        in_specs=[pl.BlockSpec((ts, 1024), lambda i: (i, 0)),
                  pl.BlockSpec((16, 1024), lambda i: (jnp.maximum(i * (ts // 16) - 1, 0), 0)),
                  pl.BlockSpec((3, 256), lambda i: (0, 0)), pl.BlockSpec((256, 256), lambda i: (0, 0)),
                  pl.BlockSpec((1, 256), lambda i: (0, 0))],
        out_specs=pl.BlockSpec((ts, 512), lambda i: (i, 0)),
        out_shape=jax.ShapeDtypeStruct((L, 512), BF16),
        compiler_params=_params(("parallel",)),
    )(hc, hc, wc, wblk, pscale)


def _cp_bwd(hc, dcat, wc, wblk, pscale, name):
    L = hc.shape[0]
    ts = _tile(L, CP_TILE_CAP, 16)
    n = L // ts
    last16 = L // 16 - 1

    def body(h_ref, hp_ref, hn_ref, dy_ref, dyn_ref, wc_ref, wb_ref, ps_ref,
             dh_ref, dwc_ref, dwb_ref, dps_ref):
        i = pl.program_id(0)

        @pl.when(i == 0)
        def _():
            dwc_ref[...] = jnp.zeros_like(dwc_ref)
            dwb_ref[...] = jnp.zeros_like(dwb_ref)
            dps_ref[...] = jnp.zeros_like(dps_ref)

        h = h_ref[...]
        hp = jnp.where(i > 0, hp_ref[...], 0.0)
        hn = hn_ref[...]
        dy = dy_ref[...].astype(F32)
        dyn = jnp.where(i < n - 1, dyn_ref[...].astype(F32), 0.0)
        cb, cc, cv, pv = h[:, 0:256], h[:, 256:512], h[:, 512:768], h[:, 768:1024]
        w = wc_ref[...]
        p = cc * cv
        pp = hp[8:16, 256:512] * hp[8:16, 512:768]
        p1 = _shift_down(p, pp, 1)
        p2 = _shift_down(p, pp, 2)
        conv = w[2:3, :] * p + w[1:2, :] * p1 + w[0:1, :] * p2
        dyc = dy[:, 0:256]
        dcb = dyc * conv
        dconv = dyc * cb
        dconv_n = dyn[0:8, 0:256] * hn[0:8, 0:256]
        dc1 = _shift_up(dconv, dconv_n, 1)
        dc2 = _shift_up(dconv, dconv_n, 2)
        dp = w[2:3, :] * dconv + w[1:2, :] * dc1 + w[0:1, :] * dc2
        dwc_ref[...] += jnp.concatenate([jnp.sum(p * dc2, axis=0, keepdims=True),
                                         jnp.sum(p * dc1, axis=0, keepdims=True),
                                         jnp.sum(p * dconv, axis=0, keepdims=True)], axis=0)
        ps = ps_ref[...]
        wb = wb_ref[...]
        ext = jnp.concatenate([hp[:, 768:1024], pv], axis=0)
        win, wlen = _pool_window(ext, ts, i * ts, False)
        d = win / _pool_count(i * ts, ts, wlen) - pv
        db = d.astype(BF16)
        dyp = dy[:, 256:512]
        dps_ref[...] += jnp.sum(dyp * _dnn(db, wb), axis=0, keepdims=True)
        dypre = (dyp * ps).astype(BF16)
        dwb_ref[...] += _dtn(db, dypre)
        dd = _dnt(dypre, wb)
        ddn = _dnt((dyn[:, 256:512] * ps).astype(BF16), wb)
        e = dd / _pool_count(i * ts, ts, wlen)
        en = ddn / _pool_count((i + 1) * ts, 16, wlen[0:16])
        lead, _ = _pool_window(jnp.concatenate([e, en], axis=0), ts, i * ts, True)
        dpv = lead - dd
        dh_ref[...] = jnp.concatenate([dcb, dp * cv, dp * cc, dpv], axis=1).astype(BF16)

    return pl.pallas_call(
        body, name=name, grid=(n,),
        in_specs=[pl.BlockSpec((ts, 1024), lambda i: (i, 0)),
                  pl.BlockSpec((16, 1024), lambda i: (jnp.maximum(i * (ts // 16) - 1, 0), 0)),
                  pl.BlockSpec((16, 1024), lambda i: (jnp.minimum((i + 1) * (ts // 16), last16), 0)),
                  pl.BlockSpec((ts, 512), lambda i: (i, 1)),
                  pl.BlockSpec((16, 512), lambda i: (jnp.minimum((i + 1) * (ts // 16), last16), 1)),
                  pl.BlockSpec((3, 256), lambda i: (0, 0)), pl.BlockSpec((256, 256), lambda i: (0, 0)),
                  pl.BlockSpec((1, 256), lambda i: (0, 0))],
        out_specs=[pl.BlockSpec((ts, 1024), lambda i: (i, 0)), pl.BlockSpec((3, 256), lambda i: (0, 0)),
                   pl.BlockSpec((256, 256), lambda i: (0, 0)), pl.BlockSpec((1, 256), lambda i: (0, 0))],
        out_shape=[jax.ShapeDtypeStruct((L, 1024), BF16), jax.ShapeDtypeStruct((3, 256), F32),
                   jax.ShapeDtypeStruct((256, 256), F32), jax.ShapeDtypeStruct((1, 256), F32)],
        compiler_params=_params(("arbitrary",)),
    )(hc, hc, hc, dcat, dcat, wc, wblk, pscale)


def _lower_bound(lb_ref, layer):
    b0, b1 = lb_ref[0:1, :], lb_ref[1:2, :]
    m = jnp.maximum(b0, b1)
    e0, e1 = jnp.exp(b0 - m), jnp.exp(b1 - m)
    p0, p1 = e0 / (e0 + e1), e1 / (e0 + e1)
    lb = (p0 - p0) if layer == 0 else ((p0 + p1) - p0)
    return lb, p0, p1


def _cumsum_rows(x, reverse=False):
    row = lax.broadcasted_iota(jnp.int32, x.shape, 0)
    for sh in (1, 2, 4, 8):
        if reverse:
            x = x + jnp.where(row < SUB - sh, pltpu.roll(x, SUB - sh, 0), 0.0)
        else:
            x = x + jnp.where(row >= sh, pltpu.roll(x, sh, 0), 0.0)
    return x


def _gates(fz, lb):
    sig = _sigmoid(fz)
    f = lb + (1.0 - lb) * sig
    g = jnp.log(jnp.maximum(f, F_FLOOR))
    k = (1.0 - lb) * (1.0 - sig)
    return sig, f, g, k


def _head(h):
    return slice(h * HG_D, (h + 1) * HG_D)


def _hgrn_fwd(hh, lbp, gnorm, layer, name):
    L = hh.shape[0]
    ts = SEQ_TILE
    n = L // ts
    nsub = ts // SUB

    def body(q_ref, f_ref, i_ref, g_ref, lb_ref, gn_ref, y_ref, o_ref, s_ref, a_ref, St):
        @pl.when(pl.program_id(0) == 0)
        def _():
            St[...] = jnp.zeros_like(St)

        lb, _, _ = _lower_bound(lb_ref, layer)
        gn = jnp.tile(gn_ref[...], (1, HG_HEADS))
        r16 = lax.broadcasted_iota(jnp.int32, (SUB, SUB), 0)
        c16 = lax.broadcasted_iota(jnp.int32, (SUB, SUB), 1)

        def block(j, carry):
            rows = pl.ds(pl.multiple_of(j * SUB, SUB), SUB)
            q = q_ref[rows, :] * Q_SCALE
            iv = i_ref[rows, :]
            gz = g_ref[rows, :]
            _, _, g, k = _gates(f_ref[rows, :], lb)
            G = _cumsum_rows(g)
            Gl = G[SUB - 1:SUB, :]
            qt = (q * jnp.exp(G)).astype(BF16)
            kd = (k * jnp.exp(Gl - G)).astype(BF16)
            eGl = jnp.exp(Gl)
            ib = iv.astype(BF16)
            A = [jnp.zeros((SUB, SUB), F32) for _ in range(HG_HEADS)]
            for s in range(SUB):
                P = q * jnp.exp(jnp.minimum(G - G[s:s + 1, :], 0.0)) * k[s:s + 1, :]
                for h in range(HG_HEADS):
                    A[h] = jnp.where(c16 == s, jnp.sum(P[:, _head(h)], axis=-1, keepdims=True), A[h])
            outs, ons, amats = [], [], []
            for h in range(HG_HEADS):
                sl = _head(h)
                Sb = St[h].astype(BF16)
                s_ref[j, sl, :] = Sb
                Am = jnp.where(r16 >= c16, A[h], 0.0)
                amats.append(Am)
                o = _dnt(qt[:, sl], Sb) + _dnn(Am.astype(BF16), ib[:, sl])
                St[h] = eGl[:, sl] * St[h] + _dtn(ib[:, sl], kd[:, sl])
                outs.append(o)
                ons.append(o * lax.rsqrt(jnp.mean(o * o, axis=-1, keepdims=True) + RMS_EPS))
            a_ref[rows, :] = jnp.concatenate(amats, axis=1)
            o_ref[rows, :] = jnp.concatenate(outs, axis=1)
            y = jnp.concatenate(ons, axis=1) * gn * (gz * _sigmoid(gz))
            y_ref[rows, :] = y.astype(BF16)
            return carry

        lax.fori_loop(0, nsub, block, 0, unroll=2)

    col = lambda c: pl.BlockSpec((ts, HG_W), lambda i: (i, c))
    return pl.pallas_call(
        body, name=name, grid=(n,),
        in_specs=[col(0), col(1), col(2), col(3), pl.BlockSpec((2, HG_W), lambda i: (0, 0)),
                  pl.BlockSpec((1, HG_D), lambda i: (0, 0))],
        out_specs=[pl.BlockSpec((ts, HG_W), lambda i: (i, 0)), pl.BlockSpec((ts, HG_W), lambda i: (i, 0)),
                   pl.BlockSpec((nsub, HG_W, HG_D), lambda i: (i, 0, 0)),
                   pl.BlockSpec((ts, HG_HEADS * SUB), lambda i: (i, 0))],
        out_shape=[jax.ShapeDtypeStruct((L, HG_W), BF16), jax.ShapeDtypeStruct((L, HG_W), F32),
                   jax.ShapeDtypeStruct((L // SUB, HG_W, HG_D), BF16),
                   jax.ShapeDtypeStruct((L, HG_HEADS * SUB), F32)],
        scratch_shapes=[pltpu.VMEM((HG_HEADS, HG_D, HG_D), F32)],
        compiler_params=_params(("arbitrary",)),
    )(hh, hh, hh, hh, lbp, gnorm)


def _hgrn_bwd(hh, o_raw, states, amat, dcat, lbp, gnorm, layer, name):
    L = hh.shape[0]
    ts = SEQ_TILE
    n = L // ts
    nsub = ts // SUB

    def body(q_ref, f_ref, i_ref, g_ref, o_ref, s_ref, a_ref, dy_ref, lb_ref, gn_ref,
             dh_ref, dlb_ref, dgn_ref, dSt, dlb_acc, S_next):
        step = pl.program_id(0)

        @pl.when(step == 0)
        def _():
            dSt[...] = jnp.zeros_like(dSt)
            S_next[...] = jnp.zeros_like(S_next)
            dlb_acc[...] = jnp.zeros_like(dlb_acc)
            dgn_ref[...] = jnp.zeros_like(dgn_ref)

        lb, p0, p1 = _lower_bound(lb_ref, layer)
        gnh = gn_ref[...]
        gn = jnp.tile(gnh, (1, HG_HEADS))
        r16 = lax.broadcasted_iota(jnp.int32, (SUB, SUB), 0)
        c16 = lax.broadcasted_iota(jnp.int32, (SUB, SUB), 1)

        def block(jj, carry):
            j = nsub - 1 - jj
            rows = pl.ds(pl.multiple_of(j * SUB, SUB), SUB)
            q = q_ref[rows, :] * Q_SCALE
            iv = i_ref[rows, :]
            gz = g_ref[rows, :]
            o = o_ref[rows, :]
            dy = dy_ref[rows, :].astype(F32)
            sig, f, g, k = _gates(f_ref[rows, :], lb)
            G = _cumsum_rows(g)
            Gl = G[SUB - 1:SUB, :]
            eG = jnp.exp(G)
            edl = jnp.exp(Gl - G)
            eGl = jnp.exp(Gl)
            qt = (q * eG).astype(BF16)
            kd = (k * edl).astype(BF16)
            ib = iv.astype(BF16)
            sgz = _sigmoid(gz)
            sil = gz * sgz
            dyn = dy * sil
            on_parts, do_parts = [], []
            dgn = jnp.zeros((1, HG_D), F32)
            for h in range(HG_HEADS):
                sl = _head(h)
                oh = o[:, sl]
                rs = lax.rsqrt(jnp.mean(oh * oh, axis=-1, keepdims=True) + RMS_EPS)
                on = oh * rs
                dgn = dgn + jnp.sum(dyn[:, sl] * on, axis=0, keepdims=True)
                don = dyn[:, sl] * gnh
                do_parts.append(rs * (don - on * jnp.mean(don * on, axis=-1, keepdims=True)))
                on_parts.append(on)
            dgn_ref[...] += dgn
            on_all = jnp.concatenate(on_parts, axis=1)
            dgz = dy * on_all * gn * (sgz * (1.0 + gz * (1.0 - sgz)))
            do = jnp.concatenate(do_parts, axis=1)
            dob = do.astype(BF16)
            amat = a_ref[rows, :]
            dq_p, dk_p, di_p, tail_p = [], [], [], []
            for h in range(HG_HEADS):
                sl = _head(h)
                qh, kh, Gh = q[:, sl], k[:, sl], G[:, sl]
                Ap = jnp.where(r16 >= c16, _dnt(dob[:, sl], ib[:, sl]), 0.0)
                ApT = jnp.where(r16 <= c16, _dnt(ib[:, sl], dob[:, sl]), 0.0)
                dqh = jnp.zeros((SUB, HG_D), F32)
                dkh = jnp.zeros((SUB, HG_D), F32)
                for s in range(SUB):
                    dGs = Gh - Gh[s:s + 1, :]
                    e = jnp.exp(jnp.minimum(dGs, -dGs))
                    dqh = dqh + Ap[:, s:s + 1] * (e * kh[s:s + 1, :])
                    dkh = dkh + ApT[:, s:s + 1] * (e * qh[s:s + 1, :])
                Sb = s_ref[j, sl, :]
                dSb = dSt[h].astype(BF16)
                Am = amat[:, h * SUB:(h + 1) * SUB].astype(BF16)
                dq_p.append(dqh + eG[:, sl] * _dnn(dob[:, sl], Sb))
                dk_p.append(dkh + edl[:, sl] * _dnn(ib[:, sl], dSb))
                di_p.append(_dtn(Am, dob[:, sl]) + _dnt(kd[:, sl], dSb))
                tail_p.append(jnp.sum(dSt[h] * S_next[h].astype(F32), axis=0, keepdims=True))
                S_next[h] = Sb
                dSt[h] = eGl[:, sl] * dSt[h] + _dtn(dob[:, sl], qt[:, sl])
            dq = jnp.concatenate(dq_p, axis=1)
            dk = jnp.concatenate(dk_p, axis=1)
            di = jnp.concatenate(di_p, axis=1)
            dg = _cumsum_rows(q * dq - k * dk, reverse=True) + jnp.concatenate(tail_p, axis=1)
            df = jnp.where(f > F_FLOOR, dg / f, 0.0)
            dfk = df - dk
            dfz = (1.0 - lb) * dfk * sig * (1.0 - sig)
            dlb_acc[...] += jnp.sum(dfk * (1.0 - sig), axis=0, keepdims=True)
            dh_ref[rows, :] = jnp.concatenate([dq * Q_SCALE, dfz, di, dgz], axis=1).astype(BF16)
            return carry

        lax.fori_loop(0, nsub, block, 0)

        @pl.when(step == n - 1)
        def _():
            if layer == 0:
                dlb_ref[...] = jnp.zeros_like(dlb_ref)
            else:
                dz1 = p0 * p1 * dlb_acc[...]
                dlb_ref[...] = jnp.concatenate([-dz1, dz1], axis=0)

    rev = lambda i: n - 1 - i
    col = lambda c: pl.BlockSpec((ts, HG_W), lambda i: (rev(i), c))
    return pl.pallas_call(
        body, name=name, grid=(n,),
        in_specs=[col(0), col(1), col(2), col(3), col(0),
                  pl.BlockSpec((nsub, HG_W, HG_D), lambda i: (rev(i), 0, 0)),
                  pl.BlockSpec((ts, HG_HEADS * SUB), lambda i: (rev(i), 0)), col(0),
                  pl.BlockSpec((2, HG_W), lambda i: (0, 0)), pl.BlockSpec((1, HG_D), lambda i: (0, 0))],
        out_specs=[pl.BlockSpec((ts, 4 * HG_W), lambda i: (rev(i), 0)),
                   pl.BlockSpec((2, HG_W), lambda i: (0, 0)), pl.BlockSpec((1, HG_D), lambda i: (0, 0))],
        out_shape=[jax.ShapeDtypeStruct((L, 4 * HG_W), BF16), jax.ShapeDtypeStruct((2, HG_W), F32),
                   jax.ShapeDtypeStruct((1, HG_D), F32)],
        scratch_shapes=[pltpu.VMEM((HG_HEADS, HG_D, HG_D), F32), pltpu.VMEM((1, HG_W), F32),
                        pltpu.VMEM((HG_HEADS, HG_D, HG_D), BF16)],
        compiler_params=_params(("arbitrary",)),
    )(hh, hh, hh, hh, o_raw, states, amat, dcat, lbp, gnorm)


def _adamw_body(gp_ref, w_ref, m_ref, v_ref, g_ref, d_ref, mo_ref, vo_ref):
    c1 = 1.0 - ADAM_B1 ** ADAM_STEP
    c2 = 1.0 - ADAM_B2 ** ADAM_STEP
    g = gp_ref[0].astype(F32)
    for k in range(1, N_DEV):
        g = g + gp_ref[k].astype(F32)
    mn = ADAM_B1 * m_ref[...] + (1.0 - ADAM_B1) * g
    vn = ADAM_B2 * v_ref[...] + (1.0 - ADAM_B2) * (g * g)
    m_hat = mn / c1
    v_hat = vn / c2
    g_ref[...] = g
    d_ref[...] = -ADAM_LR * (m_hat / (jnp.sqrt(v_hat) + ADAM_EPS) + ADAM_WD * w_ref[...])
    mo_ref[...] = mn
    vo_ref[...] = vn


def _adamw_layers(gparts, w, m, v, name):
    depth, R, C = w.shape
    tr = _tile(R, 256, 16)
    nr = R // tr

    def body(*refs):
        layer = pl.program_id(0)
        for d in range(depth):
            @pl.when(layer == d)
            def _(d=d):
                _adamw_body(refs[d], *refs[depth:])

    def parts_spec(d):
        return pl.BlockSpec((N_DEV, tr, C),
                            lambda l, i: (0, jnp.where(l == d, i, jnp.where(l < d, 0, nr - 1)), 0))

    blk = pl.BlockSpec((None, tr, C), lambda l, i: (l, i, 0))
    shp = jax.ShapeDtypeStruct((depth, R, C), F32)
    return pl.pallas_call(
        body, name=name, grid=(depth, nr),
        in_specs=[parts_spec(d) for d in range(depth)] + [blk, blk, blk],
        out_specs=[blk, blk, blk, blk], out_shape=[shp, shp, shp, shp],
        compiler_params=_params(("arbitrary", "arbitrary")),
    )(*gparts, w, m, v)


def _adamw(gparts, w, m, v, name):
    R = w.shape[0]
    tr = _tile(R, 1024, 16) if R % 16 == 0 else R

    def body(*refs):
        _adamw_body(*refs)

    row = pl.BlockSpec((tr, LANES), lambda i: (i, 0))
    shp = jax.ShapeDtypeStruct((R, LANES), F32)
    return pl.pallas_call(
        body, name=name, grid=(R // tr,),
        in_specs=[pl.BlockSpec((N_DEV, tr, LANES), lambda i: (0, i, 0)), row, row, row],
        out_specs=[row, row, row, row], out_shape=[shp, shp, shp, shp],
        compiler_params=_params(("parallel",)),
    )(gparts, w, m, v)


def _flip(coord, bit):
    return 1 - coord if bit else coord


def _gather_many(blocks, name):
    n = len(blocks)

    def body(*refs):
        x_refs, out_refs = refs[:n], refs[n:2 * n]
        send_sems, recv_sems, local_sems = refs[2 * n:]
        x, y, c = lax.axis_index("x"), lax.axis_index("y"), lax.axis_index("c")
        me, sibling = (x, y, c), (x, y, 1 - c)
        chips = [(1 - x, y), (x, 1 - y), (1 - x, 1 - y)]

        def slot(a, px, py, pc):
            return out_refs[a].at[4 * px + 2 * py + pc]

        def copy(a, k, blk, to, src=None):
            return pltpu.make_async_remote_copy(
                src_ref=slot(a, *blk) if src is None else src, dst_ref=slot(a, *blk),
                send_sem=send_sems.at[7 * a + k], recv_sem=recv_sems.at[7 * a + k],
                device_id=to, device_id_type=pl.DeviceIdType.MESH)

        mine = [pltpu.make_async_copy(x_refs[a], slot(a, *me), local_sems.at[a]) for a in range(n)]
        for cp in mine:
            cp.start()
        first = [copy(a, 0, me, sibling, src=x_refs[a]) for a in range(n)]
        for j, chip in enumerate(chips):
            first += [copy(a, 1 + j, me, (*chip, c), src=x_refs[a]) for a in range(n)]
        for cp in first:
            cp.start()
        passed = []
        for j, chip in enumerate(chips):
            for a in range(n):
                copy(a, 1 + j, (*chip, c), me).wait_recv()
                fwd = copy(a, 4 + j, (*chip, c), sibling)
                fwd.start()
                passed.append(fwd)
        for a in range(n):
            copy(a, 0, sibling, me).wait_recv()
        for j, chip in enumerate(chips):
            for a in range(n):
                copy(a, 4 + j, (*chip, 1 - c), me).wait_recv()
        for cp in first + passed:
            cp.wait_send()
        for cp in mine:
            cp.wait()

    hbm = pl.BlockSpec(memory_space=pl.ANY)
    return pl.pallas_call(
        body, name=name,
        out_shape=[jax.ShapeDtypeStruct((N_DEV,) + b.shape, b.dtype) for b in blocks],
        in_specs=[hbm] * n, out_specs=[hbm] * n,
        scratch_shapes=[pltpu.SemaphoreType.DMA((7 * n,)), pltpu.SemaphoreType.DMA((7 * n,)),
                        pltpu.SemaphoreType.DMA((n,))],
    )(*blocks)


def _split_start(blocks, chunked, name):
    n = len(blocks)
    lands = [lax.empty(b.shape if chunked else (N_DEV,) + b.shape, b.dtype) for b in blocks]

    def body(*refs):
        x_refs, land_refs = refs[:n], refs[n:2 * n]
        send_sems, recv_sems, token = refs[2 * n], refs[2 * n + 1], refs[-1]
        x, y, c = lax.axis_index("x"), lax.axis_index("y"), lax.axis_index("c")
        me = 4 * x + 2 * y + c
        for a in range(n):
            for k in range(1, N_DEV):
                px, py, pc = _flip(x, k & 4), _flip(y, k & 2), _flip(c, k & 1)
                pltpu.make_async_remote_copy(
                    src_ref=x_refs[a].at[4 * px + 2 * py + pc] if chunked else x_refs[a],
                    dst_ref=land_refs[a].at[me],
                    send_sem=send_sems.at[7 * a + k - 1], recv_sem=recv_sems.at[7 * a + k - 1],
                    device_id=(px, py, pc), device_id_type=pl.DeviceIdType.MESH).start()
        token[...] = jnp.zeros_like(token)

    hbm = pl.BlockSpec(memory_space=pltpu.HBM)
    sem = pl.BlockSpec(memory_space=pltpu.SEMAPHORE)
    outs = pl.pallas_call(
        body, name=name,
        out_shape=(pltpu.SemaphoreType.DMA((7 * n,)), pltpu.SemaphoreType.DMA((7 * n,)),
                   *[pltpu.HBM(b.shape, b.dtype) for b in blocks], *[pltpu.HBM(l.shape, l.dtype) for l in lands],
                   jax.ShapeDtypeStruct((8, LANES), F32)),
        in_specs=[hbm] * (2 * n),
        out_specs=(sem, sem, *[hbm] * (2 * n), pl.BlockSpec(memory_space=pltpu.VMEM)),
        input_output_aliases={i: 2 + i for i in range(2 * n)},
        compiler_params=pltpu.CompilerParams(has_side_effects=pltpu.SideEffectType.DATAFLOW_SIDE_EFFECTING),
    )(*[pltpu.with_memory_space_constraint(b, pltpu.HBM) for b in blocks],
      *[pltpu.with_memory_space_constraint(l, pltpu.HBM) for l in lands])
    return outs[0], outs[1], list(outs[2:2 + n]), list(outs[2 + n:2 + 2 * n]), outs[-1]


def _split_wait(started, chunked, after, name):
    send_sems, recv_sems, blocks, lands, _ = started
    n = len(blocks)

    def body(*refs):
        x_refs, land_refs = refs[:n], refs[n:2 * n]
        send_sems, recv_sems = refs[2 * n], refs[2 * n + 1]
        x, y, c = lax.axis_index("x"), lax.axis_index("y"), lax.axis_index("c")
        for a in range(n):
            for k in range(1, N_DEV):
                px, py, pc = _flip(x, k & 4), _flip(y, k & 2), _flip(c, k & 1)
                copy = pltpu.make_async_remote_copy(
                    src_ref=x_refs[a].at[4 * px + 2 * py + pc] if chunked else x_refs[a],
                    dst_ref=land_refs[a].at[4 * px + 2 * py + pc],
                    send_sem=send_sems.at[7 * a + k - 1], recv_sem=recv_sems.at[7 * a + k - 1],
                    device_id=(px, py, pc), device_id_type=pl.DeviceIdType.MESH)
                copy.wait_send()
                copy.wait_recv()

    hbm = pl.BlockSpec(memory_space=pltpu.HBM)
    sem = pl.BlockSpec(memory_space=pltpu.SEMAPHORE)
    outs = pl.pallas_call(
        body, name=name,
        out_shape=(*[pltpu.HBM(b.shape, b.dtype) for b in blocks], *[pltpu.HBM(l.shape, l.dtype) for l in lands]),
        in_specs=[hbm] * (2 * n) + [sem, sem, pl.BlockSpec(memory_space=pl.ANY)],
        out_specs=[hbm] * (2 * n),
        input_output_aliases={i: i for i in range(2 * n)},
        compiler_params=pltpu.CompilerParams(has_side_effects=pltpu.SideEffectType.DATAFLOW_SIDE_EFFECTING),
    )(*blocks, *lands, send_sems, recv_sems, after)
    me = 4 * lax.axis_index("x") + 2 * lax.axis_index("y") + lax.axis_index("c")
    own = [lax.dynamic_index_in_dim(b, me, 0, keepdims=False) if chunked else b for b in outs[:n]]
    return [lax.dynamic_update_index_in_dim(z, o, me, 0) for z, o in zip(outs[n:], own)]


def _exchange_grads(layer_chunks, small_chunks, rep_block, name):
    flows, inputs = [], []
    for p, per_layer in enumerate(layer_chunks):
        for l, arr in enumerate(per_layer):
            flows.append(("param", p, l))
            inputs.append(arr)
    flows += [("small",), ("rep",)]
    inputs += [small_chunks, rep_block]
    n_par = len(layer_chunks)
    n_in, n_out, nf = len(inputs), n_par + 2, len(flows)

    def body(*refs):
        in_refs, out_refs = refs[:n_in], refs[n_in:n_in + n_out]
        send_sems, recv_sems, local_sems = refs[n_in + n_out:]
        x, y, c = lax.axis_index("x"), lax.axis_index("y"), lax.axis_index("c")
        me = 4 * x + 2 * y + c

        def src(f, dev):
            return in_refs[f] if flows[f][0] == "rep" else in_refs[f].at[dev]

        def dst(f, dev):
            if flows[f][0] == "param":
                _, p, l = flows[f]
                return out_refs[p].at[dev, l]
            return out_refs[n_par + (0 if flows[f][0] == "small" else 1)].at[dev]

        mine = [pltpu.make_async_copy(src(f, me), dst(f, me), local_sems.at[f]) for f in range(nf)]
        for cp in mine:
            cp.start()
        copies = []
        for k in range(1, N_DEV):
            px, py, pc = _flip(x, k & 4), _flip(y, k & 2), _flip(c, k & 1)
            peer = 4 * px + 2 * py + pc
            for f in range(nf):
                sems = dict(send_sem=send_sems.at[7 * f + k - 1], recv_sem=recv_sems.at[7 * f + k - 1],
                            device_id=(px, py, pc), device_id_type=pl.DeviceIdType.MESH)
                send = pltpu.make_async_remote_copy(src_ref=src(f, peer), dst_ref=dst(f, me), **sems)
                recv = pltpu.make_async_remote_copy(src_ref=src(f, peer), dst_ref=dst(f, peer), **sems)
                send.start()
                copies.append((send, recv))
        for send, recv in copies:
            recv.wait_recv()
        for send, recv in copies:
            send.wait_send()
        for cp in mine:
            cp.wait()

    out_shape = [jax.ShapeDtypeStruct((N_DEV, len(pl_)) + pl_[0].shape[1:], pl_[0].dtype) for pl_ in layer_chunks]
    out_shape += [jax.ShapeDtypeStruct(small_chunks.shape, small_chunks.dtype),
                  jax.ShapeDtypeStruct((N_DEV,) + rep_block.shape, rep_block.dtype)]
    hbm = pl.BlockSpec(memory_space=pl.ANY)
    return pl.pallas_call(
        body, name=name, out_shape=out_shape,
        in_specs=[hbm] * n_in, out_specs=[hbm] * n_out,
        scratch_shapes=[pltpu.SemaphoreType.DMA((7 * nf,)), pltpu.SemaphoreType.DMA((7 * nf,)),
                        pltpu.SemaphoreType.DMA((nf,))],
    )(*inputs)


def _pack_rows(size):
    return -(-size // (8 * LANES)) * 8


def _pack(arrs, dtype):
    parts, offs, r = [], [], 0
    for a in arrs:
        flat = a.astype(dtype).reshape(-1)
        nrow = _pack_rows(flat.shape[0])
        flat = jnp.pad(flat, (0, nrow * LANES - flat.shape[0]))
        parts.append(flat.reshape(nrow, LANES))
        offs.append((r, nrow))
        r += nrow
    return jnp.concatenate(parts, axis=0), offs


def _unpack(buf, offs, shapes, lead=()):
    outs = []
    for (r, nrow), shp in zip(offs, shapes):
        size = 1
        for s in shp:
            size *= s
        flat = buf[..., r:r + nrow, :].reshape(lead + (nrow * LANES,))
        outs.append(flat[..., :size].reshape(lead + tuple(shp)))
    return outs


def _cols_from_shards(g, axis):
    return jnp.concatenate([g[j] for j in range(N_DEV)], axis=axis)


BIG = ("w_in", "w_o", "w_up", "w_down")
SMALL_SHARDED = ("meta_tokens", "w_conv", "w_ffn_conv")
REPLICATED = ("hg_lower_bounds", "w_pool", "pool_scale", "hg_norm_g", "ln1_g", "ln1_b", "b_ffn_conv", "ln2_g", "ln2_b")
WEIGHTS = ("meta_tokens", "hg_lower_bounds", "w_in", "w_conv", "w_pool", "pool_scale", "hg_norm_g", "w_o",
           "ln1_g", "ln1_b", "w_up", "w_ffn_conv", "b_ffn_conv", "w_down", "ln2_g", "ln2_b")


def _pool_blockdiag(w_pool_l):
    z = jnp.zeros((POOL_GROUP, POOL_GROUP), w_pool_l.dtype)
    rows = [jnp.concatenate([w_pool_l[g] if h == g else z for h in range(4)], axis=1) for g in range(4)]
    return jnp.concatenate(rows, axis=0)


def _mixer_weights(g_in, g_o):
    w_in = jnp.transpose(g_in, (1, 0, 2)).reshape(D_MODEL, -1)
    w_o = g_o.reshape(-1, D_MODEL)
    return dict(
        w_hg=w_in[:, 768:2816],
        w_cp=jnp.concatenate([w_in[:, 0:768], w_in[:, 2816:3072]], axis=1),
        w_o=jnp.concatenate([w_o[256:768], w_o[0:256], w_o[768:1024]], axis=0))


def _ffn_weights(g_up, g_down):
    return dict(w_up=jnp.transpose(g_up, (1, 0, 2)).reshape(D_MODEL, -1), w_down=g_down.reshape(-1, D_MODEL))


def kernel(x, meta_tokens, hg_lower_bounds, w_in, w_conv, w_pool, pool_scale, hg_norm_g, w_o, ln1_g, ln1_b, w_up, w_ffn_conv, b_ffn_conv, w_down, ln2_g, ln2_b, loss_target, m_meta_tokens, m_hg_lower_bounds, m_w_in, m_w_conv, m_w_pool, m_pool_scale, m_hg_norm_g, m_w_o, m_ln1_g, m_ln1_b, m_w_up, m_w_ffn_conv, m_b_ffn_conv, m_w_down, m_ln2_g, m_ln2_b, v_meta_tokens, v_hg_lower_bounds, v_w_in, v_w_conv, v_w_pool, v_pool_scale, v_hg_norm_g, v_w_o, v_ln1_g, v_ln1_b, v_w_up, v_w_ffn_conv, v_b_ffn_conv, v_w_down, v_ln2_g, v_ln2_b):
    W = dict(meta_tokens=meta_tokens, hg_lower_bounds=hg_lower_bounds, w_in=w_in, w_conv=w_conv, w_pool=w_pool,
             pool_scale=pool_scale, hg_norm_g=hg_norm_g, w_o=w_o, ln1_g=ln1_g, ln1_b=ln1_b, w_up=w_up,
             w_ffn_conv=w_ffn_conv, b_ffn_conv=b_ffn_conv, w_down=w_down, ln2_g=ln2_g, ln2_b=ln2_b)
    M = dict(meta_tokens=m_meta_tokens, hg_lower_bounds=m_hg_lower_bounds, w_in=m_w_in, w_conv=m_w_conv,
             w_pool=m_w_pool, pool_scale=m_pool_scale, hg_norm_g=m_hg_norm_g, w_o=m_w_o, ln1_g=m_ln1_g,
             ln1_b=m_ln1_b, w_up=m_w_up, w_ffn_conv=m_w_ffn_conv, b_ffn_conv=m_b_ffn_conv, w_down=m_w_down,
             ln2_g=m_ln2_g, ln2_b=m_ln2_b)
    V = dict(meta_tokens=v_meta_tokens, hg_lower_bounds=v_hg_lower_bounds, w_in=v_w_in, w_conv=v_w_conv,
             w_pool=v_w_pool, pool_scale=v_pool_scale, hg_norm_g=v_hg_norm_g, w_o=v_w_o, ln1_g=v_ln1_g,
             ln1_b=v_ln1_b, w_up=v_w_up, w_ffn_conv=v_w_ffn_conv, b_ffn_conv=v_b_ffn_conv, w_down=v_w_down,
             ln2_g=v_ln2_g, ln2_b=v_ln2_b)
    assert x.shape[0] == 1 and x.shape[2] == D_MODEL and w_in.shape[0] == DEPTH
    seq = x.shape[1]
    L = -(-(seq + N_META) // ROW_ALIGN) * ROW_ALIGN

    small_pack, small_offs = _pack([W[n] for n in SMALL_SHARDED], F32)
    shards = {n: [W[n][l].astype(BF16) for l in range(DEPTH)] for n in BIG}
    g_in0, g_o0, small_all = _gather_many([shards["w_in"][0], shards["w_o"][0], small_pack], "gather_weights")
    full = {}
    for n, a in zip(SMALL_SHARDED, _unpack(small_all, small_offs, [W[n].shape for n in SMALL_SHARDED], (N_DEV,))):
        full[n] = _cols_from_shards(a, 1)
    order = (small_all[0, 0, 0] * 0.0).astype(BF16)
    ffn0_started = _split_start([shards["w_up"][0] + order, shards["w_down"][0] + order], False, "gather_ffn0_start")
    order = ffn0_started[4][0, 0].astype(BF16)
    layer1_started = _split_start([shards[n][1] + order for n in BIG], False, "gather_layer1_start")
    lb_in = hg_lower_bounds + layer1_started[4][0, 0]

    pad_rows = L - N_META - seq
    xp = jnp.concatenate([full["meta_tokens"], x[0], jnp.zeros((pad_rows, D_MODEL), F32)], axis=0)
    tgt = jnp.concatenate([jnp.zeros((N_META, D_MODEL), F32), loss_target[0], jnp.zeros((pad_rows, D_MODEL), F32)], axis=0)

    saved = []
    h_in, h_in_b = xp, xp.astype(BF16)
    for l in range(DEPTH):
        if l == 0:
            lw = _mixer_weights(g_in0, g_o0)
        else:
            g_in, g_o, g_up, g_down = _split_wait(layer1_started, False, h_in_b, "gather_layer1_wait")
            lw = {**_mixer_weights(g_in, g_o), **_ffn_weights(g_up, g_down)}
        wc = full["w_conv"][l].T
        wblk = _pool_blockdiag(w_pool[l]).astype(BF16)
        ps = pool_scale[l][None, :]
        gn = hg_norm_g[l][None, :]
        wf = full["w_ffn_conv"][l].T
        bf = b_ffn_conv[l][None, :]
        hh = _matmul(h_in_b, lw["w_hg"], "nn", F32, f"fwd_hg_{l}")
        hc = _matmul(h_in_b, lw["w_cp"], "nn", F32, f"fwd_cp_{l}")
        y_hg, o_raw, states, amat = _hgrn_fwd(hh, lb_in if l == 0 else hg_lower_bounds, gn, l, f"hgrn_fwd_{l}")
        y_cp = _cp_fwd(hc, wc, wblk, ps, f"convpool_fwd_{l}")
        cat = jnp.concatenate([y_hg, y_cp], axis=1)
        z1, x1, x1_b = _matmul_ln(cat, lw["w_o"], h_in, ln1_g[l][None, :], ln1_b[l][None, :], f"fwd_o_ln1_{l}")
        if l == 0:
            lw.update(_ffn_weights(*_split_wait(ffn0_started, False, x1_b, "gather_ffn0_wait")))
        up = _matmul(x1_b, lw["w_up"], "nn", BF16, f"fwd_up_{l}")
        a, u = _ffn_act_fwd(up, wf, bf, f"ffn_fwd_{l}")
        saved.append(dict(lw=lw, wc=wc, wblk=wblk, ps=ps, gn=gn, wf=wf, bf=bf, x_b=h_in_b, hh=hh, hc=hc,
                          o_raw=o_raw, states=states, amat=amat, cat=cat, z1=z1, x1_b=x1_b, up=up, u=u, a=a))
        if l < DEPTH - 1:
            saved[l]["z2"], h_in, h_in_b = _matmul_ln(a, lw["w_down"], x1, ln2_g[l][None, :], ln2_b[l][None, :],
                                                     f"fwd_down_ln2_{l}")
        else:
            saved[l]["z2"], dy, loss_part = _matmul_ln(a, lw["w_down"], x1, ln2_g[l][None, :], ln2_b[l][None, :],
                                                       f"fwd_down_ln2_loss_{l}", loss=(tgt, seq))

    loss = lax.psum(loss_part[0, 0], ("x", "y", "c"))

    G = {}
    per_layer = {n: [None] * DEPTH for n in ("w_conv", "w_pool", "pool_scale", "hg_norm_g", "ln1_g", "ln1_b",
                                             "w_ffn_conv", "b_ffn_conv", "ln2_g", "ln2_b")}
    ffn_started, mix_started = [None] * DEPTH, [None] * DEPTH
    order = jnp.zeros((), F32)
    dlb_total = jnp.zeros((DEPTH, HG_W), F32)
    for l in reversed(range(DEPTH)):
        s = saved[l]
        lw = s["lw"]
        dz2_b, dg2, db2 = _ln_bwd(s["z2"], dy, ln2_g[l][None, :] + order, f"ln2_bwd_{l}")
        da = _matmul(dz2_b, lw["w_down"], "nt", BF16, f"bwd_da_{l}")
        d_w_down = _matmul(s["a"], dz2_b, "tn", BF16, f"wgrad_down_{l}")
        dup, dwf, dbf = _ffn_act_bwd(s["up"], s["u"], da, s["wf"], f"ffn_bwd_{l}")
        dx1 = _matmul(dup, lw["w_up"], "nt", BF16, f"bwd_dx1_{l}", res=dz2_b, alpha=ALPHA)
        d_w_up = _matmul(s["x1_b"], dup, "tn", BF16, f"wgrad_up_{l}")
        ffn_started[l] = _split_start([jnp.transpose(d_w_up.reshape(D_MODEL, N_DEV, -1), (1, 0, 2)),
                                       d_w_down.reshape(N_DEV, -1, D_MODEL)], True, f"scatter_ffn{l}_start")
        order = ffn_started[l][4][0, 0]
        dz1_b, dg1, db1 = _ln_bwd(s["z1"], dx1, ln1_g[l][None, :] + order, f"ln1_bwd_{l}")
        dcat = _matmul(dz1_b, lw["w_o"], "nt", BF16, f"bwd_dcat_{l}")
        d_w_o = _matmul(s["cat"], dz1_b, "tn", BF16, f"wgrad_o_{l}")
        dhh, dlb, dgn = _hgrn_bwd(s["hh"], s["o_raw"], s["states"], s["amat"], dcat, hg_lower_bounds, s["gn"], l,
                                  f"hgrn_bwd_{l}")
        dhc, dwc, dwblk, dps = _cp_bwd(s["hc"], dcat, s["wc"], s["wblk"], s["ps"], f"convpool_bwd_{l}")
        d_w_hg = _matmul(s["x_b"], dhh, "tn", BF16, f"wgrad_hg_{l}")
        d_w_cp = _matmul(s["x_b"], dhc, "tn", BF16, f"wgrad_cp_{l}")
        d_w_in = jnp.concatenate([d_w_cp[:, 0:768], d_w_hg, d_w_cp[:, 768:1024]], axis=1)
        mix_chunks = [jnp.transpose(d_w_in.reshape(D_MODEL, N_DEV, -1), (1, 0, 2)),
                      jnp.concatenate([d_w_o[512:768], d_w_o[0:512], d_w_o[768:1024]], axis=0).reshape(N_DEV, -1, D_MODEL)]
        mix_started[l] = _split_start(mix_chunks, True, f"scatter_mix{l}_start")
        order = mix_started[l][4][0, 0]
        dx_a = _matmul(dhh, lw["w_hg"] + order.astype(BF16), "nt", F32, f"bwd_dx_hg_{l}", res=dz1_b, alpha=ALPHA)
        dx = _matmul(dhc, lw["w_cp"], "nt", F32, f"bwd_dx_cp_{l}", res=dx_a, alpha=1.0)
        per_layer["w_conv"][l] = dwc.T
        per_layer["w_ffn_conv"][l] = dwf.T
        per_layer["b_ffn_conv"][l] = dbf[0]
        per_layer["w_pool"][l] = jnp.stack([dwblk[g * 64:(g + 1) * 64, g * 64:(g + 1) * 64] for g in range(4)], axis=0)
        per_layer["pool_scale"][l] = dps[0]
        per_layer["hg_norm_g"][l] = dgn[0]
        per_layer["ln1_g"][l], per_layer["ln1_b"][l] = dg1[0], db1[0]
        per_layer["ln2_g"][l], per_layer["ln2_b"][l] = dg2[0], db2[0]
        dlb_total = dlb_total + dlb
        dy = dx
    for n, parts in per_layer.items():
        G[n] = jnp.stack(parts, axis=0)
    G["hg_lower_bounds"] = dlb_total
    grad_x = dy[N_META:N_META + seq][None]

    def shard_major(g, lead):
        g = g.reshape(g.shape[:lead] + (N_DEV, -1) + g.shape[lead + 1:])
        g = jnp.moveaxis(g, lead, 0).reshape(N_DEV, -1)
        nrow = _pack_rows(g.shape[1])
        return jnp.pad(g, ((0, 0), (0, nrow * LANES - g.shape[1]))).reshape(N_DEV, nrow, LANES)

    small_chunks = jnp.concatenate([shard_major(dy[0:N_META], 1), shard_major(G["w_conv"], 1),
                                    shard_major(G["w_ffn_conv"], 1)], axis=1)
    w_small, _ = _pack([W[n] for n in SMALL_SHARDED], F32)
    rep_pack, rep_offs = _pack([G[n] for n in REPLICATED], F32)
    small_recv, rep_all = _exchange_grads([], small_chunks, rep_pack, "exchange_grads")
    parts = {n: [] for n in BIG}
    for l in range(DEPTH):
        up_l, down_l = _split_wait(ffn_started[l], True, rep_all, f"scatter_ffn{l}_wait")
        in_l, o_l = _split_wait(mix_started[l], True, rep_all, f"scatter_mix{l}_wait")
        for n, a in zip(BIG, (in_l, o_l, up_l, down_l)):
            parts[n].append(a)

    res = {k: {} for k in ("grad", "delta", "new_m", "new_v")}
    kinds = ("grad", "delta", "new_m", "new_v")
    for n in BIG:
        for kind, a in zip(kinds, _adamw_layers(parts[n], W[n], M[n], V[n], f"adamw_{n}")):
            res[kind][n] = a
    m_small, _ = _pack([M[n] for n in SMALL_SHARDED], F32)
    v_small, _ = _pack([V[n] for n in SMALL_SHARDED], F32)
    outs_small = _adamw(small_recv, w_small, m_small, v_small, "adamw_small_sharded")
    w_rep, _ = _pack([W[n] for n in REPLICATED], F32)
    m_rep, _ = _pack([M[n] for n in REPLICATED], F32)
    v_rep, _ = _pack([V[n] for n in REPLICATED], F32)
    outs_rep = _adamw(rep_all, w_rep, m_rep, v_rep, "adamw_replicated")
    for kind, b_sm, b_rep in zip(kinds, outs_small, outs_rep):
        for n, a in zip(SMALL_SHARDED, _unpack(b_sm, small_offs, [W[n].shape for n in SMALL_SHARDED])):
            res[kind][n] = a
        for n, a in zip(REPLICATED, _unpack(b_rep, rep_offs, [W[n].shape for n in REPLICATED])):
            res[kind][n] = a

    return (loss, grad_x, *[res["grad"][n] for n in WEIGHTS], *[res["delta"][n] for n in WEIGHTS],
            *[res["new_m"][n] for n in WEIGHTS], *[res["new_v"][n] for n in WEIGHTS])
```

```python
import jax
import jax.numpy as jnp
from jax import lax
from jax.experimental import pallas as pl
from jax.experimental.pallas import tpu as pltpu

F32 = jnp.float32
BF16 = jnp.bfloat16

N_DEV = 8
D_MODEL = 1024
N_META = 16
DEPTH = 2
CONV_W = 256
HG_W = 512
HG_D = 128
HG_HEADS = 4
POOL_W = 256
POOL_GROUP = 64
D_FF = 2816
ALPHA = (2 * DEPTH) ** 0.25
LN_EPS = 1e-5
RMS_EPS = 1e-6
F_FLOOR = 1e-30
Q_SCALE = HG_D ** -0.5
SUB = 16
SEQ_TILE = 192
FFN_TILE = 192
CP_TILE_CAP = 768
ROW_ALIGN = 192
LANES = 128
VMEM_LIMIT = 48 * 1024 * 1024
MATMUL_VMEM_BUDGET = 38 * 1024 * 1024

ADAM_LR = 0.001
ADAM_B1 = 0.9
ADAM_B2 = 0.999
ADAM_EPS = 1e-08
ADAM_WD = 0.01
ADAM_STEP = 10


def _tile(n, cap, mult):
    best = 0
    for t in range(mult, min(n, cap) + 1, mult):
        if n % t == 0:
            best = t
    assert best > 0, (n, cap, mult)
    return best


def _params(sem, vmem=VMEM_LIMIT):
    return pltpu.CompilerParams(dimension_semantics=sem, vmem_limit_bytes=vmem)


def _dnt(a, b):
    return lax.dot_general(a, b, (((1,), (1,)), ((), ())), preferred_element_type=F32)


def _dtn(a, b):
    return lax.dot_general(a, b, (((0,), (0,)), ((), ())), preferred_element_type=F32)


def _dnn(a, b):
    return jnp.dot(a, b, preferred_element_type=F32)


def _sigmoid(x):
    return jax.nn.sigmoid(x)


def _matmul(a, b, mode, out_dtype, name, res=None, alpha=1.0):
    if mode == "tn":
        K, M = a.shape
    else:
        M, K = a.shape
    N = b.shape[0] if mode == "nt" else b.shape[1]
    out_bytes = jnp.dtype(out_dtype).itemsize
    if mode == "tn" and M % 512 == 0 and N % 512 == 0:
        tk, tm, tn = K, 512, 512
        nk, use_acc = 1, False
    else:
        tn = _tile(N, 1536, LANES)
        tk = _tile(K, 1536, 16) if mode == "tn" else _tile(K, 2816, LANES)
        nk = K // tk
        use_acc = nk > 1 and out_dtype != F32
        tm = M
        for cap in (1536, 768, 384):
            tm = _tile(M, cap, 16)
            blocks = 2 * (a.dtype.itemsize * tm * tk + b.dtype.itemsize * tn * tk + out_bytes * tm * tn
                          + (res.dtype.itemsize * tm * tn if res is not None else 0)) + (4 * tm * tn if use_acc else 0)
            if blocks <= MATMUL_VMEM_BUDGET:
                break
    dims = {"nn": ((1,), (0,)), "nt": ((1,), (1,)), "tn": ((0,), (0,))}[mode]

    def body(*refs):
        a_ref, b_ref = refs[0], refs[1]
        r_ref = refs[2] if res is not None else None
        o_ref = refs[3] if res is not None else refs[2]
        acc = refs[-1] if use_acc else o_ref
        k = pl.program_id(2)
        p = lax.dot_general(a_ref[...].astype(BF16), b_ref[...].astype(BF16), (dims, ((), ())),
                            preferred_element_type=F32)

        def finish(r):
            if r_ref is not None:
                r = r + alpha * r_ref[...].astype(F32)
            o_ref[...] = r.astype(out_dtype)

        if nk == 1:
            finish(p)
        else:
            @pl.when(k == 0)
            def _():
                acc[...] = p

            @pl.when((k > 0) & (k < nk - 1))
            def _():
                acc[...] += p

            @pl.when(k == nk - 1)
            def _():
                finish(acc[...] + p)

    if mode == "tn":
        a_spec = pl.BlockSpec((tk, tm), lambda i, j, k: (k, i))
    else:
        a_spec = pl.BlockSpec((tm, tk), lambda i, j, k: (i, k))
    if mode == "nt":
        b_spec = pl.BlockSpec((tn, tk), lambda i, j, k: (j, k))
    else:
        b_spec = pl.BlockSpec((tk, tn), lambda i, j, k: (k, j))
    in_specs = [a_spec, b_spec]
    args = [a, b]
    if res is not None:
        in_specs.append(pl.BlockSpec((tm, tn), lambda i, j, k: (i, j)))
        args.append(res)
    return pl.pallas_call(
        body, name=name,
        grid=(M // tm, N // tn, nk),
        in_specs=in_specs,
        out_specs=pl.BlockSpec((tm, tn), lambda i, j, k: (i, j)),
        out_shape=jax.ShapeDtypeStruct((M, N), out_dtype),
        scratch_shapes=[pltpu.VMEM((tm, tn), F32)] if use_acc else [],
        compiler_params=_params(("parallel", "parallel", "arbitrary")),
    )(*args)


def _matmul_ln(a, w, x, g, b, name, loss=None):
    L, K = a.shape
    D = w.shape[1]
    tr = L
    for cap in (1536, 768, 384):
        tr = _tile(L, cap, 16)
        if 2 * (2 * tr * K + 2 * K * D + 4 * tr * D * (4 if loss else 3) + 2 * tr * D) <= MATMUL_VMEM_BUDGET:
            break

    def body(*refs):
        a_ref, w_ref, x_ref, g_ref, b_ref = refs[:5]
        z = ALPHA * x_ref[...] + _dnn(a_ref[...], w_ref[...])
        mu = jnp.mean(z, axis=-1, keepdims=True)
        zc = z - mu
        var = jnp.mean(zc * zc, axis=-1, keepdims=True)
        y = zc * lax.rsqrt(var + LN_EPS) * g_ref[...] + b_ref[...]
        if loss is None:
            z_ref, y_ref, yb_ref = refs[5:]
            y_ref[...] = y
            yb_ref[...] = y.astype(BF16)
        else:
            t_ref, z_ref, dy_ref, loss_ref = refs[5:]
            i = pl.program_id(0)

            @pl.when(i == 0)
            def _():
                loss_ref[...] = jnp.zeros_like(loss_ref)

            r = i * tr + lax.broadcasted_iota(jnp.int32, (tr, D), 0)
            valid = (r >= N_META) & (r < N_META + loss[1])
            e = jnp.where(valid, y - t_ref[...], 0.0)
            dy_ref[...] = e * (1.0 / D)
            s = jnp.sum(jnp.sum(e * e, axis=-1, keepdims=True), axis=0, keepdims=True)
            loss_ref[...] += (0.5 / D) * s
        z_ref[...] = z.astype(BF16)

    row = pl.BlockSpec((tr, D), lambda i: (i, 0))
    vec = pl.BlockSpec((1, D), lambda i: (0, 0))
    in_specs = [pl.BlockSpec((tr, K), lambda i: (i, 0)), pl.BlockSpec((K, D), lambda i: (0, 0)), row, vec, vec]
    f32_rows = jax.ShapeDtypeStruct((L, D), F32)
    b16_rows = jax.ShapeDtypeStruct((L, D), BF16)
    if loss is None:
        args, out_specs = [a, w, x, g, b], [row, row, row]
        out_shape = [b16_rows, f32_rows, b16_rows]
    else:
        args, in_specs = [a, w, x, g, b, loss[0]], in_specs + [row]
        out_specs = [row, row, pl.BlockSpec((1, 1), lambda i: (0, 0))]
        out_shape = [b16_rows, f32_rows, jax.ShapeDtypeStruct((1, 1), F32)]
    return pl.pallas_call(
        body, name=name, grid=(L // tr,), in_specs=in_specs, out_specs=out_specs, out_shape=out_shape,
        compiler_params=_params(("arbitrary",) if loss else ("parallel",)),
    )(*args)


def _ln_bwd(z, dy, g, name):
    L, D = z.shape
    tr = _tile(L, 768, 16)

    def body(z_ref, dy_ref, g_ref, dzb_ref, dg_ref, db_ref):
        @pl.when(pl.program_id(0) == 0)
        def _():
            dg_ref[...] = jnp.zeros_like(dg_ref)
            db_ref[...] = jnp.zeros_like(db_ref)

        z = z_ref[...].astype(F32)
        mu = jnp.mean(z, axis=-1, keepdims=True)
        zc = z - mu
        var = jnp.mean(zc * zc, axis=-1, keepdims=True)
        rstd = lax.rsqrt(var + LN_EPS)
        xhat = zc * rstd
        dy = dy_ref[...].astype(F32)
        dxh = dy * g_ref[...]
        m1 = jnp.mean(dxh, axis=-1, keepdims=True)
        m2 = jnp.mean(dxh * xhat, axis=-1, keepdims=True)
        dz = rstd * (dxh - m1 - xhat * m2)
        dzb_ref[...] = dz.astype(BF16)
        dg_ref[...] += jnp.sum(dy * xhat, axis=0, keepdims=True)
        db_ref[...] += jnp.sum(dy, axis=0, keepdims=True)

    row = pl.BlockSpec((tr, D), lambda i: (i, 0))
    vec = pl.BlockSpec((1, D), lambda i: (0, 0))
    return pl.pallas_call(
        body, name=name, grid=(L // tr,),
        in_specs=[row, row, vec], out_specs=[row, vec, vec],
        out_shape=[jax.ShapeDtypeStruct((L, D), BF16),
                   jax.ShapeDtypeStruct((1, D), F32), jax.ShapeDtypeStruct((1, D), F32)],
        compiler_params=_params(("arbitrary",)),
    )(z, dy, g)


def _shift_down(x, prev, k):
    T, C = x.shape
    rot = pltpu.roll(jnp.concatenate([prev, x], axis=0).reshape(T // 8 + 1, 8, C), k, 1)
    sub = lax.broadcasted_iota(jnp.int32, (T // 8, 8, C), 1)
    return jnp.where(sub < k, rot[:-1], rot[1:]).reshape(T, C)


def _shift_up(x, nxt, k):
    T, C = x.shape
    rot = pltpu.roll(jnp.concatenate([x, nxt], axis=0).reshape(T // 8 + 1, 8, C), 8 - k, 1)
    sub = lax.broadcasted_iota(jnp.int32, (T // 8, 8, C), 1)
    return jnp.where(sub >= 8 - k, rot[1:], rot[:-1]).reshape(T, C)


def _conv3(x, prev, w, b):
    return w[2:3, :] * x + w[1:2, :] * _shift_down(x, prev, 1) + w[0:1, :] * _shift_down(x, prev, 2) + b


def _ffn_act_fwd(up, w, b, name):
    L, C = up.shape
    F = C // 2
    ts = FFN_TILE
    n = L // ts

    def body(up_ref, pv_ref, w_ref, b_ref, a_ref, u_ref):
        i = pl.program_id(0)
        x = up_ref[...].astype(F32)
        prev = jnp.where(i > 0, pv_ref[...].astype(F32)[8:16], 0.0)
        u = _conv3(x, prev, w_ref[...], b_ref[...])
        u_ref[...] = u.astype(BF16)
        gate = u[:, :F]
        a_ref[...] = (gate * _sigmoid(gate) * u[:, F:]).astype(BF16)

    return pl.pallas_call(
        body, name=name, grid=(n,),
        in_specs=[pl.BlockSpec((ts, C), lambda i: (i, 0)),
                  pl.BlockSpec((16, C), lambda i: (jnp.maximum(i * (ts // 16) - 1, 0), 0)),
                  pl.BlockSpec((3, C), lambda i: (0, 0)), pl.BlockSpec((1, C), lambda i: (0, 0))],
        out_specs=[pl.BlockSpec((ts, F), lambda i: (i, 0)), pl.BlockSpec((ts, C), lambda i: (i, 0))],
        out_shape=[jax.ShapeDtypeStruct((L, F), BF16), jax.ShapeDtypeStruct((L, C), BF16)],
        compiler_params=_params(("parallel",)),
    )(up, up, w, b)


def _ffn_act_bwd(up, u, da, w, name):
    L, C = up.shape
    F = C // 2
    ts = FFN_TILE
    n = L // ts
    last16 = L // 16 - 1

    def du_of(u, da):
        gate, val = u[:, :F], u[:, F:]
        sg = _sigmoid(gate)
        dgate = da * val * (sg * (1.0 + gate * (1.0 - sg)))
        dval = da * (gate * sg)
        return jnp.concatenate([dgate, dval], axis=1)

    def body(up_ref, u_ref, un_ref, da_ref, dan_ref, w_ref, dup_ref, dw_ref, db_ref):
        i = pl.program_id(0)

        @pl.when(i == 0)
        def _():
            dw_ref[...] = jnp.zeros_like(dw_ref)
            db_ref[...] = jnp.zeros_like(db_ref)

        w = w_ref[...]
        x = up_ref[...].astype(F32)
        du = du_of(u_ref[...].astype(F32), da_ref[...].astype(F32))
        dun = jnp.where(i < n - 1, du_of(un_ref[...].astype(F32)[0:8], dan_ref[...].astype(F32)[0:8]), 0.0)
        du1 = _shift_up(du, dun, 1)
        du2 = _shift_up(du, dun, 2)
        dup_ref[...] = (w[2:3, :] * du + w[1:2, :] * du1 + w[0:1, :] * du2).astype(BF16)
        dw_ref[...] += jnp.concatenate([jnp.sum(x * du2, axis=0, keepdims=True),
                                        jnp.sum(x * du1, axis=0, keepdims=True),
                                        jnp.sum(x * du, axis=0, keepdims=True)], axis=0)
        db_ref[...] += jnp.sum(du, axis=0, keepdims=True)

    nxt = lambda i: (jnp.minimum((i + 1) * (ts // 16), last16), 0)
    return pl.pallas_call(
        body, name=name, grid=(n,),
        in_specs=[pl.BlockSpec((ts, C), lambda i: (i, 0)),
                  pl.BlockSpec((ts, C), lambda i: (i, 0)), pl.BlockSpec((16, C), nxt),
                  pl.BlockSpec((ts, F), lambda i: (i, 0)), pl.BlockSpec((16, F), nxt),
                  pl.BlockSpec((3, C), lambda i: (0, 0))],
        out_specs=[pl.BlockSpec((ts, C), lambda i: (i, 0)), pl.BlockSpec((3, C), lambda i: (0, 0)),
                   pl.BlockSpec((1, C), lambda i: (0, 0))],
        out_shape=[jax.ShapeDtypeStruct((L, C), BF16), jax.ShapeDtypeStruct((3, C), F32),
                   jax.ShapeDtypeStruct((1, C), F32)],
        compiler_params=_params(("arbitrary",)),
    )(up, u, u, da, da, w)


def _pool_window(ext, tile_rows, first_row, lead):
    T = ext.shape[0]
    sh = (lambda x, k: pltpu.roll(x, T - k, 0)) if lead else (lambda x, k: pltpu.roll(x, k, 0))
    r2 = ext + sh(ext, 1)
    r4 = r2 + sh(r2, 2)
    r8 = r4 + sh(r4, 4)
    r16 = r8 + sh(r8, 8)
    lo = 0 if lead else 16
    grp = lax.broadcasted_iota(jnp.int32, (tile_rows, POOL_W), 1) // POOL_GROUP
    pick = lambda a, b, c, d: jnp.where(grp == 0, a, jnp.where(grp == 1, b, jnp.where(grp == 2, c, d)))
    win = pick(r2[lo:lo + tile_rows], r4[lo:lo + tile_rows], r8[lo:lo + tile_rows], r16[lo:lo + tile_rows])
    return win, pick(2.0, 4.0, 8.0, 16.0)


def _pool_count(first_row, rows, wlen):
    t1 = (first_row + lax.broadcasted_iota(jnp.int32, (rows, POOL_W), 0) + 1).astype(F32)
    return jnp.minimum(t1, wlen)


def _cp_fwd(hc, wc, wblk, pscale, name):
    L = hc.shape[0]
    ts = _tile(L, CP_TILE_CAP, 16)
    n = L // ts

    def body(h_ref, hp_ref, wc_ref, wb_ref, ps_ref, y_ref):
        i = pl.program_id(0)
        h = h_ref[...]
        hp = jnp.where(i > 0, hp_ref[...], 0.0)
        cb, cc, cv, pv = h[:, 0:256], h[:, 256:512], h[:, 512:768], h[:, 768:1024]
        p = cc * cv
        pp = hp[8:16, 256:512] * hp[8:16, 512:768]
        w = wc_ref[...]
        conv = w[2:3, :] * p + w[1:2, :] * _shift_down(p, pp, 1) + w[0:1, :] * _shift_down(p, pp, 2)
        y_conv = cb * conv
        ext = jnp.concatenate([hp[:, 768:1024], pv], axis=0)
        win, wlen = _pool_window(ext, ts, i * ts, False)
        d = win / _pool_count(i * ts, ts, wlen) - pv
        y_pool = _dnn(d.astype(BF16), wb_ref[...]) * ps_ref[...]
        y_ref[...] = jnp.concatenate([y_conv, y_pool], axis=1).astype(BF16)

    return pl.pallas_call(
        body, name=name, grid=(n,),
        in_specs=[pl.BlockSpec((ts, 1024), lambda i: (i, 0)),
                  pl.BlockSpec((16, 1024), lambda i: (jnp.maximum(i * (ts // 16) - 1, 0), 0)),
                  pl.BlockSpec((3, 256), lambda i: (0, 0)), pl.BlockSpec((256, 256), lambda i: (0, 0)),
                  pl.BlockSpec((1, 256), lambda i: (0, 0))],
        out_specs=pl.BlockSpec((ts, 512), lambda i: (i, 0)),
        out_shape=jax.ShapeDtypeStruct((L, 512), BF16),
        compiler_params=_params(("parallel",)),
    )(hc, hc, wc, wblk, pscale)


def _cp_bwd(hc, dcat, wc, wblk, pscale, name):
    L = hc.shape[0]
    ts = _tile(L, CP_TILE_CAP, 16)
    n = L // ts
    last16 = L // 16 - 1

    def body(h_ref, hp_ref, hn_ref, dy_ref, dyn_ref, wc_ref, wb_ref, ps_ref,
             dh_ref, dwc_ref, dwb_ref, dps_ref):
        i = pl.program_id(0)

        @pl.when(i == 0)
        def _():
            dwc_ref[...] = jnp.zeros_like(dwc_ref)
            dwb_ref[...] = jnp.zeros_like(dwb_ref)
            dps_ref[...] = jnp.zeros_like(dps_ref)

        h = h_ref[...]
        hp = jnp.where(i > 0, hp_ref[...], 0.0)
        hn = hn_ref[...]
        dy = dy_ref[...].astype(F32)
        dyn = jnp.where(i < n - 1, dyn_ref[...].astype(F32), 0.0)
        cb, cc, cv, pv = h[:, 0:256], h[:, 256:512], h[:, 512:768], h[:, 768:1024]
        w = wc_ref[...]
        p = cc * cv
        pp = hp[8:16, 256:512] * hp[8:16, 512:768]
        p1 = _shift_down(p, pp, 1)
        p2 = _shift_down(p, pp, 2)
        conv = w[2:3, :] * p + w[1:2, :] * p1 + w[0:1, :] * p2
        dyc = dy[:, 0:256]
        dcb = dyc * conv
        dconv = dyc * cb
        dconv_n = dyn[0:8, 0:256] * hn[0:8, 0:256]
        dc1 = _shift_up(dconv, dconv_n, 1)
        dc2 = _shift_up(dconv, dconv_n, 2)
        dp = w[2:3, :] * dconv + w[1:2, :] * dc1 + w[0:1, :] * dc2
        dwc_ref[...] += jnp.concatenate([jnp.sum(p * dc2, axis=0, keepdims=True),
                                         jnp.sum(p * dc1, axis=0, keepdims=True),
                                         jnp.sum(p * dconv, axis=0, keepdims=True)], axis=0)
        ps = ps_ref[...]
        wb = wb_ref[...]
        ext = jnp.concatenate([hp[:, 768:1024], pv], axis=0)
        win, wlen = _pool_window(ext, ts, i * ts, False)
        d = win / _pool_count(i * ts, ts, wlen) - pv
        db = d.astype(BF16)
        dyp = dy[:, 256:512]
        dps_ref[...] += jnp.sum(dyp * _dnn(db, wb), axis=0, keepdims=True)
        dypre = (dyp * ps).astype(BF16)
        dwb_ref[...] += _dtn(db, dypre)
        dd = _dnt(dypre, wb)
        ddn = _dnt((dyn[:, 256:512] * ps).astype(BF16), wb)
        e = dd / _pool_count(i * ts, ts, wlen)
        en = ddn / _pool_count((i + 1) * ts, 16, wlen[0:16])
        lead, _ = _pool_window(jnp.concatenate([e, en], axis=0), ts, i * ts, True)
        dpv = lead - dd
        dh_ref[...] = jnp.concatenate([dcb, dp * cv, dp * cc, dpv], axis=1).astype(BF16)

    return pl.pallas_call(
        body, name=name, grid=(n,),
        in_specs=[pl.BlockSpec((ts, 1024), lambda i: (i, 0)),
                  pl.BlockSpec((16, 1024), lambda i: (jnp.maximum(i * (ts // 16) - 1, 0), 0)),
                  pl.BlockSpec((16, 1024), lambda i: (jnp.minimum((i + 1) * (ts // 16), last16), 0)),
                  pl.BlockSpec((ts, 512), lambda i: (i, 1)),
                  pl.BlockSpec((16, 512), lambda i: (jnp.minimum((i + 1) * (ts // 16), last16), 1)),
                  pl.BlockSpec((3, 256), lambda i: (0, 0)), pl.BlockSpec((256, 256), lambda i: (0, 0)),
                  pl.BlockSpec((1, 256), lambda i: (0, 0))],
        out_specs=[pl.BlockSpec((ts, 1024), lambda i: (i, 0)), pl.BlockSpec((3, 256), lambda i: (0, 0)),
                   pl.BlockSpec((256, 256), lambda i: (0, 0)), pl.BlockSpec((1, 256), lambda i: (0, 0))],
        out_shape=[jax.ShapeDtypeStruct((L, 1024), BF16), jax.ShapeDtypeStruct((3, 256), F32),
                   jax.ShapeDtypeStruct((256, 256), F32), jax.ShapeDtypeStruct((1, 256), F32)],
        compiler_params=_params(("arbitrary",)),
    )(hc, hc, hc, dcat, dcat, wc, wblk, pscale)


def _lower_bound(lb_ref, layer):
    b0, b1 = lb_ref[0:1, :], lb_ref[1:2, :]
    m = jnp.maximum(b0, b1)
    e0, e1 = jnp.exp(b0 - m), jnp.exp(b1 - m)
    p0, p1 = e0 / (e0 + e1), e1 / (e0 + e1)
    lb = (p0 - p0) if layer == 0 else ((p0 + p1) - p0)
    return lb, p0, p1


def _cumsum_rows(x, reverse=False):
    row = lax.broadcasted_iota(jnp.int32, x.shape, 0)
    for sh in (1, 2, 4, 8):
        if reverse:
            x = x + jnp.where(row < SUB - sh, pltpu.roll(x, SUB - sh, 0), 0.0)
        else:
            x = x + jnp.where(row >= sh, pltpu.roll(x, sh, 0), 0.0)
    return x


def _gates(fz, lb):
    sig = _sigmoid(fz)
    f = lb + (1.0 - lb) * sig
    g = jnp.log(jnp.maximum(f, F_FLOOR))
    k = (1.0 - lb) * (1.0 - sig)
    return sig, f, g, k


def _head(h):
    return slice(h * HG_D, (h + 1) * HG_D)


def _hgrn_fwd(hh, lbp, gnorm, layer, name):
    L = hh.shape[0]
    ts = SEQ_TILE
    n = L // ts
    nsub = ts // SUB

    def body(q_ref, f_ref, i_ref, g_ref, lb_ref, gn_ref, y_ref, o_ref, s_ref, a_ref, St):
        @pl.when(pl.program_id(0) == 0)
        def _():
            St[...] = jnp.zeros_like(St)

        lb, _, _ = _lower_bound(lb_ref, layer)
        gn = jnp.tile(gn_ref[...], (1, HG_HEADS))
        r16 = lax.broadcasted_iota(jnp.int32, (SUB, SUB), 0)
        c16 = lax.broadcasted_iota(jnp.int32, (SUB, SUB), 1)

        def block(j, carry):
            rows = pl.ds(pl.multiple_of(j * SUB, SUB), SUB)
            q = q_ref[rows, :] * Q_SCALE
            iv = i_ref[rows, :]
            gz = g_ref[rows, :]
            _, _, g, k = _gates(f_ref[rows, :], lb)
            G = _cumsum_rows(g)
            Gl = G[SUB - 1:SUB, :]
            qt = (q * jnp.exp(G)).astype(BF16)
            kd = (k * jnp.exp(Gl - G)).astype(BF16)
            eGl = jnp.exp(Gl)
            ib = iv.astype(BF16)
            A = [jnp.zeros((SUB, SUB), F32) for _ in range(HG_HEADS)]
            for s in range(SUB):
                P = q * jnp.exp(jnp.minimum(G - G[s:s + 1, :], 0.0)) * k[s:s + 1, :]
                for h in range(HG_HEADS):
                    A[h] = jnp.where(c16 == s, jnp.sum(P[:, _head(h)], axis=-1, keepdims=True), A[h])
            outs, ons, amats = [], [], []
            for h in range(HG_HEADS):
                sl = _head(h)
                Sb = St[h].astype(BF16)
                s_ref[j, sl, :] = Sb
                Am = jnp.where(r16 >= c16, A[h], 0.0)
                amats.append(Am)
                o = _dnt(qt[:, sl], Sb) + _dnn(Am.astype(BF16), ib[:, sl])
                St[h] = eGl[:, sl] * St[h] + _dtn(ib[:, sl], kd[:, sl])
                outs.append(o)
                ons.append(o * lax.rsqrt(jnp.mean(o * o, axis=-1, keepdims=True) + RMS_EPS))
            a_ref[rows, :] = jnp.concatenate(amats, axis=1)
            o_ref[rows, :] = jnp.concatenate(outs, axis=1)
            y = jnp.concatenate(ons, axis=1) * gn * (gz * _sigmoid(gz))
            y_ref[rows, :] = y.astype(BF16)
            return carry

        lax.fori_loop(0, nsub, block, 0, unroll=2)

    col = lambda c: pl.BlockSpec((ts, HG_W), lambda i: (i, c))
    return pl.pallas_call(
        body, name=name, grid=(n,),
        in_specs=[col(0), col(1), col(2), col(3), pl.BlockSpec((2, HG_W), lambda i: (0, 0)),
                  pl.BlockSpec((1, HG_D), lambda i: (0, 0))],
        out_specs=[pl.BlockSpec((ts, HG_W), lambda i: (i, 0)), pl.BlockSpec((ts, HG_W), lambda i: (i, 0)),
                   pl.BlockSpec((nsub, HG_W, HG_D), lambda i: (i, 0, 0)),
                   pl.BlockSpec((ts, HG_HEADS * SUB), lambda i: (i, 0))],
        out_shape=[jax.ShapeDtypeStruct((L, HG_W), BF16), jax.ShapeDtypeStruct((L, HG_W), F32),
                   jax.ShapeDtypeStruct((L // SUB, HG_W, HG_D), BF16),
                   jax.ShapeDtypeStruct((L, HG_HEADS * SUB), F32)],
        scratch_shapes=[pltpu.VMEM((HG_HEADS, HG_D, HG_D), F32)],
        compiler_params=_params(("arbitrary",)),
    )(hh, hh, hh, hh, lbp, gnorm)


def _hgrn_bwd(hh, o_raw, states, amat, dcat, lbp, gnorm, layer, name):
    L = hh.shape[0]
    ts = SEQ_TILE
    n = L // ts
    nsub = ts // SUB

    def body(q_ref, f_ref, i_ref, g_ref, o_ref, s_ref, a_ref, dy_ref, lb_ref, gn_ref,
             dh_ref, dlb_ref, dgn_ref, dSt, dlb_acc, S_next):
        step = pl.program_id(0)

        @pl.when(step == 0)
        def _():
            dSt[...] = jnp.zeros_like(dSt)
            S_next[...] = jnp.zeros_like(S_next)
            dlb_acc[...] = jnp.zeros_like(dlb_acc)
            dgn_ref[...] = jnp.zeros_like(dgn_ref)

        lb, p0, p1 = _lower_bound(lb_ref, layer)
        gnh = gn_ref[...]
        gn = jnp.tile(gnh, (1, HG_HEADS))
        r16 = lax.broadcasted_iota(jnp.int32, (SUB, SUB), 0)
        c16 = lax.broadcasted_iota(jnp.int32, (SUB, SUB), 1)

        def block(jj, carry):
            j = nsub - 1 - jj
            rows = pl.ds(pl.multiple_of(j * SUB, SUB), SUB)
            q = q_ref[rows, :] * Q_SCALE
            iv = i_ref[rows, :]
            gz = g_ref[rows, :]
            o = o_ref[rows, :]
            dy = dy_ref[rows, :].astype(F32)
            sig, f, g, k = _gates(f_ref[rows, :], lb)
            G = _cumsum_rows(g)
            Gl = G[SUB - 1:SUB, :]
            eG = jnp.exp(G)
            edl = jnp.exp(Gl - G)
            eGl = jnp.exp(Gl)
            qt = (q * eG).astype(BF16)
            kd = (k * edl).astype(BF16)
            ib = iv.astype(BF16)
            sgz = _sigmoid(gz)
            sil = gz * sgz
            dyn = dy * sil
            on_parts, do_parts = [], []
            dgn = jnp.zeros((1, HG_D), F32)
            for h in range(HG_HEADS):
                sl = _head(h)
                oh = o[:, sl]
                rs = lax.rsqrt(jnp.mean(oh * oh, axis=-1, keepdims=True) + RMS_EPS)
                on = oh * rs
                dgn = dgn + jnp.sum(dyn[:, sl] * on, axis=0, keepdims=True)
                don = dyn[:, sl] * gnh
                do_parts.append(rs * (don - on * jnp.mean(don * on, axis=-1, keepdims=True)))
                on_parts.append(on)
            dgn_ref[...] += dgn
            on_all = jnp.concatenate(on_parts, axis=1)
            dgz = dy * on_all * gn * (sgz * (1.0 + gz * (1.0 - sgz)))
            do = jnp.concatenate(do_parts, axis=1)
            dob = do.astype(BF16)
            amat = a_ref[rows, :]
            dq_p, dk_p, di_p, tail_p = [], [], [], []
            for h in range(HG_HEADS):
                sl = _head(h)
                qh, kh, Gh = q[:, sl], k[:, sl], G[:, sl]
                Ap = jnp.where(r16 >= c16, _dnt(dob[:, sl], ib[:, sl]), 0.0)
                ApT = jnp.where(r16 <= c16, _dnt(ib[:, sl], dob[:, sl]), 0.0)
                dqh = jnp.zeros((SUB, HG_D), F32)
                dkh = jnp.zeros((SUB, HG_D), F32)
                for s in range(SUB):
                    dGs = Gh - Gh[s:s + 1, :]
                    e = jnp.exp(jnp.minimum(dGs, -dGs))
                    dqh = dqh + Ap[:, s:s + 1] * (e * kh[s:s + 1, :])
                    dkh = dkh + ApT[:, s:s + 1] * (e * qh[s:s + 1, :])
                Sb = s_ref[j, sl, :]
                dSb = dSt[h].astype(BF16)
                Am = amat[:, h * SUB:(h + 1) * SUB].astype(BF16)
                dq_p.append(dqh + eG[:, sl] * _dnn(dob[:, sl], Sb))
                dk_p.append(dkh + edl[:, sl] * _dnn(ib[:, sl], dSb))
                di_p.append(_dtn(Am, dob[:, sl]) + _dnt(kd[:, sl], dSb))
                tail_p.append(jnp.sum(dSt[h] * S_next[h].astype(F32), axis=0, keepdims=True))
                S_next[h] = Sb
                dSt[h] = eGl[:, sl] * dSt[h] + _dtn(dob[:, sl], qt[:, sl])
            dq = jnp.concatenate(dq_p, axis=1)
            dk = jnp.concatenate(dk_p, axis=1)
            di = jnp.concatenate(di_p, axis=1)
            dg = _cumsum_rows(q * dq - k * dk, reverse=True) + jnp.concatenate(tail_p, axis=1)
            df = jnp.where(f > F_FLOOR, dg / f, 0.0)
            dfk = df - dk
            dfz = (1.0 - lb) * dfk * sig * (1.0 - sig)
            dlb_acc[...] += jnp.sum(dfk * (1.0 - sig), axis=0, keepdims=True)
            dh_ref[rows, :] = jnp.concatenate([dq * Q_SCALE, dfz, di, dgz], axis=1).astype(BF16)
            return carry

        lax.fori_loop(0, nsub, block, 0)

        @pl.when(step == n - 1)
        def _():
            if layer == 0:
                dlb_ref[...] = jnp.zeros_like(dlb_ref)
            else:
                dz1 = p0 * p1 * dlb_acc[...]
                dlb_ref[...] = jnp.concatenate([-dz1, dz1], axis=0)

    rev = lambda i: n - 1 - i
    col = lambda c: pl.BlockSpec((ts, HG_W), lambda i: (rev(i), c))
    return pl.pallas_call(
        body, name=name, grid=(n,),
        in_specs=[col(0), col(1), col(2), col(3), col(0),
                  pl.BlockSpec((nsub, HG_W, HG_D), lambda i: (rev(i), 0, 0)),
                  pl.BlockSpec((ts, HG_HEADS * SUB), lambda i: (rev(i), 0)), col(0),
                  pl.BlockSpec((2, HG_W), lambda i: (0, 0)), pl.BlockSpec((1, HG_D), lambda i: (0, 0))],
        out_specs=[pl.BlockSpec((ts, 4 * HG_W), lambda i: (rev(i), 0)),
                   pl.BlockSpec((2, HG_W), lambda i: (0, 0)), pl.BlockSpec((1, HG_D), lambda i: (0, 0))],
        out_shape=[jax.ShapeDtypeStruct((L, 4 * HG_W), BF16), jax.ShapeDtypeStruct((2, HG_W), F32),
                   jax.ShapeDtypeStruct((1, HG_D), F32)],
        scratch_shapes=[pltpu.VMEM((HG_HEADS, HG_D, HG_D), F32), pltpu.VMEM((1, HG_W), F32),
                        pltpu.VMEM((HG_HEADS, HG_D, HG_D), BF16)],
        compiler_params=_params(("arbitrary",)),
    )(hh, hh, hh, hh, o_raw, states, amat, dcat, lbp, gnorm)


def _adamw_body(gp_ref, w_ref, m_ref, v_ref, g_ref, d_ref, mo_ref, vo_ref):
    c1 = 1.0 - ADAM_B1 ** ADAM_STEP
    c2 = 1.0 - ADAM_B2 ** ADAM_STEP
    g = gp_ref[0].astype(F32)
    for k in range(1, N_DEV):
        g = g + gp_ref[k].astype(F32)
    mn = ADAM_B1 * m_ref[...] + (1.0 - ADAM_B1) * g
    vn = ADAM_B2 * v_ref[...] + (1.0 - ADAM_B2) * (g * g)
    m_hat = mn / c1
    v_hat = vn / c2
    g_ref[...] = g
    d_ref[...] = -ADAM_LR * (m_hat / (jnp.sqrt(v_hat) + ADAM_EPS) + ADAM_WD * w_ref[...])
    mo_ref[...] = mn
    vo_ref[...] = vn


def _adamw_layers(gparts, w, m, v, name):
    depth, R, C = w.shape
    tr = _tile(R, 256, 16)
    nr = R // tr

    def body(*refs):
        layer = pl.program_id(0)
        for d in range(depth):
            @pl.when(layer == d)
            def _(d=d):
                _adamw_body(refs[d], *refs[depth:])

    def parts_spec(d):
        return pl.BlockSpec((N_DEV, tr, C),
                            lambda l, i: (0, jnp.where(l == d, i, jnp.where(l < d, 0, nr - 1)), 0))

    blk = pl.BlockSpec((None, tr, C), lambda l, i: (l, i, 0))
    shp = jax.ShapeDtypeStruct((depth, R, C), F32)
    return pl.pallas_call(
        body, name=name, grid=(depth, nr),
        in_specs=[parts_spec(d) for d in range(depth)] + [blk, blk, blk],
        out_specs=[blk, blk, blk, blk], out_shape=[shp, shp, shp, shp],
        compiler_params=_params(("arbitrary", "arbitrary")),
    )(*gparts, w, m, v)


def _adamw(gparts, w, m, v, name):
    R = w.shape[0]
    tr = _tile(R, 1024, 16) if R % 16 == 0 else R

    def body(*refs):
        _adamw_body(*refs)

    row = pl.BlockSpec((tr, LANES), lambda i: (i, 0))
    shp = jax.ShapeDtypeStruct((R, LANES), F32)
    return pl.pallas_call(
        body, name=name, grid=(R // tr,),
        in_specs=[pl.BlockSpec((N_DEV, tr, LANES), lambda i: (0, i, 0)), row, row, row],
        out_specs=[row, row, row, row], out_shape=[shp, shp, shp, shp],
        compiler_params=_params(("parallel",)),
    )(gparts, w, m, v)


def _flip(coord, bit):
    return 1 - coord if bit else coord


def _gather_many(blocks, name):
    n = len(blocks)

    def body(*refs):
        x_refs, out_refs = refs[:n], refs[n:2 * n]
        send_sems, recv_sems, local_sems = refs[2 * n:]
        x, y, c = lax.axis_index("x"), lax.axis_index("y"), lax.axis_index("c")
        me, sibling = (x, y, c), (x, y, 1 - c)
        chips = [(1 - x, y), (x, 1 - y), (1 - x, 1 - y)]

        def slot(a, px, py, pc):
            return out_refs[a].at[4 * px + 2 * py + pc]

        def copy(a, k, blk, to, src=None):
            return pltpu.make_async_remote_copy(
                src_ref=slot(a, *blk) if src is None else src, dst_ref=slot(a, *blk),
                send_sem=send_sems.at[7 * a + k], recv_sem=recv_sems.at[7 * a + k],
                device_id=to, device_id_type=pl.DeviceIdType.MESH)

        mine = [pltpu.make_async_copy(x_refs[a], slot(a, *me), local_sems.at[a]) for a in range(n)]
        for cp in mine:
            cp.start()
        first = [copy(a, 0, me, sibling, src=x_refs[a]) for a in range(n)]
        for j, chip in enumerate(chips):
            first += [copy(a, 1 + j, me, (*chip, c), src=x_refs[a]) for a in range(n)]
        for cp in first:
            cp.start()
        passed = []
        for j, chip in enumerate(chips):
            for a in range(n):
                copy(a, 1 + j, (*chip, c), me).wait_recv()
                fwd = copy(a, 4 + j, (*chip, c), sibling)
                fwd.start()
                passed.append(fwd)
        for a in range(n):
            copy(a, 0, sibling, me).wait_recv()
        for j, chip in enumerate(chips):
            for a in range(n):
                copy(a, 4 + j, (*chip, 1 - c), me).wait_recv()
        for cp in first + passed:
            cp.wait_send()
        for cp in mine:
            cp.wait()

    hbm = pl.BlockSpec(memory_space=pl.ANY)
    return pl.pallas_call(
        body, name=name,
        out_shape=[jax.ShapeDtypeStruct((N_DEV,) + b.shape, b.dtype) for b in blocks],
        in_specs=[hbm] * n, out_specs=[hbm] * n,
        scratch_shapes=[pltpu.SemaphoreType.DMA((7 * n,)), pltpu.SemaphoreType.DMA((7 * n,)),
                        pltpu.SemaphoreType.DMA((n,))],
    )(*blocks)


def _split_start(blocks, chunked, name):
    n = len(blocks)
    lands = [lax.empty(b.shape if chunked else (N_DEV,) + b.shape, b.dtype) for b in blocks]

    def body(*refs):
        x_refs, land_refs = refs[:n], refs[n:2 * n]
        send_sems, recv_sems, token = refs[2 * n], refs[2 * n + 1], refs[-1]
        x, y, c = lax.axis_index("x"), lax.axis_index("y"), lax.axis_index("c")
        me = 4 * x + 2 * y + c
        for a in range(n):
            for k in range(1, N_DEV):
                px, py, pc = _flip(x, k & 4), _flip(y, k & 2), _flip(c, k & 1)
                pltpu.make_async_remote_copy(
                    src_ref=x_refs[a].at[4 * px + 2 * py + pc] if chunked else x_refs[a],
                    dst_ref=land_refs[a].at[me],
                    send_sem=send_sems.at[7 * a + k - 1], recv_sem=recv_sems.at[7 * a + k - 1],
                    device_id=(px, py, pc), device_id_type=pl.DeviceIdType.MESH).start()
        token[...] = jnp.zeros_like(token)

    hbm = pl.BlockSpec(memory_space=pltpu.HBM)
    sem = pl.BlockSpec(memory_space=pltpu.SEMAPHORE)
    outs = pl.pallas_call(
        body, name=name,
        out_shape=(pltpu.SemaphoreType.DMA((7 * n,)), pltpu.SemaphoreType.DMA((7 * n,)),
                   *[pltpu.HBM(b.shape, b.dtype) for b in blocks], *[pltpu.HBM(l.shape, l.dtype) for l in lands],
                   jax.ShapeDtypeStruct((8, LANES), F32)),
        in_specs=[hbm] * (2 * n),
        out_specs=(sem, sem, *[hbm] * (2 * n), pl.BlockSpec(memory_space=pltpu.VMEM)),
        input_output_aliases={i: 2 + i for i in range(2 * n)},
        compiler_params=pltpu.CompilerParams(has_side_effects=pltpu.SideEffectType.DATAFLOW_SIDE_EFFECTING),
    )(*[pltpu.with_memory_space_constraint(b, pltpu.HBM) for b in blocks],
      *[pltpu.with_memory_space_constraint(l, pltpu.HBM) for l in lands])
    return outs[0], outs[1], list(outs[2:2 + n]), list(outs[2 + n:2 + 2 * n]), outs[-1]


def _split_wait(started, chunked, after, name):
    send_sems, recv_sems, blocks, lands, _ = started
    n = len(blocks)

    def body(*refs):
        x_refs, land_refs = refs[:n], refs[n:2 * n]
        send_sems, recv_sems = refs[2 * n], refs[2 * n + 1]
        x, y, c = lax.axis_index("x"), lax.axis_index("y"), lax.axis_index("c")
        for a in range(n):
            for k in range(1, N_DEV):
                px, py, pc = _flip(x, k & 4), _flip(y, k & 2), _flip(c, k & 1)
                copy = pltpu.make_async_remote_copy(
                    src_ref=x_refs[a].at[4 * px + 2 * py + pc] if chunked else x_refs[a],
                    dst_ref=land_refs[a].at[4 * px + 2 * py + pc],
                    send_sem=send_sems.at[7 * a + k - 1], recv_sem=recv_sems.at[7 * a + k - 1],
                    device_id=(px, py, pc), device_id_type=pl.DeviceIdType.MESH)
                copy.wait_send()
                copy.wait_recv()

    hbm = pl.BlockSpec(memory_space=pltpu.HBM)
    sem = pl.BlockSpec(memory_space=pltpu.SEMAPHORE)
    outs = pl.pallas_call(
        body, name=name,
        out_shape=(*[pltpu.HBM(b.shape, b.dtype) for b in blocks], *[pltpu.HBM(l.shape, l.dtype) for l in lands]),
        in_specs=[hbm] * (2 * n) + [sem, sem, pl.BlockSpec(memory_space=pl.ANY)],
        out_specs=[hbm] * (2 * n),
        input_output_aliases={i: i for i in range(2 * n)},
        compiler_params=pltpu.CompilerParams(has_side_effects=pltpu.SideEffectType.DATAFLOW_SIDE_EFFECTING),
    )(*blocks, *lands, send_sems, recv_sems, after)
    me = 4 * lax.axis_index("x") + 2 * lax.axis_index("y") + lax.axis_index("c")
    own = [lax.dynamic_index_in_dim(b, me, 0, keepdims=False) if chunked else b for b in outs[:n]]
    return [lax.dynamic_update_index_in_dim(z, o, me, 0) for z, o in zip(outs[n:], own)]


def _exchange_grads(layer_chunks, small_chunks, rep_block, name):
    flows, inputs = [], []
    for p, per_layer in enumerate(layer_chunks):
        for l, arr in enumerate(per_layer):
            flows.append(("param", p, l))
            inputs.append(arr)
    flows += [("small",), ("rep",)]
    inputs += [small_chunks, rep_block]
    n_par = len(layer_chunks)
    n_in, n_out, nf = len(inputs), n_par + 2, len(flows)

    def body(*refs):
        in_refs, out_refs = refs[:n_in], refs[n_in:n_in + n_out]
        send_sems, recv_sems, local_sems = refs[n_in + n_out:]
        x, y, c = lax.axis_index("x"), lax.axis_index("y"), lax.axis_index("c")
        me = 4 * x + 2 * y + c

        def src(f, dev):
            return in_refs[f] if flows[f][0] == "rep" else in_refs[f].at[dev]

        def dst(f, dev):
            if flows[f][0] == "param":
                _, p, l = flows[f]
                return out_refs[p].at[dev, l]
            return out_refs[n_par + (0 if flows[f][0] == "small" else 1)].at[dev]

        mine = [pltpu.make_async_copy(src(f, me), dst(f, me), local_sems.at[f]) for f in range(nf)]
        for cp in mine:
            cp.start()
        copies = []
        for k in range(1, N_DEV):
            px, py, pc = _flip(x, k & 4), _flip(y, k & 2), _flip(c, k & 1)
            peer = 4 * px + 2 * py + pc
            for f in range(nf):
                sems = dict(send_sem=send_sems.at[7 * f + k - 1], recv_sem=recv_sems.at[7 * f + k - 1],
                            device_id=(px, py, pc), device_id_type=pl.DeviceIdType.MESH)
                send = pltpu.make_async_remote_copy(src_ref=src(f, peer), dst_ref=dst(f, me), **sems)
                recv = pltpu.make_async_remote_copy(src_ref=src(f, peer), dst_ref=dst(f, peer), **sems)
                send.start()
                copies.append((send, recv))
        for send, recv in copies:
            recv.wait_recv()
        for send, recv in copies:
            send.wait_send()
        for cp in mine:
            cp.wait()

    out_shape = [jax.ShapeDtypeStruct((N_DEV, len(pl_)) + pl_[0].shape[1:], pl_[0].dtype) for pl_ in layer_chunks]
    out_shape += [jax.ShapeDtypeStruct(small_chunks.shape, small_chunks.dtype),
                  jax.ShapeDtypeStruct((N_DEV,) + rep_block.shape, rep_block.dtype)]
    hbm = pl.BlockSpec(memory_space=pl.ANY)
    return pl.pallas_call(
        body, name=name, out_shape=out_shape,
        in_specs=[hbm] * n_in, out_specs=[hbm] * n_out,
        scratch_shapes=[pltpu.SemaphoreType.DMA((7 * nf,)), pltpu.SemaphoreType.DMA((7 * nf,)),
                        pltpu.SemaphoreType.DMA((nf,))],
    )(*inputs)


def _pack_rows(size):
    return -(-size // (8 * LANES)) * 8


def _pack(arrs, dtype):
    parts, offs, r = [], [], 0
    for a in arrs:
        flat = a.astype(dtype).reshape(-1)
        nrow = _pack_rows(flat.shape[0])
        flat = jnp.pad(flat, (0, nrow * LANES - flat.shape[0]))
        parts.append(flat.reshape(nrow, LANES))
        offs.append((r, nrow))
        r += nrow
    return jnp.concatenate(parts, axis=0), offs


def _unpack(buf, offs, shapes, lead=()):
    outs = []
    for (r, nrow), shp in zip(offs, shapes):
        size = 1
        for s in shp:
            size *= s
        flat = buf[..., r:r + nrow, :].reshape(lead + (nrow * LANES,))
        outs.append(flat[..., :size].reshape(lead + tuple(shp)))
    return outs


def _cols_from_shards(g, axis):
    return jnp.concatenate([g[j] for j in range(N_DEV)], axis=axis)


BIG = ("w_in", "w_o", "w_up", "w_down")
SMALL_SHARDED = ("meta_tokens", "w_conv", "w_ffn_conv")
REPLICATED = ("hg_lower_bounds", "w_pool", "pool_scale", "hg_norm_g", "ln1_g", "ln1_b", "b_ffn_conv", "ln2_g", "ln2_b")
WEIGHTS = ("meta_tokens", "hg_lower_bounds", "w_in", "w_conv", "w_pool", "pool_scale", "hg_norm_g", "w_o",
           "ln1_g", "ln1_b", "w_up", "w_ffn_conv", "b_ffn_conv", "w_down", "ln2_g", "ln2_b")


def _pool_blockdiag(w_pool_l):
    z = jnp.zeros((POOL_GROUP, POOL_GROUP), w_pool_l.dtype)
    rows = [jnp.concatenate([w_pool_l[g] if h == g else z for h in range(4)], axis=1) for g in range(4)]
    return jnp.concatenate(rows, axis=0)


def _in_weights(g_in):
    w_in = jnp.transpose(g_in, (1, 0, 2)).reshape(D_MODEL, -1)
    return dict(w_hg=w_in[:, 768:2816], w_cp=jnp.concatenate([w_in[:, 0:768], w_in[:, 2816:3072]], axis=1))


def _rest_weights(g_o, g_up, g_down):
    w_o = g_o.reshape(-1, D_MODEL)
    return dict(w_o=jnp.concatenate([w_o[256:768], w_o[0:256], w_o[768:1024]], axis=0),
                w_up=jnp.transpose(g_up, (1, 0, 2)).reshape(D_MODEL, -1), w_down=g_down.reshape(-1, D_MODEL))


def kernel(x, meta_tokens, hg_lower_bounds, w_in, w_conv, w_pool, pool_scale, hg_norm_g, w_o, ln1_g, ln1_b, w_up, w_ffn_conv, b_ffn_conv, w_down, ln2_g, ln2_b, loss_target, m_meta_tokens, m_hg_lower_bounds, m_w_in, m_w_conv, m_w_pool, m_pool_scale, m_hg_norm_g, m_w_o, m_ln1_g, m_ln1_b, m_w_up, m_w_ffn_conv, m_b_ffn_conv, m_w_down, m_ln2_g, m_ln2_b, v_meta_tokens, v_hg_lower_bounds, v_w_in, v_w_conv, v_w_pool, v_pool_scale, v_hg_norm_g, v_w_o, v_ln1_g, v_ln1_b, v_w_up, v_w_ffn_conv, v_b_ffn_conv, v_w_down, v_ln2_g, v_ln2_b):
    W = dict(meta_tokens=meta_tokens, hg_lower_bounds=hg_lower_bounds, w_in=w_in, w_conv=w_conv, w_pool=w_pool,
             pool_scale=pool_scale, hg_norm_g=hg_norm_g, w_o=w_o, ln1_g=ln1_g, ln1_b=ln1_b, w_up=w_up,
             w_ffn_conv=w_ffn_conv, b_ffn_conv=b_ffn_conv, w_down=w_down, ln2_g=ln2_g, ln2_b=ln2_b)
    M = dict(meta_tokens=m_meta_tokens, hg_lower_bounds=m_hg_lower_bounds, w_in=m_w_in, w_conv=m_w_conv,
             w_pool=m_w_pool, pool_scale=m_pool_scale, hg_norm_g=m_hg_norm_g, w_o=m_w_o, ln1_g=m_ln1_g,
             ln1_b=m_ln1_b, w_up=m_w_up, w_ffn_conv=m_w_ffn_conv, b_ffn_conv=m_b_ffn_conv, w_down=m_w_down,
             ln2_g=m_ln2_g, ln2_b=m_ln2_b)
    V = dict(meta_tokens=v_meta_tokens, hg_lower_bounds=v_hg_lower_bounds, w_in=v_w_in, w_conv=v_w_conv,
             w_pool=v_w_pool, pool_scale=v_pool_scale, hg_norm_g=v_hg_norm_g, w_o=v_w_o, ln1_g=v_ln1_g,
             ln1_b=v_ln1_b, w_up=v_w_up, w_ffn_conv=v_w_ffn_conv, b_ffn_conv=v_b_ffn_conv, w_down=v_w_down,
             ln2_g=v_ln2_g, ln2_b=v_ln2_b)
    assert x.shape[0] == 1 and x.shape[2] == D_MODEL and w_in.shape[0] == DEPTH
    seq = x.shape[1]
    L = -(-(seq + N_META) // ROW_ALIGN) * ROW_ALIGN

    small_pack, small_offs = _pack([W[n] for n in SMALL_SHARDED], F32)
    shards = {n: [W[n][l].astype(BF16) for l in range(DEPTH)] for n in BIG}
    g_in0, small_all = _gather_many([shards["w_in"][0], small_pack], "gather_weights")
    full = {}
    for n, a in zip(SMALL_SHARDED, _unpack(small_all, small_offs, [W[n].shape for n in SMALL_SHARDED], (N_DEV,))):
        full[n] = _cols_from_shards(a, 1)
    order = (small_all[0, 0, 0] * 0.0).astype(BF16)
    rest0_started = _split_start([shards[n][0] + order for n in BIG[1:]], False, "gather_rest0_start")
    order = rest0_started[4][0, 0].astype(BF16)
    layer1_started = _split_start([shards[n][1] + order for n in BIG], False, "gather_layer1_start")
    lb_in = hg_lower_bounds + layer1_started[4][0, 0]

    pad_rows = L - N_META - seq
    xp = jnp.concatenate([full["meta_tokens"], x[0], jnp.zeros((pad_rows, D_MODEL), F32)], axis=0)
    tgt = jnp.concatenate([jnp.zeros((N_META, D_MODEL), F32), loss_target[0], jnp.zeros((pad_rows, D_MODEL), F32)], axis=0)

    saved = []
    h_in, h_in_b = xp, xp.astype(BF16)
    for l in range(DEPTH):
        if l == 0:
            lw = _in_weights(g_in0)
        else:
            g_in, *g_rest = _split_wait(layer1_started, False, h_in_b, "gather_layer1_wait")
            lw = {**_in_weights(g_in), **_rest_weights(*g_rest)}
        wc = full["w_conv"][l].T
        wblk = _pool_blockdiag(w_pool[l]).astype(BF16)
        ps = pool_scale[l][None, :]
        gn = hg_norm_g[l][None, :]
        wf = full["w_ffn_conv"][l].T
        bf = b_ffn_conv[l][None, :]
        hh = _matmul(h_in_b, lw["w_hg"], "nn", F32, f"fwd_hg_{l}")
        hc = _matmul(h_in_b, lw["w_cp"], "nn", F32, f"fwd_cp_{l}")
        y_hg, o_raw, states, amat = _hgrn_fwd(hh, lb_in if l == 0 else hg_lower_bounds, gn, l, f"hgrn_fwd_{l}")
        y_cp = _cp_fwd(hc, wc, wblk, ps, f"convpool_fwd_{l}")
        cat = jnp.concatenate([y_hg, y_cp], axis=1)
        if l == 0:
            lw.update(_rest_weights(*_split_wait(rest0_started, False, cat, "gather_rest0_wait")))
        z1, x1, x1_b = _matmul_ln(cat, lw["w_o"], h_in, ln1_g[l][None, :], ln1_b[l][None, :], f"fwd_o_ln1_{l}")
        up = _matmul(x1_b, lw["w_up"], "nn", BF16, f"fwd_up_{l}")
        a, u = _ffn_act_fwd(up, wf, bf, f"ffn_fwd_{l}")
        saved.append(dict(lw=lw, wc=wc, wblk=wblk, ps=ps, gn=gn, wf=wf, bf=bf, x_b=h_in_b, hh=hh, hc=hc,
                          o_raw=o_raw, states=states, amat=amat, cat=cat, z1=z1, x1_b=x1_b, up=up, u=u, a=a))
        if l < DEPTH - 1:
            saved[l]["z2"], h_in, h_in_b = _matmul_ln(a, lw["w_down"], x1, ln2_g[l][None, :], ln2_b[l][None, :],
                                                     f"fwd_down_ln2_{l}")
        else:
            saved[l]["z2"], dy, loss_part = _matmul_ln(a, lw["w_down"], x1, ln2_g[l][None, :], ln2_b[l][None, :],
                                                       f"fwd_down_ln2_loss_{l}", loss=(tgt, seq))

    loss = lax.psum(loss_part[0, 0], ("x", "y", "c"))

    G = {}
    per_layer = {n: [None] * DEPTH for n in ("w_conv", "w_pool", "pool_scale", "hg_norm_g", "ln1_g", "ln1_b",
                                             "w_ffn_conv", "b_ffn_conv", "ln2_g", "ln2_b")}
    ffn_started, mix_started = [None] * DEPTH, [None] * DEPTH
    order = jnp.zeros((), F32)
    dlb_total = jnp.zeros((DEPTH, HG_W), F32)
    for l in reversed(range(DEPTH)):
        s = saved[l]
        lw = s["lw"]
        dz2_b, dg2, db2 = _ln_bwd(s["z2"], dy, ln2_g[l][None, :] + order, f"ln2_bwd_{l}")
        da = _matmul(dz2_b, lw["w_down"], "nt", BF16, f"bwd_da_{l}")
        d_w_down = _matmul(s["a"], dz2_b, "tn", BF16, f"wgrad_down_{l}")
        dup, dwf, dbf = _ffn_act_bwd(s["up"], s["u"], da, s["wf"], f"ffn_bwd_{l}")
        dx1 = _matmul(dup, lw["w_up"], "nt", BF16, f"bwd_dx1_{l}", res=dz2_b, alpha=ALPHA)
        d_w_up = _matmul(s["x1_b"], dup, "tn", BF16, f"wgrad_up_{l}")
        ffn_started[l] = _split_start([jnp.transpose(d_w_up.reshape(D_MODEL, N_DEV, -1), (1, 0, 2)),
                                       d_w_down.reshape(N_DEV, -1, D_MODEL)], True, f"scatter_ffn{l}_start")
        order = ffn_started[l][4][0, 0]
        dz1_b, dg1, db1 = _ln_bwd(s["z1"], dx1, ln1_g[l][None, :] + order, f"ln1_bwd_{l}")
        dcat = _matmul(dz1_b, lw["w_o"], "nt", BF16, f"bwd_dcat_{l}")
        d_w_o = _matmul(s["cat"], dz1_b, "tn", BF16, f"wgrad_o_{l}")
        dhh, dlb, dgn = _hgrn_bwd(s["hh"], s["o_raw"], s["states"], s["amat"], dcat, hg_lower_bounds, s["gn"], l,
                                  f"hgrn_bwd_{l}")
        dhc, dwc, dwblk, dps = _cp_bwd(s["hc"], dcat, s["wc"], s["wblk"], s["ps"], f"convpool_bwd_{l}")
        d_w_hg = _matmul(s["x_b"], dhh, "tn", BF16, f"wgrad_hg_{l}")
        d_w_cp = _matmul(s["x_b"], dhc, "tn", BF16, f"wgrad_cp_{l}")
        d_w_in = jnp.concatenate([d_w_cp[:, 0:768], d_w_hg, d_w_cp[:, 768:1024]], axis=1)
        mix_chunks = [jnp.transpose(d_w_in.reshape(D_MODEL, N_DEV, -1), (1, 0, 2)),
                      jnp.concatenate([d_w_o[512:768], d_w_o[0:512], d_w_o[768:1024]], axis=0).reshape(N_DEV, -1, D_MODEL)]
        mix_started[l] = _split_start(mix_chunks, True, f"scatter_mix{l}_start")
        order = mix_started[l][4][0, 0]
        dx_a = _matmul(dhh, lw["w_hg"] + order.astype(BF16), "nt", F32, f"bwd_dx_hg_{l}", res=dz1_b, alpha=ALPHA)
        dx = _matmul(dhc, lw["w_cp"], "nt", F32, f"bwd_dx_cp_{l}", res=dx_a, alpha=1.0)
        per_layer["w_conv"][l] = dwc.T
        per_layer["w_ffn_conv"][l] = dwf.T
        per_layer["b_ffn_conv"][l] = dbf[0]
        per_layer["w_pool"][l] = jnp.stack([dwblk[g * 64:(g + 1) * 64, g * 64:(g + 1) * 64] for g in range(4)], axis=0)
        per_layer["pool_scale"][l] = dps[0]
        per_layer["hg_norm_g"][l] = dgn[0]
        per_layer["ln1_g"][l], per_layer["ln1_b"][l] = dg1[0], db1[0]
        per_layer["ln2_g"][l], per_layer["ln2_b"][l] = dg2[0], db2[0]
        dlb_total = dlb_total + dlb
        dy = dx
    for n, parts in per_layer.items():
        G[n] = jnp.stack(parts, axis=0)
    G["hg_lower_bounds"] = dlb_total
    grad_x = dy[N_META:N_META + seq][None]

    def shard_major(g, lead):
        g = g.reshape(g.shape[:lead] + (N_DEV, -1) + g.shape[lead + 1:])
        g = jnp.moveaxis(g, lead, 0).reshape(N_DEV, -1)
        nrow = _pack_rows(g.shape[1])
        return jnp.pad(g, ((0, 0), (0, nrow * LANES - g.shape[1]))).reshape(N_DEV, nrow, LANES)

    small_chunks = jnp.concatenate([shard_major(dy[0:N_META], 1), shard_major(G["w_conv"], 1),
                                    shard_major(G["w_ffn_conv"], 1)], axis=1)
    w_small, _ = _pack([W[n] for n in SMALL_SHARDED], F32)
    rep_pack, rep_offs = _pack([G[n] for n in REPLICATED], F32)
    small_recv, rep_all = _exchange_grads([], small_chunks, rep_pack, "exchange_grads")
    parts = {n: [] for n in BIG}
    for l in range(DEPTH):
        up_l, down_l = _split_wait(ffn_started[l], True, rep_all, f"scatter_ffn{l}_wait")
        in_l, o_l = _split_wait(mix_started[l], True, rep_all, f"scatter_mix{l}_wait")
        for n, a in zip(BIG, (in_l, o_l, up_l, down_l)):
            parts[n].append(a)

    res = {k: {} for k in ("grad", "delta", "new_m", "new_v")}
    kinds = ("grad", "delta", "new_m", "new_v")
    for n in BIG:
        for kind, a in zip(kinds, _adamw_layers(parts[n], W[n], M[n], V[n], f"adamw_{n}")):
            res[kind][n] = a
    m_small, _ = _pack([M[n] for n in SMALL_SHARDED], F32)
    v_small, _ = _pack([V[n] for n in SMALL_SHARDED], F32)
    outs_small = _adamw(small_recv, w_small, m_small, v_small, "adamw_small_sharded")
    w_rep, _ = _pack([W[n] for n in REPLICATED], F32)
    m_rep, _ = _pack([M[n] for n in REPLICATED], F32)
    v_rep, _ = _pack([V[n] for n in REPLICATED], F32)
    outs_rep = _adamw(rep_all, w_rep, m_rep, v_rep, "adamw_replicated")
    for kind, b_sm, b_rep in zip(kinds, outs_small, outs_rep):
        for n, a in zip(SMALL_SHARDED, _unpack(b_sm, small_offs, [W[n].shape for n in SMALL_SHARDED])):
            res[kind][n] = a
        for n, a in zip(REPLICATED, _unpack(b_rep, rep_offs, [W[n].shape for n in REPLICATED])):
            res[kind][n] = a

    return (loss, grad_x, *[res["grad"][n] for n in WEIGHTS], *[res["delta"][n] for n in WEIGHTS],
            *[res["new_m"][n] for n in WEIGHTS], *[res["new_v"][n] for n in WEIGHTS])
```

```python
import jax
import jax.numpy as jnp
from jax import lax
from jax.experimental import pallas as pl
from jax.experimental.pallas import tpu as pltpu

F32 = jnp.float32
BF16 = jnp.bfloat16

N_DEV = 8
D_MODEL = 1024
N_META = 16
DEPTH = 2
CONV_W = 256
HG_W = 512
HG_D = 128
HG_HEADS = 4
POOL_W = 256
POOL_GROUP = 64
D_FF = 2816
ALPHA = (2 * DEPTH) ** 0.25
LN_EPS = 1e-5
RMS_EPS = 1e-6
F_FLOOR = 1e-30
Q_SCALE = HG_D ** -0.5
SUB = 16
SEQ_TILE = 192
FFN_TILE = 192
CP_TILE_CAP = 768
ROW_ALIGN = 192
LANES = 128
VMEM_LIMIT = 48 * 1024 * 1024
MATMUL_VMEM_BUDGET = 38 * 1024 * 1024

ADAM_LR = 0.001
ADAM_B1 = 0.9
ADAM_B2 = 0.999
ADAM_EPS = 1e-08
ADAM_WD = 0.01
ADAM_STEP = 10


def _tile(n, cap, mult):
    best = 0
    for t in range(mult, min(n, cap) + 1, mult):
        if n % t == 0:
            best = t
    assert best > 0, (n, cap, mult)
    return best


def _params(sem, vmem=VMEM_LIMIT):
    return pltpu.CompilerParams(dimension_semantics=sem, vmem_limit_bytes=vmem)


def _dnt(a, b):
    return lax.dot_general(a, b, (((1,), (1,)), ((), ())), preferred_element_type=F32)


def _dtn(a, b):
    return lax.dot_general(a, b, (((0,), (0,)), ((), ())), preferred_element_type=F32)


def _dnn(a, b):
    return jnp.dot(a, b, preferred_element_type=F32)


def _sigmoid(x):
    return jax.nn.sigmoid(x)


def _matmul(a, b, mode, out_dtype, name, res=None, alpha=1.0):
    if mode == "tn":
        K, M = a.shape
    else:
        M, K = a.shape
    N = b.shape[0] if mode == "nt" else b.shape[1]
    out_bytes = jnp.dtype(out_dtype).itemsize
    if mode == "tn" and M % 512 == 0 and N % 512 == 0:
        tk, tm, tn = K, 512, 512
        nk, use_acc = 1, False
    else:
        tn = _tile(N, 1536, LANES)
        tk = _tile(K, 1536, 16) if mode == "tn" else _tile(K, 2816, LANES)
        nk = K // tk
        use_acc = nk > 1 and out_dtype != F32
        tm = M
        for cap in (1536, 768, 384):
            tm = _tile(M, cap, 16)
            blocks = 2 * (a.dtype.itemsize * tm * tk + b.dtype.itemsize * tn * tk + out_bytes * tm * tn
                          + (res.dtype.itemsize * tm * tn if res is not None else 0)) + (4 * tm * tn if use_acc else 0)
            if blocks <= MATMUL_VMEM_BUDGET:
                break
    dims = {"nn": ((1,), (0,)), "nt": ((1,), (1,)), "tn": ((0,), (0,))}[mode]

    def body(*refs):
        a_ref, b_ref = refs[0], refs[1]
        r_ref = refs[2] if res is not None else None
        o_ref = refs[3] if res is not None else refs[2]
        acc = refs[-1] if use_acc else o_ref
        k = pl.program_id(2)
        p = lax.dot_general(a_ref[...].astype(BF16), b_ref[...].astype(BF16), (dims, ((), ())),
                            preferred_element_type=F32)

        def finish(r):
            if r_ref is not None:
                r = r + alpha * r_ref[...].astype(F32)
            o_ref[...] = r.astype(out_dtype)

        if nk == 1:
            finish(p)
        else:
            @pl.when(k == 0)
            def _():
                acc[...] = p

            @pl.when((k > 0) & (k < nk - 1))
            def _():
                acc[...] += p

            @pl.when(k == nk - 1)
            def _():
                finish(acc[...] + p)

    if mode == "tn":
        a_spec = pl.BlockSpec((tk, tm), lambda i, j, k: (k, i))
    else:
        a_spec = pl.BlockSpec((tm, tk), lambda i, j, k: (i, k))
    if mode == "nt":
        b_spec = pl.BlockSpec((tn, tk), lambda i, j, k: (j, k))
    else:
        b_spec = pl.BlockSpec((tk, tn), lambda i, j, k: (k, j))
    in_specs = [a_spec, b_spec]
    args = [a, b]
    if res is not None:
        in_specs.append(pl.BlockSpec((tm, tn), lambda i, j, k: (i, j)))
        args.append(res)
    return pl.pallas_call(
        body, name=name,
        grid=(M // tm, N // tn, nk),
        in_specs=in_specs,
        out_specs=pl.BlockSpec((tm, tn), lambda i, j, k: (i, j)),
        out_shape=jax.ShapeDtypeStruct((M, N), out_dtype),
        scratch_shapes=[pltpu.VMEM((tm, tn), F32)] if use_acc else [],
        compiler_params=_params(("parallel", "parallel", "arbitrary")),
    )(*args)


def _matmul_ln(a, w, x, g, b, name, loss=None):
    L, K = a.shape
    D = w.shape[1]
    tr = L
    for cap in (1536, 768, 384):
        tr = _tile(L, cap, 16)
        if 2 * (2 * tr * K + 2 * K * D + 4 * tr * D * (4 if loss else 3) + 2 * tr * D) <= MATMUL_VMEM_BUDGET:
            break

    def body(*refs):
        a_ref, w_ref, x_ref, g_ref, b_ref = refs[:5]
        z = ALPHA * x_ref[...] + _dnn(a_ref[...], w_ref[...])
        mu = jnp.mean(z, axis=-1, keepdims=True)
        zc = z - mu
        var = jnp.mean(zc * zc, axis=-1, keepdims=True)
        y = zc * lax.rsqrt(var + LN_EPS) * g_ref[...] + b_ref[...]
        if loss is None:
            z_ref, y_ref, yb_ref = refs[5:]
            y_ref[...] = y
            yb_ref[...] = y.astype(BF16)
        else:
            t_ref, z_ref, dy_ref, loss_ref = refs[5:]
            i = pl.program_id(0)

            @pl.when(i == 0)
            def _():
                loss_ref[...] = jnp.zeros_like(loss_ref)

            r = i * tr + lax.broadcasted_iota(jnp.int32, (tr, D), 0)
            valid = (r >= N_META) & (r < N_META + loss[1])
            e = jnp.where(valid, y - t_ref[...], 0.0)
            dy_ref[...] = e * (1.0 / D)
            s = jnp.sum(jnp.sum(e * e, axis=-1, keepdims=True), axis=0, keepdims=True)
            loss_ref[...] += (0.5 / D) * s
        z_ref[...] = z.astype(BF16)

    row = pl.BlockSpec((tr, D), lambda i: (i, 0))
    vec = pl.BlockSpec((1, D), lambda i: (0, 0))
    in_specs = [pl.BlockSpec((tr, K), lambda i: (i, 0)), pl.BlockSpec((K, D), lambda i: (0, 0)), row, vec, vec]
    f32_rows = jax.ShapeDtypeStruct((L, D), F32)
    b16_rows = jax.ShapeDtypeStruct((L, D), BF16)
    if loss is None:
        args, out_specs = [a, w, x, g, b], [row, row, row]
        out_shape = [b16_rows, f32_rows, b16_rows]
    else:
        args, in_specs = [a, w, x, g, b, loss[0]], in_specs + [row]
        out_specs = [row, row, pl.BlockSpec((1, 1), lambda i: (0, 0))]
        out_shape = [b16_rows, f32_rows, jax.ShapeDtypeStruct((1, 1), F32)]
    return pl.pallas_call(
        body, name=name, grid=(L // tr,), in_specs=in_specs, out_specs=out_specs, out_shape=out_shape,
        compiler_params=_params(("arbitrary",) if loss else ("parallel",)),
    )(*args)


def _ln_bwd(z, dy, g, name):
    L, D = z.shape
    tr = _tile(L, 768, 16)

    def body(z_ref, dy_ref, g_ref, dzb_ref, dg_ref, db_ref):
        @pl.when(pl.program_id(0) == 0)
        def _():
            dg_ref[...] = jnp.zeros_like(dg_ref)
            db_ref[...] = jnp.zeros_like(db_ref)

        z = z_ref[...].astype(F32)
        mu = jnp.mean(z, axis=-1, keepdims=True)
        zc = z - mu
        var = jnp.mean(zc * zc, axis=-1, keepdims=True)
        rstd = lax.rsqrt(var + LN_EPS)
        xhat = zc * rstd
        dy = dy_ref[...].astype(F32)
        dxh = dy * g_ref[...]
        m1 = jnp.mean(dxh, axis=-1, keepdims=True)
        m2 = jnp.mean(dxh * xhat, axis=-1, keepdims=True)
        dz = rstd * (dxh - m1 - xhat * m2)
        dzb_ref[...] = dz.astype(BF16)
        dg_ref[...] += jnp.sum(dy * xhat, axis=0, keepdims=True)
        db_ref[...] += jnp.sum(dy, axis=0, keepdims=True)

    row = pl.BlockSpec((tr, D), lambda i: (i, 0))
    vec = pl.BlockSpec((1, D), lambda i: (0, 0))
    return pl.pallas_call(
        body, name=name, grid=(L // tr,),
        in_specs=[row, row, vec], out_specs=[row, vec, vec],
        out_shape=[jax.ShapeDtypeStruct((L, D), BF16),
                   jax.ShapeDtypeStruct((1, D), F32), jax.ShapeDtypeStruct((1, D), F32)],
        compiler_params=_params(("arbitrary",)),
    )(z, dy, g)


def _shift_down(x, prev, k):
    T, C = x.shape
    rot = pltpu.roll(jnp.concatenate([prev, x], axis=0).reshape(T // 8 + 1, 8, C), k, 1)
    sub = lax.broadcasted_iota(jnp.int32, (T // 8, 8, C), 1)
    return jnp.where(sub < k, rot[:-1], rot[1:]).reshape(T, C)


def _shift_up(x, nxt, k):
    T, C = x.shape
    rot = pltpu.roll(jnp.concatenate([x, nxt], axis=0).reshape(T // 8 + 1, 8, C), 8 - k, 1)
    sub = lax.broadcasted_iota(jnp.int32, (T // 8, 8, C), 1)
    return jnp.where(sub >= 8 - k, rot[1:], rot[:-1]).reshape(T, C)


def _conv3(x, prev, w, b):
    return w[2:3, :] * x + w[1:2, :] * _shift_down(x, prev, 1) + w[0:1, :] * _shift_down(x, prev, 2) + b


def _ffn_act_fwd(up, w, b, name):
    L, C = up.shape
    F = C // 2
    ts = FFN_TILE
    n = L // ts

    def body(up_ref, pv_ref, w_ref, b_ref, a_ref, u_ref):
        i = pl.program_id(0)
        x = up_ref[...].astype(F32)
        prev = jnp.where(i > 0, pv_ref[...].astype(F32)[8:16], 0.0)
        u = _conv3(x, prev, w_ref[...], b_ref[...])
        u_ref[...] = u.astype(BF16)
        gate = u[:, :F]
        a_ref[...] = (gate * _sigmoid(gate) * u[:, F:]).astype(BF16)

    return pl.pallas_call(
        body, name=name, grid=(n,),
        in_specs=[pl.BlockSpec((ts, C), lambda i: (i, 0)),
                  pl.BlockSpec((16, C), lambda i: (jnp.maximum(i * (ts // 16) - 1, 0), 0)),
                  pl.BlockSpec((3, C), lambda i: (0, 0)), pl.BlockSpec((1, C), lambda i: (0, 0))],
        out_specs=[pl.BlockSpec((ts, F), lambda i: (i, 0)), pl.BlockSpec((ts, C), lambda i: (i, 0))],
        out_shape=[jax.ShapeDtypeStruct((L, F), BF16), jax.ShapeDtypeStruct((L, C), BF16)],
        compiler_params=_params(("parallel",)),
    )(up, up, w, b)


def _ffn_act_bwd(up, u, da, w, name):
    L, C = up.shape
    F = C // 2
    ts = FFN_TILE
    n = L // ts
    last16 = L // 16 - 1

    def du_of(u, da):
        gate, val = u[:, :F], u[:, F:]
        sg = _sigmoid(gate)
        dgate = da * val * (sg * (1.0 + gate * (1.0 - sg)))
        dval = da * (gate * sg)
        return jnp.concatenate([dgate, dval], axis=1)

    def body(up_ref, u_ref, un_ref, da_ref, dan_ref, w_ref, dup_ref, dw_ref, db_ref):
        i = pl.program_id(0)

        @pl.when(i == 0)
        def _():
            dw_ref[...] = jnp.zeros_like(dw_ref)
            db_ref[...] = jnp.zeros_like(db_ref)

        w = w_ref[...]
        x = up_ref[...].astype(F32)
        du = du_of(u_ref[...].astype(F32), da_ref[...].astype(F32))
        dun = jnp.where(i < n - 1, du_of(un_ref[...].astype(F32)[0:8], dan_ref[...].astype(F32)[0:8]), 0.0)
        du1 = _shift_up(du, dun, 1)
        du2 = _shift_up(du, dun, 2)
        dup_ref[...] = (w[2:3, :] * du + w[1:2, :] * du1 + w[0:1, :] * du2).astype(BF16)
        dw_ref[...] += jnp.concatenate([jnp.sum(x * du2, axis=0, keepdims=True),
                                        jnp.sum(x * du1, axis=0, keepdims=True),
                                        jnp.sum(x * du, axis=0, keepdims=True)], axis=0)
        db_ref[...] += jnp.sum(du, axis=0, keepdims=True)

    nxt = lambda i: (jnp.minimum((i + 1) * (ts // 16), last16), 0)
    return pl.pallas_call(
        body, name=name, grid=(n,),
        in_specs=[pl.BlockSpec((ts, C), lambda i: (i, 0)),
                  pl.BlockSpec((ts, C), lambda i: (i, 0)), pl.BlockSpec((16, C), nxt),
                  pl.BlockSpec((ts, F), lambda i: (i, 0)), pl.BlockSpec((16, F), nxt),
                  pl.BlockSpec((3, C), lambda i: (0, 0))],
        out_specs=[pl.BlockSpec((ts, C), lambda i: (i, 0)), pl.BlockSpec((3, C), lambda i: (0, 0)),
                   pl.BlockSpec((1, C), lambda i: (0, 0))],
        out_shape=[jax.ShapeDtypeStruct((L, C), BF16), jax.ShapeDtypeStruct((3, C), F32),
                   jax.ShapeDtypeStruct((1, C), F32)],
        compiler_params=_params(("arbitrary",)),
    )(up, u, u, da, da, w)


def _pool_window(ext, tile_rows, first_row, lead):
    T = ext.shape[0]
    sh = (lambda x, k: pltpu.roll(x, T - k, 0)) if lead else (lambda x, k: pltpu.roll(x, k, 0))
    r2 = ext + sh(ext, 1)
    r4 = r2 + sh(r2, 2)
    r8 = r4 + sh(r4, 4)
    r16 = r8 + sh(r8, 8)
    lo = 0 if lead else 16
    grp = lax.broadcasted_iota(jnp.int32, (tile_rows, POOL_W), 1) // POOL_GROUP
    pick = lambda a, b, c, d: jnp.where(grp == 0, a, jnp.where(grp == 1, b, jnp.where(grp == 2, c, d)))
    win = pick(r2[lo:lo + tile_rows], r4[lo:lo + tile_rows], r8[lo:lo + tile_rows], r16[lo:lo + tile_rows])
    return win, pick(2.0, 4.0, 8.0, 16.0)


def _pool_count(first_row, rows, wlen):
    t1 = (first_row + lax.broadcasted_iota(jnp.int32, (rows, POOL_W), 0) + 1).astype(F32)
    return jnp.minimum(t1, wlen)


def _cp_fwd(h, col, wc, wblk, pscale, name):
    L = h.shape[0]
    ts = _tile(L, CP_TILE_CAP, 16)
    n = L // ts

    def body(h_ref, hp_ref, wc_ref, wb_ref, ps_ref, y_ref):
        i = pl.program_id(0)
        h = h_ref[...]
        hp = jnp.where(i > 0, hp_ref[...], 0.0)
        cb, cc, cv, pv = h[:, 0:256], h[:, 256:512], h[:, 512:768], h[:, 768:1024]
        p = cc * cv
        pp = hp[8:16, 256:512] * hp[8:16, 512:768]
        w = wc_ref[...]
        conv = w[2:3, :] * p + w[1:2, :] * _shift_down(p, pp, 1) + w[0:1, :] * _shift_down(p, pp, 2)
        y_conv = cb * conv
        ext = jnp.concatenate([hp[:, 768:1024], pv], axis=0)
        win, wlen = _pool_window(ext, ts, i * ts, False)
        d = win / _pool_count(i * ts, ts, wlen) - pv
        y_pool = _dnn(d.astype(BF16), wb_ref[...]) * ps_ref[...]
        y_ref[...] = jnp.concatenate([y_conv, y_pool], axis=1).astype(BF16)

    return pl.pallas_call(
        body, name=name, grid=(n,),
        in_specs=[pl.BlockSpec((ts, 1024), lambda i: (i, col)),
                  pl.BlockSpec((16, 1024), lambda i: (jnp.maximum(i * (ts // 16) - 1, 0), col)),
                  pl.BlockSpec((3, 256), lambda i: (0, 0)), pl.BlockSpec((256, 256), lambda i: (0, 0)),
                  pl.BlockSpec((1, 256), lambda i: (0, 0))],
        out_specs=pl.BlockSpec((ts, 512), lambda i: (i, 0)),
        out_shape=jax.ShapeDtypeStruct((L, 512), BF16),
        compiler_params=_params(("parallel",)),
    )(h, h, wc, wblk, pscale)


def _cp_bwd(h, col, dcat, dh, wc, wblk, pscale, name):
    L = h.shape[0]
    ts = _tile(L, CP_TILE_CAP, 16)
    n = L // ts
    last16 = L // 16 - 1

    def body(h_ref, hp_ref, hn_ref, dy_ref, dyn_ref, wc_ref, wb_ref, ps_ref, dh_in_ref,
             dh_ref, dwc_ref, dwb_ref, dps_ref):
        i = pl.program_id(0)

        @pl.when(i == 0)
        def _():
            dwc_ref[...] = jnp.zeros_like(dwc_ref)
            dwb_ref[...] = jnp.zeros_like(dwb_ref)
            dps_ref[...] = jnp.zeros_like(dps_ref)

        h = h_ref[...]
        hp = jnp.where(i > 0, hp_ref[...], 0.0)
        hn = hn_ref[...]
        dy = dy_ref[...].astype(F32)
        dyn = jnp.where(i < n - 1, dyn_ref[...].astype(F32), 0.0)
        cb, cc, cv, pv = h[:, 0:256], h[:, 256:512], h[:, 512:768], h[:, 768:1024]
        w = wc_ref[...]
        p = cc * cv
        pp = hp[8:16, 256:512] * hp[8:16, 512:768]
        p1 = _shift_down(p, pp, 1)
        p2 = _shift_down(p, pp, 2)
        conv = w[2:3, :] * p + w[1:2, :] * p1 + w[0:1, :] * p2
        dyc = dy[:, 0:256]
        dcb = dyc * conv
        dconv = dyc * cb
        dconv_n = dyn[0:8, 0:256] * hn[0:8, 0:256]
        dc1 = _shift_up(dconv, dconv_n, 1)
        dc2 = _shift_up(dconv, dconv_n, 2)
        dp = w[2:3, :] * dconv + w[1:2, :] * dc1 + w[0:1, :] * dc2
        dwc_ref[...] += jnp.concatenate([jnp.sum(p * dc2, axis=0, keepdims=True),
                                         jnp.sum(p * dc1, axis=0, keepdims=True),
                                         jnp.sum(p * dconv, axis=0, keepdims=True)], axis=0)
        ps = ps_ref[...]
        wb = wb_ref[...]
        ext = jnp.concatenate([hp[:, 768:1024], pv], axis=0)
        win, wlen = _pool_window(ext, ts, i * ts, False)
        d = win / _pool_count(i * ts, ts, wlen) - pv
        db = d.astype(BF16)
        dyp = dy[:, 256:512]
        dps_ref[...] += jnp.sum(dyp * _dnn(db, wb), axis=0, keepdims=True)
        dypre = (dyp * ps).astype(BF16)
        dwb_ref[...] += _dtn(db, dypre)
        dd = _dnt(dypre, wb)
        ddn = _dnt((dyn[:, 256:512] * ps).astype(BF16), wb)
        e = dd / _pool_count(i * ts, ts, wlen)
        en = ddn / _pool_count((i + 1) * ts, 16, wlen[0:16])
        lead, _ = _pool_window(jnp.concatenate([e, en], axis=0), ts, i * ts, True)
        dpv = lead - dd
        dh_ref[...] = jnp.concatenate([dcb, dp * cv, dp * cc, dpv], axis=1).astype(BF16)

    return pl.pallas_call(
        body, name=name, grid=(n,),
        in_specs=[pl.BlockSpec((ts, 1024), lambda i: (i, col)),
                  pl.BlockSpec((16, 1024), lambda i: (jnp.maximum(i * (ts // 16) - 1, 0), col)),
                  pl.BlockSpec((16, 1024), lambda i: (jnp.minimum((i + 1) * (ts // 16), last16), col)),
                  pl.BlockSpec((ts, 512), lambda i: (i, 1)),
                  pl.BlockSpec((16, 512), lambda i: (jnp.minimum((i + 1) * (ts // 16), last16), 1)),
                  pl.BlockSpec((3, 256), lambda i: (0, 0)), pl.BlockSpec((256, 256), lambda i: (0, 0)),
                  pl.BlockSpec((1, 256), lambda i: (0, 0)), pl.BlockSpec(memory_space=pl.ANY)],
        out_specs=[pl.BlockSpec((ts, 1024), lambda i: (i, col)), pl.BlockSpec((3, 256), lambda i: (0, 0)),
                   pl.BlockSpec((256, 256), lambda i: (0, 0)), pl.BlockSpec((1, 256), lambda i: (0, 0))],
        out_shape=[jax.ShapeDtypeStruct(dh.shape, BF16), jax.ShapeDtypeStruct((3, 256), F32),
                   jax.ShapeDtypeStruct((256, 256), F32), jax.ShapeDtypeStruct((1, 256), F32)],
        input_output_aliases={8: 0},
        compiler_params=_params(("arbitrary",)),
    )(h, h, h, dcat, dcat, wc, wblk, pscale, dh)


def _lower_bound(lb_ref, layer):
    b0, b1 = lb_ref[0:1, :], lb_ref[1:2, :]
    m = jnp.maximum(b0, b1)
    e0, e1 = jnp.exp(b0 - m), jnp.exp(b1 - m)
    p0, p1 = e0 / (e0 + e1), e1 / (e0 + e1)
    lb = (p0 - p0) if layer == 0 else ((p0 + p1) - p0)
    return lb, p0, p1


def _cumsum_rows(x, reverse=False):
    row = lax.broadcasted_iota(jnp.int32, x.shape, 0)
    for sh in (1, 2, 4, 8):
        if reverse:
            x = x + jnp.where(row < SUB - sh, pltpu.roll(x, SUB - sh, 0), 0.0)
        else:
            x = x + jnp.where(row >= sh, pltpu.roll(x, sh, 0), 0.0)
    return x


def _gates(fz, lb):
    sig = _sigmoid(fz)
    f = lb + (1.0 - lb) * sig
    g = jnp.log(jnp.maximum(f, F_FLOOR))
    k = (1.0 - lb) * (1.0 - sig)
    return sig, f, g, k


def _head(h):
    return slice(h * HG_D, (h + 1) * HG_D)


def _hgrn_fwd(hh, lbp, gnorm, layer, name):
    L = hh.shape[0]
    ts = SEQ_TILE
    n = L // ts
    nsub = ts // SUB

    def body(q_ref, f_ref, i_ref, g_ref, lb_ref, gn_ref, y_ref, o_ref, s_ref, a_ref, St):
        @pl.when(pl.program_id(0) == 0)
        def _():
            St[...] = jnp.zeros_like(St)

        lb, _, _ = _lower_bound(lb_ref, layer)
        gn = jnp.tile(gn_ref[...], (1, HG_HEADS))
        r16 = lax.broadcasted_iota(jnp.int32, (SUB, SUB), 0)
        c16 = lax.broadcasted_iota(jnp.int32, (SUB, SUB), 1)

        def block(j, carry):
            rows = pl.ds(pl.multiple_of(j * SUB, SUB), SUB)
            q = q_ref[rows, :] * Q_SCALE
            iv = i_ref[rows, :]
            gz = g_ref[rows, :]
            _, _, g, k = _gates(f_ref[rows, :], lb)
            G = _cumsum_rows(g)
            Gl = G[SUB - 1:SUB, :]
            qt = (q * jnp.exp(G)).astype(BF16)
            kd = (k * jnp.exp(Gl - G)).astype(BF16)
            eGl = jnp.exp(Gl)
            ib = iv.astype(BF16)
            A = [jnp.zeros((SUB, SUB), F32) for _ in range(HG_HEADS)]
            for s in range(SUB):
                P = q * jnp.exp(jnp.minimum(G - G[s:s + 1, :], 0.0)) * k[s:s + 1, :]
                for h in range(HG_HEADS):
                    A[h] = jnp.where(c16 == s, jnp.sum(P[:, _head(h)], axis=-1, keepdims=True), A[h])
            outs, ons, amats = [], [], []
            for h in range(HG_HEADS):
                sl = _head(h)
                Sb = St[h].astype(BF16)
                s_ref[j, sl, :] = Sb
                Am = jnp.where(r16 >= c16, A[h], 0.0)
                amats.append(Am)
                o = _dnt(qt[:, sl], Sb) + _dnn(Am.astype(BF16), ib[:, sl])
                St[h] = eGl[:, sl] * St[h] + _dtn(ib[:, sl], kd[:, sl])
                outs.append(o)
                ons.append(o * lax.rsqrt(jnp.mean(o * o, axis=-1, keepdims=True) + RMS_EPS))
            a_ref[rows, :] = jnp.concatenate(amats, axis=1)
            o_ref[rows, :] = jnp.concatenate(outs, axis=1)
            y = jnp.concatenate(ons, axis=1) * gn * (gz * _sigmoid(gz))
            y_ref[rows, :] = y.astype(BF16)
            return carry

        lax.fori_loop(0, nsub, block, 0, unroll=2)

    col = lambda c: pl.BlockSpec((ts, HG_W), lambda i: (i, c))
    return pl.pallas_call(
        body, name=name, grid=(n,),
        in_specs=[col(0), col(1), col(2), col(3), pl.BlockSpec((2, HG_W), lambda i: (0, 0)),
                  pl.BlockSpec((1, HG_D), lambda i: (0, 0))],
        out_specs=[pl.BlockSpec((ts, HG_W), lambda i: (i, 0)), pl.BlockSpec((ts, HG_W), lambda i: (i, 0)),
                   pl.BlockSpec((nsub, HG_W, HG_D), lambda i: (i, 0, 0)),
                   pl.BlockSpec((ts, HG_HEADS * SUB), lambda i: (i, 0))],
        out_shape=[jax.ShapeDtypeStruct((L, HG_W), BF16), jax.ShapeDtypeStruct((L, HG_W), F32),
                   jax.ShapeDtypeStruct((L // SUB, HG_W, HG_D), BF16),
                   jax.ShapeDtypeStruct((L, HG_HEADS * SUB), F32)],
        scratch_shapes=[pltpu.VMEM((HG_HEADS, HG_D, HG_D), F32)],
        compiler_params=_params(("arbitrary",)),
    )(hh, hh, hh, hh, lbp, gnorm)


def _hgrn_bwd(hh, o_raw, states, amat, dcat, lbp, gnorm, layer, name):
    L = hh.shape[0]
    ts = SEQ_TILE
    n = L // ts
    nsub = ts // SUB

    def body(q_ref, f_ref, i_ref, g_ref, o_ref, s_ref, a_ref, dy_ref, lb_ref, gn_ref,
             dh_ref, dlb_ref, dgn_ref, dSt, dlb_acc, S_next):
        step = pl.program_id(0)

        @pl.when(step == 0)
        def _():
            dSt[...] = jnp.zeros_like(dSt)
            S_next[...] = jnp.zeros_like(S_next)
            dlb_acc[...] = jnp.zeros_like(dlb_acc)
            dgn_ref[...] = jnp.zeros_like(dgn_ref)

        lb, p0, p1 = _lower_bound(lb_ref, layer)
        gnh = gn_ref[...]
        gn = jnp.tile(gnh, (1, HG_HEADS))
        r16 = lax.broadcasted_iota(jnp.int32, (SUB, SUB), 0)
        c16 = lax.broadcasted_iota(jnp.int32, (SUB, SUB), 1)

        def block(jj, carry):
            j = nsub - 1 - jj
            rows = pl.ds(pl.multiple_of(j * SUB, SUB), SUB)
            q = q_ref[rows, :] * Q_SCALE
            iv = i_ref[rows, :]
            gz = g_ref[rows, :]
            o = o_ref[rows, :]
            dy = dy_ref[rows, :].astype(F32)
            sig, f, g, k = _gates(f_ref[rows, :], lb)
            G = _cumsum_rows(g)
            Gl = G[SUB - 1:SUB, :]
            eG = jnp.exp(G)
            edl = jnp.exp(Gl - G)
            eGl = jnp.exp(Gl)
            qt = (q * eG).astype(BF16)
            kd = (k * edl).astype(BF16)
            ib = iv.astype(BF16)
            sgz = _sigmoid(gz)
            sil = gz * sgz
            dyn = dy * sil
            on_parts, do_parts = [], []
            dgn = jnp.zeros((1, HG_D), F32)
            for h in range(HG_HEADS):
                sl = _head(h)
                oh = o[:, sl]
                rs = lax.rsqrt(jnp.mean(oh * oh, axis=-1, keepdims=True) + RMS_EPS)
                on = oh * rs
                dgn = dgn + jnp.sum(dyn[:, sl] * on, axis=0, keepdims=True)
                don = dyn[:, sl] * gnh
                do_parts.append(rs * (don - on * jnp.mean(don * on, axis=-1, keepdims=True)))
                on_parts.append(on)
            dgn_ref[...] += dgn
            on_all = jnp.concatenate(on_parts, axis=1)
            dgz = dy * on_all * gn * (sgz * (1.0 + gz * (1.0 - sgz)))
            do = jnp.concatenate(do_parts, axis=1)
            dob = do.astype(BF16)
            amat = a_ref[rows, :]
            dq_p, dk_p, di_p, tail_p = [], [], [], []
            for h in range(HG_HEADS):
                sl = _head(h)
                qh, kh, Gh = q[:, sl], k[:, sl], G[:, sl]
                Ap = jnp.where(r16 >= c16, _dnt(dob[:, sl], ib[:, sl]), 0.0)
                ApT = jnp.where(r16 <= c16, _dnt(ib[:, sl], dob[:, sl]), 0.0)
                dqh = jnp.zeros((SUB, HG_D), F32)
                dkh = jnp.zeros((SUB, HG_D), F32)
                for s in range(SUB):
                    dGs = Gh - Gh[s:s + 1, :]
                    e = jnp.exp(jnp.minimum(dGs, -dGs))
                    dqh = dqh + Ap[:, s:s + 1] * (e * kh[s:s + 1, :])
                    dkh = dkh + ApT[:, s:s + 1] * (e * qh[s:s + 1, :])
                Sb = s_ref[j, sl, :]
                dSb = dSt[h].astype(BF16)
                Am = amat[:, h * SUB:(h + 1) * SUB].astype(BF16)
                dq_p.append(dqh + eG[:, sl] * _dnn(dob[:, sl], Sb))
                dk_p.append(dkh + edl[:, sl] * _dnn(ib[:, sl], dSb))
                di_p.append(_dtn(Am, dob[:, sl]) + _dnt(kd[:, sl], dSb))
                tail_p.append(jnp.sum(dSt[h] * S_next[h].astype(F32), axis=0, keepdims=True))
                S_next[h] = Sb
                dSt[h] = eGl[:, sl] * dSt[h] + _dtn(dob[:, sl], qt[:, sl])
            dq = jnp.concatenate(dq_p, axis=1)
            dk = jnp.concatenate(dk_p, axis=1)
            di = jnp.concatenate(di_p, axis=1)
            dg = _cumsum_rows(q * dq - k * dk, reverse=True) + jnp.concatenate(tail_p, axis=1)
            df = jnp.where(f > F_FLOOR, dg / f, 0.0)
            dfk = df - dk
            dfz = (1.0 - lb) * dfk * sig * (1.0 - sig)
            dlb_acc[...] += jnp.sum(dfk * (1.0 - sig), axis=0, keepdims=True)
            dh_ref[rows, :] = jnp.concatenate([dq * Q_SCALE, dfz, di, dgz], axis=1).astype(BF16)
            return carry

        lax.fori_loop(0, nsub, block, 0)

        @pl.when(step == n - 1)
        def _():
            if layer == 0:
                dlb_ref[...] = jnp.zeros_like(dlb_ref)
            else:
                dz1 = p0 * p1 * dlb_acc[...]
                dlb_ref[...] = jnp.concatenate([-dz1, dz1], axis=0)

    rev = lambda i: n - 1 - i
    col = lambda c: pl.BlockSpec((ts, HG_W), lambda i: (rev(i), c))
    return pl.pallas_call(
        body, name=name, grid=(n,),
        in_specs=[col(0), col(1), col(2), col(3), col(0),
                  pl.BlockSpec((nsub, HG_W, HG_D), lambda i: (rev(i), 0, 0)),
                  pl.BlockSpec((ts, HG_HEADS * SUB), lambda i: (rev(i), 0)), col(0),
                  pl.BlockSpec((2, HG_W), lambda i: (0, 0)), pl.BlockSpec((1, HG_D), lambda i: (0, 0))],
        out_specs=[pl.BlockSpec((ts, 4 * HG_W), lambda i: (rev(i), 0)),
                   pl.BlockSpec((2, HG_W), lambda i: (0, 0)), pl.BlockSpec((1, HG_D), lambda i: (0, 0))],
        out_shape=[jax.ShapeDtypeStruct((L, 4 * HG_W + 1024), BF16), jax.ShapeDtypeStruct((2, HG_W), F32),
                   jax.ShapeDtypeStruct((1, HG_D), F32)],
        scratch_shapes=[pltpu.VMEM((HG_HEADS, HG_D, HG_D), F32), pltpu.VMEM((1, HG_W), F32),
                        pltpu.VMEM((HG_HEADS, HG_D, HG_D), BF16)],
        compiler_params=_params(("arbitrary",)),
    )(hh, hh, hh, hh, o_raw, states, amat, dcat, lbp, gnorm)


def _adamw_body(gp_ref, w_ref, m_ref, v_ref, g_ref, d_ref, mo_ref, vo_ref):
    c1 = 1.0 - ADAM_B1 ** ADAM_STEP
    c2 = 1.0 - ADAM_B2 ** ADAM_STEP
    g = gp_ref[0].astype(F32)
    for k in range(1, N_DEV):
        g = g + gp_ref[k].astype(F32)
    mn = ADAM_B1 * m_ref[...] + (1.0 - ADAM_B1) * g
    vn = ADAM_B2 * v_ref[...] + (1.0 - ADAM_B2) * (g * g)
    m_hat = mn / c1
    v_hat = vn / c2
    g_ref[...] = g
    d_ref[...] = -ADAM_LR * (m_hat / (jnp.sqrt(v_hat) + ADAM_EPS) + ADAM_WD * w_ref[...])
    mo_ref[...] = mn
    vo_ref[...] = vn


def _adamw_layers(gparts, w, m, v, name):
    depth, R, C = w.shape
    tr = _tile(R, 256, 16)
    nr = R // tr

    def body(*refs):
        layer = pl.program_id(0)
        for d in range(depth):
            @pl.when(layer == d)
            def _(d=d):
                _adamw_body(refs[d], *refs[depth:])

    def parts_spec(d):
        return pl.BlockSpec((N_DEV, tr, C),
                            lambda l, i: (0, jnp.where(l == d, i, jnp.where(l < d, 0, nr - 1)), 0))

    blk = pl.BlockSpec((None, tr, C), lambda l, i: (l, i, 0))
    shp = jax.ShapeDtypeStruct((depth, R, C), F32)
    return pl.pallas_call(
        body, name=name, grid=(depth, nr),
        in_specs=[parts_spec(d) for d in range(depth)] + [blk, blk, blk],
        out_specs=[blk, blk, blk, blk], out_shape=[shp, shp, shp, shp],
        compiler_params=_params(("arbitrary", "arbitrary")),
    )(*gparts, w, m, v)


def _adamw(gparts, w, m, v, name):
    R = w.shape[0]
    tr = _tile(R, 1024, 16) if R % 16 == 0 else R

    def body(*refs):
        _adamw_body(*refs)

    row = pl.BlockSpec((tr, LANES), lambda i: (i, 0))
    shp = jax.ShapeDtypeStruct((R, LANES), F32)
    return pl.pallas_call(
        body, name=name, grid=(R // tr,),
        in_specs=[pl.BlockSpec((N_DEV, tr, LANES), lambda i: (0, i, 0)), row, row, row],
        out_specs=[row, row, row, row], out_shape=[shp, shp, shp, shp],
        compiler_params=_params(("parallel",)),
    )(gparts, w, m, v)


def _flip(coord, bit):
    return 1 - coord if bit else coord


def _gather_many(blocks, name):
    n = len(blocks)

    def body(*refs):
        x_refs, out_refs = refs[:n], refs[n:2 * n]
        send_sems, recv_sems, local_sems = refs[2 * n:]
        x, y, c = lax.axis_index("x"), lax.axis_index("y"), lax.axis_index("c")
        me, sibling = (x, y, c), (x, y, 1 - c)
        chips = [(1 - x, y), (x, 1 - y), (1 - x, 1 - y)]

        def slot(a, px, py, pc):
            return out_refs[a].at[4 * px + 2 * py + pc]

        def copy(a, k, blk, to, src=None):
            return pltpu.make_async_remote_copy(
                src_ref=slot(a, *blk) if src is None else src, dst_ref=slot(a, *blk),
                send_sem=send_sems.at[7 * a + k], recv_sem=recv_sems.at[7 * a + k],
                device_id=to, device_id_type=pl.DeviceIdType.MESH)

        mine = [pltpu.make_async_copy(x_refs[a], slot(a, *me), local_sems.at[a]) for a in range(n)]
        for cp in mine:
            cp.start()
        first = [copy(a, 0, me, sibling, src=x_refs[a]) for a in range(n)]
        for j, chip in enumerate(chips):
            first += [copy(a, 1 + j, me, (*chip, c), src=x_refs[a]) for a in range(n)]
        for cp in first:
            cp.start()
        passed = []
        for j, chip in enumerate(chips):
            for a in range(n):
                copy(a, 1 + j, (*chip, c), me).wait_recv()
                fwd = copy(a, 4 + j, (*chip, c), sibling)
                fwd.start()
                passed.append(fwd)
        for a in range(n):
            copy(a, 0, sibling, me).wait_recv()
        for j, chip in enumerate(chips):
            for a in range(n):
                copy(a, 4 + j, (*chip, 1 - c), me).wait_recv()
        for cp in first + passed:
            cp.wait_send()
        for cp in mine:
            cp.wait()

    hbm = pl.BlockSpec(memory_space=pl.ANY)
    return pl.pallas_call(
        body, name=name,
        out_shape=[jax.ShapeDtypeStruct((N_DEV,) + b.shape, b.dtype) for b in blocks],
        in_specs=[hbm] * n, out_specs=[hbm] * n,
        scratch_shapes=[pltpu.SemaphoreType.DMA((7 * n,)), pltpu.SemaphoreType.DMA((7 * n,)),
                        pltpu.SemaphoreType.DMA((n,))],
    )(*blocks)


def _split_start(blocks, chunked, name):
    n = len(blocks)
    lands = [lax.empty(b.shape if chunked else (N_DEV,) + b.shape, b.dtype) for b in blocks]

    def body(*refs):
        x_refs, land_refs = refs[:n], refs[n:2 * n]
        send_sems, recv_sems, token = refs[2 * n], refs[2 * n + 1], refs[-1]
        x, y, c = lax.axis_index("x"), lax.axis_index("y"), lax.axis_index("c")
        me = 4 * x + 2 * y + c
        for a in range(n):
            for k in range(1, N_DEV):
                px, py, pc = _flip(x, k & 4), _flip(y, k & 2), _flip(c, k & 1)
                pltpu.make_async_remote_copy(
                    src_ref=x_refs[a].at[4 * px + 2 * py + pc] if chunked else x_refs[a],
                    dst_ref=land_refs[a].at[me],
                    send_sem=send_sems.at[7 * a + k - 1], recv_sem=recv_sems.at[7 * a + k - 1],
                    device_id=(px, py, pc), device_id_type=pl.DeviceIdType.MESH).start()
        token[...] = jnp.zeros_like(token)

    hbm = pl.BlockSpec(memory_space=pltpu.HBM)
    sem = pl.BlockSpec(memory_space=pltpu.SEMAPHORE)
    outs = pl.pallas_call(
        body, name=name,
        out_shape=(pltpu.SemaphoreType.DMA((7 * n,)), pltpu.SemaphoreType.DMA((7 * n,)),
                   *[pltpu.HBM(b.shape, b.dtype) for b in blocks], *[pltpu.HBM(l.shape, l.dtype) for l in lands],
                   jax.ShapeDtypeStruct((8, LANES), F32)),
        in_specs=[hbm] * (2 * n),
        out_specs=(sem, sem, *[hbm] * (2 * n), pl.BlockSpec(memory_space=pltpu.VMEM)),
        input_output_aliases={i: 2 + i for i in range(2 * n)},
        compiler_params=pltpu.CompilerParams(has_side_effects=pltpu.SideEffectType.DATAFLOW_SIDE_EFFECTING),
    )(*[pltpu.with_memory_space_constraint(b, pltpu.HBM) for b in blocks],
      *[pltpu.with_memory_space_constraint(l, pltpu.HBM) for l in lands])
    return outs[0], outs[1], list(outs[2:2 + n]), list(outs[2 + n:2 + 2 * n]), outs[-1]


def _split_wait(started, chunked, after, name):
    send_sems, recv_sems, blocks, lands, _ = started
    n = len(blocks)

    def body(*refs):
        x_refs, land_refs = refs[:n], refs[n:2 * n]
        send_sems, recv_sems = refs[2 * n], refs[2 * n + 1]
        x, y, c = lax.axis_index("x"), lax.axis_index("y"), lax.axis_index("c")
        for a in range(n):
            for k in range(1, N_DEV):
                px, py, pc = _flip(x, k & 4), _flip(y, k & 2), _flip(c, k & 1)
                copy = pltpu.make_async_remote_copy(
                    src_ref=x_refs[a].at[4 * px + 2 * py + pc] if chunked else x_refs[a],
                    dst_ref=land_refs[a].at[4 * px + 2 * py + pc],
                    send_sem=send_sems.at[7 * a + k - 1], recv_sem=recv_sems.at[7 * a + k - 1],
                    device_id=(px, py, pc), device_id_type=pl.DeviceIdType.MESH)
                copy.wait_send()
                copy.wait_recv()

    hbm = pl.BlockSpec(memory_space=pltpu.HBM)
    sem = pl.BlockSpec(memory_space=pltpu.SEMAPHORE)
    outs = pl.pallas_call(
        body, name=name,
        out_shape=(*[pltpu.HBM(b.shape, b.dtype) for b in blocks], *[pltpu.HBM(l.shape, l.dtype) for l in lands]),
        in_specs=[hbm] * (2 * n) + [sem, sem, pl.BlockSpec(memory_space=pl.ANY)],
        out_specs=[hbm] * (2 * n),
        input_output_aliases={i: i for i in range(2 * n)},
        compiler_params=pltpu.CompilerParams(has_side_effects=pltpu.SideEffectType.DATAFLOW_SIDE_EFFECTING),
    )(*blocks, *lands, send_sems, recv_sems, after)
    me = 4 * lax.axis_index("x") + 2 * lax.axis_index("y") + lax.axis_index("c")
    own = [lax.dynamic_index_in_dim(b, me, 0, keepdims=False) if chunked else b for b in outs[:n]]
    return [lax.dynamic_update_index_in_dim(z, o, me, 0) for z, o in zip(outs[n:], own)]


def _exchange_grads(layer_chunks, small_chunks, rep_block, name):
    flows, inputs = [], []
    for p, per_layer in enumerate(layer_chunks):
        for l, arr in enumerate(per_layer):
            flows.append(("param", p, l))
            inputs.append(arr)
    flows += [("small",), ("rep",)]
    inputs += [small_chunks, rep_block]
    n_par = len(layer_chunks)
    n_in, n_out, nf = len(inputs), n_par + 2, len(flows)

    def body(*refs):
        in_refs, out_refs = refs[:n_in], refs[n_in:n_in + n_out]
        send_sems, recv_sems, local_sems = refs[n_in + n_out:]
        x, y, c = lax.axis_index("x"), lax.axis_index("y"), lax.axis_index("c")
        me = 4 * x + 2 * y + c

        def src(f, dev):
            return in_refs[f] if flows[f][0] == "rep" else in_refs[f].at[dev]

        def dst(f, dev):
            if flows[f][0] == "param":
                _, p, l = flows[f]
                return out_refs[p].at[dev, l]
            return out_refs[n_par + (0 if flows[f][0] == "small" else 1)].at[dev]

        mine = [pltpu.make_async_copy(src(f, me), dst(f, me), local_sems.at[f]) for f in range(nf)]
        for cp in mine:
            cp.start()
        copies = []
        for k in range(1, N_DEV):
            px, py, pc = _flip(x, k & 4), _flip(y, k & 2), _flip(c, k & 1)
            peer = 4 * px + 2 * py + pc
            for f in range(nf):
                sems = dict(send_sem=send_sems.at[7 * f + k - 1], recv_sem=recv_sems.at[7 * f + k - 1],
                            device_id=(px, py, pc), device_id_type=pl.DeviceIdType.MESH)
                send = pltpu.make_async_remote_copy(src_ref=src(f, peer), dst_ref=dst(f, me), **sems)
                recv = pltpu.make_async_remote_copy(src_ref=src(f, peer), dst_ref=dst(f, peer), **sems)
                send.start()
                copies.append((send, recv))
        for send, recv in copies:
            recv.wait_recv()
        for send, recv in copies:
            send.wait_send()
        for cp in mine:
            cp.wait()

    out_shape = [jax.ShapeDtypeStruct((N_DEV, len(pl_)) + pl_[0].shape[1:], pl_[0].dtype) for pl_ in layer_chunks]
    out_shape += [jax.ShapeDtypeStruct(small_chunks.shape, small_chunks.dtype),
                  jax.ShapeDtypeStruct((N_DEV,) + rep_block.shape, rep_block.dtype)]
    hbm = pl.BlockSpec(memory_space=pl.ANY)
    return pl.pallas_call(
        body, name=name, out_shape=out_shape,
        in_specs=[hbm] * n_in, out_specs=[hbm] * n_out,
        scratch_shapes=[pltpu.SemaphoreType.DMA((7 * nf,)), pltpu.SemaphoreType.DMA((7 * nf,)),
                        pltpu.SemaphoreType.DMA((nf,))],
    )(*inputs)


def _pack_rows(size):
    return -(-size // (8 * LANES)) * 8


def _pack(arrs, dtype):
    parts, offs, r = [], [], 0
    for a in arrs:
        flat = a.astype(dtype).reshape(-1)
        nrow = _pack_rows(flat.shape[0])
        flat = jnp.pad(flat, (0, nrow * LANES - flat.shape[0]))
        parts.append(flat.reshape(nrow, LANES))
        offs.append((r, nrow))
        r += nrow
    return jnp.concatenate(parts, axis=0), offs


def _unpack(buf, offs, shapes, lead=()):
    outs = []
    for (r, nrow), shp in zip(offs, shapes):
        size = 1
        for s in shp:
            size *= s
        flat = buf[..., r:r + nrow, :].reshape(lead + (nrow * LANES,))
        outs.append(flat[..., :size].reshape(lead + tuple(shp)))
    return outs


def _cols_from_shards(g, axis):
    return jnp.concatenate([g[j] for j in range(N_DEV)], axis=axis)


BIG = ("w_in", "w_o", "w_up", "w_down")
SMALL_SHARDED = ("meta_tokens", "w_conv", "w_ffn_conv")
REPLICATED = ("hg_lower_bounds", "w_pool", "pool_scale", "hg_norm_g", "ln1_g", "ln1_b", "b_ffn_conv", "ln2_g", "ln2_b")
WEIGHTS = ("meta_tokens", "hg_lower_bounds", "w_in", "w_conv", "w_pool", "pool_scale", "hg_norm_g", "w_o",
           "ln1_g", "ln1_b", "w_up", "w_ffn_conv", "b_ffn_conv", "w_down", "ln2_g", "ln2_b")


def _pool_blockdiag(w_pool_l):
    z = jnp.zeros((POOL_GROUP, POOL_GROUP), w_pool_l.dtype)
    rows = [jnp.concatenate([w_pool_l[g] if h == g else z for h in range(4)], axis=1) for g in range(4)]
    return jnp.concatenate(rows, axis=0)


def _in_weights(g_in):
    w_in = jnp.transpose(g_in, (1, 0, 2)).reshape(D_MODEL, -1)
    return dict(w_hc=jnp.concatenate([w_in[:, 768:2816], w_in[:, 0:768], w_in[:, 2816:3072]], axis=1))


def _rest_weights(g_o, g_up, g_down):
    w_o = g_o.reshape(-1, D_MODEL)
    return dict(w_o=jnp.concatenate([w_o[256:768], w_o[0:256], w_o[768:1024]], axis=0),
                w_up=jnp.transpose(g_up, (1, 0, 2)).reshape(D_MODEL, -1), w_down=g_down.reshape(-1, D_MODEL))


def kernel(x, meta_tokens, hg_lower_bounds, w_in, w_conv, w_pool, pool_scale, hg_norm_g, w_o, ln1_g, ln1_b, w_up, w_ffn_conv, b_ffn_conv, w_down, ln2_g, ln2_b, loss_target, m_meta_tokens, m_hg_lower_bounds, m_w_in, m_w_conv, m_w_pool, m_pool_scale, m_hg_norm_g, m_w_o, m_ln1_g, m_ln1_b, m_w_up, m_w_ffn_conv, m_b_ffn_conv, m_w_down, m_ln2_g, m_ln2_b, v_meta_tokens, v_hg_lower_bounds, v_w_in, v_w_conv, v_w_pool, v_pool_scale, v_hg_norm_g, v_w_o, v_ln1_g, v_ln1_b, v_w_up, v_w_ffn_conv, v_b_ffn_conv, v_w_down, v_ln2_g, v_ln2_b):
    W = dict(meta_tokens=meta_tokens, hg_lower_bounds=hg_lower_bounds, w_in=w_in, w_conv=w_conv, w_pool=w_pool,
             pool_scale=pool_scale, hg_norm_g=hg_norm_g, w_o=w_o, ln1_g=ln1_g, ln1_b=ln1_b, w_up=w_up,
             w_ffn_conv=w_ffn_conv, b_ffn_conv=b_ffn_conv, w_down=w_down, ln2_g=ln2_g, ln2_b=ln2_b)
    M = dict(meta_tokens=m_meta_tokens, hg_lower_bounds=m_hg_lower_bounds, w_in=m_w_in, w_conv=m_w_conv,
             w_pool=m_w_pool, pool_scale=m_pool_scale, hg_norm_g=m_hg_norm_g, w_o=m_w_o, ln1_g=m_ln1_g,
             ln1_b=m_ln1_b, w_up=m_w_up, w_ffn_conv=m_w_ffn_conv, b_ffn_conv=m_b_ffn_conv, w_down=m_w_down,
             ln2_g=m_ln2_g, ln2_b=m_ln2_b)
    V = dict(meta_tokens=v_meta_tokens, hg_lower_bounds=v_hg_lower_bounds, w_in=v_w_in, w_conv=v_w_conv,
             w_pool=v_w_pool, pool_scale=v_pool_scale, hg_norm_g=v_hg_norm_g, w_o=v_w_o, ln1_g=v_ln1_g,
             ln1_b=v_ln1_b, w_up=v_w_up, w_ffn_conv=v_w_ffn_conv, b_ffn_conv=v_b_ffn_conv, w_down=v_w_down,
             ln2_g=v_ln2_g, ln2_b=v_ln2_b)
    assert x.shape[0] == 1 and x.shape[2] == D_MODEL and w_in.shape[0] == DEPTH
    seq = x.shape[1]
    L = -(-(seq + N_META) // ROW_ALIGN) * ROW_ALIGN

    small_pack, small_offs = _pack([W[n] for n in SMALL_SHARDED], F32)
    shards = {n: [W[n][l].astype(BF16) for l in range(DEPTH)] for n in BIG}
    g_in0, small_all = _gather_many([shards["w_in"][0], small_pack], "gather_weights")
    full = {}
    for n, a in zip(SMALL_SHARDED, _unpack(small_all, small_offs, [W[n].shape for n in SMALL_SHARDED], (N_DEV,))):
        full[n] = _cols_from_shards(a, 1)
    order = (small_all[0, 0, 0] * 0.0).astype(BF16)
    rest0_started = _split_start([shards[n][0] + order for n in BIG[1:]], False, "gather_rest0_start")
    order = rest0_started[4][0, 0].astype(BF16)
    layer1_started = _split_start([shards[n][1] + order for n in BIG], False, "gather_layer1_start")
    lb_in = hg_lower_bounds + layer1_started[4][0, 0]

    pad_rows = L - N_META - seq
    xp = jnp.concatenate([full["meta_tokens"], x[0], jnp.zeros((pad_rows, D_MODEL), F32)], axis=0)
    tgt = jnp.concatenate([jnp.zeros((N_META, D_MODEL), F32), loss_target[0], jnp.zeros((pad_rows, D_MODEL), F32)], axis=0)

    saved = []
    h_in, h_in_b = xp, xp.astype(BF16)
    for l in range(DEPTH):
        if l == 0:
            lw = _in_weights(g_in0)
        else:
            g_in, *g_rest = _split_wait(layer1_started, False, h_in_b, "gather_layer1_wait")
            lw = {**_in_weights(g_in), **_rest_weights(*g_rest)}
        wc = full["w_conv"][l].T
        wblk = _pool_blockdiag(w_pool[l]).astype(BF16)
        ps = pool_scale[l][None, :]
        gn = hg_norm_g[l][None, :]
        wf = full["w_ffn_conv"][l].T
        bf = b_ffn_conv[l][None, :]
        h = _matmul(h_in_b, lw["w_hc"], "nn", F32, f"fwd_in_{l}")
        y_hg, o_raw, states, amat = _hgrn_fwd(h, lb_in if l == 0 else hg_lower_bounds, gn, l, f"hgrn_fwd_{l}")
        y_cp = _cp_fwd(h, 2, wc, wblk, ps, f"convpool_fwd_{l}")
        cat = jnp.concatenate([y_hg, y_cp], axis=1)
        if l == 0:
            lw.update(_rest_weights(*_split_wait(rest0_started, False, cat, "gather_rest0_wait")))
        z1, x1, x1_b = _matmul_ln(cat, lw["w_o"], h_in, ln1_g[l][None, :], ln1_b[l][None, :], f"fwd_o_ln1_{l}")
        up = _matmul(x1_b, lw["w_up"], "nn", BF16, f"fwd_up_{l}")
        a, u = _ffn_act_fwd(up, wf, bf, f"ffn_fwd_{l}")
        saved.append(dict(lw=lw, wc=wc, wblk=wblk, ps=ps, gn=gn, wf=wf, bf=bf, x_b=h_in_b, h=h,
                          o_raw=o_raw, states=states, amat=amat, cat=cat, z1=z1, x1_b=x1_b, up=up, u=u, a=a))
        if l < DEPTH - 1:
            saved[l]["z2"], h_in, h_in_b = _matmul_ln(a, lw["w_down"], x1, ln2_g[l][None, :], ln2_b[l][None, :],
                                                     f"fwd_down_ln2_{l}")
        else:
            saved[l]["z2"], dy, loss_part = _matmul_ln(a, lw["w_down"], x1, ln2_g[l][None, :], ln2_b[l][None, :],
                                                       f"fwd_down_ln2_loss_{l}", loss=(tgt, seq))

    loss = lax.psum(loss_part[0, 0], ("x", "y", "c"))

    G = {}
    per_layer = {n: [None] * DEPTH for n in ("w_conv", "w_pool", "pool_scale", "hg_norm_g", "ln1_g", "ln1_b",
                                             "w_ffn_conv", "b_ffn_conv", "ln2_g", "ln2_b")}
    ffn_started, mix_started = [None] * DEPTH, [None] * DEPTH
    order = jnp.zeros((), F32)
    dlb_total = jnp.zeros((DEPTH, HG_W), F32)
    for l in reversed(range(DEPTH)):
        s = saved[l]
        lw = s["lw"]
        dz2_b, dg2, db2 = _ln_bwd(s["z2"], dy, ln2_g[l][None, :] + order, f"ln2_bwd_{l}")
        da = _matmul(dz2_b, lw["w_down"], "nt", BF16, f"bwd_da_{l}")
        d_w_down = _matmul(s["a"], dz2_b, "tn", BF16, f"wgrad_down_{l}")
        dup, dwf, dbf = _ffn_act_bwd(s["up"], s["u"], da, s["wf"], f"ffn_bwd_{l}")
        dx1 = _matmul(dup, lw["w_up"], "nt", BF16, f"bwd_dx1_{l}", res=dz2_b, alpha=ALPHA)
        d_w_up = _matmul(s["x1_b"], dup, "tn", BF16, f"wgrad_up_{l}")
        ffn_started[l] = _split_start([jnp.transpose(d_w_up.reshape(D_MODEL, N_DEV, -1), (1, 0, 2)),
                                       d_w_down.reshape(N_DEV, -1, D_MODEL)], True, f"scatter_ffn{l}_start")
        order = ffn_started[l][4][0, 0]
        dz1_b, dg1, db1 = _ln_bwd(s["z1"], dx1, ln1_g[l][None, :] + order, f"ln1_bwd_{l}")
        dcat = _matmul(dz1_b, lw["w_o"], "nt", BF16, f"bwd_dcat_{l}")
        d_w_o = _matmul(s["cat"], dz1_b, "tn", BF16, f"wgrad_o_{l}")
        dh, dlb, dgn = _hgrn_bwd(s["h"], s["o_raw"], s["states"], s["amat"], dcat, hg_lower_bounds, s["gn"], l,
                                 f"hgrn_bwd_{l}")
        dh, dwc, dwblk, dps = _cp_bwd(s["h"], 2, dcat, dh, s["wc"], s["wblk"], s["ps"], f"convpool_bwd_{l}")
        d_w_hc = _matmul(s["x_b"], dh, "tn", BF16, f"wgrad_in_{l}")
        d_w_in = jnp.concatenate([d_w_hc[:, 2048:2816], d_w_hc[:, 0:2048], d_w_hc[:, 2816:3072]], axis=1)
        mix_chunks = [jnp.transpose(d_w_in.reshape(D_MODEL, N_DEV, -1), (1, 0, 2)),
                      jnp.concatenate([d_w_o[512:768], d_w_o[0:512], d_w_o[768:1024]], axis=0).reshape(N_DEV, -1, D_MODEL)]
        mix_started[l] = _split_start(mix_chunks, True, f"scatter_mix{l}_start")
        order = mix_started[l][4][0, 0]
        dx = _matmul(dh, lw["w_hc"] + order.astype(BF16), "nt", F32, f"bwd_dx_{l}", res=dz1_b, alpha=ALPHA)
        per_layer["w_conv"][l] = dwc.T
        per_layer["w_ffn_conv"][l] = dwf.T
        per_layer["b_ffn_conv"][l] = dbf[0]
        per_layer["w_pool"][l] = jnp.stack([dwblk[g * 64:(g + 1) * 64, g * 64:(g + 1) * 64] for g in range(4)], axis=0)
        per_layer["pool_scale"][l] = dps[0]
        per_layer["hg_norm_g"][l] = dgn[0]
        per_layer["ln1_g"][l], per_layer["ln1_b"][l] = dg1[0], db1[0]
        per_layer["ln2_g"][l], per_layer["ln2_b"][l] = dg2[0], db2[0]
        dlb_total = dlb_total + dlb
        dy = dx
    for n, parts in per_layer.items():
        G[n] = jnp.stack(parts, axis=0)
    G["hg_lower_bounds"] = dlb_total
    grad_x = dy[N_META:N_META + seq][None]

    def shard_major(g, lead):
        g = g.reshape(g.shape[:lead] + (N_DEV, -1) + g.shape[lead + 1:])
        g = jnp.moveaxis(g, lead, 0).reshape(N_DEV, -1)
        nrow = _pack_rows(g.shape[1])
        return jnp.pad(g, ((0, 0), (0, nrow * LANES - g.shape[1]))).reshape(N_DEV, nrow, LANES)

    small_chunks = jnp.concatenate([shard_major(dy[0:N_META], 1), shard_major(G["w_conv"], 1),
                                    shard_major(G["w_ffn_conv"], 1)], axis=1)
    w_small, _ = _pack([W[n] for n in SMALL_SHARDED], F32)
    rep_pack, rep_offs = _pack([G[n] for n in REPLICATED], F32)
    small_recv, rep_all = _exchange_grads([], small_chunks, rep_pack, "exchange_grads")
    parts = {n: [] for n in BIG}
    for l in range(DEPTH):
        up_l, down_l = _split_wait(ffn_started[l], True, rep_all, f"scatter_ffn{l}_wait")
        in_l, o_l = _split_wait(mix_started[l], True, rep_all, f"scatter_mix{l}_wait")
        for n, a in zip(BIG, (in_l, o_l, up_l, down_l)):
            parts[n].append(a)

    res = {k: {} for k in ("grad", "delta", "new_m", "new_v")}
    kinds = ("grad", "delta", "new_m", "new_v")
    for n in BIG:
        for kind, a in zip(kinds, _adamw_layers(parts[n], W[n], M[n], V[n], f"adamw_{n}")):
            res[kind][n] = a
    m_small, _ = _pack([M[n] for n in SMALL_SHARDED], F32)
    v_small, _ = _pack([V[n] for n in SMALL_SHARDED], F32)
    outs_small = _adamw(small_recv, w_small, m_small, v_small, "adamw_small_sharded")
    w_rep, _ = _pack([W[n] for n in REPLICATED], F32)
    m_rep, _ = _pack([M[n] for n in REPLICATED], F32)
    v_rep, _ = _pack([V[n] for n in REPLICATED], F32)
    outs_rep = _adamw(rep_all, w_rep, m_rep, v_rep, "adamw_replicated")
    for kind, b_sm, b_rep in zip(kinds, outs_small, outs_rep):
        for n, a in zip(SMALL_SHARDED, _unpack(b_sm, small_offs, [W[n].shape for n in SMALL_SHARDED])):
            res[kind][n] = a
        for n, a in zip(REPLICATED, _unpack(b_rep, rep_offs, [W[n].shape for n in REPLICATED])):
            res[kind][n] = a

    return (loss, grad_x, *[res["grad"][n] for n in WEIGHTS], *[res["delta"][n] for n in WEIGHTS],
            *[res["new_m"][n] for n in WEIGHTS], *[res["new_v"][n] for n in WEIGHTS])
```

```python
import jax
import jax.numpy as jnp
from jax import lax
from jax.experimental import pallas as pl
from jax.experimental.pallas import tpu as pltpu

F32 = jnp.float32
BF16 = jnp.bfloat16

N_DEV = 8
D_MODEL = 1024
N_META = 16
DEPTH = 2
CONV_W = 256
HG_W = 512
HG_D = 128
HG_HEADS = 4
POOL_W = 256
POOL_GROUP = 64
D_FF = 2816
ALPHA = (2 * DEPTH) ** 0.25
LN_EPS = 1e-5
RMS_EPS = 1e-6
F_FLOOR = 1e-30
Q_SCALE = HG_D ** -0.5
SUB = 16
SEQ_TILE = 192
FFN_TILE = 192
CP_TILE_CAP = 768
ROW_ALIGN = 192
LANES = 128
VMEM_LIMIT = 48 * 1024 * 1024
MATMUL_VMEM_BUDGET = 38 * 1024 * 1024

ADAM_LR = 0.001
ADAM_B1 = 0.9
ADAM_B2 = 0.999
ADAM_EPS = 1e-08
ADAM_WD = 0.01
ADAM_STEP = 10


def _tile(n, cap, mult):
    best = 0
    for t in range(mult, min(n, cap) + 1, mult):
        if n % t == 0:
            best = t
    assert best > 0, (n, cap, mult)
    return best


def _params(sem, vmem=VMEM_LIMIT):
    return pltpu.CompilerParams(dimension_semantics=sem, vmem_limit_bytes=vmem)


def _dnt(a, b):
    return lax.dot_general(a, b, (((1,), (1,)), ((), ())), preferred_element_type=F32)


def _dtn(a, b):
    return lax.dot_general(a, b, (((0,), (0,)), ((), ())), preferred_element_type=F32)


def _dnn(a, b):
    return jnp.dot(a, b, preferred_element_type=F32)


def _sigmoid(x):
    return jax.nn.sigmoid(x)


def _matmul(a, b, mode, out_dtype, name, res=None, alpha=1.0):
    if mode == "tn":
        K, M = a.shape
    else:
        M, K = a.shape
    N = b.shape[0] if mode == "nt" else b.shape[1]
    out_bytes = jnp.dtype(out_dtype).itemsize
    if mode == "tn" and M % 512 == 0 and N % 512 == 0:
        tk, tm, tn = K, 512, 512
        nk, use_acc = 1, False
    else:
        tn = _tile(N, 1536, LANES)
        tk = _tile(K, 1536, 16) if mode == "tn" else _tile(K, 2816, LANES)
        nk = K // tk
        use_acc = nk > 1 and out_dtype != F32
        tm = M
        for cap in (1536, 768, 384):
            tm = _tile(M, cap, 16)
            blocks = 2 * (a.dtype.itemsize * tm * tk + b.dtype.itemsize * tn * tk + out_bytes * tm * tn
                          + (res.dtype.itemsize * tm * tn if res is not None else 0)) + (4 * tm * tn if use_acc else 0)
            if blocks <= MATMUL_VMEM_BUDGET:
                break
    dims = {"nn": ((1,), (0,)), "nt": ((1,), (1,)), "tn": ((0,), (0,))}[mode]

    def body(*refs):
        a_ref, b_ref = refs[0], refs[1]
        r_ref = refs[2] if res is not None else None
        o_ref = refs[3] if res is not None else refs[2]
        acc = refs[-1] if use_acc else o_ref
        k = pl.program_id(2)
        p = lax.dot_general(a_ref[...].astype(BF16), b_ref[...].astype(BF16), (dims, ((), ())),
                            preferred_element_type=F32)

        def finish(r):
            if r_ref is not None:
                r = r + alpha * r_ref[...].astype(F32)
            o_ref[...] = r.astype(out_dtype)

        if nk == 1:
            finish(p)
        else:
            @pl.when(k == 0)
            def _():
                acc[...] = p

            @pl.when((k > 0) & (k < nk - 1))
            def _():
                acc[...] += p

            @pl.when(k == nk - 1)
            def _():
                finish(acc[...] + p)

    if mode == "tn":
        a_spec = pl.BlockSpec((tk, tm), lambda i, j, k: (k, i))
    else:
        a_spec = pl.BlockSpec((tm, tk), lambda i, j, k: (i, k))
    if mode == "nt":
        b_spec = pl.BlockSpec((tn, tk), lambda i, j, k: (j, k))
    else:
        b_spec = pl.BlockSpec((tk, tn), lambda i, j, k: (k, j))
    in_specs = [a_spec, b_spec]
    args = [a, b]
    if res is not None:
        in_specs.append(pl.BlockSpec((tm, tn), lambda i, j, k: (i, j)))
        args.append(res)
    return pl.pallas_call(
        body, name=name,
        grid=(M // tm, N // tn, nk),
        in_specs=in_specs,
        out_specs=pl.BlockSpec((tm, tn), lambda i, j, k: (i, j)),
        out_shape=jax.ShapeDtypeStruct((M, N), out_dtype),
        scratch_shapes=[pltpu.VMEM((tm, tn), F32)] if use_acc else [],
        compiler_params=_params(("parallel", "parallel", "arbitrary")),
    )(*args)


def _matmul_ln(a, w, x, g, b, name, loss=None):
    L, K = a.shape
    D = w.shape[1]
    tr = L
    for cap in (1536, 768, 384):
        tr = _tile(L, cap, 16)
        if 2 * (2 * tr * K + 2 * K * D + 4 * tr * D * (4 if loss else 3) + 2 * tr * D) <= MATMUL_VMEM_BUDGET:
            break

    def body(*refs):
        a_ref, w_ref, x_ref, g_ref, b_ref = refs[:5]
        z = ALPHA * x_ref[...] + _dnn(a_ref[...], w_ref[...])
        mu = jnp.mean(z, axis=-1, keepdims=True)
        zc = z - mu
        var = jnp.mean(zc * zc, axis=-1, keepdims=True)
        y = zc * lax.rsqrt(var + LN_EPS) * g_ref[...] + b_ref[...]
        if loss is None:
            z_ref, y_ref, yb_ref = refs[5:]
            y_ref[...] = y
            yb_ref[...] = y.astype(BF16)
        else:
            t_ref, z_ref, dy_ref, loss_ref = refs[5:]
            i = pl.program_id(0)

            @pl.when(i == 0)
            def _():
                loss_ref[...] = jnp.zeros_like(loss_ref)

            r = i * tr + lax.broadcasted_iota(jnp.int32, (tr, D), 0)
            valid = (r >= N_META) & (r < N_META + loss[1])
            e = jnp.where(valid, y - t_ref[...], 0.0)
            dy_ref[...] = e * (1.0 / D)
            s = jnp.sum(jnp.sum(e * e, axis=-1, keepdims=True), axis=0, keepdims=True)
            loss_ref[...] += (0.5 / D) * s
        z_ref[...] = z.astype(BF16)

    row = pl.BlockSpec((tr, D), lambda i: (i, 0))
    vec = pl.BlockSpec((1, D), lambda i: (0, 0))
    in_specs = [pl.BlockSpec((tr, K), lambda i: (i, 0)), pl.BlockSpec((K, D), lambda i: (0, 0)), row, vec, vec]
    f32_rows = jax.ShapeDtypeStruct((L, D), F32)
    b16_rows = jax.ShapeDtypeStruct((L, D), BF16)
    if loss is None:
        args, out_specs = [a, w, x, g, b], [row, row, row]
        out_shape = [b16_rows, f32_rows, b16_rows]
    else:
        args, in_specs = [a, w, x, g, b, loss[0]], in_specs + [row]
        out_specs = [row, row, pl.BlockSpec((1, 1), lambda i: (0, 0))]
        out_shape = [b16_rows, f32_rows, jax.ShapeDtypeStruct((1, 1), F32)]
    return pl.pallas_call(
        body, name=name, grid=(L // tr,), in_specs=in_specs, out_specs=out_specs, out_shape=out_shape,
        compiler_params=_params(("arbitrary",) if loss else ("parallel",)),
    )(*args)


def _ln_bwd(z, dy, g, name):
    L, D = z.shape
    tr = _tile(L, 768, 16)

    def body(z_ref, dy_ref, g_ref, dzb_ref, dg_ref, db_ref):
        @pl.when(pl.program_id(0) == 0)
        def _():
            dg_ref[...] = jnp.zeros_like(dg_ref)
            db_ref[...] = jnp.zeros_like(db_ref)

        z = z_ref[...].astype(F32)
        mu = jnp.mean(z, axis=-1, keepdims=True)
        zc = z - mu
        var = jnp.mean(zc * zc, axis=-1, keepdims=True)
        rstd = lax.rsqrt(var + LN_EPS)
        xhat = zc * rstd
        dy = dy_ref[...].astype(F32)
        dxh = dy * g_ref[...]
        m1 = jnp.mean(dxh, axis=-1, keepdims=True)
        m2 = jnp.mean(dxh * xhat, axis=-1, keepdims=True)
        dz = rstd * (dxh - m1 - xhat * m2)
        dzb_ref[...] = dz.astype(BF16)
        dg_ref[...] += jnp.sum(dy * xhat, axis=0, keepdims=True)
        db_ref[...] += jnp.sum(dy, axis=0, keepdims=True)

    row = pl.BlockSpec((tr, D), lambda i: (i, 0))
    vec = pl.BlockSpec((1, D), lambda i: (0, 0))
    return pl.pallas_call(
        body, name=name, grid=(L // tr,),
        in_specs=[row, row, vec], out_specs=[row, vec, vec],
        out_shape=[jax.ShapeDtypeStruct((L, D), BF16),
                   jax.ShapeDtypeStruct((1, D), F32), jax.ShapeDtypeStruct((1, D), F32)],
        compiler_params=_params(("arbitrary",)),
    )(z, dy, g)


def _shift_down(x, prev, k):
    T, C = x.shape
    rot = pltpu.roll(jnp.concatenate([prev, x], axis=0).reshape(T // 8 + 1, 8, C), k, 1)
    sub = lax.broadcasted_iota(jnp.int32, (T // 8, 8, C), 1)
    return jnp.where(sub < k, rot[:-1], rot[1:]).reshape(T, C)


def _shift_up(x, nxt, k):
    T, C = x.shape
    rot = pltpu.roll(jnp.concatenate([x, nxt], axis=0).reshape(T // 8 + 1, 8, C), 8 - k, 1)
    sub = lax.broadcasted_iota(jnp.int32, (T // 8, 8, C), 1)
    return jnp.where(sub >= 8 - k, rot[1:], rot[:-1]).reshape(T, C)


def _conv3(x, prev, w, b):
    return w[2:3, :] * x + w[1:2, :] * _shift_down(x, prev, 1) + w[0:1, :] * _shift_down(x, prev, 2) + b


def _ffn_act_fwd(up, w, b, name):
    L, C = up.shape
    F = C // 2
    ts = FFN_TILE
    n = L // ts

    def body(up_ref, pv_ref, w_ref, b_ref, a_ref, u_ref):
        i = pl.program_id(0)
        x = up_ref[...].astype(F32)
        prev = jnp.where(i > 0, pv_ref[...].astype(F32)[8:16], 0.0)
        u = _conv3(x, prev, w_ref[...], b_ref[...])
        u_ref[...] = u.astype(BF16)
        gate = u[:, :F]
        a_ref[...] = (gate * _sigmoid(gate) * u[:, F:]).astype(BF16)

    return pl.pallas_call(
        body, name=name, grid=(n,),
        in_specs=[pl.BlockSpec((ts, C), lambda i: (i, 0)),
                  pl.BlockSpec((16, C), lambda i: (jnp.maximum(i * (ts // 16) - 1, 0), 0)),
                  pl.BlockSpec((3, C), lambda i: (0, 0)), pl.BlockSpec((1, C), lambda i: (0, 0))],
        out_specs=[pl.BlockSpec((ts, F), lambda i: (i, 0)), pl.BlockSpec((ts, C), lambda i: (i, 0))],
        out_shape=[jax.ShapeDtypeStruct((L, F), BF16), jax.ShapeDtypeStruct((L, C), BF16)],
        compiler_params=_params(("parallel",)),
    )(up, up, w, b)


def _ffn_act_bwd(up, u, da, w, name):
    L, C = up.shape
    F = C // 2
    ts = FFN_TILE
    n = L // ts
    last16 = L // 16 - 1

    def du_of(u, da):
        gate, val = u[:, :F], u[:, F:]
        sg = _sigmoid(gate)
        dgate = da * val * (sg * (1.0 + gate * (1.0 - sg)))
        dval = da * (gate * sg)
        return jnp.concatenate([dgate, dval], axis=1)

    def body(up_ref, u_ref, un_ref, da_ref, dan_ref, w_ref, dup_ref, dw_ref, db_ref):
        i = pl.program_id(0)

        @pl.when(i == 0)
        def _():
            dw_ref[...] = jnp.zeros_like(dw_ref)
            db_ref[...] = jnp.zeros_like(db_ref)

        w = w_ref[...]
        x = up_ref[...].astype(F32)
        du = du_of(u_ref[...].astype(F32), da_ref[...].astype(F32))
        dun = jnp.where(i < n - 1, du_of(un_ref[...].astype(F32)[0:8], dan_ref[...].astype(F32)[0:8]), 0.0)
        du1 = _shift_up(du, dun, 1)
        du2 = _shift_up(du, dun, 2)
        dup_ref[...] = (w[2:3, :] * du + w[1:2, :] * du1 + w[0:1, :] * du2).astype(BF16)
        dw_ref[...] += jnp.concatenate([jnp.sum(x * du2, axis=0, keepdims=True),
                                        jnp.sum(x * du1, axis=0, keepdims=True),
                                        jnp.sum(x * du, axis=0, keepdims=True)], axis=0)
        db_ref[...] += jnp.sum(du, axis=0, keepdims=True)

    nxt = lambda i: (jnp.minimum((i + 1) * (ts // 16), last16), 0)
    return pl.pallas_call(
        body, name=name, grid=(n,),
        in_specs=[pl.BlockSpec((ts, C), lambda i: (i, 0)),
                  pl.BlockSpec((ts, C), lambda i: (i, 0)), pl.BlockSpec((16, C), nxt),
                  pl.BlockSpec((ts, F), lambda i: (i, 0)), pl.BlockSpec((16, F), nxt),
                  pl.BlockSpec((3, C), lambda i: (0, 0))],
        out_specs=[pl.BlockSpec((ts, C), lambda i: (i, 0)), pl.BlockSpec((3, C), lambda i: (0, 0)),
                   pl.BlockSpec((1, C), lambda i: (0, 0))],
        out_shape=[jax.ShapeDtypeStruct((L, C), BF16), jax.ShapeDtypeStruct((3, C), F32),
                   jax.ShapeDtypeStruct((1, C), F32)],
        compiler_params=_params(("arbitrary",)),
    )(up, u, u, da, da, w)


def _pool_window(ext, tile_rows, first_row, lead):
    T = ext.shape[0]
    sh = (lambda x, k: pltpu.roll(x, T - k, 0)) if lead else (lambda x, k: pltpu.roll(x, k, 0))
    r2 = ext + sh(ext, 1)
    r4 = r2 + sh(r2, 2)
    r8 = r4 + sh(r4, 4)
    r16 = r8 + sh(r8, 8)
    lo = 0 if lead else 16
    grp = lax.broadcasted_iota(jnp.int32, (tile_rows, POOL_W), 1) // POOL_GROUP
    pick = lambda a, b, c, d: jnp.where(grp == 0, a, jnp.where(grp == 1, b, jnp.where(grp == 2, c, d)))
    win = pick(r2[lo:lo + tile_rows], r4[lo:lo + tile_rows], r8[lo:lo + tile_rows], r16[lo:lo + tile_rows])
    return win, pick(2.0, 4.0, 8.0, 16.0)


def _pool_count(first_row, rows, wlen):
    t1 = (first_row + lax.broadcasted_iota(jnp.int32, (rows, POOL_W), 0) + 1).astype(F32)
    return jnp.minimum(t1, wlen)


def _cp_fwd(h, col, cat, wc, wblk, pscale, name):
    L = h.shape[0]
    ts = _tile(L, CP_TILE_CAP, 16)
    n = L // ts

    def body(h_ref, hp_ref, wc_ref, wb_ref, ps_ref, cat_in_ref, y_ref):
        i = pl.program_id(0)
        h = h_ref[...]
        hp = jnp.where(i > 0, hp_ref[...], 0.0)
        cb, cc, cv, pv = h[:, 0:256], h[:, 256:512], h[:, 512:768], h[:, 768:1024]
        p = cc * cv
        pp = hp[8:16, 256:512] * hp[8:16, 512:768]
        w = wc_ref[...]
        conv = w[2:3, :] * p + w[1:2, :] * _shift_down(p, pp, 1) + w[0:1, :] * _shift_down(p, pp, 2)
        y_conv = cb * conv
        ext = jnp.concatenate([hp[:, 768:1024], pv], axis=0)
        win, wlen = _pool_window(ext, ts, i * ts, False)
        d = win / _pool_count(i * ts, ts, wlen) - pv
        y_pool = _dnn(d.astype(BF16), wb_ref[...]) * ps_ref[...]
        y_ref[...] = jnp.concatenate([y_conv, y_pool], axis=1).astype(BF16)

    return pl.pallas_call(
        body, name=name, grid=(n,),
        in_specs=[pl.BlockSpec((ts, 1024), lambda i: (i, col)),
                  pl.BlockSpec((16, 1024), lambda i: (jnp.maximum(i * (ts // 16) - 1, 0), col)),
                  pl.BlockSpec((3, 256), lambda i: (0, 0)), pl.BlockSpec((256, 256), lambda i: (0, 0)),
                  pl.BlockSpec((1, 256), lambda i: (0, 0)), pl.BlockSpec(memory_space=pl.ANY)],
        out_specs=pl.BlockSpec((ts, 512), lambda i: (i, 1)),
        out_shape=jax.ShapeDtypeStruct(cat.shape, BF16),
        input_output_aliases={5: 0},
        compiler_params=_params(("parallel",)),
    )(h, h, wc, wblk, pscale, cat)


def _cp_bwd(h, col, dcat, dh, wc, wblk, pscale, name):
    L = h.shape[0]
    ts = _tile(L, CP_TILE_CAP, 16)
    n = L // ts
    last16 = L // 16 - 1

    def body(h_ref, hp_ref, hn_ref, dy_ref, dyn_ref, wc_ref, wb_ref, ps_ref, dh_in_ref,
             dh_ref, dwc_ref, dwb_ref, dps_ref):
        i = pl.program_id(0)

        @pl.when(i == 0)
        def _():
            dwc_ref[...] = jnp.zeros_like(dwc_ref)
            dwb_ref[...] = jnp.zeros_like(dwb_ref)
            dps_ref[...] = jnp.zeros_like(dps_ref)

        h = h_ref[...]
        hp = jnp.where(i > 0, hp_ref[...], 0.0)
        hn = hn_ref[...]
        dy = dy_ref[...].astype(F32)
        dyn = jnp.where(i < n - 1, dyn_ref[...].astype(F32), 0.0)
        cb, cc, cv, pv = h[:, 0:256], h[:, 256:512], h[:, 512:768], h[:, 768:1024]
        w = wc_ref[...]
        p = cc * cv
        pp = hp[8:16, 256:512] * hp[8:16, 512:768]
        p1 = _shift_down(p, pp, 1)
        p2 = _shift_down(p, pp, 2)
        conv = w[2:3, :] * p + w[1:2, :] * p1 + w[0:1, :] * p2
        dyc = dy[:, 0:256]
        dcb = dyc * conv
        dconv = dyc * cb
        dconv_n = dyn[0:8, 0:256] * hn[0:8, 0:256]
        dc1 = _shift_up(dconv, dconv_n, 1)
        dc2 = _shift_up(dconv, dconv_n, 2)
        dp = w[2:3, :] * dconv + w[1:2, :] * dc1 + w[0:1, :] * dc2
        dwc_ref[...] += jnp.concatenate([jnp.sum(p * dc2, axis=0, keepdims=True),
                                         jnp.sum(p * dc1, axis=0, keepdims=True),
                                         jnp.sum(p * dconv, axis=0, keepdims=True)], axis=0)
        ps = ps_ref[...]
        wb = wb_ref[...]
        ext = jnp.concatenate([hp[:, 768:1024], pv], axis=0)
        win, wlen = _pool_window(ext, ts, i * ts, False)
        d = win / _pool_count(i * ts, ts, wlen) - pv
        db = d.astype(BF16)
        dyp = dy[:, 256:512]
        dps_ref[...] += jnp.sum(dyp * _dnn(db, wb), axis=0, keepdims=True)
        dypre = (dyp * ps).astype(BF16)
        dwb_ref[...] += _dtn(db, dypre)
        dd = _dnt(dypre, wb)
        ddn = _dnt((dyn[:, 256:512] * ps).astype(BF16), wb)
        e = dd / _pool_count(i * ts, ts, wlen)
        en = ddn / _pool_count((i + 1) * ts, 16, wlen[0:16])
        lead, _ = _pool_window(jnp.concatenate([e, en], axis=0), ts, i * ts, True)
        dpv = lead - dd
        dh_ref[...] = jnp.concatenate([dcb, dp * cv, dp * cc, dpv], axis=1).astype(BF16)

    return pl.pallas_call(
        body, name=name, grid=(n,),
        in_specs=[pl.BlockSpec((ts, 1024), lambda i: (i, col)),
                  pl.BlockSpec((16, 1024), lambda i: (jnp.maximum(i * (ts // 16) - 1, 0), col)),
                  pl.BlockSpec((16, 1024), lambda i: (jnp.minimum((i + 1) * (ts // 16), last16), col)),
                  pl.BlockSpec((ts, 512), lambda i: (i, 1)),
                  pl.BlockSpec((16, 512), lambda i: (jnp.minimum((i + 1) * (ts // 16), last16), 1)),
                  pl.BlockSpec((3, 256), lambda i: (0, 0)), pl.BlockSpec((256, 256), lambda i: (0, 0)),
                  pl.BlockSpec((1, 256), lambda i: (0, 0)), pl.BlockSpec(memory_space=pl.ANY)],
        out_specs=[pl.BlockSpec((ts, 1024), lambda i: (i, col)), pl.BlockSpec((3, 256), lambda i: (0, 0)),
                   pl.BlockSpec((256, 256), lambda i: (0, 0)), pl.BlockSpec((1, 256), lambda i: (0, 0))],
        out_shape=[jax.ShapeDtypeStruct(dh.shape, BF16), jax.ShapeDtypeStruct((3, 256), F32),
                   jax.ShapeDtypeStruct((256, 256), F32), jax.ShapeDtypeStruct((1, 256), F32)],
        input_output_aliases={8: 0},
        compiler_params=_params(("arbitrary",)),
    )(h, h, h, dcat, dcat, wc, wblk, pscale, dh)


def _lower_bound(lb_ref, layer):
    b0, b1 = lb_ref[0:1, :], lb_ref[1:2, :]
    m = jnp.maximum(b0, b1)
    e0, e1 = jnp.exp(b0 - m), jnp.exp(b1 - m)
    p0, p1 = e0 / (e0 + e1), e1 / (e0 + e1)
    lb = (p0 - p0) if layer == 0 else ((p0 + p1) - p0)
    return lb, p0, p1


def _cumsum_rows(x, reverse=False):
    row = lax.broadcasted_iota(jnp.int32, x.shape, 0)
    for sh in (1, 2, 4, 8):
        if reverse:
            x = x + jnp.where(row < SUB - sh, pltpu.roll(x, SUB - sh, 0), 0.0)
        else:
            x = x + jnp.where(row >= sh, pltpu.roll(x, sh, 0), 0.0)
    return x


def _gates(fz, lb):
    sig = _sigmoid(fz)
    f = lb + (1.0 - lb) * sig
    g = jnp.log(jnp.maximum(f, F_FLOOR))
    k = (1.0 - lb) * (1.0 - sig)
    return sig, f, g, k


def _head(h):
    return slice(h * HG_D, (h + 1) * HG_D)


def _hgrn_fwd(hh, lbp, gnorm, layer, name):
    L = hh.shape[0]
    ts = SEQ_TILE
    n = L // ts
    nsub = ts // SUB

    def body(q_ref, f_ref, i_ref, g_ref, lb_ref, gn_ref, y_ref, o_ref, s_ref, a_ref, St):
        @pl.when(pl.program_id(0) == 0)
        def _():
            St[...] = jnp.zeros_like(St)

        lb, _, _ = _lower_bound(lb_ref, layer)
        gn = jnp.tile(gn_ref[...], (1, HG_HEADS))
        r16 = lax.broadcasted_iota(jnp.int32, (SUB, SUB), 0)
        c16 = lax.broadcasted_iota(jnp.int32, (SUB, SUB), 1)

        def block(j, carry):
            rows = pl.ds(pl.multiple_of(j * SUB, SUB), SUB)
            q = q_ref[rows, :] * Q_SCALE
            iv = i_ref[rows, :]
            gz = g_ref[rows, :]
            _, _, g, k = _gates(f_ref[rows, :], lb)
            G = _cumsum_rows(g)
            Gl = G[SUB - 1:SUB, :]
            qt = (q * jnp.exp(G)).astype(BF16)
            kd = (k * jnp.exp(Gl - G)).astype(BF16)
            eGl = jnp.exp(Gl)
            ib = iv.astype(BF16)
            A = [jnp.zeros((SUB, SUB), F32) for _ in range(HG_HEADS)]
            for s in range(SUB):
                P = q * jnp.exp(jnp.minimum(G - G[s:s + 1, :], 0.0)) * k[s:s + 1, :]
                for h in range(HG_HEADS):
                    A[h] = jnp.where(c16 == s, jnp.sum(P[:, _head(h)], axis=-1, keepdims=True), A[h])
            outs, ons, amats = [], [], []
            for h in range(HG_HEADS):
                sl = _head(h)
                Sb = St[h].astype(BF16)
                s_ref[j, sl, :] = Sb
                Am = jnp.where(r16 >= c16, A[h], 0.0)
                amats.append(Am)
                o = _dnt(qt[:, sl], Sb) + _dnn(Am.astype(BF16), ib[:, sl])
                St[h] = eGl[:, sl] * St[h] + _dtn(ib[:, sl], kd[:, sl])
                outs.append(o)
                ons.append(o * lax.rsqrt(jnp.mean(o * o, axis=-1, keepdims=True) + RMS_EPS))
            a_ref[rows, :] = jnp.concatenate(amats, axis=1)
            o_ref[rows, :] = jnp.concatenate(outs, axis=1)
            y = jnp.concatenate(ons, axis=1) * gn * (gz * _sigmoid(gz))
            y_ref[rows, :] = y.astype(BF16)
            return carry

        lax.fori_loop(0, nsub, block, 0, unroll=2)

    col = lambda c: pl.BlockSpec((ts, HG_W), lambda i: (i, c))
    return pl.pallas_call(
        body, name=name, grid=(n,),
        in_specs=[col(0), col(1), col(2), col(3), pl.BlockSpec((2, HG_W), lambda i: (0, 0)),
                  pl.BlockSpec((1, HG_D), lambda i: (0, 0))],
        out_specs=[pl.BlockSpec((ts, HG_W), lambda i: (i, 0)), pl.BlockSpec((ts, HG_W), lambda i: (i, 0)),
                   pl.BlockSpec((nsub, HG_W, HG_D), lambda i: (i, 0, 0)),
                   pl.BlockSpec((ts, HG_HEADS * SUB), lambda i: (i, 0))],
        out_shape=[jax.ShapeDtypeStruct((L, 2 * HG_W), BF16), jax.ShapeDtypeStruct((L, HG_W), F32),
                   jax.ShapeDtypeStruct((L // SUB, HG_W, HG_D), BF16),
                   jax.ShapeDtypeStruct((L, HG_HEADS * SUB), F32)],
        scratch_shapes=[pltpu.VMEM((HG_HEADS, HG_D, HG_D), F32)],
        compiler_params=_params(("arbitrary",)),
    )(hh, hh, hh, hh, lbp, gnorm)


def _hgrn_bwd(hh, o_raw, states, amat, dcat, lbp, gnorm, layer, name):
    L = hh.shape[0]
    ts = SEQ_TILE
    n = L // ts
    nsub = ts // SUB

    def body(q_ref, f_ref, i_ref, g_ref, o_ref, s_ref, a_ref, dy_ref, lb_ref, gn_ref,
             dh_ref, dlb_ref, dgn_ref, dSt, dlb_acc, S_next):
        step = pl.program_id(0)

        @pl.when(step == 0)
        def _():
            dSt[...] = jnp.zeros_like(dSt)
            S_next[...] = jnp.zeros_like(S_next)
            dlb_acc[...] = jnp.zeros_like(dlb_acc)
            dgn_ref[...] = jnp.zeros_like(dgn_ref)

        lb, p0, p1 = _lower_bound(lb_ref, layer)
        gnh = gn_ref[...]
        gn = jnp.tile(gnh, (1, HG_HEADS))
        r16 = lax.broadcasted_iota(jnp.int32, (SUB, SUB), 0)
        c16 = lax.broadcasted_iota(jnp.int32, (SUB, SUB), 1)

        def block(jj, carry):
            j = nsub - 1 - jj
            rows = pl.ds(pl.multiple_of(j * SUB, SUB), SUB)
            q = q_ref[rows, :] * Q_SCALE
            iv = i_ref[rows, :]
            gz = g_ref[rows, :]
            o = o_ref[rows, :]
            dy = dy_ref[rows, :].astype(F32)
            sig, f, g, k = _gates(f_ref[rows, :], lb)
            G = _cumsum_rows(g)
            Gl = G[SUB - 1:SUB, :]
            eG = jnp.exp(G)
            edl = jnp.exp(Gl - G)
            eGl = jnp.exp(Gl)
            qt = (q * eG).astype(BF16)
            kd = (k * edl).astype(BF16)
            ib = iv.astype(BF16)
            sgz = _sigmoid(gz)
            sil = gz * sgz
            dyn = dy * sil
            on_parts, do_parts = [], []
            dgn = jnp.zeros((1, HG_D), F32)
            for h in range(HG_HEADS):
                sl = _head(h)
                oh = o[:, sl]
                rs = lax.rsqrt(jnp.mean(oh * oh, axis=-1, keepdims=True) + RMS_EPS)
                on = oh * rs
                dgn = dgn + jnp.sum(dyn[:, sl] * on, axis=0, keepdims=True)
                don = dyn[:, sl] * gnh
                do_parts.append(rs * (don - on * jnp.mean(don * on, axis=-1, keepdims=True)))
                on_parts.append(on)
            dgn_ref[...] += dgn
            on_all = jnp.concatenate(on_parts, axis=1)
            dgz = dy * on_all * gn * (sgz * (1.0 + gz * (1.0 - sgz)))
            do = jnp.concatenate(do_parts, axis=1)
            dob = do.astype(BF16)
            amat = a_ref[rows, :]
            dq_p, dk_p, di_p, tail_p = [], [], [], []
            for h in range(HG_HEADS):
                sl = _head(h)
                qh, kh, Gh = q[:, sl], k[:, sl], G[:, sl]
                Ap = jnp.where(r16 >= c16, _dnt(dob[:, sl], ib[:, sl]), 0.0)
                ApT = jnp.where(r16 <= c16, _dnt(ib[:, sl], dob[:, sl]), 0.0)
                dqh = jnp.zeros((SUB, HG_D), F32)
                dkh = jnp.zeros((SUB, HG_D), F32)
                for s in range(SUB):
                    dGs = Gh - Gh[s:s + 1, :]
                    e = jnp.exp(jnp.minimum(dGs, -dGs))
                    dqh = dqh + Ap[:, s:s + 1] * (e * kh[s:s + 1, :])
                    dkh = dkh + ApT[:, s:s + 1] * (e * qh[s:s + 1, :])
                Sb = s_ref[j, sl, :]
                dSb = dSt[h].astype(BF16)
                Am = amat[:, h * SUB:(h + 1) * SUB].astype(BF16)
                dq_p.append(dqh + eG[:, sl] * _dnn(dob[:, sl], Sb))
                dk_p.append(dkh + edl[:, sl] * _dnn(ib[:, sl], dSb))
                di_p.append(_dtn(Am, dob[:, sl]) + _dnt(kd[:, sl], dSb))
                tail_p.append(jnp.sum(dSt[h] * S_next[h].astype(F32), axis=0, keepdims=True))
                S_next[h] = Sb
                dSt[h] = eGl[:, sl] * dSt[h] + _dtn(dob[:, sl], qt[:, sl])
            dq = jnp.concatenate(dq_p, axis=1)
            dk = jnp.concatenate(dk_p, axis=1)
            di = jnp.concatenate(di_p, axis=1)
            dg = _cumsum_rows(q * dq - k * dk, reverse=True) + jnp.concatenate(tail_p, axis=1)
            df = jnp.where(f > F_FLOOR, dg / f, 0.0)
            dfk = df - dk
            dfz = (1.0 - lb) * dfk * sig * (1.0 - sig)
            dlb_acc[...] += jnp.sum(dfk * (1.0 - sig), axis=0, keepdims=True)
            dh_ref[rows, :] = jnp.concatenate([dq * Q_SCALE, dfz, di, dgz], axis=1).astype(BF16)
            return carry

        lax.fori_loop(0, nsub, block, 0)

        @pl.when(step == n - 1)
        def _():
            if layer == 0:
                dlb_ref[...] = jnp.zeros_like(dlb_ref)
            else:
                dz1 = p0 * p1 * dlb_acc[...]
                dlb_ref[...] = jnp.concatenate([-dz1, dz1], axis=0)

    rev = lambda i: n - 1 - i
    col = lambda c: pl.BlockSpec((ts, HG_W), lambda i: (rev(i), c))
    return pl.pallas_call(
        body, name=name, grid=(n,),
        in_specs=[col(0), col(1), col(2), col(3), col(0),
                  pl.BlockSpec((nsub, HG_W, HG_D), lambda i: (rev(i), 0, 0)),
                  pl.BlockSpec((ts, HG_HEADS * SUB), lambda i: (rev(i), 0)), col(0),
                  pl.BlockSpec((2, HG_W), lambda i: (0, 0)), pl.BlockSpec((1, HG_D), lambda i: (0, 0))],
        out_specs=[pl.BlockSpec((ts, 4 * HG_W), lambda i: (rev(i), 0)),
                   pl.BlockSpec((2, HG_W), lambda i: (0, 0)), pl.BlockSpec((1, HG_D), lambda i: (0, 0))],
        out_shape=[jax.ShapeDtypeStruct((L, 4 * HG_W + 1024), BF16), jax.ShapeDtypeStruct((2, HG_W), F32),
                   jax.ShapeDtypeStruct((1, HG_D), F32)],
        scratch_shapes=[pltpu.VMEM((HG_HEADS, HG_D, HG_D), F32), pltpu.VMEM((1, HG_W), F32),
                        pltpu.VMEM((HG_HEADS, HG_D, HG_D), BF16)],
        compiler_params=_params(("arbitrary",)),
    )(hh, hh, hh, hh, o_raw, states, amat, dcat, lbp, gnorm)


def _adamw_body(gp_ref, w_ref, m_ref, v_ref, g_ref, d_ref, mo_ref, vo_ref):
    c1 = 1.0 - ADAM_B1 ** ADAM_STEP
    c2 = 1.0 - ADAM_B2 ** ADAM_STEP
    g = gp_ref[0].astype(F32)
    for k in range(1, N_DEV):
        g = g + gp_ref[k].astype(F32)
    mn = ADAM_B1 * m_ref[...] + (1.0 - ADAM_B1) * g
    vn = ADAM_B2 * v_ref[...] + (1.0 - ADAM_B2) * (g * g)
    m_hat = mn / c1
    v_hat = vn / c2
    g_ref[...] = g
    d_ref[...] = -ADAM_LR * (m_hat / (jnp.sqrt(v_hat) + ADAM_EPS) + ADAM_WD * w_ref[...])
    mo_ref[...] = mn
    vo_ref[...] = vn


def _adamw_layers(gparts, w, m, v, name):
    depth, R, C = w.shape
    tr = _tile(R, 256, 16)
    nr = R // tr

    def body(*refs):
        layer = pl.program_id(0)
        for d in range(depth):
            @pl.when(layer == d)
            def _(d=d):
                _adamw_body(refs[d], *refs[depth:])

    def parts_spec(d):
        return pl.BlockSpec((N_DEV, tr, C),
                            lambda l, i: (0, jnp.where(l == d, i, jnp.where(l < d, 0, nr - 1)), 0))

    blk = pl.BlockSpec((None, tr, C), lambda l, i: (l, i, 0))
    shp = jax.ShapeDtypeStruct((depth, R, C), F32)
    return pl.pallas_call(
        body, name=name, grid=(depth, nr),
        in_specs=[parts_spec(d) for d in range(depth)] + [blk, blk, blk],
        out_specs=[blk, blk, blk, blk], out_shape=[shp, shp, shp, shp],
        compiler_params=_params(("arbitrary", "arbitrary")),
    )(*gparts, w, m, v)


def _adamw(gparts, w, m, v, name):
    R = w.shape[0]
    tr = _tile(R, 1024, 16) if R % 16 == 0 else R

    def body(*refs):
        _adamw_body(*refs)

    row = pl.BlockSpec((tr, LANES), lambda i: (i, 0))
    shp = jax.ShapeDtypeStruct((R, LANES), F32)
    return pl.pallas_call(
        body, name=name, grid=(R // tr,),
        in_specs=[pl.BlockSpec((N_DEV, tr, LANES), lambda i: (0, i, 0)), row, row, row],
        out_specs=[row, row, row, row], out_shape=[shp, shp, shp, shp],
        compiler_params=_params(("parallel",)),
    )(gparts, w, m, v)


def _flip(coord, bit):
    return 1 - coord if bit else coord


def _gather_many(blocks, name):
    n = len(blocks)

    def body(*refs):
        x_refs, out_refs = refs[:n], refs[n:2 * n]
        send_sems, recv_sems, local_sems = refs[2 * n:]
        x, y, c = lax.axis_index("x"), lax.axis_index("y"), lax.axis_index("c")
        me, sibling = (x, y, c), (x, y, 1 - c)
        chips = [(1 - x, y), (x, 1 - y), (1 - x, 1 - y)]

        def slot(a, px, py, pc):
            return out_refs[a].at[4 * px + 2 * py + pc]

        def copy(a, k, blk, to, src=None):
            return pltpu.make_async_remote_copy(
                src_ref=slot(a, *blk) if src is None else src, dst_ref=slot(a, *blk),
                send_sem=send_sems.at[7 * a + k], recv_sem=recv_sems.at[7 * a + k],
                device_id=to, device_id_type=pl.DeviceIdType.MESH)

        mine = [pltpu.make_async_copy(x_refs[a], slot(a, *me), local_sems.at[a]) for a in range(n)]
        for cp in mine:
            cp.start()
        first = [copy(a, 0, me, sibling, src=x_refs[a]) for a in range(n)]
        for j, chip in enumerate(chips):
            first += [copy(a, 1 + j, me, (*chip, c), src=x_refs[a]) for a in range(n)]
        for cp in first:
            cp.start()
        passed = []
        for j, chip in enumerate(chips):
            for a in range(n):
                copy(a, 1 + j, (*chip, c), me).wait_recv()
                fwd = copy(a, 4 + j, (*chip, c), sibling)
                fwd.start()
                passed.append(fwd)
        for a in range(n):
            copy(a, 0, sibling, me).wait_recv()
        for j, chip in enumerate(chips):
            for a in range(n):
                copy(a, 4 + j, (*chip, 1 - c), me).wait_recv()
        for cp in first + passed:
            cp.wait_send()
        for cp in mine:
            cp.wait()

    hbm = pl.BlockSpec(memory_space=pl.ANY)
    return pl.pallas_call(
        body, name=name,
        out_shape=[jax.ShapeDtypeStruct((N_DEV,) + b.shape, b.dtype) for b in blocks],
        in_specs=[hbm] * n, out_specs=[hbm] * n,
        scratch_shapes=[pltpu.SemaphoreType.DMA((7 * n,)), pltpu.SemaphoreType.DMA((7 * n,)),
                        pltpu.SemaphoreType.DMA((n,))],
    )(*blocks)


def _split_start(blocks, chunked, name):
    n = len(blocks)
    lands = [lax.empty(b.shape if chunked else (N_DEV,) + b.shape, b.dtype) for b in blocks]

    def body(*refs):
        x_refs, land_refs = refs[:n], refs[n:2 * n]
        send_sems, recv_sems, token = refs[2 * n], refs[2 * n + 1], refs[-1]
        x, y, c = lax.axis_index("x"), lax.axis_index("y"), lax.axis_index("c")
        me = 4 * x + 2 * y + c
        for a in range(n):
            for k in range(1, N_DEV):
                px, py, pc = _flip(x, k & 4), _flip(y, k & 2), _flip(c, k & 1)
                pltpu.make_async_remote_copy(
                    src_ref=x_refs[a].at[4 * px + 2 * py + pc] if chunked else x_refs[a],
                    dst_ref=land_refs[a].at[me],
                    send_sem=send_sems.at[7 * a + k - 1], recv_sem=recv_sems.at[7 * a + k - 1],
                    device_id=(px, py, pc), device_id_type=pl.DeviceIdType.MESH).start()
        token[...] = jnp.zeros_like(token)

    hbm = pl.BlockSpec(memory_space=pltpu.HBM)
    sem = pl.BlockSpec(memory_space=pltpu.SEMAPHORE)
    outs = pl.pallas_call(
        body, name=name,
        out_shape=(pltpu.SemaphoreType.DMA((7 * n,)), pltpu.SemaphoreType.DMA((7 * n,)),
                   *[pltpu.HBM(b.shape, b.dtype) for b in blocks], *[pltpu.HBM(l.shape, l.dtype) for l in lands],
                   jax.ShapeDtypeStruct((8, LANES), F32)),
        in_specs=[hbm] * (2 * n),
        out_specs=(sem, sem, *[hbm] * (2 * n), pl.BlockSpec(memory_space=pltpu.VMEM)),
        input_output_aliases={i: 2 + i for i in range(2 * n)},
        compiler_params=pltpu.CompilerParams(has_side_effects=pltpu.SideEffectType.DATAFLOW_SIDE_EFFECTING),
    )(*[pltpu.with_memory_space_constraint(b, pltpu.HBM) for b in blocks],
      *[pltpu.with_memory_space_constraint(l, pltpu.HBM) for l in lands])
    return outs[0], outs[1], list(outs[2:2 + n]), list(outs[2 + n:2 + 2 * n]), outs[-1]


def _split_wait(started, chunked, after, name):
    send_sems, recv_sems, blocks, lands, _ = started
    n = len(blocks)

    def body(*refs):
        x_refs, land_refs = refs[:n], refs[n:2 * n]
        send_sems, recv_sems = refs[2 * n], refs[2 * n + 1]
        x, y, c = lax.axis_index("x"), lax.axis_index("y"), lax.axis_index("c")
        for a in range(n):
            for k in range(1, N_DEV):
                px, py, pc = _flip(x, k & 4), _flip(y, k & 2), _flip(c, k & 1)
                copy = pltpu.make_async_remote_copy(
                    src_ref=x_refs[a].at[4 * px + 2 * py + pc] if chunked else x_refs[a],
                    dst_ref=land_refs[a].at[4 * px + 2 * py + pc],
                    send_sem=send_sems.at[7 * a + k - 1], recv_sem=recv_sems.at[7 * a + k - 1],
                    device_id=(px, py, pc), device_id_type=pl.DeviceIdType.MESH)
                copy.wait_send()
                copy.wait_recv()

    hbm = pl.BlockSpec(memory_space=pltpu.HBM)
    sem = pl.BlockSpec(memory_space=pltpu.SEMAPHORE)
    outs = pl.pallas_call(
        body, name=name,
        out_shape=(*[pltpu.HBM(b.shape, b.dtype) for b in blocks], *[pltpu.HBM(l.shape, l.dtype) for l in lands]),
        in_specs=[hbm] * (2 * n) + [sem, sem, pl.BlockSpec(memory_space=pl.ANY)],
        out_specs=[hbm] * (2 * n),
        input_output_aliases={i: i for i in range(2 * n)},
        compiler_params=pltpu.CompilerParams(has_side_effects=pltpu.SideEffectType.DATAFLOW_SIDE_EFFECTING),
    )(*blocks, *lands, send_sems, recv_sems, after)
    me = 4 * lax.axis_index("x") + 2 * lax.axis_index("y") + lax.axis_index("c")
    own = [lax.dynamic_index_in_dim(b, me, 0, keepdims=False) if chunked else b for b in outs[:n]]
    return [lax.dynamic_update_index_in_dim(z, o, me, 0) for z, o in zip(outs[n:], own)]


def _exchange_grads(layer_chunks, small_chunks, rep_block, name):
    flows, inputs = [], []
    for p, per_layer in enumerate(layer_chunks):
        for l, arr in enumerate(per_layer):
            flows.append(("param", p, l))
            inputs.append(arr)
    flows += [("small",), ("rep",)]
    inputs += [small_chunks, rep_block]
    n_par = len(layer_chunks)
    n_in, n_out, nf = len(inputs), n_par + 2, len(flows)

    def body(*refs):
        in_refs, out_refs = refs[:n_in], refs[n_in:n_in + n_out]
        send_sems, recv_sems, local_sems = refs[n_in + n_out:]
        x, y, c = lax.axis_index("x"), lax.axis_index("y"), lax.axis_index("c")
        me = 4 * x + 2 * y + c

        def src(f, dev):
            return in_refs[f] if flows[f][0] == "rep" else in_refs[f].at[dev]

        def dst(f, dev):
            if flows[f][0] == "param":
                _, p, l = flows[f]
                return out_refs[p].at[dev, l]
            return out_refs[n_par + (0 if flows[f][0] == "small" else 1)].at[dev]

        mine = [pltpu.make_async_copy(src(f, me), dst(f, me), local_sems.at[f]) for f in range(nf)]
        for cp in mine:
            cp.start()
        copies = []
        for k in range(1, N_DEV):
            px, py, pc = _flip(x, k & 4), _flip(y, k & 2), _flip(c, k & 1)
            peer = 4 * px + 2 * py + pc
            for f in range(nf):
                sems = dict(send_sem=send_sems.at[7 * f + k - 1], recv_sem=recv_sems.at[7 * f + k - 1],
                            device_id=(px, py, pc), device_id_type=pl.DeviceIdType.MESH)
                send = pltpu.make_async_remote_copy(src_ref=src(f, peer), dst_ref=dst(f, me), **sems)
                recv = pltpu.make_async_remote_copy(src_ref=src(f, peer), dst_ref=dst(f, peer), **sems)
                send.start()
                copies.append((send, recv))
        for send, recv in copies:
            recv.wait_recv()
        for send, recv in copies:
            send.wait_send()
        for cp in mine:
            cp.wait()

    out_shape = [jax.ShapeDtypeStruct((N_DEV, len(pl_)) + pl_[0].shape[1:], pl_[0].dtype) for pl_ in layer_chunks]
    out_shape += [jax.ShapeDtypeStruct(small_chunks.shape, small_chunks.dtype),
                  jax.ShapeDtypeStruct((N_DEV,) + rep_block.shape, rep_block.dtype)]
    hbm = pl.BlockSpec(memory_space=pl.ANY)
    return pl.pallas_call(
        body, name=name, out_shape=out_shape,
        in_specs=[hbm] * n_in, out_specs=[hbm] * n_out,
        scratch_shapes=[pltpu.SemaphoreType.DMA((7 * nf,)), pltpu.SemaphoreType.DMA((7 * nf,)),
                        pltpu.SemaphoreType.DMA((nf,))],
    )(*inputs)


def _pack_rows(size):
    return -(-size // (8 * LANES)) * 8


def _pack(arrs, dtype):
    parts, offs, r = [], [], 0
    for a in arrs:
        flat = a.astype(dtype).reshape(-1)
        nrow = _pack_rows(flat.shape[0])
        flat = jnp.pad(flat, (0, nrow * LANES - flat.shape[0]))
        parts.append(flat.reshape(nrow, LANES))
        offs.append((r, nrow))
        r += nrow
    return jnp.concatenate(parts, axis=0), offs


def _unpack(buf, offs, shapes, lead=()):
    outs = []
    for (r, nrow), shp in zip(offs, shapes):
        size = 1
        for s in shp:
            size *= s
        flat = buf[..., r:r + nrow, :].reshape(lead + (nrow * LANES,))
        outs.append(flat[..., :size].reshape(lead + tuple(shp)))
    return outs


def _cols_from_shards(g, axis):
    return jnp.concatenate([g[j] for j in range(N_DEV)], axis=axis)


BIG = ("w_in", "w_o", "w_up", "w_down")
SMALL_SHARDED = ("meta_tokens", "w_conv", "w_ffn_conv")
REPLICATED = ("hg_lower_bounds", "w_pool", "pool_scale", "hg_norm_g", "ln1_g", "ln1_b", "b_ffn_conv", "ln2_g", "ln2_b")
WEIGHTS = ("meta_tokens", "hg_lower_bounds", "w_in", "w_conv", "w_pool", "pool_scale", "hg_norm_g", "w_o",
           "ln1_g", "ln1_b", "w_up", "w_ffn_conv", "b_ffn_conv", "w_down", "ln2_g", "ln2_b")


def _pool_blockdiag(w_pool_l):
    z = jnp.zeros((POOL_GROUP, POOL_GROUP), w_pool_l.dtype)
    rows = [jnp.concatenate([w_pool_l[g] if h == g else z for h in range(4)], axis=1) for g in range(4)]
    return jnp.concatenate(rows, axis=0)


def _in_weights(g_in):
    w_in = jnp.transpose(g_in, (1, 0, 2)).reshape(D_MODEL, -1)
    return dict(w_hc=jnp.concatenate([w_in[:, 768:2816], w_in[:, 0:768], w_in[:, 2816:3072]], axis=1))


def _rest_weights(g_o, g_up, g_down):
    w_o = g_o.reshape(-1, D_MODEL)
    return dict(w_o=jnp.concatenate([w_o[256:768], w_o[0:256], w_o[768:1024]], axis=0),
                w_up=jnp.transpose(g_up, (1, 0, 2)).reshape(D_MODEL, -1), w_down=g_down.reshape(-1, D_MODEL))


def kernel(x, meta_tokens, hg_lower_bounds, w_in, w_conv, w_pool, pool_scale, hg_norm_g, w_o, ln1_g, ln1_b, w_up, w_ffn_conv, b_ffn_conv, w_down, ln2_g, ln2_b, loss_target, m_meta_tokens, m_hg_lower_bounds, m_w_in, m_w_conv, m_w_pool, m_pool_scale, m_hg_norm_g, m_w_o, m_ln1_g, m_ln1_b, m_w_up, m_w_ffn_conv, m_b_ffn_conv, m_w_down, m_ln2_g, m_ln2_b, v_meta_tokens, v_hg_lower_bounds, v_w_in, v_w_conv, v_w_pool, v_pool_scale, v_hg_norm_g, v_w_o, v_ln1_g, v_ln1_b, v_w_up, v_w_ffn_conv, v_b_ffn_conv, v_w_down, v_ln2_g, v_ln2_b):
    W = dict(meta_tokens=meta_tokens, hg_lower_bounds=hg_lower_bounds, w_in=w_in, w_conv=w_conv, w_pool=w_pool,
             pool_scale=pool_scale, hg_norm_g=hg_norm_g, w_o=w_o, ln1_g=ln1_g, ln1_b=ln1_b, w_up=w_up,
             w_ffn_conv=w_ffn_conv, b_ffn_conv=b_ffn_conv, w_down=w_down, ln2_g=ln2_g, ln2_b=ln2_b)
    M = dict(meta_tokens=m_meta_tokens, hg_lower_bounds=m_hg_lower_bounds, w_in=m_w_in, w_conv=m_w_conv,
             w_pool=m_w_pool, pool_scale=m_pool_scale, hg_norm_g=m_hg_norm_g, w_o=m_w_o, ln1_g=m_ln1_g,
             ln1_b=m_ln1_b, w_up=m_w_up, w_ffn_conv=m_w_ffn_conv, b_ffn_conv=m_b_ffn_conv, w_down=m_w_down,
             ln2_g=m_ln2_g, ln2_b=m_ln2_b)
    V = dict(meta_tokens=v_meta_tokens, hg_lower_bounds=v_hg_lower_bounds, w_in=v_w_in, w_conv=v_w_conv,
             w_pool=v_w_pool, pool_scale=v_pool_scale, hg_norm_g=v_hg_norm_g, w_o=v_w_o, ln1_g=v_ln1_g,
             ln1_b=v_ln1_b, w_up=v_w_up, w_ffn_conv=v_w_ffn_conv, b_ffn_conv=v_b_ffn_conv, w_down=v_w_down,
             ln2_g=v_ln2_g, ln2_b=v_ln2_b)
    assert x.shape[0] == 1 and x.shape[2] == D_MODEL and w_in.shape[0] == DEPTH
    seq = x.shape[1]
    L = -(-(seq + N_META) // ROW_ALIGN) * ROW_ALIGN

    small_pack, small_offs = _pack([W[n] for n in SMALL_SHARDED], F32)
    shards = {n: [W[n][l].astype(BF16) for l in range(DEPTH)] for n in BIG}
    g_in0, small_all = _gather_many([shards["w_in"][0], small_pack], "gather_weights")
    full = {}
    for n, a in zip(SMALL_SHARDED, _unpack(small_all, small_offs, [W[n].shape for n in SMALL_SHARDED], (N_DEV,))):
        full[n] = _cols_from_shards(a, 1)
    order = (small_all[0, 0, 0] * 0.0).astype(BF16)
    rest0_started = _split_start([shards[n][0] + order for n in BIG[1:]], False, "gather_rest0_start")
    order = rest0_started[4][0, 0].astype(BF16)
    layer1_started = _split_start([shards[n][1] + order for n in BIG], False, "gather_layer1_start")
    lb_in = hg_lower_bounds + layer1_started[4][0, 0]

    pad_rows = L - N_META - seq
    xp = jnp.concatenate([full["meta_tokens"], x[0], jnp.zeros((pad_rows, D_MODEL), F32)], axis=0)
    tgt = jnp.concatenate([jnp.zeros((N_META, D_MODEL), F32), loss_target[0], jnp.zeros((pad_rows, D_MODEL), F32)], axis=0)

    saved = []
    h_in, h_in_b = xp, xp.astype(BF16)
    for l in range(DEPTH):
        if l == 0:
            lw = _in_weights(g_in0)
        else:
            g_in, *g_rest = _split_wait(layer1_started, False, h_in_b, "gather_layer1_wait")
            lw = {**_in_weights(g_in), **_rest_weights(*g_rest)}
        wc = full["w_conv"][l].T
        wblk = _pool_blockdiag(w_pool[l]).astype(BF16)
        ps = pool_scale[l][None, :]
        gn = hg_norm_g[l][None, :]
        wf = full["w_ffn_conv"][l].T
        bf = b_ffn_conv[l][None, :]
        h = _matmul(h_in_b, lw["w_hc"], "nn", F32, f"fwd_in_{l}")
        y_hg, o_raw, states, amat = _hgrn_fwd(h, lb_in if l == 0 else hg_lower_bounds, gn, l, f"hgrn_fwd_{l}")
        cat = _cp_fwd(h, 2, y_hg, wc, wblk, ps, f"convpool_fwd_{l}")
        if l == 0:
            lw.update(_rest_weights(*_split_wait(rest0_started, False, cat, "gather_rest0_wait")))
        z1, x1, x1_b = _matmul_ln(cat, lw["w_o"], h_in, ln1_g[l][None, :], ln1_b[l][None, :], f"fwd_o_ln1_{l}")
        up = _matmul(x1_b, lw["w_up"], "nn", BF16, f"fwd_up_{l}")
        a, u = _ffn_act_fwd(up, wf, bf, f"ffn_fwd_{l}")
        saved.append(dict(lw=lw, wc=wc, wblk=wblk, ps=ps, gn=gn, wf=wf, bf=bf, x_b=h_in_b, h=h,
                          o_raw=o_raw, states=states, amat=amat, cat=cat, z1=z1, x1_b=x1_b, up=up, u=u, a=a))
        if l < DEPTH - 1:
            saved[l]["z2"], h_in, h_in_b = _matmul_ln(a, lw["w_down"], x1, ln2_g[l][None, :], ln2_b[l][None, :],
                                                     f"fwd_down_ln2_{l}")
        else:
            saved[l]["z2"], dy, loss_part = _matmul_ln(a, lw["w_down"], x1, ln2_g[l][None, :], ln2_b[l][None, :],
                                                       f"fwd_down_ln2_loss_{l}", loss=(tgt, seq))

    loss = lax.psum(loss_part[0, 0], ("x", "y", "c"))

    G = {}
    per_layer = {n: [None] * DEPTH for n in ("w_conv", "w_pool", "pool_scale", "hg_norm_g", "ln1_g", "ln1_b",
                                             "w_ffn_conv", "b_ffn_conv", "ln2_g", "ln2_b")}
    ffn_started, mix_started = [None] * DEPTH, [None] * DEPTH
    order = jnp.zeros((), F32)
    dlb_total = jnp.zeros((DEPTH, HG_W), F32)
    for l in reversed(range(DEPTH)):
        s = saved[l]
        lw = s["lw"]
        dz2_b, dg2, db2 = _ln_bwd(s["z2"], dy, ln2_g[l][None, :] + order, f"ln2_bwd_{l}")
        da = _matmul(dz2_b, lw["w_down"], "nt", BF16, f"bwd_da_{l}")
        d_w_down = _matmul(s["a"], dz2_b, "tn", BF16, f"wgrad_down_{l}")
        dup, dwf, dbf = _ffn_act_bwd(s["up"], s["u"], da, s["wf"], f"ffn_bwd_{l}")
        dx1 = _matmul(dup, lw["w_up"], "nt", BF16, f"bwd_dx1_{l}", res=dz2_b, alpha=ALPHA)
        d_w_up = _matmul(s["x1_b"], dup, "tn", BF16, f"wgrad_up_{l}")
        ffn_started[l] = _split_start([jnp.transpose(d_w_up.reshape(D_MODEL, N_DEV, -1), (1, 0, 2)),
                                       d_w_down.reshape(N_DEV, -1, D_MODEL)], True, f"scatter_ffn{l}_start")
        order = ffn_started[l][4][0, 0]
        dz1_b, dg1, db1 = _ln_bwd(s["z1"], dx1, ln1_g[l][None, :] + order, f"ln1_bwd_{l}")
        dcat = _matmul(dz1_b, lw["w_o"], "nt", BF16, f"bwd_dcat_{l}")
        d_w_o = _matmul(s["cat"], dz1_b, "tn", BF16, f"wgrad_o_{l}")
        dh, dlb, dgn = _hgrn_bwd(s["h"], s["o_raw"], s["states"], s["amat"], dcat, hg_lower_bounds, s["gn"], l,
                                 f"hgrn_bwd_{l}")
        dh, dwc, dwblk, dps = _cp_bwd(s["h"], 2, dcat, dh, s["wc"], s["wblk"], s["ps"], f"convpool_bwd_{l}")
        d_w_hc = _matmul(s["x_b"], dh, "tn", BF16, f"wgrad_in_{l}")
        d_w_in = jnp.concatenate([d_w_hc[:, 2048:2816], d_w_hc[:, 0:2048], d_w_hc[:, 2816:3072]], axis=1)
        mix_chunks = [jnp.transpose(d_w_in.reshape(D_MODEL, N_DEV, -1), (1, 0, 2)),
                      jnp.concatenate([d_w_o[512:768], d_w_o[0:512], d_w_o[768:1024]], axis=0).reshape(N_DEV, -1, D_MODEL)]
        mix_started[l] = _split_start(mix_chunks, True, f"scatter_mix{l}_start")
        order = mix_started[l][4][0, 0]
        dx = _matmul(dh, lw["w_hc"] + order.astype(BF16), "nt", F32, f"bwd_dx_{l}", res=dz1_b, alpha=ALPHA)
        per_layer["w_conv"][l] = dwc.T
        per_layer["w_ffn_conv"][l] = dwf.T
        per_layer["b_ffn_conv"][l] = dbf[0]
        per_layer["w_pool"][l] = jnp.stack([dwblk[g * 64:(g + 1) * 64, g * 64:(g + 1) * 64] for g in range(4)], axis=0)
        per_layer["pool_scale"][l] = dps[0]
        per_layer["hg_norm_g"][l] = dgn[0]
        per_layer["ln1_g"][l], per_layer["ln1_b"][l] = dg1[0], db1[0]
        per_layer["ln2_g"][l], per_layer["ln2_b"][l] = dg2[0], db2[0]
        dlb_total = dlb_total + dlb
        dy = dx
    for n, parts in per_layer.items():
        G[n] = jnp.stack(parts, axis=0)
    G["hg_lower_bounds"] = dlb_total
    grad_x = dy[N_META:N_META + seq][None]

    def shard_major(g, lead):
        g = g.reshape(g.shape[:lead] + (N_DEV, -1) + g.shape[lead + 1:])
        g = jnp.moveaxis(g, lead, 0).reshape(N_DEV, -1)
        nrow = _pack_rows(g.shape[1])
        return jnp.pad(g, ((0, 0), (0, nrow * LANES - g.shape[1]))).reshape(N_DEV, nrow, LANES)

    small_chunks = jnp.concatenate([shard_major(dy[0:N_META], 1), shard_major(G["w_conv"], 1),
                                    shard_major(G["w_ffn_conv"], 1)], axis=1)
    w_small, _ = _pack([W[n] for n in SMALL_SHARDED], F32)
    rep_pack, rep_offs = _pack([G[n] for n in REPLICATED], F32)
    small_recv, rep_all = _exchange_grads([], small_chunks, rep_pack, "exchange_grads")
    parts = {n: [] for n in BIG}
    for l in range(DEPTH):
        up_l, down_l = _split_wait(ffn_started[l], True, rep_all, f"scatter_ffn{l}_wait")
        in_l, o_l = _split_wait(mix_started[l], True, rep_all, f"scatter_mix{l}_wait")
        for n, a in zip(BIG, (in_l, o_l, up_l, down_l)):
            parts[n].append(a)

    res = {k: {} for k in ("grad", "delta", "new_m", "new_v")}
    kinds = ("grad", "delta", "new_m", "new_v")
    for n in BIG:
        for kind, a in zip(kinds, _adamw_layers(parts[n], W[n], M[n], V[n], f"adamw_{n}")):
            res[kind][n] = a
    m_small, _ = _pack([M[n] for n in SMALL_SHARDED], F32)
    v_small, _ = _pack([V[n] for n in SMALL_SHARDED], F32)
    outs_small = _adamw(small_recv, w_small, m_small, v_small, "adamw_small_sharded")
    w_rep, _ = _pack([W[n] for n in REPLICATED], F32)
    m_rep, _ = _pack([M[n] for n in REPLICATED], F32)
    v_rep, _ = _pack([V[n] for n in REPLICATED], F32)
    outs_rep = _adamw(rep_all, w_rep, m_rep, v_rep, "adamw_replicated")
    for kind, b_sm, b_rep in zip(kinds, outs_small, outs_rep):
        for n, a in zip(SMALL_SHARDED, _unpack(b_sm, small_offs, [W[n].shape for n in SMALL_SHARDED])):
            res[kind][n] = a
        for n, a in zip(REPLICATED, _unpack(b_rep, rep_offs, [W[n].shape for n in REPLICATED])):
            res[kind][n] = a

    return (loss, grad_x, *[res["grad"][n] for n in WEIGHTS], *[res["delta"][n] for n in WEIGHTS],
            *[res["new_m"][n] for n in WEIGHTS], *[res["new_v"][n] for n in WEIGHTS])
```

```python
import jax
import jax.numpy as jnp
from jax import lax
from jax.experimental import pallas as pl
from jax.experimental.pallas import tpu as pltpu

F32 = jnp.float32
BF16 = jnp.bfloat16

N_DEV = 8
D_MODEL = 1024
N_META = 16
DEPTH = 2
CONV_W = 256
HG_W = 512
HG_D = 128
HG_HEADS = 4
POOL_W = 256
POOL_GROUP = 64
D_FF = 2816
ALPHA = (2 * DEPTH) ** 0.25
LN_EPS = 1e-5
RMS_EPS = 1e-6
F_FLOOR = 1e-30
Q_SCALE = HG_D ** -0.5
SUB = 16
SEQ_TILE = 192
FFN_TILE = 192
CP_TILE_CAP = 768
ROW_ALIGN = 192
LANES = 128
VMEM_LIMIT = 48 * 1024 * 1024
MATMUL_VMEM_BUDGET = 38 * 1024 * 1024

ADAM_LR = 0.001
ADAM_B1 = 0.9
ADAM_B2 = 0.999
ADAM_EPS = 1e-08
ADAM_WD = 0.01
ADAM_STEP = 10


def _tile(n, cap, mult):
    best = 0
    for t in range(mult, min(n, cap) + 1, mult):
        if n % t == 0:
            best = t
    assert best > 0, (n, cap, mult)
    return best


def _params(sem, vmem=VMEM_LIMIT):
    return pltpu.CompilerParams(dimension_semantics=sem, vmem_limit_bytes=vmem)


def _dnt(a, b):
    return lax.dot_general(a, b, (((1,), (1,)), ((), ())), preferred_element_type=F32)


def _dtn(a, b):
    return lax.dot_general(a, b, (((0,), (0,)), ((), ())), preferred_element_type=F32)


def _dnn(a, b):
    return jnp.dot(a, b, preferred_element_type=F32)


def _sigmoid(x):
    return jax.nn.sigmoid(x)


def _matmul(a, b, mode, out_dtype, name, res=None, alpha=1.0):
    if mode == "tn":
        K, M = a.shape
    else:
        M, K = a.shape
    N = b.shape[0] if mode == "nt" else b.shape[1]
    out_bytes = jnp.dtype(out_dtype).itemsize
    if mode == "tn" and M % 512 == 0 and N % 512 == 0:
        tk, tm, tn = K, 512, 512
        nk, use_acc = 1, False
    else:
        tn = _tile(N, 1536, LANES)
        tk = _tile(K, 1536, 16) if mode == "tn" else _tile(K, 2816, LANES)
        nk = K // tk
        use_acc = nk > 1 and out_dtype != F32
        tm = M
        for cap in (1536, 768, 384):
            tm = _tile(M, cap, 16)
            blocks = 2 * (a.dtype.itemsize * tm * tk + b.dtype.itemsize * tn * tk + out_bytes * tm * tn
                          + (res.dtype.itemsize * tm * tn if res is not None else 0)) + (4 * tm * tn if use_acc else 0)
            if blocks <= MATMUL_VMEM_BUDGET:
                break
    dims = {"nn": ((1,), (0,)), "nt": ((1,), (1,)), "tn": ((0,), (0,))}[mode]

    def body(*refs):
        a_ref, b_ref = refs[0], refs[1]
        r_ref = refs[2] if res is not None else None
        o_ref = refs[3] if res is not None else refs[2]
        acc = refs[-1] if use_acc else o_ref
        k = pl.program_id(2)
        p = lax.dot_general(a_ref[...].astype(BF16), b_ref[...].astype(BF16), (dims, ((), ())),
                            preferred_element_type=F32)

        def finish(r):
            if r_ref is not None:
                r = r + alpha * r_ref[...].astype(F32)
            o_ref[...] = r.astype(out_dtype)

        if nk == 1:
            finish(p)
        else:
            @pl.when(k == 0)
            def _():
                acc[...] = p

            @pl.when((k > 0) & (k < nk - 1))
            def _():
                acc[...] += p

            @pl.when(k == nk - 1)
            def _():
                finish(acc[...] + p)

    if mode == "tn":
        a_spec = pl.BlockSpec((tk, tm), lambda i, j, k: (k, i))
    else:
        a_spec = pl.BlockSpec((tm, tk), lambda i, j, k: (i, k))
    if mode == "nt":
        b_spec = pl.BlockSpec((tn, tk), lambda i, j, k: (j, k))
    else:
        b_spec = pl.BlockSpec((tk, tn), lambda i, j, k: (k, j))
    in_specs = [a_spec, b_spec]
    args = [a, b]
    if res is not None:
        in_specs.append(pl.BlockSpec((tm, tn), lambda i, j, k: (i, j)))
        args.append(res)
    return pl.pallas_call(
        body, name=name,
        grid=(M // tm, N // tn, nk),
        in_specs=in_specs,
        out_specs=pl.BlockSpec((tm, tn), lambda i, j, k: (i, j)),
        out_shape=jax.ShapeDtypeStruct((M, N), out_dtype),
        scratch_shapes=[pltpu.VMEM((tm, tn), F32)] if use_acc else [],
        compiler_params=_params(("parallel", "parallel", "arbitrary")),
    )(*args)


def _matmul_ln(a, w, x, g, b, name, loss=None):
    L, K = a.shape
    D = w.shape[1]
    tr = L
    for cap in (1536, 768, 384):
        tr = _tile(L, cap, 16)
        if 2 * (2 * tr * K + 2 * K * D + 4 * tr * D * (4 if loss else 3) + 2 * tr * D) <= MATMUL_VMEM_BUDGET:
            break

    def body(*refs):
        a_ref, w_ref, x_ref, g_ref, b_ref = refs[:5]
        z = ALPHA * x_ref[...] + _dnn(a_ref[...], w_ref[...])
        mu = jnp.mean(z, axis=-1, keepdims=True)
        zc = z - mu
        var = jnp.mean(zc * zc, axis=-1, keepdims=True)
        y = zc * lax.rsqrt(var + LN_EPS) * g_ref[...] + b_ref[...]
        if loss is None:
            z_ref, y_ref, yb_ref = refs[5:]
            y_ref[...] = y
            yb_ref[...] = y.astype(BF16)
        else:
            t_ref, z_ref, dy_ref, loss_ref = refs[5:]
            i = pl.program_id(0)

            @pl.when(i == 0)
            def _():
                loss_ref[...] = jnp.zeros_like(loss_ref)

            r = i * tr + lax.broadcasted_iota(jnp.int32, (tr, D), 0)
            valid = (r >= N_META) & (r < N_META + loss[1])
            e = jnp.where(valid, y - t_ref[...], 0.0)
            dy_ref[...] = e * (1.0 / D)
            s = jnp.sum(jnp.sum(e * e, axis=-1, keepdims=True), axis=0, keepdims=True)
            loss_ref[...] += (0.5 / D) * s
        z_ref[...] = z.astype(BF16)

    row = pl.BlockSpec((tr, D), lambda i: (i, 0))
    vec = pl.BlockSpec((1, D), lambda i: (0, 0))
    in_specs = [pl.BlockSpec((tr, K), lambda i: (i, 0)), pl.BlockSpec((K, D), lambda i: (0, 0)), row, vec, vec]
    f32_rows = jax.ShapeDtypeStruct((L, D), F32)
    b16_rows = jax.ShapeDtypeStruct((L, D), BF16)
    if loss is None:
        args, out_specs = [a, w, x, g, b], [row, row, row]
        out_shape = [b16_rows, f32_rows, b16_rows]
    else:
        args, in_specs = [a, w, x, g, b, loss[0]], in_specs + [row]
        out_specs = [row, row, pl.BlockSpec((1, 1), lambda i: (0, 0))]
        out_shape = [b16_rows, f32_rows, jax.ShapeDtypeStruct((1, 1), F32)]
    return pl.pallas_call(
        body, name=name, grid=(L // tr,), in_specs=in_specs, out_specs=out_specs, out_shape=out_shape,
        compiler_params=_params(("arbitrary",) if loss else ("parallel",)),
    )(*args)


def _ln_bwd(z, dy, g, name):
    L, D = z.shape
    tr = _tile(L, 768, 16)

    def body(z_ref, dy_ref, g_ref, dzb_ref, dg_ref, db_ref):
        @pl.when(pl.program_id(0) == 0)
        def _():
            dg_ref[...] = jnp.zeros_like(dg_ref)
            db_ref[...] = jnp.zeros_like(db_ref)

        z = z_ref[...].astype(F32)
        mu = jnp.mean(z, axis=-1, keepdims=True)
        zc = z - mu
        var = jnp.mean(zc * zc, axis=-1, keepdims=True)
        rstd = lax.rsqrt(var + LN_EPS)
        xhat = zc * rstd
        dy = dy_ref[...].astype(F32)
        dxh = dy * g_ref[...]
        m1 = jnp.mean(dxh, axis=-1, keepdims=True)
        m2 = jnp.mean(dxh * xhat, axis=-1, keepdims=True)
        dz = rstd * (dxh - m1 - xhat * m2)
        dzb_ref[...] = dz.astype(BF16)
        dg_ref[...] += jnp.sum(dy * xhat, axis=0, keepdims=True)
        db_ref[...] += jnp.sum(dy, axis=0, keepdims=True)

    row = pl.BlockSpec((tr, D), lambda i: (i, 0))
    vec = pl.BlockSpec((1, D), lambda i: (0, 0))
    return pl.pallas_call(
        body, name=name, grid=(L // tr,),
        in_specs=[row, row, vec], out_specs=[row, vec, vec],
        out_shape=[jax.ShapeDtypeStruct((L, D), BF16),
                   jax.ShapeDtypeStruct((1, D), F32), jax.ShapeDtypeStruct((1, D), F32)],
        compiler_params=_params(("arbitrary",)),
    )(z, dy, g)


def _shift_down(x, prev, k):
    T, C = x.shape
    rot = pltpu.roll(jnp.concatenate([prev, x], axis=0).reshape(T // 8 + 1, 8, C), k, 1)
    sub = lax.broadcasted_iota(jnp.int32, (T // 8, 8, C), 1)
    return jnp.where(sub < k, rot[:-1], rot[1:]).reshape(T, C)


def _shift_up(x, nxt, k):
    T, C = x.shape
    rot = pltpu.roll(jnp.concatenate([x, nxt], axis=0).reshape(T // 8 + 1, 8, C), 8 - k, 1)
    sub = lax.broadcasted_iota(jnp.int32, (T // 8, 8, C), 1)
    return jnp.where(sub >= 8 - k, rot[1:], rot[:-1]).reshape(T, C)


def _conv3(x, prev, w, b):
    return w[2:3, :] * x + w[1:2, :] * _shift_down(x, prev, 1) + w[0:1, :] * _shift_down(x, prev, 2) + b


def _ffn_act_fwd(up, w, b, name):
    L, C = up.shape
    F = C // 2
    ts = FFN_TILE
    n = L // ts

    def body(up_ref, pv_ref, w_ref, b_ref, a_ref, u_ref):
        i = pl.program_id(0)
        x = up_ref[...].astype(F32)
        prev = jnp.where(i > 0, pv_ref[...].astype(F32)[8:16], 0.0)
        u = _conv3(x, prev, w_ref[...], b_ref[...])
        u_ref[...] = u.astype(BF16)
        gate = u[:, :F]
        a_ref[...] = (gate * _sigmoid(gate) * u[:, F:]).astype(BF16)

    return pl.pallas_call(
        body, name=name, grid=(n,),
        in_specs=[pl.BlockSpec((ts, C), lambda i: (i, 0)),
                  pl.BlockSpec((16, C), lambda i: (jnp.maximum(i * (ts // 16) - 1, 0), 0)),
                  pl.BlockSpec((3, C), lambda i: (0, 0)), pl.BlockSpec((1, C), lambda i: (0, 0))],
        out_specs=[pl.BlockSpec((ts, F), lambda i: (i, 0)), pl.BlockSpec((ts, C), lambda i: (i, 0))],
        out_shape=[jax.ShapeDtypeStruct((L, F), BF16), jax.ShapeDtypeStruct((L, C), BF16)],
        compiler_params=_params(("parallel",)),
    )(up, up, w, b)


def _ffn_act_bwd(up, u, da, w, name):
    L, C = up.shape
    F = C // 2
    ts = FFN_TILE
    n = L // ts
    last16 = L // 16 - 1

    def du_of(u, da):
        gate, val = u[:, :F], u[:, F:]
        sg = _sigmoid(gate)
        dgate = da * val * (sg * (1.0 + gate * (1.0 - sg)))
        dval = da * (gate * sg)
        return jnp.concatenate([dgate, dval], axis=1)

    def body(up_ref, u_ref, un_ref, da_ref, dan_ref, w_ref, dup_ref, dw_ref, db_ref):
        i = pl.program_id(0)

        @pl.when(i == 0)
        def _():
            dw_ref[...] = jnp.zeros_like(dw_ref)
            db_ref[...] = jnp.zeros_like(db_ref)

        w = w_ref[...]
        x = up_ref[...].astype(F32)
        du = du_of(u_ref[...].astype(F32), da_ref[...].astype(F32))
        dun = jnp.where(i < n - 1, du_of(un_ref[...].astype(F32)[0:8], dan_ref[...].astype(F32)[0:8]), 0.0)
        du1 = _shift_up(du, dun, 1)
        du2 = _shift_up(du, dun, 2)
        dup_ref[...] = (w[2:3, :] * du + w[1:2, :] * du1 + w[0:1, :] * du2).astype(BF16)
        dw_ref[...] += jnp.concatenate([jnp.sum(x * du2, axis=0, keepdims=True),
                                        jnp.sum(x * du1, axis=0, keepdims=True),
                                        jnp.sum(x * du, axis=0, keepdims=True)], axis=0)
        db_ref[...] += jnp.sum(du, axis=0, keepdims=True)

    nxt = lambda i: (jnp.minimum((i + 1) * (ts // 16), last16), 0)
    return pl.pallas_call(
        body, name=name, grid=(n,),
        in_specs=[pl.BlockSpec((ts, C), lambda i: (i, 0)),
                  pl.BlockSpec((ts, C), lambda i: (i, 0)), pl.BlockSpec((16, C), nxt),
                  pl.BlockSpec((ts, F), lambda i: (i, 0)), pl.BlockSpec((16, F), nxt),
                  pl.BlockSpec((3, C), lambda i: (0, 0))],
        out_specs=[pl.BlockSpec((ts, C), lambda i: (i, 0)), pl.BlockSpec((3, C), lambda i: (0, 0)),
                   pl.BlockSpec((1, C), lambda i: (0, 0))],
        out_shape=[jax.ShapeDtypeStruct((L, C), BF16), jax.ShapeDtypeStruct((3, C), F32),
                   jax.ShapeDtypeStruct((1, C), F32)],
        compiler_params=_params(("arbitrary",)),
    )(up, u, u, da, da, w)


def _pool_window(ext, tile_rows, first_row, lead):
    T = ext.shape[0]
    sh = (lambda x, k: pltpu.roll(x, T - k, 0)) if lead else (lambda x, k: pltpu.roll(x, k, 0))
    r2 = ext + sh(ext, 1)
    r4 = r2 + sh(r2, 2)
    r8 = r4 + sh(r4, 4)
    r16 = r8 + sh(r8, 8)
    lo = 0 if lead else 16
    grp = lax.broadcasted_iota(jnp.int32, (tile_rows, POOL_W), 1) // POOL_GROUP
    pick = lambda a, b, c, d: jnp.where(grp == 0, a, jnp.where(grp == 1, b, jnp.where(grp == 2, c, d)))
    win = pick(r2[lo:lo + tile_rows], r4[lo:lo + tile_rows], r8[lo:lo + tile_rows], r16[lo:lo + tile_rows])
    return win, pick(2.0, 4.0, 8.0, 16.0)


def _pool_count(first_row, rows, wlen):
    t1 = (first_row + lax.broadcasted_iota(jnp.int32, (rows, POOL_W), 0) + 1).astype(F32)
    return jnp.minimum(t1, wlen)


def _cp_fwd(h, col, cat, wc, wblk, pscale, name):
    L = h.shape[0]
    ts = _tile(L, CP_TILE_CAP, 16)
    n = L // ts

    def body(h_ref, hp_ref, wc_ref, wb_ref, ps_ref, cat_in_ref, y_ref):
        i = pl.program_id(0)
        h = h_ref[...]
        hp = jnp.where(i > 0, hp_ref[...], 0.0)
        cb, cc, cv, pv = h[:, 0:256], h[:, 256:512], h[:, 512:768], h[:, 768:1024]
        p = cc * cv
        pp = hp[8:16, 256:512] * hp[8:16, 512:768]
        w = wc_ref[...]
        conv = w[2:3, :] * p + w[1:2, :] * _shift_down(p, pp, 1) + w[0:1, :] * _shift_down(p, pp, 2)
        y_conv = cb * conv
        ext = jnp.concatenate([hp[:, 768:1024], pv], axis=0)
        win, wlen = _pool_window(ext, ts, i * ts, False)
        d = win / _pool_count(i * ts, ts, wlen) - pv
        y_pool = _dnn(d.astype(BF16), wb_ref[...]) * ps_ref[...]
        y_ref[...] = jnp.concatenate([y_conv, y_pool], axis=1).astype(BF16)

    return pl.pallas_call(
        body, name=name, grid=(n,),
        in_specs=[pl.BlockSpec((ts, 1024), lambda i: (i, col)),
                  pl.BlockSpec((16, 1024), lambda i: (jnp.maximum(i * (ts // 16) - 1, 0), col)),
                  pl.BlockSpec((3, 256), lambda i: (0, 0)), pl.BlockSpec((256, 256), lambda i: (0, 0)),
                  pl.BlockSpec((1, 256), lambda i: (0, 0)), pl.BlockSpec(memory_space=pl.ANY)],
        out_specs=pl.BlockSpec((ts, 512), lambda i: (i, 1)),
        out_shape=jax.ShapeDtypeStruct(cat.shape, BF16),
        input_output_aliases={5: 0},
        compiler_params=_params(("parallel",)),
    )(h, h, wc, wblk, pscale, cat)


def _cp_bwd(h, col, dcat, dh, wc, wblk, pscale, name):
    L = h.shape[0]
    ts = _tile(L, CP_TILE_CAP, 16)
    n = L // ts
    last16 = L // 16 - 1

    def body(h_ref, hp_ref, hn_ref, dy_ref, dyn_ref, wc_ref, wb_ref, ps_ref, dh_in_ref,
             dh_ref, dwc_ref, dwb_ref, dps_ref):
        i = pl.program_id(0)

        @pl.when(i == 0)
        def _():
            dwc_ref[...] = jnp.zeros_like(dwc_ref)
            dwb_ref[...] = jnp.zeros_like(dwb_ref)
            dps_ref[...] = jnp.zeros_like(dps_ref)

        h = h_ref[...]
        hp = jnp.where(i > 0, hp_ref[...], 0.0)
        hn = hn_ref[...]
        dy = dy_ref[...].astype(F32)
        dyn = jnp.where(i < n - 1, dyn_ref[...].astype(F32), 0.0)
        cb, cc, cv, pv = h[:, 0:256], h[:, 256:512], h[:, 512:768], h[:, 768:1024]
        w = wc_ref[...]
        p = cc * cv
        pp = hp[8:16, 256:512] * hp[8:16, 512:768]
        p1 = _shift_down(p, pp, 1)
        p2 = _shift_down(p, pp, 2)
        conv = w[2:3, :] * p + w[1:2, :] * p1 + w[0:1, :] * p2
        dyc = dy[:, 0:256]
        dcb = dyc * conv
        dconv = dyc * cb
        dconv_n = dyn[0:8, 0:256] * hn[0:8, 0:256]
        dc1 = _shift_up(dconv, dconv_n, 1)
        dc2 = _shift_up(dconv, dconv_n, 2)
        dp = w[2:3, :] * dconv + w[1:2, :] * dc1 + w[0:1, :] * dc2
        dwc_ref[...] += jnp.concatenate([jnp.sum(p * dc2, axis=0, keepdims=True),
                                         jnp.sum(p * dc1, axis=0, keepdims=True),
                                         jnp.sum(p * dconv, axis=0, keepdims=True)], axis=0)
        ps = ps_ref[...]
        wb = wb_ref[...]
        ext = jnp.concatenate([hp[:, 768:1024], pv], axis=0)
        win, wlen = _pool_window(ext, ts, i * ts, False)
        d = win / _pool_count(i * ts, ts, wlen) - pv
        db = d.astype(BF16)
        dyp = dy[:, 256:512]
        dps_ref[...] += jnp.sum(dyp * _dnn(db, wb), axis=0, keepdims=True)
        dypre = (dyp * ps).astype(BF16)
        dwb_ref[...] += _dtn(db, dypre)
        dd = _dnt(dypre, wb)
        ddn = _dnt((dyn[:, 256:512] * ps).astype(BF16), wb)
        e = dd / _pool_count(i * ts, ts, wlen)
        en = ddn / _pool_count((i + 1) * ts, 16, wlen[0:16])
        lead, _ = _pool_window(jnp.concatenate([e, en], axis=0), ts, i * ts, True)
        dpv = lead - dd
        dh_ref[...] = jnp.concatenate([dcb, dp * cv, dp * cc, dpv], axis=1).astype(BF16)

    return pl.pallas_call(
        body, name=name, grid=(n,),
        in_specs=[pl.BlockSpec((ts, 1024), lambda i: (i, col)),
                  pl.BlockSpec((16, 1024), lambda i: (jnp.maximum(i * (ts // 16) - 1, 0), col)),
                  pl.BlockSpec((16, 1024), lambda i: (jnp.minimum((i + 1) * (ts // 16), last16), col)),
                  pl.BlockSpec((ts, 512), lambda i: (i, 1)),
                  pl.BlockSpec((16, 512), lambda i: (jnp.minimum((i + 1) * (ts // 16), last16), 1)),
                  pl.BlockSpec((3, 256), lambda i: (0, 0)), pl.BlockSpec((256, 256), lambda i: (0, 0)),
                  pl.BlockSpec((1, 256), lambda i: (0, 0)), pl.BlockSpec(memory_space=pl.ANY)],
        out_specs=[pl.BlockSpec((ts, 1024), lambda i: (i, col)), pl.BlockSpec((3, 256), lambda i: (0, 0)),
                   pl.BlockSpec((256, 256), lambda i: (0, 0)), pl.BlockSpec((1, 256), lambda i: (0, 0))],
        out_shape=[jax.ShapeDtypeStruct(dh.shape, BF16), jax.ShapeDtypeStruct((3, 256), F32),
                   jax.ShapeDtypeStruct((256, 256), F32), jax.ShapeDtypeStruct((1, 256), F32)],
        input_output_aliases={8: 0},
        compiler_params=_params(("arbitrary",)),
    )(h, h, h, dcat, dcat, wc, wblk, pscale, dh)


def _lower_bound(lb_ref, layer):
    b0, b1 = lb_ref[0:1, :], lb_ref[1:2, :]
    m = jnp.maximum(b0, b1)
    e0, e1 = jnp.exp(b0 - m), jnp.exp(b1 - m)
    p0, p1 = e0 / (e0 + e1), e1 / (e0 + e1)
    lb = (p0 - p0) if layer == 0 else ((p0 + p1) - p0)
    return lb, p0, p1


def _cumsum_rows(x, reverse=False):
    row = lax.broadcasted_iota(jnp.int32, x.shape, 0)
    for sh in (1, 2, 4, 8):
        if reverse:
            x = x + jnp.where(row < SUB - sh, pltpu.roll(x, SUB - sh, 0), 0.0)
        else:
            x = x + jnp.where(row >= sh, pltpu.roll(x, sh, 0), 0.0)
    return x


def _gates(fz, lb):
    sig = _sigmoid(fz)
    f = lb + (1.0 - lb) * sig
    g = jnp.log(jnp.maximum(f, F_FLOOR))
    k = (1.0 - lb) * (1.0 - sig)
    return sig, f, g, k


def _head(h):
    return slice(h * HG_D, (h + 1) * HG_D)


def _hgrn_fwd(hh, lbp, gnorm, layer, name):
    L = hh.shape[0]
    ts = SEQ_TILE
    n = L // ts
    nsub = ts // SUB

    def body(q_ref, f_ref, i_ref, g_ref, lb_ref, gn_ref, y_ref, o_ref, s_ref, a_ref, St):
        @pl.when(pl.program_id(0) == 0)
        def _():
            St[...] = jnp.zeros_like(St)

        lb, _, _ = _lower_bound(lb_ref, layer)
        gn = jnp.tile(gn_ref[...], (1, HG_HEADS))
        r16 = lax.broadcasted_iota(jnp.int32, (SUB, SUB), 0)
        c16 = lax.broadcasted_iota(jnp.int32, (SUB, SUB), 1)

        def block(j, carry):
            rows = pl.ds(pl.multiple_of(j * SUB, SUB), SUB)
            q = q_ref[rows, :] * Q_SCALE
            iv = i_ref[rows, :]
            gz = g_ref[rows, :]
            _, _, g, k = _gates(f_ref[rows, :], lb)
            G = _cumsum_rows(g)
            Gl = G[SUB - 1:SUB, :]
            qt = (q * jnp.exp(G)).astype(BF16)
            kd = (k * jnp.exp(Gl - G)).astype(BF16)
            eGl = jnp.exp(Gl)
            ib = iv.astype(BF16)
            A = [jnp.zeros((SUB, SUB), F32) for _ in range(HG_HEADS)]
            for s in range(SUB):
                P = q * jnp.exp(jnp.minimum(G - G[s:s + 1, :], 0.0)) * k[s:s + 1, :]
                for h in range(HG_HEADS):
                    A[h] = jnp.where(c16 == s, jnp.sum(P[:, _head(h)], axis=-1, keepdims=True), A[h])
            outs, ons, amats = [], [], []
            for h in range(HG_HEADS):
                sl = _head(h)
                Sb = St[h].astype(BF16)
                s_ref[j, sl, :] = Sb
                Am = jnp.where(r16 >= c16, A[h], 0.0)
                amats.append(Am)
                o = _dnt(qt[:, sl], Sb) + _dnn(Am.astype(BF16), ib[:, sl])
                St[h] = eGl[:, sl] * St[h] + _dtn(ib[:, sl], kd[:, sl])
                outs.append(o)
                ons.append(o * lax.rsqrt(jnp.mean(o * o, axis=-1, keepdims=True) + RMS_EPS))
            a_ref[rows, :] = jnp.concatenate(amats, axis=1)
            o_ref[rows, :] = jnp.concatenate(outs, axis=1)
            y = jnp.concatenate(ons, axis=1) * gn * (gz * _sigmoid(gz))
            y_ref[rows, :] = y.astype(BF16)
            return carry

        lax.fori_loop(0, nsub, block, 0, unroll=2)

    col = lambda c: pl.BlockSpec((ts, HG_W), lambda i: (i, c))
    return pl.pallas_call(
        body, name=name, grid=(n,),
        in_specs=[col(0), col(1), col(2), col(3), pl.BlockSpec((2, HG_W), lambda i: (0, 0)),
                  pl.BlockSpec((1, HG_D), lambda i: (0, 0))],
        out_specs=[pl.BlockSpec((ts, HG_W), lambda i: (i, 0)), pl.BlockSpec((ts, HG_W), lambda i: (i, 0)),
                   pl.BlockSpec((nsub, HG_W, HG_D), lambda i: (i, 0, 0)),
                   pl.BlockSpec((ts, HG_HEADS * SUB), lambda i: (i, 0))],
        out_shape=[jax.ShapeDtypeStruct((L, 2 * HG_W), BF16), jax.ShapeDtypeStruct((L, HG_W), F32),
                   jax.ShapeDtypeStruct((L // SUB, HG_W, HG_D), BF16),
                   jax.ShapeDtypeStruct((L, HG_HEADS * SUB), F32)],
        scratch_shapes=[pltpu.VMEM((HG_HEADS, HG_D, HG_D), F32)],
        compiler_params=_params(("arbitrary",)),
    )(hh, hh, hh, hh, lbp, gnorm)


def _hgrn_bwd(hh, o_raw, states, amat, dcat, lbp, gnorm, layer, name):
    L = hh.shape[0]
    ts = _tile(L, CP_TILE_CAP, SUB)
    n = L // ts
    nsub = ts // SUB

    def body(q_ref, f_ref, i_ref, g_ref, o_ref, s_ref, a_ref, dy_ref, lb_ref, gn_ref,
             dh_ref, dlb_ref, dgn_ref, dSt, dlb_acc, S_next):
        step = pl.program_id(0)

        @pl.when(step == 0)
        def _():
            dSt[...] = jnp.zeros_like(dSt)
            S_next[...] = jnp.zeros_like(S_next)
            dlb_acc[...] = jnp.zeros_like(dlb_acc)
            dgn_ref[...] = jnp.zeros_like(dgn_ref)

        lb, p0, p1 = _lower_bound(lb_ref, layer)
        gnh = gn_ref[...]
        gn = jnp.tile(gnh, (1, HG_HEADS))
        r16 = lax.broadcasted_iota(jnp.int32, (SUB, SUB), 0)
        c16 = lax.broadcasted_iota(jnp.int32, (SUB, SUB), 1)

        def block(jj, carry):
            j = nsub - 1 - jj
            rows = pl.ds(pl.multiple_of(j * SUB, SUB), SUB)
            q = q_ref[rows, :] * Q_SCALE
            iv = i_ref[rows, :]
            gz = g_ref[rows, :]
            o = o_ref[rows, :]
            dy = dy_ref[rows, :].astype(F32)
            sig, f, g, k = _gates(f_ref[rows, :], lb)
            G = _cumsum_rows(g)
            Gl = G[SUB - 1:SUB, :]
            eG = jnp.exp(G)
            edl = jnp.exp(Gl - G)
            eGl = jnp.exp(Gl)
            qt = (q * eG).astype(BF16)
            kd = (k * edl).astype(BF16)
            ib = iv.astype(BF16)
            sgz = _sigmoid(gz)
            sil = gz * sgz
            dyn = dy * sil
            on_parts, do_parts = [], []
            dgn = jnp.zeros((1, HG_D), F32)
            for h in range(HG_HEADS):
                sl = _head(h)
                oh = o[:, sl]
                rs = lax.rsqrt(jnp.mean(oh * oh, axis=-1, keepdims=True) + RMS_EPS)
                on = oh * rs
                dgn = dgn + jnp.sum(dyn[:, sl] * on, axis=0, keepdims=True)
                don = dyn[:, sl] * gnh
                do_parts.append(rs * (don - on * jnp.mean(don * on, axis=-1, keepdims=True)))
                on_parts.append(on)
            dgn_ref[...] += dgn
            on_all = jnp.concatenate(on_parts, axis=1)
            dgz = dy * on_all * gn * (sgz * (1.0 + gz * (1.0 - sgz)))
            do = jnp.concatenate(do_parts, axis=1)
            dob = do.astype(BF16)
            amat = a_ref[rows, :]
            dq_p, dk_p, di_p, tail_p = [], [], [], []
            for h in range(HG_HEADS):
                sl = _head(h)
                qh, kh, Gh = q[:, sl], k[:, sl], G[:, sl]
                Ap = jnp.where(r16 >= c16, _dnt(dob[:, sl], ib[:, sl]), 0.0)
                ApT = jnp.where(r16 <= c16, _dnt(ib[:, sl], dob[:, sl]), 0.0)
                dqh = jnp.zeros((SUB, HG_D), F32)
                dkh = jnp.zeros((SUB, HG_D), F32)
                for s in range(SUB):
                    dGs = Gh - Gh[s:s + 1, :]
                    e = jnp.exp(jnp.minimum(dGs, -dGs))
                    dqh = dqh + Ap[:, s:s + 1] * (e * kh[s:s + 1, :])
                    dkh = dkh + ApT[:, s:s + 1] * (e * qh[s:s + 1, :])
                Sb = s_ref[j, sl, :]
                dSb = dSt[h].astype(BF16)
                Am = amat[:, h * SUB:(h + 1) * SUB].astype(BF16)
                dq_p.append(dqh + eG[:, sl] * _dnn(dob[:, sl], Sb))
                dk_p.append(dkh + edl[:, sl] * _dnn(ib[:, sl], dSb))
                di_p.append(_dtn(Am, dob[:, sl]) + _dnt(kd[:, sl], dSb))
                tail_p.append(jnp.sum(dSt[h] * S_next[h].astype(F32), axis=0, keepdims=True))
                S_next[h] = Sb
                dSt[h] = eGl[:, sl] * dSt[h] + _dtn(dob[:, sl], qt[:, sl])
            dq = jnp.concatenate(dq_p, axis=1)
            dk = jnp.concatenate(dk_p, axis=1)
            di = jnp.concatenate(di_p, axis=1)
            dg = _cumsum_rows(q * dq - k * dk, reverse=True) + jnp.concatenate(tail_p, axis=1)
            df = jnp.where(f > F_FLOOR, dg / f, 0.0)
            dfk = df - dk
            dfz = (1.0 - lb) * dfk * sig * (1.0 - sig)
            dlb_acc[...] += jnp.sum(dfk * (1.0 - sig), axis=0, keepdims=True)
            dh_ref[rows, :] = jnp.concatenate([dq * Q_SCALE, dfz, di, dgz], axis=1).astype(BF16)
            return carry

        lax.fori_loop(0, nsub, block, 0)

        @pl.when(step == n - 1)
        def _():
            if layer == 0:
                dlb_ref[...] = jnp.zeros_like(dlb_ref)
            else:
                dz1 = p0 * p1 * dlb_acc[...]
                dlb_ref[...] = jnp.concatenate([-dz1, dz1], axis=0)

    rev = lambda i: n - 1 - i
    col = lambda c: pl.BlockSpec((ts, HG_W), lambda i: (rev(i), c))
    return pl.pallas_call(
        body, name=name, grid=(n,),
        in_specs=[col(0), col(1), col(2), col(3), col(0),
                  pl.BlockSpec((nsub, HG_W, HG_D), lambda i: (rev(i), 0, 0)),
                  pl.BlockSpec((ts, HG_HEADS * SUB), lambda i: (rev(i), 0)), col(0),
                  pl.BlockSpec((2, HG_W), lambda i: (0, 0)), pl.BlockSpec((1, HG_D), lambda i: (0, 0))],
        out_specs=[pl.BlockSpec((ts, 4 * HG_W), lambda i: (rev(i), 0)),
                   pl.BlockSpec((2, HG_W), lambda i: (0, 0)), pl.BlockSpec((1, HG_D), lambda i: (0, 0))],
        out_shape=[jax.ShapeDtypeStruct((L, 4 * HG_W + 1024), BF16), jax.ShapeDtypeStruct((2, HG_W), F32),
                   jax.ShapeDtypeStruct((1, HG_D), F32)],
        scratch_shapes=[pltpu.VMEM((HG_HEADS, HG_D, HG_D), F32), pltpu.VMEM((1, HG_W), F32),
                        pltpu.VMEM((HG_HEADS, HG_D, HG_D), BF16)],
        compiler_params=_params(("arbitrary",)),
    )(hh, hh, hh, hh, o_raw, states, amat, dcat, lbp, gnorm)


def _adamw_body(gp_ref, w_ref, m_ref, v_ref, g_ref, d_ref, mo_ref, vo_ref):
    c1 = 1.0 - ADAM_B1 ** ADAM_STEP
    c2 = 1.0 - ADAM_B2 ** ADAM_STEP
    g = gp_ref[0].astype(F32)
    for k in range(1, N_DEV):
        g = g + gp_ref[k].astype(F32)
    mn = ADAM_B1 * m_ref[...] + (1.0 - ADAM_B1) * g
    vn = ADAM_B2 * v_ref[...] + (1.0 - ADAM_B2) * (g * g)
    m_hat = mn / c1
    v_hat = vn / c2
    g_ref[...] = g
    d_ref[...] = -ADAM_LR * (m_hat / (jnp.sqrt(v_hat) + ADAM_EPS) + ADAM_WD * w_ref[...])
    mo_ref[...] = mn
    vo_ref[...] = vn


def _adamw_layers(gparts, w, m, v, name):
    depth, R, C = w.shape
    tr = _tile(R, 256, 16)
    nr = R // tr

    def body(*refs):
        layer = pl.program_id(0)
        for d in range(depth):
            @pl.when(layer == d)
            def _(d=d):
                _adamw_body(refs[d], *refs[depth:])

    def parts_spec(d):
        return pl.BlockSpec((N_DEV, tr, C),
                            lambda l, i: (0, jnp.where(l == d, i, jnp.where(l < d, 0, nr - 1)), 0))

    blk = pl.BlockSpec((None, tr, C), lambda l, i: (l, i, 0))
    shp = jax.ShapeDtypeStruct((depth, R, C), F32)
    return pl.pallas_call(
        body, name=name, grid=(depth, nr),
        in_specs=[parts_spec(d) for d in range(depth)] + [blk, blk, blk],
        out_specs=[blk, blk, blk, blk], out_shape=[shp, shp, shp, shp],
        compiler_params=_params(("arbitrary", "arbitrary")),
    )(*gparts, w, m, v)


def _adamw(gparts, w, m, v, name):
    R = w.shape[0]
    tr = _tile(R, 1024, 16) if R % 16 == 0 else R

    def body(*refs):
        _adamw_body(*refs)

    row = pl.BlockSpec((tr, LANES), lambda i: (i, 0))
    shp = jax.ShapeDtypeStruct((R, LANES), F32)
    return pl.pallas_call(
        body, name=name, grid=(R // tr,),
        in_specs=[pl.BlockSpec((N_DEV, tr, LANES), lambda i: (0, i, 0)), row, row, row],
        out_specs=[row, row, row, row], out_shape=[shp, shp, shp, shp],
        compiler_params=_params(("parallel",)),
    )(gparts, w, m, v)


def _flip(coord, bit):
    return 1 - coord if bit else coord


def _gather_many(blocks, name):
    n = len(blocks)

    def body(*refs):
        x_refs, out_refs = refs[:n], refs[n:2 * n]
        send_sems, recv_sems, local_sems = refs[2 * n:]
        x, y, c = lax.axis_index("x"), lax.axis_index("y"), lax.axis_index("c")
        me, sibling = (x, y, c), (x, y, 1 - c)
        chips = [(1 - x, y), (x, 1 - y), (1 - x, 1 - y)]

        def slot(a, px, py, pc):
            return out_refs[a].at[4 * px + 2 * py + pc]

        def copy(a, k, blk, to, src=None):
            return pltpu.make_async_remote_copy(
                src_ref=slot(a, *blk) if src is None else src, dst_ref=slot(a, *blk),
                send_sem=send_sems.at[7 * a + k], recv_sem=recv_sems.at[7 * a + k],
                device_id=to, device_id_type=pl.DeviceIdType.MESH)

        mine = [pltpu.make_async_copy(x_refs[a], slot(a, *me), local_sems.at[a]) for a in range(n)]
        for cp in mine:
            cp.start()
        first = [copy(a, 0, me, sibling, src=x_refs[a]) for a in range(n)]
        for j, chip in enumerate(chips):
            first += [copy(a, 1 + j, me, (*chip, c), src=x_refs[a]) for a in range(n)]
        for cp in first:
            cp.start()
        passed = []
        for j, chip in enumerate(chips):
            for a in range(n):
                copy(a, 1 + j, (*chip, c), me).wait_recv()
                fwd = copy(a, 4 + j, (*chip, c), sibling)
                fwd.start()
                passed.append(fwd)
        for a in range(n):
            copy(a, 0, sibling, me).wait_recv()
        for j, chip in enumerate(chips):
            for a in range(n):
                copy(a, 4 + j, (*chip, 1 - c), me).wait_recv()
        for cp in first + passed:
            cp.wait_send()
        for cp in mine:
            cp.wait()

    hbm = pl.BlockSpec(memory_space=pl.ANY)
    return pl.pallas_call(
        body, name=name,
        out_shape=[jax.ShapeDtypeStruct((N_DEV,) + b.shape, b.dtype) for b in blocks],
        in_specs=[hbm] * n, out_specs=[hbm] * n,
        scratch_shapes=[pltpu.SemaphoreType.DMA((7 * n,)), pltpu.SemaphoreType.DMA((7 * n,)),
                        pltpu.SemaphoreType.DMA((n,))],
    )(*blocks)


def _split_start(blocks, chunked, name):
    n = len(blocks)
    lands = [lax.empty(b.shape if chunked else (N_DEV,) + b.shape, b.dtype) for b in blocks]

    def body(*refs):
        x_refs, land_refs = refs[:n], refs[n:2 * n]
        send_sems, recv_sems, token = refs[2 * n], refs[2 * n + 1], refs[-1]
        x, y, c = lax.axis_index("x"), lax.axis_index("y"), lax.axis_index("c")
        me = 4 * x + 2 * y + c
        for a in range(n):
            for k in range(1, N_DEV):
                px, py, pc = _flip(x, k & 4), _flip(y, k & 2), _flip(c, k & 1)
                pltpu.make_async_remote_copy(
                    src_ref=x_refs[a].at[4 * px + 2 * py + pc] if chunked else x_refs[a],
                    dst_ref=land_refs[a].at[me],
                    send_sem=send_sems.at[7 * a + k - 1], recv_sem=recv_sems.at[7 * a + k - 1],
                    device_id=(px, py, pc), device_id_type=pl.DeviceIdType.MESH).start()
        token[...] = jnp.zeros_like(token)

    hbm = pl.BlockSpec(memory_space=pltpu.HBM)
    sem = pl.BlockSpec(memory_space=pltpu.SEMAPHORE)
    outs = pl.pallas_call(
        body, name=name,
        out_shape=(pltpu.SemaphoreType.DMA((7 * n,)), pltpu.SemaphoreType.DMA((7 * n,)),
                   *[pltpu.HBM(b.shape, b.dtype) for b in blocks], *[pltpu.HBM(l.shape, l.dtype) for l in lands],
                   jax.ShapeDtypeStruct((8, LANES), F32)),
        in_specs=[hbm] * (2 * n),
        out_specs=(sem, sem, *[hbm] * (2 * n), pl.BlockSpec(memory_space=pltpu.VMEM)),
        input_output_aliases={i: 2 + i for i in range(2 * n)},
        compiler_params=pltpu.CompilerParams(has_side_effects=pltpu.SideEffectType.DATAFLOW_SIDE_EFFECTING),
    )(*[pltpu.with_memory_space_constraint(b, pltpu.HBM) for b in blocks],
      *[pltpu.with_memory_space_constraint(l, pltpu.HBM) for l in lands])
    return outs[0], outs[1], list(outs[2:2 + n]), list(outs[2 + n:2 + 2 * n]), outs[-1]


def _split_wait(started, chunked, after, name):
    send_sems, recv_sems, blocks, lands, _ = started
    n = len(blocks)

    def body(*refs):
        x_refs, land_refs = refs[:n], refs[n:2 * n]
        send_sems, recv_sems = refs[2 * n], refs[2 * n + 1]
        x, y, c = lax.axis_index("x"), lax.axis_index("y"), lax.axis_index("c")
        for a in range(n):
            for k in range(1, N_DEV):
                px, py, pc = _flip(x, k & 4), _flip(y, k & 2), _flip(c, k & 1)
                copy = pltpu.make_async_remote_copy(
                    src_ref=x_refs[a].at[4 * px + 2 * py + pc] if chunked else x_refs[a],
                    dst_ref=land_refs[a].at[4 * px + 2 * py + pc],
                    send_sem=send_sems.at[7 * a + k - 1], recv_sem=recv_sems.at[7 * a + k - 1],
                    device_id=(px, py, pc), device_id_type=pl.DeviceIdType.MESH)
                copy.wait_send()
                copy.wait_recv()

    hbm = pl.BlockSpec(memory_space=pltpu.HBM)
    sem = pl.BlockSpec(memory_space=pltpu.SEMAPHORE)
    outs = pl.pallas_call(
        body, name=name,
        out_shape=(*[pltpu.HBM(b.shape, b.dtype) for b in blocks], *[pltpu.HBM(l.shape, l.dtype) for l in lands]),
        in_specs=[hbm] * (2 * n) + [sem, sem, pl.BlockSpec(memory_space=pl.ANY)],
        out_specs=[hbm] * (2 * n),
        input_output_aliases={i: i for i in range(2 * n)},
        compiler_params=pltpu.CompilerParams(has_side_effects=pltpu.SideEffectType.DATAFLOW_SIDE_EFFECTING),
    )(*blocks, *lands, send_sems, recv_sems, after)
    me = 4 * lax.axis_index("x") + 2 * lax.axis_index("y") + lax.axis_index("c")
    own = [lax.dynamic_index_in_dim(b, me, 0, keepdims=False) if chunked else b for b in outs[:n]]
    return [lax.dynamic_update_index_in_dim(z, o, me, 0) for z, o in zip(outs[n:], own)]


def _exchange_grads(layer_chunks, small_chunks, rep_block, name):
    flows, inputs = [], []
    for p, per_layer in enumerate(layer_chunks):
        for l, arr in enumerate(per_layer):
            flows.append(("param", p, l))
            inputs.append(arr)
    flows += [("small",), ("rep",)]
    inputs += [small_chunks, rep_block]
    n_par = len(layer_chunks)
    n_in, n_out, nf = len(inputs), n_par + 2, len(flows)

    def body(*refs):
        in_refs, out_refs = refs[:n_in], refs[n_in:n_in + n_out]
        send_sems, recv_sems, local_sems = refs[n_in + n_out:]
        x, y, c = lax.axis_index("x"), lax.axis_index("y"), lax.axis_index("c")
        me = 4 * x + 2 * y + c

        def src(f, dev):
            return in_refs[f] if flows[f][0] == "rep" else in_refs[f].at[dev]

        def dst(f, dev):
            if flows[f][0] == "param":
                _, p, l = flows[f]
                return out_refs[p].at[dev, l]
            return out_refs[n_par + (0 if flows[f][0] == "small" else 1)].at[dev]

        mine = [pltpu.make_async_copy(src(f, me), dst(f, me), local_sems.at[f]) for f in range(nf)]
        for cp in mine:
            cp.start()
        copies = []
        for k in range(1, N_DEV):
            px, py, pc = _flip(x, k & 4), _flip(y, k & 2), _flip(c, k & 1)
            peer = 4 * px + 2 * py + pc
            for f in range(nf):
                sems = dict(send_sem=send_sems.at[7 * f + k - 1], recv_sem=recv_sems.at[7 * f + k - 1],
                            device_id=(px, py, pc), device_id_type=pl.DeviceIdType.MESH)
                send = pltpu.make_async_remote_copy(src_ref=src(f, peer), dst_ref=dst(f, me), **sems)
                recv = pltpu.make_async_remote_copy(src_ref=src(f, peer), dst_ref=dst(f, peer), **sems)
                send.start()
                copies.append((send, recv))
        for send, recv in copies:
            recv.wait_recv()
        for send, recv in copies:
            send.wait_send()
        for cp in mine:
            cp.wait()

    out_shape = [jax.ShapeDtypeStruct((N_DEV, len(pl_)) + pl_[0].shape[1:], pl_[0].dtype) for pl_ in layer_chunks]
    out_shape += [jax.ShapeDtypeStruct(small_chunks.shape, small_chunks.dtype),
                  jax.ShapeDtypeStruct((N_DEV,) + rep_block.shape, rep_block.dtype)]
    hbm = pl.BlockSpec(memory_space=pl.ANY)
    return pl.pallas_call(
        body, name=name, out_shape=out_shape,
        in_specs=[hbm] * n_in, out_specs=[hbm] * n_out,
        scratch_shapes=[pltpu.SemaphoreType.DMA((7 * nf,)), pltpu.SemaphoreType.DMA((7 * nf,)),
                        pltpu.SemaphoreType.DMA((nf,))],
    )(*inputs)


def _pack_rows(size):
    return -(-size // (8 * LANES)) * 8


def _pack(arrs, dtype):
    parts, offs, r = [], [], 0
    for a in arrs:
        flat = a.astype(dtype).reshape(-1)
        nrow = _pack_rows(flat.shape[0])
        flat = jnp.pad(flat, (0, nrow * LANES - flat.shape[0]))
        parts.append(flat.reshape(nrow, LANES))
        offs.append((r, nrow))
        r += nrow
    return jnp.concatenate(parts, axis=0), offs


def _unpack(buf, offs, shapes, lead=()):
    outs = []
    for (r, nrow), shp in zip(offs, shapes):
        size = 1
        for s in shp:
            size *= s
        flat = buf[..., r:r + nrow, :].reshape(lead + (nrow * LANES,))
        outs.append(flat[..., :size].reshape(lead + tuple(shp)))
    return outs


def _cols_from_shards(g, axis):
    return jnp.concatenate([g[j] for j in range(N_DEV)], axis=axis)


BIG = ("w_in", "w_o", "w_up", "w_down")
SMALL_SHARDED = ("meta_tokens", "w_conv", "w_ffn_conv")
REPLICATED = ("hg_lower_bounds", "w_pool", "pool_scale", "hg_norm_g", "ln1_g", "ln1_b", "b_ffn_conv", "ln2_g", "ln2_b")
WEIGHTS = ("meta_tokens", "hg_lower_bounds", "w_in", "w_conv", "w_pool", "pool_scale", "hg_norm_g", "w_o",
           "ln1_g", "ln1_b", "w_up", "w_ffn_conv", "b_ffn_conv", "w_down", "ln2_g", "ln2_b")


def _pool_blockdiag(w_pool_l):
    z = jnp.zeros((POOL_GROUP, POOL_GROUP), w_pool_l.dtype)
    rows = [jnp.concatenate([w_pool_l[g] if h == g else z for h in range(4)], axis=1) for g in range(4)]
    return jnp.concatenate(rows, axis=0)


def _in_weights(g_in):
    w_in = jnp.transpose(g_in, (1, 0, 2)).reshape(D_MODEL, -1)
    return dict(w_hc=jnp.concatenate([w_in[:, 768:2816], w_in[:, 0:768], w_in[:, 2816:3072]], axis=1))


def _rest_weights(g_o, g_up, g_down):
    w_o = g_o.reshape(-1, D_MODEL)
    return dict(w_o=jnp.concatenate([w_o[256:768], w_o[0:256], w_o[768:1024]], axis=0),
                w_up=jnp.transpose(g_up, (1, 0, 2)).reshape(D_MODEL, -1), w_down=g_down.reshape(-1, D_MODEL))


def kernel(x, meta_tokens, hg_lower_bounds, w_in, w_conv, w_pool, pool_scale, hg_norm_g, w_o, ln1_g, ln1_b, w_up, w_ffn_conv, b_ffn_conv, w_down, ln2_g, ln2_b, loss_target, m_meta_tokens, m_hg_lower_bounds, m_w_in, m_w_conv, m_w_pool, m_pool_scale, m_hg_norm_g, m_w_o, m_ln1_g, m_ln1_b, m_w_up, m_w_ffn_conv, m_b_ffn_conv, m_w_down, m_ln2_g, m_ln2_b, v_meta_tokens, v_hg_lower_bounds, v_w_in, v_w_conv, v_w_pool, v_pool_scale, v_hg_norm_g, v_w_o, v_ln1_g, v_ln1_b, v_w_up, v_w_ffn_conv, v_b_ffn_conv, v_w_down, v_ln2_g, v_ln2_b):
    W = dict(meta_tokens=meta_tokens, hg_lower_bounds=hg_lower_bounds, w_in=w_in, w_conv=w_conv, w_pool=w_pool,
             pool_scale=pool_scale, hg_norm_g=hg_norm_g, w_o=w_o, ln1_g=ln1_g, ln1_b=ln1_b, w_up=w_up,
             w_ffn_conv=w_ffn_conv, b_ffn_conv=b_ffn_conv, w_down=w_down, ln2_g=ln2_g, ln2_b=ln2_b)
    M = dict(meta_tokens=m_meta_tokens, hg_lower_bounds=m_hg_lower_bounds, w_in=m_w_in, w_conv=m_w_conv,
             w_pool=m_w_pool, pool_scale=m_pool_scale, hg_norm_g=m_hg_norm_g, w_o=m_w_o, ln1_g=m_ln1_g,
             ln1_b=m_ln1_b, w_up=m_w_up, w_ffn_conv=m_w_ffn_conv, b_ffn_conv=m_b_ffn_conv, w_down=m_w_down,
             ln2_g=m_ln2_g, ln2_b=m_ln2_b)
    V = dict(meta_tokens=v_meta_tokens, hg_lower_bounds=v_hg_lower_bounds, w_in=v_w_in, w_conv=v_w_conv,
             w_pool=v_w_pool, pool_scale=v_pool_scale, hg_norm_g=v_hg_norm_g, w_o=v_w_o, ln1_g=v_ln1_g,
             ln1_b=v_ln1_b, w_up=v_w_up, w_ffn_conv=v_w_ffn_conv, b_ffn_conv=v_b_ffn_conv, w_down=v_w_down,
             ln2_g=v_ln2_g, ln2_b=v_ln2_b)
    assert x.shape[0] == 1 and x.shape[2] == D_MODEL and w_in.shape[0] == DEPTH
    seq = x.shape[1]
    L = -(-(seq + N_META) // ROW_ALIGN) * ROW_ALIGN

    small_pack, small_offs = _pack([W[n] for n in SMALL_SHARDED], F32)
    shards = {n: [W[n][l].astype(BF16) for l in range(DEPTH)] for n in BIG}
    g_in0, small_all = _gather_many([shards["w_in"][0], small_pack], "gather_weights")
    full = {}
    for n, a in zip(SMALL_SHARDED, _unpack(small_all, small_offs, [W[n].shape for n in SMALL_SHARDED], (N_DEV,))):
        full[n] = _cols_from_shards(a, 1)
    order = (small_all[0, 0, 0] * 0.0).astype(BF16)
    rest0_started = _split_start([shards[n][0] + order for n in BIG[1:]], False, "gather_rest0_start")
    order = rest0_started[4][0, 0].astype(BF16)
    layer1_started = _split_start([shards[n][1] + order for n in BIG], False, "gather_layer1_start")
    lb_in = hg_lower_bounds + layer1_started[4][0, 0]

    pad_rows = L - N_META - seq
    xp = jnp.concatenate([full["meta_tokens"], x[0], jnp.zeros((pad_rows, D_MODEL), F32)], axis=0)
    tgt = jnp.concatenate([jnp.zeros((N_META, D_MODEL), F32), loss_target[0], jnp.zeros((pad_rows, D_MODEL), F32)], axis=0)

    saved = []
    h_in, h_in_b = xp, xp.astype(BF16)
    for l in range(DEPTH):
        if l == 0:
            lw = _in_weights(g_in0)
        else:
            g_in, *g_rest = _split_wait(layer1_started, False, h_in_b, "gather_layer1_wait")
            lw = {**_in_weights(g_in), **_rest_weights(*g_rest)}
        wc = full["w_conv"][l].T
        wblk = _pool_blockdiag(w_pool[l]).astype(BF16)
        ps = pool_scale[l][None, :]
        gn = hg_norm_g[l][None, :]
        wf = full["w_ffn_conv"][l].T
        bf = b_ffn_conv[l][None, :]
        h = _matmul(h_in_b, lw["w_hc"], "nn", F32, f"fwd_in_{l}")
        y_hg, o_raw, states, amat = _hgrn_fwd(h, lb_in if l == 0 else hg_lower_bounds, gn, l, f"hgrn_fwd_{l}")
        cat = _cp_fwd(h, 2, y_hg, wc, wblk, ps, f"convpool_fwd_{l}")
        if l == 0:
            lw.update(_rest_weights(*_split_wait(rest0_started, False, cat, "gather_rest0_wait")))
        z1, x1, x1_b = _matmul_ln(cat, lw["w_o"], h_in, ln1_g[l][None, :], ln1_b[l][None, :], f"fwd_o_ln1_{l}")
        up = _matmul(x1_b, lw["w_up"], "nn", BF16, f"fwd_up_{l}")
        a, u = _ffn_act_fwd(up, wf, bf, f"ffn_fwd_{l}")
        saved.append(dict(lw=lw, wc=wc, wblk=wblk, ps=ps, gn=gn, wf=wf, bf=bf, x_b=h_in_b, h=h,
                          o_raw=o_raw, states=states, amat=amat, cat=cat, z1=z1, x1_b=x1_b, up=up, u=u, a=a))
        if l < DEPTH - 1:
            saved[l]["z2"], h_in, h_in_b = _matmul_ln(a, lw["w_down"], x1, ln2_g[l][None, :], ln2_b[l][None, :],
                                                     f"fwd_down_ln2_{l}")
        else:
            saved[l]["z2"], dy, loss_part = _matmul_ln(a, lw["w_down"], x1, ln2_g[l][None, :], ln2_b[l][None, :],
                                                       f"fwd_down_ln2_loss_{l}", loss=(tgt, seq))

    loss = lax.psum(loss_part[0, 0], ("x", "y", "c"))

    G = {}
    per_layer = {n: [None] * DEPTH for n in ("w_conv", "w_pool", "pool_scale", "hg_norm_g", "ln1_g", "ln1_b",
                                             "w_ffn_conv", "b_ffn_conv", "ln2_g", "ln2_b")}
    ffn_started, mix_started = [None] * DEPTH, [None] * DEPTH
    order = jnp.zeros((), F32)
    dlb_total = jnp.zeros((DEPTH, HG_W), F32)
    for l in reversed(range(DEPTH)):
        s = saved[l]
        lw = s["lw"]
        dz2_b, dg2, db2 = _ln_bwd(s["z2"], dy, ln2_g[l][None, :] + order, f"ln2_bwd_{l}")
        da = _matmul(dz2_b, lw["w_down"], "nt", BF16, f"bwd_da_{l}")
        d_w_down = _matmul(s["a"], dz2_b, "tn", BF16, f"wgrad_down_{l}")
        dup, dwf, dbf = _ffn_act_bwd(s["up"], s["u"], da, s["wf"], f"ffn_bwd_{l}")
        dx1 = _matmul(dup, lw["w_up"], "nt", BF16, f"bwd_dx1_{l}", res=dz2_b, alpha=ALPHA)
        d_w_up = _matmul(s["x1_b"], dup, "tn", BF16, f"wgrad_up_{l}")
        ffn_started[l] = _split_start([jnp.transpose(d_w_up.reshape(D_MODEL, N_DEV, -1), (1, 0, 2)),
                                       d_w_down.reshape(N_DEV, -1, D_MODEL)], True, f"scatter_ffn{l}_start")
        order = ffn_started[l][4][0, 0]
        dz1_b, dg1, db1 = _ln_bwd(s["z1"], dx1, ln1_g[l][None, :] + order, f"ln1_bwd_{l}")
        dcat = _matmul(dz1_b, lw["w_o"], "nt", BF16, f"bwd_dcat_{l}")
        d_w_o = _matmul(s["cat"], dz1_b, "tn", BF16, f"wgrad_o_{l}")
        dh, dlb, dgn = _hgrn_bwd(s["h"], s["o_raw"], s["states"], s["amat"], dcat, hg_lower_bounds, s["gn"], l,
                                 f"hgrn_bwd_{l}")
        dh, dwc, dwblk, dps = _cp_bwd(s["h"], 2, dcat, dh, s["wc"], s["wblk"], s["ps"], f"convpool_bwd_{l}")
        d_w_hc = _matmul(s["x_b"], dh, "tn", BF16, f"wgrad_in_{l}")
        d_w_in = jnp.concatenate([d_w_hc[:, 2048:2816], d_w_hc[:, 0:2048], d_w_hc[:, 2816:3072]], axis=1)
        mix_chunks = [jnp.transpose(d_w_in.reshape(D_MODEL, N_DEV, -1), (1, 0, 2)),
                      jnp.concatenate([d_w_o[512:768], d_w_o[0:512], d_w_o[768:1024]], axis=0).reshape(N_DEV, -1, D_MODEL)]
        mix_started[l] = _split_start(mix_chunks, True, f"scatter_mix{l}_start")
        order = mix_started[l][4][0, 0]
        dx = _matmul(dh, lw["w_hc"] + order.astype(BF16), "nt", F32, f"bwd_dx_{l}", res=dz1_b, alpha=ALPHA)
        per_layer["w_conv"][l] = dwc.T
        per_layer["w_ffn_conv"][l] = dwf.T
        per_layer["b_ffn_conv"][l] = dbf[0]
        per_layer["w_pool"][l] = jnp.stack([dwblk[g * 64:(g + 1) * 64, g * 64:(g + 1) * 64] for g in range(4)], axis=0)
        per_layer["pool_scale"][l] = dps[0]
        per_layer["hg_norm_g"][l] = dgn[0]
        per_layer["ln1_g"][l], per_layer["ln1_b"][l] = dg1[0], db1[0]
        per_layer["ln2_g"][l], per_layer["ln2_b"][l] = dg2[0], db2[0]
        dlb_total = dlb_total + dlb
        dy = dx
    for n, parts in per_layer.items():
        G[n] = jnp.stack(parts, axis=0)
    G["hg_lower_bounds"] = dlb_total
    grad_x = dy[N_META:N_META + seq][None]

    def shard_major(g, lead):
        g = g.reshape(g.shape[:lead] + (N_DEV, -1) + g.shape[lead + 1:])
        g = jnp.moveaxis(g, lead, 0).reshape(N_DEV, -1)
        nrow = _pack_rows(g.shape[1])
        return jnp.pad(g, ((0, 0), (0, nrow * LANES - g.shape[1]))).reshape(N_DEV, nrow, LANES)

    small_chunks = jnp.concatenate([shard_major(dy[0:N_META], 1), shard_major(G["w_conv"], 1),
                                    shard_major(G["w_ffn_conv"], 1)], axis=1)
    w_small, _ = _pack([W[n] for n in SMALL_SHARDED], F32)
    rep_pack, rep_offs = _pack([G[n] for n in REPLICATED], F32)
    small_recv, rep_all = _exchange_grads([], small_chunks, rep_pack, "exchange_grads")
    parts = {n: [] for n in BIG}
    for l in range(DEPTH):
        up_l, down_l = _split_wait(ffn_started[l], True, rep_all, f"scatter_ffn{l}_wait")
        in_l, o_l = _split_wait(mix_started[l], True, rep_all, f"scatter_mix{l}_wait")
        for n, a in zip(BIG, (in_l, o_l, up_l, down_l)):
            parts[n].append(a)

    res = {k: {} for k in ("grad", "delta", "new_m", "new_v")}
    kinds = ("grad", "delta", "new_m", "new_v")
    for n in BIG:
        for kind, a in zip(kinds, _adamw_layers(parts[n], W[n], M[n], V[n], f"adamw_{n}")):
            res[kind][n] = a
    m_small, _ = _pack([M[n] for n in SMALL_SHARDED], F32)
    v_small, _ = _pack([V[n] for n in SMALL_SHARDED], F32)
    outs_small = _adamw(small_recv, w_small, m_small, v_small, "adamw_small_sharded")
    w_rep, _ = _pack([W[n] for n in REPLICATED], F32)
    m_rep, _ = _pack([M[n] for n in REPLICATED], F32)
    v_rep, _ = _pack([V[n] for n in REPLICATED], F32)
    outs_rep = _adamw(rep_all, w_rep, m_rep, v_rep, "adamw_replicated")
    for kind, b_sm, b_rep in zip(kinds, outs_small, outs_rep):
        for n, a in zip(SMALL_SHARDED, _unpack(b_sm, small_offs, [W[n].shape for n in SMALL_SHARDED])):
            res[kind][n] = a
        for n, a in zip(REPLICATED, _unpack(b_rep, rep_offs, [W[n].shape for n in REPLICATED])):
            res[kind][n] = a

    return (loss, grad_x, *[res["grad"][n] for n in WEIGHTS], *[res["delta"][n] for n in WEIGHTS],
            *[res["new_m"][n] for n in WEIGHTS], *[res["new_v"][n] for n in WEIGHTS])
```

```python
import jax
import jax.numpy as jnp
from jax import lax
from jax.experimental import pallas as pl
from jax.experimental.pallas import tpu as pltpu

F32 = jnp.float32
BF16 = jnp.bfloat16

N_DEV = 8
D_MODEL = 1024
N_META = 16
DEPTH = 2
CONV_W = 256
HG_W = 512
HG_D = 128
HG_HEADS = 4
POOL_W = 256
POOL_GROUP = 64
D_FF = 2816
ALPHA = (2 * DEPTH) ** 0.25
LN_EPS = 1e-5
RMS_EPS = 1e-6
F_FLOOR = 1e-30
Q_SCALE = HG_D ** -0.5
SUB = 16
SEQ_TILE = 192
FFN_TILE = 192
CP_TILE_CAP = 1536
ROW_ALIGN = 192
LANES = 128
VMEM_LIMIT = 48 * 1024 * 1024
MATMUL_VMEM_BUDGET = 38 * 1024 * 1024

ADAM_LR = 0.001
ADAM_B1 = 0.9
ADAM_B2 = 0.999
ADAM_EPS = 1e-08
ADAM_WD = 0.01
ADAM_STEP = 10


def _tile(n, cap, mult):
    best = 0
    for t in range(mult, min(n, cap) + 1, mult):
        if n % t == 0:
            best = t
    assert best > 0, (n, cap, mult)
    return best


def _params(sem, vmem=VMEM_LIMIT):
    return pltpu.CompilerParams(dimension_semantics=sem, vmem_limit_bytes=vmem)


def _dnt(a, b):
    return lax.dot_general(a, b, (((1,), (1,)), ((), ())), preferred_element_type=F32)


def _dtn(a, b):
    return lax.dot_general(a, b, (((0,), (0,)), ((), ())), preferred_element_type=F32)


def _dnn(a, b):
    return jnp.dot(a, b, preferred_element_type=F32)


def _sigmoid(x):
    return jax.nn.sigmoid(x)


def _matmul(a, b, mode, out_dtype, name, res=None, alpha=1.0):
    if mode == "tn":
        K, M = a.shape
    else:
        M, K = a.shape
    N = b.shape[0] if mode == "nt" else b.shape[1]
    out_bytes = jnp.dtype(out_dtype).itemsize
    if mode == "tn" and M % 512 == 0 and N % 512 == 0:
        tk, tm, tn = K, 512, 512
        nk, use_acc = 1, False
    else:
        tn = _tile(N, 1536, LANES)
        tk = _tile(K, 1536, 16) if mode == "tn" else _tile(K, 2816, LANES)
        nk = K // tk
        use_acc = nk > 1 and out_dtype != F32
        tm = M
        for cap in (1536, 768, 384):
            tm = _tile(M, cap, 16)
            blocks = 2 * (a.dtype.itemsize * tm * tk + b.dtype.itemsize * tn * tk + out_bytes * tm * tn
                          + (res.dtype.itemsize * tm * tn if res is not None else 0)) + (4 * tm * tn if use_acc else 0)
            if blocks <= MATMUL_VMEM_BUDGET:
                break
    dims = {"nn": ((1,), (0,)), "nt": ((1,), (1,)), "tn": ((0,), (0,))}[mode]

    def body(*refs):
        a_ref, b_ref = refs[0], refs[1]
        r_ref = refs[2] if res is not None else None
        o_ref = refs[3] if res is not None else refs[2]
        acc = refs[-1] if use_acc else o_ref
        k = pl.program_id(2)
        p = lax.dot_general(a_ref[...].astype(BF16), b_ref[...].astype(BF16), (dims, ((), ())),
                            preferred_element_type=F32)

        def finish(r):
            if r_ref is not None:
                r = r + alpha * r_ref[...].astype(F32)
            o_ref[...] = r.astype(out_dtype)

        if nk == 1:
            finish(p)
        else:
            @pl.when(k == 0)
            def _():
                acc[...] = p

            @pl.when((k > 0) & (k < nk - 1))
            def _():
                acc[...] += p

            @pl.when(k == nk - 1)
            def _():
                finish(acc[...] + p)

    if mode == "tn":
        a_spec = pl.BlockSpec((tk, tm), lambda i, j, k: (k, i))
    else:
        a_spec = pl.BlockSpec((tm, tk), lambda i, j, k: (i, k))
    if mode == "nt":
        b_spec = pl.BlockSpec((tn, tk), lambda i, j, k: (j, k))
    else:
        b_spec = pl.BlockSpec((tk, tn), lambda i, j, k: (k, j))
    in_specs = [a_spec, b_spec]
    args = [a, b]
    if res is not None:
        in_specs.append(pl.BlockSpec((tm, tn), lambda i, j, k: (i, j)))
        args.append(res)
    return pl.pallas_call(
        body, name=name,
        grid=(M // tm, N // tn, nk),
        in_specs=in_specs,
        out_specs=pl.BlockSpec((tm, tn), lambda i, j, k: (i, j)),
        out_shape=jax.ShapeDtypeStruct((M, N), out_dtype),
        scratch_shapes=[pltpu.VMEM((tm, tn), F32)] if use_acc else [],
        compiler_params=_params(("parallel", "parallel", "arbitrary")),
    )(*args)


def _matmul_ln(a, w, x, g, b, name, loss=None):
    L, K = a.shape
    D = w.shape[1]
    tr = L
    for cap in (1536, 768, 384):
        tr = _tile(L, cap, 16)
        if 2 * (2 * tr * K + 2 * K * D + 4 * tr * D * (4 if loss else 3) + 2 * tr * D) <= MATMUL_VMEM_BUDGET:
            break

    def body(*refs):
        a_ref, w_ref, x_ref, g_ref, b_ref = refs[:5]
        z = ALPHA * x_ref[...] + _dnn(a_ref[...], w_ref[...])
        mu = jnp.mean(z, axis=-1, keepdims=True)
        zc = z - mu
        var = jnp.mean(zc * zc, axis=-1, keepdims=True)
        y = zc * lax.rsqrt(var + LN_EPS) * g_ref[...] + b_ref[...]
        if loss is None:
            z_ref, y_ref, yb_ref = refs[5:]
            y_ref[...] = y
            yb_ref[...] = y.astype(BF16)
        else:
            t_ref, z_ref, dy_ref, loss_ref = refs[5:]
            i = pl.program_id(0)

            @pl.when(i == 0)
            def _():
                loss_ref[...] = jnp.zeros_like(loss_ref)

            r = i * tr + lax.broadcasted_iota(jnp.int32, (tr, D), 0)
            valid = (r >= N_META) & (r < N_META + loss[1])
            e = jnp.where(valid, y - t_ref[...], 0.0)
            dy_ref[...] = e * (1.0 / D)
            s = jnp.sum(jnp.sum(e * e, axis=-1, keepdims=True), axis=0, keepdims=True)
            loss_ref[...] += (0.5 / D) * s
        z_ref[...] = z.astype(BF16)

    row = pl.BlockSpec((tr, D), lambda i: (i, 0))
    vec = pl.BlockSpec((1, D), lambda i: (0, 0))
    in_specs = [pl.BlockSpec((tr, K), lambda i: (i, 0)), pl.BlockSpec((K, D), lambda i: (0, 0)), row, vec, vec]
    f32_rows = jax.ShapeDtypeStruct((L, D), F32)
    b16_rows = jax.ShapeDtypeStruct((L, D), BF16)
    if loss is None:
        args, out_specs = [a, w, x, g, b], [row, row, row]
        out_shape = [b16_rows, f32_rows, b16_rows]
    else:
        args, in_specs = [a, w, x, g, b, loss[0]], in_specs + [row]
        out_specs = [row, row, pl.BlockSpec((1, 1), lambda i: (0, 0))]
        out_shape = [b16_rows, f32_rows, jax.ShapeDtypeStruct((1, 1), F32)]
    return pl.pallas_call(
        body, name=name, grid=(L // tr,), in_specs=in_specs, out_specs=out_specs, out_shape=out_shape,
        compiler_params=_params(("arbitrary",) if loss else ("parallel",)),
    )(*args)


def _ln_bwd(z, dy, g, name):
    L, D = z.shape
    tr = _tile(L, 768, 16)

    def body(z_ref, dy_ref, g_ref, dzb_ref, dg_ref, db_ref):
        @pl.when(pl.program_id(0) == 0)
        def _():
            dg_ref[...] = jnp.zeros_like(dg_ref)
            db_ref[...] = jnp.zeros_like(db_ref)

        z = z_ref[...].astype(F32)
        mu = jnp.mean(z, axis=-1, keepdims=True)
        zc = z - mu
        var = jnp.mean(zc * zc, axis=-1, keepdims=True)
        rstd = lax.rsqrt(var + LN_EPS)
        xhat = zc * rstd
        dy = dy_ref[...].astype(F32)
        dxh = dy * g_ref[...]
        m1 = jnp.mean(dxh, axis=-1, keepdims=True)
        m2 = jnp.mean(dxh * xhat, axis=-1, keepdims=True)
        dz = rstd * (dxh - m1 - xhat * m2)
        dzb_ref[...] = dz.astype(BF16)
        dg_ref[...] += jnp.sum(dy * xhat, axis=0, keepdims=True)
        db_ref[...] += jnp.sum(dy, axis=0, keepdims=True)

    row = pl.BlockSpec((tr, D), lambda i: (i, 0))
    vec = pl.BlockSpec((1, D), lambda i: (0, 0))
    return pl.pallas_call(
        body, name=name, grid=(L // tr,),
        in_specs=[row, row, vec], out_specs=[row, vec, vec],
        out_shape=[jax.ShapeDtypeStruct((L, D), BF16),
                   jax.ShapeDtypeStruct((1, D), F32), jax.ShapeDtypeStruct((1, D), F32)],
        compiler_params=_params(("arbitrary",)),
    )(z, dy, g)


def _shift_down(x, prev, k):
    T, C = x.shape
    rot = pltpu.roll(jnp.concatenate([prev, x], axis=0).reshape(T // 8 + 1, 8, C), k, 1)
    sub = lax.broadcasted_iota(jnp.int32, (T // 8, 8, C), 1)
    return jnp.where(sub < k, rot[:-1], rot[1:]).reshape(T, C)


def _shift_up(x, nxt, k):
    T, C = x.shape
    rot = pltpu.roll(jnp.concatenate([x, nxt], axis=0).reshape(T // 8 + 1, 8, C), 8 - k, 1)
    sub = lax.broadcasted_iota(jnp.int32, (T // 8, 8, C), 1)
    return jnp.where(sub >= 8 - k, rot[1:], rot[:-1]).reshape(T, C)


def _conv3(x, prev, w, b):
    return w[2:3, :] * x + w[1:2, :] * _shift_down(x, prev, 1) + w[0:1, :] * _shift_down(x, prev, 2) + b


def _ffn_act_fwd(up, w, b, name):
    L, C = up.shape
    F = C // 2
    ts = FFN_TILE
    n = L // ts

    def body(up_ref, pv_ref, w_ref, b_ref, a_ref, u_ref):
        i = pl.program_id(0)
        x = up_ref[...].astype(F32)
        prev = jnp.where(i > 0, pv_ref[...].astype(F32)[8:16], 0.0)
        u = _conv3(x, prev, w_ref[...], b_ref[...])
        u_ref[...] = u.astype(BF16)
        gate = u[:, :F]
        a_ref[...] = (gate * _sigmoid(gate) * u[:, F:]).astype(BF16)

    return pl.pallas_call(
        body, name=name, grid=(n,),
        in_specs=[pl.BlockSpec((ts, C), lambda i: (i, 0)),
                  pl.BlockSpec((16, C), lambda i: (jnp.maximum(i * (ts // 16) - 1, 0), 0)),
                  pl.BlockSpec((3, C), lambda i: (0, 0)), pl.BlockSpec((1, C), lambda i: (0, 0))],
        out_specs=[pl.BlockSpec((ts, F), lambda i: (i, 0)), pl.BlockSpec((ts, C), lambda i: (i, 0))],
        out_shape=[jax.ShapeDtypeStruct((L, F), BF16), jax.ShapeDtypeStruct((L, C), BF16)],
        compiler_params=_params(("parallel",)),
    )(up, up, w, b)


def _ffn_act_bwd(up, u, da, w, name):
    L, C = up.shape
    F = C // 2
    ts = FFN_TILE
    n = L // ts
    last16 = L // 16 - 1

    def du_of(u, da):
        gate, val = u[:, :F], u[:, F:]
        sg = _sigmoid(gate)
        dgate = da * val * (sg * (1.0 + gate * (1.0 - sg)))
        dval = da * (gate * sg)
        return jnp.concatenate([dgate, dval], axis=1)

    def body(up_ref, u_ref, un_ref, da_ref, dan_ref, w_ref, dup_ref, dw_ref, db_ref):
        i = pl.program_id(0)

        @pl.when(i == 0)
        def _():
            dw_ref[...] = jnp.zeros_like(dw_ref)
            db_ref[...] = jnp.zeros_like(db_ref)

        w = w_ref[...]
        x = up_ref[...].astype(F32)
        du = du_of(u_ref[...].astype(F32), da_ref[...].astype(F32))
        dun = jnp.where(i < n - 1, du_of(un_ref[...].astype(F32)[0:8], dan_ref[...].astype(F32)[0:8]), 0.0)
        du1 = _shift_up(du, dun, 1)
        du2 = _shift_up(du, dun, 2)
        dup_ref[...] = (w[2:3, :] * du + w[1:2, :] * du1 + w[0:1, :] * du2).astype(BF16)
        dw_ref[...] += jnp.concatenate([jnp.sum(x * du2, axis=0, keepdims=True),
                                        jnp.sum(x * du1, axis=0, keepdims=True),
                                        jnp.sum(x * du, axis=0, keepdims=True)], axis=0)
        db_ref[...] += jnp.sum(du, axis=0, keepdims=True)

    nxt = lambda i: (jnp.minimum((i + 1) * (ts // 16), last16), 0)
    return pl.pallas_call(
        body, name=name, grid=(n,),
        in_specs=[pl.BlockSpec((ts, C), lambda i: (i, 0)),
                  pl.BlockSpec((ts, C), lambda i: (i, 0)), pl.BlockSpec((16, C), nxt),
                  pl.BlockSpec((ts, F), lambda i: (i, 0)), pl.BlockSpec((16, F), nxt),
                  pl.BlockSpec((3, C), lambda i: (0, 0))],
        out_specs=[pl.BlockSpec((ts, C), lambda i: (i, 0)), pl.BlockSpec((3, C), lambda i: (0, 0)),
                   pl.BlockSpec((1, C), lambda i: (0, 0))],
        out_shape=[jax.ShapeDtypeStruct((L, C), BF16), jax.ShapeDtypeStruct((3, C), F32),
                   jax.ShapeDtypeStruct((1, C), F32)],
        compiler_params=_params(("arbitrary",)),
    )(up, u, u, da, da, w)


def _pool_window(ext, tile_rows, first_row, lead):
    T = ext.shape[0]
    sh = (lambda x, k: pltpu.roll(x, T - k, 0)) if lead else (lambda x, k: pltpu.roll(x, k, 0))
    r2 = ext + sh(ext, 1)
    r4 = r2 + sh(r2, 2)
    r8 = r4 + sh(r4, 4)
    r16 = r8 + sh(r8, 8)
    lo = 0 if lead else 16
    grp = lax.broadcasted_iota(jnp.int32, (tile_rows, POOL_W), 1) // POOL_GROUP
    pick = lambda a, b, c, d: jnp.where(grp == 0, a, jnp.where(grp == 1, b, jnp.where(grp == 2, c, d)))
    win = pick(r2[lo:lo + tile_rows], r4[lo:lo + tile_rows], r8[lo:lo + tile_rows], r16[lo:lo + tile_rows])
    return win, pick(2.0, 4.0, 8.0, 16.0)


def _pool_count(first_row, rows, wlen):
    t1 = (first_row + lax.broadcasted_iota(jnp.int32, (rows, POOL_W), 0) + 1).astype(F32)
    return jnp.minimum(t1, wlen)


def _cp_fwd(h, col, cat, wc, wblk, pscale, name):
    L = h.shape[0]
    ts = _tile(L, CP_TILE_CAP, 16)
    n = L // ts

    def body(h_ref, hp_ref, wc_ref, wb_ref, ps_ref, cat_in_ref, y_ref):
        i = pl.program_id(0)
        h = h_ref[...]
        hp = jnp.where(i > 0, hp_ref[...], 0.0)
        cb, cc, cv, pv = h[:, 0:256], h[:, 256:512], h[:, 512:768], h[:, 768:1024]
        p = cc * cv
        pp = hp[8:16, 256:512] * hp[8:16, 512:768]
        w = wc_ref[...]
        conv = w[2:3, :] * p + w[1:2, :] * _shift_down(p, pp, 1) + w[0:1, :] * _shift_down(p, pp, 2)
        y_conv = cb * conv
        ext = jnp.concatenate([hp[:, 768:1024], pv], axis=0)
        win, wlen = _pool_window(ext, ts, i * ts, False)
        d = win / _pool_count(i * ts, ts, wlen) - pv
        y_pool = _dnn(d.astype(BF16), wb_ref[...]) * ps_ref[...]
        y_ref[...] = jnp.concatenate([y_conv, y_pool], axis=1).astype(BF16)

    return pl.pallas_call(
        body, name=name, grid=(n,),
        in_specs=[pl.BlockSpec((ts, 1024), lambda i: (i, col)),
                  pl.BlockSpec((16, 1024), lambda i: (jnp.maximum(i * (ts // 16) - 1, 0), col)),
                  pl.BlockSpec((3, 256), lambda i: (0, 0)), pl.BlockSpec((256, 256), lambda i: (0, 0)),
                  pl.BlockSpec((1, 256), lambda i: (0, 0)), pl.BlockSpec(memory_space=pl.ANY)],
        out_specs=pl.BlockSpec((ts, 512), lambda i: (i, 1)),
        out_shape=jax.ShapeDtypeStruct(cat.shape, BF16),
        input_output_aliases={5: 0},
        compiler_params=_params(("parallel",)),
    )(h, h, wc, wblk, pscale, cat)


def _cp_bwd(h, col, dcat, dh, wc, wblk, pscale, name):
    L = h.shape[0]
    ts = _tile(L, CP_TILE_CAP, 16)
    n = L // ts
    last16 = L // 16 - 1

    def body(h_ref, hp_ref, hn_ref, dy_ref, dyn_ref, wc_ref, wb_ref, ps_ref, dh_in_ref,
             dh_ref, dwc_ref, dwb_ref, dps_ref):
        i = pl.program_id(0)

        @pl.when(i == 0)
        def _():
            dwc_ref[...] = jnp.zeros_like(dwc_ref)
            dwb_ref[...] = jnp.zeros_like(dwb_ref)
            dps_ref[...] = jnp.zeros_like(dps_ref)

        h = h_ref[...]
        hp = jnp.where(i > 0, hp_ref[...], 0.0)
        hn = hn_ref[...]
        dy = dy_ref[...].astype(F32)
        dyn = jnp.where(i < n - 1, dyn_ref[...].astype(F32), 0.0)
        cb, cc, cv, pv = h[:, 0:256], h[:, 256:512], h[:, 512:768], h[:, 768:1024]
        w = wc_ref[...]
        p = cc * cv
        pp = hp[8:16, 256:512] * hp[8:16, 512:768]
        p1 = _shift_down(p, pp, 1)
        p2 = _shift_down(p, pp, 2)
        conv = w[2:3, :] * p + w[1:2, :] * p1 + w[0:1, :] * p2
        dyc = dy[:, 0:256]
        dcb = dyc * conv
        dconv = dyc * cb
        dconv_n = dyn[0:8, 0:256] * hn[0:8, 0:256]
        dc1 = _shift_up(dconv, dconv_n, 1)
        dc2 = _shift_up(dconv, dconv_n, 2)
        dp = w[2:3, :] * dconv + w[1:2, :] * dc1 + w[0:1, :] * dc2
        dwc_ref[...] += jnp.concatenate([jnp.sum(p * dc2, axis=0, keepdims=True),
                                         jnp.sum(p * dc1, axis=0, keepdims=True),
                                         jnp.sum(p * dconv, axis=0, keepdims=True)], axis=0)
        ps = ps_ref[...]
        wb = wb_ref[...]
        ext = jnp.concatenate([hp[:, 768:1024], pv], axis=0)
        win, wlen = _pool_window(ext, ts, i * ts, False)
        d = win / _pool_count(i * ts, ts, wlen) - pv
        db = d.astype(BF16)
        dyp = dy[:, 256:512]
        dps_ref[...] += jnp.sum(dyp * _dnn(db, wb), axis=0, keepdims=True)
        dypre = (dyp * ps).astype(BF16)
        dwb_ref[...] += _dtn(db, dypre)
        dd = _dnt(dypre, wb)
        ddn = _dnt((dyn[:, 256:512] * ps).astype(BF16), wb)
        e = dd / _pool_count(i * ts, ts, wlen)
        en = ddn / _pool_count((i + 1) * ts, 16, wlen[0:16])
        lead, _ = _pool_window(jnp.concatenate([e, en], axis=0), ts, i * ts, True)
        dpv = lead - dd
        dh_ref[...] = jnp.concatenate([dcb, dp * cv, dp * cc, dpv], axis=1).astype(BF16)

    return pl.pallas_call(
        body, name=name, grid=(n,),
        in_specs=[pl.BlockSpec((ts, 1024), lambda i: (i, col)),
                  pl.BlockSpec((16, 1024), lambda i: (jnp.maximum(i * (ts // 16) - 1, 0), col)),
                  pl.BlockSpec((16, 1024), lambda i: (jnp.minimum((i + 1) * (ts // 16), last16), col)),
                  pl.BlockSpec((ts, 512), lambda i: (i, 1)),
                  pl.BlockSpec((16, 512), lambda i: (jnp.minimum((i + 1) * (ts // 16), last16), 1)),
                  pl.BlockSpec((3, 256), lambda i: (0, 0)), pl.BlockSpec((256, 256), lambda i: (0, 0)),
                  pl.BlockSpec((1, 256), lambda i: (0, 0)), pl.BlockSpec(memory_space=pl.ANY)],
        out_specs=[pl.BlockSpec((ts, 1024), lambda i: (i, col)), pl.BlockSpec((3, 256), lambda i: (0, 0)),
                   pl.BlockSpec((256, 256), lambda i: (0, 0)), pl.BlockSpec((1, 256), lambda i: (0, 0))],
        out_shape=[jax.ShapeDtypeStruct(dh.shape, BF16), jax.ShapeDtypeStruct((3, 256), F32),
                   jax.ShapeDtypeStruct((256, 256), F32), jax.ShapeDtypeStruct((1, 256), F32)],
        input_output_aliases={8: 0},
        compiler_params=_params(("arbitrary",)),
    )(h, h, h, dcat, dcat, wc, wblk, pscale, dh)


def _lower_bound(lb_ref, layer):
    b0, b1 = lb_ref[0:1, :], lb_ref[1:2, :]
    m = jnp.maximum(b0, b1)
    e0, e1 = jnp.exp(b0 - m), jnp.exp(b1 - m)
    p0, p1 = e0 / (e0 + e1), e1 / (e0 + e1)
    lb = (p0 - p0) if layer == 0 else ((p0 + p1) - p0)
    return lb, p0, p1


def _cumsum_rows(x, reverse=False):
    row = lax.broadcasted_iota(jnp.int32, x.shape, 0)
    for sh in (1, 2, 4, 8):
        if reverse:
            x = x + jnp.where(row < SUB - sh, pltpu.roll(x, SUB - sh, 0), 0.0)
        else:
            x = x + jnp.where(row >= sh, pltpu.roll(x, sh, 0), 0.0)
    return x


def _gates(fz, lb):
    sig = _sigmoid(fz)
    f = lb + (1.0 - lb) * sig
    g = jnp.log(jnp.maximum(f, F_FLOOR))
    k = (1.0 - lb) * (1.0 - sig)
    return sig, f, g, k


def _head(h):
    return slice(h * HG_D, (h + 1) * HG_D)


def _hgrn_fwd(hh, lbp, gnorm, layer, name):
    L = hh.shape[0]
    ts = SEQ_TILE
    n = L // ts
    nsub = ts // SUB

    def body(q_ref, f_ref, i_ref, g_ref, lb_ref, gn_ref, y_ref, o_ref, s_ref, a_ref, St):
        @pl.when(pl.program_id(0) == 0)
        def _():
            St[...] = jnp.zeros_like(St)

        lb, _, _ = _lower_bound(lb_ref, layer)
        gn = jnp.tile(gn_ref[...], (1, HG_HEADS))
        r16 = lax.broadcasted_iota(jnp.int32, (SUB, SUB), 0)
        c16 = lax.broadcasted_iota(jnp.int32, (SUB, SUB), 1)

        def block(j, carry):
            rows = pl.ds(pl.multiple_of(j * SUB, SUB), SUB)
            q = q_ref[rows, :] * Q_SCALE
            iv = i_ref[rows, :]
            gz = g_ref[rows, :]
            _, _, g, k = _gates(f_ref[rows, :], lb)
            G = _cumsum_rows(g)
            Gl = G[SUB - 1:SUB, :]
            qt = (q * jnp.exp(G)).astype(BF16)
            kd = (k * jnp.exp(Gl - G)).astype(BF16)
            eGl = jnp.exp(Gl)
            ib = iv.astype(BF16)
            A = [jnp.zeros((SUB, SUB), F32) for _ in range(HG_HEADS)]
            for s in range(SUB):
                P = q * jnp.exp(jnp.minimum(G - G[s:s + 1, :], 0.0)) * k[s:s + 1, :]
                for h in range(HG_HEADS):
                    A[h] = jnp.where(c16 == s, jnp.sum(P[:, _head(h)], axis=-1, keepdims=True), A[h])
            outs, ons, amats = [], [], []
            for h in range(HG_HEADS):
                sl = _head(h)
                Sb = St[h].astype(BF16)
                s_ref[j, sl, :] = Sb
                Am = jnp.where(r16 >= c16, A[h], 0.0)
                amats.append(Am)
                o = _dnt(qt[:, sl], Sb) + _dnn(Am.astype(BF16), ib[:, sl])
                St[h] = eGl[:, sl] * St[h] + _dtn(ib[:, sl], kd[:, sl])
                outs.append(o)
                ons.append(o * lax.rsqrt(jnp.mean(o * o, axis=-1, keepdims=True) + RMS_EPS))
            a_ref[rows, :] = jnp.concatenate(amats, axis=1)
            o_ref[rows, :] = jnp.concatenate(outs, axis=1)
            y = jnp.concatenate(ons, axis=1) * gn * (gz * _sigmoid(gz))
            y_ref[rows, :] = y.astype(BF16)
            return carry

        lax.fori_loop(0, nsub, block, 0, unroll=2)

    col = lambda c: pl.BlockSpec((ts, HG_W), lambda i: (i, c))
    return pl.pallas_call(
        body, name=name, grid=(n,),
        in_specs=[col(0), col(1), col(2), col(3), pl.BlockSpec((2, HG_W), lambda i: (0, 0)),
                  pl.BlockSpec((1, HG_D), lambda i: (0, 0))],
        out_specs=[pl.BlockSpec((ts, HG_W), lambda i: (i, 0)), pl.BlockSpec((ts, HG_W), lambda i: (i, 0)),
                   pl.BlockSpec((nsub, HG_W, HG_D), lambda i: (i, 0, 0)),
                   pl.BlockSpec((ts, HG_HEADS * SUB), lambda i: (i, 0))],
        out_shape=[jax.ShapeDtypeStruct((L, 2 * HG_W), BF16), jax.ShapeDtypeStruct((L, HG_W), F32),
                   jax.ShapeDtypeStruct((L // SUB, HG_W, HG_D), BF16),
                   jax.ShapeDtypeStruct((L, HG_HEADS * SUB), F32)],
        scratch_shapes=[pltpu.VMEM((HG_HEADS, HG_D, HG_D), F32)],
        compiler_params=_params(("arbitrary",)),
    )(hh, hh, hh, hh, lbp, gnorm)


def _hgrn_bwd(hh, o_raw, states, amat, dcat, lbp, gnorm, layer, name):
    L = hh.shape[0]
    ts = SEQ_TILE
    n = L // ts
    nsub = ts // SUB

    def body(q_ref, f_ref, i_ref, g_ref, o_ref, s_ref, a_ref, dy_ref, lb_ref, gn_ref,
             dh_ref, dlb_ref, dgn_ref, dSt, dlb_acc, S_next):
        step = pl.program_id(0)

        @pl.when(step == 0)
        def _():
            dSt[...] = jnp.zeros_like(dSt)
            S_next[...] = jnp.zeros_like(S_next)
            dlb_acc[...] = jnp.zeros_like(dlb_acc)
            dgn_ref[...] = jnp.zeros_like(dgn_ref)

        lb, p0, p1 = _lower_bound(lb_ref, layer)
        gnh = gn_ref[...]
        gn = jnp.tile(gnh, (1, HG_HEADS))
        r16 = lax.broadcasted_iota(jnp.int32, (SUB, SUB), 0)
        c16 = lax.broadcasted_iota(jnp.int32, (SUB, SUB), 1)

        def block(jj, carry):
            j = nsub - 1 - jj
            rows = pl.ds(pl.multiple_of(j * SUB, SUB), SUB)
            q = q_ref[rows, :] * Q_SCALE
            iv = i_ref[rows, :]
            gz = g_ref[rows, :]
            o = o_ref[rows, :]
            dy = dy_ref[rows, :].astype(F32)
            sig, f, g, k = _gates(f_ref[rows, :], lb)
            G = _cumsum_rows(g)
            Gl = G[SUB - 1:SUB, :]
            eG = jnp.exp(G)
            edl = jnp.exp(Gl - G)
            eGl = jnp.exp(Gl)
            qt = (q * eG).astype(BF16)
            kd = (k * edl).astype(BF16)
            ib = iv.astype(BF16)
            sgz = _sigmoid(gz)
            sil = gz * sgz
            dyn = dy * sil
            on_parts, do_parts = [], []
            dgn = jnp.zeros((1, HG_D), F32)
            for h in range(HG_HEADS):
                sl = _head(h)
                oh = o[:, sl]
                rs = lax.rsqrt(jnp.mean(oh * oh, axis=-1, keepdims=True) + RMS_EPS)
                on = oh * rs
                dgn = dgn + jnp.sum(dyn[:, sl] * on, axis=0, keepdims=True)
                don = dyn[:, sl] * gnh
                do_parts.append(rs * (don - on * jnp.mean(don * on, axis=-1, keepdims=True)))
                on_parts.append(on)
            dgn_ref[...] += dgn
            on_all = jnp.concatenate(on_parts, axis=1)
            dgz = dy * on_all * gn * (sgz * (1.0 + gz * (1.0 - sgz)))
            do = jnp.concatenate(do_parts, axis=1)
            dob = do.astype(BF16)
            amat = a_ref[rows, :]
            dq_p, dk_p, di_p, tail_p = [], [], [], []
            for h in range(HG_HEADS):
                sl = _head(h)
                qh, kh, Gh = q[:, sl], k[:, sl], G[:, sl]
                Ap = jnp.where(r16 >= c16, _dnt(dob[:, sl], ib[:, sl]), 0.0)
                ApT = jnp.where(r16 <= c16, _dnt(ib[:, sl], dob[:, sl]), 0.0)
                dqh = jnp.zeros((SUB, HG_D), F32)
                dkh = jnp.zeros((SUB, HG_D), F32)
                for s in range(SUB):
                    dGs = Gh - Gh[s:s + 1, :]
                    e = jnp.exp(jnp.minimum(dGs, -dGs))
                    dqh = dqh + Ap[:, s:s + 1] * (e * kh[s:s + 1, :])
                    dkh = dkh + ApT[:, s:s + 1] * (e * qh[s:s + 1, :])
                Sb = s_ref[j, sl, :]
                dSb = dSt[h].astype(BF16)
                Am = amat[:, h * SUB:(h + 1) * SUB].astype(BF16)
                dq_p.append(dqh + eG[:, sl] * _dnn(dob[:, sl], Sb))
                dk_p.append(dkh + edl[:, sl] * _dnn(ib[:, sl], dSb))
                di_p.append(_dtn(Am, dob[:, sl]) + _dnt(kd[:, sl], dSb))
                tail_p.append(jnp.sum(dSt[h] * S_next[h].astype(F32), axis=0, keepdims=True))
                S_next[h] = Sb
                dSt[h] = eGl[:, sl] * dSt[h] + _dtn(dob[:, sl], qt[:, sl])
            dq = jnp.concatenate(dq_p, axis=1)
            dk = jnp.concatenate(dk_p, axis=1)
            di = jnp.concatenate(di_p, axis=1)
            dg = _cumsum_rows(q * dq - k * dk, reverse=True) + jnp.concatenate(tail_p, axis=1)
            df = jnp.where(f > F_FLOOR, dg / f, 0.0)
            dfk = df - dk
            dfz = (1.0 - lb) * dfk * sig * (1.0 - sig)
            dlb_acc[...] += jnp.sum(dfk * (1.0 - sig), axis=0, keepdims=True)
            dh_ref[rows, :] = jnp.concatenate([dq * Q_SCALE, dfz, di, dgz], axis=1).astype(BF16)
            return carry

        lax.fori_loop(0, nsub, block, 0)

        @pl.when(step == n - 1)
        def _():
            if layer == 0:
                dlb_ref[...] = jnp.zeros_like(dlb_ref)
            else:
                dz1 = p0 * p1 * dlb_acc[...]
                dlb_ref[...] = jnp.concatenate([-dz1, dz1], axis=0)

    rev = lambda i: n - 1 - i
    col = lambda c: pl.BlockSpec((ts, HG_W), lambda i: (rev(i), c))
    return pl.pallas_call(
        body, name=name, grid=(n,),
        in_specs=[col(0), col(1), col(2), col(3), col(0),
                  pl.BlockSpec((nsub, HG_W, HG_D), lambda i: (rev(i), 0, 0)),
                  pl.BlockSpec((ts, HG_HEADS * SUB), lambda i: (rev(i), 0)), col(0),
                  pl.BlockSpec((2, HG_W), lambda i: (0, 0)), pl.BlockSpec((1, HG_D), lambda i: (0, 0))],
        out_specs=[pl.BlockSpec((ts, 4 * HG_W), lambda i: (rev(i), 0)),
                   pl.BlockSpec((2, HG_W), lambda i: (0, 0)), pl.BlockSpec((1, HG_D), lambda i: (0, 0))],
        out_shape=[jax.ShapeDtypeStruct((L, 4 * HG_W + 1024), BF16), jax.ShapeDtypeStruct((2, HG_W), F32),
                   jax.ShapeDtypeStruct((1, HG_D), F32)],
        scratch_shapes=[pltpu.VMEM((HG_HEADS, HG_D, HG_D), F32), pltpu.VMEM((1, HG_W), F32),
                        pltpu.VMEM((HG_HEADS, HG_D, HG_D), BF16)],
        compiler_params=_params(("arbitrary",)),
    )(hh, hh, hh, hh, o_raw, states, amat, dcat, lbp, gnorm)


def _adamw_body(gp_ref, w_ref, m_ref, v_ref, g_ref, d_ref, mo_ref, vo_ref):
    c1 = 1.0 - ADAM_B1 ** ADAM_STEP
    c2 = 1.0 - ADAM_B2 ** ADAM_STEP
    g = gp_ref[0].astype(F32)
    for k in range(1, N_DEV):
        g = g + gp_ref[k].astype(F32)
    mn = ADAM_B1 * m_ref[...] + (1.0 - ADAM_B1) * g
    vn = ADAM_B2 * v_ref[...] + (1.0 - ADAM_B2) * (g * g)
    m_hat = mn / c1
    v_hat = vn / c2
    g_ref[...] = g
    d_ref[...] = -ADAM_LR * (m_hat / (jnp.sqrt(v_hat) + ADAM_EPS) + ADAM_WD * w_ref[...])
    mo_ref[...] = mn
    vo_ref[...] = vn


def _adamw_layers(gparts, w, m, v, name):
    depth, R, C = w.shape
    tr = _tile(R, 256, 16)
    nr = R // tr

    def body(*refs):
        layer = pl.program_id(0)
        for d in range(depth):
            @pl.when(layer == d)
            def _(d=d):
                _adamw_body(refs[d], *refs[depth:])

    def parts_spec(d):
        return pl.BlockSpec((N_DEV, tr, C),
                            lambda l, i: (0, jnp.where(l == d, i, jnp.where(l < d, 0, nr - 1)), 0))

    blk = pl.BlockSpec((None, tr, C), lambda l, i: (l, i, 0))
    shp = jax.ShapeDtypeStruct((depth, R, C), F32)
    return pl.pallas_call(
        body, name=name, grid=(depth, nr),
        in_specs=[parts_spec(d) for d in range(depth)] + [blk, blk, blk],
        out_specs=[blk, blk, blk, blk], out_shape=[shp, shp, shp, shp],
        compiler_params=_params(("arbitrary", "arbitrary")),
    )(*gparts, w, m, v)


def _adamw(gparts, w, m, v, name):
    R = w.shape[0]
    tr = _tile(R, 1024, 16) if R % 16 == 0 else R

    def body(*refs):
        _adamw_body(*refs)

    row = pl.BlockSpec((tr, LANES), lambda i: (i, 0))
    shp = jax.ShapeDtypeStruct((R, LANES), F32)
    return pl.pallas_call(
        body, name=name, grid=(R // tr,),
        in_specs=[pl.BlockSpec((N_DEV, tr, LANES), lambda i: (0, i, 0)), row, row, row],
        out_specs=[row, row, row, row], out_shape=[shp, shp, shp, shp],
        compiler_params=_params(("parallel",)),
    )(gparts, w, m, v)


def _flip(coord, bit):
    return 1 - coord if bit else coord


def _gather_many(blocks, name):
    n = len(blocks)

    def body(*refs):
        x_refs, out_refs = refs[:n], refs[n:2 * n]
        send_sems, recv_sems, local_sems = refs[2 * n:]
        x, y, c = lax.axis_index("x"), lax.axis_index("y"), lax.axis_index("c")
        me, sibling = (x, y, c), (x, y, 1 - c)
        chips = [(1 - x, y), (x, 1 - y), (1 - x, 1 - y)]

        def slot(a, px, py, pc):
            return out_refs[a].at[4 * px + 2 * py + pc]

        def copy(a, k, blk, to, src=None):
            return pltpu.make_async_remote_copy(
                src_ref=slot(a, *blk) if src is None else src, dst_ref=slot(a, *blk),
                send_sem=send_sems.at[7 * a + k], recv_sem=recv_sems.at[7 * a + k],
                device_id=to, device_id_type=pl.DeviceIdType.MESH)

        mine = [pltpu.make_async_copy(x_refs[a], slot(a, *me), local_sems.at[a]) for a in range(n)]
        for cp in mine:
            cp.start()
        first = [copy(a, 0, me, sibling, src=x_refs[a]) for a in range(n)]
        for j, chip in enumerate(chips):
            first += [copy(a, 1 + j, me, (*chip, c), src=x_refs[a]) for a in range(n)]
        for cp in first:
            cp.start()
        passed = []
        for j, chip in enumerate(chips):
            for a in range(n):
                copy(a, 1 + j, (*chip, c), me).wait_recv()
                fwd = copy(a, 4 + j, (*chip, c), sibling)
                fwd.start()
                passed.append(fwd)
        for a in range(n):
            copy(a, 0, sibling, me).wait_recv()
        for j, chip in enumerate(chips):
            for a in range(n):
                copy(a, 4 + j, (*chip, 1 - c), me).wait_recv()
        for cp in first + passed:
            cp.wait_send()
        for cp in mine:
            cp.wait()

    hbm = pl.BlockSpec(memory_space=pl.ANY)
    return pl.pallas_call(
        body, name=name,
        out_shape=[jax.ShapeDtypeStruct((N_DEV,) + b.shape, b.dtype) for b in blocks],
        in_specs=[hbm] * n, out_specs=[hbm] * n,
        scratch_shapes=[pltpu.SemaphoreType.DMA((7 * n,)), pltpu.SemaphoreType.DMA((7 * n,)),
                        pltpu.SemaphoreType.DMA((n,))],
    )(*blocks)


def _split_start(blocks, chunked, name):
    n = len(blocks)
    lands = [lax.empty(b.shape if chunked else (N_DEV,) + b.shape, b.dtype) for b in blocks]

    def body(*refs):
        x_refs, land_refs = refs[:n], refs[n:2 * n]
        send_sems, recv_sems, token = refs[2 * n], refs[2 * n + 1], refs[-1]
        x, y, c = lax.axis_index("x"), lax.axis_index("y"), lax.axis_index("c")
        me = 4 * x + 2 * y + c
        for a in range(n):
            for k in range(1, N_DEV):
                px, py, pc = _flip(x, k & 4), _flip(y, k & 2), _flip(c, k & 1)
                pltpu.make_async_remote_copy(
                    src_ref=x_refs[a].at[4 * px + 2 * py + pc] if chunked else x_refs[a],
                    dst_ref=land_refs[a].at[me],
                    send_sem=send_sems.at[7 * a + k - 1], recv_sem=recv_sems.at[7 * a + k - 1],
                    device_id=(px, py, pc), device_id_type=pl.DeviceIdType.MESH).start()
        token[...] = jnp.zeros_like(token)

    hbm = pl.BlockSpec(memory_space=pltpu.HBM)
    sem = pl.BlockSpec(memory_space=pltpu.SEMAPHORE)
    outs = pl.pallas_call(
        body, name=name,
        out_shape=(pltpu.SemaphoreType.DMA((7 * n,)), pltpu.SemaphoreType.DMA((7 * n,)),
                   *[pltpu.HBM(b.shape, b.dtype) for b in blocks], *[pltpu.HBM(l.shape, l.dtype) for l in lands],
                   jax.ShapeDtypeStruct((8, LANES), F32)),
        in_specs=[hbm] * (2 * n),
        out_specs=(sem, sem, *[hbm] * (2 * n), pl.BlockSpec(memory_space=pltpu.VMEM)),
        input_output_aliases={i: 2 + i for i in range(2 * n)},
        compiler_params=pltpu.CompilerParams(has_side_effects=pltpu.SideEffectType.DATAFLOW_SIDE_EFFECTING),
    )(*[pltpu.with_memory_space_constraint(b, pltpu.HBM) for b in blocks],
      *[pltpu.with_memory_space_constraint(l, pltpu.HBM) for l in lands])
    return outs[0], outs[1], list(outs[2:2 + n]), list(outs[2 + n:2 + 2 * n]), outs[-1]


def _split_wait(started, chunked, after, name):
    send_sems, recv_sems, blocks, lands, _ = started
    n = len(blocks)

    def body(*refs):
        x_refs, land_refs = refs[:n], refs[n:2 * n]
        send_sems, recv_sems = refs[2 * n], refs[2 * n + 1]
        x, y, c = lax.axis_index("x"), lax.axis_index("y"), lax.axis_index("c")
        for a in range(n):
            for k in range(1, N_DEV):
                px, py, pc = _flip(x, k & 4), _flip(y, k & 2), _flip(c, k & 1)
                copy = pltpu.make_async_remote_copy(
                    src_ref=x_refs[a].at[4 * px + 2 * py + pc] if chunked else x_refs[a],
                    dst_ref=land_refs[a].at[4 * px + 2 * py + pc],
                    send_sem=send_sems.at[7 * a + k - 1], recv_sem=recv_sems.at[7 * a + k - 1],
                    device_id=(px, py, pc), device_id_type=pl.DeviceIdType.MESH)
                copy.wait_send()
                copy.wait_recv()

    hbm = pl.BlockSpec(memory_space=pltpu.HBM)
    sem = pl.BlockSpec(memory_space=pltpu.SEMAPHORE)
    outs = pl.pallas_call(
        body, name=name,
        out_shape=(*[pltpu.HBM(b.shape, b.dtype) for b in blocks], *[pltpu.HBM(l.shape, l.dtype) for l in lands]),
        in_specs=[hbm] * (2 * n) + [sem, sem, pl.BlockSpec(memory_space=pl.ANY)],
        out_specs=[hbm] * (2 * n),
        input_output_aliases={i: i for i in range(2 * n)},
        compiler_params=pltpu.CompilerParams(has_side_effects=pltpu.SideEffectType.DATAFLOW_SIDE_EFFECTING),
    )(*blocks, *lands, send_sems, recv_sems, after)
    me = 4 * lax.axis_index("x") + 2 * lax.axis_index("y") + lax.axis_index("c")
    own = [lax.dynamic_index_in_dim(b, me, 0, keepdims=False) if chunked else b for b in outs[:n]]
    return [lax.dynamic_update_index_in_dim(z, o, me, 0) for z, o in zip(outs[n:], own)]


def _exchange_grads(layer_chunks, small_chunks, rep_block, name):
    flows, inputs = [], []
    for p, per_layer in enumerate(layer_chunks):
        for l, arr in enumerate(per_layer):
            flows.append(("param", p, l))
            inputs.append(arr)
    flows += [("small",), ("rep",)]
    inputs += [small_chunks, rep_block]
    n_par = len(layer_chunks)
    n_in, n_out, nf = len(inputs), n_par + 2, len(flows)

    def body(*refs):
        in_refs, out_refs = refs[:n_in], refs[n_in:n_in + n_out]
        send_sems, recv_sems, local_sems = refs[n_in + n_out:]
        x, y, c = lax.axis_index("x"), lax.axis_index("y"), lax.axis_index("c")
        me = 4 * x + 2 * y + c

        def src(f, dev):
            return in_refs[f] if flows[f][0] == "rep" else in_refs[f].at[dev]

        def dst(f, dev):
            if flows[f][0] == "param":
                _, p, l = flows[f]
                return out_refs[p].at[dev, l]
            return out_refs[n_par + (0 if flows[f][0] == "small" else 1)].at[dev]

        mine = [pltpu.make_async_copy(src(f, me), dst(f, me), local_sems.at[f]) for f in range(nf)]
        for cp in mine:
            cp.start()
        copies = []
        for k in range(1, N_DEV):
            px, py, pc = _flip(x, k & 4), _flip(y, k & 2), _flip(c, k & 1)
            peer = 4 * px + 2 * py + pc
            for f in range(nf):
                sems = dict(send_sem=send_sems.at[7 * f + k - 1], recv_sem=recv_sems.at[7 * f + k - 1],
                            device_id=(px, py, pc), device_id_type=pl.DeviceIdType.MESH)
                send = pltpu.make_async_remote_copy(src_ref=src(f, peer), dst_ref=dst(f, me), **sems)
                recv = pltpu.make_async_remote_copy(src_ref=src(f, peer), dst_ref=dst(f, peer), **sems)
                send.start()
                copies.append((send, recv))
        for send, recv in copies:
            recv.wait_recv()
        for send, recv in copies:
            send.wait_send()
        for cp in mine:
            cp.wait()

    out_shape = [jax.ShapeDtypeStruct((N_DEV, len(pl_)) + pl_[0].shape[1:], pl_[0].dtype) for pl_ in layer_chunks]
    out_shape += [jax.ShapeDtypeStruct(small_chunks.shape, small_chunks.dtype),
                  jax.ShapeDtypeStruct((N_DEV,) + rep_block.shape, rep_block.dtype)]
    hbm = pl.BlockSpec(memory_space=pl.ANY)
    return pl.pallas_call(
        body, name=name, out_shape=out_shape,
        in_specs=[hbm] * n_in, out_specs=[hbm] * n_out,
        scratch_shapes=[pltpu.SemaphoreType.DMA((7 * nf,)), pltpu.SemaphoreType.DMA((7 * nf,)),
                        pltpu.SemaphoreType.DMA((nf,))],
    )(*inputs)


def _pack_rows(size):
    return -(-size // (8 * LANES)) * 8


def _pack(arrs, dtype):
    parts, offs, r = [], [], 0
    for a in arrs:
        flat = a.astype(dtype).reshape(-1)
        nrow = _pack_rows(flat.shape[0])
        flat = jnp.pad(flat, (0, nrow * LANES - flat.shape[0]))
        parts.append(flat.reshape(nrow, LANES))
        offs.append((r, nrow))
        r += nrow
    return jnp.concatenate(parts, axis=0), offs


def _unpack(buf, offs, shapes, lead=()):
    outs = []
    for (r, nrow), shp in zip(offs, shapes):
        size = 1
        for s in shp:
            size *= s
        flat = buf[..., r:r + nrow, :].reshape(lead + (nrow * LANES,))
        outs.append(flat[..., :size].reshape(lead + tuple(shp)))
    return outs


def _cols_from_shards(g, axis):
    return jnp.concatenate([g[j] for j in range(N_DEV)], axis=axis)


BIG = ("w_in", "w_o", "w_up", "w_down")
SMALL_SHARDED = ("meta_tokens", "w_conv", "w_ffn_conv")
REPLICATED = ("hg_lower_bounds", "w_pool", "pool_scale", "hg_norm_g", "ln1_g", "ln1_b", "b_ffn_conv", "ln2_g", "ln2_b")
WEIGHTS = ("meta_tokens", "hg_lower_bounds", "w_in", "w_conv", "w_pool", "pool_scale", "hg_norm_g", "w_o",
           "ln1_g", "ln1_b", "w_up", "w_ffn_conv", "b_ffn_conv", "w_down", "ln2_g", "ln2_b")


def _pool_blockdiag(w_pool_l):
    z = jnp.zeros((POOL_GROUP, POOL_GROUP), w_pool_l.dtype)
    rows = [jnp.concatenate([w_pool_l[g] if h == g else z for h in range(4)], axis=1) for g in range(4)]
    return jnp.concatenate(rows, axis=0)


def _in_weights(g_in):
    w_in = jnp.transpose(g_in, (1, 0, 2)).reshape(D_MODEL, -1)
    return dict(w_hc=jnp.concatenate([w_in[:, 768:2816], w_in[:, 0:768], w_in[:, 2816:3072]], axis=1))


def _rest_weights(g_o, g_up, g_down):
    w_o = g_o.reshape(-1, D_MODEL)
    return dict(w_o=jnp.concatenate([w_o[256:768], w_o[0:256], w_o[768:1024]], axis=0),
                w_up=jnp.transpose(g_up, (1, 0, 2)).reshape(D_MODEL, -1), w_down=g_down.reshape(-1, D_MODEL))


def kernel(x, meta_tokens, hg_lower_bounds, w_in, w_conv, w_pool, pool_scale, hg_norm_g, w_o, ln1_g, ln1_b, w_up, w_ffn_conv, b_ffn_conv, w_down, ln2_g, ln2_b, loss_target, m_meta_tokens, m_hg_lower_bounds, m_w_in, m_w_conv, m_w_pool, m_pool_scale, m_hg_norm_g, m_w_o, m_ln1_g, m_ln1_b, m_w_up, m_w_ffn_conv, m_b_ffn_conv, m_w_down, m_ln2_g, m_ln2_b, v_meta_tokens, v_hg_lower_bounds, v_w_in, v_w_conv, v_w_pool, v_pool_scale, v_hg_norm_g, v_w_o, v_ln1_g, v_ln1_b, v_w_up, v_w_ffn_conv, v_b_ffn_conv, v_w_down, v_ln2_g, v_ln2_b):
    W = dict(meta_tokens=meta_tokens, hg_lower_bounds=hg_lower_bounds, w_in=w_in, w_conv=w_conv, w_pool=w_pool,
             pool_scale=pool_scale, hg_norm_g=hg_norm_g, w_o=w_o, ln1_g=ln1_g, ln1_b=ln1_b, w_up=w_up,
             w_ffn_conv=w_ffn_conv, b_ffn_conv=b_ffn_conv, w_down=w_down, ln2_g=ln2_g, ln2_b=ln2_b)
    M = dict(meta_tokens=m_meta_tokens, hg_lower_bounds=m_hg_lower_bounds, w_in=m_w_in, w_conv=m_w_conv,
             w_pool=m_w_pool, pool_scale=m_pool_scale, hg_norm_g=m_hg_norm_g, w_o=m_w_o, ln1_g=m_ln1_g,
             ln1_b=m_ln1_b, w_up=m_w_up, w_ffn_conv=m_w_ffn_conv, b_ffn_conv=m_b_ffn_conv, w_down=m_w_down,
             ln2_g=m_ln2_g, ln2_b=m_ln2_b)
    V = dict(meta_tokens=v_meta_tokens, hg_lower_bounds=v_hg_lower_bounds, w_in=v_w_in, w_conv=v_w_conv,
             w_pool=v_w_pool, pool_scale=v_pool_scale, hg_norm_g=v_hg_norm_g, w_o=v_w_o, ln1_g=v_ln1_g,
             ln1_b=v_ln1_b, w_up=v_w_up, w_ffn_conv=v_w_ffn_conv, b_ffn_conv=v_b_ffn_conv, w_down=v_w_down,
             ln2_g=v_ln2_g, ln2_b=v_ln2_b)
    assert x.shape[0] == 1 and x.shape[2] == D_MODEL and w_in.shape[0] == DEPTH
    seq = x.shape[1]
    L = -(-(seq + N_META) // ROW_ALIGN) * ROW_ALIGN

    small_pack, small_offs = _pack([W[n] for n in SMALL_SHARDED], F32)
    shards = {n: [W[n][l].astype(BF16) for l in range(DEPTH)] for n in BIG}
    g_in0, small_all = _gather_many([shards["w_in"][0], small_pack], "gather_weights")
    full = {}
    for n, a in zip(SMALL_SHARDED, _unpack(small_all, small_offs, [W[n].shape for n in SMALL_SHARDED], (N_DEV,))):
        full[n] = _cols_from_shards(a, 1)
    order = (small_all[0, 0, 0] * 0.0).astype(BF16)
    rest0_started = _split_start([shards[n][0] + order for n in BIG[1:]], False, "gather_rest0_start")
    order = rest0_started[4][0, 0].astype(BF16)
    layer1_started = _split_start([shards[n][1] + order for n in BIG], False, "gather_layer1_start")
    lb_in = hg_lower_bounds + layer1_started[4][0, 0]

    pad_rows = L - N_META - seq
    xp = jnp.concatenate([full["meta_tokens"], x[0], jnp.zeros((pad_rows, D_MODEL), F32)], axis=0)
    tgt = jnp.concatenate([jnp.zeros((N_META, D_MODEL), F32), loss_target[0], jnp.zeros((pad_rows, D_MODEL), F32)], axis=0)

    saved = []
    h_in, h_in_b = xp, xp.astype(BF16)
    for l in range(DEPTH):
        if l == 0:
            lw = _in_weights(g_in0)
        else:
            g_in, *g_rest = _split_wait(layer1_started, False, h_in_b, "gather_layer1_wait")
            lw = {**_in_weights(g_in), **_rest_weights(*g_rest)}
        wc = full["w_conv"][l].T
        wblk = _pool_blockdiag(w_pool[l]).astype(BF16)
        ps = pool_scale[l][None, :]
        gn = hg_norm_g[l][None, :]
        wf = full["w_ffn_conv"][l].T
        bf = b_ffn_conv[l][None, :]
        h = _matmul(h_in_b, lw["w_hc"], "nn", F32, f"fwd_in_{l}")
        y_hg, o_raw, states, amat = _hgrn_fwd(h, lb_in if l == 0 else hg_lower_bounds, gn, l, f"hgrn_fwd_{l}")
        cat = _cp_fwd(h, 2, y_hg, wc, wblk, ps, f"convpool_fwd_{l}")
        if l == 0:
            lw.update(_rest_weights(*_split_wait(rest0_started, False, cat, "gather_rest0_wait")))
        z1, x1, x1_b = _matmul_ln(cat, lw["w_o"], h_in, ln1_g[l][None, :], ln1_b[l][None, :], f"fwd_o_ln1_{l}")
        up = _matmul(x1_b, lw["w_up"], "nn", BF16, f"fwd_up_{l}")
        a, u = _ffn_act_fwd(up, wf, bf, f"ffn_fwd_{l}")
        saved.append(dict(lw=lw, wc=wc, wblk=wblk, ps=ps, gn=gn, wf=wf, bf=bf, x_b=h_in_b, h=h,
                          o_raw=o_raw, states=states, amat=amat, cat=cat, z1=z1, x1_b=x1_b, up=up, u=u, a=a))
        if l < DEPTH - 1:
            saved[l]["z2"], h_in, h_in_b = _matmul_ln(a, lw["w_down"], x1, ln2_g[l][None, :], ln2_b[l][None, :],
                                                     f"fwd_down_ln2_{l}")
        else:
            saved[l]["z2"], dy, loss_part = _matmul_ln(a, lw["w_down"], x1, ln2_g[l][None, :], ln2_b[l][None, :],
                                                       f"fwd_down_ln2_loss_{l}", loss=(tgt, seq))

    loss = lax.psum(loss_part[0, 0], ("x", "y", "c"))

    G = {}
    per_layer = {n: [None] * DEPTH for n in ("w_conv", "w_pool", "pool_scale", "hg_norm_g", "ln1_g", "ln1_b",
                                             "w_ffn_conv", "b_ffn_conv", "ln2_g", "ln2_b")}
    ffn_started, mix_started = [None] * DEPTH, [None] * DEPTH
    order = jnp.zeros((), F32)
    dlb_total = jnp.zeros((DEPTH, HG_W), F32)
    for l in reversed(range(DEPTH)):
        s = saved[l]
        lw = s["lw"]
        dz2_b, dg2, db2 = _ln_bwd(s["z2"], dy, ln2_g[l][None, :] + order, f"ln2_bwd_{l}")
        da = _matmul(dz2_b, lw["w_down"], "nt", BF16, f"bwd_da_{l}")
        d_w_down = _matmul(s["a"], dz2_b, "tn", BF16, f"wgrad_down_{l}")
        dup, dwf, dbf = _ffn_act_bwd(s["up"], s["u"], da, s["wf"], f"ffn_bwd_{l}")
        dx1 = _matmul(dup, lw["w_up"], "nt", BF16, f"bwd_dx1_{l}", res=dz2_b, alpha=ALPHA)
        d_w_up = _matmul(s["x1_b"], dup, "tn", BF16, f"wgrad_up_{l}")
        ffn_started[l] = _split_start([jnp.transpose(d_w_up.reshape(D_MODEL, N_DEV, -1), (1, 0, 2)),
                                       d_w_down.reshape(N_DEV, -1, D_MODEL)], True, f"scatter_ffn{l}_start")
        order = ffn_started[l][4][0, 0]
        dz1_b, dg1, db1 = _ln_bwd(s["z1"], dx1, ln1_g[l][None, :] + order, f"ln1_bwd_{l}")
        dcat = _matmul(dz1_b, lw["w_o"], "nt", BF16, f"bwd_dcat_{l}")
        d_w_o = _matmul(s["cat"], dz1_b, "tn", BF16, f"wgrad_o_{l}")
        dh, dlb, dgn = _hgrn_bwd(s["h"], s["o_raw"], s["states"], s["amat"], dcat, hg_lower_bounds, s["gn"], l,
                                 f"hgrn_bwd_{l}")
        dh, dwc, dwblk, dps = _cp_bwd(s["h"], 2, dcat, dh, s["wc"], s["wblk"], s["ps"], f"convpool_bwd_{l}")
        d_w_hc = _matmul(s["x_b"], dh, "tn", BF16, f"wgrad_in_{l}")
        d_w_in = jnp.concatenate([d_w_hc[:, 2048:2816], d_w_hc[:, 0:2048], d_w_hc[:, 2816:3072]], axis=1)
        mix_chunks = [jnp.transpose(d_w_in.reshape(D_MODEL, N_DEV, -1), (1, 0, 2)),
                      jnp.concatenate([d_w_o[512:768], d_w_o[0:512], d_w_o[768:1024]], axis=0).reshape(N_DEV, -1, D_MODEL)]
        mix_started[l] = _split_start(mix_chunks, True, f"scatter_mix{l}_start")
        order = mix_started[l][4][0, 0]
        dx = _matmul(dh, lw["w_hc"] + order.astype(BF16), "nt", F32, f"bwd_dx_{l}", res=dz1_b, alpha=ALPHA)
        per_layer["w_conv"][l] = dwc.T
        per_layer["w_ffn_conv"][l] = dwf.T
        per_layer["b_ffn_conv"][l] = dbf[0]
        per_layer["w_pool"][l] = jnp.stack([dwblk[g * 64:(g + 1) * 64, g * 64:(g + 1) * 64] for g in range(4)], axis=0)
        per_layer["pool_scale"][l] = dps[0]
        per_layer["hg_norm_g"][l] = dgn[0]
        per_layer["ln1_g"][l], per_layer["ln1_b"][l] = dg1[0], db1[0]
        per_layer["ln2_g"][l], per_layer["ln2_b"][l] = dg2[0], db2[0]
        dlb_total = dlb_total + dlb
        dy = dx
    for n, parts in per_layer.items():
        G[n] = jnp.stack(parts, axis=0)
    G["hg_lower_bounds"] = dlb_total
    grad_x = dy[N_META:N_META + seq][None]

    def shard_major(g, lead):
        g = g.reshape(g.shape[:lead] + (N_DEV, -1) + g.shape[lead + 1:])
        g = jnp.moveaxis(g, lead, 0).reshape(N_DEV, -1)
        nrow = _pack_rows(g.shape[1])
        return jnp.pad(g, ((0, 0), (0, nrow * LANES - g.shape[1]))).reshape(N_DEV, nrow, LANES)

    small_chunks = jnp.concatenate([shard_major(dy[0:N_META], 1), shard_major(G["w_conv"], 1),
                                    shard_major(G["w_ffn_conv"], 1)], axis=1)
    w_small, _ = _pack([W[n] for n in SMALL_SHARDED], F32)
    rep_pack, rep_offs = _pack([G[n] for n in REPLICATED], F32)
    small_recv, rep_all = _exchange_grads([], small_chunks, rep_pack, "exchange_grads")
    parts = {n: [] for n in BIG}
    for l in range(DEPTH):
        up_l, down_l = _split_wait(ffn_started[l], True, rep_all, f"scatter_ffn{l}_wait")
        in_l, o_l = _split_wait(mix_started[l], True, rep_all, f"scatter_mix{l}_wait")
        for n, a in zip(BIG, (in_l, o_l, up_l, down_l)):
            parts[n].append(a)

    res = {k: {} for k in ("grad", "delta", "new_m", "new_v")}
    kinds = ("grad", "delta", "new_m", "new_v")
    for n in BIG:
        for kind, a in zip(kinds, _adamw_layers(parts[n], W[n], M[n], V[n], f"adamw_{n}")):
            res[kind][n] = a
    m_small, _ = _pack([M[n] for n in SMALL_SHARDED], F32)
    v_small, _ = _pack([V[n] for n in SMALL_SHARDED], F32)
    outs_small = _adamw(small_recv, w_small, m_small, v_small, "adamw_small_sharded")
    w_rep, _ = _pack([W[n] for n in REPLICATED], F32)
    m_rep, _ = _pack([M[n] for n in REPLICATED], F32)
    v_rep, _ = _pack([V[n] for n in REPLICATED], F32)
    outs_rep = _adamw(rep_all, w_rep, m_rep, v_rep, "adamw_replicated")
    for kind, b_sm, b_rep in zip(kinds, outs_small, outs_rep):
        for n, a in zip(SMALL_SHARDED, _unpack(b_sm, small_offs, [W[n].shape for n in SMALL_SHARDED])):
            res[kind][n] = a
        for n, a in zip(REPLICATED, _unpack(b_rep, rep_offs, [W[n].shape for n in REPLICATED])):
            res[kind][n] = a

    return (loss, grad_x, *[res["grad"][n] for n in WEIGHTS], *[res["delta"][n] for n in WEIGHTS],
            *[res["new_m"][n] for n in WEIGHTS], *[res["new_v"][n] for n in WEIGHTS])
```

```python
import jax
import jax.numpy as jnp
from jax import lax
from jax.experimental import pallas as pl
from jax.experimental.pallas import tpu as pltpu

F32 = jnp.float32
BF16 = jnp.bfloat16

N_DEV = 8
D_MODEL = 1024
N_META = 16
DEPTH = 2
CONV_W = 256
HG_W = 512
HG_D = 128
HG_HEADS = 4
POOL_W = 256
POOL_GROUP = 64
D_FF = 2816
ALPHA = (2 * DEPTH) ** 0.25
LN_EPS = 1e-5
RMS_EPS = 1e-6
F_FLOOR = 1e-30
Q_SCALE = HG_D ** -0.5
SUB = 16
SEQ_TILE = 192
FFN_TILE = 192
CP_TILE_CAP = 768
ROW_ALIGN = 192
LANES = 128
VMEM_LIMIT = 48 * 1024 * 1024
MATMUL_VMEM_BUDGET = 38 * 1024 * 1024

ADAM_LR = 0.001
ADAM_B1 = 0.9
ADAM_B2 = 0.999
ADAM_EPS = 1e-08
ADAM_WD = 0.01
ADAM_STEP = 10


def _tile(n, cap, mult):
    best = 0
    for t in range(mult, min(n, cap) + 1, mult):
        if n % t == 0:
            best = t
    assert best > 0, (n, cap, mult)
    return best


def _params(sem, vmem=VMEM_LIMIT):
    return pltpu.CompilerParams(dimension_semantics=sem, vmem_limit_bytes=vmem)


def _dnt(a, b):
    return lax.dot_general(a, b, (((1,), (1,)), ((), ())), preferred_element_type=F32)


def _dtn(a, b):
    return lax.dot_general(a, b, (((0,), (0,)), ((), ())), preferred_element_type=F32)


def _dnn(a, b):
    return jnp.dot(a, b, preferred_element_type=F32)


def _sigmoid(x):
    return jax.nn.sigmoid(x)


def _matmul(a, b, mode, out_dtype, name, res=None, alpha=1.0):
    if mode == "tn":
        K, M = a.shape
    else:
        M, K = a.shape
    N = b.shape[0] if mode == "nt" else b.shape[1]
    out_bytes = jnp.dtype(out_dtype).itemsize
    if mode == "tn" and M % 512 == 0 and N % 512 == 0:
        tk, tm, tn = K, 512, 512
        nk, use_acc = 1, False
    else:
        tn = _tile(N, 1536, LANES)
        tk = _tile(K, 1536, 16) if mode == "tn" else _tile(K, 2816, LANES)
        nk = K // tk
        use_acc = nk > 1 and out_dtype != F32
        tm = M
        for cap in (1536, 768, 384):
            tm = _tile(M, cap, 16)
            blocks = 2 * (a.dtype.itemsize * tm * tk + b.dtype.itemsize * tn * tk + out_bytes * tm * tn
                          + (res.dtype.itemsize * tm * tn if res is not None else 0)) + (4 * tm * tn if use_acc else 0)
            if blocks <= MATMUL_VMEM_BUDGET:
                break
    dims = {"nn": ((1,), (0,)), "nt": ((1,), (1,)), "tn": ((0,), (0,))}[mode]

    def body(*refs):
        a_ref, b_ref = refs[0], refs[1]
        r_ref = refs[2] if res is not None else None
        o_ref = refs[3] if res is not None else refs[2]
        acc = refs[-1] if use_acc else o_ref
        k = pl.program_id(2)
        p = lax.dot_general(a_ref[...].astype(BF16), b_ref[...].astype(BF16), (dims, ((), ())),
                            preferred_element_type=F32)

        def finish(r):
            if r_ref is not None:
                r = r + alpha * r_ref[...].astype(F32)
            o_ref[...] = r.astype(out_dtype)

        if nk == 1:
            finish(p)
        else:
            @pl.when(k == 0)
            def _():
                acc[...] = p

            @pl.when((k > 0) & (k < nk - 1))
            def _():
                acc[...] += p

            @pl.when(k == nk - 1)
            def _():
                finish(acc[...] + p)

    if mode == "tn":
        a_spec = pl.BlockSpec((tk, tm), lambda i, j, k: (k, i))
    else:
        a_spec = pl.BlockSpec((tm, tk), lambda i, j, k: (i, k))
    if mode == "nt":
        b_spec = pl.BlockSpec((tn, tk), lambda i, j, k: (j, k))
    else:
        b_spec = pl.BlockSpec((tk, tn), lambda i, j, k: (k, j))
    in_specs = [a_spec, b_spec]
    args = [a, b]
    if res is not None:
        in_specs.append(pl.BlockSpec((tm, tn), lambda i, j, k: (i, j)))
        args.append(res)
    return pl.pallas_call(
        body, name=name,
        grid=(M // tm, N // tn, nk),
        in_specs=in_specs,
        out_specs=pl.BlockSpec((tm, tn), lambda i, j, k: (i, j)),
        out_shape=jax.ShapeDtypeStruct((M, N), out_dtype),
        scratch_shapes=[pltpu.VMEM((tm, tn), F32)] if use_acc else [],
        compiler_params=_params(("parallel", "parallel", "arbitrary")),
    )(*args)


def _matmul_ln(a, w, x, g, b, name, loss=None):
    L, K = a.shape
    D = w.shape[1]
    tr = L
    for cap in (1536, 768, 384):
        tr = _tile(L, cap, 16)
        if 2 * (2 * tr * K + 2 * K * D + 4 * tr * D * (4 if loss else 3) + 2 * tr * D) <= MATMUL_VMEM_BUDGET:
            break

    def body(*refs):
        a_ref, w_ref, x_ref, g_ref, b_ref = refs[:5]
        z = ALPHA * x_ref[...] + _dnn(a_ref[...], w_ref[...])
        mu = jnp.mean(z, axis=-1, keepdims=True)
        zc = z - mu
        var = jnp.mean(zc * zc, axis=-1, keepdims=True)
        y = zc * lax.rsqrt(var + LN_EPS) * g_ref[...] + b_ref[...]
        if loss is None:
            z_ref, y_ref, yb_ref = refs[5:]
            y_ref[...] = y
            yb_ref[...] = y.astype(BF16)
        else:
            t_ref, z_ref, dy_ref, loss_ref = refs[5:]
            i = pl.program_id(0)

            @pl.when(i == 0)
            def _():
                loss_ref[...] = jnp.zeros_like(loss_ref)

            r = i * tr + lax.broadcasted_iota(jnp.int32, (tr, D), 0)
            valid = (r >= N_META) & (r < N_META + loss[1])
            e = jnp.where(valid, y - t_ref[...], 0.0)
            dy_ref[...] = e * (1.0 / D)
            s = jnp.sum(jnp.sum(e * e, axis=-1, keepdims=True), axis=0, keepdims=True)
            loss_ref[...] += (0.5 / D) * s
        z_ref[...] = z.astype(BF16)

    row = pl.BlockSpec((tr, D), lambda i: (i, 0))
    vec = pl.BlockSpec((1, D), lambda i: (0, 0))
    in_specs = [pl.BlockSpec((tr, K), lambda i: (i, 0)), pl.BlockSpec((K, D), lambda i: (0, 0)), row, vec, vec]
    f32_rows = jax.ShapeDtypeStruct((L, D), F32)
    b16_rows = jax.ShapeDtypeStruct((L, D), BF16)
    if loss is None:
        args, out_specs = [a, w, x, g, b], [row, row, row]
        out_shape = [b16_rows, f32_rows, b16_rows]
    else:
        args, in_specs = [a, w, x, g, b, loss[0]], in_specs + [row]
        out_specs = [row, row, pl.BlockSpec((1, 1), lambda i: (0, 0))]
        out_shape = [b16_rows, f32_rows, jax.ShapeDtypeStruct((1, 1), F32)]
    return pl.pallas_call(
        body, name=name, grid=(L // tr,), in_specs=in_specs, out_specs=out_specs, out_shape=out_shape,
        compiler_params=_params(("arbitrary",) if loss else ("parallel",)),
    )(*args)


def _ln_bwd(z, dy, g, name):
    L, D = z.shape
    tr = _tile(L, 1536, 16)

    def body(z_ref, dy_ref, g_ref, dzb_ref, dg_ref, db_ref):
        @pl.when(pl.program_id(0) == 0)
        def _():
            dg_ref[...] = jnp.zeros_like(dg_ref)
            db_ref[...] = jnp.zeros_like(db_ref)

        z = z_ref[...].astype(F32)
        mu = jnp.mean(z, axis=-1, keepdims=True)
        zc = z - mu
        var = jnp.mean(zc * zc, axis=-1, keepdims=True)
        rstd = lax.rsqrt(var + LN_EPS)
        xhat = zc * rstd
        dy = dy_ref[...].astype(F32)
        dxh = dy * g_ref[...]
        m1 = jnp.mean(dxh, axis=-1, keepdims=True)
        m2 = jnp.mean(dxh * xhat, axis=-1, keepdims=True)
        dz = rstd * (dxh - m1 - xhat * m2)
        dzb_ref[...] = dz.astype(BF16)
        dg_ref[...] += jnp.sum(dy * xhat, axis=0, keepdims=True)
        db_ref[...] += jnp.sum(dy, axis=0, keepdims=True)

    row = pl.BlockSpec((tr, D), lambda i: (i, 0))
    vec = pl.BlockSpec((1, D), lambda i: (0, 0))
    return pl.pallas_call(
        body, name=name, grid=(L // tr,),
        in_specs=[row, row, vec], out_specs=[row, vec, vec],
        out_shape=[jax.ShapeDtypeStruct((L, D), BF16),
                   jax.ShapeDtypeStruct((1, D), F32), jax.ShapeDtypeStruct((1, D), F32)],
        compiler_params=_params(("arbitrary",)),
    )(z, dy, g)


def _shift_down(x, prev, k):
    T, C = x.shape
    rot = pltpu.roll(jnp.concatenate([prev, x], axis=0).reshape(T // 8 + 1, 8, C), k, 1)
    sub = lax.broadcasted_iota(jnp.int32, (T // 8, 8, C), 1)
    return jnp.where(sub < k, rot[:-1], rot[1:]).reshape(T, C)


def _shift_up(x, nxt, k):
    T, C = x.shape
    rot = pltpu.roll(jnp.concatenate([x, nxt], axis=0).reshape(T // 8 + 1, 8, C), 8 - k, 1)
    sub = lax.broadcasted_iota(jnp.int32, (T // 8, 8, C), 1)
    return jnp.where(sub >= 8 - k, rot[1:], rot[:-1]).reshape(T, C)


def _conv3(x, prev, w, b):
    return w[2:3, :] * x + w[1:2, :] * _shift_down(x, prev, 1) + w[0:1, :] * _shift_down(x, prev, 2) + b


def _ffn_act_fwd(up, w, b, name):
    L, C = up.shape
    F = C // 2
    ts = FFN_TILE
    n = L // ts

    def body(up_ref, pv_ref, w_ref, b_ref, a_ref, u_ref):
        i = pl.program_id(0)
        x = up_ref[...].astype(F32)
        prev = jnp.where(i > 0, pv_ref[...].astype(F32)[8:16], 0.0)
        u = _conv3(x, prev, w_ref[...], b_ref[...])
        u_ref[...] = u.astype(BF16)
        gate = u[:, :F]
        a_ref[...] = (gate * _sigmoid(gate) * u[:, F:]).astype(BF16)

    return pl.pallas_call(
        body, name=name, grid=(n,),
        in_specs=[pl.BlockSpec((ts, C), lambda i: (i, 0)),
                  pl.BlockSpec((16, C), lambda i: (jnp.maximum(i * (ts // 16) - 1, 0), 0)),
                  pl.BlockSpec((3, C), lambda i: (0, 0)), pl.BlockSpec((1, C), lambda i: (0, 0))],
        out_specs=[pl.BlockSpec((ts, F), lambda i: (i, 0)), pl.BlockSpec((ts, C), lambda i: (i, 0))],
        out_shape=[jax.ShapeDtypeStruct((L, F), BF16), jax.ShapeDtypeStruct((L, C), BF16)],
        compiler_params=_params(("parallel",)),
    )(up, up, w, b)


def _ffn_act_bwd(up, u, da, w, name):
    L, C = up.shape
    F = C // 2
    ts = FFN_TILE
    n = L // ts
    last16 = L // 16 - 1

    def du_of(u, da):
        gate, val = u[:, :F], u[:, F:]
        sg = _sigmoid(gate)
        dgate = da * val * (sg * (1.0 + gate * (1.0 - sg)))
        dval = da * (gate * sg)
        return jnp.concatenate([dgate, dval], axis=1)

    def body(up_ref, u_ref, un_ref, da_ref, dan_ref, w_ref, dup_ref, dw_ref, db_ref):
        i = pl.program_id(0)

        @pl.when(i == 0)
        def _():
            dw_ref[...] = jnp.zeros_like(dw_ref)
            db_ref[...] = jnp.zeros_like(db_ref)

        w = w_ref[...]
        x = up_ref[...].astype(F32)
        du = du_of(u_ref[...].astype(F32), da_ref[...].astype(F32))
        dun = jnp.where(i < n - 1, du_of(un_ref[...].astype(F32)[0:8], dan_ref[...].astype(F32)[0:8]), 0.0)
        du1 = _shift_up(du, dun, 1)
        du2 = _shift_up(du, dun, 2)
        dup_ref[...] = (w[2:3, :] * du + w[1:2, :] * du1 + w[0:1, :] * du2).astype(BF16)
        dw_ref[...] += jnp.concatenate([jnp.sum(x * du2, axis=0, keepdims=True),
                                        jnp.sum(x * du1, axis=0, keepdims=True),
                                        jnp.sum(x * du, axis=0, keepdims=True)], axis=0)
        db_ref[...] += jnp.sum(du, axis=0, keepdims=True)

    nxt = lambda i: (jnp.minimum((i + 1) * (ts // 16), last16), 0)
    return pl.pallas_call(
        body, name=name, grid=(n,),
        in_specs=[pl.BlockSpec((ts, C), lambda i: (i, 0)),
                  pl.BlockSpec((ts, C), lambda i: (i, 0)), pl.BlockSpec((16, C), nxt),
                  pl.BlockSpec((ts, F), lambda i: (i, 0)), pl.BlockSpec((16, F), nxt),
                  pl.BlockSpec((3, C), lambda i: (0, 0))],
        out_specs=[pl.BlockSpec((ts, C), lambda i: (i, 0)), pl.BlockSpec((3, C), lambda i: (0, 0)),
                   pl.BlockSpec((1, C), lambda i: (0, 0))],
        out_shape=[jax.ShapeDtypeStruct((L, C), BF16), jax.ShapeDtypeStruct((3, C), F32),
                   jax.ShapeDtypeStruct((1, C), F32)],
        compiler_params=_params(("arbitrary",)),
    )(up, u, u, da, da, w)


def _pool_window(ext, tile_rows, first_row, lead):
    T = ext.shape[0]
    sh = (lambda x, k: pltpu.roll(x, T - k, 0)) if lead else (lambda x, k: pltpu.roll(x, k, 0))
    r2 = ext + sh(ext, 1)
    r4 = r2 + sh(r2, 2)
    r8 = r4 + sh(r4, 4)
    r16 = r8 + sh(r8, 8)
    lo = 0 if lead else 16
    grp = lax.broadcasted_iota(jnp.int32, (tile_rows, POOL_W), 1) // POOL_GROUP
    pick = lambda a, b, c, d: jnp.where(grp == 0, a, jnp.where(grp == 1, b, jnp.where(grp == 2, c, d)))
    win = pick(r2[lo:lo + tile_rows], r4[lo:lo + tile_rows], r8[lo:lo + tile_rows], r16[lo:lo + tile_rows])
    return win, pick(2.0, 4.0, 8.0, 16.0)


def _pool_count(first_row, rows, wlen):
    t1 = (first_row + lax.broadcasted_iota(jnp.int32, (rows, POOL_W), 0) + 1).astype(F32)
    return jnp.minimum(t1, wlen)


def _cp_fwd(h, col, cat, wc, wblk, pscale, name):
    L = h.shape[0]
    ts = _tile(L, CP_TILE_CAP, 16)
    n = L // ts

    def body(h_ref, hp_ref, wc_ref, wb_ref, ps_ref, cat_in_ref, y_ref):
        i = pl.program_id(0)
        h = h_ref[...]
        hp = jnp.where(i > 0, hp_ref[...], 0.0)
        cb, cc, cv, pv = h[:, 0:256], h[:, 256:512], h[:, 512:768], h[:, 768:1024]
        p = cc * cv
        pp = hp[8:16, 256:512] * hp[8:16, 512:768]
        w = wc_ref[...]
        conv = w[2:3, :] * p + w[1:2, :] * _shift_down(p, pp, 1) + w[0:1, :] * _shift_down(p, pp, 2)
        y_conv = cb * conv
        ext = jnp.concatenate([hp[:, 768:1024], pv], axis=0)
        win, wlen = _pool_window(ext, ts, i * ts, False)
        d = win / _pool_count(i * ts, ts, wlen) - pv
        y_pool = _dnn(d.astype(BF16), wb_ref[...]) * ps_ref[...]
        y_ref[...] = jnp.concatenate([y_conv, y_pool], axis=1).astype(BF16)

    return pl.pallas_call(
        body, name=name, grid=(n,),
        in_specs=[pl.BlockSpec((ts, 1024), lambda i: (i, col)),
                  pl.BlockSpec((16, 1024), lambda i: (jnp.maximum(i * (ts // 16) - 1, 0), col)),
                  pl.BlockSpec((3, 256), lambda i: (0, 0)), pl.BlockSpec((256, 256), lambda i: (0, 0)),
                  pl.BlockSpec((1, 256), lambda i: (0, 0)), pl.BlockSpec(memory_space=pl.ANY)],
        out_specs=pl.BlockSpec((ts, 512), lambda i: (i, 1)),
        out_shape=jax.ShapeDtypeStruct(cat.shape, BF16),
        input_output_aliases={5: 0},
        compiler_params=_params(("parallel",)),
    )(h, h, wc, wblk, pscale, cat)


def _cp_bwd(h, col, dcat, dh, wc, wblk, pscale, name):
    L = h.shape[0]
    ts = _tile(L, CP_TILE_CAP, 16)
    n = L // ts
    last16 = L // 16 - 1

    def body(h_ref, hp_ref, hn_ref, dy_ref, dyn_ref, wc_ref, wb_ref, ps_ref, dh_in_ref,
             dh_ref, dwc_ref, dwb_ref, dps_ref):
        i = pl.program_id(0)

        @pl.when(i == 0)
        def _():
            dwc_ref[...] = jnp.zeros_like(dwc_ref)
            dwb_ref[...] = jnp.zeros_like(dwb_ref)
            dps_ref[...] = jnp.zeros_like(dps_ref)

        h = h_ref[...]
        hp = jnp.where(i > 0, hp_ref[...], 0.0)
        hn = hn_ref[...]
        dy = dy_ref[...].astype(F32)
        dyn = jnp.where(i < n - 1, dyn_ref[...].astype(F32), 0.0)
        cb, cc, cv, pv = h[:, 0:256], h[:, 256:512], h[:, 512:768], h[:, 768:1024]
        w = wc_ref[...]
        p = cc * cv
        pp = hp[8:16, 256:512] * hp[8:16, 512:768]
        p1 = _shift_down(p, pp, 1)
        p2 = _shift_down(p, pp, 2)
        conv = w[2:3, :] * p + w[1:2, :] * p1 + w[0:1, :] * p2
        dyc = dy[:, 0:256]
        dcb = dyc * conv
        dconv = dyc * cb
        dconv_n = dyn[0:8, 0:256] * hn[0:8, 0:256]
        dc1 = _shift_up(dconv, dconv_n, 1)
        dc2 = _shift_up(dconv, dconv_n, 2)
        dp = w[2:3, :] * dconv + w[1:2, :] * dc1 + w[0:1, :] * dc2
        dwc_ref[...] += jnp.concatenate([jnp.sum(p * dc2, axis=0, keepdims=True),
                                         jnp.sum(p * dc1, axis=0, keepdims=True),
                                         jnp.sum(p * dconv, axis=0, keepdims=True)], axis=0)
        ps = ps_ref[...]
        wb = wb_ref[...]
        ext = jnp.concatenate([hp[:, 768:1024], pv], axis=0)
        win, wlen = _pool_window(ext, ts, i * ts, False)
        d = win / _pool_count(i * ts, ts, wlen) - pv
        db = d.astype(BF16)
        dyp = dy[:, 256:512]
        dps_ref[...] += jnp.sum(dyp * _dnn(db, wb), axis=0, keepdims=True)
        dypre = (dyp * ps).astype(BF16)
        dwb_ref[...] += _dtn(db, dypre)
        dd = _dnt(dypre, wb)
        ddn = _dnt((dyn[:, 256:512] * ps).astype(BF16), wb)
        e = dd / _pool_count(i * ts, ts, wlen)
        en = ddn / _pool_count((i + 1) * ts, 16, wlen[0:16])
        lead, _ = _pool_window(jnp.concatenate([e, en], axis=0), ts, i * ts, True)
        dpv = lead - dd
        dh_ref[...] = jnp.concatenate([dcb, dp * cv, dp * cc, dpv], axis=1).astype(BF16)

    return pl.pallas_call(
        body, name=name, grid=(n,),
        in_specs=[pl.BlockSpec((ts, 1024), lambda i: (i, col)),
                  pl.BlockSpec((16, 1024), lambda i: (jnp.maximum(i * (ts // 16) - 1, 0), col)),
                  pl.BlockSpec((16, 1024), lambda i: (jnp.minimum((i + 1) * (ts // 16), last16), col)),
                  pl.BlockSpec((ts, 512), lambda i: (i, 1)),
                  pl.BlockSpec((16, 512), lambda i: (jnp.minimum((i + 1) * (ts // 16), last16), 1)),
                  pl.BlockSpec((3, 256), lambda i: (0, 0)), pl.BlockSpec((256, 256), lambda i: (0, 0)),
                  pl.BlockSpec((1, 256), lambda i: (0, 0)), pl.BlockSpec(memory_space=pl.ANY)],
        out_specs=[pl.BlockSpec((ts, 1024), lambda i: (i, col)), pl.BlockSpec((3, 256), lambda i: (0, 0)),
                   pl.BlockSpec((256, 256), lambda i: (0, 0)), pl.BlockSpec((1, 256), lambda i: (0, 0))],
        out_shape=[jax.ShapeDtypeStruct(dh.shape, BF16), jax.ShapeDtypeStruct((3, 256), F32),
                   jax.ShapeDtypeStruct((256, 256), F32), jax.ShapeDtypeStruct((1, 256), F32)],
        input_output_aliases={8: 0},
        compiler_params=_params(("arbitrary",)),
    )(h, h, h, dcat, dcat, wc, wblk, pscale, dh)


def _lower_bound(lb_ref, layer):
    b0, b1 = lb_ref[0:1, :], lb_ref[1:2, :]
    m = jnp.maximum(b0, b1)
    e0, e1 = jnp.exp(b0 - m), jnp.exp(b1 - m)
    p0, p1 = e0 / (e0 + e1), e1 / (e0 + e1)
    lb = (p0 - p0) if layer == 0 else ((p0 + p1) - p0)
    return lb, p0, p1


def _cumsum_rows(x, reverse=False):
    row = lax.broadcasted_iota(jnp.int32, x.shape, 0)
    for sh in (1, 2, 4, 8):
        if reverse:
            x = x + jnp.where(row < SUB - sh, pltpu.roll(x, SUB - sh, 0), 0.0)
        else:
            x = x + jnp.where(row >= sh, pltpu.roll(x, sh, 0), 0.0)
    return x


def _gates(fz, lb):
    sig = _sigmoid(fz)
    f = lb + (1.0 - lb) * sig
    g = jnp.log(jnp.maximum(f, F_FLOOR))
    k = (1.0 - lb) * (1.0 - sig)
    return sig, f, g, k


def _head(h):
    return slice(h * HG_D, (h + 1) * HG_D)


def _hgrn_fwd(hh, lbp, gnorm, layer, name):
    L = hh.shape[0]
    ts = SEQ_TILE
    n = L // ts
    nsub = ts // SUB

    def body(q_ref, f_ref, i_ref, g_ref, lb_ref, gn_ref, y_ref, o_ref, s_ref, a_ref, St):
        @pl.when(pl.program_id(0) == 0)
        def _():
            St[...] = jnp.zeros_like(St)

        lb, _, _ = _lower_bound(lb_ref, layer)
        gn = jnp.tile(gn_ref[...], (1, HG_HEADS))
        r16 = lax.broadcasted_iota(jnp.int32, (SUB, SUB), 0)
        c16 = lax.broadcasted_iota(jnp.int32, (SUB, SUB), 1)

        def block(j, carry):
            rows = pl.ds(pl.multiple_of(j * SUB, SUB), SUB)
            q = q_ref[rows, :] * Q_SCALE
            iv = i_ref[rows, :]
            gz = g_ref[rows, :]
            _, _, g, k = _gates(f_ref[rows, :], lb)
            G = _cumsum_rows(g)
            Gl = G[SUB - 1:SUB, :]
            qt = (q * jnp.exp(G)).astype(BF16)
            kd = (k * jnp.exp(Gl - G)).astype(BF16)
            eGl = jnp.exp(Gl)
            ib = iv.astype(BF16)
            A = [jnp.zeros((SUB, SUB), F32) for _ in range(HG_HEADS)]
            for s in range(SUB):
                P = q * jnp.exp(jnp.minimum(G - G[s:s + 1, :], 0.0)) * k[s:s + 1, :]
                for h in range(HG_HEADS):
                    A[h] = jnp.where(c16 == s, jnp.sum(P[:, _head(h)], axis=-1, keepdims=True), A[h])
            outs, ons, amats = [], [], []
            for h in range(HG_HEADS):
                sl = _head(h)
                Sb = St[h].astype(BF16)
                s_ref[j, sl, :] = Sb
                Am = jnp.where(r16 >= c16, A[h], 0.0)
                amats.append(Am)
                o = _dnt(qt[:, sl], Sb) + _dnn(Am.astype(BF16), ib[:, sl])
                St[h] = eGl[:, sl] * St[h] + _dtn(ib[:, sl], kd[:, sl])
                outs.append(o)
                ons.append(o * lax.rsqrt(jnp.mean(o * o, axis=-1, keepdims=True) + RMS_EPS))
            a_ref[rows, :] = jnp.concatenate(amats, axis=1)
            o_ref[rows, :] = jnp.concatenate(outs, axis=1)
            y = jnp.concatenate(ons, axis=1) * gn * (gz * _sigmoid(gz))
            y_ref[rows, :] = y.astype(BF16)
            return carry

        lax.fori_loop(0, nsub, block, 0, unroll=2)

    col = lambda c: pl.BlockSpec((ts, HG_W), lambda i: (i, c))
    return pl.pallas_call(
        body, name=name, grid=(n,),
        in_specs=[col(0), col(1), col(2), col(3), pl.BlockSpec((2, HG_W), lambda i: (0, 0)),
                  pl.BlockSpec((1, HG_D), lambda i: (0, 0))],
        out_specs=[pl.BlockSpec((ts, HG_W), lambda i: (i, 0)), pl.BlockSpec((ts, HG_W), lambda i: (i, 0)),
                   pl.BlockSpec((nsub, HG_W, HG_D), lambda i: (i, 0, 0)),
                   pl.BlockSpec((ts, HG_HEADS * SUB), lambda i: (i, 0))],
        out_shape=[jax.ShapeDtypeStruct((L, 2 * HG_W), BF16), jax.ShapeDtypeStruct((L, HG_W), F32),
                   jax.ShapeDtypeStruct((L // SUB, HG_W, HG_D), BF16),
                   jax.ShapeDtypeStruct((L, HG_HEADS * SUB), F32)],
        scratch_shapes=[pltpu.VMEM((HG_HEADS, HG_D, HG_D), F32)],
        compiler_params=_params(("arbitrary",)),
    )(hh, hh, hh, hh, lbp, gnorm)


def _hgrn_bwd(hh, o_raw, states, amat, dcat, lbp, gnorm, layer, name):
    L = hh.shape[0]
    ts = SEQ_TILE
    n = L // ts
    nsub = ts // SUB

    def body(q_ref, f_ref, i_ref, g_ref, o_ref, s_ref, a_ref, dy_ref, lb_ref, gn_ref,
             dh_ref, dlb_ref, dgn_ref, dSt, dlb_acc, S_next):
        step = pl.program_id(0)

        @pl.when(step == 0)
        def _():
            dSt[...] = jnp.zeros_like(dSt)
            S_next[...] = jnp.zeros_like(S_next)
            dlb_acc[...] = jnp.zeros_like(dlb_acc)
            dgn_ref[...] = jnp.zeros_like(dgn_ref)

        lb, p0, p1 = _lower_bound(lb_ref, layer)
        gnh = gn_ref[...]
        gn = jnp.tile(gnh, (1, HG_HEADS))
        r16 = lax.broadcasted_iota(jnp.int32, (SUB, SUB), 0)
        c16 = lax.broadcasted_iota(jnp.int32, (SUB, SUB), 1)

        def block(jj, carry):
            j = nsub - 1 - jj
            rows = pl.ds(pl.multiple_of(j * SUB, SUB), SUB)
            q = q_ref[rows, :] * Q_SCALE
            iv = i_ref[rows, :]
            gz = g_ref[rows, :]
            o = o_ref[rows, :]
            dy = dy_ref[rows, :].astype(F32)
            sig, f, g, k = _gates(f_ref[rows, :], lb)
            G = _cumsum_rows(g)
            Gl = G[SUB - 1:SUB, :]
            eG = jnp.exp(G)
            edl = jnp.exp(Gl - G)
            eGl = jnp.exp(Gl)
            qt = (q * eG).astype(BF16)
            kd = (k * edl).astype(BF16)
            ib = iv.astype(BF16)
            sgz = _sigmoid(gz)
            sil = gz * sgz
            dyn = dy * sil
            on_parts, do_parts = [], []
            dgn = jnp.zeros((1, HG_D), F32)
            for h in range(HG_HEADS):
                sl = _head(h)
                oh = o[:, sl]
                rs = lax.rsqrt(jnp.mean(oh * oh, axis=-1, keepdims=True) + RMS_EPS)
                on = oh * rs
                dgn = dgn + jnp.sum(dyn[:, sl] * on, axis=0, keepdims=True)
                don = dyn[:, sl] * gnh
                do_parts.append(rs * (don - on * jnp.mean(don * on, axis=-1, keepdims=True)))
                on_parts.append(on)
            dgn_ref[...] += dgn
            on_all = jnp.concatenate(on_parts, axis=1)
            dgz = dy * on_all * gn * (sgz * (1.0 + gz * (1.0 - sgz)))
            do = jnp.concatenate(do_parts, axis=1)
            dob = do.astype(BF16)
            amat = a_ref[rows, :]
            dq_p, dk_p, di_p, tail_p = [], [], [], []
            for h in range(HG_HEADS):
                sl = _head(h)
                qh, kh, Gh = q[:, sl], k[:, sl], G[:, sl]
                Ap = jnp.where(r16 >= c16, _dnt(dob[:, sl], ib[:, sl]), 0.0)
                ApT = jnp.where(r16 <= c16, _dnt(ib[:, sl], dob[:, sl]), 0.0)
                dqh = jnp.zeros((SUB, HG_D), F32)
                dkh = jnp.zeros((SUB, HG_D), F32)
                for s in range(SUB):
                    dGs = Gh - Gh[s:s + 1, :]
                    e = jnp.exp(jnp.minimum(dGs, -dGs))
                    dqh = dqh + Ap[:, s:s + 1] * (e * kh[s:s + 1, :])
                    dkh = dkh + ApT[:, s:s + 1] * (e * qh[s:s + 1, :])
                Sb = s_ref[j, sl, :]
                dSb = dSt[h].astype(BF16)
                Am = amat[:, h * SUB:(h + 1) * SUB].astype(BF16)
                dq_p.append(dqh + eG[:, sl] * _dnn(dob[:, sl], Sb))
                dk_p.append(dkh + edl[:, sl] * _dnn(ib[:, sl], dSb))
                di_p.append(_dtn(Am, dob[:, sl]) + _dnt(kd[:, sl], dSb))
                tail_p.append(jnp.sum(dSt[h] * S_next[h].astype(F32), axis=0, keepdims=True))
                S_next[h] = Sb
                dSt[h] = eGl[:, sl] * dSt[h] + _dtn(dob[:, sl], qt[:, sl])
            dq = jnp.concatenate(dq_p, axis=1)
            dk = jnp.concatenate(dk_p, axis=1)
            di = jnp.concatenate(di_p, axis=1)
            dg = _cumsum_rows(q * dq - k * dk, reverse=True) + jnp.concatenate(tail_p, axis=1)
            df = jnp.where(f > F_FLOOR, dg / f, 0.0)
            dfk = df - dk
            dfz = (1.0 - lb) * dfk * sig * (1.0 - sig)
            dlb_acc[...] += jnp.sum(dfk * (1.0 - sig), axis=0, keepdims=True)
            dh_ref[rows, :] = jnp.concatenate([dq * Q_SCALE, dfz, di, dgz], axis=1).astype(BF16)
            return carry

        lax.fori_loop(0, nsub, block, 0)

        @pl.when(step == n - 1)
        def _():
            if layer == 0:
                dlb_ref[...] = jnp.zeros_like(dlb_ref)
            else:
                dz1 = p0 * p1 * dlb_acc[...]
                dlb_ref[...] = jnp.concatenate([-dz1, dz1], axis=0)

    rev = lambda i: n - 1 - i
    col = lambda c: pl.BlockSpec((ts, HG_W), lambda i: (rev(i), c))
    return pl.pallas_call(
        body, name=name, grid=(n,),
        in_specs=[col(0), col(1), col(2), col(3), col(0),
                  pl.BlockSpec((nsub, HG_W, HG_D), lambda i: (rev(i), 0, 0)),
                  pl.BlockSpec((ts, HG_HEADS * SUB), lambda i: (rev(i), 0)), col(0),
                  pl.BlockSpec((2, HG_W), lambda i: (0, 0)), pl.BlockSpec((1, HG_D), lambda i: (0, 0))],
        out_specs=[pl.BlockSpec((ts, 4 * HG_W), lambda i: (rev(i), 0)),
                   pl.BlockSpec((2, HG_W), lambda i: (0, 0)), pl.BlockSpec((1, HG_D), lambda i: (0, 0))],
        out_shape=[jax.ShapeDtypeStruct((L, 4 * HG_W + 1024), BF16), jax.ShapeDtypeStruct((2, HG_W), F32),
                   jax.ShapeDtypeStruct((1, HG_D), F32)],
        scratch_shapes=[pltpu.VMEM((HG_HEADS, HG_D, HG_D), F32), pltpu.VMEM((1, HG_W), F32),
                        pltpu.VMEM((HG_HEADS, HG_D, HG_D), BF16)],
        compiler_params=_params(("arbitrary",)),
    )(hh, hh, hh, hh, o_raw, states, amat, dcat, lbp, gnorm)


def _adamw_body(gp_ref, w_ref, m_ref, v_ref, g_ref, d_ref, mo_ref, vo_ref):
    c1 = 1.0 - ADAM_B1 ** ADAM_STEP
    c2 = 1.0 - ADAM_B2 ** ADAM_STEP
    g = gp_ref[0].astype(F32)
    for k in range(1, N_DEV):
        g = g + gp_ref[k].astype(F32)
    mn = ADAM_B1 * m_ref[...] + (1.0 - ADAM_B1) * g
    vn = ADAM_B2 * v_ref[...] + (1.0 - ADAM_B2) * (g * g)
    m_hat = mn / c1
    v_hat = vn / c2
    g_ref[...] = g
    d_ref[...] = -ADAM_LR * (m_hat / (jnp.sqrt(v_hat) + ADAM_EPS) + ADAM_WD * w_ref[...])
    mo_ref[...] = mn
    vo_ref[...] = vn


def _adamw_layers(gparts, w, m, v, name):
    depth, R, C = w.shape
    tr = _tile(R, 256, 16)
    nr = R // tr

    def body(*refs):
        layer = pl.program_id(0)
        for d in range(depth):
            @pl.when(layer == d)
            def _(d=d):
                _adamw_body(refs[d], *refs[depth:])

    def parts_spec(d):
        return pl.BlockSpec((N_DEV, tr, C),
                            lambda l, i: (0, jnp.where(l == d, i, jnp.where(l < d, 0, nr - 1)), 0))

    blk = pl.BlockSpec((None, tr, C), lambda l, i: (l, i, 0))
    shp = jax.ShapeDtypeStruct((depth, R, C), F32)
    return pl.pallas_call(
        body, name=name, grid=(depth, nr),
        in_specs=[parts_spec(d) for d in range(depth)] + [blk, blk, blk],
        out_specs=[blk, blk, blk, blk], out_shape=[shp, shp, shp, shp],
        compiler_params=_params(("arbitrary", "arbitrary")),
    )(*gparts, w, m, v)


def _adamw(gparts, w, m, v, name):
    R = w.shape[0]
    tr = _tile(R, 1024, 16) if R % 16 == 0 else R

    def body(*refs):
        _adamw_body(*refs)

    row = pl.BlockSpec((tr, LANES), lambda i: (i, 0))
    shp = jax.ShapeDtypeStruct((R, LANES), F32)
    return pl.pallas_call(
        body, name=name, grid=(R // tr,),
        in_specs=[pl.BlockSpec((N_DEV, tr, LANES), lambda i: (0, i, 0)), row, row, row],
        out_specs=[row, row, row, row], out_shape=[shp, shp, shp, shp],
        compiler_params=_params(("parallel",)),
    )(gparts, w, m, v)


def _flip(coord, bit):
    return 1 - coord if bit else coord


def _gather_many(blocks, name):
    n = len(blocks)

    def body(*refs):
        x_refs, out_refs = refs[:n], refs[n:2 * n]
        send_sems, recv_sems, local_sems = refs[2 * n:]
        x, y, c = lax.axis_index("x"), lax.axis_index("y"), lax.axis_index("c")
        me, sibling = (x, y, c), (x, y, 1 - c)
        chips = [(1 - x, y), (x, 1 - y), (1 - x, 1 - y)]

        def slot(a, px, py, pc):
            return out_refs[a].at[4 * px + 2 * py + pc]

        def copy(a, k, blk, to, src=None):
            return pltpu.make_async_remote_copy(
                src_ref=slot(a, *blk) if src is None else src, dst_ref=slot(a, *blk),
                send_sem=send_sems.at[7 * a + k], recv_sem=recv_sems.at[7 * a + k],
                device_id=to, device_id_type=pl.DeviceIdType.MESH)

        mine = [pltpu.make_async_copy(x_refs[a], slot(a, *me), local_sems.at[a]) for a in range(n)]
        for cp in mine:
            cp.start()
        first = [copy(a, 0, me, sibling, src=x_refs[a]) for a in range(n)]
        for j, chip in enumerate(chips):
            first += [copy(a, 1 + j, me, (*chip, c), src=x_refs[a]) for a in range(n)]
        for cp in first:
            cp.start()
        passed = []
        for j, chip in enumerate(chips):
            for a in range(n):
                copy(a, 1 + j, (*chip, c), me).wait_recv()
                fwd = copy(a, 4 + j, (*chip, c), sibling)
                fwd.start()
                passed.append(fwd)
        for a in range(n):
            copy(a, 0, sibling, me).wait_recv()
        for j, chip in enumerate(chips):
            for a in range(n):
                copy(a, 4 + j, (*chip, 1 - c), me).wait_recv()
        for cp in first + passed:
            cp.wait_send()
        for cp in mine:
            cp.wait()

    hbm = pl.BlockSpec(memory_space=pl.ANY)
    return pl.pallas_call(
        body, name=name,
        out_shape=[jax.ShapeDtypeStruct((N_DEV,) + b.shape, b.dtype) for b in blocks],
        in_specs=[hbm] * n, out_specs=[hbm] * n,
        scratch_shapes=[pltpu.SemaphoreType.DMA((7 * n,)), pltpu.SemaphoreType.DMA((7 * n,)),
                        pltpu.SemaphoreType.DMA((n,))],
    )(*blocks)


def _split_start(blocks, chunked, name):
    n = len(blocks)
    lands = [lax.empty(b.shape if chunked else (N_DEV,) + b.shape, b.dtype) for b in blocks]

    def body(*refs):
        x_refs, land_refs = refs[:n], refs[n:2 * n]
        send_sems, recv_sems, token = refs[2 * n], refs[2 * n + 1], refs[-1]
        x, y, c = lax.axis_index("x"), lax.axis_index("y"), lax.axis_index("c")
        me = 4 * x + 2 * y + c
        for a in range(n):
            for k in range(1, N_DEV):
                px, py, pc = _flip(x, k & 4), _flip(y, k & 2), _flip(c, k & 1)
                pltpu.make_async_remote_copy(
                    src_ref=x_refs[a].at[4 * px + 2 * py + pc] if chunked else x_refs[a],
                    dst_ref=land_refs[a].at[me],
                    send_sem=send_sems.at[7 * a + k - 1], recv_sem=recv_sems.at[7 * a + k - 1],
                    device_id=(px, py, pc), device_id_type=pl.DeviceIdType.MESH).start()
        token[...] = jnp.zeros_like(token)

    hbm = pl.BlockSpec(memory_space=pltpu.HBM)
    sem = pl.BlockSpec(memory_space=pltpu.SEMAPHORE)
    outs = pl.pallas_call(
        body, name=name,
        out_shape=(pltpu.SemaphoreType.DMA((7 * n,)), pltpu.SemaphoreType.DMA((7 * n,)),
                   *[pltpu.HBM(b.shape, b.dtype) for b in blocks], *[pltpu.HBM(l.shape, l.dtype) for l in lands],
                   jax.ShapeDtypeStruct((8, LANES), F32)),
        in_specs=[hbm] * (2 * n),
        out_specs=(sem, sem, *[hbm] * (2 * n), pl.BlockSpec(memory_space=pltpu.VMEM)),
        input_output_aliases={i: 2 + i for i in range(2 * n)},
        compiler_params=pltpu.CompilerParams(has_side_effects=pltpu.SideEffectType.DATAFLOW_SIDE_EFFECTING),
    )(*[pltpu.with_memory_space_constraint(b, pltpu.HBM) for b in blocks],
      *[pltpu.with_memory_space_constraint(l, pltpu.HBM) for l in lands])
    return outs[0], outs[1], list(outs[2:2 + n]), list(outs[2 + n:2 + 2 * n]), outs[-1]


def _split_wait(started, chunked, after, name):
    send_sems, recv_sems, blocks, lands, _ = started
    n = len(blocks)

    def body(*refs):
        x_refs, land_refs = refs[:n], refs[n:2 * n]
        send_sems, recv_sems = refs[2 * n], refs[2 * n + 1]
        x, y, c = lax.axis_index("x"), lax.axis_index("y"), lax.axis_index("c")
        for a in range(n):
            for k in range(1, N_DEV):
                px, py, pc = _flip(x, k & 4), _flip(y, k & 2), _flip(c, k & 1)
                copy = pltpu.make_async_remote_copy(
                    src_ref=x_refs[a].at[4 * px + 2 * py + pc] if chunked else x_refs[a],
                    dst_ref=land_refs[a].at[4 * px + 2 * py + pc],
                    send_sem=send_sems.at[7 * a + k - 1], recv_sem=recv_sems.at[7 * a + k - 1],
                    device_id=(px, py, pc), device_id_type=pl.DeviceIdType.MESH)
                copy.wait_send()
                copy.wait_recv()

    hbm = pl.BlockSpec(memory_space=pltpu.HBM)
    sem = pl.BlockSpec(memory_space=pltpu.SEMAPHORE)
    outs = pl.pallas_call(
        body, name=name,
        out_shape=(*[pltpu.HBM(b.shape, b.dtype) for b in blocks], *[pltpu.HBM(l.shape, l.dtype) for l in lands]),
        in_specs=[hbm] * (2 * n) + [sem, sem, pl.BlockSpec(memory_space=pl.ANY)],
        out_specs=[hbm] * (2 * n),
        input_output_aliases={i: i for i in range(2 * n)},
        compiler_params=pltpu.CompilerParams(has_side_effects=pltpu.SideEffectType.DATAFLOW_SIDE_EFFECTING),
    )(*blocks, *lands, send_sems, recv_sems, after)
    me = 4 * lax.axis_index("x") + 2 * lax.axis_index("y") + lax.axis_index("c")
    own = [lax.dynamic_index_in_dim(b, me, 0, keepdims=False) if chunked else b for b in outs[:n]]
    return [lax.dynamic_update_index_in_dim(z, o, me, 0) for z, o in zip(outs[n:], own)]


def _exchange_grads(layer_chunks, small_chunks, rep_block, name):
    flows, inputs = [], []
    for p, per_layer in enumerate(layer_chunks):
        for l, arr in enumerate(per_layer):
            flows.append(("param", p, l))
            inputs.append(arr)
    flows += [("small",), ("rep",)]
    inputs += [small_chunks, rep_block]
    n_par = len(layer_chunks)
    n_in, n_out, nf = len(inputs), n_par + 2, len(flows)

    def body(*refs):
        in_refs, out_refs = refs[:n_in], refs[n_in:n_in + n_out]
        send_sems, recv_sems, local_sems = refs[n_in + n_out:]
        x, y, c = lax.axis_index("x"), lax.axis_index("y"), lax.axis_index("c")
        me = 4 * x + 2 * y + c

        def src(f, dev):
            return in_refs[f] if flows[f][0] == "rep" else in_refs[f].at[dev]

        def dst(f, dev):
            if flows[f][0] == "param":
                _, p, l = flows[f]
                return out_refs[p].at[dev, l]
            return out_refs[n_par + (0 if flows[f][0] == "small" else 1)].at[dev]

        mine = [pltpu.make_async_copy(src(f, me), dst(f, me), local_sems.at[f]) for f in range(nf)]
        for cp in mine:
            cp.start()
        copies = []
        for k in range(1, N_DEV):
            px, py, pc = _flip(x, k & 4), _flip(y, k & 2), _flip(c, k & 1)
            peer = 4 * px + 2 * py + pc
            for f in range(nf):
                sems = dict(send_sem=send_sems.at[7 * f + k - 1], recv_sem=recv_sems.at[7 * f + k - 1],
                            device_id=(px, py, pc), device_id_type=pl.DeviceIdType.MESH)
                send = pltpu.make_async_remote_copy(src_ref=src(f, peer), dst_ref=dst(f, me), **sems)
                recv = pltpu.make_async_remote_copy(src_ref=src(f, peer), dst_ref=dst(f, peer), **sems)
                send.start()
                copies.append((send, recv))
        for send, recv in copies:
            recv.wait_recv()
        for send, recv in copies:
            send.wait_send()
        for cp in mine:
            cp.wait()

    out_shape = [jax.ShapeDtypeStruct((N_DEV, len(pl_)) + pl_[0].shape[1:], pl_[0].dtype) for pl_ in layer_chunks]
    out_shape += [jax.ShapeDtypeStruct(small_chunks.shape, small_chunks.dtype),
                  jax.ShapeDtypeStruct((N_DEV,) + rep_block.shape, rep_block.dtype)]
    hbm = pl.BlockSpec(memory_space=pl.ANY)
    return pl.pallas_call(
        body, name=name, out_shape=out_shape,
        in_specs=[hbm] * n_in, out_specs=[hbm] * n_out,
        scratch_shapes=[pltpu.SemaphoreType.DMA((7 * nf,)), pltpu.SemaphoreType.DMA((7 * nf,)),
                        pltpu.SemaphoreType.DMA((nf,))],
    )(*inputs)


def _pack_rows(size):
    return -(-size // (8 * LANES)) * 8


def _pack(arrs, dtype):
    parts, offs, r = [], [], 0
    for a in arrs:
        flat = a.astype(dtype).reshape(-1)
        nrow = _pack_rows(flat.shape[0])
        flat = jnp.pad(flat, (0, nrow * LANES - flat.shape[0]))
        parts.append(flat.reshape(nrow, LANES))
        offs.append((r, nrow))
        r += nrow
    return jnp.concatenate(parts, axis=0), offs


def _unpack(buf, offs, shapes, lead=()):
    outs = []
    for (r, nrow), shp in zip(offs, shapes):
        size = 1
        for s in shp:
            size *= s
        flat = buf[..., r:r + nrow, :].reshape(lead + (nrow * LANES,))
        outs.append(flat[..., :size].reshape(lead + tuple(shp)))
    return outs


def _cols_from_shards(g, axis):
    return jnp.concatenate([g[j] for j in range(N_DEV)], axis=axis)


BIG = ("w_in", "w_o", "w_up", "w_down")
SMALL_SHARDED = ("meta_tokens", "w_conv", "w_ffn_conv")
REPLICATED = ("hg_lower_bounds", "w_pool", "pool_scale", "hg_norm_g", "ln1_g", "ln1_b", "b_ffn_conv", "ln2_g", "ln2_b")
WEIGHTS = ("meta_tokens", "hg_lower_bounds", "w_in", "w_conv", "w_pool", "pool_scale", "hg_norm_g", "w_o",
           "ln1_g", "ln1_b", "w_up", "w_ffn_conv", "b_ffn_conv", "w_down", "ln2_g", "ln2_b")


def _pool_blockdiag(w_pool_l):
    z = jnp.zeros((POOL_GROUP, POOL_GROUP), w_pool_l.dtype)
    rows = [jnp.concatenate([w_pool_l[g] if h == g else z for h in range(4)], axis=1) for g in range(4)]
    return jnp.concatenate(rows, axis=0)


def _in_weights(g_in):
    w_in = jnp.transpose(g_in, (1, 0, 2)).reshape(D_MODEL, -1)
    return dict(w_hc=jnp.concatenate([w_in[:, 768:2816], w_in[:, 0:768], w_in[:, 2816:3072]], axis=1))


def _rest_weights(g_o, g_up, g_down):
    w_o = g_o.reshape(-1, D_MODEL)
    return dict(w_o=jnp.concatenate([w_o[256:768], w_o[0:256], w_o[768:1024]], axis=0),
                w_up=jnp.transpose(g_up, (1, 0, 2)).reshape(D_MODEL, -1), w_down=g_down.reshape(-1, D_MODEL))


def kernel(x, meta_tokens, hg_lower_bounds, w_in, w_conv, w_pool, pool_scale, hg_norm_g, w_o, ln1_g, ln1_b, w_up, w_ffn_conv, b_ffn_conv, w_down, ln2_g, ln2_b, loss_target, m_meta_tokens, m_hg_lower_bounds, m_w_in, m_w_conv, m_w_pool, m_pool_scale, m_hg_norm_g, m_w_o, m_ln1_g, m_ln1_b, m_w_up, m_w_ffn_conv, m_b_ffn_conv, m_w_down, m_ln2_g, m_ln2_b, v_meta_tokens, v_hg_lower_bounds, v_w_in, v_w_conv, v_w_pool, v_pool_scale, v_hg_norm_g, v_w_o, v_ln1_g, v_ln1_b, v_w_up, v_w_ffn_conv, v_b_ffn_conv, v_w_down, v_ln2_g, v_ln2_b):
    W = dict(meta_tokens=meta_tokens, hg_lower_bounds=hg_lower_bounds, w_in=w_in, w_conv=w_conv, w_pool=w_pool,
             pool_scale=pool_scale, hg_norm_g=hg_norm_g, w_o=w_o, ln1_g=ln1_g, ln1_b=ln1_b, w_up=w_up,
             w_ffn_conv=w_ffn_conv, b_ffn_conv=b_ffn_conv, w_down=w_down, ln2_g=ln2_g, ln2_b=ln2_b)
    M = dict(meta_tokens=m_meta_tokens, hg_lower_bounds=m_hg_lower_bounds, w_in=m_w_in, w_conv=m_w_conv,
             w_pool=m_w_pool, pool_scale=m_pool_scale, hg_norm_g=m_hg_norm_g, w_o=m_w_o, ln1_g=m_ln1_g,
             ln1_b=m_ln1_b, w_up=m_w_up, w_ffn_conv=m_w_ffn_conv, b_ffn_conv=m_b_ffn_conv, w_down=m_w_down,
             ln2_g=m_ln2_g, ln2_b=m_ln2_b)
    V = dict(meta_tokens=v_meta_tokens, hg_lower_bounds=v_hg_lower_bounds, w_in=v_w_in, w_conv=v_w_conv,
             w_pool=v_w_pool, pool_scale=v_pool_scale, hg_norm_g=v_hg_norm_g, w_o=v_w_o, ln1_g=v_ln1_g,
             ln1_b=v_ln1_b, w_up=v_w_up, w_ffn_conv=v_w_ffn_conv, b_ffn_conv=v_b_ffn_conv, w_down=v_w_down,
             ln2_g=v_ln2_g, ln2_b=v_ln2_b)
    assert x.shape[0] == 1 and x.shape[2] == D_MODEL and w_in.shape[0] == DEPTH
    seq = x.shape[1]
    L = -(-(seq + N_META) // ROW_ALIGN) * ROW_ALIGN

    small_pack, small_offs = _pack([W[n] for n in SMALL_SHARDED], F32)
    shards = {n: [W[n][l].astype(BF16) for l in range(DEPTH)] for n in BIG}
    g_in0, small_all = _gather_many([shards["w_in"][0], small_pack], "gather_weights")
    full = {}
    for n, a in zip(SMALL_SHARDED, _unpack(small_all, small_offs, [W[n].shape for n in SMALL_SHARDED], (N_DEV,))):
        full[n] = _cols_from_shards(a, 1)
    order = (small_all[0, 0, 0] * 0.0).astype(BF16)
    rest0_started = _split_start([shards[n][0] + order for n in BIG[1:]], False, "gather_rest0_start")
    order = rest0_started[4][0, 0].astype(BF16)
    layer1_started = _split_start([shards[n][1] + order for n in BIG], False, "gather_layer1_start")
    lb_in = hg_lower_bounds + layer1_started[4][0, 0]

    pad_rows = L - N_META - seq
    xp = jnp.concatenate([full["meta_tokens"], x[0], jnp.zeros((pad_rows, D_MODEL), F32)], axis=0)
    tgt = jnp.concatenate([jnp.zeros((N_META, D_MODEL), F32), loss_target[0], jnp.zeros((pad_rows, D_MODEL), F32)], axis=0)

    saved = []
    h_in, h_in_b = xp, xp.astype(BF16)
    for l in range(DEPTH):
        if l == 0:
            lw = _in_weights(g_in0)
        else:
            g_in, *g_rest = _split_wait(layer1_started, False, h_in_b, "gather_layer1_wait")
            lw = {**_in_weights(g_in), **_rest_weights(*g_rest)}
        wc = full["w_conv"][l].T
        wblk = _pool_blockdiag(w_pool[l]).astype(BF16)
        ps = pool_scale[l][None, :]
        gn = hg_norm_g[l][None, :]
        wf = full["w_ffn_conv"][l].T
        bf = b_ffn_conv[l][None, :]
        h = _matmul(h_in_b, lw["w_hc"], "nn", F32, f"fwd_in_{l}")
        y_hg, o_raw, states, amat = _hgrn_fwd(h, lb_in if l == 0 else hg_lower_bounds, gn, l, f"hgrn_fwd_{l}")
        cat = _cp_fwd(h, 2, y_hg, wc, wblk, ps, f"convpool_fwd_{l}")
        if l == 0:
            lw.update(_rest_weights(*_split_wait(rest0_started, False, cat, "gather_rest0_wait")))
        z1, x1, x1_b = _matmul_ln(cat, lw["w_o"], h_in, ln1_g[l][None, :], ln1_b[l][None, :], f"fwd_o_ln1_{l}")
        up = _matmul(x1_b, lw["w_up"], "nn", BF16, f"fwd_up_{l}")
        a, u = _ffn_act_fwd(up, wf, bf, f"ffn_fwd_{l}")
        saved.append(dict(lw=lw, wc=wc, wblk=wblk, ps=ps, gn=gn, wf=wf, bf=bf, x_b=h_in_b, h=h,
                          o_raw=o_raw, states=states, amat=amat, cat=cat, z1=z1, x1_b=x1_b, up=up, u=u, a=a))
        if l < DEPTH - 1:
            saved[l]["z2"], h_in, h_in_b = _matmul_ln(a, lw["w_down"], x1, ln2_g[l][None, :], ln2_b[l][None, :],
                                                     f"fwd_down_ln2_{l}")
        else:
            saved[l]["z2"], dy, loss_part = _matmul_ln(a, lw["w_down"], x1, ln2_g[l][None, :], ln2_b[l][None, :],
                                                       f"fwd_down_ln2_loss_{l}", loss=(tgt, seq))

    loss = lax.psum(loss_part[0, 0], ("x", "y", "c"))

    G = {}
    per_layer = {n: [None] * DEPTH for n in ("w_conv", "w_pool", "pool_scale", "hg_norm_g", "ln1_g", "ln1_b",
                                             "w_ffn_conv", "b_ffn_conv", "ln2_g", "ln2_b")}
    ffn_started, mix_started = [None] * DEPTH, [None] * DEPTH
    order = jnp.zeros((), F32)
    dlb_total = jnp.zeros((DEPTH, HG_W), F32)
    for l in reversed(range(DEPTH)):
        s = saved[l]
        lw = s["lw"]
        dz2_b, dg2, db2 = _ln_bwd(s["z2"], dy, ln2_g[l][None, :] + order, f"ln2_bwd_{l}")
        da = _matmul(dz2_b, lw["w_down"], "nt", BF16, f"bwd_da_{l}")
        d_w_down = _matmul(s["a"], dz2_b, "tn", BF16, f"wgrad_down_{l}")
        dup, dwf, dbf = _ffn_act_bwd(s["up"], s["u"], da, s["wf"], f"ffn_bwd_{l}")
        dx1 = _matmul(dup, lw["w_up"], "nt", BF16, f"bwd_dx1_{l}", res=dz2_b, alpha=ALPHA)
        d_w_up = _matmul(s["x1_b"], dup, "tn", BF16, f"wgrad_up_{l}")
        ffn_started[l] = _split_start([jnp.transpose(d_w_up.reshape(D_MODEL, N_DEV, -1), (1, 0, 2)),
                                       d_w_down.reshape(N_DEV, -1, D_MODEL)], True, f"scatter_ffn{l}_start")
        order = ffn_started[l][4][0, 0]
        dz1_b, dg1, db1 = _ln_bwd(s["z1"], dx1, ln1_g[l][None, :] + order, f"ln1_bwd_{l}")
        dcat = _matmul(dz1_b, lw["w_o"], "nt", BF16, f"bwd_dcat_{l}")
        d_w_o = _matmul(s["cat"], dz1_b, "tn", BF16, f"wgrad_o_{l}")
        dh, dlb, dgn = _hgrn_bwd(s["h"], s["o_raw"], s["states"], s["amat"], dcat, hg_lower_bounds, s["gn"], l,
                                 f"hgrn_bwd_{l}")
        dh, dwc, dwblk, dps = _cp_bwd(s["h"], 2, dcat, dh, s["wc"], s["wblk"], s["ps"], f"convpool_bwd_{l}")
        d_w_hc = _matmul(s["x_b"], dh, "tn", BF16, f"wgrad_in_{l}")
        d_w_in = jnp.concatenate([d_w_hc[:, 2048:2816], d_w_hc[:, 0:2048], d_w_hc[:, 2816:3072]], axis=1)
        mix_chunks = [jnp.transpose(d_w_in.reshape(D_MODEL, N_DEV, -1), (1, 0, 2)),
                      jnp.concatenate([d_w_o[512:768], d_w_o[0:512], d_w_o[768:1024]], axis=0).reshape(N_DEV, -1, D_MODEL)]
        mix_started[l] = _split_start(mix_chunks, True, f"scatter_mix{l}_start")
        order = mix_started[l][4][0, 0]
        dx = _matmul(dh, lw["w_hc"] + order.astype(BF16), "nt", F32, f"bwd_dx_{l}", res=dz1_b, alpha=ALPHA)
        per_layer["w_conv"][l] = dwc.T
        per_layer["w_ffn_conv"][l] = dwf.T
        per_layer["b_ffn_conv"][l] = dbf[0]
        per_layer["w_pool"][l] = jnp.stack([dwblk[g * 64:(g + 1) * 64, g * 64:(g + 1) * 64] for g in range(4)], axis=0)
        per_layer["pool_scale"][l] = dps[0]
        per_layer["hg_norm_g"][l] = dgn[0]
        per_layer["ln1_g"][l], per_layer["ln1_b"][l] = dg1[0], db1[0]
        per_layer["ln2_g"][l], per_layer["ln2_b"][l] = dg2[0], db2[0]
        dlb_total = dlb_total + dlb
        dy = dx
    for n, parts in per_layer.items():
        G[n] = jnp.stack(parts, axis=0)
    G["hg_lower_bounds"] = dlb_total
    grad_x = dy[N_META:N_META + seq][None]

    def shard_major(g, lead):
        g = g.reshape(g.shape[:lead] + (N_DEV, -1) + g.shape[lead + 1:])
        g = jnp.moveaxis(g, lead, 0).reshape(N_DEV, -1)
        nrow = _pack_rows(g.shape[1])
        return jnp.pad(g, ((0, 0), (0, nrow * LANES - g.shape[1]))).reshape(N_DEV, nrow, LANES)

    small_chunks = jnp.concatenate([shard_major(dy[0:N_META], 1), shard_major(G["w_conv"], 1),
                                    shard_major(G["w_ffn_conv"], 1)], axis=1)
    w_small, _ = _pack([W[n] for n in SMALL_SHARDED], F32)
    rep_pack, rep_offs = _pack([G[n] for n in REPLICATED], F32)
    small_recv, rep_all = _exchange_grads([], small_chunks, rep_pack, "exchange_grads")
    parts = {n: [] for n in BIG}
    for l in range(DEPTH):
        up_l, down_l = _split_wait(ffn_started[l], True, rep_all, f"scatter_ffn{l}_wait")
        in_l, o_l = _split_wait(mix_started[l], True, rep_all, f"scatter_mix{l}_wait")
        for n, a in zip(BIG, (in_l, o_l, up_l, down_l)):
            parts[n].append(a)

    res = {k: {} for k in ("grad", "delta", "new_m", "new_v")}
    kinds = ("grad", "delta", "new_m", "new_v")
    for n in BIG:
        for kind, a in zip(kinds, _adamw_layers(parts[n], W[n], M[n], V[n], f"adamw_{n}")):
            res[kind][n] = a
    m_small, _ = _pack([M[n] for n in SMALL_SHARDED], F32)
    v_small, _ = _pack([V[n] for n in SMALL_SHARDED], F32)
    outs_small = _adamw(small_recv, w_small, m_small, v_small, "adamw_small_sharded")
    w_rep, _ = _pack([W[n] for n in REPLICATED], F32)
    m_rep, _ = _pack([M[n] for n in REPLICATED], F32)
    v_rep, _ = _pack([V[n] for n in REPLICATED], F32)
    outs_rep = _adamw(rep_all, w_rep, m_rep, v_rep, "adamw_replicated")
    for kind, b_sm, b_rep in zip(kinds, outs_small, outs_rep):
        for n, a in zip(SMALL_SHARDED, _unpack(b_sm, small_offs, [W[n].shape for n in SMALL_SHARDED])):
            res[kind][n] = a
        for n, a in zip(REPLICATED, _unpack(b_rep, rep_offs, [W[n].shape for n in REPLICATED])):
            res[kind][n] = a

    return (loss, grad_x, *[res["grad"][n] for n in WEIGHTS], *[res["delta"][n] for n in WEIGHTS],
            *[res["new_m"][n] for n in WEIGHTS], *[res["new_v"][n] for n in WEIGHTS])
```
